```python
import jax
import jax.numpy as jnp
from jax import lax
import numpy as np

D_MODEL = 1024
BATCH = 32
SEQ = 256
DEPTH = 2
DEC_BATCH = 2
DEC_SEQ = 1024
PAST_LEN = 512

GRID_W = 64
HEAD_DIM = 64
H_A = 6
KV_A = 2
G_A = H_A // KV_A
WINDOW = 128
BLOCK = 128
H_B = 5
NA_ROWS = 8
NA_COLS = 16
H_C = 5
Q_LORA = 384
KV_LORA = 256
QK_NOPE = 64
QK_ROPE = 32
V_C = 64
D_MIX = H_A * HEAD_DIM + H_B * HEAD_DIM + H_C * V_C
PROJ_SIZES = (H_A * HEAD_DIM, KV_A * HEAD_DIM, KV_A * HEAD_DIM,
              H_B * HEAD_DIM, H_B * HEAD_DIM, H_B * HEAD_DIM,
              Q_LORA, KV_LORA, QK_ROPE)
D_IN = sum(PROJ_SIZES)
N_EXPERTS = 32
TOP_K = 4
D_FF = 1024
MOE_BLOCK = 128
SWIGLU_ALPHA = 1.702
SWIGLU_LIMIT = 7.0
ROPE_BASE = 10000.0
EPS = 1e-6
NEG = -1e30

kernel_name = 'hybrid_prefix_diffusion_step'


def rmsnorm(x, g):
    xf = x.astype(jnp.float32)
    xf = xf * lax.rsqrt(jnp.mean(xf * xf, axis=-1, keepdims=True) + EPS)
    return (xf * g.astype(jnp.float32)).astype(x.dtype)


def modulate(h, shift, scale):
    return h * (1.0 + scale) + shift


def split_proj(p):
    cuts = [int(s) for s in np.cumsum(PROJ_SIZES)[:-1]]
    return jnp.split(p, cuts, axis=-1)


def axial_rope(x):
    n, d = x.shape[1], x.shape[-1]
    nf = d // 4
    t = jnp.arange(n)
    inv = ROPE_BASE ** (-jnp.arange(nf, dtype=jnp.float32) / nf)

    def rot(xh, pos):
        ang = pos.astype(jnp.float32)[:, None] * inv
        cos, sin = jnp.cos(ang)[None, :, None, :], jnp.sin(ang)[None, :, None, :]
        x1, x2 = xh[..., :nf].astype(jnp.float32), xh[..., nf:].astype(jnp.float32)
        return jnp.concatenate([x1 * cos - x2 * sin, x1 * sin + x2 * cos], axis=-1)

    half = d // 2
    out = jnp.concatenate([rot(x[..., :half], t // GRID_W), rot(x[..., half:], t % GRID_W)], axis=-1)
    return out.astype(x.dtype)


def full_attn(q, k, v, scale, sink=None):
    b, nq = q.shape[:2]
    nb = nq // BLOCK
    qb = jnp.moveaxis(q.reshape(b, nb, BLOCK, *q.shape[2:]), 1, 0)

    def one_block(qblk):
        s = jnp.einsum('bqkgd,bskd->bqkgs', qblk, k).astype(jnp.float32) * scale
        if sink is not None:
            sk = jnp.broadcast_to(sink.astype(jnp.float32).reshape(1, 1, q.shape[2], q.shape[3], 1), s.shape[:-1] + (1,))
            p = jax.nn.softmax(jnp.concatenate([s, sk], axis=-1), axis=-1)[..., :-1]
        else:
            p = jax.nn.softmax(s, axis=-1)
        return jnp.einsum('bqkgs,bskd->bqkgd', p.astype(v.dtype), v)

    out = lax.map(one_block, qb)
    return jnp.moveaxis(out, 0, 1).reshape(b, nq, *out.shape[3:])


def window_attn(q, k, v, k_ctx, v_ctx, sink):
    b, n = q.shape[:2]
    nb = n // BLOCK
    scale = HEAD_DIM ** -0.5
    qb = q.reshape(b, nb, BLOCK, KV_A, G_A, HEAD_DIM)
    pad = ((0, 0), (BLOCK, BLOCK), (0, 0), (0, 0))
    idx = jnp.arange(nb)[:, None] * BLOCK + jnp.arange(3 * BLOCK)[None, :]
    kb = jnp.pad(k, pad)[:, idx]
    vb = jnp.pad(v, pad)[:, idx]
    qpos = jnp.arange(n).reshape(nb, BLOCK)
    kpos = idx - BLOCK
    valid = ((kpos[:, None, :] >= 0) & (kpos[:, None, :] < n)
             & (jnp.abs(qpos[:, :, None] - kpos[:, None, :]) <= WINDOW))
    s_loc = jnp.einsum('bnqkgd,bnskd->bnqkgs', qb, kb).astype(jnp.float32) * scale
    s_loc = jnp.where(valid[None, :, :, None, None, :], s_loc, NEG)
    s_ctx = jnp.einsum('bnqkgd,blkd->bnqkgl', qb, k_ctx).astype(jnp.float32) * scale
    s_sink = jnp.broadcast_to(sink.astype(jnp.float32).reshape(1, 1, 1, KV_A, G_A, 1), s_loc.shape[:-1] + (1,))
    p = jax.nn.softmax(jnp.concatenate([s_loc, s_ctx, s_sink], axis=-1), axis=-1).astype(v.dtype)
    nloc = 3 * BLOCK
    out = (jnp.einsum('bnqkgs,bnskd->bnqkgd', p[..., :nloc], vb)
           + jnp.einsum('bnqkgl,blkd->bnqkgd', p[..., nloc:-1], v_ctx))
    return out.reshape(b, n, H_A * HEAD_DIM)


def neighbourhood_attn(q, k, v, k_ctx, v_ctx, rpb):
    b, n = q.shape[:2]
    rows = n // GRID_W
    kh = min(NA_ROWS, rows)
    scale = HEAD_DIM ** -0.5
    r = jnp.arange(rows)
    key_rows = jnp.clip(r - kh // 2, 0, rows - kh)[:, None] + jnp.arange(kh)[None, :]
    kg = k.reshape(b, rows, GRID_W, H_B, HEAD_DIM)[:, key_rows]
    vg = v.reshape(b, rows, GRID_W, H_B, HEAD_DIM)[:, key_rows]
    qg = q.reshape(b, rows, GRID_W, H_B, HEAD_DIM)
    col = jnp.arange(GRID_W)
    col_start = jnp.clip(col - NA_COLS // 2, 0, GRID_W - NA_COLS)
    col_ok = (col[None, :] >= col_start[:, None]) & (col[None, :] < col_start[:, None] + NA_COLS)
    dr = key_rows - r[:, None] + (NA_ROWS - 1)
    dc = jnp.clip(col[None, :] - col[:, None] + (NA_COLS - 1), 0, 2 * NA_COLS - 2)
    bias = rpb[:, dr][:, :, :, dc]
    bias = jnp.transpose(bias, (1, 3, 0, 2, 4)).astype(jnp.float32)
    s_loc = jnp.einsum('brqhd,brikhd->brqhik', qg, kg).astype(jnp.float32) * scale + bias[None]
    s_loc = jnp.where(col_ok[None, None, :, None, None, :], s_loc, NEG).reshape(b, rows, GRID_W, H_B, kh * GRID_W)
    s_ctx = jnp.einsum('brqhd,blhd->brqhl', qg, k_ctx).astype(jnp.float32) * scale
    p = jax.nn.softmax(jnp.concatenate([s_loc, s_ctx], axis=-1), axis=-1).astype(v.dtype)
    nloc = kh * GRID_W
    p_loc = p[..., :nloc].reshape(b, rows, GRID_W, H_B, kh, GRID_W)
    out = (jnp.einsum('brqhik,brikhd->brqhd', p_loc, vg)
           + jnp.einsum('brqhl,blhd->brqhd', p[..., nloc:], v_ctx))
    return out.reshape(b, n, H_B * HEAD_DIM)


def mla_attn(cq, ckv_all, kr_all, g_cq, w_uq, w_ukv, rotary):
    b, n = cq.shape[:2]
    nk = ckv_all.shape[1]
    q = (rmsnorm(cq, g_cq) @ w_uq).reshape(b, n, H_C, QK_NOPE + QK_ROPE)
    if rotary:
        q = jnp.concatenate([q[..., :QK_NOPE], axial_rope(q[..., QK_NOPE:])], axis=-1)
    kv = (ckv_all @ w_ukv).reshape(b, nk, H_C, QK_NOPE + V_C)
    k = jnp.concatenate([kv[..., :QK_NOPE], jnp.broadcast_to(kr_all[:, :, None, :], (b, nk, H_C, QK_ROPE))], axis=-1)
    o = full_attn(q[:, :, :, None, :], k, kv[..., QK_NOPE:], (QK_NOPE + QK_ROPE) ** -0.5)
    return o.reshape(b, n, H_C * V_C)


def context_mixers(h, w_in, sink_a, g_cq, g_ckv, w_uq, w_ukv):
    b, l, _ = h.shape
    qa, ka, va, qb, kb, vb, cq, ckv, kr = split_proj(h @ w_in)
    ka = ka.reshape(b, l, KV_A, HEAD_DIM)
    va = va.reshape(b, l, KV_A, HEAD_DIM)
    kb = kb.reshape(b, l, H_B, HEAD_DIM)
    vb = vb.reshape(b, l, H_B, HEAD_DIM)
    ckv = rmsnorm(ckv, g_ckv)
    o_a = full_attn(qa.reshape(b, l, KV_A, G_A, HEAD_DIM), ka, va, HEAD_DIM ** -0.5, sink_a)
    o_b = full_attn(qb.reshape(b, l, H_B, 1, HEAD_DIM), kb, vb, HEAD_DIM ** -0.5)
    o_c = mla_attn(cq, ckv, kr, g_cq, w_uq, w_ukv, rotary=False)
    o = jnp.concatenate([o_a.reshape(b, l, H_A * HEAD_DIM), o_b.reshape(b, l, H_B * HEAD_DIM), o_c], axis=-1)
    return o, ka, va, kb, vb, ckv, kr


def latent_mixers(h, ctx_a_k, ctx_a_v, ctx_b_k, ctx_b_v, ctx_c_kv, ctx_c_kr,
                  w_in, sink_a, rpb_b, g_cq, g_ckv, w_uq, w_ukv):
    b, n, _ = h.shape
    qa, ka, va, qb, kb, vb, cq, ckv, kr = split_proj(h @ w_in)
    qa = axial_rope(qa.reshape(b, n, H_A, HEAD_DIM))
    ka = axial_rope(ka.reshape(b, n, KV_A, HEAD_DIM))
    o_a = window_attn(qa, ka, va.reshape(b, n, KV_A, HEAD_DIM), ctx_a_k, ctx_a_v, sink_a)
    o_b = neighbourhood_attn(qb.reshape(b, n, H_B, HEAD_DIM), kb.reshape(b, n, H_B, HEAD_DIM),
                             vb.reshape(b, n, H_B, HEAD_DIM), ctx_b_k, ctx_b_v, rpb_b)
    ckv_all = jnp.concatenate([rmsnorm(ckv, g_ckv), ctx_c_kv], axis=1)
    kr_all = jnp.concatenate([axial_rope(kr[:, :, None, :])[:, :, 0, :], ctx_c_kr], axis=1)
    o_c = mla_attn(cq, ckv_all, kr_all, g_cq, w_uq, w_ukv, rotary=True)
    return jnp.concatenate([o_a, o_b, o_c], axis=-1)


def moe_ffn(h, w_router, b_router, w_gu, b_gu, w_down, b_down):
    t = h.shape[0]
    tk = t * TOP_K
    logits = h.astype(jnp.float32) @ w_router.astype(jnp.float32) + b_router.astype(jnp.float32)
    top_logit, top_e = lax.top_k(logits, TOP_K)
    gates = jax.nn.softmax(top_logit, axis=-1).reshape(tk)
    flat_e = top_e.reshape(tk)
    order = jnp.argsort(flat_e)
    e_sorted = flat_e[order]
    tok_sorted = order // TOP_K
    counts = jnp.zeros((N_EXPERTS,), jnp.int32).at[flat_e].add(1)
    padded = (counts + MOE_BLOCK - 1) // MOE_BLOCK * MOE_BLOCK
    grp_start = jnp.cumsum(counts) - counts
    pad_end = jnp.cumsum(padded)
    pad_start = pad_end - padded
    dest = pad_start[e_sorted] + jnp.arange(tk, dtype=jnp.int32) - grp_start[e_sorted]
    n_blocks = -(-tk // MOE_BLOCK) + N_EXPERTS
    buf = jnp.zeros((n_blocks * MOE_BLOCK, h.shape[1]), h.dtype).at[dest].set(h[tok_sorted])
    block_e = jnp.minimum(jnp.searchsorted(pad_end, jnp.arange(n_blocks, dtype=jnp.int32) * MOE_BLOCK, side='right'),
                          N_EXPERTS - 1)

    def expert_block(args):
        xb, e = args
        gu = xb @ w_gu[e] + b_gu[e]
        x_glu = jnp.minimum(gu[:, :D_FF], SWIGLU_LIMIT)
        x_lin = jnp.clip(gu[:, D_FF:], -SWIGLU_LIMIT, SWIGLU_LIMIT)
        act = x_glu * jax.nn.sigmoid(SWIGLU_ALPHA * x_glu) * (x_lin + 1.0)
        return act @ w_down[e] + b_down[e]

    y = lax.map(expert_block, (buf.reshape(n_blocks, MOE_BLOCK, h.shape[1]), block_e))
    y = y.reshape(n_blocks * MOE_BLOCK, h.shape[1])[dest] * gates[order][:, None].astype(h.dtype)
    return jax.ops.segment_sum(y, tok_sorted, num_segments=t)


def setup_inputs(seed: int = 0) -> dict:
    key = jax.random.key(seed)
    ks = jax.random.split(key, 29)

    def nrm(i, shape, scale=1.0):
        return jax.random.normal(ks[i], shape, jnp.float32) * scale

    d = D_MODEL
    return {
        'x_prompt': nrm(0, (BATCH, SEQ, d)),
        'x_sample': nrm(1, (DEC_BATCH, DEC_SEQ, d)),
        'cache_a_k': nrm(2, (DEC_BATCH, DEPTH, PAST_LEN, KV_A, HEAD_DIM)),
        'cache_a_v': nrm(3, (DEC_BATCH, DEPTH, PAST_LEN, KV_A, HEAD_DIM)),
        'cache_b_k': nrm(4, (DEC_BATCH, DEPTH, PAST_LEN, H_B, HEAD_DIM)),
        'cache_b_v': nrm(5, (DEC_BATCH, DEPTH, PAST_LEN, H_B, HEAD_DIM)),
        'cache_c_kv': nrm(6, (DEC_BATCH, DEPTH, PAST_LEN, KV_LORA)),
        'cache_c_kr': nrm(7, (DEC_BATCH, DEPTH, PAST_LEN, QK_ROPE)),
        'c': nrm(8, (DEC_BATCH, d)),
        'c_ctx': nrm(9, (d,)),
        'w_ada': nrm(10, (DEPTH, d, 6 * d), 0.5 * d ** -0.5),
        'b_ada': nrm(11, (DEPTH, 6 * d), 0.02),
        'g_attn': 1.0 + nrm(12, (DEPTH, d), 0.02),
        'g_ffn': 1.0 + nrm(13, (DEPTH, d), 0.02),
        'w_in': nrm(14, (DEPTH, d, D_IN), d ** -0.5),
        'sink_a': nrm(15, (DEPTH, H_A), 0.5),
        'rpb_b': nrm(16, (DEPTH, H_B, 2 * NA_ROWS - 1, 2 * NA_COLS - 1), 0.1),
        'g_cq': 1.0 + nrm(17, (DEPTH, Q_LORA), 0.02),
        'g_ckv': 1.0 + nrm(18, (DEPTH, KV_LORA), 0.02),
        'w_uq': nrm(19, (DEPTH, Q_LORA, H_C * (QK_NOPE + QK_ROPE)), Q_LORA ** -0.5),
        'w_ukv': nrm(20, (DEPTH, KV_LORA, H_C * (QK_NOPE + V_C)), KV_LORA ** -0.5),
        'w_out': nrm(21, (DEPTH, D_MIX, d), D_MIX ** -0.5),
        'w_router': nrm(22, (DEPTH, d, N_EXPERTS), d ** -0.5),
        'b_router': nrm(23, (DEPTH, N_EXPERTS), 0.01),
        'w_gu': nrm(24, (DEPTH, N_EXPERTS, d, 2 * D_FF), d ** -0.5),
        'b_gu': nrm(25, (DEPTH, N_EXPERTS, 2 * D_FF), 0.02),
        'w_down': nrm(26, (DEPTH, N_EXPERTS, D_FF, d), D_FF ** -0.5),
        'b_down': nrm(27, (DEPTH, N_EXPERTS, d), 0.02),
        'g_final': 1.0 + nrm(28, (d,), 0.02),
    }


def reference(x_prompt, x_sample, cache_a_k, cache_a_v, cache_b_k, cache_b_v, cache_c_kv, cache_c_kr,
              c, c_ctx, w_ada, b_ada, g_attn, g_ffn, w_in, sink_a, rpb_b, g_cq, g_ckv, w_uq, w_ukv,
              w_out, w_router, b_router, w_gu, b_gu, w_down, b_down, g_final):
    xp, xs = x_prompt, x_sample
    bp, sp, _ = xp.shape
    bs, ns, _ = xs.shape
    st_a_k, st_a_v, st_b_k, st_b_v, st_c_kv, st_c_kr = [], [], [], [], [], []
    for layer in range(DEPTH):
        m_ctx = jnp.split(jax.nn.silu(c_ctx) @ w_ada[layer] + b_ada[layer], 6, axis=-1)
        m_lat = [m[:, None, :] for m in jnp.split(jax.nn.silu(c) @ w_ada[layer] + b_ada[layer], 6, axis=-1)]

        h = modulate(rmsnorm(xp, g_attn[layer]), m_ctx[0], m_ctx[1])
        o, ka, va, kb, vb, ckv, kr = context_mixers(h, w_in[layer], sink_a[layer], g_cq[layer], g_ckv[layer],
                                                   w_uq[layer], w_ukv[layer])
        xp = xp + m_ctx[2] * (o @ w_out[layer])
        h = modulate(rmsnorm(xp, g_ffn[layer]), m_ctx[3], m_ctx[4])
        f = moe_ffn(h.reshape(bp * sp, D_MODEL), w_router[layer], b_router[layer], w_gu[layer], b_gu[layer],
                    w_down[layer], b_down[layer])
        xp = xp + m_ctx[5] * f.reshape(bp, sp, D_MODEL)
        st_a_k.append(ka)
        st_a_v.append(va)
        st_b_k.append(kb)
        st_b_v.append(vb)
        st_c_kv.append(ckv)
        st_c_kr.append(kr)

        h = modulate(rmsnorm(xs, g_attn[layer]), m_lat[0], m_lat[1])
        o = latent_mixers(h, cache_a_k[:, layer], cache_a_v[:, layer], cache_b_k[:, layer], cache_b_v[:, layer],
                          cache_c_kv[:, layer], cache_c_kr[:, layer], w_in[layer], sink_a[layer], rpb_b[layer],
                          g_cq[layer], g_ckv[layer], w_uq[layer], w_ukv[layer])
        xs = xs + m_lat[2] * (o @ w_out[layer])
        h = modulate(rmsnorm(xs, g_ffn[layer]), m_lat[3], m_lat[4])
        f = moe_ffn(h.reshape(bs * ns, D_MODEL), w_router[layer], b_router[layer], w_gu[layer], b_gu[layer],
                    w_down[layer], b_down[layer])
        xs = xs + m_lat[5] * f.reshape(bs, ns, D_MODEL)

    y_prompt = rmsnorm(xp, g_final)
    y_sample = rmsnorm(xs, g_final)
    new_a_k = jnp.stack(st_a_k, axis=1)
    new_a_v = jnp.stack(st_a_v, axis=1)
    new_b_k = jnp.stack(st_b_k, axis=1)
    new_b_v = jnp.stack(st_b_v, axis=1)
    new_c_kv = jnp.stack(st_c_kv, axis=1)
    new_c_kr = jnp.stack(st_c_kr, axis=1)
    return (y_prompt, y_sample, new_a_k, new_a_v, new_b_k, new_b_v, new_c_kv, new_c_kr)
```

```python
import functools

import numpy as np
import jax
import jax.numpy as jnp
from jax import lax
from jax.experimental import pallas as pl
from jax.experimental.pallas import tpu as pltpu

D_MODEL = 1024
BATCH = 32
SEQ = 256
DEPTH = 2
DEC_BATCH = 2
DEC_SEQ = 1024
PAST_LEN = 512
GRID_W = 64
HEAD_DIM = 64
H_A = 6
KV_A = 2
G_A = H_A // KV_A
WINDOW = 128
BLOCK = 128
H_B = 5
NA_ROWS = 8
NA_COLS = 16
H_C = 5
Q_LORA = 384
KV_LORA = 256
QK_NOPE = 64
QK_ROPE = 32
V_C = 64
N_EXPERTS = 32
TOP_K = 4
D_FF = 1024
SWIGLU_ALPHA = 1.702
SWIGLU_LIMIT = 7.0
ROPE_BASE = 10000.0
EPS = 1e-6
NEG = -1e30

T_CTX = BATCH * SEQ
T_LAT = DEC_BATCH * DEC_SEQ
T_ALL = T_CTX + T_LAT
N_GROUPS = 1 + DEC_BATCH
LANE = 128
QC_PAD = 128
ROWS = DEC_SEQ // GRID_W

W_QA, W_KA, W_VA = H_A * HEAD_DIM, KV_A * HEAD_DIM, KV_A * HEAD_DIM
W_B = H_B * HEAD_DIM
OFF_QA = 0
OFF_KA = 384
OFF_VA = 512
OFF_QB = 640
OFF_KB = 1024
OFF_VB = 1408
OFF_CQ = 1792
OFF_CKV = 2176
OFF_KR = 2432
NW_CTX = 2560
OFF_QA_P = 2560
OFF_KA_P = 2944
OFF_KR_P = 3072
NW_LAT = 3200

TM_TOK = 256
TM_LAT_IN = 512
TM_MOE = 256
N_MOE_BLOCKS = (T_ALL * TOP_K) // TM_MOE + N_EXPERTS
VMEM_LIMIT = 56 * 1024 * 1024

f32 = jnp.float32
bf16 = jnp.bfloat16


def _cparams(*sem):
    return pltpu.CompilerParams(dimension_semantics=sem, vmem_limit_bytes=VMEM_LIMIT)


def _rms(xf, g):
    return xf * lax.rsqrt(jnp.mean(xf * xf, axis=-1, keepdims=True) + EPS) * g


def _dot(a, b):
    return jnp.dot(a, b, preferred_element_type=f32)


def _dot_nt(a, b):
    return lax.dot_general(a, b, (((1,), (1,)), ((), ())), preferred_element_type=f32)


def _softmax_pv(s, v, sink=None):
    m = jnp.max(s, axis=-1, keepdims=True)
    if sink is not None:
        m = jnp.maximum(m, sink)
    p = jnp.exp(s - m)
    l = jnp.sum(p, axis=-1, keepdims=True)
    if sink is not None:
        l = l + jnp.exp(sink - m)
    return _dot(p.astype(bf16), v) / l


def _ada_kernel(c_ref, w_ref, b_ref, o_ref):
    c = c_ref[...]
    s = c * jax.nn.sigmoid(c)
    o_ref[0] = jnp.dot(s, w_ref[0], preferred_element_type=f32, precision=lax.Precision.HIGHEST) + b_ref[0]


def _ada(cvec, w_ada, b_ada):
    tn = 1536
    return pl.pallas_call(
        _ada_kernel,
        grid=(DEPTH, 6 * D_MODEL // tn),
        in_specs=[pl.BlockSpec((8, D_MODEL), lambda l, j: (0, 0)),
                  pl.BlockSpec((1, D_MODEL, tn), lambda l, j: (l, 0, j)),
                  pl.BlockSpec((1, 1, tn), lambda l, j: (l, 0, j))],
        out_specs=pl.BlockSpec((1, 8, tn), lambda l, j: (l, 0, j)),
        out_shape=jax.ShapeDtypeStruct((DEPTH, 8, 6 * D_MODEL), f32),
        compiler_params=_cparams("arbitrary", "arbitrary"),
        name="ada",
    )(cvec, w_ada, b_ada.reshape(DEPTH, 1, 6 * D_MODEL))


def _inproj_ctx_kernel(x_ref, g_ref, sh_ref, sc_ref, w_ref, gcq_ref, gckv_ref, wuq_ref,
                       qa_ref, ka_ref, va_ref, qb_ref, kb_ref, vb_ref, qc_ref, ckv_ref, kr_ref):
    h = _rms(x_ref[...], g_ref[...]) * (1.0 + sc_ref[0]) + sh_ref[0]
    p = _dot(h.astype(bf16), w_ref[...])
    qa_ref[...] = p[:, OFF_QA:OFF_QA + W_QA].astype(bf16)
    ka_ref[...] = p[:, OFF_KA:OFF_KA + W_KA]
    va_ref[...] = p[:, OFF_VA:OFF_VA + W_VA]
    qb_ref[...] = p[:, OFF_QB:OFF_QB + W_B].astype(bf16)
    kb_ref[...] = p[:, OFF_KB:OFF_KB + W_B]
    vb_ref[...] = p[:, OFF_VB:OFF_VB + W_B]
    cqn = _rms(p[:, OFF_CQ:OFF_CQ + Q_LORA], gcq_ref[...])
    qc_ref[...] = _dot(cqn.astype(bf16), wuq_ref[...]).astype(bf16)
    ckv_ref[...] = _rms(p[:, OFF_CKV:OFF_CKV + KV_LORA], gckv_ref[...])
    kr_ref[...] = p[:, OFF_KR:OFF_KR + QK_ROPE]


def _inproj_ctx(x, g, shift, scale, w, gcq, gckv, wuq):
    tm = TM_TOK
    row = lambda i: (i, 0)
    const = lambda i: (0, 0)
    widths = (W_QA, W_KA, W_VA, W_B, W_B, W_B, H_C * QC_PAD, KV_LORA, QK_ROPE)
    dtypes = (bf16, f32, f32, bf16, f32, f32, bf16, f32, f32)
    return pl.pallas_call(
        _inproj_ctx_kernel,
        grid=(T_CTX // tm,),
        in_specs=[pl.BlockSpec((tm, D_MODEL), row),
                  pl.BlockSpec((1, D_MODEL), const),
                  pl.BlockSpec((1, 1, D_MODEL), lambda i: (0, 0, 0)),
                  pl.BlockSpec((1, 1, D_MODEL), lambda i: (0, 0, 0)),
                  pl.BlockSpec((D_MODEL, NW_CTX), const),
                  pl.BlockSpec((1, Q_LORA), const),
                  pl.BlockSpec((1, KV_LORA), const),
                  pl.BlockSpec((Q_LORA, H_C * QC_PAD), const)],
        out_specs=[pl.BlockSpec((tm, wd), row) for wd in widths],
        out_shape=[jax.ShapeDtypeStruct((T_CTX, wd), dt) for wd, dt in zip(widths, dtypes)],
        compiler_params=_cparams("arbitrary"),
        name="inproj_ctx",
    )(x, g, shift, scale, w, gcq, gckv, wuq)


def _inproj_lat_kernel(x_ref, g_ref, sh_ref, sc_ref, w_ref, gcq_ref, gckv_ref, wuq_ref,
                       cosa_ref, sina_ref, cosq_ref, sinq_ref, cosr_ref, sinr_ref,
                       qa_ref, ka_ref, va_ref, qb_ref, kb_ref, vb_ref, qc_ref, ckv_ref, kr_ref):
    h = _rms(x_ref[...], g_ref[...]) * (1.0 + sc_ref[0]) + sh_ref[0]
    p = _dot(h.astype(bf16), w_ref[...])
    cosa = cosa_ref[...]
    sina = sina_ref[...]
    qa = p[:, OFF_QA:OFF_QA + W_QA] * cosa + p[:, OFF_QA_P:OFF_QA_P + W_QA] * sina
    ka = p[:, OFF_KA:OFF_KA + W_KA] * cosa[:, :W_KA] + p[:, OFF_KA_P:OFF_KA_P + W_KA] * sina[:, :W_KA]
    kr = p[:, OFF_KR:OFF_KR + QK_ROPE] * cosr_ref[...] + p[:, OFF_KR_P:OFF_KR_P + QK_ROPE] * sinr_ref[...]
    qa_ref[...] = qa.astype(bf16)
    ka_ref[...] = ka.astype(bf16)
    va_ref[...] = p[:, OFF_VA:OFF_VA + W_VA].astype(bf16)
    qb_ref[...] = p[:, OFF_QB:OFF_QB + W_B].astype(bf16)
    kb_ref[...] = p[:, OFF_KB:OFF_KB + W_B].astype(bf16)
    vb_ref[...] = p[:, OFF_VB:OFF_VB + W_B].astype(bf16)
    cqn = _rms(p[:, OFF_CQ:OFF_CQ + Q_LORA], gcq_ref[...])
    q2 = _dot(cqn.astype(bf16), wuq_ref[...])
    nq = H_C * QC_PAD
    qc_ref[...] = (q2[:, :nq] * cosq_ref[...] + q2[:, nq:] * sinq_ref[...]).astype(bf16)
    ckv_ref[...] = _rms(p[:, OFF_CKV:OFF_CKV + KV_LORA], gckv_ref[...]).astype(bf16)
    kr_ref[...] = kr.astype(bf16)


def _inproj_lat(x, g, shift, scale, w, gcq, gckv, wuq2, tabs):
    tm = TM_LAT_IN
    per_b = DEC_SEQ // tm
    row0 = T_CTX // tm
    xrow = lambda i: (row0 + i, 0)
    row = lambda i: (i, 0)
    const = lambda i: (0, 0)
    grp = lambda i: (1 + i // per_b, 0, 0)
    pos = lambda i: (i % per_b, 0)
    cosa, sina, cosq, sinq, cosr, sinr = tabs
    widths = (W_QA, W_KA, W_VA, W_B, W_B, W_B, H_C * QC_PAD, KV_LORA, QK_ROPE)
    return pl.pallas_call(
        _inproj_lat_kernel,
        grid=(T_LAT // tm,),
        in_specs=[pl.BlockSpec((tm, D_MODEL), xrow),
                  pl.BlockSpec((1, D_MODEL), const),
                  pl.BlockSpec((1, 1, D_MODEL), grp),
                  pl.BlockSpec((1, 1, D_MODEL), grp),
                  pl.BlockSpec((D_MODEL, NW_LAT), const),
                  pl.BlockSpec((1, Q_LORA), const),
                  pl.BlockSpec((1, KV_LORA), const),
                  pl.BlockSpec((Q_LORA, 2 * H_C * QC_PAD), const),
                  pl.BlockSpec((tm, W_QA), pos), pl.BlockSpec((tm, W_QA), pos),
                  pl.BlockSpec((tm, H_C * QC_PAD), pos), pl.BlockSpec((tm, H_C * QC_PAD), pos),
                  pl.BlockSpec((tm, QK_ROPE), pos), pl.BlockSpec((tm, QK_ROPE), pos)],
        out_specs=[pl.BlockSpec((tm, wd), row) for wd in widths],
        out_shape=[jax.ShapeDtypeStruct((T_LAT, wd), bf16) for wd in widths],
        compiler_params=_cparams("arbitrary"),
        name="inproj_lat",
    )(x, g, shift, scale, w, gcq, gckv, wuq2, cosa, sina, cosq, sinq, cosr, sinr)


def _ctx_attn_kernel(sink_ref, qa_ref, ka_ref, va_ref, qb_ref, kb_ref, vb_ref, qc_ref, ckv_ref, kr_ref,
                     wukv_ref, wout_ref, x_ref, gate_ref, o_ref, o_scr):
    scale = HEAD_DIM ** -0.5
    ka = ka_ref[...].astype(bf16)
    va = va_ref[...].astype(bf16)
    for h in range(H_A):
        g = h // G_A
        q = qa_ref[:, h * HEAD_DIM:(h + 1) * HEAD_DIM]
        s = _dot_nt(q, ka[:, g * HEAD_DIM:(g + 1) * HEAD_DIM]) * scale
        o_scr[:, h * HEAD_DIM:(h + 1) * HEAD_DIM] = _softmax_pv(s, va[:, g * HEAD_DIM:(g + 1) * HEAD_DIM],
                                                               sink_ref[h])
    kb = kb_ref[...].astype(bf16)
    vb = vb_ref[...].astype(bf16)
    for h in range(H_B):
        sl = slice(h * HEAD_DIM, (h + 1) * HEAD_DIM)
        s = _dot_nt(qb_ref[:, sl], kb[:, sl]) * scale
        o_scr[:, W_QA + h * HEAD_DIM:W_QA + (h + 1) * HEAD_DIM] = _softmax_pv(s, vb[:, sl])
    kv = _dot(ckv_ref[...].astype(bf16), wukv_ref[...]).astype(bf16)
    kr = kr_ref[...].astype(bf16)
    scale_c = (QK_NOPE + QK_ROPE) ** -0.5
    for h in range(H_C):
        qn = qc_ref[:, h * QC_PAD:h * QC_PAD + QK_NOPE]
        qr = qc_ref[:, h * QC_PAD + QK_NOPE:h * QC_PAD + QK_NOPE + QK_ROPE]
        c0 = h * (QK_NOPE + V_C)
        s = (_dot_nt(qn, kv[:, c0:c0 + QK_NOPE]) + _dot_nt(qr, kr)) * scale_c
        off = W_QA + W_B + h * V_C
        o_scr[:, off:off + V_C] = _softmax_pv(s, kv[:, c0 + QK_NOPE:c0 + QK_NOPE + V_C])
    y = _dot(o_scr[...].astype(bf16), wout_ref[...])
    o_ref[...] = x_ref[...] + gate_ref[0] * y


def _ctx_attn(sink, proj, wukv, wout, x, gate):
    qa, ka, va, qb, kb, vb, qc, ckv, kr = proj
    row = lambda b: (b, 0)
    const = lambda b: (0, 0)
    in_specs = [pl.BlockSpec(memory_space=pltpu.SMEM)]
    in_specs += [pl.BlockSpec((SEQ, a.shape[1]), row) for a in proj]
    in_specs += [pl.BlockSpec((KV_LORA, H_C * (QK_NOPE + V_C)), const),
                 pl.BlockSpec((D_MODEL, D_MODEL), const),
                 pl.BlockSpec((SEQ, D_MODEL), row),
                 pl.BlockSpec((1, 1, D_MODEL), lambda b: (0, 0, 0))]
    return pl.pallas_call(
        _ctx_attn_kernel,
        grid=(BATCH,),
        in_specs=in_specs,
        out_specs=pl.BlockSpec((SEQ, D_MODEL), row),
        out_shape=jax.ShapeDtypeStruct((T_CTX, D_MODEL), f32),
        scratch_shapes=[pltpu.VMEM((SEQ, D_MODEL), f32)],
        compiler_params=_cparams("arbitrary"),
        name="ctx_attn",
    )(sink, qa, ka, va, qb, kb, vb, qc, ckv, kr, wukv, wout, x, gate)


def _lat_attn_kernel(sink_ref, qa_ref, qb_ref, qc_ref, ka_ref, va_ref, kb_ref, vb_ref, ckv_ref, kr_ref,
                     cak_ref, cav_ref, cbk_ref, cbv_ref, cckv_ref, ckr_ref, bias_ref,
                     wukv_ref, wout_ref, x_ref, gate_ref, o_ref, o_scr, kv_scr):
    qi = pl.program_id(1)
    nb = DEC_SEQ // BLOCK
    scale = HEAD_DIM ** -0.5

    @pl.when(qi == 0)
    def _():
        kv_scr[0:DEC_SEQ, :] = _dot(ckv_ref[...], wukv_ref[...]).astype(bf16)
        kv_scr[DEC_SEQ:DEC_SEQ + PAST_LEN, :] = _dot(cckv_ref[0, 0].astype(bf16), wukv_ref[...]).astype(bf16)

    def blk(ref, j):
        idx = jnp.clip(qi + j, 0, nb - 1)
        return ref[pl.ds(pl.multiple_of(idx * BLOCK, BLOCK), BLOCK), :]

    ka = jnp.concatenate([blk(ka_ref, -1), blk(ka_ref, 0), blk(ka_ref, 1), cak_ref[0, 0].astype(bf16)], axis=0)
    va = jnp.concatenate([blk(va_ref, -1), blk(va_ref, 0), blk(va_ref, 1), cav_ref[0, 0].astype(bf16)], axis=0)
    nk_a = 3 * BLOCK + PAST_LEN
    r = lax.broadcasted_iota(jnp.int32, (BLOCK, nk_a), 0)
    c = lax.broadcasted_iota(jnp.int32, (BLOCK, nk_a), 1)
    valid = (((c < BLOCK) & (c >= r) & (qi > 0))
             | ((c >= BLOCK) & (c < 2 * BLOCK))
             | ((c >= 2 * BLOCK) & (c < 3 * BLOCK) & (c - 2 * BLOCK <= r) & (qi < nb - 1))
             | (c >= 3 * BLOCK))
    for h in range(H_A):
        g = h // G_A
        q = qa_ref[:, h * HEAD_DIM:(h + 1) * HEAD_DIM]
        s = _dot_nt(q, ka[:, g * HEAD_DIM:(g + 1) * HEAD_DIM]) * scale
        s = jnp.where(valid, s, NEG)
        o_scr[:, h * HEAD_DIM:(h + 1) * HEAD_DIM] = _softmax_pv(s, va[:, g * HEAD_DIM:(g + 1) * HEAD_DIM],
                                                               sink_ref[h])

    cbk = cbk_ref[0, 0].astype(bf16)
    cbv = cbv_ref[0, 0].astype(bf16)
    rows_per_blk = BLOCK // GRID_W
    nloc = NA_ROWS * GRID_W
    for half in range(rows_per_blk):
        grow = qi * rows_per_blk + half
        start = jnp.clip(grow - NA_ROWS // 2, 0, ROWS - NA_ROWS)
        kloc = kb_ref[pl.ds(pl.multiple_of(start * GRID_W, GRID_W), nloc), :]
        vloc = vb_ref[pl.ds(pl.multiple_of(start * GRID_W, GRID_W), nloc), :]
        vcat = jnp.concatenate([vloc, cbv], axis=0)
        qrows = slice(half * GRID_W, (half + 1) * GRID_W)
        for h in range(H_B):
            sl = slice(h * HEAD_DIM, (h + 1) * HEAD_DIM)
            q = qb_ref[qrows, sl]
            s_loc = _dot_nt(q, kloc[:, sl]) * scale + bias_ref[half, h]
            s_ctx = _dot_nt(q, cbk[:, sl]) * scale
            s = jnp.concatenate([s_loc, s_ctx], axis=1)
            o_scr[qrows, W_QA + h * HEAD_DIM:W_QA + (h + 1) * HEAD_DIM] = _softmax_pv(s, vcat[:, sl])

    kr = jnp.concatenate([kr_ref[...], ckr_ref[0, 0].astype(bf16)], axis=0)
    scale_c = (QK_NOPE + QK_ROPE) ** -0.5
    for h in range(H_C):
        qn = qc_ref[:, h * QC_PAD:h * QC_PAD + QK_NOPE]
        qr = qc_ref[:, h * QC_PAD + QK_NOPE:h * QC_PAD + QK_NOPE + QK_ROPE]
        c0 = h * (QK_NOPE + V_C)
        s = (_dot_nt(qn, kv_scr[:, c0:c0 + QK_NOPE]) + _dot_nt(qr, kr)) * scale_c
        off = W_QA + W_B + h * V_C
        o_scr[:, off:off + V_C] = _softmax_pv(s, kv_scr[:, c0 + QK_NOPE:c0 + QK_NOPE + V_C])

    y = _dot(o_scr[...].astype(bf16), wout_ref[...])
    o_ref[...] = x_ref[...] + gate_ref[0] * y


def _lat_attn(layer, sink, proj, caches, bias_tab, wukv, wout, x, gate):
    qa, ka, va, qb, kb, vb, qc, ckv, kr = proj
    nb = DEC_SEQ // BLOCK
    qrow = lambda b, q: (b * nb + q, 0)
    xrow = lambda b, q: (T_CTX // BLOCK + b * nb + q, 0)
    brow = lambda b, q: (b, 0)
    const = lambda b, q: (0, 0)
    cidx = lambda b, q: (b, layer, 0, 0)
    in_specs = [pl.BlockSpec(memory_space=pltpu.SMEM)]
    in_specs += [pl.BlockSpec((BLOCK, a.shape[1]), qrow) for a in (qa, qb, qc)]
    in_specs += [pl.BlockSpec((DEC_SEQ, a.shape[1]), brow) for a in (ka, va, kb, vb, ckv, kr)]
    in_specs += [pl.BlockSpec((1, 1, PAST_LEN, a.shape[3]), cidx) for a in caches]
    in_specs += [pl.BlockSpec((BLOCK // GRID_W, H_B, GRID_W, NA_ROWS * GRID_W), lambda b, q: (q, 0, 0, 0)),
                 pl.BlockSpec((KV_LORA, H_C * (QK_NOPE + V_C)), const),
                 pl.BlockSpec((D_MODEL, D_MODEL), const),
                 pl.BlockSpec((BLOCK, D_MODEL), xrow),
                 pl.BlockSpec((1, 1, D_MODEL), lambda b, q: (1 + b, 0, 0))]
    return pl.pallas_call(
        _lat_attn_kernel,
        grid=(DEC_BATCH, nb),
        in_specs=in_specs,
        out_specs=pl.BlockSpec((BLOCK, D_MODEL), qrow),
        out_shape=jax.ShapeDtypeStruct((T_LAT, D_MODEL), f32),
        scratch_shapes=[pltpu.VMEM((BLOCK, D_MODEL), f32),
                        pltpu.VMEM((DEC_SEQ + PAST_LEN, H_C * (QK_NOPE + V_C)), bf16)],
        compiler_params=_cparams("arbitrary", "arbitrary"),
        name="lat_attn",
    )(sink, qa, qb, qc, ka, va, kb, vb, ckv, kr, *caches, bias_tab, wukv, wout, x, gate)


def _router_kernel(x_ref, g_ref, sh_ref, sc_ref, wr_ref, br_ref, h_ref, e_ref, gt_ref):
    h = _rms(x_ref[...], g_ref[...]) * (1.0 + sc_ref[0]) + sh_ref[0]
    h_ref[...] = h
    logits = jnp.dot(h, wr_ref[...], preferred_element_type=f32, precision=lax.Precision.HIGHEST) + br_ref[...]
    lane = lax.broadcasted_iota(jnp.int32, logits.shape, 1).astype(f32)
    l = jnp.where(lane < N_EXPERTS, logits, -jnp.inf)
    tops, idxs = [], []
    for _ in range(TOP_K):
        m = jnp.max(l, axis=-1, keepdims=True)
        idx = jnp.min(jnp.where(l == m, lane, float(LANE)), axis=-1, keepdims=True)
        tops.append(m)
        idxs.append(idx)
        l = jnp.where(lane == idx, -jnp.inf, l)
    ex = [jnp.exp(t - tops[0]) for t in tops]
    den = ex[0] + ex[1] + ex[2] + ex[3]
    e_out = jnp.zeros(logits.shape, f32)
    g_out = jnp.zeros(logits.shape, f32)
    for k in range(TOP_K):
        e_out = jnp.where(lane == k, idxs[k], e_out)
        g_out = jnp.where(lane == k, ex[k] / den, g_out)
    e_ref[...] = e_out.astype(jnp.int32)
    gt_ref[...] = g_out


def _group_of_tile(i):
    per_b = DEC_SEQ // TM_TOK
    n_ctx = T_CTX // TM_TOK
    return jnp.where(i < n_ctx, 0, 1 + (i - n_ctx) // per_b)


def _router(x, g, shift, scale, wr, br):
    tm = TM_TOK
    row = lambda i: (i, 0)
    const = lambda i: (0, 0)
    grp = lambda i: (_group_of_tile(i), 0, 0)
    return pl.pallas_call(
        _router_kernel,
        grid=(T_ALL // tm,),
        in_specs=[pl.BlockSpec((tm, D_MODEL), row),
                  pl.BlockSpec((1, D_MODEL), const),
                  pl.BlockSpec((1, 1, D_MODEL), grp),
                  pl.BlockSpec((1, 1, D_MODEL), grp),
                  pl.BlockSpec((D_MODEL, LANE), const),
                  pl.BlockSpec((1, LANE), const)],
        out_specs=[pl.BlockSpec((tm, D_MODEL), row), pl.BlockSpec((tm, LANE), row), pl.BlockSpec((tm, LANE), row)],
        out_shape=[jax.ShapeDtypeStruct((T_ALL, D_MODEL), f32),
                   jax.ShapeDtypeStruct((T_ALL, LANE), jnp.int32),
                   jax.ShapeDtypeStruct((T_ALL, LANE), f32)],
        compiler_params=_cparams("arbitrary"),
        name="router",
    )(x, g, shift, scale, wr, br)


def _moe_kernel(be_ref, nu_ref, x_ref, wgu_ref, bgu_ref, wd_ref, bd_ref, o_ref, wgu_bf, wd_bf):
    i = pl.program_id(0)

    @pl.when(i < nu_ref[0])
    def _():
        first = jnp.logical_or(i == 0, be_ref[i] != be_ref[jnp.maximum(i - 1, 0)])

        @pl.when(first)
        def _():
            wgu_bf[...] = wgu_ref[0].astype(bf16)
            wd_bf[...] = wd_ref[0].astype(bf16)

        gu = _dot(x_ref[...], wgu_bf[...]) + bgu_ref[0]
        x_glu = jnp.minimum(gu[:, :D_FF], SWIGLU_LIMIT)
        x_lin = jnp.clip(gu[:, D_FF:], -SWIGLU_LIMIT, SWIGLU_LIMIT)
        act = x_glu * jax.nn.sigmoid(SWIGLU_ALPHA * x_glu) * (x_lin + 1.0)
        o_ref[...] = _dot(act.astype(bf16), wd_bf[...]) + bd_ref[0]

    @pl.when(i >= nu_ref[0])
    def _():
        o_ref[...] = jnp.zeros_like(o_ref)


def _moe(block_e, n_used, xs, w_gu, b_gu, w_down, b_down):
    tm = TM_MOE
    row = lambda i, be, nu: (i, 0)
    ex3 = lambda i, be, nu: (be[i], 0, 0)
    return pl.pallas_call(
        _moe_kernel,
        grid_spec=pltpu.PrefetchScalarGridSpec(
            num_scalar_prefetch=2,
            grid=(N_MOE_BLOCKS,),
            in_specs=[pl.BlockSpec((tm, D_MODEL), row),
                      pl.BlockSpec((1, D_MODEL, 2 * D_FF), ex3),
                      pl.BlockSpec((1, 1, 2 * D_FF), ex3),
                      pl.BlockSpec((1, D_FF, D_MODEL), ex3),
                      pl.BlockSpec((1, 1, D_MODEL), ex3)],
            out_specs=pl.BlockSpec((tm, D_MODEL), row),
            scratch_shapes=[pltpu.VMEM((D_MODEL, 2 * D_FF), bf16), pltpu.VMEM((D_FF, D_MODEL), bf16)]),
        out_shape=jax.ShapeDtypeStruct((N_MOE_BLOCKS * tm, D_MODEL), f32),
        compiler_params=_cparams("arbitrary"),
        name="moe",
    )(block_e, n_used, xs, w_gu, b_gu.reshape(N_EXPERTS, 1, 2 * D_FF), w_down, b_down.reshape(N_EXPERTS, 1, D_MODEL))


def _combine_kernel(final, x_ref, y_ref, gt_ref, gate_ref, gf_ref, o_ref):
    gt = gt_ref[...]
    f = gt[:, 0:1] * y_ref[:, 0:D_MODEL]
    for k in range(1, TOP_K):
        f = f + gt[:, k:k + 1] * y_ref[:, k * D_MODEL:(k + 1) * D_MODEL]
    out = x_ref[...] + gate_ref[0] * f
    if final:
        out = _rms(out, gf_ref[...])
    o_ref[...] = out


def _combine(final, x, y4, gates, gate, g_final):
    tm = TM_TOK
    row = lambda i: (i, 0)
    const = lambda i: (0, 0)
    grp = lambda i: (_group_of_tile(i), 0, 0)
    return pl.pallas_call(
        functools.partial(_combine_kernel, final),
        grid=(T_ALL // tm,),
        in_specs=[pl.BlockSpec((tm, D_MODEL), row),
                  pl.BlockSpec((tm, TOP_K * D_MODEL), row),
                  pl.BlockSpec((tm, LANE), row),
                  pl.BlockSpec((1, 1, D_MODEL), grp),
                  pl.BlockSpec((1, D_MODEL), const)],
        out_specs=pl.BlockSpec((tm, D_MODEL), row),
        out_shape=jax.ShapeDtypeStruct((T_ALL, D_MODEL), f32),
        compiler_params=_cparams("arbitrary"),
        name="combine",
    )(x, y4, gates, gate, g_final)


def _rope_head_tables(d):
    nf = d // 4
    half = d // 2
    t = np.arange(DEC_SEQ)
    inv = ROPE_BASE ** (-np.arange(nf, dtype=np.float32) / nf)
    i = np.arange(d)
    pos = np.where(i[None, :] < half, (t // GRID_W)[:, None], (t % GRID_W)[:, None]).astype(np.float32)
    ang = pos * inv[i % nf][None, :].astype(np.float32)
    first = (i % half) < nf
    cos = np.cos(ang)
    sin = np.where(first[None, :], -np.sin(ang), np.sin(ang))
    partner = np.where(first, i + nf, i - nf)
    return cos.astype(np.float32), sin.astype(np.float32), partner


def _rope_tables():
    cos64, sin64, _ = _rope_head_tables(HEAD_DIM)
    cos32, sin32, _ = _rope_head_tables(QK_ROPE)
    cosa = np.tile(cos64, (1, H_A))
    sina = np.tile(sin64, (1, H_A))
    cosq1 = np.concatenate([np.ones((DEC_SEQ, QK_NOPE), np.float32), cos32,
                            np.ones((DEC_SEQ, QC_PAD - QK_NOPE - QK_ROPE), np.float32)], axis=1)
    sinq1 = np.concatenate([np.zeros((DEC_SEQ, QK_NOPE), np.float32), sin32,
                            np.zeros((DEC_SEQ, QC_PAD - QK_NOPE - QK_ROPE), np.float32)], axis=1)
    cosq = np.tile(cosq1, (1, H_C))
    sinq = np.tile(sinq1, (1, H_C))
    return tuple(jnp.asarray(a) for a in (cosa, sina, cosq, sinq, cos32, sin32))


def _pad_cols(w, n):
    return jnp.pad(w, ((0, 0), (0, n - w.shape[1])))


def _layer_weights(w_in, w_uq):
    cuts = np.cumsum((W_QA, W_KA, W_VA, W_B, W_B, W_B, Q_LORA, KV_LORA, QK_ROPE))[:-1]
    qa, ka, va, qb, kb, vb, cq, ckv, kr = jnp.split(w_in, [int(c) for c in cuts], axis=1)
    _, _, p64 = _rope_head_tables(HEAD_DIM)
    _, _, p32 = _rope_head_tables(QK_ROPE)
    pa = np.concatenate([h * HEAD_DIM + p64 for h in range(H_A)])
    base = jnp.concatenate([qa, ka, va, _pad_cols(qb, 384), _pad_cols(kb, 384), _pad_cols(vb, 384), cq, ckv,
                            _pad_cols(kr, 128)], axis=1)
    w_ctx = base.astype(bf16)
    w_lat = jnp.concatenate([base, qa[:, pa], ka[:, pa[:W_KA]], _pad_cols(kr[:, p32], 128)], axis=1).astype(bf16)
    hq = QK_NOPE + QK_ROPE
    heads = [_pad_cols(w_uq[:, h * hq:(h + 1) * hq], QC_PAD) for h in range(H_C)]
    pq = np.concatenate([np.arange(QK_NOPE), QK_NOPE + p32])
    heads_p = [_pad_cols(w_uq[:, h * hq:(h + 1) * hq][:, pq], QC_PAD) for h in range(H_C)]
    wuq = jnp.concatenate(heads, axis=1).astype(bf16)
    wuq2 = jnp.concatenate(heads + heads_p, axis=1).astype(bf16)
    return w_ctx, w_lat, wuq, wuq2


def _bias_table(rpb):
    r = np.arange(ROWS)
    key_rows = np.clip(r - NA_ROWS // 2, 0, ROWS - NA_ROWS)[:, None] + np.arange(NA_ROWS)[None, :]
    col = np.arange(GRID_W)
    col_start = np.clip(col - NA_COLS // 2, 0, GRID_W - NA_COLS)
    col_ok = (col[None, :] >= col_start[:, None]) & (col[None, :] < col_start[:, None] + NA_COLS)
    dr = key_rows - r[:, None] + (NA_ROWS - 1)
    dc = np.clip(col[None, :] - col[:, None] + (NA_COLS - 1), 0, 2 * NA_COLS - 2)
    bias = rpb[:, dr][:, :, :, dc]
    bias = jnp.where(col_ok[None, None, None], bias, NEG)
    bias = jnp.transpose(bias, (1, 0, 3, 2, 4))
    return bias.reshape(ROWS, H_B, GRID_W, NA_ROWS * GRID_W).astype(f32)


def _routing(top_e):
    tk = T_ALL * TOP_K
    flat_e = top_e.reshape(tk)
    order = jnp.argsort(flat_e)
    e_sorted = flat_e[order]
    counts = jnp.sum((flat_e[:, None] == jnp.arange(N_EXPERTS)[None, :]).astype(jnp.int32), axis=0)
    padded = (counts + TM_MOE - 1) // TM_MOE * TM_MOE
    grp_start = jnp.cumsum(counts) - counts
    pad_end = jnp.cumsum(padded)
    pad_start = pad_end - padded
    dest_sorted = pad_start[e_sorted] + jnp.arange(tk, dtype=jnp.int32) - grp_start[e_sorted]
    block_e = jnp.minimum(jnp.searchsorted(pad_end, jnp.arange(N_MOE_BLOCKS, dtype=jnp.int32) * TM_MOE,
                                           side='right'), N_EXPERTS - 1).astype(jnp.int32)
    n_used = (pad_end[-1] // TM_MOE).astype(jnp.int32).reshape(1)
    row_tok = jnp.zeros((N_MOE_BLOCKS * TM_MOE,), jnp.int32).at[dest_sorted].set(order // TOP_K)
    dest = jnp.zeros((tk,), jnp.int32).at[order].set(dest_sorted)
    return block_e, n_used, row_tok, dest


def kernel(x_prompt, x_sample, cache_a_k, cache_a_v, cache_b_k, cache_b_v, cache_c_kv, cache_c_kr, c, c_ctx, w_ada, b_ada, g_attn, g_ffn, w_in, sink_a, rpb_b, g_cq, g_ckv, w_uq, w_ukv, w_out, w_router, b_router, w_gu, b_gu, w_down, b_down, g_final):
    x = jnp.concatenate([x_prompt.reshape(T_CTX, D_MODEL), x_sample.reshape(T_LAT, D_MODEL)], axis=0)
    cvec = jnp.concatenate([c_ctx[None, :], c, jnp.zeros((8 - N_GROUPS, D_MODEL), f32)], axis=0)
    mods = _ada(cvec, w_ada, b_ada)[:, :N_GROUPS].reshape(DEPTH, N_GROUPS, 6, 1, D_MODEL)
    tabs = _rope_tables()
    caches = (cache_a_k.reshape(DEC_BATCH, DEPTH, PAST_LEN, W_KA), cache_a_v.reshape(DEC_BATCH, DEPTH, PAST_LEN, W_VA),
              cache_b_k.reshape(DEC_BATCH, DEPTH, PAST_LEN, W_B), cache_b_v.reshape(DEC_BATCH, DEPTH, PAST_LEN, W_B),
              cache_c_kv, cache_c_kr)
    new = [[] for _ in range(6)]
    for layer in range(DEPTH):
        m = [mods[layer, :, j] for j in range(6)]
        w_ctx, w_lat, wuq, wuq2 = _layer_weights(w_in[layer], w_uq[layer])
        wukv = w_ukv[layer].astype(bf16)
        wout = w_out[layer].astype(bf16)
        g1 = g_attn[layer][None, :]
        gcq = g_cq[layer][None, :]
        gckv = g_ckv[layer][None, :]
        sink = sink_a[layer]

        pc = _inproj_ctx(x, g1, m[0], m[1], w_ctx, gcq, gckv, wuq)
        for lst, a in zip(new, (pc[1], pc[2], pc[4], pc[5], pc[7], pc[8])):
            lst.append(a)
        x_ctx = _ctx_attn(sink, pc, wukv, wout, x, m[2])

        plat = _inproj_lat(x, g1, m[0], m[1], w_lat, gcq, gckv, wuq2, tabs)
        x_lat = _lat_attn(layer, sink, plat, caches, _bias_table(rpb_b[layer]), wukv, wout, x, m[2])
        x = jnp.concatenate([x_ctx, x_lat], axis=0)

        wr = _pad_cols(w_router[layer], LANE)
        br = _pad_cols(b_router[layer][None, :], LANE)
        h2, top_e, gates = _router(x, g_ffn[layer][None, :], m[3], m[4], wr, br)
        block_e, n_used, row_tok, dest = _routing(top_e[:, :TOP_K])
        xs = h2[row_tok].astype(bf16)
        y = _moe(block_e, n_used, xs, w_gu[layer], b_gu[layer], w_down[layer], b_down[layer])
        y4 = y[dest].reshape(T_ALL, TOP_K * D_MODEL)
        x = _combine(layer == DEPTH - 1, x, y4, gates, m[5], g_final[None, :])

    y_prompt = x[:T_CTX].reshape(BATCH, SEQ, D_MODEL)
    y_sample = x[T_CTX:].reshape(DEC_BATCH, DEC_SEQ, D_MODEL)
    shapes = ((KV_A, HEAD_DIM), (KV_A, HEAD_DIM), (H_B, HEAD_DIM), (H_B, HEAD_DIM), (KV_LORA,), (QK_ROPE,))
    outs = [jnp.stack([a.reshape((BATCH, SEQ) + s) for a in lst], axis=1) for lst, s in zip(new, shapes)]
    return (y_prompt, y_sample, *outs)
```

```python
import functools

import numpy as np
import jax
import jax.numpy as jnp
from jax import lax
from jax.experimental import pallas as pl
from jax.experimental.pallas import tpu as pltpu

D_MODEL = 1024
BATCH = 32
SEQ = 256
DEPTH = 2
DEC_BATCH = 2
DEC_SEQ = 1024
PAST_LEN = 512
GRID_W = 64
HEAD_DIM = 64
H_A = 6
KV_A = 2
G_A = H_A // KV_A
WINDOW = 128
BLOCK = 128
H_B = 5
NA_ROWS = 8
NA_COLS = 16
H_C = 5
Q_LORA = 384
KV_LORA = 256
QK_NOPE = 64
QK_ROPE = 32
V_C = 64
N_EXPERTS = 32
TOP_K = 4
D_FF = 1024
SWIGLU_ALPHA = 1.702
SWIGLU_LIMIT = 7.0
ROPE_BASE = 10000.0
EPS = 1e-6
NEG = -1e30

T_CTX = BATCH * SEQ
T_LAT = DEC_BATCH * DEC_SEQ
T_ALL = T_CTX + T_LAT
N_GROUPS = 1 + DEC_BATCH
LANE = 128
QC_PAD = 128
ROWS = DEC_SEQ // GRID_W

W_QA, W_KA, W_VA = H_A * HEAD_DIM, KV_A * HEAD_DIM, KV_A * HEAD_DIM
W_B = H_B * HEAD_DIM
OFF_QA = 0
OFF_KA = 384
OFF_VA = 512
OFF_QB = 640
OFF_KB = 1024
OFF_VB = 1408
OFF_CQ = 1792
OFF_CKV = 2176
OFF_KR = 2432
NW_CTX = 2560
OFF_QA_P = 2560
OFF_KA_P = 2944
OFF_KR_P = 3072
NW_LAT = 3200

TM_TOK = 256
TM_LAT_IN = 512
TM_MOE = 256
N_ASSIGN = T_ALL * TOP_K
N_MOE_BLOCKS = N_ASSIGN // TM_MOE + N_EXPERTS
VMEM_LIMIT = 56 * 1024 * 1024

f32 = jnp.float32
bf16 = jnp.bfloat16


def _cparams(*sem):
    return pltpu.CompilerParams(dimension_semantics=sem, vmem_limit_bytes=VMEM_LIMIT)


def _rms(xf, g):
    return xf * lax.rsqrt(jnp.mean(xf * xf, axis=-1, keepdims=True) + EPS) * g


def _dot(a, b):
    return jnp.dot(a, b, preferred_element_type=f32)


def _dot_nt(a, b):
    return lax.dot_general(a, b, (((1,), (1,)), ((), ())), preferred_element_type=f32)


def _softmax_pv(s, v, sink=None):
    m = jnp.max(s, axis=-1, keepdims=True)
    if sink is not None:
        m = jnp.maximum(m, sink)
    p = jnp.exp(s - m)
    l = jnp.sum(p, axis=-1, keepdims=True)
    if sink is not None:
        l = l + jnp.exp(sink - m)
    return _dot(p.astype(bf16), v) / l


def _ada_kernel(c_ref, w_ref, b_ref, o_ref):
    c = c_ref[...]
    s = c * jax.nn.sigmoid(c)
    o_ref[0] = jnp.dot(s, w_ref[0], preferred_element_type=f32, precision=lax.Precision.HIGHEST) + b_ref[0]


def _ada(cvec, w_ada, b_ada):
    tn = 1536
    return pl.pallas_call(
        _ada_kernel,
        grid=(DEPTH, 6 * D_MODEL // tn),
        in_specs=[pl.BlockSpec((8, D_MODEL), lambda l, j: (0, 0)),
                  pl.BlockSpec((1, D_MODEL, tn), lambda l, j: (l, 0, j)),
                  pl.BlockSpec((1, 1, tn), lambda l, j: (l, 0, j))],
        out_specs=pl.BlockSpec((1, 8, tn), lambda l, j: (l, 0, j)),
        out_shape=jax.ShapeDtypeStruct((DEPTH, 8, 6 * D_MODEL), f32),
        compiler_params=_cparams("arbitrary", "arbitrary"),
        name="ada",
    )(cvec, w_ada, b_ada.reshape(DEPTH, 1, 6 * D_MODEL))


def _inproj_ctx_kernel(x_ref, g_ref, sh_ref, sc_ref, w_ref, gcq_ref, gckv_ref, wuq_ref,
                       qa_ref, ka_ref, va_ref, qb_ref, kb_ref, vb_ref, qc_ref, ckv_ref, kr_ref):
    h = _rms(x_ref[...], g_ref[...]) * (1.0 + sc_ref[0]) + sh_ref[0]
    p = _dot(h.astype(bf16), w_ref[...])
    qa_ref[...] = p[:, OFF_QA:OFF_QA + W_QA].astype(bf16)
    ka_ref[...] = p[:, OFF_KA:OFF_KA + W_KA]
    va_ref[...] = p[:, OFF_VA:OFF_VA + W_VA]
    qb_ref[...] = p[:, OFF_QB:OFF_QB + W_B].astype(bf16)
    kb_ref[...] = p[:, OFF_KB:OFF_KB + W_B]
    vb_ref[...] = p[:, OFF_VB:OFF_VB + W_B]
    cqn = _rms(p[:, OFF_CQ:OFF_CQ + Q_LORA], gcq_ref[...])
    qc_ref[...] = _dot(cqn.astype(bf16), wuq_ref[...]).astype(bf16)
    ckv_ref[...] = _rms(p[:, OFF_CKV:OFF_CKV + KV_LORA], gckv_ref[...])
    kr_ref[...] = p[:, OFF_KR:OFF_KR + QK_ROPE]


def _inproj_ctx(x, g, shift, scale, w, gcq, gckv, wuq):
    tm = TM_TOK
    row = lambda i: (i, 0)
    const = lambda i: (0, 0)
    widths = (W_QA, W_KA, W_VA, W_B, W_B, W_B, H_C * QC_PAD, KV_LORA, QK_ROPE)
    dtypes = (bf16, f32, f32, bf16, f32, f32, bf16, f32, f32)
    return pl.pallas_call(
        _inproj_ctx_kernel,
        grid=(T_CTX // tm,),
        in_specs=[pl.BlockSpec((tm, D_MODEL), row),
                  pl.BlockSpec((1, D_MODEL), const),
                  pl.BlockSpec((1, 1, D_MODEL), lambda i: (0, 0, 0)),
                  pl.BlockSpec((1, 1, D_MODEL), lambda i: (0, 0, 0)),
                  pl.BlockSpec((D_MODEL, NW_CTX), const),
                  pl.BlockSpec((1, Q_LORA), const),
                  pl.BlockSpec((1, KV_LORA), const),
                  pl.BlockSpec((Q_LORA, H_C * QC_PAD), const)],
        out_specs=[pl.BlockSpec((tm, wd), row) for wd in widths],
        out_shape=[jax.ShapeDtypeStruct((T_CTX, wd), dt) for wd, dt in zip(widths, dtypes)],
        compiler_params=_cparams("arbitrary"),
        name="inproj_ctx",
    )(x, g, shift, scale, w, gcq, gckv, wuq)


def _inproj_lat_kernel(x_ref, g_ref, sh_ref, sc_ref, w_ref, gcq_ref, gckv_ref, wuq_ref,
                       cosa_ref, sina_ref, cosq_ref, sinq_ref, cosr_ref, sinr_ref,
                       qa_ref, ka_ref, va_ref, qb_ref, kb_ref, vb_ref, qc_ref, ckv_ref, kr_ref):
    h = _rms(x_ref[...], g_ref[...]) * (1.0 + sc_ref[0]) + sh_ref[0]
    p = _dot(h.astype(bf16), w_ref[...])
    cosa = cosa_ref[...]
    sina = sina_ref[...]
    qa = p[:, OFF_QA:OFF_QA + W_QA] * cosa + p[:, OFF_QA_P:OFF_QA_P + W_QA] * sina
    ka = p[:, OFF_KA:OFF_KA + W_KA] * cosa[:, :W_KA] + p[:, OFF_KA_P:OFF_KA_P + W_KA] * sina[:, :W_KA]
    kr = p[:, OFF_KR:OFF_KR + QK_ROPE] * cosr_ref[...] + p[:, OFF_KR_P:OFF_KR_P + QK_ROPE] * sinr_ref[...]
    qa_ref[...] = qa.astype(bf16)
    ka_ref[...] = ka.astype(bf16)
    va_ref[...] = p[:, OFF_VA:OFF_VA + W_VA].astype(bf16)
    qb_ref[...] = p[:, OFF_QB:OFF_QB + W_B].astype(bf16)
    kb_ref[...] = p[:, OFF_KB:OFF_KB + W_B].astype(bf16)
    vb_ref[...] = p[:, OFF_VB:OFF_VB + W_B].astype(bf16)
    cqn = _rms(p[:, OFF_CQ:OFF_CQ + Q_LORA], gcq_ref[...])
    q2 = _dot(cqn.astype(bf16), wuq_ref[...])
    nq = H_C * QC_PAD
    qc_ref[...] = (q2[:, :nq] * cosq_ref[...] + q2[:, nq:] * sinq_ref[...]).astype(bf16)
    ckv_ref[...] = _rms(p[:, OFF_CKV:OFF_CKV + KV_LORA], gckv_ref[...]).astype(bf16)
    kr_ref[...] = kr.astype(bf16)


def _inproj_lat(x, g, shift, scale, w, gcq, gckv, wuq2, tabs):
    tm = TM_LAT_IN
    per_b = DEC_SEQ // tm
    row0 = T_CTX // tm
    xrow = lambda i: (row0 + i, 0)
    row = lambda i: (i, 0)
    const = lambda i: (0, 0)
    grp = lambda i: (1 + i // per_b, 0, 0)
    pos = lambda i: (i % per_b, 0)
    cosa, sina, cosq, sinq, cosr, sinr = tabs
    widths = (W_QA, W_KA, W_VA, W_B, W_B, W_B, H_C * QC_PAD, KV_LORA, QK_ROPE)
    return pl.pallas_call(
        _inproj_lat_kernel,
        grid=(T_LAT // tm,),
        in_specs=[pl.BlockSpec((tm, D_MODEL), xrow),
                  pl.BlockSpec((1, D_MODEL), const),
                  pl.BlockSpec((1, 1, D_MODEL), grp),
                  pl.BlockSpec((1, 1, D_MODEL), grp),
                  pl.BlockSpec((D_MODEL, NW_LAT), const),
                  pl.BlockSpec((1, Q_LORA), const),
                  pl.BlockSpec((1, KV_LORA), const),
                  pl.BlockSpec((Q_LORA, 2 * H_C * QC_PAD), const),
                  pl.BlockSpec((tm, W_QA), pos), pl.BlockSpec((tm, W_QA), pos),
                  pl.BlockSpec((tm, H_C * QC_PAD), pos), pl.BlockSpec((tm, H_C * QC_PAD), pos),
                  pl.BlockSpec((tm, QK_ROPE), pos), pl.BlockSpec((tm, QK_ROPE), pos)],
        out_specs=[pl.BlockSpec((tm, wd), row) for wd in widths],
        out_shape=[jax.ShapeDtypeStruct((T_LAT, wd), bf16) for wd in widths],
        compiler_params=_cparams("arbitrary"),
        name="inproj_lat",
    )(x, g, shift, scale, w, gcq, gckv, wuq2, cosa, sina, cosq, sinq, cosr, sinr)


def _ctx_attn_kernel(sink_ref, qa_ref, ka_ref, va_ref, qb_ref, kb_ref, vb_ref, qc_ref, ckv_ref, kr_ref,
                     wukv_ref, wout_ref, x_ref, gate_ref, o_ref, o_scr):
    scale = HEAD_DIM ** -0.5
    ka = ka_ref[...].astype(bf16)
    va = va_ref[...].astype(bf16)
    for h in range(H_A):
        g = h // G_A
        q = qa_ref[:, h * HEAD_DIM:(h + 1) * HEAD_DIM]
        s = _dot_nt(q, ka[:, g * HEAD_DIM:(g + 1) * HEAD_DIM]) * scale
        o_scr[:, h * HEAD_DIM:(h + 1) * HEAD_DIM] = _softmax_pv(s, va[:, g * HEAD_DIM:(g + 1) * HEAD_DIM],
                                                               sink_ref[h])
    kb = kb_ref[...].astype(bf16)
    vb = vb_ref[...].astype(bf16)
    for h in range(H_B):
        sl = slice(h * HEAD_DIM, (h + 1) * HEAD_DIM)
        s = _dot_nt(qb_ref[:, sl], kb[:, sl]) * scale
        o_scr[:, W_QA + h * HEAD_DIM:W_QA + (h + 1) * HEAD_DIM] = _softmax_pv(s, vb[:, sl])
    kv = _dot(ckv_ref[...].astype(bf16), wukv_ref[...]).astype(bf16)
    kr = kr_ref[...].astype(bf16)
    scale_c = (QK_NOPE + QK_ROPE) ** -0.5
    for h in range(H_C):
        qn = qc_ref[:, h * QC_PAD:h * QC_PAD + QK_NOPE]
        qr = qc_ref[:, h * QC_PAD + QK_NOPE:h * QC_PAD + QK_NOPE + QK_ROPE]
        c0 = h * (QK_NOPE + V_C)
        s = (_dot_nt(qn, kv[:, c0:c0 + QK_NOPE]) + _dot_nt(qr, kr)) * scale_c
        off = W_QA + W_B + h * V_C
        o_scr[:, off:off + V_C] = _softmax_pv(s, kv[:, c0 + QK_NOPE:c0 + QK_NOPE + V_C])
    y = _dot(o_scr[...].astype(bf16), wout_ref[...])
    o_ref[...] = x_ref[...] + gate_ref[0] * y


def _ctx_attn(sink, proj, wukv, wout, x, gate):
    qa, ka, va, qb, kb, vb, qc, ckv, kr = proj
    row = lambda b: (b, 0)
    const = lambda b: (0, 0)
    in_specs = [pl.BlockSpec(memory_space=pltpu.SMEM)]
    in_specs += [pl.BlockSpec((SEQ, a.shape[1]), row) for a in proj]
    in_specs += [pl.BlockSpec((KV_LORA, H_C * (QK_NOPE + V_C)), const),
                 pl.BlockSpec((D_MODEL, D_MODEL), const),
                 pl.BlockSpec((SEQ, D_MODEL), row),
                 pl.BlockSpec((1, 1, D_MODEL), lambda b: (0, 0, 0))]
    return pl.pallas_call(
        _ctx_attn_kernel,
        grid=(BATCH,),
        in_specs=in_specs,
        out_specs=pl.BlockSpec((SEQ, D_MODEL), row),
        out_shape=jax.ShapeDtypeStruct((T_CTX, D_MODEL), f32),
        scratch_shapes=[pltpu.VMEM((SEQ, D_MODEL), f32)],
        compiler_params=_cparams("arbitrary"),
        name="ctx_attn",
    )(sink, qa, ka, va, qb, kb, vb, qc, ckv, kr, wukv, wout, x, gate)


def _lat_attn_kernel(sink_ref, qa_ref, qb_ref, qc_ref, ka_ref, va_ref, kb_ref, vb_ref, ckv_ref, kr_ref,
                     cak_ref, cav_ref, cbk_ref, cbv_ref, cckv_ref, ckr_ref, bias_ref,
                     wukv_ref, wout_ref, x_ref, gate_ref, o_ref, o_scr, kv_scr):
    qi = pl.program_id(1)
    nb = DEC_SEQ // BLOCK
    scale = HEAD_DIM ** -0.5

    @pl.when(qi == 0)
    def _():
        kv_scr[0:DEC_SEQ, :] = _dot(ckv_ref[...], wukv_ref[...]).astype(bf16)
        kv_scr[DEC_SEQ:DEC_SEQ + PAST_LEN, :] = _dot(cckv_ref[0, 0].astype(bf16), wukv_ref[...]).astype(bf16)

    def blk(ref, j):
        idx = jnp.clip(qi + j, 0, nb - 1)
        return ref[pl.ds(pl.multiple_of(idx * BLOCK, BLOCK), BLOCK), :]

    ka = jnp.concatenate([blk(ka_ref, -1), blk(ka_ref, 0), blk(ka_ref, 1), cak_ref[0, 0].astype(bf16)], axis=0)
    va = jnp.concatenate([blk(va_ref, -1), blk(va_ref, 0), blk(va_ref, 1), cav_ref[0, 0].astype(bf16)], axis=0)
    nk_a = 3 * BLOCK + PAST_LEN
    r = lax.broadcasted_iota(jnp.int32, (BLOCK, nk_a), 0)
    c = lax.broadcasted_iota(jnp.int32, (BLOCK, nk_a), 1)
    valid = (((c < BLOCK) & (c >= r) & (qi > 0))
             | ((c >= BLOCK) & (c < 2 * BLOCK))
             | ((c >= 2 * BLOCK) & (c < 3 * BLOCK) & (c - 2 * BLOCK <= r) & (qi < nb - 1))
             | (c >= 3 * BLOCK))
    for h in range(H_A):
        g = h // G_A
        q = qa_ref[:, h * HEAD_DIM:(h + 1) * HEAD_DIM]
        s = _dot_nt(q, ka[:, g * HEAD_DIM:(g + 1) * HEAD_DIM]) * scale
        s = jnp.where(valid, s, NEG)
        o_scr[:, h * HEAD_DIM:(h + 1) * HEAD_DIM] = _softmax_pv(s, va[:, g * HEAD_DIM:(g + 1) * HEAD_DIM],
                                                               sink_ref[h])

    cbk = cbk_ref[0, 0].astype(bf16)
    cbv = cbv_ref[0, 0].astype(bf16)
    rows_per_blk = BLOCK // GRID_W
    nloc = NA_ROWS * GRID_W
    for half in range(rows_per_blk):
        grow = qi * rows_per_blk + half
        start = jnp.clip(grow - NA_ROWS // 2, 0, ROWS - NA_ROWS)
        kloc = kb_ref[pl.ds(pl.multiple_of(start * GRID_W, GRID_W), nloc), :]
        vloc = vb_ref[pl.ds(pl.multiple_of(start * GRID_W, GRID_W), nloc), :]
        vcat = jnp.concatenate([vloc, cbv], axis=0)
        qrows = slice(half * GRID_W, (half + 1) * GRID_W)
        for h in range(H_B):
            sl = slice(h * HEAD_DIM, (h + 1) * HEAD_DIM)
            q = qb_ref[qrows, sl]
            s_loc = _dot_nt(q, kloc[:, sl]) * scale + bias_ref[half, h]
            s_ctx = _dot_nt(q, cbk[:, sl]) * scale
            s = jnp.concatenate([s_loc, s_ctx], axis=1)
            o_scr[qrows, W_QA + h * HEAD_DIM:W_QA + (h + 1) * HEAD_DIM] = _softmax_pv(s, vcat[:, sl])

    kr = jnp.concatenate([kr_ref[...], ckr_ref[0, 0].astype(bf16)], axis=0)
    scale_c = (QK_NOPE + QK_ROPE) ** -0.5
    for h in range(H_C):
        qn = qc_ref[:, h * QC_PAD:h * QC_PAD + QK_NOPE]
        qr = qc_ref[:, h * QC_PAD + QK_NOPE:h * QC_PAD + QK_NOPE + QK_ROPE]
        c0 = h * (QK_NOPE + V_C)
        s = (_dot_nt(qn, kv_scr[:, c0:c0 + QK_NOPE]) + _dot_nt(qr, kr)) * scale_c
        off = W_QA + W_B + h * V_C
        o_scr[:, off:off + V_C] = _softmax_pv(s, kv_scr[:, c0 + QK_NOPE:c0 + QK_NOPE + V_C])

    y = _dot(o_scr[...].astype(bf16), wout_ref[...])
    o_ref[...] = x_ref[...] + gate_ref[0] * y


def _lat_attn(layer, sink, proj, caches, bias_tab, wukv, wout, x, gate):
    qa, ka, va, qb, kb, vb, qc, ckv, kr = proj
    nb = DEC_SEQ // BLOCK
    qrow = lambda b, q: (b * nb + q, 0)
    xrow = lambda b, q: (T_CTX // BLOCK + b * nb + q, 0)
    brow = lambda b, q: (b, 0)
    const = lambda b, q: (0, 0)
    cidx = lambda b, q: (b, layer, 0, 0)
    in_specs = [pl.BlockSpec(memory_space=pltpu.SMEM)]
    in_specs += [pl.BlockSpec((BLOCK, a.shape[1]), qrow) for a in (qa, qb, qc)]
    in_specs += [pl.BlockSpec((DEC_SEQ, a.shape[1]), brow) for a in (ka, va, kb, vb, ckv, kr)]
    in_specs += [pl.BlockSpec((1, 1, PAST_LEN, a.shape[3]), cidx) for a in caches]
    in_specs += [pl.BlockSpec((BLOCK // GRID_W, H_B, GRID_W, NA_ROWS * GRID_W), lambda b, q: (q, 0, 0, 0)),
                 pl.BlockSpec((KV_LORA, H_C * (QK_NOPE + V_C)), const),
                 pl.BlockSpec((D_MODEL, D_MODEL), const),
                 pl.BlockSpec((BLOCK, D_MODEL), xrow),
                 pl.BlockSpec((1, 1, D_MODEL), lambda b, q: (1 + b, 0, 0))]
    return pl.pallas_call(
        _lat_attn_kernel,
        grid=(DEC_BATCH, nb),
        in_specs=in_specs,
        out_specs=pl.BlockSpec((BLOCK, D_MODEL), qrow),
        out_shape=jax.ShapeDtypeStruct((T_LAT, D_MODEL), f32),
        scratch_shapes=[pltpu.VMEM((BLOCK, D_MODEL), f32),
                        pltpu.VMEM((DEC_SEQ + PAST_LEN, H_C * (QK_NOPE + V_C)), bf16)],
        compiler_params=_cparams("arbitrary", "arbitrary"),
        name="lat_attn",
    )(sink, qa, qb, qc, ka, va, kb, vb, ckv, kr, *caches, bias_tab, wukv, wout, x, gate)


def _router_kernel(x_ref, g_ref, sh_ref, sc_ref, wr_ref, br_ref, h_ref, e_ref, gt_ref):
    h = _rms(x_ref[...], g_ref[...]) * (1.0 + sc_ref[0]) + sh_ref[0]
    h_ref[...] = h
    logits = jnp.dot(h, wr_ref[...], preferred_element_type=f32, precision=lax.Precision.HIGHEST) + br_ref[...]
    lane = lax.broadcasted_iota(jnp.int32, logits.shape, 1).astype(f32)
    l = jnp.where(lane < N_EXPERTS, logits, -jnp.inf)
    tops, idxs = [], []
    for _ in range(TOP_K):
        m = jnp.max(l, axis=-1, keepdims=True)
        idx = jnp.min(jnp.where(l == m, lane, float(LANE)), axis=-1, keepdims=True)
        tops.append(m)
        idxs.append(idx)
        l = jnp.where(lane == idx, -jnp.inf, l)
    ex = [jnp.exp(t - tops[0]) for t in tops]
    den = ex[0] + ex[1] + ex[2] + ex[3]
    e_out = jnp.zeros(logits.shape, f32)
    g_out = jnp.zeros(logits.shape, f32)
    for k in range(TOP_K):
        e_out = jnp.where(lane == k, idxs[k], e_out)
        g_out = jnp.where(lane == k, ex[k] / den, g_out)
    e_ref[...] = e_out.astype(jnp.int32)
    gt_ref[...] = g_out


def _group_of_tile(i):
    per_b = DEC_SEQ // TM_TOK
    n_ctx = T_CTX // TM_TOK
    return jnp.where(i < n_ctx, 0, 1 + (i - n_ctx) // per_b)


def _router(x, g, shift, scale, wr, br):
    tm = TM_TOK
    row = lambda i: (i, 0)
    const = lambda i: (0, 0)
    grp = lambda i: (_group_of_tile(i), 0, 0)
    return pl.pallas_call(
        _router_kernel,
        grid=(T_ALL // tm,),
        in_specs=[pl.BlockSpec((tm, D_MODEL), row),
                  pl.BlockSpec((1, D_MODEL), const),
                  pl.BlockSpec((1, 1, D_MODEL), grp),
                  pl.BlockSpec((1, 1, D_MODEL), grp),
                  pl.BlockSpec((D_MODEL, LANE), const),
                  pl.BlockSpec((1, LANE), const)],
        out_specs=[pl.BlockSpec((tm, D_MODEL), row), pl.BlockSpec((tm, LANE), row), pl.BlockSpec((tm, LANE), row)],
        out_shape=[jax.ShapeDtypeStruct((T_ALL, D_MODEL), f32),
                   jax.ShapeDtypeStruct((T_ALL, LANE), jnp.int32),
                   jax.ShapeDtypeStruct((T_ALL, LANE), f32)],
        compiler_params=_cparams("arbitrary"),
        name="router",
    )(x, g, shift, scale, wr, br)


def _moe_kernel(be_ref, nu_ref, tok_ref, dst_ref, h_hbm, wgu_ref, bgu_ref, wd_ref, bd_ref, y_hbm,
                xbuf, ybuf, wgu_bf, wd_bf, gsem, ssem):
    tm = TM_MOE
    i = pl.program_id(0)
    nb = pl.num_programs(0)
    n_used = nu_ref[0]
    slot = i % 2

    def gather_start(blk, s):
        def body(r, carry):
            t = tok_ref[blk * tm + r]
            pltpu.make_async_copy(h_hbm.at[pl.ds(t, 1)], xbuf.at[s, pl.ds(r, 1)], gsem.at[s]).start()
            return carry
        lax.fori_loop(0, tm, body, 0, unroll=8)

    def gather_wait(s):
        pltpu.make_async_copy(h_hbm.at[pl.ds(0, tm)], xbuf.at[s], gsem.at[s]).wait()

    def scatter_start(blk, s):
        def body(r, carry):
            d = dst_ref[blk * tm + r]
            pltpu.make_async_copy(ybuf.at[s, pl.ds(r, 1)], y_hbm.at[pl.ds(d, 1)], ssem.at[s]).start()
            return carry
        lax.fori_loop(0, tm, body, 0, unroll=8)

    def scatter_wait(s):
        pltpu.make_async_copy(ybuf.at[s], y_hbm.at[pl.ds(0, tm)], ssem.at[s]).wait()

    @pl.when(i == 0)
    def _():
        ybuf[...] = jnp.zeros_like(ybuf)
        for s in range(2):
            dummy = pltpu.make_async_copy(ybuf.at[s], y_hbm.at[pl.ds(N_ASSIGN + s * tm, tm)], ssem.at[s])
            dummy.start()
            dummy.wait()
        gather_start(0, 0)

    @pl.when(i < n_used)
    def _():
        gather_wait(slot)

    @pl.when(i + 1 < n_used)
    def _():
        gather_start(i + 1, 1 - slot)

    @pl.when(jnp.logical_and(i >= 2, i - 2 < n_used))
    def _():
        scatter_wait(slot)

    @pl.when(i < n_used)
    def _():
        first = jnp.logical_or(i == 0, be_ref[i] != be_ref[jnp.maximum(i - 1, 0)])

        @pl.when(first)
        def _():
            wgu_bf[...] = wgu_ref[0, 0].astype(bf16)
            wd_bf[...] = wd_ref[0, 0].astype(bf16)

        gu = _dot(xbuf[slot].astype(bf16), wgu_bf[...]) + bgu_ref[0, 0]
        x_glu = jnp.minimum(gu[:, :D_FF], SWIGLU_LIMIT)
        x_lin = jnp.clip(gu[:, D_FF:], -SWIGLU_LIMIT, SWIGLU_LIMIT)
        act = x_glu * jax.nn.sigmoid(SWIGLU_ALPHA * x_glu) * (x_lin + 1.0)
        ybuf[slot] = _dot(act.astype(bf16), wd_bf[...]) + bd_ref[0, 0]
        scatter_start(i, slot)

    @pl.when(i == nb - 1)
    def _():
        @pl.when(nb - 2 < n_used)
        def _():
            scatter_wait(1 - slot)

        @pl.when(nb - 1 < n_used)
        def _():
            scatter_wait(slot)


def _moe(layer, block_e, n_used, row_tok, row_dst, h, w_gu, b_gu, w_down, b_down):
    tm = TM_MOE
    ex4 = lambda i, be, nu, tok, dst: (layer, be[i], 0, 0)
    return pl.pallas_call(
        _moe_kernel,
        grid_spec=pltpu.PrefetchScalarGridSpec(
            num_scalar_prefetch=4,
            grid=(N_MOE_BLOCKS,),
            in_specs=[pl.BlockSpec(memory_space=pl.ANY),
                      pl.BlockSpec((1, 1, D_MODEL, 2 * D_FF), ex4),
                      pl.BlockSpec((1, 1, 1, 2 * D_FF), ex4),
                      pl.BlockSpec((1, 1, D_FF, D_MODEL), ex4),
                      pl.BlockSpec((1, 1, 1, D_MODEL), ex4)],
            out_specs=pl.BlockSpec(memory_space=pl.ANY),
            scratch_shapes=[pltpu.VMEM((2, tm, D_MODEL), f32), pltpu.VMEM((2, tm, D_MODEL), f32),
                            pltpu.VMEM((D_MODEL, 2 * D_FF), bf16), pltpu.VMEM((D_FF, D_MODEL), bf16),
                            pltpu.SemaphoreType.DMA((2,)), pltpu.SemaphoreType.DMA((2,))]),
        out_shape=jax.ShapeDtypeStruct((N_ASSIGN + 2 * tm, D_MODEL), f32),
        compiler_params=_cparams("arbitrary"),
        name="moe",
    )(block_e, n_used, row_tok, row_dst, h, w_gu, b_gu.reshape(DEPTH, N_EXPERTS, 1, 2 * D_FF), w_down,
      b_down.reshape(DEPTH, N_EXPERTS, 1, D_MODEL))


def _combine_kernel(final, x_ref, y0_ref, y1_ref, y2_ref, y3_ref, gt_ref, gate_ref, gf_ref, o_ref):
    gt = gt_ref[...]
    f = gt[:, 0:1] * y0_ref[...]
    for k, y_ref in ((1, y1_ref), (2, y2_ref), (3, y3_ref)):
        f = f + gt[:, k:k + 1] * y_ref[...]
    out = x_ref[...] + gate_ref[0] * f
    if final:
        out = _rms(out, gf_ref[...])
    o_ref[...] = out


def _combine(final, x, y, gates, gate, g_final):
    tm = TM_TOK
    nt = T_ALL // tm
    row = lambda i: (i, 0)
    const = lambda i: (0, 0)
    grp = lambda i: (_group_of_tile(i), 0, 0)
    ysel = [pl.BlockSpec((tm, D_MODEL), functools.partial(lambda k, i: (k * nt + i, 0), k)) for k in range(TOP_K)]
    return pl.pallas_call(
        functools.partial(_combine_kernel, final),
        grid=(nt,),
        in_specs=[pl.BlockSpec((tm, D_MODEL), row)] + ysel +
                 [pl.BlockSpec((tm, LANE), row),
                  pl.BlockSpec((1, 1, D_MODEL), grp),
                  pl.BlockSpec((1, D_MODEL), const)],
        out_specs=pl.BlockSpec((tm, D_MODEL), row),
        out_shape=jax.ShapeDtypeStruct((T_ALL, D_MODEL), f32),
        compiler_params=_cparams("arbitrary"),
        name="combine",
    )(x, y, y, y, y, gates, gate, g_final)


def _rope_head_tables(d):
    nf = d // 4
    half = d // 2
    t = np.arange(DEC_SEQ)
    inv = ROPE_BASE ** (-np.arange(nf, dtype=np.float32) / nf)
    i = np.arange(d)
    pos = np.where(i[None, :] < half, (t // GRID_W)[:, None], (t % GRID_W)[:, None]).astype(np.float32)
    ang = pos * inv[i % nf][None, :].astype(np.float32)
    first = (i % half) < nf
    cos = np.cos(ang)
    sin = np.where(first[None, :], -np.sin(ang), np.sin(ang))
    partner = np.where(first, i + nf, i - nf)
    return cos.astype(np.float32), sin.astype(np.float32), partner


def _rope_tables():
    cos64, sin64, _ = _rope_head_tables(HEAD_DIM)
    cos32, sin32, _ = _rope_head_tables(QK_ROPE)
    cosa = np.tile(cos64, (1, H_A))
    sina = np.tile(sin64, (1, H_A))
    cosq1 = np.concatenate([np.ones((DEC_SEQ, QK_NOPE), np.float32), cos32,
                            np.ones((DEC_SEQ, QC_PAD - QK_NOPE - QK_ROPE), np.float32)], axis=1)
    sinq1 = np.concatenate([np.zeros((DEC_SEQ, QK_NOPE), np.float32), sin32,
                            np.zeros((DEC_SEQ, QC_PAD - QK_NOPE - QK_ROPE), np.float32)], axis=1)
    cosq = np.tile(cosq1, (1, H_C))
    sinq = np.tile(sinq1, (1, H_C))
    return tuple(jnp.asarray(a) for a in (cosa, sina, cosq, sinq, cos32, sin32))


def _pad_cols(w, n):
    return jnp.pad(w, ((0, 0), (0, n - w.shape[1])))


def _layer_weights(w_in, w_uq):
    cuts = np.cumsum((W_QA, W_KA, W_VA, W_B, W_B, W_B, Q_LORA, KV_LORA, QK_ROPE))[:-1]
    qa, ka, va, qb, kb, vb, cq, ckv, kr = jnp.split(w_in, [int(c) for c in cuts], axis=1)
    _, _, p64 = _rope_head_tables(HEAD_DIM)
    _, _, p32 = _rope_head_tables(QK_ROPE)
    pa = np.concatenate([h * HEAD_DIM + p64 for h in range(H_A)])
    base = jnp.concatenate([qa, ka, va, _pad_cols(qb, 384), _pad_cols(kb, 384), _pad_cols(vb, 384), cq, ckv,
                            _pad_cols(kr, 128)], axis=1)
    w_ctx = base.astype(bf16)
    w_lat = jnp.concatenate([base, qa[:, pa], ka[:, pa[:W_KA]], _pad_cols(kr[:, p32], 128)], axis=1).astype(bf16)
    hq = QK_NOPE + QK_ROPE
    heads = [_pad_cols(w_uq[:, h * hq:(h + 1) * hq], QC_PAD) for h in range(H_C)]
    pq = np.concatenate([np.arange(QK_NOPE), QK_NOPE + p32])
    heads_p = [_pad_cols(w_uq[:, h * hq:(h + 1) * hq][:, pq], QC_PAD) for h in range(H_C)]
    wuq = jnp.concatenate(heads, axis=1).astype(bf16)
    wuq2 = jnp.concatenate(heads + heads_p, axis=1).astype(bf16)
    return w_ctx, w_lat, wuq, wuq2


def _bias_table(rpb):
    r = np.arange(ROWS)
    key_rows = np.clip(r - NA_ROWS // 2, 0, ROWS - NA_ROWS)[:, None] + np.arange(NA_ROWS)[None, :]
    col = np.arange(GRID_W)
    col_start = np.clip(col - NA_COLS // 2, 0, GRID_W - NA_COLS)
    col_ok = (col[None, :] >= col_start[:, None]) & (col[None, :] < col_start[:, None] + NA_COLS)
    dr = key_rows - r[:, None] + (NA_ROWS - 1)
    dc = np.clip(col[None, :] - col[:, None] + (NA_COLS - 1), 0, 2 * NA_COLS - 2)
    bias = rpb[:, dr][:, :, :, dc]
    bias = jnp.where(col_ok[None, None, None], bias, NEG)
    bias = jnp.transpose(bias, (1, 0, 3, 2, 4))
    return bias.reshape(ROWS, H_B, GRID_W, NA_ROWS * GRID_W).astype(f32)


def _routing(top_e):
    tm = TM_MOE
    flat_e = top_e.T.reshape(N_ASSIGN)
    order = jnp.argsort(flat_e).astype(jnp.int32)
    experts = jnp.arange(N_EXPERTS, dtype=jnp.int32)
    counts = jnp.sum((flat_e[:, None] == experts[None, :]).astype(jnp.int32), axis=0)
    nblk = (counts + tm - 1) // tm
    blk_end = jnp.cumsum(nblk)
    blk_start = blk_end - nblk
    grp_start = jnp.cumsum(counts) - counts
    blocks = jnp.arange(N_MOE_BLOCKS, dtype=jnp.int32)
    block_e = jnp.minimum(jnp.sum((blk_end[None, :] <= blocks[:, None]).astype(jnp.int32), axis=1), N_EXPERTS - 1)
    n_used = blk_end[-1].astype(jnp.int32).reshape(1)
    sel = (block_e[:, None] == experts[None, :]).astype(jnp.int32)
    b_first = jnp.sum(sel * blk_start[None, :], axis=1)
    b_count = jnp.sum(sel * counts[None, :], axis=1)
    b_grp = jnp.sum(sel * grp_start[None, :], axis=1)
    r = jnp.arange(tm, dtype=jnp.int32)[None, :]
    off = (blocks - b_first)[:, None] * tm + r
    valid = (off < b_count[:, None]) & (blocks[:, None] < n_used[0])
    asg = order[jnp.clip(b_grp[:, None] + off, 0, N_ASSIGN - 1)]
    row_tok = jnp.where(valid, asg % T_ALL, 0).reshape(-1)
    row_dst = jnp.where(valid, asg, N_ASSIGN + (blocks[:, None] % 2) * tm + r).reshape(-1)
    return block_e.astype(jnp.int32), n_used, row_tok.astype(jnp.int32), row_dst.astype(jnp.int32)


def kernel(x_prompt, x_sample, cache_a_k, cache_a_v, cache_b_k, cache_b_v, cache_c_kv, cache_c_kr, c, c_ctx, w_ada, b_ada, g_attn, g_ffn, w_in, sink_a, rpb_b, g_cq, g_ckv, w_uq, w_ukv, w_out, w_router, b_router, w_gu, b_gu, w_down, b_down, g_final):
    x = jnp.concatenate([x_prompt.reshape(T_CTX, D_MODEL), x_sample.reshape(T_LAT, D_MODEL)], axis=0)
    cvec = jnp.concatenate([c_ctx[None, :], c, jnp.zeros((8 - N_GROUPS, D_MODEL), f32)], axis=0)
    mods = _ada(cvec, w_ada, b_ada)[:, :N_GROUPS].reshape(DEPTH, N_GROUPS, 6, 1, D_MODEL)
    tabs = _rope_tables()
    caches = (cache_a_k.reshape(DEC_BATCH, DEPTH, PAST_LEN, W_KA), cache_a_v.reshape(DEC_BATCH, DEPTH, PAST_LEN, W_VA),
              cache_b_k.reshape(DEC_BATCH, DEPTH, PAST_LEN, W_B), cache_b_v.reshape(DEC_BATCH, DEPTH, PAST_LEN, W_B),
              cache_c_kv, cache_c_kr)
    new = [[] for _ in range(6)]
    for layer in range(DEPTH):
        m = [mods[layer, :, j] for j in range(6)]
        w_ctx, w_lat, wuq, wuq2 = _layer_weights(w_in[layer], w_uq[layer])
        wukv = w_ukv[layer].astype(bf16)
        wout = w_out[layer].astype(bf16)
        g1 = g_attn[layer][None, :]
        gcq = g_cq[layer][None, :]
        gckv = g_ckv[layer][None, :]
        sink = sink_a[layer]

        pc = _inproj_ctx(x, g1, m[0], m[1], w_ctx, gcq, gckv, wuq)
        for lst, a in zip(new, (pc[1], pc[2], pc[4], pc[5], pc[7], pc[8])):
            lst.append(a)
        x_ctx = _ctx_attn(sink, pc, wukv, wout, x, m[2])

        plat = _inproj_lat(x, g1, m[0], m[1], w_lat, gcq, gckv, wuq2, tabs)
        x_lat = _lat_attn(layer, sink, plat, caches, _bias_table(rpb_b[layer]), wukv, wout, x, m[2])
        x = jnp.concatenate([x_ctx, x_lat], axis=0)

        wr = _pad_cols(w_router[layer], LANE)
        br = _pad_cols(b_router[layer][None, :], LANE)
        h2, top_e, gates = _router(x, g_ffn[layer][None, :], m[3], m[4], wr, br)
        block_e, n_used, row_tok, row_dst = _routing(top_e[:, :TOP_K])
        y = _moe(layer, block_e, n_used, row_tok, row_dst, h2, w_gu, b_gu, w_down, b_down)
        x = _combine(layer == DEPTH - 1, x, y, gates, m[5], g_final[None, :])

    y_prompt = x[:T_CTX].reshape(BATCH, SEQ, D_MODEL)
    y_sample = x[T_CTX:].reshape(DEC_BATCH, DEC_SEQ, D_MODEL)
    shapes = ((KV_A, HEAD_DIM), (KV_A, HEAD_DIM), (H_B, HEAD_DIM), (H_B, HEAD_DIM), (KV_LORA,), (QK_ROPE,))
    outs = [jnp.stack([a.reshape((BATCH, SEQ) + s) for a in lst], axis=1) for lst, s in zip(new, shapes)]
    return (y_prompt, y_sample, *outs)
```

```python
import functools

import numpy as np
import jax
import jax.numpy as jnp
from jax import lax
from jax.experimental import pallas as pl
from jax.experimental.pallas import tpu as pltpu

D_MODEL = 1024
BATCH = 32
SEQ = 256
DEPTH = 2
DEC_BATCH = 2
DEC_SEQ = 1024
PAST_LEN = 512
GRID_W = 64
HEAD_DIM = 64
H_A = 6
KV_A = 2
G_A = H_A // KV_A
WINDOW = 128
BLOCK = 128
H_B = 5
NA_ROWS = 8
NA_COLS = 16
H_C = 5
Q_LORA = 384
KV_LORA = 256
QK_NOPE = 64
QK_ROPE = 32
V_C = 64
N_EXPERTS = 32
TOP_K = 4
D_FF = 1024
SWIGLU_ALPHA = 1.702
SWIGLU_LIMIT = 7.0
ROPE_BASE = 10000.0
EPS = 1e-6
NEG = -1e30

T_CTX = BATCH * SEQ
T_LAT = DEC_BATCH * DEC_SEQ
T_ALL = T_CTX + T_LAT
N_GROUPS = 1 + DEC_BATCH
LANE = 128
QC_PAD = 128
ROWS = DEC_SEQ // GRID_W

W_QA, W_KA, W_VA = H_A * HEAD_DIM, KV_A * HEAD_DIM, KV_A * HEAD_DIM
W_B = H_B * HEAD_DIM
OFF_QA = 0
OFF_KA = 384
OFF_VA = 512
OFF_QB = 640
OFF_KB = 1024
OFF_VB = 1408
OFF_CQ = 1792
OFF_CKV = 2176
OFF_KR = 2432
NW_CTX = 2560
OFF_QA_P = 2560
OFF_KA_P = 2944
OFF_KR_P = 3072
NW_LAT = 3200

TM_TOK = 256
TM_LAT_IN = 512
TM_MOE = 256
N_ASSIGN = T_ALL * TOP_K
N_MOE_BLOCKS = N_ASSIGN // TM_MOE + N_EXPERTS
VMEM_LIMIT = 56 * 1024 * 1024

f32 = jnp.float32
bf16 = jnp.bfloat16


def _cparams(*sem):
    return pltpu.CompilerParams(dimension_semantics=sem, vmem_limit_bytes=VMEM_LIMIT)


def _rms(xf, g):
    return xf * lax.rsqrt(jnp.mean(xf * xf, axis=-1, keepdims=True) + EPS) * g


def _dot(a, b):
    return jnp.dot(a, b, preferred_element_type=f32)


def _dot_nt(a, b):
    return lax.dot_general(a, b, (((1,), (1,)), ((), ())), preferred_element_type=f32)


def _softmax_pv(s, v, sink=None):
    m = jnp.max(s, axis=-1, keepdims=True)
    if sink is not None:
        m = jnp.maximum(m, sink)
    p = jnp.exp(s - m)
    l = jnp.sum(p, axis=-1, keepdims=True)
    if sink is not None:
        l = l + jnp.exp(sink - m)
    return _dot(p.astype(bf16), v) / l


def _ada_kernel(c_ref, w_ref, b_ref, o_ref):
    c = c_ref[...]
    s = c * jax.nn.sigmoid(c)
    o_ref[0] = jnp.dot(s, w_ref[0], preferred_element_type=f32, precision=lax.Precision.HIGHEST) + b_ref[0]


def _ada(cvec, w_ada, b_ada):
    tn = 1536
    return pl.pallas_call(
        _ada_kernel,
        grid=(DEPTH, 6 * D_MODEL // tn),
        in_specs=[pl.BlockSpec((8, D_MODEL), lambda l, j: (0, 0)),
                  pl.BlockSpec((1, D_MODEL, tn), lambda l, j: (l, 0, j)),
                  pl.BlockSpec((1, 1, tn), lambda l, j: (l, 0, j))],
        out_specs=pl.BlockSpec((1, 8, tn), lambda l, j: (l, 0, j)),
        out_shape=jax.ShapeDtypeStruct((DEPTH, 8, 6 * D_MODEL), f32),
        compiler_params=_cparams("arbitrary", "arbitrary"),
        name="ada",
    )(cvec, w_ada, b_ada.reshape(DEPTH, 1, 6 * D_MODEL))


def _inproj_ctx_kernel(x_ref, g_ref, sh_ref, sc_ref, w_ref, gcq_ref, gckv_ref, wuq_ref,
                       qa_ref, ka_ref, va_ref, qb_ref, kb_ref, vb_ref, qc_ref, ckv_ref, kr_ref):
    h = _rms(x_ref[...], g_ref[...]) * (1.0 + sc_ref[0]) + sh_ref[0]
    p = _dot(h.astype(bf16), w_ref[...])
    qa_ref[...] = p[:, OFF_QA:OFF_QA + W_QA].astype(bf16)
    ka_ref[...] = p[:, OFF_KA:OFF_KA + W_KA]
    va_ref[...] = p[:, OFF_VA:OFF_VA + W_VA]
    qb_ref[...] = p[:, OFF_QB:OFF_QB + W_B].astype(bf16)
    kb_ref[...] = p[:, OFF_KB:OFF_KB + W_B]
    vb_ref[...] = p[:, OFF_VB:OFF_VB + W_B]
    cqn = _rms(p[:, OFF_CQ:OFF_CQ + Q_LORA], gcq_ref[...])
    qc_ref[...] = _dot(cqn.astype(bf16), wuq_ref[...]).astype(bf16)
    ckv_ref[...] = _rms(p[:, OFF_CKV:OFF_CKV + KV_LORA], gckv_ref[...])
    kr_ref[...] = p[:, OFF_KR:OFF_KR + QK_ROPE]


def _inproj_ctx(x, g, shift, scale, w, gcq, gckv, wuq):
    tm = TM_TOK
    row = lambda i: (i, 0)
    const = lambda i: (0, 0)
    widths = (W_QA, W_KA, W_VA, W_B, W_B, W_B, H_C * QC_PAD, KV_LORA, QK_ROPE)
    dtypes = (bf16, f32, f32, bf16, f32, f32, bf16, f32, f32)
    return pl.pallas_call(
        _inproj_ctx_kernel,
        grid=(T_CTX // tm,),
        in_specs=[pl.BlockSpec((tm, D_MODEL), row),
                  pl.BlockSpec((1, D_MODEL), const),
                  pl.BlockSpec((1, 1, D_MODEL), lambda i: (0, 0, 0)),
                  pl.BlockSpec((1, 1, D_MODEL), lambda i: (0, 0, 0)),
                  pl.BlockSpec((D_MODEL, NW_CTX), const),
                  pl.BlockSpec((1, Q_LORA), const),
                  pl.BlockSpec((1, KV_LORA), const),
                  pl.BlockSpec((Q_LORA, H_C * QC_PAD), const)],
        out_specs=[pl.BlockSpec((tm, wd), row) for wd in widths],
        out_shape=[jax.ShapeDtypeStruct((T_CTX, wd), dt) for wd, dt in zip(widths, dtypes)],
        compiler_params=_cparams("arbitrary"),
        name="inproj_ctx",
    )(x, g, shift, scale, w, gcq, gckv, wuq)


def _inproj_lat_kernel(x_ref, g_ref, sh_ref, sc_ref, w_ref, gcq_ref, gckv_ref, wuq_ref,
                       cosa_ref, sina_ref, cosq_ref, sinq_ref, cosr_ref, sinr_ref,
                       qa_ref, ka_ref, va_ref, qb_ref, kb_ref, vb_ref, qc_ref, ckv_ref, kr_ref):
    h = _rms(x_ref[...], g_ref[...]) * (1.0 + sc_ref[0]) + sh_ref[0]
    p = _dot(h.astype(bf16), w_ref[...])
    cosa = cosa_ref[...]
    sina = sina_ref[...]
    qa = p[:, OFF_QA:OFF_QA + W_QA] * cosa + p[:, OFF_QA_P:OFF_QA_P + W_QA] * sina
    ka = p[:, OFF_KA:OFF_KA + W_KA] * cosa[:, :W_KA] + p[:, OFF_KA_P:OFF_KA_P + W_KA] * sina[:, :W_KA]
    kr = p[:, OFF_KR:OFF_KR + QK_ROPE] * cosr_ref[...] + p[:, OFF_KR_P:OFF_KR_P + QK_ROPE] * sinr_ref[...]
    qa_ref[...] = qa.astype(bf16)
    ka_ref[...] = ka.astype(bf16)
    va_ref[...] = p[:, OFF_VA:OFF_VA + W_VA].astype(bf16)
    qb_ref[...] = p[:, OFF_QB:OFF_QB + W_B].astype(bf16)
    kb_ref[...] = p[:, OFF_KB:OFF_KB + W_B].astype(bf16)
    vb_ref[...] = p[:, OFF_VB:OFF_VB + W_B].astype(bf16)
    cqn = _rms(p[:, OFF_CQ:OFF_CQ + Q_LORA], gcq_ref[...])
    q2 = _dot(cqn.astype(bf16), wuq_ref[...])
    nq = H_C * QC_PAD
    qc_ref[...] = (q2[:, :nq] * cosq_ref[...] + q2[:, nq:] * sinq_ref[...]).astype(bf16)
    ckv_ref[...] = _rms(p[:, OFF_CKV:OFF_CKV + KV_LORA], gckv_ref[...]).astype(bf16)
    kr_ref[...] = kr.astype(bf16)


def _inproj_lat(x, g, shift, scale, w, gcq, gckv, wuq2, tabs):
    tm = TM_LAT_IN
    per_b = DEC_SEQ // tm
    row0 = T_CTX // tm
    xrow = lambda i: (row0 + i, 0)
    row = lambda i: (i, 0)
    const = lambda i: (0, 0)
    grp = lambda i: (1 + i // per_b, 0, 0)
    pos = lambda i: (i % per_b, 0)
    cosa, sina, cosq, sinq, cosr, sinr = tabs
    widths = (W_QA, W_KA, W_VA, W_B, W_B, W_B, H_C * QC_PAD, KV_LORA, QK_ROPE)
    return pl.pallas_call(
        _inproj_lat_kernel,
        grid=(T_LAT // tm,),
        in_specs=[pl.BlockSpec((tm, D_MODEL), xrow),
                  pl.BlockSpec((1, D_MODEL), const),
                  pl.BlockSpec((1, 1, D_MODEL), grp),
                  pl.BlockSpec((1, 1, D_MODEL), grp),
                  pl.BlockSpec((D_MODEL, NW_LAT), const),
                  pl.BlockSpec((1, Q_LORA), const),
                  pl.BlockSpec((1, KV_LORA), const),
                  pl.BlockSpec((Q_LORA, 2 * H_C * QC_PAD), const),
                  pl.BlockSpec((tm, W_QA), pos), pl.BlockSpec((tm, W_QA), pos),
                  pl.BlockSpec((tm, H_C * QC_PAD), pos), pl.BlockSpec((tm, H_C * QC_PAD), pos),
                  pl.BlockSpec((tm, QK_ROPE), pos), pl.BlockSpec((tm, QK_ROPE), pos)],
        out_specs=[pl.BlockSpec((tm, wd), row) for wd in widths],
        out_shape=[jax.ShapeDtypeStruct((T_LAT, wd), bf16) for wd in widths],
        compiler_params=_cparams("arbitrary"),
        name="inproj_lat",
    )(x, g, shift, scale, w, gcq, gckv, wuq2, cosa, sina, cosq, sinq, cosr, sinr)


def _ctx_attn_kernel(sink_ref, qa_ref, ka_ref, va_ref, qb_ref, kb_ref, vb_ref, qc_ref, ckv_ref, kr_ref,
                     wukv_ref, wout_ref, x_ref, gate_ref, o_ref, o_scr):
    scale = HEAD_DIM ** -0.5
    ka = ka_ref[...].astype(bf16)
    va = va_ref[...].astype(bf16)
    for h in range(H_A):
        g = h // G_A
        q = qa_ref[:, h * HEAD_DIM:(h + 1) * HEAD_DIM]
        s = _dot_nt(q, ka[:, g * HEAD_DIM:(g + 1) * HEAD_DIM]) * scale
        o_scr[:, h * HEAD_DIM:(h + 1) * HEAD_DIM] = _softmax_pv(s, va[:, g * HEAD_DIM:(g + 1) * HEAD_DIM],
                                                               sink_ref[h])
    kb = kb_ref[...].astype(bf16)
    vb = vb_ref[...].astype(bf16)
    for h in range(H_B):
        sl = slice(h * HEAD_DIM, (h + 1) * HEAD_DIM)
        s = _dot_nt(qb_ref[:, sl], kb[:, sl]) * scale
        o_scr[:, W_QA + h * HEAD_DIM:W_QA + (h + 1) * HEAD_DIM] = _softmax_pv(s, vb[:, sl])
    kv = _dot(ckv_ref[...].astype(bf16), wukv_ref[...]).astype(bf16)
    kr = kr_ref[...].astype(bf16)
    scale_c = (QK_NOPE + QK_ROPE) ** -0.5
    for h in range(H_C):
        qn = qc_ref[:, h * QC_PAD:h * QC_PAD + QK_NOPE]
        qr = qc_ref[:, h * QC_PAD + QK_NOPE:h * QC_PAD + QK_NOPE + QK_ROPE]
        c0 = h * (QK_NOPE + V_C)
        s = (_dot_nt(qn, kv[:, c0:c0 + QK_NOPE]) + _dot_nt(qr, kr)) * scale_c
        off = W_QA + W_B + h * V_C
        o_scr[:, off:off + V_C] = _softmax_pv(s, kv[:, c0 + QK_NOPE:c0 + QK_NOPE + V_C])
    y = _dot(o_scr[...].astype(bf16), wout_ref[...])
    o_ref[...] = x_ref[...] + gate_ref[0] * y


def _ctx_attn(sink, proj, wukv, wout, x, gate):
    qa, ka, va, qb, kb, vb, qc, ckv, kr = proj
    row = lambda b: (b, 0)
    const = lambda b: (0, 0)
    in_specs = [pl.BlockSpec(memory_space=pltpu.SMEM)]
    in_specs += [pl.BlockSpec((SEQ, a.shape[1]), row) for a in proj]
    in_specs += [pl.BlockSpec((KV_LORA, H_C * (QK_NOPE + V_C)), const),
                 pl.BlockSpec((D_MODEL, D_MODEL), const),
                 pl.BlockSpec((SEQ, D_MODEL), row),
                 pl.BlockSpec((1, 1, D_MODEL), lambda b: (0, 0, 0))]
    return pl.pallas_call(
        _ctx_attn_kernel,
        grid=(BATCH,),
        in_specs=in_specs,
        out_specs=pl.BlockSpec((SEQ, D_MODEL), row),
        out_shape=jax.ShapeDtypeStruct((T_CTX, D_MODEL), f32),
        scratch_shapes=[pltpu.VMEM((SEQ, D_MODEL), f32)],
        compiler_params=_cparams("arbitrary"),
        name="ctx_attn",
    )(sink, qa, ka, va, qb, kb, vb, qc, ckv, kr, wukv, wout, x, gate)


def _lat_attn_kernel(sink_ref, qa_ref, qb_ref, qc_ref, ka_ref, va_ref, kb_ref, vb_ref, ckv_ref, kr_ref,
                     cak_ref, cav_ref, cbk_ref, cbv_ref, cckv_ref, ckr_ref, bias_ref,
                     wukv_ref, wout_ref, x_ref, gate_ref, o_ref, o_scr, kv_scr):
    qi = pl.program_id(1)
    nb = DEC_SEQ // BLOCK
    scale = HEAD_DIM ** -0.5

    @pl.when(qi == 0)
    def _():
        kv_scr[0:DEC_SEQ, :] = _dot(ckv_ref[...], wukv_ref[...]).astype(bf16)
        kv_scr[DEC_SEQ:DEC_SEQ + PAST_LEN, :] = _dot(cckv_ref[0, 0].astype(bf16), wukv_ref[...]).astype(bf16)

    def blk(ref, j):
        idx = jnp.clip(qi + j, 0, nb - 1)
        return ref[pl.ds(pl.multiple_of(idx * BLOCK, BLOCK), BLOCK), :]

    ka = jnp.concatenate([blk(ka_ref, -1), blk(ka_ref, 0), blk(ka_ref, 1), cak_ref[0, 0].astype(bf16)], axis=0)
    va = jnp.concatenate([blk(va_ref, -1), blk(va_ref, 0), blk(va_ref, 1), cav_ref[0, 0].astype(bf16)], axis=0)
    nk_a = 3 * BLOCK + PAST_LEN
    r = lax.broadcasted_iota(jnp.int32, (BLOCK, nk_a), 0)
    c = lax.broadcasted_iota(jnp.int32, (BLOCK, nk_a), 1)
    valid = (((c < BLOCK) & (c >= r) & (qi > 0))
             | ((c >= BLOCK) & (c < 2 * BLOCK))
             | ((c >= 2 * BLOCK) & (c < 3 * BLOCK) & (c - 2 * BLOCK <= r) & (qi < nb - 1))
             | (c >= 3 * BLOCK))
    for h in range(H_A):
        g = h // G_A
        q = qa_ref[:, h * HEAD_DIM:(h + 1) * HEAD_DIM]
        s = _dot_nt(q, ka[:, g * HEAD_DIM:(g + 1) * HEAD_DIM]) * scale
        s = jnp.where(valid, s, NEG)
        o_scr[:, h * HEAD_DIM:(h + 1) * HEAD_DIM] = _softmax_pv(s, va[:, g * HEAD_DIM:(g + 1) * HEAD_DIM],
                                                               sink_ref[h])

    cbk = cbk_ref[0, 0].astype(bf16)
    cbv = cbv_ref[0, 0].astype(bf16)
    rows_per_blk = BLOCK // GRID_W
    nloc = NA_ROWS * GRID_W
    for half in range(rows_per_blk):
        grow = qi * rows_per_blk + half
        start = jnp.clip(grow - NA_ROWS // 2, 0, ROWS - NA_ROWS)
        kloc = kb_ref[pl.ds(pl.multiple_of(start * GRID_W, GRID_W), nloc), :]
        vloc = vb_ref[pl.ds(pl.multiple_of(start * GRID_W, GRID_W), nloc), :]
        vcat = jnp.concatenate([vloc, cbv], axis=0)
        qrows = slice(half * GRID_W, (half + 1) * GRID_W)
        for h in range(H_B):
            sl = slice(h * HEAD_DIM, (h + 1) * HEAD_DIM)
            q = qb_ref[qrows, sl]
            s_loc = _dot_nt(q, kloc[:, sl]) * scale + bias_ref[half, h]
            s_ctx = _dot_nt(q, cbk[:, sl]) * scale
            s = jnp.concatenate([s_loc, s_ctx], axis=1)
            o_scr[qrows, W_QA + h * HEAD_DIM:W_QA + (h + 1) * HEAD_DIM] = _softmax_pv(s, vcat[:, sl])

    kr = jnp.concatenate([kr_ref[...], ckr_ref[0, 0].astype(bf16)], axis=0)
    scale_c = (QK_NOPE + QK_ROPE) ** -0.5
    for h in range(H_C):
        qn = qc_ref[:, h * QC_PAD:h * QC_PAD + QK_NOPE]
        qr = qc_ref[:, h * QC_PAD + QK_NOPE:h * QC_PAD + QK_NOPE + QK_ROPE]
        c0 = h * (QK_NOPE + V_C)
        s = (_dot_nt(qn, kv_scr[:, c0:c0 + QK_NOPE]) + _dot_nt(qr, kr)) * scale_c
        off = W_QA + W_B + h * V_C
        o_scr[:, off:off + V_C] = _softmax_pv(s, kv_scr[:, c0 + QK_NOPE:c0 + QK_NOPE + V_C])

    y = _dot(o_scr[...].astype(bf16), wout_ref[...])
    o_ref[...] = x_ref[...] + gate_ref[0] * y


def _lat_attn(layer, sink, proj, caches, bias_tab, wukv, wout, x, gate):
    qa, ka, va, qb, kb, vb, qc, ckv, kr = proj
    nb = DEC_SEQ // BLOCK
    qrow = lambda b, q: (b * nb + q, 0)
    xrow = lambda b, q: (T_CTX // BLOCK + b * nb + q, 0)
    brow = lambda b, q: (b, 0)
    const = lambda b, q: (0, 0)
    cidx = lambda b, q: (b, layer, 0, 0)
    in_specs = [pl.BlockSpec(memory_space=pltpu.SMEM)]
    in_specs += [pl.BlockSpec((BLOCK, a.shape[1]), qrow) for a in (qa, qb, qc)]
    in_specs += [pl.BlockSpec((DEC_SEQ, a.shape[1]), brow) for a in (ka, va, kb, vb, ckv, kr)]
    in_specs += [pl.BlockSpec((1, 1, PAST_LEN, a.shape[3]), cidx) for a in caches]
    in_specs += [pl.BlockSpec((BLOCK // GRID_W, H_B, GRID_W, NA_ROWS * GRID_W), lambda b, q: (q, 0, 0, 0)),
                 pl.BlockSpec((KV_LORA, H_C * (QK_NOPE + V_C)), const),
                 pl.BlockSpec((D_MODEL, D_MODEL), const),
                 pl.BlockSpec((BLOCK, D_MODEL), xrow),
                 pl.BlockSpec((1, 1, D_MODEL), lambda b, q: (1 + b, 0, 0))]
    return pl.pallas_call(
        _lat_attn_kernel,
        grid=(DEC_BATCH, nb),
        in_specs=in_specs,
        out_specs=pl.BlockSpec((BLOCK, D_MODEL), qrow),
        out_shape=jax.ShapeDtypeStruct((T_LAT, D_MODEL), f32),
        scratch_shapes=[pltpu.VMEM((BLOCK, D_MODEL), f32),
                        pltpu.VMEM((DEC_SEQ + PAST_LEN, H_C * (QK_NOPE + V_C)), bf16)],
        compiler_params=_cparams("arbitrary", "arbitrary"),
        name="lat_attn",
    )(sink, qa, qb, qc, ka, va, kb, vb, ckv, kr, *caches, bias_tab, wukv, wout, x, gate)


def _router_kernel(x_ref, g_ref, sh_ref, sc_ref, wr_ref, br_ref, h_ref, e_ref, gt_ref):
    h = _rms(x_ref[...], g_ref[...]) * (1.0 + sc_ref[0]) + sh_ref[0]
    h_ref[...] = h
    logits = jnp.dot(h, wr_ref[...], preferred_element_type=f32, precision=lax.Precision.HIGHEST) + br_ref[...]
    lane = lax.broadcasted_iota(jnp.int32, logits.shape, 1).astype(f32)
    l = jnp.where(lane < N_EXPERTS, logits, -jnp.inf)
    tops, idxs = [], []
    for _ in range(TOP_K):
        m = jnp.max(l, axis=-1, keepdims=True)
        idx = jnp.min(jnp.where(l == m, lane, float(LANE)), axis=-1, keepdims=True)
        tops.append(m)
        idxs.append(idx)
        l = jnp.where(lane == idx, -jnp.inf, l)
    ex = [jnp.exp(t - tops[0]) for t in tops]
    den = ex[0] + ex[1] + ex[2] + ex[3]
    e_out = jnp.zeros(logits.shape, f32)
    g_out = jnp.zeros(logits.shape, f32)
    for k in range(TOP_K):
        e_out = jnp.where(lane == k, idxs[k], e_out)
        g_out = jnp.where(lane == k, ex[k] / den, g_out)
    e_ref[...] = e_out.astype(jnp.int32)
    gt_ref[...] = g_out


def _group_of_tile(i):
    per_b = DEC_SEQ // TM_TOK
    n_ctx = T_CTX // TM_TOK
    return jnp.where(i < n_ctx, 0, 1 + (i - n_ctx) // per_b)


def _router(x, g, shift, scale, wr, br):
    tm = TM_TOK
    row = lambda i: (i, 0)
    const = lambda i: (0, 0)
    grp = lambda i: (_group_of_tile(i), 0, 0)
    return pl.pallas_call(
        _router_kernel,
        grid=(T_ALL // tm,),
        in_specs=[pl.BlockSpec((tm, D_MODEL), row),
                  pl.BlockSpec((1, D_MODEL), const),
                  pl.BlockSpec((1, 1, D_MODEL), grp),
                  pl.BlockSpec((1, 1, D_MODEL), grp),
                  pl.BlockSpec((D_MODEL, LANE), const),
                  pl.BlockSpec((1, LANE), const)],
        out_specs=[pl.BlockSpec((tm, D_MODEL), row), pl.BlockSpec((tm, LANE), row), pl.BlockSpec((tm, LANE), row)],
        out_shape=[jax.ShapeDtypeStruct((T_ALL, D_MODEL), f32),
                   jax.ShapeDtypeStruct((T_ALL, LANE), jnp.int32),
                   jax.ShapeDtypeStruct((T_ALL, LANE), f32)],
        compiler_params=_cparams("arbitrary"),
        name="router",
    )(x, g, shift, scale, wr, br)


def _moe_kernel(be_ref, nu_ref, tok_ref, dst_ref, h_hbm, wgu_ref, bgu_ref, wd_ref, bd_ref, y_hbm,
                x0, x1, y0, y1, wgu_bf, wd_bf, gsem, ssem):
    tm = TM_MOE
    i = pl.program_id(0)
    nb = pl.num_programs(0)
    used = i < nu_ref[0]
    xb, yb = (x0, x1), (y0, y1)

    def gather_desc(src_row, buf, r, s):
        return pltpu.make_async_copy(h_hbm.at[pl.ds(src_row, 1)], buf.at[pl.ds(r, 1)], gsem.at[s])

    def scatter_desc(buf, r, dst_row, s):
        return pltpu.make_async_copy(buf.at[pl.ds(r, 1)], y_hbm.at[pl.ds(dst_row, 1)], ssem.at[s])

    def gather_wait(s):
        pltpu.make_async_copy(h_hbm.at[pl.ds(0, tm)], xb[s], gsem.at[s]).wait()

    def scatter_wait(s):
        pltpu.make_async_copy(yb[s], y_hbm.at[pl.ds(0, tm)], ssem.at[s]).wait()

    def issue(blk_gather, s_gather, blk_scatter, s_scatter, unrolled):
        def one(r):
            scatter_desc(yb[s_scatter], r, dst_ref[(blk_scatter + 1) * tm + r], s_scatter).start()
            gather_desc(tok_ref[blk_gather * tm + r], xb[s_gather], r, s_gather).start()
        if unrolled:
            for r in range(tm):
                one(r)
        else:
            def body(r, carry):
                one(r)
                return carry
            lax.fori_loop(0, tm, body, 0, unroll=8)

    @pl.when(i == 0)
    def _():
        for s in range(2):
            yb[s][...] = jnp.zeros_like(yb[s])
            dummy = pltpu.make_async_copy(yb[s], y_hbm.at[pl.ds(N_ASSIGN + s * tm, tm)], ssem.at[s])
            dummy.start()
            dummy.wait()

        def body(r, carry):
            gather_desc(tok_ref[r], xb[0], r, 0).start()
            return carry
        lax.fori_loop(0, tm, body, 0, unroll=8)

    first = jnp.logical_or(i == 0, be_ref[i] != be_ref[jnp.maximum(i - 1, 0)])

    @pl.when(jnp.logical_and(used, first))
    def _():
        wgu_bf[...] = wgu_ref[0, 0].astype(bf16)
        wd_bf[...] = wd_ref[0, 0].astype(bf16)

    def step(par):
        cur, oth = par, 1 - par
        gather_wait(cur)

        @pl.when(i >= 1)
        def _():
            scatter_wait(cur)

        @pl.when(used)
        def _():
            issue(i + 1, oth, i - 1, oth, unrolled=True)
            gu = _dot(xb[cur][...].astype(bf16), wgu_bf[...]) + bgu_ref[0, 0]
            x_glu = jnp.minimum(gu[:, :D_FF], SWIGLU_LIMIT)
            x_lin = jnp.clip(gu[:, D_FF:], -SWIGLU_LIMIT, SWIGLU_LIMIT)
            act = x_glu * jax.nn.sigmoid(SWIGLU_ALPHA * x_glu) * (x_lin + 1.0)
            yb[cur][...] = _dot(act.astype(bf16), wd_bf[...]) + bd_ref[0, 0]

        @pl.when(jnp.logical_and(jnp.logical_not(used), i + 1 < nb))
        def _():
            issue(i + 1, oth, i - 1, oth, unrolled=False)

        @pl.when(i == nb - 1)
        def _():
            def body(r, carry):
                scatter_desc(yb[oth], r, dst_ref[i * tm + r], oth).start()
                return carry
            lax.fori_loop(0, tm, body, 0, unroll=8)
            scatter_wait(oth)

    @pl.when(i % 2 == 0)
    def _():
        step(0)

    @pl.when(i % 2 == 1)
    def _():
        step(1)


def _moe(layer, block_e, n_used, row_tok, row_dst, h, w_gu, b_gu, w_down, b_down):
    tm = TM_MOE
    ex4 = lambda i, be, nu, tok, dst: (layer, be[i], 0, 0)
    return pl.pallas_call(
        _moe_kernel,
        grid_spec=pltpu.PrefetchScalarGridSpec(
            num_scalar_prefetch=4,
            grid=(N_MOE_BLOCKS,),
            in_specs=[pl.BlockSpec(memory_space=pl.ANY),
                      pl.BlockSpec((1, 1, D_MODEL, 2 * D_FF), ex4),
                      pl.BlockSpec((1, 1, 1, 2 * D_FF), ex4),
                      pl.BlockSpec((1, 1, D_FF, D_MODEL), ex4),
                      pl.BlockSpec((1, 1, 1, D_MODEL), ex4)],
            out_specs=pl.BlockSpec(memory_space=pl.ANY),
            scratch_shapes=[pltpu.VMEM((tm, D_MODEL), f32), pltpu.VMEM((tm, D_MODEL), f32),
                            pltpu.VMEM((tm, D_MODEL), f32), pltpu.VMEM((tm, D_MODEL), f32),
                            pltpu.VMEM((D_MODEL, 2 * D_FF), bf16), pltpu.VMEM((D_FF, D_MODEL), bf16),
                            pltpu.SemaphoreType.DMA((2,)), pltpu.SemaphoreType.DMA((2,))]),
        out_shape=jax.ShapeDtypeStruct((N_ASSIGN + 2 * tm, D_MODEL), f32),
        compiler_params=_cparams("arbitrary"),
        name="moe",
    )(block_e, n_used, row_tok, row_dst, h, w_gu, b_gu.reshape(DEPTH, N_EXPERTS, 1, 2 * D_FF), w_down,
      b_down.reshape(DEPTH, N_EXPERTS, 1, D_MODEL))


def _combine_kernel(final, x_ref, y0_ref, y1_ref, y2_ref, y3_ref, gt_ref, gate_ref, gf_ref, o_ref):
    gt = gt_ref[...]
    f = gt[:, 0:1] * y0_ref[...]
    for k, y_ref in ((1, y1_ref), (2, y2_ref), (3, y3_ref)):
        f = f + gt[:, k:k + 1] * y_ref[...]
    out = x_ref[...] + gate_ref[0] * f
    if final:
        out = _rms(out, gf_ref[...])
    o_ref[...] = out


def _combine(final, x, y, gates, gate, g_final):
    tm = TM_TOK
    nt = T_ALL // tm
    row = lambda i: (i, 0)
    const = lambda i: (0, 0)
    grp = lambda i: (_group_of_tile(i), 0, 0)
    ysel = [pl.BlockSpec((tm, D_MODEL), functools.partial(lambda k, i: (k * nt + i, 0), k)) for k in range(TOP_K)]
    return pl.pallas_call(
        functools.partial(_combine_kernel, final),
        grid=(nt,),
        in_specs=[pl.BlockSpec((tm, D_MODEL), row)] + ysel +
                 [pl.BlockSpec((tm, LANE), row),
                  pl.BlockSpec((1, 1, D_MODEL), grp),
                  pl.BlockSpec((1, D_MODEL), const)],
        out_specs=pl.BlockSpec((tm, D_MODEL), row),
        out_shape=jax.ShapeDtypeStruct((T_ALL, D_MODEL), f32),
        compiler_params=_cparams("arbitrary"),
        name="combine",
    )(x, y, y, y, y, gates, gate, g_final)


def _rope_head_tables(d):
    nf = d // 4
    half = d // 2
    t = np.arange(DEC_SEQ)
    inv = ROPE_BASE ** (-np.arange(nf, dtype=np.float32) / nf)
    i = np.arange(d)
    pos = np.where(i[None, :] < half, (t // GRID_W)[:, None], (t % GRID_W)[:, None]).astype(np.float32)
    ang = pos * inv[i % nf][None, :].astype(np.float32)
    first = (i % half) < nf
    cos = np.cos(ang)
    sin = np.where(first[None, :], -np.sin(ang), np.sin(ang))
    partner = np.where(first, i + nf, i - nf)
    return cos.astype(np.float32), sin.astype(np.float32), partner


def _rope_tables():
    cos64, sin64, _ = _rope_head_tables(HEAD_DIM)
    cos32, sin32, _ = _rope_head_tables(QK_ROPE)
    cosa = np.tile(cos64, (1, H_A))
    sina = np.tile(sin64, (1, H_A))
    cosq1 = np.concatenate([np.ones((DEC_SEQ, QK_NOPE), np.float32), cos32,
                            np.ones((DEC_SEQ, QC_PAD - QK_NOPE - QK_ROPE), np.float32)], axis=1)
    sinq1 = np.concatenate([np.zeros((DEC_SEQ, QK_NOPE), np.float32), sin32,
                            np.zeros((DEC_SEQ, QC_PAD - QK_NOPE - QK_ROPE), np.float32)], axis=1)
    cosq = np.tile(cosq1, (1, H_C))
    sinq = np.tile(sinq1, (1, H_C))
    return tuple(jnp.asarray(a) for a in (cosa, sina, cosq, sinq, cos32, sin32))


def _pad_cols(w, n):
    return jnp.pad(w, ((0, 0), (0, n - w.shape[1])))


def _layer_weights(w_in, w_uq):
    cuts = np.cumsum((W_QA, W_KA, W_VA, W_B, W_B, W_B, Q_LORA, KV_LORA, QK_ROPE))[:-1]
    qa, ka, va, qb, kb, vb, cq, ckv, kr = jnp.split(w_in, [int(c) for c in cuts], axis=1)
    _, _, p64 = _rope_head_tables(HEAD_DIM)
    _, _, p32 = _rope_head_tables(QK_ROPE)
    pa = np.concatenate([h * HEAD_DIM + p64 for h in range(H_A)])
    base = jnp.concatenate([qa, ka, va, _pad_cols(qb, 384), _pad_cols(kb, 384), _pad_cols(vb, 384), cq, ckv,
                            _pad_cols(kr, 128)], axis=1)
    w_ctx = base.astype(bf16)
    w_lat = jnp.concatenate([base, qa[:, pa], ka[:, pa[:W_KA]], _pad_cols(kr[:, p32], 128)], axis=1).astype(bf16)
    hq = QK_NOPE + QK_ROPE
    heads = [_pad_cols(w_uq[:, h * hq:(h + 1) * hq], QC_PAD) for h in range(H_C)]
    pq = np.concatenate([np.arange(QK_NOPE), QK_NOPE + p32])
    heads_p = [_pad_cols(w_uq[:, h * hq:(h + 1) * hq][:, pq], QC_PAD) for h in range(H_C)]
    wuq = jnp.concatenate(heads, axis=1).astype(bf16)
    wuq2 = jnp.concatenate(heads + heads_p, axis=1).astype(bf16)
    return w_ctx, w_lat, wuq, wuq2


def _bias_table(rpb):
    r = np.arange(ROWS)
    key_rows = np.clip(r - NA_ROWS // 2, 0, ROWS - NA_ROWS)[:, None] + np.arange(NA_ROWS)[None, :]
    col = np.arange(GRID_W)
    col_start = np.clip(col - NA_COLS // 2, 0, GRID_W - NA_COLS)
    col_ok = (col[None, :] >= col_start[:, None]) & (col[None, :] < col_start[:, None] + NA_COLS)
    dr = key_rows - r[:, None] + (NA_ROWS - 1)
    dc = np.clip(col[None, :] - col[:, None] + (NA_COLS - 1), 0, 2 * NA_COLS - 2)
    bias = rpb[:, dr][:, :, :, dc]
    bias = jnp.where(col_ok[None, None, None], bias, NEG)
    bias = jnp.transpose(bias, (1, 0, 3, 2, 4))
    return bias.reshape(ROWS, H_B, GRID_W, NA_ROWS * GRID_W).astype(f32)


def _routing(top_e):
    tm = TM_MOE
    flat_e = top_e.T.reshape(N_ASSIGN)
    order = jnp.argsort(flat_e).astype(jnp.int32)
    experts = jnp.arange(N_EXPERTS, dtype=jnp.int32)
    counts = jnp.sum((flat_e[:, None] == experts[None, :]).astype(jnp.int32), axis=0)
    nblk = (counts + tm - 1) // tm
    blk_end = jnp.cumsum(nblk)
    blk_start = blk_end - nblk
    grp_start = jnp.cumsum(counts) - counts
    blocks = jnp.arange(N_MOE_BLOCKS, dtype=jnp.int32)
    block_e = jnp.minimum(jnp.sum((blk_end[None, :] <= blocks[:, None]).astype(jnp.int32), axis=1), N_EXPERTS - 1)
    n_used = blk_end[-1].astype(jnp.int32).reshape(1)
    sel = (block_e[:, None] == experts[None, :]).astype(jnp.int32)
    b_first = jnp.sum(sel * blk_start[None, :], axis=1)
    b_count = jnp.sum(sel * counts[None, :], axis=1)
    b_grp = jnp.sum(sel * grp_start[None, :], axis=1)
    r = jnp.arange(tm, dtype=jnp.int32)[None, :]
    off = (blocks - b_first)[:, None] * tm + r
    valid = (off < b_count[:, None]) & (blocks[:, None] < n_used[0])
    asg = order[jnp.clip(b_grp[:, None] + off, 0, N_ASSIGN - 1)]
    row_tok = jnp.where(valid, asg % T_ALL, 0).reshape(-1)
    row_dst = jnp.where(valid, asg, N_ASSIGN + (blocks[:, None] % 2) * tm + r)
    row_dst = jnp.concatenate([N_ASSIGN + tm + r, row_dst], axis=0).reshape(-1)
    return block_e.astype(jnp.int32), n_used, row_tok.astype(jnp.int32), row_dst.astype(jnp.int32)


def kernel(x_prompt, x_sample, cache_a_k, cache_a_v, cache_b_k, cache_b_v, cache_c_kv, cache_c_kr, c, c_ctx, w_ada, b_ada, g_attn, g_ffn, w_in, sink_a, rpb_b, g_cq, g_ckv, w_uq, w_ukv, w_out, w_router, b_router, w_gu, b_gu, w_down, b_down, g_final):
    x = jnp.concatenate([x_prompt.reshape(T_CTX, D_MODEL), x_sample.reshape(T_LAT, D_MODEL)], axis=0)
    cvec = jnp.concatenate([c_ctx[None, :], c, jnp.zeros((8 - N_GROUPS, D_MODEL), f32)], axis=0)
    mods = _ada(cvec, w_ada, b_ada)[:, :N_GROUPS].reshape(DEPTH, N_GROUPS, 6, 1, D_MODEL)
    tabs = _rope_tables()
    caches = (cache_a_k.reshape(DEC_BATCH, DEPTH, PAST_LEN, W_KA), cache_a_v.reshape(DEC_BATCH, DEPTH, PAST_LEN, W_VA),
              cache_b_k.reshape(DEC_BATCH, DEPTH, PAST_LEN, W_B), cache_b_v.reshape(DEC_BATCH, DEPTH, PAST_LEN, W_B),
              cache_c_kv, cache_c_kr)
    new = [[] for _ in range(6)]
    for layer in range(DEPTH):
        m = [mods[layer, :, j] for j in range(6)]
        w_ctx, w_lat, wuq, wuq2 = _layer_weights(w_in[layer], w_uq[layer])
        wukv = w_ukv[layer].astype(bf16)
        wout = w_out[layer].astype(bf16)
        g1 = g_attn[layer][None, :]
        gcq = g_cq[layer][None, :]
        gckv = g_ckv[layer][None, :]
        sink = sink_a[layer]

        pc = _inproj_ctx(x, g1, m[0], m[1], w_ctx, gcq, gckv, wuq)
        for lst, a in zip(new, (pc[1], pc[2], pc[4], pc[5], pc[7], pc[8])):
            lst.append(a)
        x_ctx = _ctx_attn(sink, pc, wukv, wout, x, m[2])

        plat = _inproj_lat(x, g1, m[0], m[1], w_lat, gcq, gckv, wuq2, tabs)
        x_lat = _lat_attn(layer, sink, plat, caches, _bias_table(rpb_b[layer]), wukv, wout, x, m[2])
        x = jnp.concatenate([x_ctx, x_lat], axis=0)

        wr = _pad_cols(w_router[layer], LANE)
        br = _pad_cols(b_router[layer][None, :], LANE)
        h2, top_e, gates = _router(x, g_ffn[layer][None, :], m[3], m[4], wr, br)
        block_e, n_used, row_tok, row_dst = _routing(top_e[:, :TOP_K])
        y = _moe(layer, block_e, n_used, row_tok, row_dst, h2, w_gu, b_gu, w_down, b_down)
        x = _combine(layer == DEPTH - 1, x, y, gates, m[5], g_final[None, :])

    y_prompt = x[:T_CTX].reshape(BATCH, SEQ, D_MODEL)
    y_sample = x[T_CTX:].reshape(DEC_BATCH, DEC_SEQ, D_MODEL)
    shapes = ((KV_A, HEAD_DIM), (KV_A, HEAD_DIM), (H_B, HEAD_DIM), (H_B, HEAD_DIM), (KV_LORA,), (QK_ROPE,))
    outs = [jnp.stack([a.reshape((BATCH, SEQ) + s) for a in lst], axis=1) for lst, s in zip(new, shapes)]
    return (y_prompt, y_sample, *outs)
```

```python
import functools

import numpy as np
import jax
import jax.numpy as jnp
from jax import lax
from jax.experimental import pallas as pl
from jax.experimental.pallas import tpu as pltpu

D_MODEL = 1024
BATCH = 32
SEQ = 256
DEPTH = 2
DEC_BATCH = 2
DEC_SEQ = 1024
PAST_LEN = 512
GRID_W = 64
HEAD_DIM = 64
H_A = 6
KV_A = 2
G_A = H_A // KV_A
WINDOW = 128
BLOCK = 128
H_B = 5
NA_ROWS = 8
NA_COLS = 16
H_C = 5
Q_LORA = 384
KV_LORA = 256
QK_NOPE = 64
QK_ROPE = 32
V_C = 64
N_EXPERTS = 32
TOP_K = 4
D_FF = 1024
SWIGLU_ALPHA = 1.702
SWIGLU_LIMIT = 7.0
ROPE_BASE = 10000.0
EPS = 1e-6
NEG = -1e30

T_CTX = BATCH * SEQ
T_LAT = DEC_BATCH * DEC_SEQ
T_ALL = T_CTX + T_LAT
N_GROUPS = 1 + DEC_BATCH
LANE = 128
QC_PAD = 128
ROWS = DEC_SEQ // GRID_W

W_QA, W_KA, W_VA = H_A * HEAD_DIM, KV_A * HEAD_DIM, KV_A * HEAD_DIM
W_B = H_B * HEAD_DIM
OFF_QA = 0
OFF_KA = 384
OFF_VA = 512
OFF_QB = 640
OFF_KB = 1024
OFF_VB = 1408
OFF_CQ = 1792
OFF_CKV = 2176
OFF_KR = 2432
NW_CTX = 2560
OFF_QA_P = 2560
OFF_KA_P = 2944
OFF_KR_P = 3072
NW_LAT = 3200

TM_TOK = 256
TM_LAT_IN = 512
TM_MOE = 256
N_ASSIGN = T_ALL * TOP_K
N_MOE_BLOCKS = N_ASSIGN // TM_MOE + N_EXPERTS
VMEM_LIMIT = 56 * 1024 * 1024

f32 = jnp.float32
bf16 = jnp.bfloat16


def _cparams(*sem):
    return pltpu.CompilerParams(dimension_semantics=sem, vmem_limit_bytes=VMEM_LIMIT)


def _rms(xf, g):
    return xf * lax.rsqrt(jnp.mean(xf * xf, axis=-1, keepdims=True) + EPS) * g


def _dot(a, b):
    return jnp.dot(a, b, preferred_element_type=f32)


def _dot_nt(a, b):
    return lax.dot_general(a, b, (((1,), (1,)), ((), ())), preferred_element_type=f32)


ROW_TILE = D_MODEL // LANE


def _store_row_tiles(ref, val):
    n = val.shape[0]
    for c in range(ROW_TILE):
        ref[pl.ds(c, n, stride=ROW_TILE), :] = val[:, c * LANE:(c + 1) * LANE]


def _load_row_tiles(ref):
    n = ref.shape[0] // ROW_TILE
    return jnp.concatenate([ref[pl.ds(c, n, stride=ROW_TILE), :] for c in range(ROW_TILE)], axis=1)


def _softmax_pv(s, v, sink=None):
    m = jnp.max(s, axis=-1, keepdims=True)
    if sink is not None:
        m = jnp.maximum(m, sink)
    p = jnp.exp(s - m)
    l = jnp.sum(p, axis=-1, keepdims=True)
    if sink is not None:
        l = l + jnp.exp(sink - m)
    return _dot(p.astype(bf16), v) / l


def _ada_kernel(c_ref, w_ref, b_ref, o_ref):
    c = c_ref[...]
    s = c * jax.nn.sigmoid(c)
    o_ref[0] = jnp.dot(s, w_ref[0], preferred_element_type=f32, precision=lax.Precision.HIGHEST) + b_ref[0]


def _ada(cvec, w_ada, b_ada):
    tn = 1536
    return pl.pallas_call(
        _ada_kernel,
        grid=(DEPTH, 6 * D_MODEL // tn),
        in_specs=[pl.BlockSpec((8, D_MODEL), lambda l, j: (0, 0)),
                  pl.BlockSpec((1, D_MODEL, tn), lambda l, j: (l, 0, j)),
                  pl.BlockSpec((1, 1, tn), lambda l, j: (l, 0, j))],
        out_specs=pl.BlockSpec((1, 8, tn), lambda l, j: (l, 0, j)),
        out_shape=jax.ShapeDtypeStruct((DEPTH, 8, 6 * D_MODEL), f32),
        compiler_params=_cparams("arbitrary", "arbitrary"),
        name="ada",
    )(cvec, w_ada, b_ada.reshape(DEPTH, 1, 6 * D_MODEL))


def _inproj_ctx_kernel(x_ref, g_ref, sh_ref, sc_ref, w_ref, gcq_ref, gckv_ref, wuq_ref,
                       qa_ref, ka_ref, va_ref, qb_ref, kb_ref, vb_ref, qc_ref, ckv_ref, kr_ref):
    h = _rms(x_ref[...], g_ref[...]) * (1.0 + sc_ref[0]) + sh_ref[0]
    p = _dot(h.astype(bf16), w_ref[...])
    qa_ref[...] = p[:, OFF_QA:OFF_QA + W_QA].astype(bf16)
    ka_ref[...] = p[:, OFF_KA:OFF_KA + W_KA]
    va_ref[...] = p[:, OFF_VA:OFF_VA + W_VA]
    qb_ref[...] = p[:, OFF_QB:OFF_QB + W_B].astype(bf16)
    kb_ref[...] = p[:, OFF_KB:OFF_KB + W_B]
    vb_ref[...] = p[:, OFF_VB:OFF_VB + W_B]
    cqn = _rms(p[:, OFF_CQ:OFF_CQ + Q_LORA], gcq_ref[...])
    qc_ref[...] = _dot(cqn.astype(bf16), wuq_ref[...]).astype(bf16)
    ckv_ref[...] = _rms(p[:, OFF_CKV:OFF_CKV + KV_LORA], gckv_ref[...])
    kr_ref[...] = p[:, OFF_KR:OFF_KR + QK_ROPE]


def _inproj_ctx(x, g, shift, scale, w, gcq, gckv, wuq):
    tm = TM_TOK
    row = lambda i: (i, 0)
    const = lambda i: (0, 0)
    widths = (W_QA, W_KA, W_VA, W_B, W_B, W_B, H_C * QC_PAD, KV_LORA, QK_ROPE)
    dtypes = (bf16, f32, f32, bf16, f32, f32, bf16, f32, f32)
    return pl.pallas_call(
        _inproj_ctx_kernel,
        grid=(T_CTX // tm,),
        in_specs=[pl.BlockSpec((tm, D_MODEL), row),
                  pl.BlockSpec((1, D_MODEL), const),
                  pl.BlockSpec((1, 1, D_MODEL), lambda i: (0, 0, 0)),
                  pl.BlockSpec((1, 1, D_MODEL), lambda i: (0, 0, 0)),
                  pl.BlockSpec((D_MODEL, NW_CTX), const),
                  pl.BlockSpec((1, Q_LORA), const),
                  pl.BlockSpec((1, KV_LORA), const),
                  pl.BlockSpec((Q_LORA, H_C * QC_PAD), const)],
        out_specs=[pl.BlockSpec((tm, wd), row) for wd in widths],
        out_shape=[jax.ShapeDtypeStruct((T_CTX, wd), dt) for wd, dt in zip(widths, dtypes)],
        compiler_params=_cparams("arbitrary"),
        name="inproj_ctx",
    )(x, g, shift, scale, w, gcq, gckv, wuq)


def _inproj_lat_kernel(x_ref, g_ref, sh_ref, sc_ref, w_ref, gcq_ref, gckv_ref, wuq_ref,
                       cosa_ref, sina_ref, cosq_ref, sinq_ref, cosr_ref, sinr_ref,
                       qa_ref, ka_ref, va_ref, qb_ref, kb_ref, vb_ref, qc_ref, ckv_ref, kr_ref):
    h = _rms(x_ref[...], g_ref[...]) * (1.0 + sc_ref[0]) + sh_ref[0]
    p = _dot(h.astype(bf16), w_ref[...])
    cosa = cosa_ref[...]
    sina = sina_ref[...]
    qa = p[:, OFF_QA:OFF_QA + W_QA] * cosa + p[:, OFF_QA_P:OFF_QA_P + W_QA] * sina
    ka = p[:, OFF_KA:OFF_KA + W_KA] * cosa[:, :W_KA] + p[:, OFF_KA_P:OFF_KA_P + W_KA] * sina[:, :W_KA]
    kr = p[:, OFF_KR:OFF_KR + QK_ROPE] * cosr_ref[...] + p[:, OFF_KR_P:OFF_KR_P + QK_ROPE] * sinr_ref[...]
    qa_ref[...] = qa.astype(bf16)
    ka_ref[...] = ka.astype(bf16)
    va_ref[...] = p[:, OFF_VA:OFF_VA + W_VA].astype(bf16)
    qb_ref[...] = p[:, OFF_QB:OFF_QB + W_B].astype(bf16)
    kb_ref[...] = p[:, OFF_KB:OFF_KB + W_B].astype(bf16)
    vb_ref[...] = p[:, OFF_VB:OFF_VB + W_B].astype(bf16)
    cqn = _rms(p[:, OFF_CQ:OFF_CQ + Q_LORA], gcq_ref[...])
    q2 = _dot(cqn.astype(bf16), wuq_ref[...])
    nq = H_C * QC_PAD
    qc_ref[...] = (q2[:, :nq] * cosq_ref[...] + q2[:, nq:] * sinq_ref[...]).astype(bf16)
    ckv_ref[...] = _rms(p[:, OFF_CKV:OFF_CKV + KV_LORA], gckv_ref[...]).astype(bf16)
    kr_ref[...] = kr.astype(bf16)


def _inproj_lat(x, g, shift, scale, w, gcq, gckv, wuq2, tabs):
    tm = TM_LAT_IN
    per_b = DEC_SEQ // tm
    row0 = T_CTX // tm
    xrow = lambda i: (row0 + i, 0)
    row = lambda i: (i, 0)
    const = lambda i: (0, 0)
    grp = lambda i: (1 + i // per_b, 0, 0)
    pos = lambda i: (i % per_b, 0)
    cosa, sina, cosq, sinq, cosr, sinr = tabs
    widths = (W_QA, W_KA, W_VA, W_B, W_B, W_B, H_C * QC_PAD, KV_LORA, QK_ROPE)
    return pl.pallas_call(
        _inproj_lat_kernel,
        grid=(T_LAT // tm,),
        in_specs=[pl.BlockSpec((tm, D_MODEL), xrow),
                  pl.BlockSpec((1, D_MODEL), const),
                  pl.BlockSpec((1, 1, D_MODEL), grp),
                  pl.BlockSpec((1, 1, D_MODEL), grp),
                  pl.BlockSpec((D_MODEL, NW_LAT), const),
                  pl.BlockSpec((1, Q_LORA), const),
                  pl.BlockSpec((1, KV_LORA), const),
                  pl.BlockSpec((Q_LORA, 2 * H_C * QC_PAD), const),
                  pl.BlockSpec((tm, W_QA), pos), pl.BlockSpec((tm, W_QA), pos),
                  pl.BlockSpec((tm, H_C * QC_PAD), pos), pl.BlockSpec((tm, H_C * QC_PAD), pos),
                  pl.BlockSpec((tm, QK_ROPE), pos), pl.BlockSpec((tm, QK_ROPE), pos)],
        out_specs=[pl.BlockSpec((tm, wd), row) for wd in widths],
        out_shape=[jax.ShapeDtypeStruct((T_LAT, wd), bf16) for wd in widths],
        compiler_params=_cparams("arbitrary"),
        name="inproj_lat",
    )(x, g, shift, scale, w, gcq, gckv, wuq2, cosa, sina, cosq, sinq, cosr, sinr)


def _ctx_attn_kernel(sink_ref, qa_ref, ka_ref, va_ref, qb_ref, kb_ref, vb_ref, qc_ref, ckv_ref, kr_ref,
                     wukv_ref, wout_ref, x_ref, gate_ref, o_ref, o_scr):
    scale = HEAD_DIM ** -0.5
    ka = ka_ref[...].astype(bf16)
    va = va_ref[...].astype(bf16)
    for h in range(H_A):
        g = h // G_A
        q = qa_ref[:, h * HEAD_DIM:(h + 1) * HEAD_DIM]
        s = _dot_nt(q, ka[:, g * HEAD_DIM:(g + 1) * HEAD_DIM]) * scale
        o_scr[:, h * HEAD_DIM:(h + 1) * HEAD_DIM] = _softmax_pv(s, va[:, g * HEAD_DIM:(g + 1) * HEAD_DIM],
                                                               sink_ref[h])
    kb = kb_ref[...].astype(bf16)
    vb = vb_ref[...].astype(bf16)
    for h in range(H_B):
        sl = slice(h * HEAD_DIM, (h + 1) * HEAD_DIM)
        s = _dot_nt(qb_ref[:, sl], kb[:, sl]) * scale
        o_scr[:, W_QA + h * HEAD_DIM:W_QA + (h + 1) * HEAD_DIM] = _softmax_pv(s, vb[:, sl])
    kv = _dot(ckv_ref[...].astype(bf16), wukv_ref[...]).astype(bf16)
    kr = kr_ref[...].astype(bf16)
    scale_c = (QK_NOPE + QK_ROPE) ** -0.5
    for h in range(H_C):
        qn = qc_ref[:, h * QC_PAD:h * QC_PAD + QK_NOPE]
        qr = qc_ref[:, h * QC_PAD + QK_NOPE:h * QC_PAD + QK_NOPE + QK_ROPE]
        c0 = h * (QK_NOPE + V_C)
        s = (_dot_nt(qn, kv[:, c0:c0 + QK_NOPE]) + _dot_nt(qr, kr)) * scale_c
        off = W_QA + W_B + h * V_C
        o_scr[:, off:off + V_C] = _softmax_pv(s, kv[:, c0 + QK_NOPE:c0 + QK_NOPE + V_C])
    y = _dot(o_scr[...].astype(bf16), wout_ref[...])
    o_ref[...] = x_ref[...] + gate_ref[0] * y


def _ctx_attn(sink, proj, wukv, wout, x, gate):
    qa, ka, va, qb, kb, vb, qc, ckv, kr = proj
    row = lambda b: (b, 0)
    const = lambda b: (0, 0)
    in_specs = [pl.BlockSpec(memory_space=pltpu.SMEM)]
    in_specs += [pl.BlockSpec((SEQ, a.shape[1]), row) for a in proj]
    in_specs += [pl.BlockSpec((KV_LORA, H_C * (QK_NOPE + V_C)), const),
                 pl.BlockSpec((D_MODEL, D_MODEL), const),
                 pl.BlockSpec((SEQ, D_MODEL), row),
                 pl.BlockSpec((1, 1, D_MODEL), lambda b: (0, 0, 0))]
    return pl.pallas_call(
        _ctx_attn_kernel,
        grid=(BATCH,),
        in_specs=in_specs,
        out_specs=pl.BlockSpec((SEQ, D_MODEL), row),
        out_shape=jax.ShapeDtypeStruct((T_CTX, D_MODEL), f32),
        scratch_shapes=[pltpu.VMEM((SEQ, D_MODEL), f32)],
        compiler_params=_cparams("arbitrary"),
        name="ctx_attn",
    )(sink, qa, ka, va, qb, kb, vb, qc, ckv, kr, wukv, wout, x, gate)


def _lat_attn_kernel(sink_ref, qa_ref, qb_ref, qc_ref, ka_ref, va_ref, kb_ref, vb_ref, ckv_ref, kr_ref,
                     cak_ref, cav_ref, cbk_ref, cbv_ref, cckv_ref, ckr_ref, bias_ref,
                     wukv_ref, wout_ref, x_ref, gate_ref, o_ref, o_scr, kv_scr):
    qi = pl.program_id(1)
    nb = DEC_SEQ // BLOCK
    scale = HEAD_DIM ** -0.5

    @pl.when(qi == 0)
    def _():
        kv_scr[0:DEC_SEQ, :] = _dot(ckv_ref[...], wukv_ref[...]).astype(bf16)
        kv_scr[DEC_SEQ:DEC_SEQ + PAST_LEN, :] = _dot(cckv_ref[0, 0].astype(bf16), wukv_ref[...]).astype(bf16)

    def blk(ref, j):
        idx = jnp.clip(qi + j, 0, nb - 1)
        return ref[pl.ds(pl.multiple_of(idx * BLOCK, BLOCK), BLOCK), :]

    ka = jnp.concatenate([blk(ka_ref, -1), blk(ka_ref, 0), blk(ka_ref, 1), cak_ref[0, 0].astype(bf16)], axis=0)
    va = jnp.concatenate([blk(va_ref, -1), blk(va_ref, 0), blk(va_ref, 1), cav_ref[0, 0].astype(bf16)], axis=0)
    nk_a = 3 * BLOCK + PAST_LEN
    r = lax.broadcasted_iota(jnp.int32, (BLOCK, nk_a), 0)
    c = lax.broadcasted_iota(jnp.int32, (BLOCK, nk_a), 1)
    valid = (((c < BLOCK) & (c >= r) & (qi > 0))
             | ((c >= BLOCK) & (c < 2 * BLOCK))
             | ((c >= 2 * BLOCK) & (c < 3 * BLOCK) & (c - 2 * BLOCK <= r) & (qi < nb - 1))
             | (c >= 3 * BLOCK))
    for h in range(H_A):
        g = h // G_A
        q = qa_ref[:, h * HEAD_DIM:(h + 1) * HEAD_DIM]
        s = _dot_nt(q, ka[:, g * HEAD_DIM:(g + 1) * HEAD_DIM]) * scale
        s = jnp.where(valid, s, NEG)
        o_scr[:, h * HEAD_DIM:(h + 1) * HEAD_DIM] = _softmax_pv(s, va[:, g * HEAD_DIM:(g + 1) * HEAD_DIM],
                                                               sink_ref[h])

    cbk = cbk_ref[0, 0].astype(bf16)
    cbv = cbv_ref[0, 0].astype(bf16)
    rows_per_blk = BLOCK // GRID_W
    nloc = NA_ROWS * GRID_W
    for half in range(rows_per_blk):
        grow = qi * rows_per_blk + half
        start = jnp.clip(grow - NA_ROWS // 2, 0, ROWS - NA_ROWS)
        kloc = kb_ref[pl.ds(pl.multiple_of(start * GRID_W, GRID_W), nloc), :]
        vloc = vb_ref[pl.ds(pl.multiple_of(start * GRID_W, GRID_W), nloc), :]
        vcat = jnp.concatenate([vloc, cbv], axis=0)
        qrows = slice(half * GRID_W, (half + 1) * GRID_W)
        for h in range(H_B):
            sl = slice(h * HEAD_DIM, (h + 1) * HEAD_DIM)
            q = qb_ref[qrows, sl]
            s_loc = _dot_nt(q, kloc[:, sl]) * scale + bias_ref[half, h]
            s_ctx = _dot_nt(q, cbk[:, sl]) * scale
            s = jnp.concatenate([s_loc, s_ctx], axis=1)
            o_scr[qrows, W_QA + h * HEAD_DIM:W_QA + (h + 1) * HEAD_DIM] = _softmax_pv(s, vcat[:, sl])

    kr = jnp.concatenate([kr_ref[...], ckr_ref[0, 0].astype(bf16)], axis=0)
    scale_c = (QK_NOPE + QK_ROPE) ** -0.5
    for h in range(H_C):
        qn = qc_ref[:, h * QC_PAD:h * QC_PAD + QK_NOPE]
        qr = qc_ref[:, h * QC_PAD + QK_NOPE:h * QC_PAD + QK_NOPE + QK_ROPE]
        c0 = h * (QK_NOPE + V_C)
        s = (_dot_nt(qn, kv_scr[:, c0:c0 + QK_NOPE]) + _dot_nt(qr, kr)) * scale_c
        off = W_QA + W_B + h * V_C
        o_scr[:, off:off + V_C] = _softmax_pv(s, kv_scr[:, c0 + QK_NOPE:c0 + QK_NOPE + V_C])

    y = _dot(o_scr[...].astype(bf16), wout_ref[...])
    o_ref[...] = x_ref[...] + gate_ref[0] * y


def _lat_attn(layer, sink, proj, caches, bias_tab, wukv, wout, x, gate):
    qa, ka, va, qb, kb, vb, qc, ckv, kr = proj
    nb = DEC_SEQ // BLOCK
    qrow = lambda b, q: (b * nb + q, 0)
    xrow = lambda b, q: (T_CTX // BLOCK + b * nb + q, 0)
    brow = lambda b, q: (b, 0)
    const = lambda b, q: (0, 0)
    cidx = lambda b, q: (b, layer, 0, 0)
    in_specs = [pl.BlockSpec(memory_space=pltpu.SMEM)]
    in_specs += [pl.BlockSpec((BLOCK, a.shape[1]), qrow) for a in (qa, qb, qc)]
    in_specs += [pl.BlockSpec((DEC_SEQ, a.shape[1]), brow) for a in (ka, va, kb, vb, ckv, kr)]
    in_specs += [pl.BlockSpec((1, 1, PAST_LEN, a.shape[3]), cidx) for a in caches]
    in_specs += [pl.BlockSpec((BLOCK // GRID_W, H_B, GRID_W, NA_ROWS * GRID_W), lambda b, q: (q, 0, 0, 0)),
                 pl.BlockSpec((KV_LORA, H_C * (QK_NOPE + V_C)), const),
                 pl.BlockSpec((D_MODEL, D_MODEL), const),
                 pl.BlockSpec((BLOCK, D_MODEL), xrow),
                 pl.BlockSpec((1, 1, D_MODEL), lambda b, q: (1 + b, 0, 0))]
    return pl.pallas_call(
        _lat_attn_kernel,
        grid=(DEC_BATCH, nb),
        in_specs=in_specs,
        out_specs=pl.BlockSpec((BLOCK, D_MODEL), qrow),
        out_shape=jax.ShapeDtypeStruct((T_LAT, D_MODEL), f32),
        scratch_shapes=[pltpu.VMEM((BLOCK, D_MODEL), f32),
                        pltpu.VMEM((DEC_SEQ + PAST_LEN, H_C * (QK_NOPE + V_C)), bf16)],
        compiler_params=_cparams("arbitrary", "arbitrary"),
        name="lat_attn",
    )(sink, qa, qb, qc, ka, va, kb, vb, ckv, kr, *caches, bias_tab, wukv, wout, x, gate)


def _router_kernel(x_ref, g_ref, sh_ref, sc_ref, wr_ref, br_ref, h_ref, e_ref, gt_ref):
    h = _rms(x_ref[...], g_ref[...]) * (1.0 + sc_ref[0]) + sh_ref[0]
    _store_row_tiles(h_ref, h)
    logits = jnp.dot(h, wr_ref[...], preferred_element_type=f32, precision=lax.Precision.HIGHEST) + br_ref[...]
    lane = lax.broadcasted_iota(jnp.int32, logits.shape, 1).astype(f32)
    l = jnp.where(lane < N_EXPERTS, logits, -jnp.inf)
    tops, idxs = [], []
    for _ in range(TOP_K):
        m = jnp.max(l, axis=-1, keepdims=True)
        idx = jnp.min(jnp.where(l == m, lane, float(LANE)), axis=-1, keepdims=True)
        tops.append(m)
        idxs.append(idx)
        l = jnp.where(lane == idx, -jnp.inf, l)
    ex = [jnp.exp(t - tops[0]) for t in tops]
    den = ex[0] + ex[1] + ex[2] + ex[3]
    e_out = jnp.zeros(logits.shape, f32)
    g_out = jnp.zeros(logits.shape, f32)
    for k in range(TOP_K):
        e_out = jnp.where(lane == k, idxs[k], e_out)
        g_out = jnp.where(lane == k, ex[k] / den, g_out)
    e_ref[...] = e_out.astype(jnp.int32)
    gt_ref[...] = g_out


def _group_of_tile(i):
    per_b = DEC_SEQ // TM_TOK
    n_ctx = T_CTX // TM_TOK
    return jnp.where(i < n_ctx, 0, 1 + (i - n_ctx) // per_b)


def _router(x, g, shift, scale, wr, br):
    tm = TM_TOK
    row = lambda i: (i, 0)
    const = lambda i: (0, 0)
    grp = lambda i: (_group_of_tile(i), 0, 0)
    return pl.pallas_call(
        _router_kernel,
        grid=(T_ALL // tm,),
        in_specs=[pl.BlockSpec((tm, D_MODEL), row),
                  pl.BlockSpec((1, D_MODEL), const),
                  pl.BlockSpec((1, 1, D_MODEL), grp),
                  pl.BlockSpec((1, 1, D_MODEL), grp),
                  pl.BlockSpec((D_MODEL, LANE), const),
                  pl.BlockSpec((1, LANE), const)],
        out_specs=[pl.BlockSpec((tm * ROW_TILE, LANE), row), pl.BlockSpec((tm, LANE), row),
                   pl.BlockSpec((tm, LANE), row)],
        out_shape=[jax.ShapeDtypeStruct((T_ALL * ROW_TILE, LANE), f32),
                   jax.ShapeDtypeStruct((T_ALL, LANE), jnp.int32),
                   jax.ShapeDtypeStruct((T_ALL, LANE), f32)],
        compiler_params=_cparams("arbitrary"),
        name="router",
    )(x, g, shift, scale, wr, br)


def _moe_kernel(be_ref, nu_ref, tok_ref, dst_ref, h_hbm, wgu_ref, bgu_ref, wd_ref, bd_ref, y_hbm,
                x0, x1, y0, y1, wgu_bf, wd_bf, gsem, ssem):
    tm = TM_MOE
    i = pl.program_id(0)
    nb = pl.num_programs(0)
    used = i < nu_ref[0]
    xb, yb = (x0, x1), (y0, y1)

    def tile(row):
        return pl.ds(pl.multiple_of(row * ROW_TILE, ROW_TILE), ROW_TILE)

    def gather_desc(src_row, buf, r, s):
        return pltpu.make_async_copy(h_hbm.at[tile(src_row)], buf.at[tile(r)], gsem.at[s])

    def scatter_desc(buf, r, dst_row, s):
        return pltpu.make_async_copy(buf.at[tile(r)], y_hbm.at[tile(dst_row)], ssem.at[s])

    def gather_wait(s):
        pltpu.make_async_copy(h_hbm.at[pl.ds(0, tm * ROW_TILE)], xb[s], gsem.at[s]).wait()

    def scatter_wait(s):
        pltpu.make_async_copy(yb[s], y_hbm.at[pl.ds(0, tm * ROW_TILE)], ssem.at[s]).wait()

    def issue(blk_gather, s_gather, blk_scatter, s_scatter, unrolled):
        def one(r):
            scatter_desc(yb[s_scatter], r, dst_ref[(blk_scatter + 1) * tm + r], s_scatter).start()
            gather_desc(tok_ref[blk_gather * tm + r], xb[s_gather], r, s_gather).start()
        if unrolled:
            for r in range(tm):
                one(r)
        else:
            def body(r, carry):
                one(r)
                return carry
            lax.fori_loop(0, tm, body, 0, unroll=8)

    @pl.when(i == 0)
    def _():
        for s in range(2):
            yb[s][...] = jnp.zeros_like(yb[s])
            dummy = pltpu.make_async_copy(yb[s], y_hbm.at[pl.ds((N_ASSIGN + s * tm) * ROW_TILE, tm * ROW_TILE)],
                                          ssem.at[s])
            dummy.start()
            dummy.wait()

        def body(r, carry):
            gather_desc(tok_ref[r], xb[0], r, 0).start()
            return carry
        lax.fori_loop(0, tm, body, 0, unroll=8)

    first = jnp.logical_or(i == 0, be_ref[i] != be_ref[jnp.maximum(i - 1, 0)])

    @pl.when(jnp.logical_and(used, first))
    def _():
        wgu_bf[...] = wgu_ref[0, 0].astype(bf16)
        wd_bf[...] = wd_ref[0, 0].astype(bf16)

    def step(par):
        cur, oth = par, 1 - par
        gather_wait(cur)

        @pl.when(i >= 1)
        def _():
            scatter_wait(cur)

        @pl.when(used)
        def _():
            issue(i + 1, oth, i - 1, oth, unrolled=True)
            gu = _dot(_load_row_tiles(xb[cur]).astype(bf16), wgu_bf[...]) + bgu_ref[0, 0]
            x_glu = jnp.minimum(gu[:, :D_FF], SWIGLU_LIMIT)
            x_lin = jnp.clip(gu[:, D_FF:], -SWIGLU_LIMIT, SWIGLU_LIMIT)
            act = x_glu * jax.nn.sigmoid(SWIGLU_ALPHA * x_glu) * (x_lin + 1.0)
            _store_row_tiles(yb[cur], _dot(act.astype(bf16), wd_bf[...]) + bd_ref[0, 0])

        @pl.when(jnp.logical_and(jnp.logical_not(used), i + 1 < nb))
        def _():
            issue(i + 1, oth, i - 1, oth, unrolled=False)

        @pl.when(i == nb - 1)
        def _():
            def body(r, carry):
                scatter_desc(yb[oth], r, dst_ref[i * tm + r], oth).start()
                return carry
            lax.fori_loop(0, tm, body, 0, unroll=8)
            scatter_wait(oth)

    @pl.when(i % 2 == 0)
    def _():
        step(0)

    @pl.when(i % 2 == 1)
    def _():
        step(1)


def _moe(layer, block_e, n_used, row_tok, row_dst, h, w_gu, b_gu, w_down, b_down):
    tm = TM_MOE
    ex4 = lambda i, be, nu, tok, dst: (layer, be[i], 0, 0)
    return pl.pallas_call(
        _moe_kernel,
        grid_spec=pltpu.PrefetchScalarGridSpec(
            num_scalar_prefetch=4,
            grid=(N_MOE_BLOCKS,),
            in_specs=[pl.BlockSpec(memory_space=pl.ANY),
                      pl.BlockSpec((1, 1, D_MODEL, 2 * D_FF), ex4),
                      pl.BlockSpec((1, 1, 1, 2 * D_FF), ex4),
                      pl.BlockSpec((1, 1, D_FF, D_MODEL), ex4),
                      pl.BlockSpec((1, 1, 1, D_MODEL), ex4)],
            out_specs=pl.BlockSpec(memory_space=pl.ANY),
            scratch_shapes=[pltpu.VMEM((tm * ROW_TILE, LANE), f32), pltpu.VMEM((tm * ROW_TILE, LANE), f32),
                            pltpu.VMEM((tm * ROW_TILE, LANE), f32), pltpu.VMEM((tm * ROW_TILE, LANE), f32),
                            pltpu.VMEM((D_MODEL, 2 * D_FF), bf16), pltpu.VMEM((D_FF, D_MODEL), bf16),
                            pltpu.SemaphoreType.DMA((2,)), pltpu.SemaphoreType.DMA((2,))]),
        out_shape=jax.ShapeDtypeStruct(((N_ASSIGN + 2 * tm) * ROW_TILE, LANE), f32),
        compiler_params=_cparams("arbitrary"),
        name="moe",
    )(block_e, n_used, row_tok, row_dst, h, w_gu, b_gu.reshape(DEPTH, N_EXPERTS, 1, 2 * D_FF), w_down,
      b_down.reshape(DEPTH, N_EXPERTS, 1, D_MODEL))


def _combine_kernel(final, x_ref, y0_ref, y1_ref, y2_ref, y3_ref, gt_ref, gate_ref, gf_ref, o_ref):
    gt = gt_ref[...]
    f = gt[:, 0:1] * _load_row_tiles(y0_ref)
    for k, y_ref in ((1, y1_ref), (2, y2_ref), (3, y3_ref)):
        f = f + gt[:, k:k + 1] * _load_row_tiles(y_ref)
    out = x_ref[...] + gate_ref[0] * f
    if final:
        out = _rms(out, gf_ref[...])
    o_ref[...] = out


def _combine(final, x, y, gates, gate, g_final):
    tm = TM_TOK
    nt = T_ALL // tm
    row = lambda i: (i, 0)
    const = lambda i: (0, 0)
    grp = lambda i: (_group_of_tile(i), 0, 0)
    ysel = [pl.BlockSpec((tm * ROW_TILE, LANE), functools.partial(lambda k, i: (k * nt + i, 0), k))
            for k in range(TOP_K)]
    return pl.pallas_call(
        functools.partial(_combine_kernel, final),
        grid=(nt,),
        in_specs=[pl.BlockSpec((tm, D_MODEL), row)] + ysel +
                 [pl.BlockSpec((tm, LANE), row),
                  pl.BlockSpec((1, 1, D_MODEL), grp),
                  pl.BlockSpec((1, D_MODEL), const)],
        out_specs=pl.BlockSpec((tm, D_MODEL), row),
        out_shape=jax.ShapeDtypeStruct((T_ALL, D_MODEL), f32),
        compiler_params=_cparams("arbitrary"),
        name="combine",
    )(x, y, y, y, y, gates, gate, g_final)


def _rope_head_tables(d):
    nf = d // 4
    half = d // 2
    t = np.arange(DEC_SEQ)
    inv = ROPE_BASE ** (-np.arange(nf, dtype=np.float32) / nf)
    i = np.arange(d)
    pos = np.where(i[None, :] < half, (t // GRID_W)[:, None], (t % GRID_W)[:, None]).astype(np.float32)
    ang = pos * inv[i % nf][None, :].astype(np.float32)
    first = (i % half) < nf
    cos = np.cos(ang)
    sin = np.where(first[None, :], -np.sin(ang), np.sin(ang))
    partner = np.where(first, i + nf, i - nf)
    return cos.astype(np.float32), sin.astype(np.float32), partner


def _rope_tables():
    cos64, sin64, _ = _rope_head_tables(HEAD_DIM)
    cos32, sin32, _ = _rope_head_tables(QK_ROPE)
    cosa = np.tile(cos64, (1, H_A))
    sina = np.tile(sin64, (1, H_A))
    cosq1 = np.concatenate([np.ones((DEC_SEQ, QK_NOPE), np.float32), cos32,
                            np.ones((DEC_SEQ, QC_PAD - QK_NOPE - QK_ROPE), np.float32)], axis=1)
    sinq1 = np.concatenate([np.zeros((DEC_SEQ, QK_NOPE), np.float32), sin32,
                            np.zeros((DEC_SEQ, QC_PAD - QK_NOPE - QK_ROPE), np.float32)], axis=1)
    cosq = np.tile(cosq1, (1, H_C))
    sinq = np.tile(sinq1, (1, H_C))
    return tuple(jnp.asarray(a) for a in (cosa, sina, cosq, sinq, cos32, sin32))


def _pad_cols(w, n):
    return jnp.pad(w, ((0, 0), (0, n - w.shape[1])))


def _layer_weights(w_in, w_uq):
    cuts = np.cumsum((W_QA, W_KA, W_VA, W_B, W_B, W_B, Q_LORA, KV_LORA, QK_ROPE))[:-1]
    qa, ka, va, qb, kb, vb, cq, ckv, kr = jnp.split(w_in, [int(c) for c in cuts], axis=1)
    _, _, p64 = _rope_head_tables(HEAD_DIM)
    _, _, p32 = _rope_head_tables(QK_ROPE)
    pa = np.concatenate([h * HEAD_DIM + p64 for h in range(H_A)])
    base = jnp.concatenate([qa, ka, va, _pad_cols(qb, 384), _pad_cols(kb, 384), _pad_cols(vb, 384), cq, ckv,
                            _pad_cols(kr, 128)], axis=1)
    w_ctx = base.astype(bf16)
    w_lat = jnp.concatenate([base, qa[:, pa], ka[:, pa[:W_KA]], _pad_cols(kr[:, p32], 128)], axis=1).astype(bf16)
    hq = QK_NOPE + QK_ROPE
    heads = [_pad_cols(w_uq[:, h * hq:(h + 1) * hq], QC_PAD) for h in range(H_C)]
    pq = np.concatenate([np.arange(QK_NOPE), QK_NOPE + p32])
    heads_p = [_pad_cols(w_uq[:, h * hq:(h + 1) * hq][:, pq], QC_PAD) for h in range(H_C)]
    wuq = jnp.concatenate(heads, axis=1).astype(bf16)
    wuq2 = jnp.concatenate(heads + heads_p, axis=1).astype(bf16)
    return w_ctx, w_lat, wuq, wuq2


def _bias_table(rpb):
    r = np.arange(ROWS)
    key_rows = np.clip(r - NA_ROWS // 2, 0, ROWS - NA_ROWS)[:, None] + np.arange(NA_ROWS)[None, :]
    col = np.arange(GRID_W)
    col_start = np.clip(col - NA_COLS // 2, 0, GRID_W - NA_COLS)
    col_ok = (col[None, :] >= col_start[:, None]) & (col[None, :] < col_start[:, None] + NA_COLS)
    dr = key_rows - r[:, None] + (NA_ROWS - 1)
    dc = np.clip(col[None, :] - col[:, None] + (NA_COLS - 1), 0, 2 * NA_COLS - 2)
    bias = rpb[:, dr][:, :, :, dc]
    bias = jnp.where(col_ok[None, None, None], bias, NEG)
    bias = jnp.transpose(bias, (1, 0, 3, 2, 4))
    return bias.reshape(ROWS, H_B, GRID_W, NA_ROWS * GRID_W).astype(f32)


def _routing(top_e):
    tm = TM_MOE
    flat_e = top_e.T.reshape(N_ASSIGN)
    order = jnp.argsort(flat_e).astype(jnp.int32)
    experts = jnp.arange(N_EXPERTS, dtype=jnp.int32)
    counts = jnp.sum((flat_e[:, None] == experts[None, :]).astype(jnp.int32), axis=0)
    nblk = (counts + tm - 1) // tm
    blk_end = jnp.cumsum(nblk)
    blk_start = blk_end - nblk
    grp_start = jnp.cumsum(counts) - counts
    blocks = jnp.arange(N_MOE_BLOCKS, dtype=jnp.int32)
    block_e = jnp.minimum(jnp.sum((blk_end[None, :] <= blocks[:, None]).astype(jnp.int32), axis=1), N_EXPERTS - 1)
    n_used = blk_end[-1].astype(jnp.int32).reshape(1)
    sel = (block_e[:, None] == experts[None, :]).astype(jnp.int32)
    b_first = jnp.sum(sel * blk_start[None, :], axis=1)
    b_count = jnp.sum(sel * counts[None, :], axis=1)
    b_grp = jnp.sum(sel * grp_start[None, :], axis=1)
    r = jnp.arange(tm, dtype=jnp.int32)[None, :]
    off = (blocks - b_first)[:, None] * tm + r
    valid = (off < b_count[:, None]) & (blocks[:, None] < n_used[0])
    asg = order[jnp.clip(b_grp[:, None] + off, 0, N_ASSIGN - 1)]
    row_tok = jnp.where(valid, asg % T_ALL, 0).reshape(-1)
    row_dst = jnp.where(valid, asg, N_ASSIGN + (blocks[:, None] % 2) * tm + r)
    row_dst = jnp.concatenate([N_ASSIGN + tm + r, row_dst], axis=0).reshape(-1)
    return block_e.astype(jnp.int32), n_used, row_tok.astype(jnp.int32), row_dst.astype(jnp.int32)


def kernel(x_prompt, x_sample, cache_a_k, cache_a_v, cache_b_k, cache_b_v, cache_c_kv, cache_c_kr, c, c_ctx, w_ada, b_ada, g_attn, g_ffn, w_in, sink_a, rpb_b, g_cq, g_ckv, w_uq, w_ukv, w_out, w_router, b_router, w_gu, b_gu, w_down, b_down, g_final):
    x = jnp.concatenate([x_prompt.reshape(T_CTX, D_MODEL), x_sample.reshape(T_LAT, D_MODEL)], axis=0)
    cvec = jnp.concatenate([c_ctx[None, :], c, jnp.zeros((8 - N_GROUPS, D_MODEL), f32)], axis=0)
    mods = _ada(cvec, w_ada, b_ada)[:, :N_GROUPS].reshape(DEPTH, N_GROUPS, 6, 1, D_MODEL)
    tabs = _rope_tables()
    caches = (cache_a_k.reshape(DEC_BATCH, DEPTH, PAST_LEN, W_KA), cache_a_v.reshape(DEC_BATCH, DEPTH, PAST_LEN, W_VA),
              cache_b_k.reshape(DEC_BATCH, DEPTH, PAST_LEN, W_B), cache_b_v.reshape(DEC_BATCH, DEPTH, PAST_LEN, W_B),
              cache_c_kv, cache_c_kr)
    new = [[] for _ in range(6)]
    for layer in range(DEPTH):
        m = [mods[layer, :, j] for j in range(6)]
        w_ctx, w_lat, wuq, wuq2 = _layer_weights(w_in[layer], w_uq[layer])
        wukv = w_ukv[layer].astype(bf16)
        wout = w_out[layer].astype(bf16)
        g1 = g_attn[layer][None, :]
        gcq = g_cq[layer][None, :]
        gckv = g_ckv[layer][None, :]
        sink = sink_a[layer]

        pc = _inproj_ctx(x, g1, m[0], m[1], w_ctx, gcq, gckv, wuq)
        for lst, a in zip(new, (pc[1], pc[2], pc[4], pc[5], pc[7], pc[8])):
            lst.append(a)
        x_ctx = _ctx_attn(sink, pc, wukv, wout, x, m[2])

        plat = _inproj_lat(x, g1, m[0], m[1], w_lat, gcq, gckv, wuq2, tabs)
        x_lat = _lat_attn(layer, sink, plat, caches, _bias_table(rpb_b[layer]), wukv, wout, x, m[2])
        x = jnp.concatenate([x_ctx, x_lat], axis=0)

        wr = _pad_cols(w_router[layer], LANE)
        br = _pad_cols(b_router[layer][None, :], LANE)
        h2, top_e, gates = _router(x, g_ffn[layer][None, :], m[3], m[4], wr, br)
        block_e, n_used, row_tok, row_dst = _routing(top_e[:, :TOP_K])
        y = _moe(layer, block_e, n_used, row_tok, row_dst, h2, w_gu, b_gu, w_down, b_down)
        x = _combine(layer == DEPTH - 1, x, y, gates, m[5], g_final[None, :])

    y_prompt = x[:T_CTX].reshape(BATCH, SEQ, D_MODEL)
    y_sample = x[T_CTX:].reshape(DEC_BATCH, DEC_SEQ, D_MODEL)
    shapes = ((KV_A, HEAD_DIM), (KV_A, HEAD_DIM), (H_B, HEAD_DIM), (H_B, HEAD_DIM), (KV_LORA,), (QK_ROPE,))
    outs = [jnp.stack([a.reshape((BATCH, SEQ) + s) for a in lst], axis=1) for lst, s in zip(new, shapes)]
    return (y_prompt, y_sample, *outs)
```

```python
import functools

import numpy as np
import jax
import jax.numpy as jnp
from jax import lax
from jax.experimental import pallas as pl
from jax.experimental.pallas import tpu as pltpu

D_MODEL = 1024
BATCH = 32
SEQ = 256
DEPTH = 2
DEC_BATCH = 2
DEC_SEQ = 1024
PAST_LEN = 512
GRID_W = 64
HEAD_DIM = 64
H_A = 6
KV_A = 2
G_A = H_A // KV_A
WINDOW = 128
BLOCK = 128
H_B = 5
NA_ROWS = 8
NA_COLS = 16
H_C = 5
Q_LORA = 384
KV_LORA = 256
QK_NOPE = 64
QK_ROPE = 32
V_C = 64
N_EXPERTS = 32
TOP_K = 4
D_FF = 1024
SWIGLU_ALPHA = 1.702
SWIGLU_LIMIT = 7.0
ROPE_BASE = 10000.0
EPS = 1e-6
NEG = -1e30

T_CTX = BATCH * SEQ
T_LAT = DEC_BATCH * DEC_SEQ
T_ALL = T_CTX + T_LAT
N_GROUPS = 1 + DEC_BATCH
LANE = 128
QC_PAD = 128
ROWS = DEC_SEQ // GRID_W

W_QA, W_KA, W_VA = H_A * HEAD_DIM, KV_A * HEAD_DIM, KV_A * HEAD_DIM
W_B = H_B * HEAD_DIM
OFF_QA = 0
OFF_KA = 384
OFF_VA = 512
OFF_QB = 640
OFF_KB = 1024
OFF_VB = 1408
OFF_CQ = 1792
OFF_CKV = 2176
OFF_KR = 2432
NW_CTX = 2560
OFF_QA_P = 2560
OFF_KA_P = 2944
OFF_KR_P = 3072
NW_LAT = 3200

TM_TOK = 256
TM_LAT_IN = 512
TM_MOE = 256
N_ASSIGN = T_ALL * TOP_K
N_MOE_BLOCKS = N_ASSIGN // TM_MOE + N_EXPERTS
VMEM_LIMIT = 56 * 1024 * 1024

f32 = jnp.float32
bf16 = jnp.bfloat16


def _cparams(*sem):
    return pltpu.CompilerParams(dimension_semantics=sem, vmem_limit_bytes=VMEM_LIMIT)


def _rms(xf, g):
    return xf * lax.rsqrt(jnp.mean(xf * xf, axis=-1, keepdims=True) + EPS) * g


def _dot(a, b):
    return jnp.dot(a, b, preferred_element_type=f32)


def _dot_nt(a, b):
    return lax.dot_general(a, b, (((1,), (1,)), ((), ())), preferred_element_type=f32)


ROW_TILE = D_MODEL // LANE


def _store_row_tiles(ref, val):
    n = val.shape[0]
    for c in range(ROW_TILE):
        ref[pl.ds(c, n, stride=ROW_TILE), :] = val[:, c * LANE:(c + 1) * LANE]


def _load_row_tiles(ref):
    n = ref.shape[0] // ROW_TILE
    return jnp.concatenate([ref[pl.ds(c, n, stride=ROW_TILE), :] for c in range(ROW_TILE)], axis=1)


def _softmax_pv(s, v, sink=None):
    m = jnp.max(s, axis=-1, keepdims=True)
    if sink is not None:
        m = jnp.maximum(m, sink)
    p = jnp.exp(s - m)
    l = jnp.sum(p, axis=-1, keepdims=True)
    if sink is not None:
        l = l + jnp.exp(sink - m)
    return _dot(p.astype(bf16), v) / l


def _ada_kernel(c_ref, w_ref, b_ref, o_ref):
    c = c_ref[...]
    s = c * jax.nn.sigmoid(c)
    o_ref[0] = jnp.dot(s, w_ref[0], preferred_element_type=f32, precision=lax.Precision.HIGHEST) + b_ref[0]


def _ada(cvec, w_ada, b_ada):
    tn = 1536
    return pl.pallas_call(
        _ada_kernel,
        grid=(DEPTH, 6 * D_MODEL // tn),
        in_specs=[pl.BlockSpec((8, D_MODEL), lambda l, j: (0, 0)),
                  pl.BlockSpec((1, D_MODEL, tn), lambda l, j: (l, 0, j)),
                  pl.BlockSpec((1, 1, tn), lambda l, j: (l, 0, j))],
        out_specs=pl.BlockSpec((1, 8, tn), lambda l, j: (l, 0, j)),
        out_shape=jax.ShapeDtypeStruct((DEPTH, 8, 6 * D_MODEL), f32),
        compiler_params=_cparams("arbitrary", "arbitrary"),
        name="ada",
    )(cvec, w_ada, b_ada.reshape(DEPTH, 1, 6 * D_MODEL))


def _inproj_ctx_kernel(x_ref, g_ref, sh_ref, sc_ref, w_ref, gcq_ref, gckv_ref, wuq_ref,
                       qa_ref, ka_ref, va_ref, qb_ref, kb_ref, vb_ref, qc_ref, ckv_ref, kr_ref):
    h = _rms(x_ref[...], g_ref[...]) * (1.0 + sc_ref[0]) + sh_ref[0]
    p = _dot(h.astype(bf16), w_ref[...])
    qa_ref[...] = p[:, OFF_QA:OFF_QA + W_QA].astype(bf16)
    ka_ref[...] = p[:, OFF_KA:OFF_KA + W_KA]
    va_ref[...] = p[:, OFF_VA:OFF_VA + W_VA]
    qb_ref[...] = p[:, OFF_QB:OFF_QB + W_B].astype(bf16)
    kb_ref[...] = p[:, OFF_KB:OFF_KB + W_B]
    vb_ref[...] = p[:, OFF_VB:OFF_VB + W_B]
    cqn = _rms(p[:, OFF_CQ:OFF_CQ + Q_LORA], gcq_ref[...])
    qc_ref[...] = _dot(cqn.astype(bf16), wuq_ref[...]).astype(bf16)
    ckv_ref[...] = _rms(p[:, OFF_CKV:OFF_CKV + KV_LORA], gckv_ref[...])
    kr_ref[...] = p[:, OFF_KR:OFF_KR + QK_ROPE]


def _inproj_ctx(x, g, shift, scale, w, gcq, gckv, wuq):
    tm = TM_TOK
    row = lambda i: (i, 0)
    const = lambda i: (0, 0)
    widths = (W_QA, W_KA, W_VA, W_B, W_B, W_B, H_C * QC_PAD, KV_LORA, QK_ROPE)
    dtypes = (bf16, f32, f32, bf16, f32, f32, bf16, f32, f32)
    return pl.pallas_call(
        _inproj_ctx_kernel,
        grid=(T_CTX // tm,),
        in_specs=[pl.BlockSpec((tm, D_MODEL), row),
                  pl.BlockSpec((1, D_MODEL), const),
                  pl.BlockSpec((1, 1, D_MODEL), lambda i: (0, 0, 0)),
                  pl.BlockSpec((1, 1, D_MODEL), lambda i: (0, 0, 0)),
                  pl.BlockSpec((D_MODEL, NW_CTX), const),
                  pl.BlockSpec((1, Q_LORA), const),
                  pl.BlockSpec((1, KV_LORA), const),
                  pl.BlockSpec((Q_LORA, H_C * QC_PAD), const)],
        out_specs=[pl.BlockSpec((tm, wd), row) for wd in widths],
        out_shape=[jax.ShapeDtypeStruct((T_CTX, wd), dt) for wd, dt in zip(widths, dtypes)],
        compiler_params=_cparams("arbitrary"),
        name="inproj_ctx",
    )(x, g, shift, scale, w, gcq, gckv, wuq)


def _inproj_lat_kernel(x_ref, g_ref, sh_ref, sc_ref, w_ref, gcq_ref, gckv_ref, wuq_ref,
                       cosa_ref, sina_ref, cosq_ref, sinq_ref, cosr_ref, sinr_ref,
                       qa_ref, ka_ref, va_ref, qb_ref, kb_ref, vb_ref, qc_ref, ckv_ref, kr_ref):
    h = _rms(x_ref[...], g_ref[...]) * (1.0 + sc_ref[0]) + sh_ref[0]
    p = _dot(h.astype(bf16), w_ref[...])
    cosa = cosa_ref[...]
    sina = sina_ref[...]
    qa = p[:, OFF_QA:OFF_QA + W_QA] * cosa + p[:, OFF_QA_P:OFF_QA_P + W_QA] * sina
    ka = p[:, OFF_KA:OFF_KA + W_KA] * cosa[:, :W_KA] + p[:, OFF_KA_P:OFF_KA_P + W_KA] * sina[:, :W_KA]
    kr = p[:, OFF_KR:OFF_KR + QK_ROPE] * cosr_ref[...] + p[:, OFF_KR_P:OFF_KR_P + QK_ROPE] * sinr_ref[...]
    qa_ref[...] = qa.astype(bf16)
    ka_ref[...] = ka.astype(bf16)
    va_ref[...] = p[:, OFF_VA:OFF_VA + W_VA].astype(bf16)
    qb_ref[...] = p[:, OFF_QB:OFF_QB + W_B].astype(bf16)
    kb_ref[...] = p[:, OFF_KB:OFF_KB + W_B].astype(bf16)
    vb_ref[...] = p[:, OFF_VB:OFF_VB + W_B].astype(bf16)
    cqn = _rms(p[:, OFF_CQ:OFF_CQ + Q_LORA], gcq_ref[...])
    q2 = _dot(cqn.astype(bf16), wuq_ref[...])
    nq = H_C * QC_PAD
    qc_ref[...] = (q2[:, :nq] * cosq_ref[...] + q2[:, nq:] * sinq_ref[...]).astype(bf16)
    ckv_ref[...] = _rms(p[:, OFF_CKV:OFF_CKV + KV_LORA], gckv_ref[...]).astype(bf16)
    kr_ref[...] = kr.astype(bf16)


def _inproj_lat(x, g, shift, scale, w, gcq, gckv, wuq2, tabs):
    tm = TM_LAT_IN
    per_b = DEC_SEQ // tm
    row0 = T_CTX // tm
    xrow = lambda i: (row0 + i, 0)
    row = lambda i: (i, 0)
    const = lambda i: (0, 0)
    grp = lambda i: (1 + i // per_b, 0, 0)
    pos = lambda i: (i % per_b, 0)
    cosa, sina, cosq, sinq, cosr, sinr = tabs
    widths = (W_QA, W_KA, W_VA, W_B, W_B, W_B, H_C * QC_PAD, KV_LORA, QK_ROPE)
    return pl.pallas_call(
        _inproj_lat_kernel,
        grid=(T_LAT // tm,),
        in_specs=[pl.BlockSpec((tm, D_MODEL), xrow),
                  pl.BlockSpec((1, D_MODEL), const),
                  pl.BlockSpec((1, 1, D_MODEL), grp),
                  pl.BlockSpec((1, 1, D_MODEL), grp),
                  pl.BlockSpec((D_MODEL, NW_LAT), const),
                  pl.BlockSpec((1, Q_LORA), const),
                  pl.BlockSpec((1, KV_LORA), const),
                  pl.BlockSpec((Q_LORA, 2 * H_C * QC_PAD), const),
                  pl.BlockSpec((tm, W_QA), pos), pl.BlockSpec((tm, W_QA), pos),
                  pl.BlockSpec((tm, H_C * QC_PAD), pos), pl.BlockSpec((tm, H_C * QC_PAD), pos),
                  pl.BlockSpec((tm, QK_ROPE), pos), pl.BlockSpec((tm, QK_ROPE), pos)],
        out_specs=[pl.BlockSpec((tm, wd), row) for wd in widths],
        out_shape=[jax.ShapeDtypeStruct((T_LAT, wd), bf16) for wd in widths],
        compiler_params=_cparams("arbitrary"),
        name="inproj_lat",
    )(x, g, shift, scale, w, gcq, gckv, wuq2, cosa, sina, cosq, sinq, cosr, sinr)


def _ctx_attn_kernel(sink_ref, qa_ref, ka_ref, va_ref, qb_ref, kb_ref, vb_ref, qc_ref, ckv_ref, kr_ref,
                     wukv_ref, wout_ref, x_ref, gate_ref, o_ref, o_scr):
    scale = HEAD_DIM ** -0.5
    ka = ka_ref[...].astype(bf16)
    va = va_ref[...].astype(bf16)
    for h in range(H_A):
        g = h // G_A
        q = qa_ref[:, h * HEAD_DIM:(h + 1) * HEAD_DIM]
        s = _dot_nt(q, ka[:, g * HEAD_DIM:(g + 1) * HEAD_DIM]) * scale
        o_scr[:, h * HEAD_DIM:(h + 1) * HEAD_DIM] = _softmax_pv(s, va[:, g * HEAD_DIM:(g + 1) * HEAD_DIM],
                                                               sink_ref[h])
    kb = kb_ref[...].astype(bf16)
    vb = vb_ref[...].astype(bf16)
    for h in range(H_B):
        sl = slice(h * HEAD_DIM, (h + 1) * HEAD_DIM)
        s = _dot_nt(qb_ref[:, sl], kb[:, sl]) * scale
        o_scr[:, W_QA + h * HEAD_DIM:W_QA + (h + 1) * HEAD_DIM] = _softmax_pv(s, vb[:, sl])
    kv = _dot(ckv_ref[...].astype(bf16), wukv_ref[...]).astype(bf16)
    kr = kr_ref[...].astype(bf16)
    scale_c = (QK_NOPE + QK_ROPE) ** -0.5
    for h in range(H_C):
        qn = qc_ref[:, h * QC_PAD:h * QC_PAD + QK_NOPE]
        qr = qc_ref[:, h * QC_PAD + QK_NOPE:h * QC_PAD + QK_NOPE + QK_ROPE]
        c0 = h * (QK_NOPE + V_C)
        s = (_dot_nt(qn, kv[:, c0:c0 + QK_NOPE]) + _dot_nt(qr, kr)) * scale_c
        off = W_QA + W_B + h * V_C
        o_scr[:, off:off + V_C] = _softmax_pv(s, kv[:, c0 + QK_NOPE:c0 + QK_NOPE + V_C])
    y = _dot(o_scr[...].astype(bf16), wout_ref[...])
    o_ref[...] = x_ref[...] + gate_ref[0] * y


def _ctx_attn(sink, proj, wukv, wout, x, gate):
    qa, ka, va, qb, kb, vb, qc, ckv, kr = proj
    row = lambda b: (b, 0)
    const = lambda b: (0, 0)
    in_specs = [pl.BlockSpec(memory_space=pltpu.SMEM)]
    in_specs += [pl.BlockSpec((SEQ, a.shape[1]), row) for a in proj]
    in_specs += [pl.BlockSpec((KV_LORA, H_C * (QK_NOPE + V_C)), const),
                 pl.BlockSpec((D_MODEL, D_MODEL), const),
                 pl.BlockSpec((SEQ, D_MODEL), row),
                 pl.BlockSpec((1, 1, D_MODEL), lambda b: (0, 0, 0))]
    return pl.pallas_call(
        _ctx_attn_kernel,
        grid=(BATCH,),
        in_specs=in_specs,
        out_specs=pl.BlockSpec((SEQ, D_MODEL), row),
        out_shape=jax.ShapeDtypeStruct((T_CTX, D_MODEL), f32),
        scratch_shapes=[pltpu.VMEM((SEQ, D_MODEL), f32)],
        compiler_params=_cparams("arbitrary"),
        name="ctx_attn",
    )(sink, qa, ka, va, qb, kb, vb, qc, ckv, kr, wukv, wout, x, gate)


def _lat_attn_kernel(sink_ref, qa_ref, qb_ref, qc_ref, ka_ref, va_ref, kb_ref, vb_ref, ckv_ref, kr_ref,
                     cak_ref, cav_ref, cbk_ref, cbv_ref, cckv_ref, ckr_ref, bias_ref,
                     wukv_ref, wout_ref, x_ref, gate_ref, o_ref, o_scr, kv_scr):
    qi = pl.program_id(1)
    nb = DEC_SEQ // BLOCK
    scale = HEAD_DIM ** -0.5

    @pl.when(qi == 0)
    def _():
        kv_scr[0:DEC_SEQ, :] = _dot(ckv_ref[...], wukv_ref[...]).astype(bf16)
        kv_scr[DEC_SEQ:DEC_SEQ + PAST_LEN, :] = _dot(cckv_ref[0, 0].astype(bf16), wukv_ref[...]).astype(bf16)

    def blk(ref, j):
        idx = jnp.clip(qi + j, 0, nb - 1)
        return ref[pl.ds(pl.multiple_of(idx * BLOCK, BLOCK), BLOCK), :]

    ka = jnp.concatenate([blk(ka_ref, -1), blk(ka_ref, 0), blk(ka_ref, 1), cak_ref[0, 0].astype(bf16)], axis=0)
    va = jnp.concatenate([blk(va_ref, -1), blk(va_ref, 0), blk(va_ref, 1), cav_ref[0, 0].astype(bf16)], axis=0)
    nk_a = 3 * BLOCK + PAST_LEN
    r = lax.broadcasted_iota(jnp.int32, (BLOCK, nk_a), 0)
    c = lax.broadcasted_iota(jnp.int32, (BLOCK, nk_a), 1)
    valid = (((c < BLOCK) & (c >= r) & (qi > 0))
             | ((c >= BLOCK) & (c < 2 * BLOCK))
             | ((c >= 2 * BLOCK) & (c < 3 * BLOCK) & (c - 2 * BLOCK <= r) & (qi < nb - 1))
             | (c >= 3 * BLOCK))
    for h in range(H_A):
        g = h // G_A
        q = qa_ref[:, h * HEAD_DIM:(h + 1) * HEAD_DIM]
        s = _dot_nt(q, ka[:, g * HEAD_DIM:(g + 1) * HEAD_DIM]) * scale
        s = jnp.where(valid, s, NEG)
        o_scr[:, h * HEAD_DIM:(h + 1) * HEAD_DIM] = _softmax_pv(s, va[:, g * HEAD_DIM:(g + 1) * HEAD_DIM],
                                                               sink_ref[h])

    cbk = cbk_ref[0, 0].astype(bf16)
    cbv = cbv_ref[0, 0].astype(bf16)
    rows_per_blk = BLOCK // GRID_W
    nloc = NA_ROWS * GRID_W
    for half in range(rows_per_blk):
        grow = qi * rows_per_blk + half
        start = jnp.clip(grow - NA_ROWS // 2, 0, ROWS - NA_ROWS)
        kloc = kb_ref[pl.ds(pl.multiple_of(start * GRID_W, GRID_W), nloc), :]
        vloc = vb_ref[pl.ds(pl.multiple_of(start * GRID_W, GRID_W), nloc), :]
        vcat = jnp.concatenate([vloc, cbv], axis=0)
        qrows = slice(half * GRID_W, (half + 1) * GRID_W)
        for h in range(H_B):
            sl = slice(h * HEAD_DIM, (h + 1) * HEAD_DIM)
            q = qb_ref[qrows, sl]
            s_loc = _dot_nt(q, kloc[:, sl]) * scale + bias_ref[half, h]
            s_ctx = _dot_nt(q, cbk[:, sl]) * scale
            s = jnp.concatenate([s_loc, s_ctx], axis=1)
            o_scr[qrows, W_QA + h * HEAD_DIM:W_QA + (h + 1) * HEAD_DIM] = _softmax_pv(s, vcat[:, sl])

    kr = jnp.concatenate([kr_ref[...], ckr_ref[0, 0].astype(bf16)], axis=0)
    scale_c = (QK_NOPE + QK_ROPE) ** -0.5
    for h in range(H_C):
        qn = qc_ref[:, h * QC_PAD:h * QC_PAD + QK_NOPE]
        qr = qc_ref[:, h * QC_PAD + QK_NOPE:h * QC_PAD + QK_NOPE + QK_ROPE]
        c0 = h * (QK_NOPE + V_C)
        s = (_dot_nt(qn, kv_scr[:, c0:c0 + QK_NOPE]) + _dot_nt(qr, kr)) * scale_c
        off = W_QA + W_B + h * V_C
        o_scr[:, off:off + V_C] = _softmax_pv(s, kv_scr[:, c0 + QK_NOPE:c0 + QK_NOPE + V_C])

    y = _dot(o_scr[...].astype(bf16), wout_ref[...])
    o_ref[...] = x_ref[...] + gate_ref[0] * y


def _lat_attn(layer, sink, proj, caches, bias_tab, wukv, wout, x, gate):
    qa, ka, va, qb, kb, vb, qc, ckv, kr = proj
    nb = DEC_SEQ // BLOCK
    qrow = lambda b, q: (b * nb + q, 0)
    xrow = lambda b, q: (T_CTX // BLOCK + b * nb + q, 0)
    brow = lambda b, q: (b, 0)
    const = lambda b, q: (0, 0)
    cidx = lambda b, q: (b, layer, 0, 0)
    in_specs = [pl.BlockSpec(memory_space=pltpu.SMEM)]
    in_specs += [pl.BlockSpec((BLOCK, a.shape[1]), qrow) for a in (qa, qb, qc)]
    in_specs += [pl.BlockSpec((DEC_SEQ, a.shape[1]), brow) for a in (ka, va, kb, vb, ckv, kr)]
    in_specs += [pl.BlockSpec((1, 1, PAST_LEN, a.shape[3]), cidx) for a in caches]
    in_specs += [pl.BlockSpec((BLOCK // GRID_W, H_B, GRID_W, NA_ROWS * GRID_W), lambda b, q: (q, 0, 0, 0)),
                 pl.BlockSpec((KV_LORA, H_C * (QK_NOPE + V_C)), const),
                 pl.BlockSpec((D_MODEL, D_MODEL), const),
                 pl.BlockSpec((BLOCK, D_MODEL), xrow),
                 pl.BlockSpec((1, 1, D_MODEL), lambda b, q: (1 + b, 0, 0))]
    return pl.pallas_call(
        _lat_attn_kernel,
        grid=(DEC_BATCH, nb),
        in_specs=in_specs,
        out_specs=pl.BlockSpec((BLOCK, D_MODEL), qrow),
        out_shape=jax.ShapeDtypeStruct((T_LAT, D_MODEL), f32),
        scratch_shapes=[pltpu.VMEM((BLOCK, D_MODEL), f32),
                        pltpu.VMEM((DEC_SEQ + PAST_LEN, H_C * (QK_NOPE + V_C)), bf16)],
        compiler_params=_cparams("arbitrary", "arbitrary"),
        name="lat_attn",
    )(sink, qa, qb, qc, ka, va, kb, vb, ckv, kr, *caches, bias_tab, wukv, wout, x, gate)


def _router_kernel(x_ref, g_ref, sh_ref, sc_ref, wr_ref, br_ref, h_ref, e_ref, gt_ref):
    h = _rms(x_ref[...], g_ref[...]) * (1.0 + sc_ref[0]) + sh_ref[0]
    _store_row_tiles(h_ref, h)
    logits = jnp.dot(h, wr_ref[...], preferred_element_type=f32, precision=lax.Precision.HIGHEST) + br_ref[...]
    lane = lax.broadcasted_iota(jnp.int32, logits.shape, 1).astype(f32)
    l = jnp.where(lane < N_EXPERTS, logits, -jnp.inf)
    tops, idxs = [], []
    for _ in range(TOP_K):
        m = jnp.max(l, axis=-1, keepdims=True)
        idx = jnp.min(jnp.where(l == m, lane, float(LANE)), axis=-1, keepdims=True)
        tops.append(m)
        idxs.append(idx)
        l = jnp.where(lane == idx, -jnp.inf, l)
    ex = [jnp.exp(t - tops[0]) for t in tops]
    den = ex[0] + ex[1] + ex[2] + ex[3]
    e_out = jnp.zeros(logits.shape, f32)
    g_out = jnp.zeros(logits.shape, f32)
    for k in range(TOP_K):
        e_out = jnp.where(lane == k, idxs[k], e_out)
        g_out = jnp.where(lane == k, ex[k] / den, g_out)
    e_ref[...] = e_out.astype(jnp.int32)
    gt_ref[...] = g_out


def _group_of_tile(i):
    per_b = DEC_SEQ // TM_TOK
    n_ctx = T_CTX // TM_TOK
    return jnp.where(i < n_ctx, 0, 1 + (i - n_ctx) // per_b)


def _router(x, g, shift, scale, wr, br):
    tm = TM_TOK
    row = lambda i: (i, 0)
    const = lambda i: (0, 0)
    grp = lambda i: (_group_of_tile(i), 0, 0)
    return pl.pallas_call(
        _router_kernel,
        grid=(T_ALL // tm,),
        in_specs=[pl.BlockSpec((tm, D_MODEL), row),
                  pl.BlockSpec((1, D_MODEL), const),
                  pl.BlockSpec((1, 1, D_MODEL), grp),
                  pl.BlockSpec((1, 1, D_MODEL), grp),
                  pl.BlockSpec((D_MODEL, LANE), const),
                  pl.BlockSpec((1, LANE), const)],
        out_specs=[pl.BlockSpec((tm * ROW_TILE, LANE), row), pl.BlockSpec((tm, LANE), row),
                   pl.BlockSpec((tm, LANE), row)],
        out_shape=[jax.ShapeDtypeStruct((T_ALL * ROW_TILE, LANE), f32),
                   jax.ShapeDtypeStruct((T_ALL, LANE), jnp.int32),
                   jax.ShapeDtypeStruct((T_ALL, LANE), f32)],
        compiler_params=_cparams("arbitrary"),
        name="router",
    )(x, g, shift, scale, wr, br)


def _moe_kernel(be_ref, nu_ref, tok_ref, dst_ref, h_hbm, wgu_ref, bgu_ref, wd_ref, bd_ref, y_hbm,
                x0, x1, y0, y1, wgu_bf, wd_bf, gsem, ssem):
    tm = TM_MOE
    i = pl.program_id(0)
    nb = pl.num_programs(0)
    used = i < nu_ref[0]
    xb, yb = (x0, x1), (y0, y1)

    def tile(row):
        return pl.ds(pl.multiple_of(row * ROW_TILE, ROW_TILE), ROW_TILE)

    def gather_desc(src_row, buf, r, s):
        return pltpu.make_async_copy(h_hbm.at[tile(src_row)], buf.at[tile(r)], gsem.at[s])

    def scatter_desc(buf, r, dst_row, s):
        return pltpu.make_async_copy(buf.at[tile(r)], y_hbm.at[tile(dst_row)], ssem.at[s])

    def gather_wait(s):
        pltpu.make_async_copy(h_hbm.at[pl.ds(0, tm * ROW_TILE)], xb[s], gsem.at[s]).wait()

    def scatter_wait(s):
        pltpu.make_async_copy(yb[s], y_hbm.at[pl.ds(0, tm * ROW_TILE)], ssem.at[s]).wait()

    def issue(blk_gather, s_gather, blk_scatter, s_scatter, unrolled):
        def one(r, prio):
            scatter_desc(yb[s_scatter], r, dst_ref[(blk_scatter + 1) * tm + r], s_scatter).start(priority=prio)
            gather_desc(tok_ref[blk_gather * tm + r], xb[s_gather], r, s_gather).start(priority=1 - prio)
        if unrolled:
            for r in range(tm):
                one(r, r % 2)
        else:
            def body(r, carry):
                one(r, 0)
                return carry
            lax.fori_loop(0, tm, body, 0, unroll=8)

    @pl.when(i == 0)
    def _():
        for s in range(2):
            yb[s][...] = jnp.zeros_like(yb[s])
            dummy = pltpu.make_async_copy(yb[s], y_hbm.at[pl.ds((N_ASSIGN + s * tm) * ROW_TILE, tm * ROW_TILE)],
                                          ssem.at[s])
            dummy.start()
            dummy.wait()

        def body(r, carry):
            gather_desc(tok_ref[r], xb[0], r, 0).start()
            return carry
        lax.fori_loop(0, tm, body, 0, unroll=8)

    first = jnp.logical_or(i == 0, be_ref[i] != be_ref[jnp.maximum(i - 1, 0)])

    @pl.when(jnp.logical_and(used, first))
    def _():
        wgu_bf[...] = wgu_ref[0, 0].astype(bf16)
        wd_bf[...] = wd_ref[0, 0].astype(bf16)

    def step(par):
        cur, oth = par, 1 - par
        gather_wait(cur)

        @pl.when(i >= 1)
        def _():
            scatter_wait(cur)

        @pl.when(used)
        def _():
            issue(i + 1, oth, i - 1, oth, unrolled=True)
            gu = _dot(_load_row_tiles(xb[cur]).astype(bf16), wgu_bf[...]) + bgu_ref[0, 0]
            x_glu = jnp.minimum(gu[:, :D_FF], SWIGLU_LIMIT)
            x_lin = jnp.clip(gu[:, D_FF:], -SWIGLU_LIMIT, SWIGLU_LIMIT)
            act = x_glu * jax.nn.sigmoid(SWIGLU_ALPHA * x_glu) * (x_lin + 1.0)
            _store_row_tiles(yb[cur], _dot(act.astype(bf16), wd_bf[...]) + bd_ref[0, 0])

        @pl.when(jnp.logical_and(jnp.logical_not(used), i + 1 < nb))
        def _():
            issue(i + 1, oth, i - 1, oth, unrolled=False)

        @pl.when(i == nb - 1)
        def _():
            def body(r, carry):
                scatter_desc(yb[oth], r, dst_ref[i * tm + r], oth).start()
                return carry
            lax.fori_loop(0, tm, body, 0, unroll=8)
            scatter_wait(oth)

    @pl.when(i % 2 == 0)
    def _():
        step(0)

    @pl.when(i % 2 == 1)
    def _():
        step(1)


def _moe(layer, block_e, n_used, row_tok, row_dst, h, w_gu, b_gu, w_down, b_down):
    tm = TM_MOE
    ex4 = lambda i, be, nu, tok, dst: (layer, be[i], 0, 0)
    return pl.pallas_call(
        _moe_kernel,
        grid_spec=pltpu.PrefetchScalarGridSpec(
            num_scalar_prefetch=4,
            grid=(N_MOE_BLOCKS,),
            in_specs=[pl.BlockSpec(memory_space=pl.ANY),
                      pl.BlockSpec((1, 1, D_MODEL, 2 * D_FF), ex4),
                      pl.BlockSpec((1, 1, 1, 2 * D_FF), ex4),
                      pl.BlockSpec((1, 1, D_FF, D_MODEL), ex4),
                      pl.BlockSpec((1, 1, 1, D_MODEL), ex4)],
            out_specs=pl.BlockSpec(memory_space=pl.ANY),
            scratch_shapes=[pltpu.VMEM((tm * ROW_TILE, LANE), f32), pltpu.VMEM((tm * ROW_TILE, LANE), f32),
                            pltpu.VMEM((tm * ROW_TILE, LANE), f32), pltpu.VMEM((tm * ROW_TILE, LANE), f32),
                            pltpu.VMEM((D_MODEL, 2 * D_FF), bf16), pltpu.VMEM((D_FF, D_MODEL), bf16),
                            pltpu.SemaphoreType.DMA((2,)), pltpu.SemaphoreType.DMA((2,))]),
        out_shape=jax.ShapeDtypeStruct(((N_ASSIGN + 2 * tm) * ROW_TILE, LANE), f32),
        compiler_params=_cparams("arbitrary"),
        name="moe",
    )(block_e, n_used, row_tok, row_dst, h, w_gu, b_gu.reshape(DEPTH, N_EXPERTS, 1, 2 * D_FF), w_down,
      b_down.reshape(DEPTH, N_EXPERTS, 1, D_MODEL))


def _combine_kernel(final, x_ref, y0_ref, y1_ref, y2_ref, y3_ref, gt_ref, gate_ref, gf_ref, o_ref):
    gt = gt_ref[...]
    f = gt[:, 0:1] * _load_row_tiles(y0_ref)
    for k, y_ref in ((1, y1_ref), (2, y2_ref), (3, y3_ref)):
        f = f + gt[:, k:k + 1] * _load_row_tiles(y_ref)
    out = x_ref[...] + gate_ref[0] * f
    if final:
        out = _rms(out, gf_ref[...])
    o_ref[...] = out


def _combine(final, x, y, gates, gate, g_final):
    tm = TM_TOK
    nt = T_ALL // tm
    row = lambda i: (i, 0)
    const = lambda i: (0, 0)
    grp = lambda i: (_group_of_tile(i), 0, 0)
    ysel = [pl.BlockSpec((tm * ROW_TILE, LANE), functools.partial(lambda k, i: (k * nt + i, 0), k))
            for k in range(TOP_K)]
    return pl.pallas_call(
        functools.partial(_combine_kernel, final),
        grid=(nt,),
        in_specs=[pl.BlockSpec((tm, D_MODEL), row)] + ysel +
                 [pl.BlockSpec((tm, LANE), row),
                  pl.BlockSpec((1, 1, D_MODEL), grp),
                  pl.BlockSpec((1, D_MODEL), const)],
        out_specs=pl.BlockSpec((tm, D_MODEL), row),
        out_shape=jax.ShapeDtypeStruct((T_ALL, D_MODEL), f32),
        compiler_params=_cparams("arbitrary"),
        name="combine",
    )(x, y, y, y, y, gates, gate, g_final)


def _rope_head_tables(d):
    nf = d // 4
    half = d // 2
    t = np.arange(DEC_SEQ)
    inv = ROPE_BASE ** (-np.arange(nf, dtype=np.float32) / nf)
    i = np.arange(d)
    pos = np.where(i[None, :] < half, (t // GRID_W)[:, None], (t % GRID_W)[:, None]).astype(np.float32)
    ang = pos * inv[i % nf][None, :].astype(np.float32)
    first = (i % half) < nf
    cos = np.cos(ang)
    sin = np.where(first[None, :], -np.sin(ang), np.sin(ang))
    partner = np.where(first, i + nf, i - nf)
    return cos.astype(np.float32), sin.astype(np.float32), partner


def _rope_tables():
    cos64, sin64, _ = _rope_head_tables(HEAD_DIM)
    cos32, sin32, _ = _rope_head_tables(QK_ROPE)
    cosa = np.tile(cos64, (1, H_A))
    sina = np.tile(sin64, (1, H_A))
    cosq1 = np.concatenate([np.ones((DEC_SEQ, QK_NOPE), np.float32), cos32,
                            np.ones((DEC_SEQ, QC_PAD - QK_NOPE - QK_ROPE), np.float32)], axis=1)
    sinq1 = np.concatenate([np.zeros((DEC_SEQ, QK_NOPE), np.float32), sin32,
                            np.zeros((DEC_SEQ, QC_PAD - QK_NOPE - QK_ROPE), np.float32)], axis=1)
    cosq = np.tile(cosq1, (1, H_C))
    sinq = np.tile(sinq1, (1, H_C))
    return tuple(jnp.asarray(a) for a in (cosa, sina, cosq, sinq, cos32, sin32))


def _pad_cols(w, n):
    return jnp.pad(w, ((0, 0), (0, n - w.shape[1])))


def _layer_weights(w_in, w_uq):
    cuts = np.cumsum((W_QA, W_KA, W_VA, W_B, W_B, W_B, Q_LORA, KV_LORA, QK_ROPE))[:-1]
    qa, ka, va, qb, kb, vb, cq, ckv, kr = jnp.split(w_in, [int(c) for c in cuts], axis=1)
    _, _, p64 = _rope_head_tables(HEAD_DIM)
    _, _, p32 = _rope_head_tables(QK_ROPE)
    pa = np.concatenate([h * HEAD_DIM + p64 for h in range(H_A)])
    base = jnp.concatenate([qa, ka, va, _pad_cols(qb, 384), _pad_cols(kb, 384), _pad_cols(vb, 384), cq, ckv,
                            _pad_cols(kr, 128)], axis=1)
    w_ctx = base.astype(bf16)
    w_lat = jnp.concatenate([base, qa[:, pa], ka[:, pa[:W_KA]], _pad_cols(kr[:, p32], 128)], axis=1).astype(bf16)
    hq = QK_NOPE + QK_ROPE
    heads = [_pad_cols(w_uq[:, h * hq:(h + 1) * hq], QC_PAD) for h in range(H_C)]
    pq = np.concatenate([np.arange(QK_NOPE), QK_NOPE + p32])
    heads_p = [_pad_cols(w_uq[:, h * hq:(h + 1) * hq][:, pq], QC_PAD) for h in range(H_C)]
    wuq = jnp.concatenate(heads, axis=1).astype(bf16)
    wuq2 = jnp.concatenate(heads + heads_p, axis=1).astype(bf16)
    return w_ctx, w_lat, wuq, wuq2


def _bias_table(rpb):
    r = np.arange(ROWS)
    key_rows = np.clip(r - NA_ROWS // 2, 0, ROWS - NA_ROWS)[:, None] + np.arange(NA_ROWS)[None, :]
    col = np.arange(GRID_W)
    col_start = np.clip(col - NA_COLS // 2, 0, GRID_W - NA_COLS)
    col_ok = (col[None, :] >= col_start[:, None]) & (col[None, :] < col_start[:, None] + NA_COLS)
    dr = key_rows - r[:, None] + (NA_ROWS - 1)
    dc = np.clip(col[None, :] - col[:, None] + (NA_COLS - 1), 0, 2 * NA_COLS - 2)
    bias = rpb[:, dr][:, :, :, dc]
    bias = jnp.where(col_ok[None, None, None], bias, NEG)
    bias = jnp.transpose(bias, (1, 0, 3, 2, 4))
    return bias.reshape(ROWS, H_B, GRID_W, NA_ROWS * GRID_W).astype(f32)


def _routing(top_e):
    tm = TM_MOE
    flat_e = top_e.T.reshape(N_ASSIGN)
    order = jnp.argsort(flat_e).astype(jnp.int32)
    experts = jnp.arange(N_EXPERTS, dtype=jnp.int32)
    counts = jnp.sum((flat_e[:, None] == experts[None, :]).astype(jnp.int32), axis=0)
    nblk = (counts + tm - 1) // tm
    blk_end = jnp.cumsum(nblk)
    blk_start = blk_end - nblk
    grp_start = jnp.cumsum(counts) - counts
    blocks = jnp.arange(N_MOE_BLOCKS, dtype=jnp.int32)
    block_e = jnp.minimum(jnp.sum((blk_end[None, :] <= blocks[:, None]).astype(jnp.int32), axis=1), N_EXPERTS - 1)
    n_used = blk_end[-1].astype(jnp.int32).reshape(1)
    sel = (block_e[:, None] == experts[None, :]).astype(jnp.int32)
    b_first = jnp.sum(sel * blk_start[None, :], axis=1)
    b_count = jnp.sum(sel * counts[None, :], axis=1)
    b_grp = jnp.sum(sel * grp_start[None, :], axis=1)
    r = jnp.arange(tm, dtype=jnp.int32)[None, :]
    off = (blocks - b_first)[:, None] * tm + r
    valid = (off < b_count[:, None]) & (blocks[:, None] < n_used[0])
    asg = order[jnp.clip(b_grp[:, None] + off, 0, N_ASSIGN - 1)]
    row_tok = jnp.where(valid, asg % T_ALL, 0).reshape(-1)
    row_dst = jnp.where(valid, asg, N_ASSIGN + (blocks[:, None] % 2) * tm + r)
    row_dst = jnp.concatenate([N_ASSIGN + tm + r, row_dst], axis=0).reshape(-1)
    return block_e.astype(jnp.int32), n_used, row_tok.astype(jnp.int32), row_dst.astype(jnp.int32)


def kernel(x_prompt, x_sample, cache_a_k, cache_a_v, cache_b_k, cache_b_v, cache_c_kv, cache_c_kr, c, c_ctx, w_ada, b_ada, g_attn, g_ffn, w_in, sink_a, rpb_b, g_cq, g_ckv, w_uq, w_ukv, w_out, w_router, b_router, w_gu, b_gu, w_down, b_down, g_final):
    x = jnp.concatenate([x_prompt.reshape(T_CTX, D_MODEL), x_sample.reshape(T_LAT, D_MODEL)], axis=0)
    cvec = jnp.concatenate([c_ctx[None, :], c, jnp.zeros((8 - N_GROUPS, D_MODEL), f32)], axis=0)
    mods = _ada(cvec, w_ada, b_ada)[:, :N_GROUPS].reshape(DEPTH, N_GROUPS, 6, 1, D_MODEL)
    tabs = _rope_tables()
    caches = (cache_a_k.reshape(DEC_BATCH, DEPTH, PAST_LEN, W_KA), cache_a_v.reshape(DEC_BATCH, DEPTH, PAST_LEN, W_VA),
              cache_b_k.reshape(DEC_BATCH, DEPTH, PAST_LEN, W_B), cache_b_v.reshape(DEC_BATCH, DEPTH, PAST_LEN, W_B),
              cache_c_kv, cache_c_kr)
    new = [[] for _ in range(6)]
    for layer in range(DEPTH):
        m = [mods[layer, :, j] for j in range(6)]
        w_ctx, w_lat, wuq, wuq2 = _layer_weights(w_in[layer], w_uq[layer])
        wukv = w_ukv[layer].astype(bf16)
        wout = w_out[layer].astype(bf16)
        g1 = g_attn[layer][None, :]
        gcq = g_cq[layer][None, :]
        gckv = g_ckv[layer][None, :]
        sink = sink_a[layer]

        pc = _inproj_ctx(x, g1, m[0], m[1], w_ctx, gcq, gckv, wuq)
        for lst, a in zip(new, (pc[1], pc[2], pc[4], pc[5], pc[7], pc[8])):
            lst.append(a)
        x_ctx = _ctx_attn(sink, pc, wukv, wout, x, m[2])

        plat = _inproj_lat(x, g1, m[0], m[1], w_lat, gcq, gckv, wuq2, tabs)
        x_lat = _lat_attn(layer, sink, plat, caches, _bias_table(rpb_b[layer]), wukv, wout, x, m[2])
        x = jnp.concatenate([x_ctx, x_lat], axis=0)

        wr = _pad_cols(w_router[layer], LANE)
        br = _pad_cols(b_router[layer][None, :], LANE)
        h2, top_e, gates = _router(x, g_ffn[layer][None, :], m[3], m[4], wr, br)
        block_e, n_used, row_tok, row_dst = _routing(top_e[:, :TOP_K])
        y = _moe(layer, block_e, n_used, row_tok, row_dst, h2, w_gu, b_gu, w_down, b_down)
        x = _combine(layer == DEPTH - 1, x, y, gates, m[5], g_final[None, :])

    y_prompt = x[:T_CTX].reshape(BATCH, SEQ, D_MODEL)
    y_sample = x[T_CTX:].reshape(DEC_BATCH, DEC_SEQ, D_MODEL)
    shapes = ((KV_A, HEAD_DIM), (KV_A, HEAD_DIM), (H_B, HEAD_DIM), (H_B, HEAD_DIM), (KV_LORA,), (QK_ROPE,))
    outs = [jnp.stack([a.reshape((BATCH, SEQ) + s) for a in lst], axis=1) for lst, s in zip(new, shapes)]
    return (y_prompt, y_sample, *outs)
```

```python
import functools

import numpy as np
import jax
import jax.numpy as jnp
from jax import lax
from jax.experimental import pallas as pl
from jax.experimental.pallas import tpu as pltpu

D_MODEL = 1024
BATCH = 32
SEQ = 256
DEPTH = 2
DEC_BATCH = 2
DEC_SEQ = 1024
PAST_LEN = 512
GRID_W = 64
HEAD_DIM = 64
H_A = 6
KV_A = 2
G_A = H_A // KV_A
WINDOW = 128
BLOCK = 128
H_B = 5
NA_ROWS = 8
NA_COLS = 16
H_C = 5
Q_LORA = 384
KV_LORA = 256
QK_NOPE = 64
QK_ROPE = 32
V_C = 64
N_EXPERTS = 32
TOP_K = 4
D_FF = 1024
SWIGLU_ALPHA = 1.702
SWIGLU_LIMIT = 7.0
ROPE_BASE = 10000.0
EPS = 1e-6
NEG = -1e30

T_CTX = BATCH * SEQ
T_LAT = DEC_BATCH * DEC_SEQ
T_ALL = T_CTX + T_LAT
N_GROUPS = 1 + DEC_BATCH
LANE = 128
QC_PAD = 128
ROWS = DEC_SEQ // GRID_W

W_QA, W_KA, W_VA = H_A * HEAD_DIM, KV_A * HEAD_DIM, KV_A * HEAD_DIM
W_B = H_B * HEAD_DIM
OFF_QA = 0
OFF_KA = 384
OFF_VA = 512
OFF_QB = 640
OFF_KB = 1024
OFF_VB = 1408
OFF_CQ = 1792
OFF_CKV = 2176
OFF_KR = 2432
NW_CTX = 2560
OFF_QA_P = 2560
OFF_KA_P = 2944
OFF_KR_P = 3072
NW_LAT = 3200

TM_TOK = 256
TM_LAT_IN = 512
TM_MOE = 256
N_ASSIGN = T_ALL * TOP_K
N_MOE_BLOCKS = N_ASSIGN // TM_MOE + N_EXPERTS
VMEM_LIMIT = 56 * 1024 * 1024

f32 = jnp.float32
bf16 = jnp.bfloat16


def _cparams(*sem):
    return pltpu.CompilerParams(dimension_semantics=sem, vmem_limit_bytes=VMEM_LIMIT)


def _rms(xf, g):
    return xf * lax.rsqrt(jnp.mean(xf * xf, axis=-1, keepdims=True) + EPS) * g


def _dot(a, b):
    return jnp.dot(a, b, preferred_element_type=f32)


def _dot_nt(a, b):
    return lax.dot_general(a, b, (((1,), (1,)), ((), ())), preferred_element_type=f32)


ROW_TILE = D_MODEL // LANE


def _store_row_tiles(ref, val):
    n = val.shape[0]
    for c in range(ROW_TILE):
        ref[pl.ds(c, n, stride=ROW_TILE), :] = val[:, c * LANE:(c + 1) * LANE]


def _load_row_tiles(ref):
    n = ref.shape[0] // ROW_TILE
    return jnp.concatenate([ref[pl.ds(c, n, stride=ROW_TILE), :] for c in range(ROW_TILE)], axis=1)


def _softmax_pv(s, v, sink=None):
    m = jnp.max(s, axis=-1, keepdims=True)
    if sink is not None:
        m = jnp.maximum(m, sink)
    p = jnp.exp(s - m)
    l = jnp.sum(p, axis=-1, keepdims=True)
    if sink is not None:
        l = l + jnp.exp(sink - m)
    return _dot(p.astype(bf16), v) / l


def _ada_kernel(c_ref, w_ref, b_ref, o_ref):
    c = c_ref[...]
    s = c * jax.nn.sigmoid(c)
    o_ref[0] = jnp.dot(s, w_ref[0], preferred_element_type=f32, precision=lax.Precision.HIGHEST) + b_ref[0]


def _ada(cvec, w_ada, b_ada):
    tn = 1536
    return pl.pallas_call(
        _ada_kernel,
        grid=(DEPTH, 6 * D_MODEL // tn),
        in_specs=[pl.BlockSpec((8, D_MODEL), lambda l, j: (0, 0)),
                  pl.BlockSpec((1, D_MODEL, tn), lambda l, j: (l, 0, j)),
                  pl.BlockSpec((1, 1, tn), lambda l, j: (l, 0, j))],
        out_specs=pl.BlockSpec((1, 8, tn), lambda l, j: (l, 0, j)),
        out_shape=jax.ShapeDtypeStruct((DEPTH, 8, 6 * D_MODEL), f32),
        compiler_params=_cparams("arbitrary", "arbitrary"),
        name="ada",
    )(cvec, w_ada, b_ada.reshape(DEPTH, 1, 6 * D_MODEL))


def _inproj_ctx_kernel(x_ref, g_ref, sh_ref, sc_ref, w_ref, gcq_ref, gckv_ref, wuq_ref,
                       qa_ref, ka_ref, va_ref, qb_ref, kb_ref, vb_ref, qc_ref, ckv_ref, kr_ref):
    h = _rms(x_ref[...], g_ref[...]) * (1.0 + sc_ref[0]) + sh_ref[0]
    p = _dot(h.astype(bf16), w_ref[...])
    qa_ref[...] = p[:, OFF_QA:OFF_QA + W_QA].astype(bf16)
    ka_ref[...] = p[:, OFF_KA:OFF_KA + W_KA]
    va_ref[...] = p[:, OFF_VA:OFF_VA + W_VA]
    qb_ref[...] = p[:, OFF_QB:OFF_QB + W_B].astype(bf16)
    kb_ref[...] = p[:, OFF_KB:OFF_KB + W_B]
    vb_ref[...] = p[:, OFF_VB:OFF_VB + W_B]
    cqn = _rms(p[:, OFF_CQ:OFF_CQ + Q_LORA], gcq_ref[...])
    qc_ref[...] = _dot(cqn.astype(bf16), wuq_ref[...]).astype(bf16)
    ckv_ref[...] = _rms(p[:, OFF_CKV:OFF_CKV + KV_LORA], gckv_ref[...])
    kr_ref[...] = p[:, OFF_KR:OFF_KR + QK_ROPE]


def _inproj_ctx(x, g, shift, scale, w, gcq, gckv, wuq):
    tm = TM_TOK
    row = lambda i: (i, 0)
    const = lambda i: (0, 0)
    widths = (W_QA, W_KA, W_VA, W_B, W_B, W_B, H_C * QC_PAD, KV_LORA, QK_ROPE)
    dtypes = (bf16, f32, f32, bf16, f32, f32, bf16, f32, f32)
    return pl.pallas_call(
        _inproj_ctx_kernel,
        grid=(T_CTX // tm,),
        in_specs=[pl.BlockSpec((tm, D_MODEL), row),
                  pl.BlockSpec((1, D_MODEL), const),
                  pl.BlockSpec((1, 1, D_MODEL), lambda i: (0, 0, 0)),
                  pl.BlockSpec((1, 1, D_MODEL), lambda i: (0, 0, 0)),
                  pl.BlockSpec((D_MODEL, NW_CTX), const),
                  pl.BlockSpec((1, Q_LORA), const),
                  pl.BlockSpec((1, KV_LORA), const),
                  pl.BlockSpec((Q_LORA, H_C * QC_PAD), const)],
        out_specs=[pl.BlockSpec((tm, wd), row) for wd in widths],
        out_shape=[jax.ShapeDtypeStruct((T_CTX, wd), dt) for wd, dt in zip(widths, dtypes)],
        compiler_params=_cparams("arbitrary"),
        name="inproj_ctx",
    )(x, g, shift, scale, w, gcq, gckv, wuq)


def _inproj_lat_kernel(x_ref, g_ref, sh_ref, sc_ref, w_ref, gcq_ref, gckv_ref, wuq_ref,
                       cosa_ref, sina_ref, cosq_ref, sinq_ref, cosr_ref, sinr_ref,
                       qa_ref, ka_ref, va_ref, qb_ref, kb_ref, vb_ref, qc_ref, ckv_ref, kr_ref):
    h = _rms(x_ref[...], g_ref[...]) * (1.0 + sc_ref[0]) + sh_ref[0]
    p = _dot(h.astype(bf16), w_ref[...])
    cosa = cosa_ref[...]
    sina = sina_ref[...]
    qa = p[:, OFF_QA:OFF_QA + W_QA] * cosa + p[:, OFF_QA_P:OFF_QA_P + W_QA] * sina
    ka = p[:, OFF_KA:OFF_KA + W_KA] * cosa[:, :W_KA] + p[:, OFF_KA_P:OFF_KA_P + W_KA] * sina[:, :W_KA]
    kr = p[:, OFF_KR:OFF_KR + QK_ROPE] * cosr_ref[...] + p[:, OFF_KR_P:OFF_KR_P + QK_ROPE] * sinr_ref[...]
    qa_ref[...] = qa.astype(bf16)
    ka_ref[...] = ka.astype(bf16)
    va_ref[...] = p[:, OFF_VA:OFF_VA + W_VA].astype(bf16)
    qb_ref[...] = p[:, OFF_QB:OFF_QB + W_B].astype(bf16)
    kb_ref[...] = p[:, OFF_KB:OFF_KB + W_B].astype(bf16)
    vb_ref[...] = p[:, OFF_VB:OFF_VB + W_B].astype(bf16)
    cqn = _rms(p[:, OFF_CQ:OFF_CQ + Q_LORA], gcq_ref[...])
    q2 = _dot(cqn.astype(bf16), wuq_ref[...])
    nq = H_C * QC_PAD
    qc_ref[...] = (q2[:, :nq] * cosq_ref[...] + q2[:, nq:] * sinq_ref[...]).astype(bf16)
    ckv_ref[...] = _rms(p[:, OFF_CKV:OFF_CKV + KV_LORA], gckv_ref[...]).astype(bf16)
    kr_ref[...] = kr.astype(bf16)


def _inproj_lat(x, g, shift, scale, w, gcq, gckv, wuq2, tabs):
    tm = TM_LAT_IN
    per_b = DEC_SEQ // tm
    row0 = T_CTX // tm
    xrow = lambda i: (row0 + i, 0)
    row = lambda i: (i, 0)
    const = lambda i: (0, 0)
    grp = lambda i: (1 + i // per_b, 0, 0)
    pos = lambda i: (i % per_b, 0)
    cosa, sina, cosq, sinq, cosr, sinr = tabs
    widths = (W_QA, W_KA, W_VA, W_B, W_B, W_B, H_C * QC_PAD, KV_LORA, QK_ROPE)
    return pl.pallas_call(
        _inproj_lat_kernel,
        grid=(T_LAT // tm,),
        in_specs=[pl.BlockSpec((tm, D_MODEL), xrow),
                  pl.BlockSpec((1, D_MODEL), const),
                  pl.BlockSpec((1, 1, D_MODEL), grp),
                  pl.BlockSpec((1, 1, D_MODEL), grp),
                  pl.BlockSpec((D_MODEL, NW_LAT), const),
                  pl.BlockSpec((1, Q_LORA), const),
                  pl.BlockSpec((1, KV_LORA), const),
                  pl.BlockSpec((Q_LORA, 2 * H_C * QC_PAD), const),
                  pl.BlockSpec((tm, W_QA), pos), pl.BlockSpec((tm, W_QA), pos),
                  pl.BlockSpec((tm, H_C * QC_PAD), pos), pl.BlockSpec((tm, H_C * QC_PAD), pos),
                  pl.BlockSpec((tm, QK_ROPE), pos), pl.BlockSpec((tm, QK_ROPE), pos)],
        out_specs=[pl.BlockSpec((tm, wd), row) for wd in widths],
        out_shape=[jax.ShapeDtypeStruct((T_LAT, wd), bf16) for wd in widths],
        compiler_params=_cparams("arbitrary"),
        name="inproj_lat",
    )(x, g, shift, scale, w, gcq, gckv, wuq2, cosa, sina, cosq, sinq, cosr, sinr)


def _ctx_attn_kernel(sink_ref, qa_ref, ka_ref, va_ref, qb_ref, kb_ref, vb_ref, qc_ref, ckv_ref, kr_ref,
                     wukv_ref, wout_ref, x_ref, gate_ref, o_ref, o_scr):
    scale = HEAD_DIM ** -0.5
    ka = ka_ref[...].astype(bf16)
    va = va_ref[...].astype(bf16)
    for h in range(H_A):
        g = h // G_A
        q = qa_ref[:, h * HEAD_DIM:(h + 1) * HEAD_DIM]
        s = _dot_nt(q, ka[:, g * HEAD_DIM:(g + 1) * HEAD_DIM]) * scale
        o_scr[:, h * HEAD_DIM:(h + 1) * HEAD_DIM] = _softmax_pv(s, va[:, g * HEAD_DIM:(g + 1) * HEAD_DIM],
                                                               sink_ref[h])
    kb = kb_ref[...].astype(bf16)
    vb = vb_ref[...].astype(bf16)
    for h in range(H_B):
        sl = slice(h * HEAD_DIM, (h + 1) * HEAD_DIM)
        s = _dot_nt(qb_ref[:, sl], kb[:, sl]) * scale
        o_scr[:, W_QA + h * HEAD_DIM:W_QA + (h + 1) * HEAD_DIM] = _softmax_pv(s, vb[:, sl])
    kv = _dot(ckv_ref[...].astype(bf16), wukv_ref[...]).astype(bf16)
    kr = kr_ref[...].astype(bf16)
    scale_c = (QK_NOPE + QK_ROPE) ** -0.5
    for h in range(H_C):
        qn = qc_ref[:, h * QC_PAD:h * QC_PAD + QK_NOPE]
        qr = qc_ref[:, h * QC_PAD + QK_NOPE:h * QC_PAD + QK_NOPE + QK_ROPE]
        c0 = h * (QK_NOPE + V_C)
        s = (_dot_nt(qn, kv[:, c0:c0 + QK_NOPE]) + _dot_nt(qr, kr)) * scale_c
        off = W_QA + W_B + h * V_C
        o_scr[:, off:off + V_C] = _softmax_pv(s, kv[:, c0 + QK_NOPE:c0 + QK_NOPE + V_C])
    y = _dot(o_scr[...].astype(bf16), wout_ref[...])
    o_ref[...] = x_ref[...] + gate_ref[0] * y


def _ctx_attn(sink, proj, wukv, wout, x, gate):
    qa, ka, va, qb, kb, vb, qc, ckv, kr = proj
    row = lambda b: (b, 0)
    const = lambda b: (0, 0)
    in_specs = [pl.BlockSpec(memory_space=pltpu.SMEM)]
    in_specs += [pl.BlockSpec((SEQ, a.shape[1]), row) for a in proj]
    in_specs += [pl.BlockSpec((KV_LORA, H_C * (QK_NOPE + V_C)), const),
                 pl.BlockSpec((D_MODEL, D_MODEL), const),
                 pl.BlockSpec((SEQ, D_MODEL), row),
                 pl.BlockSpec((1, 1, D_MODEL), lambda b: (0, 0, 0))]
    return pl.pallas_call(
        _ctx_attn_kernel,
        grid=(BATCH,),
        in_specs=in_specs,
        out_specs=pl.BlockSpec((SEQ, D_MODEL), row),
        out_shape=jax.ShapeDtypeStruct((T_CTX, D_MODEL), f32),
        scratch_shapes=[pltpu.VMEM((SEQ, D_MODEL), f32)],
        compiler_params=_cparams("arbitrary"),
        name="ctx_attn",
    )(sink, qa, ka, va, qb, kb, vb, qc, ckv, kr, wukv, wout, x, gate)


def _lat_attn_kernel(sink_ref, qa_ref, qb_ref, qc_ref, ka_ref, va_ref, kb_ref, vb_ref, ckv_ref, kr_ref,
                     cak_ref, cav_ref, cbk_ref, cbv_ref, cckv_ref, ckr_ref, bias_ref,
                     wukv_ref, wout_ref, x_ref, gate_ref, o_ref, o_scr, kv_scr):
    qi = pl.program_id(1)
    nb = DEC_SEQ // BLOCK
    scale = HEAD_DIM ** -0.5

    @pl.when(qi == 0)
    def _():
        kv_scr[0:DEC_SEQ, :] = _dot(ckv_ref[...], wukv_ref[...]).astype(bf16)
        kv_scr[DEC_SEQ:DEC_SEQ + PAST_LEN, :] = _dot(cckv_ref[0, 0].astype(bf16), wukv_ref[...]).astype(bf16)

    def blk(ref, j):
        idx = jnp.clip(qi + j, 0, nb - 1)
        return ref[pl.ds(pl.multiple_of(idx * BLOCK, BLOCK), BLOCK), :]

    ka = jnp.concatenate([blk(ka_ref, -1), blk(ka_ref, 0), blk(ka_ref, 1), cak_ref[0, 0].astype(bf16)], axis=0)
    va = jnp.concatenate([blk(va_ref, -1), blk(va_ref, 0), blk(va_ref, 1), cav_ref[0, 0].astype(bf16)], axis=0)
    nk_a = 3 * BLOCK + PAST_LEN
    r = lax.broadcasted_iota(jnp.int32, (BLOCK, nk_a), 0)
    c = lax.broadcasted_iota(jnp.int32, (BLOCK, nk_a), 1)
    valid = (((c < BLOCK) & (c >= r) & (qi > 0))
             | ((c >= BLOCK) & (c < 2 * BLOCK))
             | ((c >= 2 * BLOCK) & (c < 3 * BLOCK) & (c - 2 * BLOCK <= r) & (qi < nb - 1))
             | (c >= 3 * BLOCK))
    for h in range(H_A):
        g = h // G_A
        q = qa_ref[:, h * HEAD_DIM:(h + 1) * HEAD_DIM]
        s = _dot_nt(q, ka[:, g * HEAD_DIM:(g + 1) * HEAD_DIM]) * scale
        s = jnp.where(valid, s, NEG)
        o_scr[:, h * HEAD_DIM:(h + 1) * HEAD_DIM] = _softmax_pv(s, va[:, g * HEAD_DIM:(g + 1) * HEAD_DIM],
                                                               sink_ref[h])

    cbk = cbk_ref[0, 0].astype(bf16)
    cbv = cbv_ref[0, 0].astype(bf16)
    rows_per_blk = BLOCK // GRID_W
    nloc = NA_ROWS * GRID_W
    for half in range(rows_per_blk):
        grow = qi * rows_per_blk + half
        start = jnp.clip(grow - NA_ROWS // 2, 0, ROWS - NA_ROWS)
        kloc = kb_ref[pl.ds(pl.multiple_of(start * GRID_W, GRID_W), nloc), :]
        vloc = vb_ref[pl.ds(pl.multiple_of(start * GRID_W, GRID_W), nloc), :]
        vcat = jnp.concatenate([vloc, cbv], axis=0)
        qrows = slice(half * GRID_W, (half + 1) * GRID_W)
        for h in range(H_B):
            sl = slice(h * HEAD_DIM, (h + 1) * HEAD_DIM)
            q = qb_ref[qrows, sl]
            s_loc = _dot_nt(q, kloc[:, sl]) * scale + bias_ref[half, h]
            s_ctx = _dot_nt(q, cbk[:, sl]) * scale
            s = jnp.concatenate([s_loc, s_ctx], axis=1)
            o_scr[qrows, W_QA + h * HEAD_DIM:W_QA + (h + 1) * HEAD_DIM] = _softmax_pv(s, vcat[:, sl])

    kr = jnp.concatenate([kr_ref[...], ckr_ref[0, 0].astype(bf16)], axis=0)
    scale_c = (QK_NOPE + QK_ROPE) ** -0.5
    for h in range(H_C):
        qn = qc_ref[:, h * QC_PAD:h * QC_PAD + QK_NOPE]
        qr = qc_ref[:, h * QC_PAD + QK_NOPE:h * QC_PAD + QK_NOPE + QK_ROPE]
        c0 = h * (QK_NOPE + V_C)
        s = (_dot_nt(qn, kv_scr[:, c0:c0 + QK_NOPE]) + _dot_nt(qr, kr)) * scale_c
        off = W_QA + W_B + h * V_C
        o_scr[:, off:off + V_C] = _softmax_pv(s, kv_scr[:, c0 + QK_NOPE:c0 + QK_NOPE + V_C])

    y = _dot(o_scr[...].astype(bf16), wout_ref[...])
    o_ref[...] = x_ref[...] + gate_ref[0] * y


def _lat_attn(layer, sink, proj, caches, bias_tab, wukv, wout, x, gate):
    qa, ka, va, qb, kb, vb, qc, ckv, kr = proj
    nb = DEC_SEQ // BLOCK
    qrow = lambda b, q: (b * nb + q, 0)
    xrow = lambda b, q: (T_CTX // BLOCK + b * nb + q, 0)
    brow = lambda b, q: (b, 0)
    const = lambda b, q: (0, 0)
    cidx = lambda b, q: (b, layer, 0, 0)
    in_specs = [pl.BlockSpec(memory_space=pltpu.SMEM)]
    in_specs += [pl.BlockSpec((BLOCK, a.shape[1]), qrow) for a in (qa, qb, qc)]
    in_specs += [pl.BlockSpec((DEC_SEQ, a.shape[1]), brow) for a in (ka, va, kb, vb, ckv, kr)]
    in_specs += [pl.BlockSpec((1, 1, PAST_LEN, a.shape[3]), cidx) for a in caches]
    in_specs += [pl.BlockSpec((BLOCK // GRID_W, H_B, GRID_W, NA_ROWS * GRID_W), lambda b, q: (q, 0, 0, 0)),
                 pl.BlockSpec((KV_LORA, H_C * (QK_NOPE + V_C)), const),
                 pl.BlockSpec((D_MODEL, D_MODEL), const),
                 pl.BlockSpec((BLOCK, D_MODEL), xrow),
                 pl.BlockSpec((1, 1, D_MODEL), lambda b, q: (1 + b, 0, 0))]
    return pl.pallas_call(
        _lat_attn_kernel,
        grid=(DEC_BATCH, nb),
        in_specs=in_specs,
        out_specs=pl.BlockSpec((BLOCK, D_MODEL), qrow),
        out_shape=jax.ShapeDtypeStruct((T_LAT, D_MODEL), f32),
        scratch_shapes=[pltpu.VMEM((BLOCK, D_MODEL), f32),
                        pltpu.VMEM((DEC_SEQ + PAST_LEN, H_C * (QK_NOPE + V_C)), bf16)],
        compiler_params=_cparams("arbitrary", "arbitrary"),
        name="lat_attn",
    )(sink, qa, qb, qc, ka, va, kb, vb, ckv, kr, *caches, bias_tab, wukv, wout, x, gate)


def _router_kernel(x_ref, g_ref, sh_ref, sc_ref, wr_ref, br_ref, h_ref, e_ref, gt_ref):
    h = _rms(x_ref[...], g_ref[...]) * (1.0 + sc_ref[0]) + sh_ref[0]
    _store_row_tiles(h_ref, h)
    logits = jnp.dot(h, wr_ref[...], preferred_element_type=f32, precision=lax.Precision.HIGHEST) + br_ref[...]
    lane = lax.broadcasted_iota(jnp.int32, logits.shape, 1).astype(f32)
    l = jnp.where(lane < N_EXPERTS, logits, -jnp.inf)
    tops, idxs = [], []
    for _ in range(TOP_K):
        m = jnp.max(l, axis=-1, keepdims=True)
        idx = jnp.min(jnp.where(l == m, lane, float(LANE)), axis=-1, keepdims=True)
        tops.append(m)
        idxs.append(idx)
        l = jnp.where(lane == idx, -jnp.inf, l)
    ex = [jnp.exp(t - tops[0]) for t in tops]
    den = ex[0] + ex[1] + ex[2] + ex[3]
    e_out = jnp.zeros(logits.shape, f32)
    g_out = jnp.zeros(logits.shape, f32)
    for k in range(TOP_K):
        e_out = jnp.where(lane == k, idxs[k], e_out)
        g_out = jnp.where(lane == k, ex[k] / den, g_out)
    e_ref[...] = e_out.astype(jnp.int32)
    gt_ref[...] = g_out


def _group_of_tile(i):
    per_b = DEC_SEQ // TM_TOK
    n_ctx = T_CTX // TM_TOK
    return jnp.where(i < n_ctx, 0, 1 + (i - n_ctx) // per_b)


def _router(x, g, shift, scale, wr, br):
    tm = TM_TOK
    row = lambda i: (i, 0)
    const = lambda i: (0, 0)
    grp = lambda i: (_group_of_tile(i), 0, 0)
    return pl.pallas_call(
        _router_kernel,
        grid=(T_ALL // tm,),
        in_specs=[pl.BlockSpec((tm, D_MODEL), row),
                  pl.BlockSpec((1, D_MODEL), const),
                  pl.BlockSpec((1, 1, D_MODEL), grp),
                  pl.BlockSpec((1, 1, D_MODEL), grp),
                  pl.BlockSpec((D_MODEL, LANE), const),
                  pl.BlockSpec((1, LANE), const)],
        out_specs=[pl.BlockSpec((tm * ROW_TILE, LANE), row), pl.BlockSpec((tm, LANE), row),
                   pl.BlockSpec((tm, LANE), row)],
        out_shape=[jax.ShapeDtypeStruct((T_ALL * ROW_TILE, LANE), f32),
                   jax.ShapeDtypeStruct((T_ALL, LANE), jnp.int32),
                   jax.ShapeDtypeStruct((T_ALL, LANE), f32)],
        compiler_params=_cparams("arbitrary"),
        name="router",
    )(x, g, shift, scale, wr, br)


def _dispatch_kernel(tok_ref, nu_ref, h_hbm, o_ref, hv, xg, hsem):
    tm = TM_MOE
    i = pl.program_id(0)

    @pl.when(i == 0)
    def _():
        resident = pltpu.make_async_copy(h_hbm, hv, hsem.at[0])
        resident.start()
        resident.wait()

    @pl.when(i < nu_ref[0])
    def _():
        for r in range(tm):
            t = tok_ref[i * tm + r]
            xg[pl.ds(r * ROW_TILE, ROW_TILE), :] = hv[pl.ds(pl.multiple_of(t * ROW_TILE, ROW_TILE), ROW_TILE), :]
        o_ref[...] = _load_row_tiles(xg).astype(bf16)

    @pl.when(i >= nu_ref[0])
    def _():
        o_ref[...] = jnp.zeros_like(o_ref)


def _dispatch(row_tok, n_used, h):
    tm = TM_MOE
    return pl.pallas_call(
        _dispatch_kernel,
        grid_spec=pltpu.PrefetchScalarGridSpec(
            num_scalar_prefetch=2,
            grid=(N_MOE_BLOCKS,),
            in_specs=[pl.BlockSpec(memory_space=pl.ANY)],
            out_specs=pl.BlockSpec((tm, D_MODEL), lambda i, tok, nu: (i, 0)),
            scratch_shapes=[pltpu.VMEM((T_ALL * ROW_TILE, LANE), f32), pltpu.VMEM((tm * ROW_TILE, LANE), f32),
                            pltpu.SemaphoreType.DMA((1,))]),
        out_shape=jax.ShapeDtypeStruct((N_MOE_BLOCKS * tm, D_MODEL), bf16),
        compiler_params=_cparams("arbitrary"),
        name="dispatch",
    )(row_tok, n_used, h)


def _moe_kernel(layer, be_ref, nu_ref, nxt_ref, dst_ref, x_ref, wgu_hbm, bgu_ref, wd_hbm, bd_ref, y_hbm,
                y0, y1, wgu_st, wd_st, wgu_bf, wd_bf, wsem, ssem):
    tm = TM_MOE
    i = pl.program_id(0)
    nb = pl.num_programs(0)
    used = i < nu_ref[0]
    yb = (y0, y1)

    def out_tile(row):
        return pl.ds(pl.multiple_of(row * ROW_TILE, ROW_TILE), ROW_TILE)

    def scatter_desc(buf, r, dst_row, s):
        return pltpu.make_async_copy(buf.at[out_tile(r)], y_hbm.at[out_tile(dst_row)], ssem.at[s])

    def scatter_wait(s):
        pltpu.make_async_copy(yb[s], y_hbm.at[pl.ds(0, tm * ROW_TILE)], ssem.at[s]).wait()

    def scatter_start(blk, s, unrolled):
        if unrolled:
            for r in range(tm):
                scatter_desc(yb[s], r, dst_ref[(blk + 1) * tm + r], s).start(priority=r % 2)
        else:
            def body(r, carry):
                scatter_desc(yb[s], r, dst_ref[(blk + 1) * tm + r], s).start()
                return carry
            lax.fori_loop(0, tm, body, 0, unroll=8)

    def weight_copies(e):
        return (pltpu.make_async_copy(wgu_hbm.at[layer, e], wgu_st, wsem.at[0]),
                pltpu.make_async_copy(wd_hbm.at[layer, e], wd_st, wsem.at[1]))

    @pl.when(i == 0)
    def _():
        for s in range(2):
            yb[s][...] = jnp.zeros_like(yb[s])
            dummy = pltpu.make_async_copy(yb[s], y_hbm.at[pl.ds((N_ASSIGN + s * tm) * ROW_TILE, tm * ROW_TILE)],
                                          ssem.at[s])
            dummy.start()
            dummy.wait()
        for cp in weight_copies(be_ref[0]):
            cp.start()

    first = jnp.logical_and(used, jnp.logical_or(i == 0, be_ref[i] != be_ref[jnp.maximum(i - 1, 0)]))

    @pl.when(first)
    def _():
        for cp in weight_copies(0):
            cp.wait()
        wgu_bf[...] = wgu_st[...].astype(bf16)
        wd_bf[...] = wd_st[...].astype(bf16)

        @pl.when(nxt_ref[i] >= 0)
        def _():
            for cp in weight_copies(nxt_ref[i]):
                cp.start()

    def step(par):
        cur, oth = par, 1 - par

        @pl.when(i >= 1)
        def _():
            scatter_wait(cur)

        @pl.when(used)
        def _():
            scatter_start(i - 1, oth, unrolled=True)
            gu = _dot(x_ref[...], wgu_bf[...]) + bgu_ref[0, 0]
            x_glu = jnp.minimum(gu[:, :D_FF], SWIGLU_LIMIT)
            x_lin = jnp.clip(gu[:, D_FF:], -SWIGLU_LIMIT, SWIGLU_LIMIT)
            act = x_glu * jax.nn.sigmoid(SWIGLU_ALPHA * x_glu) * (x_lin + 1.0)
            _store_row_tiles(yb[cur], _dot(act.astype(bf16), wd_bf[...]) + bd_ref[0, 0])

        @pl.when(jnp.logical_and(jnp.logical_not(used), i + 1 < nb))
        def _():
            scatter_start(i - 1, oth, unrolled=False)

        @pl.when(i == nb - 1)
        def _():
            scatter_start(i - 1, oth, unrolled=False)
            scatter_wait(oth)

    @pl.when(i % 2 == 0)
    def _():
        step(0)

    @pl.when(i % 2 == 1)
    def _():
        step(1)


def _moe(layer, routing, h, w_gu, b_gu, w_down, b_down):
    tm = TM_MOE
    block_e, n_used, nxt_e, row_tok, row_dst = routing
    xs = _dispatch(row_tok, n_used, h)
    ex4 = lambda i, be, nu, nxt, dst: (layer, be[i], 0, 0)
    return pl.pallas_call(
        functools.partial(_moe_kernel, layer),
        grid_spec=pltpu.PrefetchScalarGridSpec(
            num_scalar_prefetch=4,
            grid=(N_MOE_BLOCKS,),
            in_specs=[pl.BlockSpec((tm, D_MODEL), lambda i, be, nu, nxt, dst: (i, 0)),
                      pl.BlockSpec(memory_space=pl.ANY),
                      pl.BlockSpec((1, 1, 1, 2 * D_FF), ex4),
                      pl.BlockSpec(memory_space=pl.ANY),
                      pl.BlockSpec((1, 1, 1, D_MODEL), ex4)],
            out_specs=pl.BlockSpec(memory_space=pl.ANY),
            scratch_shapes=[pltpu.VMEM((tm * ROW_TILE, LANE), f32), pltpu.VMEM((tm * ROW_TILE, LANE), f32),
                            pltpu.VMEM((D_MODEL, 2 * D_FF), f32), pltpu.VMEM((D_FF, D_MODEL), f32),
                            pltpu.VMEM((D_MODEL, 2 * D_FF), bf16), pltpu.VMEM((D_FF, D_MODEL), bf16),
                            pltpu.SemaphoreType.DMA((2,)), pltpu.SemaphoreType.DMA((2,))]),
        out_shape=jax.ShapeDtypeStruct(((N_ASSIGN + 2 * tm) * ROW_TILE, LANE), f32),
        compiler_params=_cparams("arbitrary"),
        name="moe",
    )(block_e, n_used, nxt_e, row_dst, xs, w_gu, b_gu.reshape(DEPTH, N_EXPERTS, 1, 2 * D_FF),
      w_down, b_down.reshape(DEPTH, N_EXPERTS, 1, D_MODEL))


def _combine_kernel(final, x_ref, y0_ref, y1_ref, y2_ref, y3_ref, gt_ref, gate_ref, gf_ref, o_ref):
    gt = gt_ref[...]
    f = gt[:, 0:1] * _load_row_tiles(y0_ref)
    for k, y_ref in ((1, y1_ref), (2, y2_ref), (3, y3_ref)):
        f = f + gt[:, k:k + 1] * _load_row_tiles(y_ref)
    out = x_ref[...] + gate_ref[0] * f
    if final:
        out = _rms(out, gf_ref[...])
    o_ref[...] = out


def _combine(final, x, y, gates, gate, g_final):
    tm = TM_TOK
    nt = T_ALL // tm
    row = lambda i: (i, 0)
    const = lambda i: (0, 0)
    grp = lambda i: (_group_of_tile(i), 0, 0)
    ysel = [pl.BlockSpec((tm * ROW_TILE, LANE), functools.partial(lambda k, i: (k * nt + i, 0), k))
            for k in range(TOP_K)]
    return pl.pallas_call(
        functools.partial(_combine_kernel, final),
        grid=(nt,),
        in_specs=[pl.BlockSpec((tm, D_MODEL), row)] + ysel +
                 [pl.BlockSpec((tm, LANE), row),
                  pl.BlockSpec((1, 1, D_MODEL), grp),
                  pl.BlockSpec((1, D_MODEL), const)],
        out_specs=pl.BlockSpec((tm, D_MODEL), row),
        out_shape=jax.ShapeDtypeStruct((T_ALL, D_MODEL), f32),
        compiler_params=_cparams("arbitrary"),
        name="combine",
    )(x, y, y, y, y, gates, gate, g_final)


def _rope_head_tables(d):
    nf = d // 4
    half = d // 2
    t = np.arange(DEC_SEQ)
    inv = ROPE_BASE ** (-np.arange(nf, dtype=np.float32) / nf)
    i = np.arange(d)
    pos = np.where(i[None, :] < half, (t // GRID_W)[:, None], (t % GRID_W)[:, None]).astype(np.float32)
    ang = pos * inv[i % nf][None, :].astype(np.float32)
    first = (i % half) < nf
    cos = np.cos(ang)
    sin = np.where(first[None, :], -np.sin(ang), np.sin(ang))
    partner = np.where(first, i + nf, i - nf)
    return cos.astype(np.float32), sin.astype(np.float32), partner


def _rope_tables():
    cos64, sin64, _ = _rope_head_tables(HEAD_DIM)
    cos32, sin32, _ = _rope_head_tables(QK_ROPE)
    cosa = np.tile(cos64, (1, H_A))
    sina = np.tile(sin64, (1, H_A))
    cosq1 = np.concatenate([np.ones((DEC_SEQ, QK_NOPE), np.float32), cos32,
                            np.ones((DEC_SEQ, QC_PAD - QK_NOPE - QK_ROPE), np.float32)], axis=1)
    sinq1 = np.concatenate([np.zeros((DEC_SEQ, QK_NOPE), np.float32), sin32,
                            np.zeros((DEC_SEQ, QC_PAD - QK_NOPE - QK_ROPE), np.float32)], axis=1)
    cosq = np.tile(cosq1, (1, H_C))
    sinq = np.tile(sinq1, (1, H_C))
    return tuple(jnp.asarray(a) for a in (cosa, sina, cosq, sinq, cos32, sin32))


def _pad_cols(w, n):
    return jnp.pad(w, ((0, 0), (0, n - w.shape[1])))


def _layer_weights(w_in, w_uq):
    cuts = np.cumsum((W_QA, W_KA, W_VA, W_B, W_B, W_B, Q_LORA, KV_LORA, QK_ROPE))[:-1]
    qa, ka, va, qb, kb, vb, cq, ckv, kr = jnp.split(w_in, [int(c) for c in cuts], axis=1)
    _, _, p64 = _rope_head_tables(HEAD_DIM)
    _, _, p32 = _rope_head_tables(QK_ROPE)
    pa = np.concatenate([h * HEAD_DIM + p64 for h in range(H_A)])
    base = jnp.concatenate([qa, ka, va, _pad_cols(qb, 384), _pad_cols(kb, 384), _pad_cols(vb, 384), cq, ckv,
                            _pad_cols(kr, 128)], axis=1)
    w_ctx = base.astype(bf16)
    w_lat = jnp.concatenate([base, qa[:, pa], ka[:, pa[:W_KA]], _pad_cols(kr[:, p32], 128)], axis=1).astype(bf16)
    hq = QK_NOPE + QK_ROPE
    heads = [_pad_cols(w_uq[:, h * hq:(h + 1) * hq], QC_PAD) for h in range(H_C)]
    pq = np.concatenate([np.arange(QK_NOPE), QK_NOPE + p32])
    heads_p = [_pad_cols(w_uq[:, h * hq:(h + 1) * hq][:, pq], QC_PAD) for h in range(H_C)]
    wuq = jnp.concatenate(heads, axis=1).astype(bf16)
    wuq2 = jnp.concatenate(heads + heads_p, axis=1).astype(bf16)
    return w_ctx, w_lat, wuq, wuq2


def _bias_table(rpb):
    r = np.arange(ROWS)
    key_rows = np.clip(r - NA_ROWS // 2, 0, ROWS - NA_ROWS)[:, None] + np.arange(NA_ROWS)[None, :]
    col = np.arange(GRID_W)
    col_start = np.clip(col - NA_COLS // 2, 0, GRID_W - NA_COLS)
    col_ok = (col[None, :] >= col_start[:, None]) & (col[None, :] < col_start[:, None] + NA_COLS)
    dr = key_rows - r[:, None] + (NA_ROWS - 1)
    dc = np.clip(col[None, :] - col[:, None] + (NA_COLS - 1), 0, 2 * NA_COLS - 2)
    bias = rpb[:, dr][:, :, :, dc]
    bias = jnp.where(col_ok[None, None, None], bias, NEG)
    bias = jnp.transpose(bias, (1, 0, 3, 2, 4))
    return bias.reshape(ROWS, H_B, GRID_W, NA_ROWS * GRID_W).astype(f32)


def _routing(top_e):
    tm = TM_MOE
    flat_e = top_e.T.reshape(N_ASSIGN)
    order = jnp.argsort(flat_e).astype(jnp.int32)
    experts = jnp.arange(N_EXPERTS, dtype=jnp.int32)
    counts = jnp.sum((flat_e[:, None] == experts[None, :]).astype(jnp.int32), axis=0)
    nblk = (counts + tm - 1) // tm
    blk_end = jnp.cumsum(nblk)
    blk_start = blk_end - nblk
    grp_start = jnp.cumsum(counts) - counts
    blocks = jnp.arange(N_MOE_BLOCKS, dtype=jnp.int32)
    block_e = jnp.minimum(jnp.sum((blk_end[None, :] <= blocks[:, None]).astype(jnp.int32), axis=1), N_EXPERTS - 1)
    n_used = blk_end[-1].astype(jnp.int32).reshape(1)
    sel = (block_e[:, None] == experts[None, :]).astype(jnp.int32)
    b_first = jnp.sum(sel * blk_start[None, :], axis=1)
    b_count = jnp.sum(sel * counts[None, :], axis=1)
    b_grp = jnp.sum(sel * grp_start[None, :], axis=1)
    r = jnp.arange(tm, dtype=jnp.int32)[None, :]
    off = (blocks - b_first)[:, None] * tm + r
    valid = (off < b_count[:, None]) & (blocks[:, None] < n_used[0])
    asg = order[jnp.clip(b_grp[:, None] + off, 0, N_ASSIGN - 1)]
    tok = jnp.where(valid, asg % T_ALL, 0)
    row_dst = jnp.where(valid, asg, N_ASSIGN + (blocks[:, None] % 2) * tm + r)
    row_dst = jnp.concatenate([N_ASSIGN + tm + r, row_dst], axis=0).reshape(-1)
    has = jnp.where(counts > 0, experts, N_EXPERTS)
    later = experts[None, :] > experts[:, None]
    nxt = jnp.min(jnp.where(later, has[None, :], N_EXPERTS), axis=1)
    nxt = jnp.where(nxt >= N_EXPERTS, -1, nxt)
    nxt_e = jnp.sum(sel * nxt[None, :], axis=1)
    i32 = lambda a: a.astype(jnp.int32)
    return i32(block_e), n_used, i32(nxt_e), i32(tok).reshape(-1), i32(row_dst)


def kernel(x_prompt, x_sample, cache_a_k, cache_a_v, cache_b_k, cache_b_v, cache_c_kv, cache_c_kr, c, c_ctx, w_ada, b_ada, g_attn, g_ffn, w_in, sink_a, rpb_b, g_cq, g_ckv, w_uq, w_ukv, w_out, w_router, b_router, w_gu, b_gu, w_down, b_down, g_final):
    x = jnp.concatenate([x_prompt.reshape(T_CTX, D_MODEL), x_sample.reshape(T_LAT, D_MODEL)], axis=0)
    cvec = jnp.concatenate([c_ctx[None, :], c, jnp.zeros((8 - N_GROUPS, D_MODEL), f32)], axis=0)
    mods = _ada(cvec, w_ada, b_ada)[:, :N_GROUPS].reshape(DEPTH, N_GROUPS, 6, 1, D_MODEL)
    tabs = _rope_tables()
    caches = (cache_a_k.reshape(DEC_BATCH, DEPTH, PAST_LEN, W_KA), cache_a_v.reshape(DEC_BATCH, DEPTH, PAST_LEN, W_VA),
              cache_b_k.reshape(DEC_BATCH, DEPTH, PAST_LEN, W_B), cache_b_v.reshape(DEC_BATCH, DEPTH, PAST_LEN, W_B),
              cache_c_kv, cache_c_kr)
    new = [[] for _ in range(6)]
    for layer in range(DEPTH):
        m = [mods[layer, :, j] for j in range(6)]
        w_ctx, w_lat, wuq, wuq2 = _layer_weights(w_in[layer], w_uq[layer])
        wukv = w_ukv[layer].astype(bf16)
        wout = w_out[layer].astype(bf16)
        g1 = g_attn[layer][None, :]
        gcq = g_cq[layer][None, :]
        gckv = g_ckv[layer][None, :]
        sink = sink_a[layer]

        pc = _inproj_ctx(x, g1, m[0], m[1], w_ctx, gcq, gckv, wuq)
        for lst, a in zip(new, (pc[1], pc[2], pc[4], pc[5], pc[7], pc[8])):
            lst.append(a)
        x_ctx = _ctx_attn(sink, pc, wukv, wout, x, m[2])

        plat = _inproj_lat(x, g1, m[0], m[1], w_lat, gcq, gckv, wuq2, tabs)
        x_lat = _lat_attn(layer, sink, plat, caches, _bias_table(rpb_b[layer]), wukv, wout, x, m[2])
        x = jnp.concatenate([x_ctx, x_lat], axis=0)

        wr = _pad_cols(w_router[layer], LANE)
        br = _pad_cols(b_router[layer][None, :], LANE)
        h2, top_e, gates = _router(x, g_ffn[layer][None, :], m[3], m[4], wr, br)
        y = _moe(layer, _routing(top_e[:, :TOP_K]), h2, w_gu, b_gu, w_down, b_down)
        x = _combine(layer == DEPTH - 1, x, y, gates, m[5], g_final[None, :])

    y_prompt = x[:T_CTX].reshape(BATCH, SEQ, D_MODEL)
    y_sample = x[T_CTX:].reshape(DEC_BATCH, DEC_SEQ, D_MODEL)
    shapes = ((KV_A, HEAD_DIM), (KV_A, HEAD_DIM), (H_B, HEAD_DIM), (H_B, HEAD_DIM), (KV_LORA,), (QK_ROPE,))
    outs = [jnp.stack([a.reshape((BATCH, SEQ) + s) for a in lst], axis=1) for lst, s in zip(new, shapes)]
    return (y_prompt, y_sample, *outs)
```

```python
import functools

import numpy as np
import jax
import jax.numpy as jnp
from jax import lax
from jax.experimental import pallas as pl
from jax.experimental.pallas import tpu as pltpu

D_MODEL = 1024
BATCH = 32
SEQ = 256
DEPTH = 2
DEC_BATCH = 2
DEC_SEQ = 1024
PAST_LEN = 512
GRID_W = 64
HEAD_DIM = 64
H_A = 6
KV_A = 2
G_A = H_A // KV_A
WINDOW = 128
BLOCK = 128
H_B = 5
NA_ROWS = 8
NA_COLS = 16
H_C = 5
Q_LORA = 384
KV_LORA = 256
QK_NOPE = 64
QK_ROPE = 32
V_C = 64
N_EXPERTS = 32
TOP_K = 4
D_FF = 1024
SWIGLU_ALPHA = 1.702
SWIGLU_LIMIT = 7.0
ROPE_BASE = 10000.0
EPS = 1e-6
NEG = -1e30

T_CTX = BATCH * SEQ
T_LAT = DEC_BATCH * DEC_SEQ
T_ALL = T_CTX + T_LAT
N_GROUPS = 1 + DEC_BATCH
LANE = 128
QC_PAD = 128
ROWS = DEC_SEQ // GRID_W

W_QA, W_KA, W_VA = H_A * HEAD_DIM, KV_A * HEAD_DIM, KV_A * HEAD_DIM
W_B = H_B * HEAD_DIM
OFF_QA = 0
OFF_KA = 384
OFF_VA = 512
OFF_QB = 640
OFF_KB = 1024
OFF_VB = 1408
OFF_CQ = 1792
OFF_CKV = 2176
OFF_KR = 2432
NW_CTX = 2560
OFF_QA_P = 2560
OFF_KA_P = 2944
OFF_KR_P = 3072
NW_LAT = 3200

TM_TOK = 256
TM_LAT_IN = 512
TM_MOE = 256
N_ASSIGN = T_ALL * TOP_K
N_MOE_BLOCKS = N_ASSIGN // TM_MOE + N_EXPERTS
VMEM_LIMIT = 56 * 1024 * 1024

f32 = jnp.float32
bf16 = jnp.bfloat16


def _cparams(*sem):
    return pltpu.CompilerParams(dimension_semantics=sem, vmem_limit_bytes=VMEM_LIMIT)


def _rms(xf, g):
    return xf * lax.rsqrt(jnp.mean(xf * xf, axis=-1, keepdims=True) + EPS) * g


def _dot(a, b):
    return jnp.dot(a, b, preferred_element_type=f32)


def _dot_nt(a, b):
    return lax.dot_general(a, b, (((1,), (1,)), ((), ())), preferred_element_type=f32)


ROW_TILE = D_MODEL // LANE


def _store_row_tiles(ref, val):
    n = val.shape[0]
    for c in range(ROW_TILE):
        ref[pl.ds(c, n, stride=ROW_TILE), :] = val[:, c * LANE:(c + 1) * LANE]


def _load_row_tiles(ref):
    n = ref.shape[0] // ROW_TILE
    return jnp.concatenate([ref[pl.ds(c, n, stride=ROW_TILE), :] for c in range(ROW_TILE)], axis=1)


def _softmax_pv(s, v, sink=None):
    m = jnp.max(s, axis=-1, keepdims=True)
    if sink is not None:
        m = jnp.maximum(m, sink)
    p = jnp.exp(s - m)
    l = jnp.sum(p, axis=-1, keepdims=True)
    if sink is not None:
        l = l + jnp.exp(sink - m)
    return _dot(p.astype(bf16), v) / l


def _ada_kernel(c_ref, w_ref, b_ref, o_ref):
    c = c_ref[...]
    s = c * jax.nn.sigmoid(c)
    o_ref[0] = jnp.dot(s, w_ref[0], preferred_element_type=f32, precision=lax.Precision.HIGHEST) + b_ref[0]


def _ada(cvec, w_ada, b_ada):
    tn = 1536
    return pl.pallas_call(
        _ada_kernel,
        grid=(DEPTH, 6 * D_MODEL // tn),
        in_specs=[pl.BlockSpec((8, D_MODEL), lambda l, j: (0, 0)),
                  pl.BlockSpec((1, D_MODEL, tn), lambda l, j: (l, 0, j)),
                  pl.BlockSpec((1, 1, tn), lambda l, j: (l, 0, j))],
        out_specs=pl.BlockSpec((1, 8, tn), lambda l, j: (l, 0, j)),
        out_shape=jax.ShapeDtypeStruct((DEPTH, 8, 6 * D_MODEL), f32),
        compiler_params=_cparams("arbitrary", "arbitrary"),
        name="ada",
    )(cvec, w_ada, b_ada.reshape(DEPTH, 1, 6 * D_MODEL))


def _inproj_ctx_kernel(x_ref, g_ref, sh_ref, sc_ref, w_ref, gcq_ref, gckv_ref, wuq_ref,
                       qa_ref, ka_ref, va_ref, qb_ref, kb_ref, vb_ref, qc_ref, ckv_ref, kr_ref):
    h = _rms(x_ref[...], g_ref[...]) * (1.0 + sc_ref[0]) + sh_ref[0]
    p = _dot(h.astype(bf16), w_ref[...])
    qa_ref[...] = p[:, OFF_QA:OFF_QA + W_QA].astype(bf16)
    ka_ref[...] = p[:, OFF_KA:OFF_KA + W_KA]
    va_ref[...] = p[:, OFF_VA:OFF_VA + W_VA]
    qb_ref[...] = p[:, OFF_QB:OFF_QB + W_B].astype(bf16)
    kb_ref[...] = p[:, OFF_KB:OFF_KB + W_B]
    vb_ref[...] = p[:, OFF_VB:OFF_VB + W_B]
    cqn = _rms(p[:, OFF_CQ:OFF_CQ + Q_LORA], gcq_ref[...])
    qc_ref[...] = _dot(cqn.astype(bf16), wuq_ref[...]).astype(bf16)
    ckv_ref[...] = _rms(p[:, OFF_CKV:OFF_CKV + KV_LORA], gckv_ref[...])
    kr_ref[...] = p[:, OFF_KR:OFF_KR + QK_ROPE]


def _inproj_ctx(x, g, shift, scale, w, gcq, gckv, wuq):
    tm = TM_TOK
    row = lambda i: (i, 0)
    const = lambda i: (0, 0)
    widths = (W_QA, W_KA, W_VA, W_B, W_B, W_B, H_C * QC_PAD, KV_LORA, QK_ROPE)
    dtypes = (bf16, f32, f32, bf16, f32, f32, bf16, f32, f32)
    return pl.pallas_call(
        _inproj_ctx_kernel,
        grid=(T_CTX // tm,),
        in_specs=[pl.BlockSpec((tm, D_MODEL), row),
                  pl.BlockSpec((1, D_MODEL), const),
                  pl.BlockSpec((1, 1, D_MODEL), lambda i: (0, 0, 0)),
                  pl.BlockSpec((1, 1, D_MODEL), lambda i: (0, 0, 0)),
                  pl.BlockSpec((D_MODEL, NW_CTX), const),
                  pl.BlockSpec((1, Q_LORA), const),
                  pl.BlockSpec((1, KV_LORA), const),
                  pl.BlockSpec((Q_LORA, H_C * QC_PAD), const)],
        out_specs=[pl.BlockSpec((tm, wd), row) for wd in widths],
        out_shape=[jax.ShapeDtypeStruct((T_CTX, wd), dt) for wd, dt in zip(widths, dtypes)],
        compiler_params=_cparams("arbitrary"),
        name="inproj_ctx",
    )(x, g, shift, scale, w, gcq, gckv, wuq)


def _inproj_lat_kernel(x_ref, g_ref, sh_ref, sc_ref, w_ref, gcq_ref, gckv_ref, wuq_ref,
                       cosa_ref, sina_ref, cosq_ref, sinq_ref, cosr_ref, sinr_ref,
                       qa_ref, ka_ref, va_ref, qb_ref, kb_ref, vb_ref, qc_ref, ckv_ref, kr_ref):
    h = _rms(x_ref[...], g_ref[...]) * (1.0 + sc_ref[0]) + sh_ref[0]
    p = _dot(h.astype(bf16), w_ref[...])
    cosa = cosa_ref[...]
    sina = sina_ref[...]
    qa = p[:, OFF_QA:OFF_QA + W_QA] * cosa + p[:, OFF_QA_P:OFF_QA_P + W_QA] * sina
    ka = p[:, OFF_KA:OFF_KA + W_KA] * cosa[:, :W_KA] + p[:, OFF_KA_P:OFF_KA_P + W_KA] * sina[:, :W_KA]
    kr = p[:, OFF_KR:OFF_KR + QK_ROPE] * cosr_ref[...] + p[:, OFF_KR_P:OFF_KR_P + QK_ROPE] * sinr_ref[...]
    qa_ref[...] = qa.astype(bf16)
    ka_ref[...] = ka.astype(bf16)
    va_ref[...] = p[:, OFF_VA:OFF_VA + W_VA].astype(bf16)
    qb_ref[...] = p[:, OFF_QB:OFF_QB + W_B].astype(bf16)
    kb_ref[...] = p[:, OFF_KB:OFF_KB + W_B].astype(bf16)
    vb_ref[...] = p[:, OFF_VB:OFF_VB + W_B].astype(bf16)
    cqn = _rms(p[:, OFF_CQ:OFF_CQ + Q_LORA], gcq_ref[...])
    q2 = _dot(cqn.astype(bf16), wuq_ref[...])
    nq = H_C * QC_PAD
    qc_ref[...] = (q2[:, :nq] * cosq_ref[...] + q2[:, nq:] * sinq_ref[...]).astype(bf16)
    ckv_ref[...] = _rms(p[:, OFF_CKV:OFF_CKV + KV_LORA], gckv_ref[...]).astype(bf16)
    kr_ref[...] = kr.astype(bf16)


def _inproj_lat(x, g, shift, scale, w, gcq, gckv, wuq2, tabs):
    tm = TM_LAT_IN
    per_b = DEC_SEQ // tm
    row0 = T_CTX // tm
    xrow = lambda i: (row0 + i, 0)
    row = lambda i: (i, 0)
    const = lambda i: (0, 0)
    grp = lambda i: (1 + i // per_b, 0, 0)
    pos = lambda i: (i % per_b, 0)
    cosa, sina, cosq, sinq, cosr, sinr = tabs
    widths = (W_QA, W_KA, W_VA, W_B, W_B, W_B, H_C * QC_PAD, KV_LORA, QK_ROPE)
    return pl.pallas_call(
        _inproj_lat_kernel,
        grid=(T_LAT // tm,),
        in_specs=[pl.BlockSpec((tm, D_MODEL), xrow),
                  pl.BlockSpec((1, D_MODEL), const),
                  pl.BlockSpec((1, 1, D_MODEL), grp),
                  pl.BlockSpec((1, 1, D_MODEL), grp),
                  pl.BlockSpec((D_MODEL, NW_LAT), const),
                  pl.BlockSpec((1, Q_LORA), const),
                  pl.BlockSpec((1, KV_LORA), const),
                  pl.BlockSpec((Q_LORA, 2 * H_C * QC_PAD), const),
                  pl.BlockSpec((tm, W_QA), pos), pl.BlockSpec((tm, W_QA), pos),
                  pl.BlockSpec((tm, H_C * QC_PAD), pos), pl.BlockSpec((tm, H_C * QC_PAD), pos),
                  pl.BlockSpec((tm, QK_ROPE), pos), pl.BlockSpec((tm, QK_ROPE), pos)],
        out_specs=[pl.BlockSpec((tm, wd), row) for wd in widths],
        out_shape=[jax.ShapeDtypeStruct((T_LAT, wd), bf16) for wd in widths],
        compiler_params=_cparams("arbitrary"),
        name="inproj_lat",
    )(x, g, shift, scale, w, gcq, gckv, wuq2, cosa, sina, cosq, sinq, cosr, sinr)


def _ctx_attn_kernel(sink_ref, qa_ref, ka_ref, va_ref, qb_ref, kb_ref, vb_ref, qc_ref, ckv_ref, kr_ref,
                     wukv_ref, wout_ref, x_ref, gate_ref, o_ref, o_scr):
    scale = HEAD_DIM ** -0.5
    ka = ka_ref[...].astype(bf16)
    va = va_ref[...].astype(bf16)
    for h in range(H_A):
        g = h // G_A
        q = qa_ref[:, h * HEAD_DIM:(h + 1) * HEAD_DIM]
        s = _dot_nt(q, ka[:, g * HEAD_DIM:(g + 1) * HEAD_DIM]) * scale
        o_scr[:, h * HEAD_DIM:(h + 1) * HEAD_DIM] = _softmax_pv(s, va[:, g * HEAD_DIM:(g + 1) * HEAD_DIM],
                                                               sink_ref[h])
    kb = kb_ref[...].astype(bf16)
    vb = vb_ref[...].astype(bf16)
    for h in range(H_B):
        sl = slice(h * HEAD_DIM, (h + 1) * HEAD_DIM)
        s = _dot_nt(qb_ref[:, sl], kb[:, sl]) * scale
        o_scr[:, W_QA + h * HEAD_DIM:W_QA + (h + 1) * HEAD_DIM] = _softmax_pv(s, vb[:, sl])
    kv = _dot(ckv_ref[...].astype(bf16), wukv_ref[...]).astype(bf16)
    kr = kr_ref[...].astype(bf16)
    scale_c = (QK_NOPE + QK_ROPE) ** -0.5
    for h in range(H_C):
        qn = qc_ref[:, h * QC_PAD:h * QC_PAD + QK_NOPE]
        qr = qc_ref[:, h * QC_PAD + QK_NOPE:h * QC_PAD + QK_NOPE + QK_ROPE]
        c0 = h * (QK_NOPE + V_C)
        s = (_dot_nt(qn, kv[:, c0:c0 + QK_NOPE]) + _dot_nt(qr, kr)) * scale_c
        off = W_QA + W_B + h * V_C
        o_scr[:, off:off + V_C] = _softmax_pv(s, kv[:, c0 + QK_NOPE:c0 + QK_NOPE + V_C])
    y = _dot(o_scr[...].astype(bf16), wout_ref[...])
    o_ref[...] = x_ref[...] + gate_ref[0] * y


def _ctx_attn(sink, proj, wukv, wout, x, gate):
    qa, ka, va, qb, kb, vb, qc, ckv, kr = proj
    row = lambda b: (b, 0)
    const = lambda b: (0, 0)
    in_specs = [pl.BlockSpec(memory_space=pltpu.SMEM)]
    in_specs += [pl.BlockSpec((SEQ, a.shape[1]), row) for a in proj]
    in_specs += [pl.BlockSpec((KV_LORA, H_C * (QK_NOPE + V_C)), const),
                 pl.BlockSpec((D_MODEL, D_MODEL), const),
                 pl.BlockSpec((SEQ, D_MODEL), row),
                 pl.BlockSpec((1, 1, D_MODEL), lambda b: (0, 0, 0))]
    return pl.pallas_call(
        _ctx_attn_kernel,
        grid=(BATCH,),
        in_specs=in_specs,
        out_specs=pl.BlockSpec((SEQ, D_MODEL), row),
        out_shape=jax.ShapeDtypeStruct((T_CTX, D_MODEL), f32),
        scratch_shapes=[pltpu.VMEM((SEQ, D_MODEL), f32)],
        compiler_params=_cparams("arbitrary"),
        name="ctx_attn",
    )(sink, qa, ka, va, qb, kb, vb, qc, ckv, kr, wukv, wout, x, gate)


def _lat_attn_kernel(sink_ref, qa_ref, qb_ref, qc_ref, ka_ref, va_ref, kb_ref, vb_ref, ckv_ref, kr_ref,
                     cak_ref, cav_ref, cbk_ref, cbv_ref, cckv_ref, ckr_ref, bias_ref,
                     wukv_ref, wout_ref, x_ref, gate_ref, o_ref, o_scr, kv_scr):
    qi = pl.program_id(1)
    nb = DEC_SEQ // BLOCK
    scale = HEAD_DIM ** -0.5

    @pl.when(qi == 0)
    def _():
        kv_scr[0:DEC_SEQ, :] = _dot(ckv_ref[...], wukv_ref[...]).astype(bf16)
        kv_scr[DEC_SEQ:DEC_SEQ + PAST_LEN, :] = _dot(cckv_ref[0, 0].astype(bf16), wukv_ref[...]).astype(bf16)

    def blk(ref, j):
        idx = jnp.clip(qi + j, 0, nb - 1)
        return ref[pl.ds(pl.multiple_of(idx * BLOCK, BLOCK), BLOCK), :]

    ka = jnp.concatenate([blk(ka_ref, -1), blk(ka_ref, 0), blk(ka_ref, 1), cak_ref[0, 0].astype(bf16)], axis=0)
    va = jnp.concatenate([blk(va_ref, -1), blk(va_ref, 0), blk(va_ref, 1), cav_ref[0, 0].astype(bf16)], axis=0)
    nk_a = 3 * BLOCK + PAST_LEN
    r = lax.broadcasted_iota(jnp.int32, (BLOCK, nk_a), 0)
    c = lax.broadcasted_iota(jnp.int32, (BLOCK, nk_a), 1)
    valid = (((c < BLOCK) & (c >= r) & (qi > 0))
             | ((c >= BLOCK) & (c < 2 * BLOCK))
             | ((c >= 2 * BLOCK) & (c < 3 * BLOCK) & (c - 2 * BLOCK <= r) & (qi < nb - 1))
             | (c >= 3 * BLOCK))
    for h in range(H_A):
        g = h // G_A
        q = qa_ref[:, h * HEAD_DIM:(h + 1) * HEAD_DIM]
        s = _dot_nt(q, ka[:, g * HEAD_DIM:(g + 1) * HEAD_DIM]) * scale
        s = jnp.where(valid, s, NEG)
        o_scr[:, h * HEAD_DIM:(h + 1) * HEAD_DIM] = _softmax_pv(s, va[:, g * HEAD_DIM:(g + 1) * HEAD_DIM],
                                                               sink_ref[h])

    cbk = cbk_ref[0, 0].astype(bf16)
    cbv = cbv_ref[0, 0].astype(bf16)
    rows_per_blk = BLOCK // GRID_W
    nloc = NA_ROWS * GRID_W
    for half in range(rows_per_blk):
        grow = qi * rows_per_blk + half
        start = jnp.clip(grow - NA_ROWS // 2, 0, ROWS - NA_ROWS)
        kloc = kb_ref[pl.ds(pl.multiple_of(start * GRID_W, GRID_W), nloc), :]
        vloc = vb_ref[pl.ds(pl.multiple_of(start * GRID_W, GRID_W), nloc), :]
        vcat = jnp.concatenate([vloc, cbv], axis=0)
        qrows = slice(half * GRID_W, (half + 1) * GRID_W)
        dr0 = start - grow + (NA_ROWS - 1)
        for h in range(H_B):
            sl = slice(h * HEAD_DIM, (h + 1) * HEAD_DIM)
            q = qb_ref[qrows, sl]
            bias = jnp.concatenate([bias_ref[h, dr0 + 2 * j] for j in range(NA_ROWS // 2)], axis=1)
            s_loc = _dot_nt(q, kloc[:, sl]) * scale + bias
            s_ctx = _dot_nt(q, cbk[:, sl]) * scale
            s = jnp.concatenate([s_loc, s_ctx], axis=1)
            o_scr[qrows, W_QA + h * HEAD_DIM:W_QA + (h + 1) * HEAD_DIM] = _softmax_pv(s, vcat[:, sl])

    kr = jnp.concatenate([kr_ref[...], ckr_ref[0, 0].astype(bf16)], axis=0)
    scale_c = (QK_NOPE + QK_ROPE) ** -0.5
    for h in range(H_C):
        qn = qc_ref[:, h * QC_PAD:h * QC_PAD + QK_NOPE]
        qr = qc_ref[:, h * QC_PAD + QK_NOPE:h * QC_PAD + QK_NOPE + QK_ROPE]
        c0 = h * (QK_NOPE + V_C)
        s = (_dot_nt(qn, kv_scr[:, c0:c0 + QK_NOPE]) + _dot_nt(qr, kr)) * scale_c
        off = W_QA + W_B + h * V_C
        o_scr[:, off:off + V_C] = _softmax_pv(s, kv_scr[:, c0 + QK_NOPE:c0 + QK_NOPE + V_C])

    y = _dot(o_scr[...].astype(bf16), wout_ref[...])
    o_ref[...] = x_ref[...] + gate_ref[0] * y


def _lat_attn(layer, sink, proj, caches, bias_tab, wukv, wout, x, gate):
    qa, ka, va, qb, kb, vb, qc, ckv, kr = proj
    nb = DEC_SEQ // BLOCK
    qrow = lambda b, q: (b * nb + q, 0)
    xrow = lambda b, q: (T_CTX // BLOCK + b * nb + q, 0)
    brow = lambda b, q: (b, 0)
    const = lambda b, q: (0, 0)
    cidx = lambda b, q: (b, layer, 0, 0)
    in_specs = [pl.BlockSpec(memory_space=pltpu.SMEM)]
    in_specs += [pl.BlockSpec((BLOCK, a.shape[1]), qrow) for a in (qa, qb, qc)]
    in_specs += [pl.BlockSpec((DEC_SEQ, a.shape[1]), brow) for a in (ka, va, kb, vb, ckv, kr)]
    in_specs += [pl.BlockSpec((1, 1, PAST_LEN, a.shape[3]), cidx) for a in caches]
    in_specs += [pl.BlockSpec(bias_tab.shape, lambda b, q: (0, 0, 0, 0)),
                 pl.BlockSpec((KV_LORA, H_C * (QK_NOPE + V_C)), const),
                 pl.BlockSpec((D_MODEL, D_MODEL), const),
                 pl.BlockSpec((BLOCK, D_MODEL), xrow),
                 pl.BlockSpec((1, 1, D_MODEL), lambda b, q: (1 + b, 0, 0))]
    return pl.pallas_call(
        _lat_attn_kernel,
        grid=(DEC_BATCH, nb),
        in_specs=in_specs,
        out_specs=pl.BlockSpec((BLOCK, D_MODEL), qrow),
        out_shape=jax.ShapeDtypeStruct((T_LAT, D_MODEL), f32),
        scratch_shapes=[pltpu.VMEM((BLOCK, D_MODEL), f32),
                        pltpu.VMEM((DEC_SEQ + PAST_LEN, H_C * (QK_NOPE + V_C)), bf16)],
        compiler_params=_cparams("arbitrary", "arbitrary"),
        name="lat_attn",
    )(sink, qa, qb, qc, ka, va, kb, vb, ckv, kr, *caches, bias_tab, wukv, wout, x, gate)


def _router_kernel(x_ref, g_ref, sh_ref, sc_ref, wr_ref, br_ref, h_ref, e_ref, gt_ref):
    h = _rms(x_ref[...], g_ref[...]) * (1.0 + sc_ref[0]) + sh_ref[0]
    _store_row_tiles(h_ref, h)
    logits = jnp.dot(h, wr_ref[...], preferred_element_type=f32, precision=lax.Precision.HIGHEST) + br_ref[...]
    lane = lax.broadcasted_iota(jnp.int32, logits.shape, 1).astype(f32)
    l = jnp.where(lane < N_EXPERTS, logits, -jnp.inf)
    tops, idxs = [], []
    for _ in range(TOP_K):
        m = jnp.max(l, axis=-1, keepdims=True)
        idx = jnp.min(jnp.where(l == m, lane, float(LANE)), axis=-1, keepdims=True)
        tops.append(m)
        idxs.append(idx)
        l = jnp.where(lane == idx, -jnp.inf, l)
    ex = [jnp.exp(t - tops[0]) for t in tops]
    den = ex[0] + ex[1] + ex[2] + ex[3]
    e_out = jnp.zeros(logits.shape, f32)
    g_out = jnp.zeros(logits.shape, f32)
    for k in range(TOP_K):
        e_out = jnp.where(lane == k, idxs[k], e_out)
        g_out = jnp.where(lane == k, ex[k] / den, g_out)
    e_ref[...] = e_out.astype(jnp.int32)
    gt_ref[...] = g_out


def _group_of_tile(i):
    per_b = DEC_SEQ // TM_TOK
    n_ctx = T_CTX // TM_TOK
    return jnp.where(i < n_ctx, 0, 1 + (i - n_ctx) // per_b)


def _router(x, g, shift, scale, wr, br):
    tm = TM_TOK
    row = lambda i: (i, 0)
    const = lambda i: (0, 0)
    grp = lambda i: (_group_of_tile(i), 0, 0)
    return pl.pallas_call(
        _router_kernel,
        grid=(T_ALL // tm,),
        in_specs=[pl.BlockSpec((tm, D_MODEL), row),
                  pl.BlockSpec((1, D_MODEL), const),
                  pl.BlockSpec((1, 1, D_MODEL), grp),
                  pl.BlockSpec((1, 1, D_MODEL), grp),
                  pl.BlockSpec((D_MODEL, LANE), const),
                  pl.BlockSpec((1, LANE), const)],
        out_specs=[pl.BlockSpec((tm * ROW_TILE, LANE), row), pl.BlockSpec((tm, LANE), row),
                   pl.BlockSpec((tm, LANE), row)],
        out_shape=[jax.ShapeDtypeStruct((T_ALL * ROW_TILE, LANE), f32),
                   jax.ShapeDtypeStruct((T_ALL, LANE), jnp.int32),
                   jax.ShapeDtypeStruct((T_ALL, LANE), f32)],
        compiler_params=_cparams("arbitrary"),
        name="router",
    )(x, g, shift, scale, wr, br)


def _dispatch_kernel(tok_ref, nu_ref, h_hbm, o_ref, hv, xg, hsem):
    tm = TM_MOE
    i = pl.program_id(0)

    @pl.when(i == 0)
    def _():
        resident = pltpu.make_async_copy(h_hbm, hv, hsem.at[0])
        resident.start()
        resident.wait()

    @pl.when(i < nu_ref[0])
    def _():
        for r in range(tm):
            t = tok_ref[i * tm + r]
            xg[pl.ds(r * ROW_TILE, ROW_TILE), :] = hv[pl.ds(pl.multiple_of(t * ROW_TILE, ROW_TILE), ROW_TILE), :]
        o_ref[...] = _load_row_tiles(xg).astype(bf16)

    @pl.when(i >= nu_ref[0])
    def _():
        o_ref[...] = jnp.zeros_like(o_ref)


def _dispatch(row_tok, n_used, h):
    tm = TM_MOE
    return pl.pallas_call(
        _dispatch_kernel,
        grid_spec=pltpu.PrefetchScalarGridSpec(
            num_scalar_prefetch=2,
            grid=(N_MOE_BLOCKS,),
            in_specs=[pl.BlockSpec(memory_space=pl.ANY)],
            out_specs=pl.BlockSpec((tm, D_MODEL), lambda i, tok, nu: (i, 0)),
            scratch_shapes=[pltpu.VMEM((T_ALL * ROW_TILE, LANE), f32), pltpu.VMEM((tm * ROW_TILE, LANE), f32),
                            pltpu.SemaphoreType.DMA((1,))]),
        out_shape=jax.ShapeDtypeStruct((N_MOE_BLOCKS * tm, D_MODEL), bf16),
        compiler_params=_cparams("arbitrary"),
        name="dispatch",
    )(row_tok, n_used, h)


def _moe_kernel(layer, be_ref, nu_ref, nxt_ref, dst_ref, x_ref, wgu_hbm, bgu_ref, wd_hbm, bd_ref, y_hbm,
                y0, y1, wgu_st, wd_st, wgu_bf, wd_bf, wsem, ssem):
    tm = TM_MOE
    i = pl.program_id(0)
    nb = pl.num_programs(0)
    used = i < nu_ref[0]
    yb = (y0, y1)

    def out_tile(row):
        return pl.ds(pl.multiple_of(row * ROW_TILE, ROW_TILE), ROW_TILE)

    def scatter_desc(buf, r, dst_row, s):
        return pltpu.make_async_copy(buf.at[out_tile(r)], y_hbm.at[out_tile(dst_row)], ssem.at[s])

    def scatter_wait(s):
        pltpu.make_async_copy(yb[s], y_hbm.at[pl.ds(0, tm * ROW_TILE)], ssem.at[s]).wait()

    def scatter_start(blk, s, unrolled):
        if unrolled:
            for r in range(tm):
                scatter_desc(yb[s], r, dst_ref[(blk + 1) * tm + r], s).start(priority=r % 2)
        else:
            def body(r, carry):
                scatter_desc(yb[s], r, dst_ref[(blk + 1) * tm + r], s).start()
                return carry
            lax.fori_loop(0, tm, body, 0, unroll=8)

    def weight_copies(e):
        return (pltpu.make_async_copy(wgu_hbm.at[layer, e], wgu_st, wsem.at[0]),
                pltpu.make_async_copy(wd_hbm.at[layer, e], wd_st, wsem.at[1]))

    @pl.when(i == 0)
    def _():
        for s in range(2):
            yb[s][...] = jnp.zeros_like(yb[s])
            dummy = pltpu.make_async_copy(yb[s], y_hbm.at[pl.ds((N_ASSIGN + s * tm) * ROW_TILE, tm * ROW_TILE)],
                                          ssem.at[s])
            dummy.start()
            dummy.wait()
        for cp in weight_copies(be_ref[0]):
            cp.start()

    first = jnp.logical_and(used, jnp.logical_or(i == 0, be_ref[i] != be_ref[jnp.maximum(i - 1, 0)]))

    @pl.when(first)
    def _():
        for cp in weight_copies(0):
            cp.wait()
        wgu_bf[...] = wgu_st[...].astype(bf16)
        wd_bf[...] = wd_st[...].astype(bf16)

        @pl.when(nxt_ref[i] >= 0)
        def _():
            for cp in weight_copies(nxt_ref[i]):
                cp.start()

    def step(par):
        cur, oth = par, 1 - par

        @pl.when(i >= 1)
        def _():
            scatter_wait(cur)

        @pl.when(used)
        def _():
            scatter_start(i - 1, oth, unrolled=True)
            gu = _dot(x_ref[...], wgu_bf[...]) + bgu_ref[0, 0]
            x_glu = jnp.minimum(gu[:, :D_FF], SWIGLU_LIMIT)
            x_lin = jnp.clip(gu[:, D_FF:], -SWIGLU_LIMIT, SWIGLU_LIMIT)
            act = x_glu * jax.nn.sigmoid(SWIGLU_ALPHA * x_glu) * (x_lin + 1.0)
            _store_row_tiles(yb[cur], _dot(act.astype(bf16), wd_bf[...]) + bd_ref[0, 0])

        @pl.when(jnp.logical_and(jnp.logical_not(used), i + 1 < nb))
        def _():
            scatter_start(i - 1, oth, unrolled=False)

        @pl.when(i == nb - 1)
        def _():
            scatter_start(i - 1, oth, unrolled=False)
            scatter_wait(oth)

    @pl.when(i % 2 == 0)
    def _():
        step(0)

    @pl.when(i % 2 == 1)
    def _():
        step(1)


def _moe(layer, routing, h, w_gu, b_gu, w_down, b_down):
    tm = TM_MOE
    block_e, n_used, nxt_e, row_tok, row_dst = routing
    xs = _dispatch(row_tok, n_used, h)
    ex4 = lambda i, be, nu, nxt, dst: (layer, be[i], 0, 0)
    return pl.pallas_call(
        functools.partial(_moe_kernel, layer),
        grid_spec=pltpu.PrefetchScalarGridSpec(
            num_scalar_prefetch=4,
            grid=(N_MOE_BLOCKS,),
            in_specs=[pl.BlockSpec((tm, D_MODEL), lambda i, be, nu, nxt, dst: (i, 0)),
                      pl.BlockSpec(memory_space=pl.ANY),
                      pl.BlockSpec((1, 1, 1, 2 * D_FF), ex4),
                      pl.BlockSpec(memory_space=pl.ANY),
                      pl.BlockSpec((1, 1, 1, D_MODEL), ex4)],
            out_specs=pl.BlockSpec(memory_space=pl.ANY),
            scratch_shapes=[pltpu.VMEM((tm * ROW_TILE, LANE), f32), pltpu.VMEM((tm * ROW_TILE, LANE), f32),
                            pltpu.VMEM((D_MODEL, 2 * D_FF), f32), pltpu.VMEM((D_FF, D_MODEL), f32),
                            pltpu.VMEM((D_MODEL, 2 * D_FF), bf16), pltpu.VMEM((D_FF, D_MODEL), bf16),
                            pltpu.SemaphoreType.DMA((2,)), pltpu.SemaphoreType.DMA((2,))]),
        out_shape=jax.ShapeDtypeStruct(((N_ASSIGN + 2 * tm) * ROW_TILE, LANE), f32),
        compiler_params=_cparams("arbitrary"),
        name="moe",
    )(block_e, n_used, nxt_e, row_dst, xs, w_gu, b_gu.reshape(DEPTH, N_EXPERTS, 1, 2 * D_FF),
      w_down, b_down.reshape(DEPTH, N_EXPERTS, 1, D_MODEL))


def _combine_kernel(final, x_ref, y0_ref, y1_ref, y2_ref, y3_ref, gt_ref, gate_ref, gf_ref, o_ref):
    gt = gt_ref[...]
    f = gt[:, 0:1] * _load_row_tiles(y0_ref)
    for k, y_ref in ((1, y1_ref), (2, y2_ref), (3, y3_ref)):
        f = f + gt[:, k:k + 1] * _load_row_tiles(y_ref)
    out = x_ref[...] + gate_ref[0] * f
    if final:
        out = _rms(out, gf_ref[...])
    o_ref[...] = out


def _combine(final, x, y, gates, gate, g_final):
    tm = TM_TOK
    nt = T_ALL // tm
    row = lambda i: (i, 0)
    const = lambda i: (0, 0)
    grp = lambda i: (_group_of_tile(i), 0, 0)
    ysel = [pl.BlockSpec((tm * ROW_TILE, LANE), functools.partial(lambda k, i: (k * nt + i, 0), k))
            for k in range(TOP_K)]
    return pl.pallas_call(
        functools.partial(_combine_kernel, final),
        grid=(nt,),
        in_specs=[pl.BlockSpec((tm, D_MODEL), row)] + ysel +
                 [pl.BlockSpec((tm, LANE), row),
                  pl.BlockSpec((1, 1, D_MODEL), grp),
                  pl.BlockSpec((1, D_MODEL), const)],
        out_specs=pl.BlockSpec((tm, D_MODEL), row),
        out_shape=jax.ShapeDtypeStruct((T_ALL, D_MODEL), f32),
        compiler_params=_cparams("arbitrary"),
        name="combine",
    )(x, y, y, y, y, gates, gate, g_final)


def _rope_head_tables(d):
    nf = d // 4
    half = d // 2
    t = np.arange(DEC_SEQ)
    inv = ROPE_BASE ** (-np.arange(nf, dtype=np.float32) / nf)
    i = np.arange(d)
    pos = np.where(i[None, :] < half, (t // GRID_W)[:, None], (t % GRID_W)[:, None]).astype(np.float32)
    ang = pos * inv[i % nf][None, :].astype(np.float32)
    first = (i % half) < nf
    cos = np.cos(ang)
    sin = np.where(first[None, :], -np.sin(ang), np.sin(ang))
    partner = np.where(first, i + nf, i - nf)
    return cos.astype(np.float32), sin.astype(np.float32), partner


def _rope_tables():
    cos64, sin64, _ = _rope_head_tables(HEAD_DIM)
    cos32, sin32, _ = _rope_head_tables(QK_ROPE)
    cosa = np.tile(cos64, (1, H_A))
    sina = np.tile(sin64, (1, H_A))
    cosq1 = np.concatenate([np.ones((DEC_SEQ, QK_NOPE), np.float32), cos32,
                            np.ones((DEC_SEQ, QC_PAD - QK_NOPE - QK_ROPE), np.float32)], axis=1)
    sinq1 = np.concatenate([np.zeros((DEC_SEQ, QK_NOPE), np.float32), sin32,
                            np.zeros((DEC_SEQ, QC_PAD - QK_NOPE - QK_ROPE), np.float32)], axis=1)
    cosq = np.tile(cosq1, (1, H_C))
    sinq = np.tile(sinq1, (1, H_C))
    return tuple(jnp.asarray(a) for a in (cosa, sina, cosq, sinq, cos32, sin32))


def _pad_cols(w, n):
    return jnp.pad(w, ((0, 0), (0, n - w.shape[1])))


def _layer_weights(w_in, w_uq):
    cuts = np.cumsum((W_QA, W_KA, W_VA, W_B, W_B, W_B, Q_LORA, KV_LORA, QK_ROPE))[:-1]
    qa, ka, va, qb, kb, vb, cq, ckv, kr = jnp.split(w_in, [int(c) for c in cuts], axis=1)
    _, _, p64 = _rope_head_tables(HEAD_DIM)
    _, _, p32 = _rope_head_tables(QK_ROPE)
    pa = np.concatenate([h * HEAD_DIM + p64 for h in range(H_A)])
    base = jnp.concatenate([qa, ka, va, _pad_cols(qb, 384), _pad_cols(kb, 384), _pad_cols(vb, 384), cq, ckv,
                            _pad_cols(kr, 128)], axis=1)
    w_ctx = base.astype(bf16)
    w_lat = jnp.concatenate([base, qa[:, pa], ka[:, pa[:W_KA]], _pad_cols(kr[:, p32], 128)], axis=1).astype(bf16)
    hq = QK_NOPE + QK_ROPE
    heads = [_pad_cols(w_uq[:, h * hq:(h + 1) * hq], QC_PAD) for h in range(H_C)]
    pq = np.concatenate([np.arange(QK_NOPE), QK_NOPE + p32])
    heads_p = [_pad_cols(w_uq[:, h * hq:(h + 1) * hq][:, pq], QC_PAD) for h in range(H_C)]
    wuq = jnp.concatenate(heads, axis=1).astype(bf16)
    wuq2 = jnp.concatenate(heads + heads_p, axis=1).astype(bf16)
    return w_ctx, w_lat, wuq, wuq2


def _bias_table(rpb):
    col = np.arange(GRID_W)
    col_start = np.clip(col - NA_COLS // 2, 0, GRID_W - NA_COLS)
    col_ok = (col[None, :] >= col_start[:, None]) & (col[None, :] < col_start[:, None] + NA_COLS)
    dc = np.clip(col[None, :] - col[:, None] + (NA_COLS - 1), 0, 2 * NA_COLS - 2)
    blocks = jnp.where(col_ok[None, None], rpb[:, :, dc], NEG).astype(f32)
    return jnp.concatenate([blocks[:, :-1], blocks[:, 1:]], axis=-1)


def _routing(top_e):
    tm = TM_MOE
    flat_e = top_e.T.reshape(N_ASSIGN)
    order = jnp.argsort(flat_e).astype(jnp.int32)
    experts = jnp.arange(N_EXPERTS, dtype=jnp.int32)
    counts = jnp.sum((flat_e[:, None] == experts[None, :]).astype(jnp.int32), axis=0)
    nblk = (counts + tm - 1) // tm
    blk_end = jnp.cumsum(nblk)
    blk_start = blk_end - nblk
    grp_start = jnp.cumsum(counts) - counts
    blocks = jnp.arange(N_MOE_BLOCKS, dtype=jnp.int32)
    block_e = jnp.minimum(jnp.sum((blk_end[None, :] <= blocks[:, None]).astype(jnp.int32), axis=1), N_EXPERTS - 1)
    n_used = blk_end[-1].astype(jnp.int32).reshape(1)
    sel = (block_e[:, None] == experts[None, :]).astype(jnp.int32)
    b_first = jnp.sum(sel * blk_start[None, :], axis=1)
    b_count = jnp.sum(sel * counts[None, :], axis=1)
    b_grp = jnp.sum(sel * grp_start[None, :], axis=1)
    r = jnp.arange(tm, dtype=jnp.int32)[None, :]
    off = (blocks - b_first)[:, None] * tm + r
    valid = (off < b_count[:, None]) & (blocks[:, None] < n_used[0])
    asg = order[jnp.clip(b_grp[:, None] + off, 0, N_ASSIGN - 1)]
    tok = jnp.where(valid, asg % T_ALL, 0)
    row_dst = jnp.where(valid, asg, N_ASSIGN + (blocks[:, None] % 2) * tm + r)
    row_dst = jnp.concatenate([N_ASSIGN + tm + r, row_dst], axis=0).reshape(-1)
    has = jnp.where(counts > 0, experts, N_EXPERTS)
    later = experts[None, :] > experts[:, None]
    nxt = jnp.min(jnp.where(later, has[None, :], N_EXPERTS), axis=1)
    nxt = jnp.where(nxt >= N_EXPERTS, -1, nxt)
    nxt_e = jnp.sum(sel * nxt[None, :], axis=1)
    i32 = lambda a: a.astype(jnp.int32)
    return i32(block_e), n_used, i32(nxt_e), i32(tok).reshape(-1), i32(row_dst)


def kernel(x_prompt, x_sample, cache_a_k, cache_a_v, cache_b_k, cache_b_v, cache_c_kv, cache_c_kr, c, c_ctx, w_ada, b_ada, g_attn, g_ffn, w_in, sink_a, rpb_b, g_cq, g_ckv, w_uq, w_ukv, w_out, w_router, b_router, w_gu, b_gu, w_down, b_down, g_final):
    x = jnp.concatenate([x_prompt.reshape(T_CTX, D_MODEL), x_sample.reshape(T_LAT, D_MODEL)], axis=0)
    cvec = jnp.concatenate([c_ctx[None, :], c, jnp.zeros((8 - N_GROUPS, D_MODEL), f32)], axis=0)
    mods = _ada(cvec, w_ada, b_ada)[:, :N_GROUPS].reshape(DEPTH, N_GROUPS, 6, 1, D_MODEL)
    tabs = _rope_tables()
    caches = (cache_a_k.reshape(DEC_BATCH, DEPTH, PAST_LEN, W_KA), cache_a_v.reshape(DEC_BATCH, DEPTH, PAST_LEN, W_VA),
              cache_b_k.reshape(DEC_BATCH, DEPTH, PAST_LEN, W_B), cache_b_v.reshape(DEC_BATCH, DEPTH, PAST_LEN, W_B),
              cache_c_kv, cache_c_kr)
    new = [[] for _ in range(6)]
    for layer in range(DEPTH):
        m = [mods[layer, :, j] for j in range(6)]
        w_ctx, w_lat, wuq, wuq2 = _layer_weights(w_in[layer], w_uq[layer])
        wukv = w_ukv[layer].astype(bf16)
        wout = w_out[layer].astype(bf16)
        g1 = g_attn[layer][None, :]
        gcq = g_cq[layer][None, :]
        gckv = g_ckv[layer][None, :]
        sink = sink_a[layer]

        pc = _inproj_ctx(x, g1, m[0], m[1], w_ctx, gcq, gckv, wuq)
        for lst, a in zip(new, (pc[1], pc[2], pc[4], pc[5], pc[7], pc[8])):
            lst.append(a)
        x_ctx = _ctx_attn(sink, pc, wukv, wout, x, m[2])

        plat = _inproj_lat(x, g1, m[0], m[1], w_lat, gcq, gckv, wuq2, tabs)
        x_lat = _lat_attn(layer, sink, plat, caches, _bias_table(rpb_b[layer]), wukv, wout, x, m[2])
        x = jnp.concatenate([x_ctx, x_lat], axis=0)

        wr = _pad_cols(w_router[layer], LANE)
        br = _pad_cols(b_router[layer][None, :], LANE)
        h2, top_e, gates = _router(x, g_ffn[layer][None, :], m[3], m[4], wr, br)
        y = _moe(layer, _routing(top_e[:, :TOP_K]), h2, w_gu, b_gu, w_down, b_down)
        x = _combine(layer == DEPTH - 1, x, y, gates, m[5], g_final[None, :])

    y_prompt = x[:T_CTX].reshape(BATCH, SEQ, D_MODEL)
    y_sample = x[T_CTX:].reshape(DEC_BATCH, DEC_SEQ, D_MODEL)
    shapes = ((KV_A, HEAD_DIM), (KV_A, HEAD_DIM), (H_B, HEAD_DIM), (H_B, HEAD_DIM), (KV_LORA,), (QK_ROPE,))
    outs = [jnp.stack([a.reshape((BATCH, SEQ) + s) for a in lst], axis=1) for lst, s in zip(new, shapes)]
    return (y_prompt, y_sample, *outs)
```

```python
import functools

import numpy as np
import jax
import jax.numpy as jnp
from jax import lax
from jax.experimental import pallas as pl
from jax.experimental.pallas import tpu as pltpu

D_MODEL = 1024
BATCH = 32
SEQ = 256
DEPTH = 2
DEC_BATCH = 2
DEC_SEQ = 1024
PAST_LEN = 512
GRID_W = 64
HEAD_DIM = 64
H_A = 6
KV_A = 2
G_A = H_A // KV_A
WINDOW = 128
BLOCK = 128
H_B = 5
NA_ROWS = 8
NA_COLS = 16
H_C = 5
Q_LORA = 384
KV_LORA = 256
QK_NOPE = 64
QK_ROPE = 32
V_C = 64
N_EXPERTS = 32
TOP_K = 4
D_FF = 1024
SWIGLU_ALPHA = 1.702
SWIGLU_LIMIT = 7.0
ROPE_BASE = 10000.0
EPS = 1e-6
NEG = -1e30

T_CTX = BATCH * SEQ
T_LAT = DEC_BATCH * DEC_SEQ
T_ALL = T_CTX + T_LAT
N_GROUPS = 1 + DEC_BATCH
LANE = 128
QC_PAD = 128
ROWS = DEC_SEQ // GRID_W

W_QA, W_KA, W_VA = H_A * HEAD_DIM, KV_A * HEAD_DIM, KV_A * HEAD_DIM
W_B = H_B * HEAD_DIM
OFF_QA = 0
OFF_KA = 384
OFF_VA = 512
OFF_QB = 640
OFF_KB = 1024
OFF_VB = 1408
OFF_CQ = 1792
OFF_CKV = 2176
OFF_KR = 2432
NW_CTX = 2560
OFF_QA_P = 2560
OFF_KA_P = 2944
OFF_KR_P = 3072
NW_LAT = 3200

TM_TOK = 256
TM_LAT_IN = 512
TM_MOE = 256
N_ASSIGN = T_ALL * TOP_K
N_MOE_BLOCKS = N_ASSIGN // TM_MOE + N_EXPERTS
VMEM_LIMIT = 56 * 1024 * 1024

f32 = jnp.float32
bf16 = jnp.bfloat16


def _cparams(*sem):
    return pltpu.CompilerParams(dimension_semantics=sem, vmem_limit_bytes=VMEM_LIMIT)


def _rms(xf, g):
    return xf * lax.rsqrt(jnp.mean(xf * xf, axis=-1, keepdims=True) + EPS) * g


def _dot(a, b):
    return jnp.dot(a, b, preferred_element_type=f32)


def _dot_nt(a, b):
    return lax.dot_general(a, b, (((1,), (1,)), ((), ())), preferred_element_type=f32)


ROW_TILE = D_MODEL // LANE


def _store_row_tiles(ref, val):
    n = val.shape[0]
    for c in range(ROW_TILE):
        ref[pl.ds(c, n, stride=ROW_TILE), :] = val[:, c * LANE:(c + 1) * LANE]


def _load_row_tiles(ref):
    n = ref.shape[0] // ROW_TILE
    return jnp.concatenate([ref[pl.ds(c, n, stride=ROW_TILE), :] for c in range(ROW_TILE)], axis=1)


def _softmax_rows(s_ref, p_ref, rows, sinks=None):
    s = s_ref[rows, :]
    m = jnp.max(s, axis=-1, keepdims=True)
    if sinks is not None:
        sink = jnp.concatenate([jnp.full((n, 1), v, f32) for v, n in sinks], axis=0)
        m = jnp.maximum(m, sink)
    p = jnp.exp(s - m)
    l = jnp.sum(p, axis=-1, keepdims=True)
    if sinks is not None:
        l = l + jnp.exp(sink - m)
    p_ref[rows, :] = (p * (1.0 / l)).astype(bf16)


def _ada_kernel(c_ref, w_ref, b_ref, o_ref):
    c = c_ref[...]
    s = c * jax.nn.sigmoid(c)
    o_ref[0] = jnp.dot(s, w_ref[0], preferred_element_type=f32, precision=lax.Precision.HIGHEST) + b_ref[0]


def _ada(cvec, w_ada, b_ada):
    tn = 1536
    return pl.pallas_call(
        _ada_kernel,
        grid=(DEPTH, 6 * D_MODEL // tn),
        in_specs=[pl.BlockSpec((8, D_MODEL), lambda l, j: (0, 0)),
                  pl.BlockSpec((1, D_MODEL, tn), lambda l, j: (l, 0, j)),
                  pl.BlockSpec((1, 1, tn), lambda l, j: (l, 0, j))],
        out_specs=pl.BlockSpec((1, 8, tn), lambda l, j: (l, 0, j)),
        out_shape=jax.ShapeDtypeStruct((DEPTH, 8, 6 * D_MODEL), f32),
        compiler_params=_cparams("arbitrary", "arbitrary"),
        name="ada",
    )(cvec, w_ada, b_ada.reshape(DEPTH, 1, 6 * D_MODEL))


def _inproj_ctx_kernel(x_ref, g_ref, sh_ref, sc_ref, w_ref, gcq_ref, gckv_ref, wuq_ref,
                       qa_ref, ka_ref, va_ref, qb_ref, kb_ref, vb_ref, qc_ref, ckv_ref, kr_ref):
    h = _rms(x_ref[...], g_ref[...]) * (1.0 + sc_ref[0]) + sh_ref[0]
    p = _dot(h.astype(bf16), w_ref[...])
    qa_ref[...] = p[:, OFF_QA:OFF_QA + W_QA].astype(bf16)
    ka_ref[...] = p[:, OFF_KA:OFF_KA + W_KA]
    va_ref[...] = p[:, OFF_VA:OFF_VA + W_VA]
    qb_ref[...] = p[:, OFF_QB:OFF_QB + W_B].astype(bf16)
    kb_ref[...] = p[:, OFF_KB:OFF_KB + W_B]
    vb_ref[...] = p[:, OFF_VB:OFF_VB + W_B]
    cqn = _rms(p[:, OFF_CQ:OFF_CQ + Q_LORA], gcq_ref[...])
    qc_ref[...] = _dot(cqn.astype(bf16), wuq_ref[...]).astype(bf16)
    ckv_ref[...] = _rms(p[:, OFF_CKV:OFF_CKV + KV_LORA], gckv_ref[...])
    kr_ref[...] = p[:, OFF_KR:OFF_KR + QK_ROPE]


def _inproj_ctx(x, g, shift, scale, w, gcq, gckv, wuq):
    tm = TM_TOK
    row = lambda i: (i, 0)
    const = lambda i: (0, 0)
    widths = (W_QA, W_KA, W_VA, W_B, W_B, W_B, H_C * QC_PAD, KV_LORA, QK_ROPE)
    dtypes = (bf16, f32, f32, bf16, f32, f32, bf16, f32, f32)
    return pl.pallas_call(
        _inproj_ctx_kernel,
        grid=(T_CTX // tm,),
        in_specs=[pl.BlockSpec((tm, D_MODEL), row),
                  pl.BlockSpec((1, D_MODEL), const),
                  pl.BlockSpec((1, 1, D_MODEL), lambda i: (0, 0, 0)),
                  pl.BlockSpec((1, 1, D_MODEL), lambda i: (0, 0, 0)),
                  pl.BlockSpec((D_MODEL, NW_CTX), const),
                  pl.BlockSpec((1, Q_LORA), const),
                  pl.BlockSpec((1, KV_LORA), const),
                  pl.BlockSpec((Q_LORA, H_C * QC_PAD), const)],
        out_specs=[pl.BlockSpec((tm, wd), row) for wd in widths],
        out_shape=[jax.ShapeDtypeStruct((T_CTX, wd), dt) for wd, dt in zip(widths, dtypes)],
        compiler_params=_cparams("arbitrary"),
        name="inproj_ctx",
    )(x, g, shift, scale, w, gcq, gckv, wuq)


def _inproj_lat_kernel(x_ref, g_ref, sh_ref, sc_ref, w_ref, gcq_ref, gckv_ref, wuq_ref,
                       cosa_ref, sina_ref, cosq_ref, sinq_ref, cosr_ref, sinr_ref,
                       qa_ref, ka_ref, va_ref, qb_ref, kb_ref, vb_ref, qc_ref, ckv_ref, kr_ref):
    h = _rms(x_ref[...], g_ref[...]) * (1.0 + sc_ref[0]) + sh_ref[0]
    p = _dot(h.astype(bf16), w_ref[...])
    cosa = cosa_ref[...]
    sina = sina_ref[...]
    qa = p[:, OFF_QA:OFF_QA + W_QA] * cosa + p[:, OFF_QA_P:OFF_QA_P + W_QA] * sina
    ka = p[:, OFF_KA:OFF_KA + W_KA] * cosa[:, :W_KA] + p[:, OFF_KA_P:OFF_KA_P + W_KA] * sina[:, :W_KA]
    kr = p[:, OFF_KR:OFF_KR + QK_ROPE] * cosr_ref[...] + p[:, OFF_KR_P:OFF_KR_P + QK_ROPE] * sinr_ref[...]
    qa_ref[...] = qa.astype(bf16)
    ka_ref[...] = ka.astype(bf16)
    va_ref[...] = p[:, OFF_VA:OFF_VA + W_VA].astype(bf16)
    qb_ref[...] = p[:, OFF_QB:OFF_QB + W_B].astype(bf16)
    kb_ref[...] = p[:, OFF_KB:OFF_KB + W_B].astype(bf16)
    vb_ref[...] = p[:, OFF_VB:OFF_VB + W_B].astype(bf16)
    cqn = _rms(p[:, OFF_CQ:OFF_CQ + Q_LORA], gcq_ref[...])
    q2 = _dot(cqn.astype(bf16), wuq_ref[...])
    nq = H_C * QC_PAD
    qc_ref[...] = (q2[:, :nq] * cosq_ref[...] + q2[:, nq:] * sinq_ref[...]).astype(bf16)
    ckv_ref[...] = _rms(p[:, OFF_CKV:OFF_CKV + KV_LORA], gckv_ref[...]).astype(bf16)
    kr_ref[...] = kr.astype(bf16)


def _inproj_lat(x, g, shift, scale, w, gcq, gckv, wuq2, tabs):
    tm = TM_LAT_IN
    per_b = DEC_SEQ // tm
    row0 = T_CTX // tm
    xrow = lambda i: (row0 + i, 0)
    row = lambda i: (i, 0)
    const = lambda i: (0, 0)
    grp = lambda i: (1 + i // per_b, 0, 0)
    pos = lambda i: (i % per_b, 0)
    cosa, sina, cosq, sinq, cosr, sinr = tabs
    widths = (W_QA, W_KA, W_VA, W_B, W_B, W_B, H_C * QC_PAD, KV_LORA, QK_ROPE)
    return pl.pallas_call(
        _inproj_lat_kernel,
        grid=(T_LAT // tm,),
        in_specs=[pl.BlockSpec((tm, D_MODEL), xrow),
                  pl.BlockSpec((1, D_MODEL), const),
                  pl.BlockSpec((1, 1, D_MODEL), grp),
                  pl.BlockSpec((1, 1, D_MODEL), grp),
                  pl.BlockSpec((D_MODEL, NW_LAT), const),
                  pl.BlockSpec((1, Q_LORA), const),
                  pl.BlockSpec((1, KV_LORA), const),
                  pl.BlockSpec((Q_LORA, 2 * H_C * QC_PAD), const),
                  pl.BlockSpec((tm, W_QA), pos), pl.BlockSpec((tm, W_QA), pos),
                  pl.BlockSpec((tm, H_C * QC_PAD), pos), pl.BlockSpec((tm, H_C * QC_PAD), pos),
                  pl.BlockSpec((tm, QK_ROPE), pos), pl.BlockSpec((tm, QK_ROPE), pos)],
        out_specs=[pl.BlockSpec((tm, wd), row) for wd in widths],
        out_shape=[jax.ShapeDtypeStruct((T_LAT, wd), bf16) for wd in widths],
        compiler_params=_cparams("arbitrary"),
        name="inproj_lat",
    )(x, g, shift, scale, w, gcq, gckv, wuq2, cosa, sina, cosq, sinq, cosr, sinr)


def _ctx_attn_kernel(sink_ref, qa_ref, ka_ref, va_ref, qb_ref, kb_ref, vb_ref, qc_ref, ckv_ref, kr_ref,
                     wukv_ref, wout_ref, x_ref, gate_ref, o_ref, o_scr, s_scr, p_scr):
    n = SEQ
    scale = HEAD_DIM ** -0.5
    scale_c = (QK_NOPE + QK_ROPE) ** -0.5
    ka = ka_ref[...].astype(bf16)
    va = va_ref[...].astype(bf16)
    kb = kb_ref[...].astype(bf16)
    vb = vb_ref[...].astype(bf16)
    kv = _dot(ckv_ref[...].astype(bf16), wukv_ref[...]).astype(bf16)
    kr = kr_ref[...].astype(bf16)
    for h in range(H_A):
        g = h // G_A
        q = qa_ref[:, h * HEAD_DIM:(h + 1) * HEAD_DIM]
        s_scr[h * n:(h + 1) * n, :] = _dot_nt(q, ka[:, g * HEAD_DIM:(g + 1) * HEAD_DIM]) * scale
    for h in range(H_B):
        sl = slice(h * HEAD_DIM, (h + 1) * HEAD_DIM)
        s_scr[(H_A + h) * n:(H_A + h + 1) * n, :] = _dot_nt(qb_ref[:, sl], kb[:, sl]) * scale
    for h in range(H_C):
        qn = qc_ref[:, h * QC_PAD:h * QC_PAD + QK_NOPE]
        qr = qc_ref[:, h * QC_PAD + QK_NOPE:h * QC_PAD + QK_NOPE + QK_ROPE]
        c0 = h * (QK_NOPE + V_C)
        r0 = (H_A + H_B + h) * n
        s_scr[r0:r0 + n, :] = (_dot_nt(qn, kv[:, c0:c0 + QK_NOPE]) + _dot_nt(qr, kr)) * scale_c
    for pair in range((H_A + H_B + H_C) // 2):
        h0 = 2 * pair
        sinks = ((sink_ref[h0], n), (sink_ref[h0 + 1], n)) if h0 < H_A else None
        _softmax_rows(s_scr, p_scr, slice(h0 * n, (h0 + 2) * n), sinks)
    for h in range(H_A):
        g = h // G_A
        o_scr[:, h * HEAD_DIM:(h + 1) * HEAD_DIM] = _dot(p_scr[h * n:(h + 1) * n, :],
                                                         va[:, g * HEAD_DIM:(g + 1) * HEAD_DIM])
    for h in range(H_B):
        sl = slice(h * HEAD_DIM, (h + 1) * HEAD_DIM)
        o_scr[:, W_QA + h * HEAD_DIM:W_QA + (h + 1) * HEAD_DIM] = _dot(p_scr[(H_A + h) * n:(H_A + h + 1) * n, :],
                                                                     vb[:, sl])
    for h in range(H_C):
        c0 = h * (QK_NOPE + V_C)
        r0 = (H_A + H_B + h) * n
        off = W_QA + W_B + h * V_C
        o_scr[:, off:off + V_C] = _dot(p_scr[r0:r0 + n, :], kv[:, c0 + QK_NOPE:c0 + QK_NOPE + V_C])
    y = _dot(o_scr[...].astype(bf16), wout_ref[...])
    o_ref[...] = x_ref[...] + gate_ref[0] * y


def _ctx_attn(sink, proj, wukv, wout, x, gate):
    qa, ka, va, qb, kb, vb, qc, ckv, kr = proj
    row = lambda b: (b, 0)
    const = lambda b: (0, 0)
    in_specs = [pl.BlockSpec(memory_space=pltpu.SMEM)]
    in_specs += [pl.BlockSpec((SEQ, a.shape[1]), row) for a in proj]
    in_specs += [pl.BlockSpec((KV_LORA, H_C * (QK_NOPE + V_C)), const),
                 pl.BlockSpec((D_MODEL, D_MODEL), const),
                 pl.BlockSpec((SEQ, D_MODEL), row),
                 pl.BlockSpec((1, 1, D_MODEL), lambda b: (0, 0, 0))]
    return pl.pallas_call(
        _ctx_attn_kernel,
        grid=(BATCH,),
        in_specs=in_specs,
        out_specs=pl.BlockSpec((SEQ, D_MODEL), row),
        out_shape=jax.ShapeDtypeStruct((T_CTX, D_MODEL), f32),
        scratch_shapes=[pltpu.VMEM((SEQ, D_MODEL), f32),
                        pltpu.VMEM(((H_A + H_B + H_C) * SEQ, SEQ), f32),
                        pltpu.VMEM(((H_A + H_B + H_C) * SEQ, SEQ), bf16)],
        compiler_params=_cparams("arbitrary"),
        name="ctx_attn",
    )(sink, qa, ka, va, qb, kb, vb, qc, ckv, kr, wukv, wout, x, gate)


def _lat_attn_kernel(sink_ref, qa_ref, qb_ref, qc_ref, ka_ref, va_ref, kb_ref, vb_ref, ckv_ref, kr_ref,
                     cak_ref, cav_ref, cbk_ref, cbv_ref, cckv_ref, ckr_ref, bias_ref,
                     wukv_ref, wout_ref, x_ref, gate_ref, o_ref, o_scr, kv_scr, sa, pa, sb, pb, sc, pc):
    qi = pl.program_id(1)
    nb = DEC_SEQ // BLOCK
    scale = HEAD_DIM ** -0.5

    @pl.when(qi == 0)
    def _():
        kv_scr[0:DEC_SEQ, :] = _dot(ckv_ref[...], wukv_ref[...]).astype(bf16)
        kv_scr[DEC_SEQ:DEC_SEQ + PAST_LEN, :] = _dot(cckv_ref[0, 0].astype(bf16), wukv_ref[...]).astype(bf16)

    def blk(ref, j):
        idx = jnp.clip(qi + j, 0, nb - 1)
        return ref[pl.ds(pl.multiple_of(idx * BLOCK, BLOCK), BLOCK), :]

    ka = jnp.concatenate([blk(ka_ref, -1), blk(ka_ref, 0), blk(ka_ref, 1), cak_ref[0, 0].astype(bf16)], axis=0)
    va = jnp.concatenate([blk(va_ref, -1), blk(va_ref, 0), blk(va_ref, 1), cav_ref[0, 0].astype(bf16)], axis=0)
    nk_a = 3 * BLOCK + PAST_LEN
    r = lax.broadcasted_iota(jnp.int32, (BLOCK, nk_a), 0)
    c = lax.broadcasted_iota(jnp.int32, (BLOCK, nk_a), 1)
    valid = (((c < BLOCK) & (c >= r) & (qi > 0))
             | ((c >= BLOCK) & (c < 2 * BLOCK))
             | ((c >= 2 * BLOCK) & (c < 3 * BLOCK) & (c - 2 * BLOCK <= r) & (qi < nb - 1))
             | (c >= 3 * BLOCK))
    for h in range(H_A):
        g = h // G_A
        q = qa_ref[:, h * HEAD_DIM:(h + 1) * HEAD_DIM]
        s = _dot_nt(q, ka[:, g * HEAD_DIM:(g + 1) * HEAD_DIM]) * scale
        sa[h * BLOCK:(h + 1) * BLOCK, :] = jnp.where(valid, s, NEG)

    cbk = cbk_ref[0, 0].astype(bf16)
    cbv = cbv_ref[0, 0].astype(bf16)
    rows_per_blk = BLOCK // GRID_W
    nloc = NA_ROWS * GRID_W
    vcats = []
    for half in range(rows_per_blk):
        grow = qi * rows_per_blk + half
        start = jnp.clip(grow - NA_ROWS // 2, 0, ROWS - NA_ROWS)
        kloc = kb_ref[pl.ds(pl.multiple_of(start * GRID_W, GRID_W), nloc), :]
        vloc = vb_ref[pl.ds(pl.multiple_of(start * GRID_W, GRID_W), nloc), :]
        vcats.append(jnp.concatenate([vloc, cbv], axis=0))
        qrows = slice(half * GRID_W, (half + 1) * GRID_W)
        dr0 = start - grow + (NA_ROWS - 1)
        for h in range(H_B):
            sl = slice(h * HEAD_DIM, (h + 1) * HEAD_DIM)
            q = qb_ref[qrows, sl]
            bias = jnp.concatenate([bias_ref[h, dr0 + 2 * j] for j in range(NA_ROWS // 2)], axis=1)
            s_loc = _dot_nt(q, kloc[:, sl]) * scale + bias
            s_ctx = _dot_nt(q, cbk[:, sl]) * scale
            r0 = (half * H_B + h) * GRID_W
            sb[r0:r0 + GRID_W, :] = jnp.concatenate([s_loc, s_ctx], axis=1)

    kr = jnp.concatenate([kr_ref[...], ckr_ref[0, 0].astype(bf16)], axis=0)
    scale_c = (QK_NOPE + QK_ROPE) ** -0.5
    for h in range(H_C):
        qn = qc_ref[:, h * QC_PAD:h * QC_PAD + QK_NOPE]
        qr = qc_ref[:, h * QC_PAD + QK_NOPE:h * QC_PAD + QK_NOPE + QK_ROPE]
        c0 = h * (QK_NOPE + V_C)
        sc[h * BLOCK:(h + 1) * BLOCK, :] = (_dot_nt(qn, kv_scr[:, c0:c0 + QK_NOPE]) + _dot_nt(qr, kr)) * scale_c

    for pair in range(H_A // 2):
        h0 = 2 * pair
        _softmax_rows(sa, pa, slice(h0 * BLOCK, (h0 + 2) * BLOCK), ((sink_ref[h0], BLOCK), (sink_ref[h0 + 1], BLOCK)))
    for blk2 in range(rows_per_blk * H_B // 2):
        _softmax_rows(sb, pb, slice(blk2 * 2 * GRID_W, (blk2 + 1) * 2 * GRID_W))
    for h in range(H_C):
        _softmax_rows(sc, pc, slice(h * BLOCK, (h + 1) * BLOCK))

    for h in range(H_A):
        g = h // G_A
        o_scr[:, h * HEAD_DIM:(h + 1) * HEAD_DIM] = _dot(pa[h * BLOCK:(h + 1) * BLOCK, :],
                                                         va[:, g * HEAD_DIM:(g + 1) * HEAD_DIM])
    for half in range(rows_per_blk):
        qrows = slice(half * GRID_W, (half + 1) * GRID_W)
        for h in range(H_B):
            sl = slice(h * HEAD_DIM, (h + 1) * HEAD_DIM)
            r0 = (half * H_B + h) * GRID_W
            o_scr[qrows, W_QA + h * HEAD_DIM:W_QA + (h + 1) * HEAD_DIM] = _dot(pb[r0:r0 + GRID_W, :],
                                                                             vcats[half][:, sl])
    for h in range(H_C):
        c0 = h * (QK_NOPE + V_C)
        off = W_QA + W_B + h * V_C
        o_scr[:, off:off + V_C] = _dot(pc[h * BLOCK:(h + 1) * BLOCK, :], kv_scr[:, c0 + QK_NOPE:c0 + QK_NOPE + V_C])

    y = _dot(o_scr[...].astype(bf16), wout_ref[...])
    o_ref[...] = x_ref[...] + gate_ref[0] * y


def _lat_attn(layer, sink, proj, caches, bias_tab, wukv, wout, x, gate):
    qa, ka, va, qb, kb, vb, qc, ckv, kr = proj
    nb = DEC_SEQ // BLOCK
    qrow = lambda b, q: (b * nb + q, 0)
    xrow = lambda b, q: (T_CTX // BLOCK + b * nb + q, 0)
    brow = lambda b, q: (b, 0)
    const = lambda b, q: (0, 0)
    cidx = lambda b, q: (b, layer, 0, 0)
    in_specs = [pl.BlockSpec(memory_space=pltpu.SMEM)]
    in_specs += [pl.BlockSpec((BLOCK, a.shape[1]), qrow) for a in (qa, qb, qc)]
    in_specs += [pl.BlockSpec((DEC_SEQ, a.shape[1]), brow) for a in (ka, va, kb, vb, ckv, kr)]
    in_specs += [pl.BlockSpec((1, 1, PAST_LEN, a.shape[3]), cidx) for a in caches]
    in_specs += [pl.BlockSpec(bias_tab.shape, lambda b, q: (0, 0, 0, 0)),
                 pl.BlockSpec((KV_LORA, H_C * (QK_NOPE + V_C)), const),
                 pl.BlockSpec((D_MODEL, D_MODEL), const),
                 pl.BlockSpec((BLOCK, D_MODEL), xrow),
                 pl.BlockSpec((1, 1, D_MODEL), lambda b, q: (1 + b, 0, 0))]
    return pl.pallas_call(
        _lat_attn_kernel,
        grid=(DEC_BATCH, nb),
        in_specs=in_specs,
        out_specs=pl.BlockSpec((BLOCK, D_MODEL), qrow),
        out_shape=jax.ShapeDtypeStruct((T_LAT, D_MODEL), f32),
        scratch_shapes=[pltpu.VMEM((BLOCK, D_MODEL), f32),
                        pltpu.VMEM((DEC_SEQ + PAST_LEN, H_C * (QK_NOPE + V_C)), bf16)]
        + [pltpu.VMEM(shape, dt) for shape in ((H_A * BLOCK, 3 * BLOCK + PAST_LEN),
                                               (H_B * BLOCK, NA_ROWS * GRID_W + PAST_LEN),
                                               (H_C * BLOCK, DEC_SEQ + PAST_LEN)) for dt in (f32, bf16)],
        compiler_params=_cparams("arbitrary", "arbitrary"),
        name="lat_attn",
    )(sink, qa, qb, qc, ka, va, kb, vb, ckv, kr, *caches, bias_tab, wukv, wout, x, gate)


def _router_kernel(x_ref, g_ref, sh_ref, sc_ref, wr_ref, br_ref, h_ref, e_ref, gt_ref):
    h = _rms(x_ref[...], g_ref[...]) * (1.0 + sc_ref[0]) + sh_ref[0]
    _store_row_tiles(h_ref, h)
    logits = jnp.dot(h, wr_ref[...], preferred_element_type=f32, precision=lax.Precision.HIGHEST) + br_ref[...]
    lane = lax.broadcasted_iota(jnp.int32, logits.shape, 1).astype(f32)
    l = jnp.where(lane < N_EXPERTS, logits, -jnp.inf)
    tops, idxs = [], []
    for _ in range(TOP_K):
        m = jnp.max(l, axis=-1, keepdims=True)
        idx = jnp.min(jnp.where(l == m, lane, float(LANE)), axis=-1, keepdims=True)
        tops.append(m)
        idxs.append(idx)
        l = jnp.where(lane == idx, -jnp.inf, l)
    ex = [jnp.exp(t - tops[0]) for t in tops]
    den = ex[0] + ex[1] + ex[2] + ex[3]
    e_out = jnp.zeros(logits.shape, f32)
    g_out = jnp.zeros(logits.shape, f32)
    for k in range(TOP_K):
        e_out = jnp.where(lane == k, idxs[k], e_out)
        g_out = jnp.where(lane == k, ex[k] / den, g_out)
    e_ref[...] = e_out.astype(jnp.int32)
    gt_ref[...] = g_out


def _group_of_tile(i):
    per_b = DEC_SEQ // TM_TOK
    n_ctx = T_CTX // TM_TOK
    return jnp.where(i < n_ctx, 0, 1 + (i - n_ctx) // per_b)


def _router(x, g, shift, scale, wr, br):
    tm = TM_TOK
    row = lambda i: (i, 0)
    const = lambda i: (0, 0)
    grp = lambda i: (_group_of_tile(i), 0, 0)
    return pl.pallas_call(
        _router_kernel,
        grid=(T_ALL // tm,),
        in_specs=[pl.BlockSpec((tm, D_MODEL), row),
                  pl.BlockSpec((1, D_MODEL), const),
                  pl.BlockSpec((1, 1, D_MODEL), grp),
                  pl.BlockSpec((1, 1, D_MODEL), grp),
                  pl.BlockSpec((D_MODEL, LANE), const),
                  pl.BlockSpec((1, LANE), const)],
        out_specs=[pl.BlockSpec((tm * ROW_TILE, LANE), row), pl.BlockSpec((tm, LANE), row),
                   pl.BlockSpec((tm, LANE), row)],
        out_shape=[jax.ShapeDtypeStruct((T_ALL * ROW_TILE, LANE), f32),
                   jax.ShapeDtypeStruct((T_ALL, LANE), jnp.int32),
                   jax.ShapeDtypeStruct((T_ALL, LANE), f32)],
        compiler_params=_cparams("arbitrary"),
        name="router",
    )(x, g, shift, scale, wr, br)


def _dispatch_kernel(tok_ref, nu_ref, h_hbm, o_ref, hv, xg, hsem):
    tm = TM_MOE
    i = pl.program_id(0)

    @pl.when(i == 0)
    def _():
        resident = pltpu.make_async_copy(h_hbm, hv, hsem.at[0])
        resident.start()
        resident.wait()

    @pl.when(i < nu_ref[0])
    def _():
        for r in range(tm):
            t = tok_ref[i * tm + r]
            xg[pl.ds(r * ROW_TILE, ROW_TILE), :] = hv[pl.ds(pl.multiple_of(t * ROW_TILE, ROW_TILE), ROW_TILE), :]
        o_ref[...] = _load_row_tiles(xg).astype(bf16)

    @pl.when(i >= nu_ref[0])
    def _():
        o_ref[...] = jnp.zeros_like(o_ref)


def _dispatch(row_tok, n_used, h):
    tm = TM_MOE
    return pl.pallas_call(
        _dispatch_kernel,
        grid_spec=pltpu.PrefetchScalarGridSpec(
            num_scalar_prefetch=2,
            grid=(N_MOE_BLOCKS,),
            in_specs=[pl.BlockSpec(memory_space=pl.ANY)],
            out_specs=pl.BlockSpec((tm, D_MODEL), lambda i, tok, nu: (i, 0)),
            scratch_shapes=[pltpu.VMEM((T_ALL * ROW_TILE, LANE), f32), pltpu.VMEM((tm * ROW_TILE, LANE), f32),
                            pltpu.SemaphoreType.DMA((1,))]),
        out_shape=jax.ShapeDtypeStruct((N_MOE_BLOCKS * tm, D_MODEL), bf16),
        compiler_params=_cparams("arbitrary"),
        name="dispatch",
    )(row_tok, n_used, h)


def _moe_kernel(layer, be_ref, nu_ref, nxt_ref, dst_ref, x_ref, wgu_hbm, bgu_ref, wd_hbm, bd_ref, y_hbm,
                y0, y1, wgu_st, wd_st, wgu_bf, wd_bf, wsem, ssem):
    tm = TM_MOE
    i = pl.program_id(0)
    nb = pl.num_programs(0)
    used = i < nu_ref[0]
    yb = (y0, y1)

    def out_tile(row):
        return pl.ds(pl.multiple_of(row * ROW_TILE, ROW_TILE), ROW_TILE)

    def scatter_desc(buf, r, dst_row, s):
        return pltpu.make_async_copy(buf.at[out_tile(r)], y_hbm.at[out_tile(dst_row)], ssem.at[s])

    def scatter_wait(s):
        pltpu.make_async_copy(yb[s], y_hbm.at[pl.ds(0, tm * ROW_TILE)], ssem.at[s]).wait()

    def scatter_start(blk, s, unrolled):
        if unrolled:
            for r in range(tm):
                scatter_desc(yb[s], r, dst_ref[(blk + 1) * tm + r], s).start(priority=r % 2)
        else:
            def body(r, carry):
                scatter_desc(yb[s], r, dst_ref[(blk + 1) * tm + r], s).start()
                return carry
            lax.fori_loop(0, tm, body, 0, unroll=8)

    def weight_copies(e):
        return (pltpu.make_async_copy(wgu_hbm.at[layer, e], wgu_st, wsem.at[0]),
                pltpu.make_async_copy(wd_hbm.at[layer, e], wd_st, wsem.at[1]))

    @pl.when(i == 0)
    def _():
        for s in range(2):
            yb[s][...] = jnp.zeros_like(yb[s])
            dummy = pltpu.make_async_copy(yb[s], y_hbm.at[pl.ds((N_ASSIGN + s * tm) * ROW_TILE, tm * ROW_TILE)],
                                          ssem.at[s])
            dummy.start()
            dummy.wait()
        for cp in weight_copies(be_ref[0]):
            cp.start()

    first = jnp.logical_and(used, jnp.logical_or(i == 0, be_ref[i] != be_ref[jnp.maximum(i - 1, 0)]))

    @pl.when(first)
    def _():
        for cp in weight_copies(0):
            cp.wait()
        wgu_bf[...] = wgu_st[...].astype(bf16)
        wd_bf[...] = wd_st[...].astype(bf16)

        @pl.when(nxt_ref[i] >= 0)
        def _():
            for cp in weight_copies(nxt_ref[i]):
                cp.start()

    def step(par):
        cur, oth = par, 1 - par

        @pl.when(i >= 1)
        def _():
            scatter_wait(cur)

        @pl.when(used)
        def _():
            scatter_start(i - 1, oth, unrolled=True)
            gu = _dot(x_ref[...], wgu_bf[...]) + bgu_ref[0, 0]
            x_glu = jnp.minimum(gu[:, :D_FF], SWIGLU_LIMIT)
            x_lin = jnp.clip(gu[:, D_FF:], -SWIGLU_LIMIT, SWIGLU_LIMIT)
            act = x_glu * jax.nn.sigmoid(SWIGLU_ALPHA * x_glu) * (x_lin + 1.0)
            _store_row_tiles(yb[cur], _dot(act.astype(bf16), wd_bf[...]) + bd_ref[0, 0])

        @pl.when(jnp.logical_and(jnp.logical_not(used), i + 1 < nb))
        def _():
            scatter_start(i - 1, oth, unrolled=False)

        @pl.when(i == nb - 1)
        def _():
            scatter_start(i - 1, oth, unrolled=False)
            scatter_wait(oth)

    @pl.when(i % 2 == 0)
    def _():
        step(0)

    @pl.when(i % 2 == 1)
    def _():
        step(1)


def _moe(layer, routing, h, w_gu, b_gu, w_down, b_down):
    tm = TM_MOE
    block_e, n_used, nxt_e, row_tok, row_dst = routing
    xs = _dispatch(row_tok, n_used, h)
    ex4 = lambda i, be, nu, nxt, dst: (layer, be[i], 0, 0)
    return pl.pallas_call(
        functools.partial(_moe_kernel, layer),
        grid_spec=pltpu.PrefetchScalarGridSpec(
            num_scalar_prefetch=4,
            grid=(N_MOE_BLOCKS,),
            in_specs=[pl.BlockSpec((tm, D_MODEL), lambda i, be, nu, nxt, dst: (i, 0)),
                      pl.BlockSpec(memory_space=pl.ANY),
                      pl.BlockSpec((1, 1, 1, 2 * D_FF), ex4),
                      pl.BlockSpec(memory_space=pl.ANY),
                      pl.BlockSpec((1, 1, 1, D_MODEL), ex4)],
            out_specs=pl.BlockSpec(memory_space=pl.ANY),
            scratch_shapes=[pltpu.VMEM((tm * ROW_TILE, LANE), f32), pltpu.VMEM((tm * ROW_TILE, LANE), f32),
                            pltpu.VMEM((D_MODEL, 2 * D_FF), f32), pltpu.VMEM((D_FF, D_MODEL), f32),
                            pltpu.VMEM((D_MODEL, 2 * D_FF), bf16), pltpu.VMEM((D_FF, D_MODEL), bf16),
                            pltpu.SemaphoreType.DMA((2,)), pltpu.SemaphoreType.DMA((2,))]),
        out_shape=jax.ShapeDtypeStruct(((N_ASSIGN + 2 * tm) * ROW_TILE, LANE), f32),
        compiler_params=_cparams("arbitrary"),
        name="moe",
    )(block_e, n_used, nxt_e, row_dst, xs, w_gu, b_gu.reshape(DEPTH, N_EXPERTS, 1, 2 * D_FF),
      w_down, b_down.reshape(DEPTH, N_EXPERTS, 1, D_MODEL))


def _combine_kernel(final, x_ref, y0_ref, y1_ref, y2_ref, y3_ref, gt_ref, gate_ref, gf_ref, o_ref):
    gt = gt_ref[...]
    f = gt[:, 0:1] * _load_row_tiles(y0_ref)
    for k, y_ref in ((1, y1_ref), (2, y2_ref), (3, y3_ref)):
        f = f + gt[:, k:k + 1] * _load_row_tiles(y_ref)
    out = x_ref[...] + gate_ref[0] * f
    if final:
        out = _rms(out, gf_ref[...])
    o_ref[...] = out


def _combine(final, x, y, gates, gate, g_final):
    tm = TM_TOK
    nt = T_ALL // tm
    row = lambda i: (i, 0)
    const = lambda i: (0, 0)
    grp = lambda i: (_group_of_tile(i), 0, 0)
    ysel = [pl.BlockSpec((tm * ROW_TILE, LANE), functools.partial(lambda k, i: (k * nt + i, 0), k))
            for k in range(TOP_K)]
    return pl.pallas_call(
        functools.partial(_combine_kernel, final),
        grid=(nt,),
        in_specs=[pl.BlockSpec((tm, D_MODEL), row)] + ysel +
                 [pl.BlockSpec((tm, LANE), row),
                  pl.BlockSpec((1, 1, D_MODEL), grp),
                  pl.BlockSpec((1, D_MODEL), const)],
        out_specs=pl.BlockSpec((tm, D_MODEL), row),
        out_shape=jax.ShapeDtypeStruct((T_ALL, D_MODEL), f32),
        compiler_params=_cparams("arbitrary"),
        name="combine",
    )(x, y, y, y, y, gates, gate, g_final)


def _rope_head_tables(d):
    nf = d // 4
    half = d // 2
    t = np.arange(DEC_SEQ)
    inv = ROPE_BASE ** (-np.arange(nf, dtype=np.float32) / nf)
    i = np.arange(d)
    pos = np.where(i[None, :] < half, (t // GRID_W)[:, None], (t % GRID_W)[:, None]).astype(np.float32)
    ang = pos * inv[i % nf][None, :].astype(np.float32)
    first = (i % half) < nf
    cos = np.cos(ang)
    sin = np.where(first[None, :], -np.sin(ang), np.sin(ang))
    partner = np.where(first, i + nf, i - nf)
    return cos.astype(np.float32), sin.astype(np.float32), partner


def _rope_tables():
    cos64, sin64, _ = _rope_head_tables(HEAD_DIM)
    cos32, sin32, _ = _rope_head_tables(QK_ROPE)
    cosa = np.tile(cos64, (1, H_A))
    sina = np.tile(sin64, (1, H_A))
    cosq1 = np.concatenate([np.ones((DEC_SEQ, QK_NOPE), np.float32), cos32,
                            np.ones((DEC_SEQ, QC_PAD - QK_NOPE - QK_ROPE), np.float32)], axis=1)
    sinq1 = np.concatenate([np.zeros((DEC_SEQ, QK_NOPE), np.float32), sin32,
                            np.zeros((DEC_SEQ, QC_PAD - QK_NOPE - QK_ROPE), np.float32)], axis=1)
    cosq = np.tile(cosq1, (1, H_C))
    sinq = np.tile(sinq1, (1, H_C))
    return tuple(jnp.asarray(a) for a in (cosa, sina, cosq, sinq, cos32, sin32))


def _pad_cols(w, n):
    return jnp.pad(w, ((0, 0), (0, n - w.shape[1])))


def _layer_weights(w_in, w_uq):
    cuts = np.cumsum((W_QA, W_KA, W_VA, W_B, W_B, W_B, Q_LORA, KV_LORA, QK_ROPE))[:-1]
    qa, ka, va, qb, kb, vb, cq, ckv, kr = jnp.split(w_in, [int(c) for c in cuts], axis=1)
    _, _, p64 = _rope_head_tables(HEAD_DIM)
    _, _, p32 = _rope_head_tables(QK_ROPE)
    pa = np.concatenate([h * HEAD_DIM + p64 for h in range(H_A)])
    base = jnp.concatenate([qa, ka, va, _pad_cols(qb, 384), _pad_cols(kb, 384), _pad_cols(vb, 384), cq, ckv,
                            _pad_cols(kr, 128)], axis=1)
    w_ctx = base.astype(bf16)
    w_lat = jnp.concatenate([base, qa[:, pa], ka[:, pa[:W_KA]], _pad_cols(kr[:, p32], 128)], axis=1).astype(bf16)
    hq = QK_NOPE + QK_ROPE
    heads = [_pad_cols(w_uq[:, h * hq:(h + 1) * hq], QC_PAD) for h in range(H_C)]
    pq = np.concatenate([np.arange(QK_NOPE), QK_NOPE + p32])
    heads_p = [_pad_cols(w_uq[:, h * hq:(h + 1) * hq][:, pq], QC_PAD) for h in range(H_C)]
    wuq = jnp.concatenate(heads, axis=1).astype(bf16)
    wuq2 = jnp.concatenate(heads + heads_p, axis=1).astype(bf16)
    return w_ctx, w_lat, wuq, wuq2


def _bias_table(rpb):
    col = np.arange(GRID_W)
    col_start = np.clip(col - NA_COLS // 2, 0, GRID_W - NA_COLS)
    col_ok = (col[None, :] >= col_start[:, None]) & (col[None, :] < col_start[:, None] + NA_COLS)
    dc = np.clip(col[None, :] - col[:, None] + (NA_COLS - 1), 0, 2 * NA_COLS - 2)
    blocks = jnp.where(col_ok[None, None], rpb[:, :, dc], NEG).astype(f32)
    return jnp.concatenate([blocks[:, :-1], blocks[:, 1:]], axis=-1)


def _routing(top_e):
    tm = TM_MOE
    flat_e = top_e.T.reshape(N_ASSIGN)
    order = jnp.argsort(flat_e).astype(jnp.int32)
    experts = jnp.arange(N_EXPERTS, dtype=jnp.int32)
    counts = jnp.sum((flat_e[:, None] == experts[None, :]).astype(jnp.int32), axis=0)
    nblk = (counts + tm - 1) // tm
    blk_end = jnp.cumsum(nblk)
    blk_start = blk_end - nblk
    grp_start = jnp.cumsum(counts) - counts
    blocks = jnp.arange(N_MOE_BLOCKS, dtype=jnp.int32)
    block_e = jnp.minimum(jnp.sum((blk_end[None, :] <= blocks[:, None]).astype(jnp.int32), axis=1), N_EXPERTS - 1)
    n_used = blk_end[-1].astype(jnp.int32).reshape(1)
    sel = (block_e[:, None] == experts[None, :]).astype(jnp.int32)
    b_first = jnp.sum(sel * blk_start[None, :], axis=1)
    b_count = jnp.sum(sel * counts[None, :], axis=1)
    b_grp = jnp.sum(sel * grp_start[None, :], axis=1)
    r = jnp.arange(tm, dtype=jnp.int32)[None, :]
    off = (blocks - b_first)[:, None] * tm + r
    valid = (off < b_count[:, None]) & (blocks[:, None] < n_used[0])
    asg = order[jnp.clip(b_grp[:, None] + off, 0, N_ASSIGN - 1)]
    tok = jnp.where(valid, asg % T_ALL, 0)
    row_dst = jnp.where(valid, asg, N_ASSIGN + (blocks[:, None] % 2) * tm + r)
    row_dst = jnp.concatenate([N_ASSIGN + tm + r, row_dst], axis=0).reshape(-1)
    has = jnp.where(counts > 0, experts, N_EXPERTS)
    later = experts[None, :] > experts[:, None]
    nxt = jnp.min(jnp.where(later, has[None, :], N_EXPERTS), axis=1)
    nxt = jnp.where(nxt >= N_EXPERTS, -1, nxt)
    nxt_e = jnp.sum(sel * nxt[None, :], axis=1)
    i32 = lambda a: a.astype(jnp.int32)
    return i32(block_e), n_used, i32(nxt_e), i32(tok).reshape(-1), i32(row_dst)


def kernel(x_prompt, x_sample, cache_a_k, cache_a_v, cache_b_k, cache_b_v, cache_c_kv, cache_c_kr, c, c_ctx, w_ada, b_ada, g_attn, g_ffn, w_in, sink_a, rpb_b, g_cq, g_ckv, w_uq, w_ukv, w_out, w_router, b_router, w_gu, b_gu, w_down, b_down, g_final):
    x = jnp.concatenate([x_prompt.reshape(T_CTX, D_MODEL), x_sample.reshape(T_LAT, D_MODEL)], axis=0)
    cvec = jnp.concatenate([c_ctx[None, :], c, jnp.zeros((8 - N_GROUPS, D_MODEL), f32)], axis=0)
    mods = _ada(cvec, w_ada, b_ada)[:, :N_GROUPS].reshape(DEPTH, N_GROUPS, 6, 1, D_MODEL)
    tabs = _rope_tables()
    caches = (cache_a_k.reshape(DEC_BATCH, DEPTH, PAST_LEN, W_KA), cache_a_v.reshape(DEC_BATCH, DEPTH, PAST_LEN, W_VA),
              cache_b_k.reshape(DEC_BATCH, DEPTH, PAST_LEN, W_B), cache_b_v.reshape(DEC_BATCH, DEPTH, PAST_LEN, W_B),
              cache_c_kv, cache_c_kr)
    new = [[] for _ in range(6)]
    for layer in range(DEPTH):
        m = [mods[layer, :, j] for j in range(6)]
        w_ctx, w_lat, wuq, wuq2 = _layer_weights(w_in[layer], w_uq[layer])
        wukv = w_ukv[layer].astype(bf16)
        wout = w_out[layer].astype(bf16)
        g1 = g_attn[layer][None, :]
        gcq = g_cq[layer][None, :]
        gckv = g_ckv[layer][None, :]
        sink = sink_a[layer]

        pc = _inproj_ctx(x, g1, m[0], m[1], w_ctx, gcq, gckv, wuq)
        for lst, a in zip(new, (pc[1], pc[2], pc[4], pc[5], pc[7], pc[8])):
            lst.append(a)
        x_ctx = _ctx_attn(sink, pc, wukv, wout, x, m[2])

        plat = _inproj_lat(x, g1, m[0], m[1], w_lat, gcq, gckv, wuq2, tabs)
        x_lat = _lat_attn(layer, sink, plat, caches, _bias_table(rpb_b[layer]), wukv, wout, x, m[2])
        x = jnp.concatenate([x_ctx, x_lat], axis=0)

        wr = _pad_cols(w_router[layer], LANE)
        br = _pad_cols(b_router[layer][None, :], LANE)
        h2, top_e, gates = _router(x, g_ffn[layer][None, :], m[3], m[4], wr, br)
        y = _moe(layer, _routing(top_e[:, :TOP_K]), h2, w_gu, b_gu, w_down, b_down)
        x = _combine(layer == DEPTH - 1, x, y, gates, m[5], g_final[None, :])

    y_prompt = x[:T_CTX].reshape(BATCH, SEQ, D_MODEL)
    y_sample = x[T_CTX:].reshape(DEC_BATCH, DEC_SEQ, D_MODEL)
    shapes = ((KV_A, HEAD_DIM), (KV_A, HEAD_DIM), (H_B, HEAD_DIM), (H_B, HEAD_DIM), (KV_LORA,), (QK_ROPE,))
    outs = [jnp.stack([a.reshape((BATCH, SEQ) + s) for a in lst], axis=1) for lst, s in zip(new, shapes)]
    return (y_prompt, y_sample, *outs)
```

```python
import functools

import numpy as np
import jax
import jax.numpy as jnp
from jax import lax
from jax.experimental import pallas as pl
from jax.experimental.pallas import tpu as pltpu

D_MODEL = 1024
BATCH = 32
SEQ = 256
DEPTH = 2
DEC_BATCH = 2
DEC_SEQ = 1024
PAST_LEN = 512
GRID_W = 64
HEAD_DIM = 64
H_A = 6
KV_A = 2
G_A = H_A // KV_A
WINDOW = 128
BLOCK = 128
H_B = 5
NA_ROWS = 8
NA_COLS = 16
H_C = 5
Q_LORA = 384
KV_LORA = 256
QK_NOPE = 64
QK_ROPE = 32
V_C = 64
N_EXPERTS = 32
TOP_K = 4
D_FF = 1024
SWIGLU_ALPHA = 1.702
SWIGLU_LIMIT = 7.0
ROPE_BASE = 10000.0
EPS = 1e-6
NEG = -1e30

T_CTX = BATCH * SEQ
T_LAT = DEC_BATCH * DEC_SEQ
T_ALL = T_CTX + T_LAT
N_GROUPS = 1 + DEC_BATCH
LANE = 128
QC_PAD = 128
ROWS = DEC_SEQ // GRID_W

W_QA, W_KA, W_VA = H_A * HEAD_DIM, KV_A * HEAD_DIM, KV_A * HEAD_DIM
W_B = H_B * HEAD_DIM
OFF_QA = 0
OFF_KA = 384
OFF_VA = 512
OFF_QB = 640
OFF_KB = 1024
OFF_VB = 1408
OFF_CQ = 1792
OFF_CKV = 2176
OFF_KR = 2432
NW_CTX = 2560
OFF_QA_P = 2560
OFF_KA_P = 2944
OFF_KR_P = 3072
NW_LAT = 3200

TM_TOK = 256
TM_LAT_IN = 512
TM_MOE = 256
N_ASSIGN = T_ALL * TOP_K
N_MOE_BLOCKS = N_ASSIGN // TM_MOE + N_EXPERTS
VMEM_LIMIT = 56 * 1024 * 1024

f32 = jnp.float32
bf16 = jnp.bfloat16


def _cparams(*sem):
    return pltpu.CompilerParams(dimension_semantics=sem, vmem_limit_bytes=VMEM_LIMIT)


def _rms(xf, g):
    return xf * lax.rsqrt(jnp.mean(xf * xf, axis=-1, keepdims=True) + EPS) * g


def _dot(a, b):
    return jnp.dot(a, b, preferred_element_type=f32)


def _dot_nt(a, b):
    return lax.dot_general(a, b, (((1,), (1,)), ((), ())), preferred_element_type=f32)


ROW_TILE = D_MODEL // LANE


def _store_row_tiles(ref, val):
    n = val.shape[0]
    for c in range(ROW_TILE):
        ref[pl.ds(c, n, stride=ROW_TILE), :] = val[:, c * LANE:(c + 1) * LANE]


def _load_row_tiles(ref):
    n = ref.shape[0] // ROW_TILE
    return jnp.concatenate([ref[pl.ds(c, n, stride=ROW_TILE), :] for c in range(ROW_TILE)], axis=1)


def _softmax_rows(s_ref, p_ref, rows, sinks=None):
    s = s_ref[rows, :]
    m = jnp.max(s, axis=-1, keepdims=True)
    if sinks is not None:
        sink = jnp.concatenate([jnp.full((n, 1), v, f32) for v, n in sinks], axis=0)
        m = jnp.maximum(m, sink)
    p = jnp.exp(s - m)
    l = jnp.sum(p, axis=-1, keepdims=True)
    if sinks is not None:
        l = l + jnp.exp(sink - m)
    p_ref[rows, :] = (p * (1.0 / l)).astype(bf16)


def _ada_kernel(c_ref, w_ref, b_ref, o_ref):
    c = c_ref[...]
    s = c * jax.nn.sigmoid(c)
    o_ref[0] = jnp.dot(s, w_ref[0], preferred_element_type=f32, precision=lax.Precision.HIGHEST) + b_ref[0]


def _ada(cvec, w_ada, b_ada):
    tn = 1536
    return pl.pallas_call(
        _ada_kernel,
        grid=(DEPTH, 6 * D_MODEL // tn),
        in_specs=[pl.BlockSpec((8, D_MODEL), lambda l, j: (0, 0)),
                  pl.BlockSpec((1, D_MODEL, tn), lambda l, j: (l, 0, j)),
                  pl.BlockSpec((1, 1, tn), lambda l, j: (l, 0, j))],
        out_specs=pl.BlockSpec((1, 8, tn), lambda l, j: (l, 0, j)),
        out_shape=jax.ShapeDtypeStruct((DEPTH, 8, 6 * D_MODEL), f32),
        compiler_params=_cparams("arbitrary", "arbitrary"),
        name="ada",
    )(cvec, w_ada, b_ada.reshape(DEPTH, 1, 6 * D_MODEL))


def _inproj_ctx_kernel(x_ref, g_ref, sh_ref, sc_ref, w_ref, gcq_ref, gckv_ref, wuq_ref,
                       qa_ref, ka_ref, va_ref, qb_ref, kb_ref, vb_ref, qc_ref, ckv_ref, kr_ref):
    h = _rms(x_ref[...], g_ref[...]) * (1.0 + sc_ref[0]) + sh_ref[0]
    p = _dot(h.astype(bf16), w_ref[...])
    qa_ref[...] = p[:, OFF_QA:OFF_QA + W_QA].astype(bf16)
    ka_ref[...] = p[:, OFF_KA:OFF_KA + W_KA]
    va_ref[...] = p[:, OFF_VA:OFF_VA + W_VA]
    qb_ref[...] = p[:, OFF_QB:OFF_QB + W_B].astype(bf16)
    kb_ref[...] = p[:, OFF_KB:OFF_KB + W_B]
    vb_ref[...] = p[:, OFF_VB:OFF_VB + W_B]
    cqn = _rms(p[:, OFF_CQ:OFF_CQ + Q_LORA], gcq_ref[...])
    qc_ref[...] = _dot(cqn.astype(bf16), wuq_ref[...]).astype(bf16)
    ckv_ref[...] = _rms(p[:, OFF_CKV:OFF_CKV + KV_LORA], gckv_ref[...])
    kr_ref[...] = p[:, OFF_KR:OFF_KR + QK_ROPE]


def _inproj_ctx(x, g, shift, scale, w, gcq, gckv, wuq):
    tm = TM_TOK
    row = lambda i: (i, 0)
    const = lambda i: (0, 0)
    widths = (W_QA, W_KA, W_VA, W_B, W_B, W_B, H_C * QC_PAD, KV_LORA, QK_ROPE)
    dtypes = (bf16, f32, f32, bf16, f32, f32, bf16, f32, f32)
    return pl.pallas_call(
        _inproj_ctx_kernel,
        grid=(T_CTX // tm,),
        in_specs=[pl.BlockSpec((tm, D_MODEL), row),
                  pl.BlockSpec((1, D_MODEL), const),
                  pl.BlockSpec((1, 1, D_MODEL), lambda i: (0, 0, 0)),
                  pl.BlockSpec((1, 1, D_MODEL), lambda i: (0, 0, 0)),
                  pl.BlockSpec((D_MODEL, NW_CTX), const),
                  pl.BlockSpec((1, Q_LORA), const),
                  pl.BlockSpec((1, KV_LORA), const),
                  pl.BlockSpec((Q_LORA, H_C * QC_PAD), const)],
        out_specs=[pl.BlockSpec((tm, wd), row) for wd in widths],
        out_shape=[jax.ShapeDtypeStruct((T_CTX, wd), dt) for wd, dt in zip(widths, dtypes)],
        compiler_params=_cparams("arbitrary"),
        name="inproj_ctx",
    )(x, g, shift, scale, w, gcq, gckv, wuq)


def _inproj_lat_kernel(x_ref, g_ref, sh_ref, sc_ref, w_ref, gcq_ref, gckv_ref, wuq_ref,
                       cosa_ref, sina_ref, cosq_ref, sinq_ref, cosr_ref, sinr_ref,
                       qa_ref, ka_ref, va_ref, qb_ref, kb_ref, vb_ref, qc_ref, ckv_ref, kr_ref):
    h = _rms(x_ref[...], g_ref[...]) * (1.0 + sc_ref[0]) + sh_ref[0]
    p = _dot(h.astype(bf16), w_ref[...])
    cosa = cosa_ref[...]
    sina = sina_ref[...]
    qa = p[:, OFF_QA:OFF_QA + W_QA] * cosa + p[:, OFF_QA_P:OFF_QA_P + W_QA] * sina
    ka = p[:, OFF_KA:OFF_KA + W_KA] * cosa[:, :W_KA] + p[:, OFF_KA_P:OFF_KA_P + W_KA] * sina[:, :W_KA]
    kr = p[:, OFF_KR:OFF_KR + QK_ROPE] * cosr_ref[...] + p[:, OFF_KR_P:OFF_KR_P + QK_ROPE] * sinr_ref[...]
    qa_ref[...] = qa.astype(bf16)
    ka_ref[...] = ka.astype(bf16)
    va_ref[...] = p[:, OFF_VA:OFF_VA + W_VA].astype(bf16)
    qb_ref[...] = p[:, OFF_QB:OFF_QB + W_B].astype(bf16)
    kb_ref[...] = p[:, OFF_KB:OFF_KB + W_B].astype(bf16)
    vb_ref[...] = p[:, OFF_VB:OFF_VB + W_B].astype(bf16)
    cqn = _rms(p[:, OFF_CQ:OFF_CQ + Q_LORA], gcq_ref[...])
    q2 = _dot(cqn.astype(bf16), wuq_ref[...])
    nq = H_C * QC_PAD
    qc_ref[...] = (q2[:, :nq] * cosq_ref[...] + q2[:, nq:] * sinq_ref[...]).astype(bf16)
    ckv_ref[...] = _rms(p[:, OFF_CKV:OFF_CKV + KV_LORA], gckv_ref[...]).astype(bf16)
    kr_ref[...] = kr.astype(bf16)


def _inproj_lat(x, lat_row0, g, shift, scale, w, gcq, gckv, wuq2, tabs):
    tm = TM_LAT_IN
    per_b = DEC_SEQ // tm
    row0 = lat_row0 // tm
    xrow = lambda i: (row0 + i, 0)
    row = lambda i: (i, 0)
    const = lambda i: (0, 0)
    grp = lambda i: (1 + i // per_b, 0, 0)
    pos = lambda i: (i % per_b, 0)
    cosa, sina, cosq, sinq, cosr, sinr = tabs
    widths = (W_QA, W_KA, W_VA, W_B, W_B, W_B, H_C * QC_PAD, KV_LORA, QK_ROPE)
    return pl.pallas_call(
        _inproj_lat_kernel,
        grid=(T_LAT // tm,),
        in_specs=[pl.BlockSpec((tm, D_MODEL), xrow),
                  pl.BlockSpec((1, D_MODEL), const),
                  pl.BlockSpec((1, 1, D_MODEL), grp),
                  pl.BlockSpec((1, 1, D_MODEL), grp),
                  pl.BlockSpec((D_MODEL, NW_LAT), const),
                  pl.BlockSpec((1, Q_LORA), const),
                  pl.BlockSpec((1, KV_LORA), const),
                  pl.BlockSpec((Q_LORA, 2 * H_C * QC_PAD), const),
                  pl.BlockSpec((tm, W_QA), pos), pl.BlockSpec((tm, W_QA), pos),
                  pl.BlockSpec((tm, H_C * QC_PAD), pos), pl.BlockSpec((tm, H_C * QC_PAD), pos),
                  pl.BlockSpec((tm, QK_ROPE), pos), pl.BlockSpec((tm, QK_ROPE), pos)],
        out_specs=[pl.BlockSpec((tm, wd), row) for wd in widths],
        out_shape=[jax.ShapeDtypeStruct((T_LAT, wd), bf16) for wd in widths],
        compiler_params=_cparams("arbitrary"),
        name="inproj_lat",
    )(x, g, shift, scale, w, gcq, gckv, wuq2, cosa, sina, cosq, sinq, cosr, sinr)


def _ctx_attn_kernel(sink_ref, qa_ref, ka_ref, va_ref, qb_ref, kb_ref, vb_ref, qc_ref, ckv_ref, kr_ref,
                     wukv_ref, wout_ref, x_ref, gate_ref, o_ref, o_scr, s_scr, p_scr):
    n = SEQ
    scale = HEAD_DIM ** -0.5
    scale_c = (QK_NOPE + QK_ROPE) ** -0.5
    ka = ka_ref[...].astype(bf16)
    va = va_ref[...].astype(bf16)
    kb = kb_ref[...].astype(bf16)
    vb = vb_ref[...].astype(bf16)
    kv = _dot(ckv_ref[...].astype(bf16), wukv_ref[...]).astype(bf16)
    kr = kr_ref[...].astype(bf16)
    for h in range(H_A):
        g = h // G_A
        q = qa_ref[:, h * HEAD_DIM:(h + 1) * HEAD_DIM]
        s_scr[h * n:(h + 1) * n, :] = _dot_nt(q, ka[:, g * HEAD_DIM:(g + 1) * HEAD_DIM]) * scale
    for h in range(H_B):
        sl = slice(h * HEAD_DIM, (h + 1) * HEAD_DIM)
        s_scr[(H_A + h) * n:(H_A + h + 1) * n, :] = _dot_nt(qb_ref[:, sl], kb[:, sl]) * scale
    for h in range(H_C):
        qn = qc_ref[:, h * QC_PAD:h * QC_PAD + QK_NOPE]
        qr = qc_ref[:, h * QC_PAD + QK_NOPE:h * QC_PAD + QK_NOPE + QK_ROPE]
        c0 = h * (QK_NOPE + V_C)
        r0 = (H_A + H_B + h) * n
        s_scr[r0:r0 + n, :] = (_dot_nt(qn, kv[:, c0:c0 + QK_NOPE]) + _dot_nt(qr, kr)) * scale_c
    for pair in range((H_A + H_B + H_C) // 2):
        h0 = 2 * pair
        sinks = ((sink_ref[h0], n), (sink_ref[h0 + 1], n)) if h0 < H_A else None
        _softmax_rows(s_scr, p_scr, slice(h0 * n, (h0 + 2) * n), sinks)
    for h in range(H_A):
        g = h // G_A
        o_scr[:, h * HEAD_DIM:(h + 1) * HEAD_DIM] = _dot(p_scr[h * n:(h + 1) * n, :],
                                                         va[:, g * HEAD_DIM:(g + 1) * HEAD_DIM])
    for h in range(H_B):
        sl = slice(h * HEAD_DIM, (h + 1) * HEAD_DIM)
        o_scr[:, W_QA + h * HEAD_DIM:W_QA + (h + 1) * HEAD_DIM] = _dot(p_scr[(H_A + h) * n:(H_A + h + 1) * n, :],
                                                                     vb[:, sl])
    for h in range(H_C):
        c0 = h * (QK_NOPE + V_C)
        r0 = (H_A + H_B + h) * n
        off = W_QA + W_B + h * V_C
        o_scr[:, off:off + V_C] = _dot(p_scr[r0:r0 + n, :], kv[:, c0 + QK_NOPE:c0 + QK_NOPE + V_C])
    y = _dot(o_scr[...].astype(bf16), wout_ref[...])
    o_ref[...] = x_ref[...] + gate_ref[0] * y


def _ctx_attn(sink, proj, wukv, wout, x, gate):
    qa, ka, va, qb, kb, vb, qc, ckv, kr = proj
    row = lambda b: (b, 0)
    const = lambda b: (0, 0)
    in_specs = [pl.BlockSpec(memory_space=pltpu.SMEM)]
    in_specs += [pl.BlockSpec((SEQ, a.shape[1]), row) for a in proj]
    in_specs += [pl.BlockSpec((KV_LORA, H_C * (QK_NOPE + V_C)), const),
                 pl.BlockSpec((D_MODEL, D_MODEL), const),
                 pl.BlockSpec((SEQ, D_MODEL), row),
                 pl.BlockSpec((1, 1, D_MODEL), lambda b: (0, 0, 0))]
    return pl.pallas_call(
        _ctx_attn_kernel,
        grid=(BATCH,),
        in_specs=in_specs,
        out_specs=pl.BlockSpec((SEQ, D_MODEL), row),
        out_shape=jax.ShapeDtypeStruct((T_CTX, D_MODEL), f32),
        scratch_shapes=[pltpu.VMEM((SEQ, D_MODEL), f32),
                        pltpu.VMEM(((H_A + H_B + H_C) * SEQ, SEQ), f32),
                        pltpu.VMEM(((H_A + H_B + H_C) * SEQ, SEQ), bf16)],
        compiler_params=_cparams("arbitrary"),
        name="ctx_attn",
    )(sink, qa, ka, va, qb, kb, vb, qc, ckv, kr, wukv, wout, x, gate)


def _lat_attn_kernel(sink_ref, qa_ref, qb_ref, qc_ref, ka_ref, va_ref, kb_ref, vb_ref, ckv_ref, kr_ref,
                     cak_ref, cav_ref, cbk_ref, cbv_ref, cckv_ref, ckr_ref, bias_ref,
                     wukv_ref, wout_ref, x_ref, gate_ref, o_ref, o_scr, kv_scr, sa, pa, sb, pb, sc, pc):
    qi = pl.program_id(1)
    nb = DEC_SEQ // BLOCK
    scale = HEAD_DIM ** -0.5

    @pl.when(qi == 0)
    def _():
        kv_scr[0:DEC_SEQ, :] = _dot(ckv_ref[...], wukv_ref[...]).astype(bf16)
        kv_scr[DEC_SEQ:DEC_SEQ + PAST_LEN, :] = _dot(cckv_ref[0, 0].astype(bf16), wukv_ref[...]).astype(bf16)

    def blk(ref, j):
        idx = jnp.clip(qi + j, 0, nb - 1)
        return ref[pl.ds(pl.multiple_of(idx * BLOCK, BLOCK), BLOCK), :]

    ka = jnp.concatenate([blk(ka_ref, -1), blk(ka_ref, 0), blk(ka_ref, 1), cak_ref[0, 0].astype(bf16)], axis=0)
    va = jnp.concatenate([blk(va_ref, -1), blk(va_ref, 0), blk(va_ref, 1), cav_ref[0, 0].astype(bf16)], axis=0)
    nk_a = 3 * BLOCK + PAST_LEN
    r = lax.broadcasted_iota(jnp.int32, (BLOCK, nk_a), 0)
    c = lax.broadcasted_iota(jnp.int32, (BLOCK, nk_a), 1)
    valid = (((c < BLOCK) & (c >= r) & (qi > 0))
             | ((c >= BLOCK) & (c < 2 * BLOCK))
             | ((c >= 2 * BLOCK) & (c < 3 * BLOCK) & (c - 2 * BLOCK <= r) & (qi < nb - 1))
             | (c >= 3 * BLOCK))
    for h in range(H_A):
        g = h // G_A
        q = qa_ref[:, h * HEAD_DIM:(h + 1) * HEAD_DIM]
        s = _dot_nt(q, ka[:, g * HEAD_DIM:(g + 1) * HEAD_DIM]) * scale
        sa[h * BLOCK:(h + 1) * BLOCK, :] = jnp.where(valid, s, NEG)

    cbk = cbk_ref[0, 0].astype(bf16)
    cbv = cbv_ref[0, 0].astype(bf16)
    rows_per_blk = BLOCK // GRID_W
    nloc = NA_ROWS * GRID_W
    vcats = []
    for half in range(rows_per_blk):
        grow = qi * rows_per_blk + half
        start = jnp.clip(grow - NA_ROWS // 2, 0, ROWS - NA_ROWS)
        kloc = kb_ref[pl.ds(pl.multiple_of(start * GRID_W, GRID_W), nloc), :]
        vloc = vb_ref[pl.ds(pl.multiple_of(start * GRID_W, GRID_W), nloc), :]
        vcats.append(jnp.concatenate([vloc, cbv], axis=0))
        qrows = slice(half * GRID_W, (half + 1) * GRID_W)
        dr0 = start - grow + (NA_ROWS - 1)
        for h in range(H_B):
            sl = slice(h * HEAD_DIM, (h + 1) * HEAD_DIM)
            q = qb_ref[qrows, sl]
            bias = jnp.concatenate([bias_ref[h, dr0 + 2 * j] for j in range(NA_ROWS // 2)], axis=1)
            s_loc = _dot_nt(q, kloc[:, sl]) * scale + bias
            s_ctx = _dot_nt(q, cbk[:, sl]) * scale
            r0 = (half * H_B + h) * GRID_W
            sb[r0:r0 + GRID_W, :] = jnp.concatenate([s_loc, s_ctx], axis=1)

    kr = jnp.concatenate([kr_ref[...], ckr_ref[0, 0].astype(bf16)], axis=0)
    scale_c = (QK_NOPE + QK_ROPE) ** -0.5
    for h in range(H_C):
        qn = qc_ref[:, h * QC_PAD:h * QC_PAD + QK_NOPE]
        qr = qc_ref[:, h * QC_PAD + QK_NOPE:h * QC_PAD + QK_NOPE + QK_ROPE]
        c0 = h * (QK_NOPE + V_C)
        sc[h * BLOCK:(h + 1) * BLOCK, :] = (_dot_nt(qn, kv_scr[:, c0:c0 + QK_NOPE]) + _dot_nt(qr, kr)) * scale_c

    for pair in range(H_A // 2):
        h0 = 2 * pair
        _softmax_rows(sa, pa, slice(h0 * BLOCK, (h0 + 2) * BLOCK), ((sink_ref[h0], BLOCK), (sink_ref[h0 + 1], BLOCK)))
    for blk2 in range(rows_per_blk * H_B // 2):
        _softmax_rows(sb, pb, slice(blk2 * 2 * GRID_W, (blk2 + 1) * 2 * GRID_W))
    for h in range(H_C):
        _softmax_rows(sc, pc, slice(h * BLOCK, (h + 1) * BLOCK))

    for h in range(H_A):
        g = h // G_A
        o_scr[:, h * HEAD_DIM:(h + 1) * HEAD_DIM] = _dot(pa[h * BLOCK:(h + 1) * BLOCK, :],
                                                         va[:, g * HEAD_DIM:(g + 1) * HEAD_DIM])
    for half in range(rows_per_blk):
        qrows = slice(half * GRID_W, (half + 1) * GRID_W)
        for h in range(H_B):
            sl = slice(h * HEAD_DIM, (h + 1) * HEAD_DIM)
            r0 = (half * H_B + h) * GRID_W
            o_scr[qrows, W_QA + h * HEAD_DIM:W_QA + (h + 1) * HEAD_DIM] = _dot(pb[r0:r0 + GRID_W, :],
                                                                             vcats[half][:, sl])
    for h in range(H_C):
        c0 = h * (QK_NOPE + V_C)
        off = W_QA + W_B + h * V_C
        o_scr[:, off:off + V_C] = _dot(pc[h * BLOCK:(h + 1) * BLOCK, :], kv_scr[:, c0 + QK_NOPE:c0 + QK_NOPE + V_C])

    y = _dot(o_scr[...].astype(bf16), wout_ref[...])
    o_ref[...] = x_ref[...] + gate_ref[0] * y


def _lat_attn(layer, sink, proj, caches, bias_tab, wukv, wout, x, lat_row0, gate):
    qa, ka, va, qb, kb, vb, qc, ckv, kr = proj
    nb = DEC_SEQ // BLOCK
    qrow = lambda b, q: (b * nb + q, 0)
    xrow = lambda b, q: (lat_row0 // BLOCK + b * nb + q, 0)
    brow = lambda b, q: (b, 0)
    const = lambda b, q: (0, 0)
    cidx = lambda b, q: (b, layer, 0, 0)
    in_specs = [pl.BlockSpec(memory_space=pltpu.SMEM)]
    in_specs += [pl.BlockSpec((BLOCK, a.shape[1]), qrow) for a in (qa, qb, qc)]
    in_specs += [pl.BlockSpec((DEC_SEQ, a.shape[1]), brow) for a in (ka, va, kb, vb, ckv, kr)]
    in_specs += [pl.BlockSpec((1, 1, PAST_LEN, a.shape[3]), cidx) for a in caches]
    in_specs += [pl.BlockSpec(bias_tab.shape, lambda b, q: (0, 0, 0, 0)),
                 pl.BlockSpec((KV_LORA, H_C * (QK_NOPE + V_C)), const),
                 pl.BlockSpec((D_MODEL, D_MODEL), const),
                 pl.BlockSpec((BLOCK, D_MODEL), xrow),
                 pl.BlockSpec((1, 1, D_MODEL), lambda b, q: (1 + b, 0, 0))]
    return pl.pallas_call(
        _lat_attn_kernel,
        grid=(DEC_BATCH, nb),
        in_specs=in_specs,
        out_specs=pl.BlockSpec((BLOCK, D_MODEL), qrow),
        out_shape=jax.ShapeDtypeStruct((T_LAT, D_MODEL), f32),
        scratch_shapes=[pltpu.VMEM((BLOCK, D_MODEL), f32),
                        pltpu.VMEM((DEC_SEQ + PAST_LEN, H_C * (QK_NOPE + V_C)), bf16)]
        + [pltpu.VMEM(shape, dt) for shape in ((H_A * BLOCK, 3 * BLOCK + PAST_LEN),
                                               (H_B * BLOCK, NA_ROWS * GRID_W + PAST_LEN),
                                               (H_C * BLOCK, DEC_SEQ + PAST_LEN)) for dt in (f32, bf16)],
        compiler_params=_cparams("arbitrary", "arbitrary"),
        name="lat_attn",
    )(sink, qa, qb, qc, ka, va, kb, vb, ckv, kr, *caches, bias_tab, wukv, wout, x, gate)


def _pick_stream(xc_ref, xl_ref, x_scr):
    i = pl.program_id(0)

    @pl.when(i < T_CTX // TM_TOK)
    def _():
        x_scr[...] = xc_ref[...]

    @pl.when(i >= T_CTX // TM_TOK)
    def _():
        x_scr[...] = xl_ref[...]

    return x_scr[...]


def _stream_specs(lat_row0):
    n_ctx = T_CTX // TM_TOK
    return [pl.BlockSpec((TM_TOK, D_MODEL), lambda i: (jnp.minimum(i, n_ctx - 1), 0)),
            pl.BlockSpec((TM_TOK, D_MODEL), lambda i: (lat_row0 // TM_TOK + jnp.maximum(i - n_ctx, 0), 0))]


def _router_kernel(xc_ref, xl_ref, g_ref, sh_ref, sc_ref, wr_ref, br_ref, h_ref, e_ref, gt_ref, x_scr):
    h = _rms(_pick_stream(xc_ref, xl_ref, x_scr), g_ref[...]) * (1.0 + sc_ref[0]) + sh_ref[0]
    _store_row_tiles(h_ref, h)
    logits = jnp.dot(h, wr_ref[...], preferred_element_type=f32, precision=lax.Precision.HIGHEST) + br_ref[...]
    lane = lax.broadcasted_iota(jnp.int32, logits.shape, 1).astype(f32)
    l = jnp.where(lane < N_EXPERTS, logits, -jnp.inf)
    tops, idxs = [], []
    for _ in range(TOP_K):
        m = jnp.max(l, axis=-1, keepdims=True)
        idx = jnp.min(jnp.where(l == m, lane, float(LANE)), axis=-1, keepdims=True)
        tops.append(m)
        idxs.append(idx)
        l = jnp.where(lane == idx, -jnp.inf, l)
    ex = [jnp.exp(t - tops[0]) for t in tops]
    den = ex[0] + ex[1] + ex[2] + ex[3]
    e_out = jnp.zeros(logits.shape, f32)
    g_out = jnp.zeros(logits.shape, f32)
    for k in range(TOP_K):
        e_out = jnp.where(lane == k, idxs[k], e_out)
        g_out = jnp.where(lane == k, ex[k] / den, g_out)
    e_ref[...] = e_out.astype(jnp.int32)
    gt_ref[...] = g_out


def _group_of_tile(i):
    per_b = DEC_SEQ // TM_TOK
    n_ctx = T_CTX // TM_TOK
    return jnp.where(i < n_ctx, 0, 1 + (i - n_ctx) // per_b)


def _router(xc, xl, lat_row0, g, shift, scale, wr, br):
    tm = TM_TOK
    row = lambda i: (i, 0)
    const = lambda i: (0, 0)
    grp = lambda i: (_group_of_tile(i), 0, 0)
    return pl.pallas_call(
        _router_kernel,
        grid=(T_ALL // tm,),
        in_specs=_stream_specs(lat_row0) +
                 [pl.BlockSpec((1, D_MODEL), const),
                  pl.BlockSpec((1, 1, D_MODEL), grp),
                  pl.BlockSpec((1, 1, D_MODEL), grp),
                  pl.BlockSpec((D_MODEL, LANE), const),
                  pl.BlockSpec((1, LANE), const)],
        out_specs=[pl.BlockSpec((tm * ROW_TILE, LANE), row), pl.BlockSpec((tm, LANE), row),
                   pl.BlockSpec((tm, LANE), row)],
        out_shape=[jax.ShapeDtypeStruct((T_ALL * ROW_TILE, LANE), f32),
                   jax.ShapeDtypeStruct((T_ALL, LANE), jnp.int32),
                   jax.ShapeDtypeStruct((T_ALL, LANE), f32)],
        scratch_shapes=[pltpu.VMEM((tm, D_MODEL), f32)],
        compiler_params=_cparams("arbitrary"),
        name="router",
    )(xc, xl, g, shift, scale, wr, br)


def _dispatch_kernel(tok_ref, nu_ref, h_hbm, o_ref, hv, xg, hsem):
    tm = TM_MOE
    i = pl.program_id(0)

    @pl.when(i == 0)
    def _():
        resident = pltpu.make_async_copy(h_hbm, hv, hsem.at[0])
        resident.start()
        resident.wait()

    @pl.when(i < nu_ref[0])
    def _():
        for r in range(tm):
            t = tok_ref[i * tm + r]
            xg[pl.ds(r * ROW_TILE, ROW_TILE), :] = hv[pl.ds(pl.multiple_of(t * ROW_TILE, ROW_TILE), ROW_TILE), :]
        o_ref[...] = _load_row_tiles(xg).astype(bf16)

    @pl.when(i >= nu_ref[0])
    def _():
        o_ref[...] = jnp.zeros_like(o_ref)


def _dispatch(row_tok, n_used, h):
    tm = TM_MOE
    return pl.pallas_call(
        _dispatch_kernel,
        grid_spec=pltpu.PrefetchScalarGridSpec(
            num_scalar_prefetch=2,
            grid=(N_MOE_BLOCKS,),
            in_specs=[pl.BlockSpec(memory_space=pl.ANY)],
            out_specs=pl.BlockSpec((tm, D_MODEL), lambda i, tok, nu: (i, 0)),
            scratch_shapes=[pltpu.VMEM((T_ALL * ROW_TILE, LANE), f32), pltpu.VMEM((tm * ROW_TILE, LANE), f32),
                            pltpu.SemaphoreType.DMA((1,))]),
        out_shape=jax.ShapeDtypeStruct((N_MOE_BLOCKS * tm, D_MODEL), bf16),
        compiler_params=_cparams("arbitrary"),
        name="dispatch",
    )(row_tok, n_used, h)


def _moe_kernel(layer, be_ref, nu_ref, nxt_ref, dst_ref, x_ref, wgu_hbm, bgu_ref, wd_hbm, bd_ref, y_hbm,
                y0, y1, wgu_st, wd_st, wgu_bf, wd_bf, wsem, ssem):
    tm = TM_MOE
    i = pl.program_id(0)
    nb = pl.num_programs(0)
    used = i < nu_ref[0]
    yb = (y0, y1)

    def out_tile(row):
        return pl.ds(pl.multiple_of(row * ROW_TILE, ROW_TILE), ROW_TILE)

    def scatter_desc(buf, r, dst_row, s):
        return pltpu.make_async_copy(buf.at[out_tile(r)], y_hbm.at[out_tile(dst_row)], ssem.at[s])

    def scatter_wait(s):
        pltpu.make_async_copy(yb[s], y_hbm.at[pl.ds(0, tm * ROW_TILE)], ssem.at[s]).wait()

    def scatter_start(blk, s, unrolled):
        if unrolled:
            for r in range(tm):
                scatter_desc(yb[s], r, dst_ref[(blk + 1) * tm + r], s).start(priority=r % 2)
        else:
            def body(r, carry):
                scatter_desc(yb[s], r, dst_ref[(blk + 1) * tm + r], s).start()
                return carry
            lax.fori_loop(0, tm, body, 0, unroll=8)

    def weight_copies(e):
        return (pltpu.make_async_copy(wgu_hbm.at[layer, e], wgu_st, wsem.at[0]),
                pltpu.make_async_copy(wd_hbm.at[layer, e], wd_st, wsem.at[1]))

    @pl.when(i == 0)
    def _():
        for s in range(2):
            yb[s][...] = jnp.zeros_like(yb[s])
            dummy = pltpu.make_async_copy(yb[s], y_hbm.at[pl.ds((N_ASSIGN + s * tm) * ROW_TILE, tm * ROW_TILE)],
                                          ssem.at[s])
            dummy.start()
            dummy.wait()
        for cp in weight_copies(be_ref[0]):
            cp.start()

    first = jnp.logical_and(used, jnp.logical_or(i == 0, be_ref[i] != be_ref[jnp.maximum(i - 1, 0)]))

    @pl.when(first)
    def _():
        for cp in weight_copies(0):
            cp.wait()
        wgu_bf[...] = wgu_st[...].astype(bf16)
        wd_bf[...] = wd_st[...].astype(bf16)

        @pl.when(nxt_ref[i] >= 0)
        def _():
            for cp in weight_copies(nxt_ref[i]):
                cp.start()

    def step(par):
        cur, oth = par, 1 - par

        @pl.when(i >= 1)
        def _():
            scatter_wait(cur)

        @pl.when(used)
        def _():
            scatter_start(i - 1, oth, unrolled=True)
            gu = _dot(x_ref[...], wgu_bf[...]) + bgu_ref[0, 0]
            x_glu = jnp.minimum(gu[:, :D_FF], SWIGLU_LIMIT)
            x_lin = jnp.clip(gu[:, D_FF:], -SWIGLU_LIMIT, SWIGLU_LIMIT)
            act = x_glu * jax.nn.sigmoid(SWIGLU_ALPHA * x_glu) * (x_lin + 1.0)
            _store_row_tiles(yb[cur], _dot(act.astype(bf16), wd_bf[...]) + bd_ref[0, 0])

        @pl.when(jnp.logical_and(jnp.logical_not(used), i + 1 < nb))
        def _():
            scatter_start(i - 1, oth, unrolled=False)

        @pl.when(i == nb - 1)
        def _():
            scatter_start(i - 1, oth, unrolled=False)
            scatter_wait(oth)

    @pl.when(i % 2 == 0)
    def _():
        step(0)

    @pl.when(i % 2 == 1)
    def _():
        step(1)


def _moe(layer, routing, h, w_gu, b_gu, w_down, b_down):
    tm = TM_MOE
    block_e, n_used, nxt_e, row_tok, row_dst = routing
    xs = _dispatch(row_tok, n_used, h)
    ex4 = lambda i, be, nu, nxt, dst: (layer, be[i], 0, 0)
    return pl.pallas_call(
        functools.partial(_moe_kernel, layer),
        grid_spec=pltpu.PrefetchScalarGridSpec(
            num_scalar_prefetch=4,
            grid=(N_MOE_BLOCKS,),
            in_specs=[pl.BlockSpec((tm, D_MODEL), lambda i, be, nu, nxt, dst: (i, 0)),
                      pl.BlockSpec(memory_space=pl.ANY),
                      pl.BlockSpec((1, 1, 1, 2 * D_FF), ex4),
                      pl.BlockSpec(memory_space=pl.ANY),
                      pl.BlockSpec((1, 1, 1, D_MODEL), ex4)],
            out_specs=pl.BlockSpec(memory_space=pl.ANY),
            scratch_shapes=[pltpu.VMEM((tm * ROW_TILE, LANE), f32), pltpu.VMEM((tm * ROW_TILE, LANE), f32),
                            pltpu.VMEM((D_MODEL, 2 * D_FF), f32), pltpu.VMEM((D_FF, D_MODEL), f32),
                            pltpu.VMEM((D_MODEL, 2 * D_FF), bf16), pltpu.VMEM((D_FF, D_MODEL), bf16),
                            pltpu.SemaphoreType.DMA((2,)), pltpu.SemaphoreType.DMA((2,))]),
        out_shape=jax.ShapeDtypeStruct(((N_ASSIGN + 2 * tm) * ROW_TILE, LANE), f32),
        compiler_params=_cparams("arbitrary"),
        name="moe",
    )(block_e, n_used, nxt_e, row_dst, xs, w_gu, b_gu.reshape(DEPTH, N_EXPERTS, 1, 2 * D_FF),
      w_down, b_down.reshape(DEPTH, N_EXPERTS, 1, D_MODEL))


def _combine_kernel(final, xc_ref, xl_ref, y0_ref, y1_ref, y2_ref, y3_ref, gt_ref, gate_ref, gf_ref, o_ref, x_scr):
    gt = gt_ref[...]
    f = gt[:, 0:1] * _load_row_tiles(y0_ref)
    for k, y_ref in ((1, y1_ref), (2, y2_ref), (3, y3_ref)):
        f = f + gt[:, k:k + 1] * _load_row_tiles(y_ref)
    out = _pick_stream(xc_ref, xl_ref, x_scr) + gate_ref[0] * f
    if final:
        out = _rms(out, gf_ref[...])
    o_ref[...] = out


def _combine(final, xc, xl, lat_row0, y, gates, gate, g_final):
    tm = TM_TOK
    nt = T_ALL // tm
    row = lambda i: (i, 0)
    const = lambda i: (0, 0)
    grp = lambda i: (_group_of_tile(i), 0, 0)
    ysel = [pl.BlockSpec((tm * ROW_TILE, LANE), functools.partial(lambda k, i: (k * nt + i, 0), k))
            for k in range(TOP_K)]
    return pl.pallas_call(
        functools.partial(_combine_kernel, final),
        grid=(nt,),
        in_specs=_stream_specs(lat_row0) + ysel +
                 [pl.BlockSpec((tm, LANE), row),
                  pl.BlockSpec((1, 1, D_MODEL), grp),
                  pl.BlockSpec((1, D_MODEL), const)],
        out_specs=pl.BlockSpec((tm, D_MODEL), row),
        out_shape=jax.ShapeDtypeStruct((T_ALL, D_MODEL), f32),
        scratch_shapes=[pltpu.VMEM((tm, D_MODEL), f32)],
        compiler_params=_cparams("arbitrary"),
        name="combine",
    )(xc, xl, y, y, y, y, gates, gate, g_final)


def _rope_head_tables(d):
    nf = d // 4
    half = d // 2
    t = np.arange(DEC_SEQ)
    inv = ROPE_BASE ** (-np.arange(nf, dtype=np.float32) / nf)
    i = np.arange(d)
    pos = np.where(i[None, :] < half, (t // GRID_W)[:, None], (t % GRID_W)[:, None]).astype(np.float32)
    ang = pos * inv[i % nf][None, :].astype(np.float32)
    first = (i % half) < nf
    cos = np.cos(ang)
    sin = np.where(first[None, :], -np.sin(ang), np.sin(ang))
    partner = np.where(first, i + nf, i - nf)
    return cos.astype(np.float32), sin.astype(np.float32), partner


def _rope_tables():
    cos64, sin64, _ = _rope_head_tables(HEAD_DIM)
    cos32, sin32, _ = _rope_head_tables(QK_ROPE)
    cosa = np.tile(cos64, (1, H_A))
    sina = np.tile(sin64, (1, H_A))
    cosq1 = np.concatenate([np.ones((DEC_SEQ, QK_NOPE), np.float32), cos32,
                            np.ones((DEC_SEQ, QC_PAD - QK_NOPE - QK_ROPE), np.float32)], axis=1)
    sinq1 = np.concatenate([np.zeros((DEC_SEQ, QK_NOPE), np.float32), sin32,
                            np.zeros((DEC_SEQ, QC_PAD - QK_NOPE - QK_ROPE), np.float32)], axis=1)
    cosq = np.tile(cosq1, (1, H_C))
    sinq = np.tile(sinq1, (1, H_C))
    return tuple(jnp.asarray(a) for a in (cosa, sina, cosq, sinq, cos32, sin32))


def _pad_cols(w, n):
    return jnp.pad(w, ((0, 0), (0, n - w.shape[1])))


def _layer_weights(w_in, w_uq):
    cuts = np.cumsum((W_QA, W_KA, W_VA, W_B, W_B, W_B, Q_LORA, KV_LORA, QK_ROPE))[:-1]
    qa, ka, va, qb, kb, vb, cq, ckv, kr = jnp.split(w_in, [int(c) for c in cuts], axis=1)
    _, _, p64 = _rope_head_tables(HEAD_DIM)
    _, _, p32 = _rope_head_tables(QK_ROPE)
    pa = np.concatenate([h * HEAD_DIM + p64 for h in range(H_A)])
    base = jnp.concatenate([qa, ka, va, _pad_cols(qb, 384), _pad_cols(kb, 384), _pad_cols(vb, 384), cq, ckv,
                            _pad_cols(kr, 128)], axis=1)
    w_ctx = base.astype(bf16)
    w_lat = jnp.concatenate([base, qa[:, pa], ka[:, pa[:W_KA]], _pad_cols(kr[:, p32], 128)], axis=1).astype(bf16)
    hq = QK_NOPE + QK_ROPE
    heads = [_pad_cols(w_uq[:, h * hq:(h + 1) * hq], QC_PAD) for h in range(H_C)]
    pq = np.concatenate([np.arange(QK_NOPE), QK_NOPE + p32])
    heads_p = [_pad_cols(w_uq[:, h * hq:(h + 1) * hq][:, pq], QC_PAD) for h in range(H_C)]
    wuq = jnp.concatenate(heads, axis=1).astype(bf16)
    wuq2 = jnp.concatenate(heads + heads_p, axis=1).astype(bf16)
    return w_ctx, w_lat, wuq, wuq2


def _bias_table(rpb):
    col = np.arange(GRID_W)
    col_start = np.clip(col - NA_COLS // 2, 0, GRID_W - NA_COLS)
    col_ok = (col[None, :] >= col_start[:, None]) & (col[None, :] < col_start[:, None] + NA_COLS)
    dc = np.clip(col[None, :] - col[:, None] + (NA_COLS - 1), 0, 2 * NA_COLS - 2)
    blocks = jnp.where(col_ok[None, None], rpb[:, :, dc], NEG).astype(f32)
    return jnp.concatenate([blocks[:, :-1], blocks[:, 1:]], axis=-1)


def _routing(top_e):
    tm = TM_MOE
    key_bits = 16
    pad_mark = (1 << key_bits) - 1
    flat_e = top_e.T.reshape(N_ASSIGN)
    experts = jnp.arange(N_EXPERTS, dtype=jnp.int32)
    counts = jnp.sum((flat_e[:, None] == experts[None, :]).astype(jnp.int32), axis=0)
    nblk = (counts + tm - 1) // tm
    blk_end = jnp.cumsum(nblk)
    pad_end = jnp.cumsum(nblk * tm - counts)
    slots = jnp.arange(N_MOE_BLOCKS * tm - N_ASSIGN, dtype=jnp.int32)
    pad_e = jnp.sum((pad_end[None, :] <= slots[:, None]).astype(jnp.int32), axis=1)
    keys = jnp.concatenate([(flat_e << key_bits) + jnp.arange(N_ASSIGN, dtype=jnp.int32),
                            (pad_e << key_bits) + pad_mark])
    asg = (jnp.sort(keys) & pad_mark).reshape(N_MOE_BLOCKS, tm)
    valid = asg != pad_mark
    blocks = jnp.arange(N_MOE_BLOCKS, dtype=jnp.int32)
    r = jnp.arange(tm, dtype=jnp.int32)[None, :]
    tok = jnp.where(valid, asg % T_ALL, 0)
    row_dst = jnp.where(valid, asg, N_ASSIGN + (blocks[:, None] % 2) * tm + r)
    row_dst = jnp.concatenate([N_ASSIGN + tm + r, row_dst], axis=0).reshape(-1)
    block_e = jnp.minimum(jnp.sum((blk_end[None, :] <= blocks[:, None]).astype(jnp.int32), axis=1), N_EXPERTS - 1)
    n_used = blk_end[-1].astype(jnp.int32).reshape(1)
    has = jnp.where(counts > 0, experts, N_EXPERTS)
    later = experts[None, :] > experts[:, None]
    nxt = jnp.min(jnp.where(later, has[None, :], N_EXPERTS), axis=1)
    nxt = jnp.where(nxt >= N_EXPERTS, -1, nxt)
    sel = (block_e[:, None] == experts[None, :]).astype(jnp.int32)
    nxt_e = jnp.sum(sel * nxt[None, :], axis=1)
    i32 = lambda a: a.astype(jnp.int32)
    return i32(block_e), n_used, i32(nxt_e), i32(tok).reshape(-1), i32(row_dst)


def kernel(x_prompt, x_sample, cache_a_k, cache_a_v, cache_b_k, cache_b_v, cache_c_kv, cache_c_kr, c, c_ctx, w_ada, b_ada, g_attn, g_ffn, w_in, sink_a, rpb_b, g_cq, g_ckv, w_uq, w_ukv, w_out, w_router, b_router, w_gu, b_gu, w_down, b_down, g_final):
    xc, xl, lat_row0 = x_prompt.reshape(T_CTX, D_MODEL), x_sample.reshape(T_LAT, D_MODEL), 0
    cvec = jnp.concatenate([c_ctx[None, :], c, jnp.zeros((8 - N_GROUPS, D_MODEL), f32)], axis=0)
    mods = _ada(cvec, w_ada, b_ada)[:, :N_GROUPS].reshape(DEPTH, N_GROUPS, 6, 1, D_MODEL)
    tabs = _rope_tables()
    caches = (cache_a_k.reshape(DEC_BATCH, DEPTH, PAST_LEN, W_KA), cache_a_v.reshape(DEC_BATCH, DEPTH, PAST_LEN, W_VA),
              cache_b_k.reshape(DEC_BATCH, DEPTH, PAST_LEN, W_B), cache_b_v.reshape(DEC_BATCH, DEPTH, PAST_LEN, W_B),
              cache_c_kv, cache_c_kr)
    new = [[] for _ in range(6)]
    for layer in range(DEPTH):
        m = [mods[layer, :, j] for j in range(6)]
        w_ctx, w_lat, wuq, wuq2 = _layer_weights(w_in[layer], w_uq[layer])
        wukv = w_ukv[layer].astype(bf16)
        wout = w_out[layer].astype(bf16)
        g1 = g_attn[layer][None, :]
        gcq = g_cq[layer][None, :]
        gckv = g_ckv[layer][None, :]
        sink = sink_a[layer]

        pc = _inproj_ctx(xc, g1, m[0], m[1], w_ctx, gcq, gckv, wuq)
        for lst, a in zip(new, (pc[1], pc[2], pc[4], pc[5], pc[7], pc[8])):
            lst.append(a)
        x_ctx = _ctx_attn(sink, pc, wukv, wout, xc, m[2])

        plat = _inproj_lat(xl, lat_row0, g1, m[0], m[1], w_lat, gcq, gckv, wuq2, tabs)
        x_lat = _lat_attn(layer, sink, plat, caches, _bias_table(rpb_b[layer]), wukv, wout, xl, lat_row0, m[2])

        wr = _pad_cols(w_router[layer], LANE)
        br = _pad_cols(b_router[layer][None, :], LANE)
        h2, top_e, gates = _router(x_ctx, x_lat, 0, g_ffn[layer][None, :], m[3], m[4], wr, br)
        y = _moe(layer, _routing(top_e[:, :TOP_K]), h2, w_gu, b_gu, w_down, b_down)
        x = _combine(layer == DEPTH - 1, x_ctx, x_lat, 0, y, gates, m[5], g_final[None, :])
        xc, xl, lat_row0 = x, x, T_CTX

    y_prompt = x[:T_CTX].reshape(BATCH, SEQ, D_MODEL)
    y_sample = x[T_CTX:].reshape(DEC_BATCH, DEC_SEQ, D_MODEL)
    shapes = ((KV_A, HEAD_DIM), (KV_A, HEAD_DIM), (H_B, HEAD_DIM), (H_B, HEAD_DIM), (KV_LORA,), (QK_ROPE,))
    outs = [jnp.stack([a.reshape((BATCH, SEQ) + s) for a in lst], axis=1) for lst, s in zip(new, shapes)]
    return (y_prompt, y_sample, *outs)
```

```python
import functools

import numpy as np
import jax
import jax.numpy as jnp
from jax import lax
from jax.experimental import pallas as pl
from jax.experimental.pallas import tpu as pltpu

D_MODEL = 1024
BATCH = 32
SEQ = 256
DEPTH = 2
DEC_BATCH = 2
DEC_SEQ = 1024
PAST_LEN = 512
GRID_W = 64
HEAD_DIM = 64
H_A = 6
KV_A = 2
G_A = H_A // KV_A
WINDOW = 128
BLOCK = 128
H_B = 5
NA_ROWS = 8
NA_COLS = 16
H_C = 5
Q_LORA = 384
KV_LORA = 256
QK_NOPE = 64
QK_ROPE = 32
V_C = 64
N_EXPERTS = 32
TOP_K = 4
D_FF = 1024
SWIGLU_ALPHA = 1.702
SWIGLU_LIMIT = 7.0
ROPE_BASE = 10000.0
EPS = 1e-6
NEG = -1e30

T_CTX = BATCH * SEQ
T_LAT = DEC_BATCH * DEC_SEQ
T_ALL = T_CTX + T_LAT
N_GROUPS = 1 + DEC_BATCH
LANE = 128
QC_PAD = 128
ROWS = DEC_SEQ // GRID_W

W_QA, W_KA, W_VA = H_A * HEAD_DIM, KV_A * HEAD_DIM, KV_A * HEAD_DIM
W_B = H_B * HEAD_DIM
OFF_QA = 0
OFF_KA = 384
OFF_VA = 512
OFF_QB = 640
OFF_KB = 1024
OFF_VB = 1408
OFF_CQ = 1792
OFF_CKV = 2176
OFF_KR = 2432
NW_CTX = 2560
OFF_QA_P = 2560
OFF_KA_P = 2944
OFF_KR_P = 3072
NW_LAT = 3200

TM_TOK = 256
TM_LAT_IN = 512
TM_MOE = 256
N_ASSIGN = T_ALL * TOP_K
N_MOE_BLOCKS = N_ASSIGN // TM_MOE + N_EXPERTS
DISPATCH_BLOCKS = 4
VMEM_LIMIT = 56 * 1024 * 1024

f32 = jnp.float32
bf16 = jnp.bfloat16


def _cparams(*sem):
    return pltpu.CompilerParams(dimension_semantics=sem, vmem_limit_bytes=VMEM_LIMIT)


def _rms(xf, g):
    return xf * lax.rsqrt(jnp.mean(xf * xf, axis=-1, keepdims=True) + EPS) * g


def _dot(a, b):
    return jnp.dot(a, b, preferred_element_type=f32)


def _dot_nt(a, b):
    return lax.dot_general(a, b, (((1,), (1,)), ((), ())), preferred_element_type=f32)


ROW_TILE = D_MODEL // LANE


def _store_row_tiles(ref, val):
    n = val.shape[0]
    for c in range(ROW_TILE):
        ref[pl.ds(c, n, stride=ROW_TILE), :] = val[:, c * LANE:(c + 1) * LANE]


def _load_row_tiles(ref):
    n = ref.shape[0] // ROW_TILE
    return jnp.concatenate([ref[pl.ds(c, n, stride=ROW_TILE), :] for c in range(ROW_TILE)], axis=1)


def _softmax_rows(s_ref, p_ref, rows, sinks=None):
    s = s_ref[rows, :]
    m = jnp.max(s, axis=-1, keepdims=True)
    if sinks is not None:
        sink = jnp.concatenate([jnp.full((n, 1), v, f32) for v, n in sinks], axis=0)
        m = jnp.maximum(m, sink)
    p = jnp.exp(s - m)
    l = jnp.sum(p, axis=-1, keepdims=True)
    if sinks is not None:
        l = l + jnp.exp(sink - m)
    p_ref[rows, :] = (p * (1.0 / l)).astype(bf16)


def _ada_kernel(c_ref, w_ref, b_ref, o_ref):
    c = c_ref[...]
    s = c * jax.nn.sigmoid(c)
    o_ref[0] = jnp.dot(s, w_ref[0], preferred_element_type=f32, precision=lax.Precision.HIGHEST) + b_ref[0]


def _ada(cvec, w_ada, b_ada):
    tn = 1536
    return pl.pallas_call(
        _ada_kernel,
        grid=(DEPTH, 6 * D_MODEL // tn),
        in_specs=[pl.BlockSpec((8, D_MODEL), lambda l, j: (0, 0)),
                  pl.BlockSpec((1, D_MODEL, tn), lambda l, j: (l, 0, j)),
                  pl.BlockSpec((1, 1, tn), lambda l, j: (l, 0, j))],
        out_specs=pl.BlockSpec((1, 8, tn), lambda l, j: (l, 0, j)),
        out_shape=jax.ShapeDtypeStruct((DEPTH, 8, 6 * D_MODEL), f32),
        compiler_params=_cparams("arbitrary", "arbitrary"),
        name="ada",
    )(cvec, w_ada, b_ada.reshape(DEPTH, 1, 6 * D_MODEL))


def _inproj_ctx_kernel(x_ref, g_ref, sh_ref, sc_ref, w_ref, gcq_ref, gckv_ref, wuq_ref,
                       qa_ref, ka_ref, va_ref, qb_ref, kb_ref, vb_ref, qc_ref, ckv_ref, kr_ref):
    h = _rms(x_ref[...], g_ref[...]) * (1.0 + sc_ref[0]) + sh_ref[0]
    p = _dot(h.astype(bf16), w_ref[...])
    qa_ref[...] = p[:, OFF_QA:OFF_QA + W_QA].astype(bf16)
    ka_ref[...] = p[:, OFF_KA:OFF_KA + W_KA]
    va_ref[...] = p[:, OFF_VA:OFF_VA + W_VA]
    qb_ref[...] = p[:, OFF_QB:OFF_QB + W_B].astype(bf16)
    kb_ref[...] = p[:, OFF_KB:OFF_KB + W_B]
    vb_ref[...] = p[:, OFF_VB:OFF_VB + W_B]
    cqn = _rms(p[:, OFF_CQ:OFF_CQ + Q_LORA], gcq_ref[...])
    qc_ref[...] = _dot(cqn.astype(bf16), wuq_ref[...]).astype(bf16)
    ckv_ref[...] = _rms(p[:, OFF_CKV:OFF_CKV + KV_LORA], gckv_ref[...])
    kr_ref[...] = p[:, OFF_KR:OFF_KR + QK_ROPE]


def _inproj_ctx(x, g, shift, scale, w, gcq, gckv, wuq):
    tm = TM_TOK
    row = lambda i: (i, 0)
    const = lambda i: (0, 0)
    widths = (W_QA, W_KA, W_VA, W_B, W_B, W_B, H_C * QC_PAD, KV_LORA, QK_ROPE)
    dtypes = (bf16, f32, f32, bf16, f32, f32, bf16, f32, f32)
    return pl.pallas_call(
        _inproj_ctx_kernel,
        grid=(T_CTX // tm,),
        in_specs=[pl.BlockSpec((tm, D_MODEL), row),
                  pl.BlockSpec((1, D_MODEL), const),
                  pl.BlockSpec((1, 1, D_MODEL), lambda i: (0, 0, 0)),
                  pl.BlockSpec((1, 1, D_MODEL), lambda i: (0, 0, 0)),
                  pl.BlockSpec((D_MODEL, NW_CTX), const),
                  pl.BlockSpec((1, Q_LORA), const),
                  pl.BlockSpec((1, KV_LORA), const),
                  pl.BlockSpec((Q_LORA, H_C * QC_PAD), const)],
        out_specs=[pl.BlockSpec((tm, wd), row) for wd in widths],
        out_shape=[jax.ShapeDtypeStruct((T_CTX, wd), dt) for wd, dt in zip(widths, dtypes)],
        compiler_params=_cparams("arbitrary"),
        name="inproj_ctx",
    )(x, g, shift, scale, w, gcq, gckv, wuq)


def _inproj_lat_kernel(x_ref, g_ref, sh_ref, sc_ref, w_ref, gcq_ref, gckv_ref, wuq_ref,
                       cosa_ref, sina_ref, cosq_ref, sinq_ref, cosr_ref, sinr_ref,
                       qa_ref, ka_ref, va_ref, qb_ref, kb_ref, vb_ref, qc_ref, ckv_ref, kr_ref):
    h = _rms(x_ref[...], g_ref[...]) * (1.0 + sc_ref[0]) + sh_ref[0]
    p = _dot(h.astype(bf16), w_ref[...])
    cosa = cosa_ref[...]
    sina = sina_ref[...]
    qa = p[:, OFF_QA:OFF_QA + W_QA] * cosa + p[:, OFF_QA_P:OFF_QA_P + W_QA] * sina
    ka = p[:, OFF_KA:OFF_KA + W_KA] * cosa[:, :W_KA] + p[:, OFF_KA_P:OFF_KA_P + W_KA] * sina[:, :W_KA]
    kr = p[:, OFF_KR:OFF_KR + QK_ROPE] * cosr_ref[...] + p[:, OFF_KR_P:OFF_KR_P + QK_ROPE] * sinr_ref[...]
    qa_ref[...] = qa.astype(bf16)
    ka_ref[...] = ka.astype(bf16)
    va_ref[...] = p[:, OFF_VA:OFF_VA + W_VA].astype(bf16)
    qb_ref[...] = p[:, OFF_QB:OFF_QB + W_B].astype(bf16)
    kb_ref[...] = p[:, OFF_KB:OFF_KB + W_B].astype(bf16)
    vb_ref[...] = p[:, OFF_VB:OFF_VB + W_B].astype(bf16)
    cqn = _rms(p[:, OFF_CQ:OFF_CQ + Q_LORA], gcq_ref[...])
    q2 = _dot(cqn.astype(bf16), wuq_ref[...])
    nq = H_C * QC_PAD
    qc_ref[...] = (q2[:, :nq] * cosq_ref[...] + q2[:, nq:] * sinq_ref[...]).astype(bf16)
    ckv_ref[...] = _rms(p[:, OFF_CKV:OFF_CKV + KV_LORA], gckv_ref[...]).astype(bf16)
    kr_ref[...] = kr.astype(bf16)


def _inproj_lat(x, lat_row0, g, shift, scale, w, gcq, gckv, wuq2, tabs):
    tm = TM_LAT_IN
    per_b = DEC_SEQ // tm
    row0 = lat_row0 // tm
    xrow = lambda i: (row0 + i, 0)
    row = lambda i: (i, 0)
    const = lambda i: (0, 0)
    grp = lambda i: (1 + i // per_b, 0, 0)
    pos = lambda i: (i % per_b, 0)
    cosa, sina, cosq, sinq, cosr, sinr = tabs
    widths = (W_QA, W_KA, W_VA, W_B, W_B, W_B, H_C * QC_PAD, KV_LORA, QK_ROPE)
    return pl.pallas_call(
        _inproj_lat_kernel,
        grid=(T_LAT // tm,),
        in_specs=[pl.BlockSpec((tm, D_MODEL), xrow),
                  pl.BlockSpec((1, D_MODEL), const),
                  pl.BlockSpec((1, 1, D_MODEL), grp),
                  pl.BlockSpec((1, 1, D_MODEL), grp),
                  pl.BlockSpec((D_MODEL, NW_LAT), const),
                  pl.BlockSpec((1, Q_LORA), const),
                  pl.BlockSpec((1, KV_LORA), const),
                  pl.BlockSpec((Q_LORA, 2 * H_C * QC_PAD), const),
                  pl.BlockSpec((tm, W_QA), pos), pl.BlockSpec((tm, W_QA), pos),
                  pl.BlockSpec((tm, H_C * QC_PAD), pos), pl.BlockSpec((tm, H_C * QC_PAD), pos),
                  pl.BlockSpec((tm, QK_ROPE), pos), pl.BlockSpec((tm, QK_ROPE), pos)],
        out_specs=[pl.BlockSpec((tm, wd), row) for wd in widths],
        out_shape=[jax.ShapeDtypeStruct((T_LAT, wd), bf16) for wd in widths],
        compiler_params=_cparams("arbitrary"),
        name="inproj_lat",
    )(x, g, shift, scale, w, gcq, gckv, wuq2, cosa, sina, cosq, sinq, cosr, sinr)


def _ctx_attn_kernel(sink_ref, qa_ref, ka_ref, va_ref, qb_ref, kb_ref, vb_ref, qc_ref, ckv_ref, kr_ref,
                     wukv_ref, wout_ref, x_ref, gate_ref, o_ref, o_scr, s_scr, p_scr):
    n = SEQ
    scale = HEAD_DIM ** -0.5
    scale_c = (QK_NOPE + QK_ROPE) ** -0.5
    ka = ka_ref[...].astype(bf16)
    va = va_ref[...].astype(bf16)
    kb = kb_ref[...].astype(bf16)
    vb = vb_ref[...].astype(bf16)
    kv = _dot(ckv_ref[...].astype(bf16), wukv_ref[...]).astype(bf16)
    kr = kr_ref[...].astype(bf16)
    for h in range(H_A):
        g = h // G_A
        q = qa_ref[:, h * HEAD_DIM:(h + 1) * HEAD_DIM]
        s_scr[h * n:(h + 1) * n, :] = _dot_nt(q, ka[:, g * HEAD_DIM:(g + 1) * HEAD_DIM]) * scale
    for h in range(H_B):
        sl = slice(h * HEAD_DIM, (h + 1) * HEAD_DIM)
        s_scr[(H_A + h) * n:(H_A + h + 1) * n, :] = _dot_nt(qb_ref[:, sl], kb[:, sl]) * scale
    for h in range(H_C):
        qn = qc_ref[:, h * QC_PAD:h * QC_PAD + QK_NOPE]
        qr = qc_ref[:, h * QC_PAD + QK_NOPE:h * QC_PAD + QK_NOPE + QK_ROPE]
        c0 = h * (QK_NOPE + V_C)
        r0 = (H_A + H_B + h) * n
        s_scr[r0:r0 + n, :] = (_dot_nt(qn, kv[:, c0:c0 + QK_NOPE]) + _dot_nt(qr, kr)) * scale_c
    for pair in range((H_A + H_B + H_C) // 2):
        h0 = 2 * pair
        sinks = ((sink_ref[h0], n), (sink_ref[h0 + 1], n)) if h0 < H_A else None
        _softmax_rows(s_scr, p_scr, slice(h0 * n, (h0 + 2) * n), sinks)
    for h in range(H_A):
        g = h // G_A
        o_scr[:, h * HEAD_DIM:(h + 1) * HEAD_DIM] = _dot(p_scr[h * n:(h + 1) * n, :],
                                                         va[:, g * HEAD_DIM:(g + 1) * HEAD_DIM])
    for h in range(H_B):
        sl = slice(h * HEAD_DIM, (h + 1) * HEAD_DIM)
        o_scr[:, W_QA + h * HEAD_DIM:W_QA + (h + 1) * HEAD_DIM] = _dot(p_scr[(H_A + h) * n:(H_A + h + 1) * n, :],
                                                                     vb[:, sl])
    for h in range(H_C):
        c0 = h * (QK_NOPE + V_C)
        r0 = (H_A + H_B + h) * n
        off = W_QA + W_B + h * V_C
        o_scr[:, off:off + V_C] = _dot(p_scr[r0:r0 + n, :], kv[:, c0 + QK_NOPE:c0 + QK_NOPE + V_C])
    y = _dot(o_scr[...].astype(bf16), wout_ref[...])
    o_ref[...] = x_ref[...] + gate_ref[0] * y


def _ctx_attn(sink, proj, wukv, wout, x, gate):
    qa, ka, va, qb, kb, vb, qc, ckv, kr = proj
    row = lambda b: (b, 0)
    const = lambda b: (0, 0)
    in_specs = [pl.BlockSpec(memory_space=pltpu.SMEM)]
    in_specs += [pl.BlockSpec((SEQ, a.shape[1]), row) for a in proj]
    in_specs += [pl.BlockSpec((KV_LORA, H_C * (QK_NOPE + V_C)), const),
                 pl.BlockSpec((D_MODEL, D_MODEL), const),
                 pl.BlockSpec((SEQ, D_MODEL), row),
                 pl.BlockSpec((1, 1, D_MODEL), lambda b: (0, 0, 0))]
    return pl.pallas_call(
        _ctx_attn_kernel,
        grid=(BATCH,),
        in_specs=in_specs,
        out_specs=pl.BlockSpec((SEQ, D_MODEL), row),
        out_shape=jax.ShapeDtypeStruct((T_CTX, D_MODEL), f32),
        scratch_shapes=[pltpu.VMEM((SEQ, D_MODEL), f32),
                        pltpu.VMEM(((H_A + H_B + H_C) * SEQ, SEQ), f32),
                        pltpu.VMEM(((H_A + H_B + H_C) * SEQ, SEQ), bf16)],
        compiler_params=_cparams("arbitrary"),
        name="ctx_attn",
    )(sink, qa, ka, va, qb, kb, vb, qc, ckv, kr, wukv, wout, x, gate)


def _lat_attn_kernel(sink_ref, qa_ref, qb_ref, qc_ref, ka_ref, va_ref, kb_ref, vb_ref, ckv_ref, kr_ref,
                     cak_ref, cav_ref, cbk_ref, cbv_ref, cckv_ref, ckr_ref, bias_ref,
                     wukv_ref, wout_ref, x_ref, gate_ref, o_ref, o_scr, kv_scr, sa, pa, sb, pb, sc, pc):
    qi = pl.program_id(1)
    nb = DEC_SEQ // BLOCK
    scale = HEAD_DIM ** -0.5

    @pl.when(qi == 0)
    def _():
        kv_scr[0:DEC_SEQ, :] = _dot(ckv_ref[...], wukv_ref[...]).astype(bf16)
        kv_scr[DEC_SEQ:DEC_SEQ + PAST_LEN, :] = _dot(cckv_ref[0, 0].astype(bf16), wukv_ref[...]).astype(bf16)

    def blk(ref, j):
        idx = jnp.clip(qi + j, 0, nb - 1)
        return ref[pl.ds(pl.multiple_of(idx * BLOCK, BLOCK), BLOCK), :]

    ka = jnp.concatenate([blk(ka_ref, -1), blk(ka_ref, 0), blk(ka_ref, 1), cak_ref[0, 0].astype(bf16)], axis=0)
    va = jnp.concatenate([blk(va_ref, -1), blk(va_ref, 0), blk(va_ref, 1), cav_ref[0, 0].astype(bf16)], axis=0)
    nk_a = 3 * BLOCK + PAST_LEN
    r = lax.broadcasted_iota(jnp.int32, (BLOCK, nk_a), 0)
    c = lax.broadcasted_iota(jnp.int32, (BLOCK, nk_a), 1)
    valid = (((c < BLOCK) & (c >= r) & (qi > 0))
             | ((c >= BLOCK) & (c < 2 * BLOCK))
             | ((c >= 2 * BLOCK) & (c < 3 * BLOCK) & (c - 2 * BLOCK <= r) & (qi < nb - 1))
             | (c >= 3 * BLOCK))
    for h in range(H_A):
        g = h // G_A
        q = qa_ref[:, h * HEAD_DIM:(h + 1) * HEAD_DIM]
        s = _dot_nt(q, ka[:, g * HEAD_DIM:(g + 1) * HEAD_DIM]) * scale
        sa[h * BLOCK:(h + 1) * BLOCK, :] = jnp.where(valid, s, NEG)

    cbk = cbk_ref[0, 0].astype(bf16)
    cbv = cbv_ref[0, 0].astype(bf16)
    rows_per_blk = BLOCK // GRID_W
    nloc = NA_ROWS * GRID_W
    vcats = []
    for half in range(rows_per_blk):
        grow = qi * rows_per_blk + half
        start = jnp.clip(grow - NA_ROWS // 2, 0, ROWS - NA_ROWS)
        kloc = kb_ref[pl.ds(pl.multiple_of(start * GRID_W, GRID_W), nloc), :]
        vloc = vb_ref[pl.ds(pl.multiple_of(start * GRID_W, GRID_W), nloc), :]
        vcats.append(jnp.concatenate([vloc, cbv], axis=0))
        qrows = slice(half * GRID_W, (half + 1) * GRID_W)
        dr0 = start - grow + (NA_ROWS - 1)
        for h in range(H_B):
            sl = slice(h * HEAD_DIM, (h + 1) * HEAD_DIM)
            q = qb_ref[qrows, sl]
            bias = jnp.concatenate([bias_ref[h, dr0 + 2 * j] for j in range(NA_ROWS // 2)], axis=1)
            s_loc = _dot_nt(q, kloc[:, sl]) * scale + bias
            s_ctx = _dot_nt(q, cbk[:, sl]) * scale
            r0 = (half * H_B + h) * GRID_W
            sb[r0:r0 + GRID_W, :] = jnp.concatenate([s_loc, s_ctx], axis=1)

    kr = jnp.concatenate([kr_ref[...], ckr_ref[0, 0].astype(bf16)], axis=0)
    scale_c = (QK_NOPE + QK_ROPE) ** -0.5
    for h in range(H_C):
        qn = qc_ref[:, h * QC_PAD:h * QC_PAD + QK_NOPE]
        qr = qc_ref[:, h * QC_PAD + QK_NOPE:h * QC_PAD + QK_NOPE + QK_ROPE]
        c0 = h * (QK_NOPE + V_C)
        sc[h * BLOCK:(h + 1) * BLOCK, :] = (_dot_nt(qn, kv_scr[:, c0:c0 + QK_NOPE]) + _dot_nt(qr, kr)) * scale_c

    for pair in range(H_A // 2):
        h0 = 2 * pair
        _softmax_rows(sa, pa, slice(h0 * BLOCK, (h0 + 2) * BLOCK), ((sink_ref[h0], BLOCK), (sink_ref[h0 + 1], BLOCK)))
    for blk2 in range(rows_per_blk * H_B // 2):
        _softmax_rows(sb, pb, slice(blk2 * 2 * GRID_W, (blk2 + 1) * 2 * GRID_W))
    for h in range(H_C):
        _softmax_rows(sc, pc, slice(h * BLOCK, (h + 1) * BLOCK))

    for h in range(H_A):
        g = h // G_A
        o_scr[:, h * HEAD_DIM:(h + 1) * HEAD_DIM] = _dot(pa[h * BLOCK:(h + 1) * BLOCK, :],
                                                         va[:, g * HEAD_DIM:(g + 1) * HEAD_DIM])
    for half in range(rows_per_blk):
        qrows = slice(half * GRID_W, (half + 1) * GRID_W)
        for h in range(H_B):
            sl = slice(h * HEAD_DIM, (h + 1) * HEAD_DIM)
            r0 = (half * H_B + h) * GRID_W
            o_scr[qrows, W_QA + h * HEAD_DIM:W_QA + (h + 1) * HEAD_DIM] = _dot(pb[r0:r0 + GRID_W, :],
                                                                             vcats[half][:, sl])
    for h in range(H_C):
        c0 = h * (QK_NOPE + V_C)
        off = W_QA + W_B + h * V_C
        o_scr[:, off:off + V_C] = _dot(pc[h * BLOCK:(h + 1) * BLOCK, :], kv_scr[:, c0 + QK_NOPE:c0 + QK_NOPE + V_C])

    y = _dot(o_scr[...].astype(bf16), wout_ref[...])
    o_ref[...] = x_ref[...] + gate_ref[0] * y


def _lat_attn(layer, sink, proj, caches, bias_tab, wukv, wout, x, lat_row0, gate):
    qa, ka, va, qb, kb, vb, qc, ckv, kr = proj
    nb = DEC_SEQ // BLOCK
    qrow = lambda b, q: (b * nb + q, 0)
    xrow = lambda b, q: (lat_row0 // BLOCK + b * nb + q, 0)
    brow = lambda b, q: (b, 0)
    const = lambda b, q: (0, 0)
    cidx = lambda b, q: (b, layer, 0, 0)
    in_specs = [pl.BlockSpec(memory_space=pltpu.SMEM)]
    in_specs += [pl.BlockSpec((BLOCK, a.shape[1]), qrow) for a in (qa, qb, qc)]
    in_specs += [pl.BlockSpec((DEC_SEQ, a.shape[1]), brow) for a in (ka, va, kb, vb, ckv, kr)]
    in_specs += [pl.BlockSpec((1, 1, PAST_LEN, a.shape[3]), cidx) for a in caches]
    in_specs += [pl.BlockSpec(bias_tab.shape, lambda b, q: (0, 0, 0, 0)),
                 pl.BlockSpec((KV_LORA, H_C * (QK_NOPE + V_C)), const),
                 pl.BlockSpec((D_MODEL, D_MODEL), const),
                 pl.BlockSpec((BLOCK, D_MODEL), xrow),
                 pl.BlockSpec((1, 1, D_MODEL), lambda b, q: (1 + b, 0, 0))]
    return pl.pallas_call(
        _lat_attn_kernel,
        grid=(DEC_BATCH, nb),
        in_specs=in_specs,
        out_specs=pl.BlockSpec((BLOCK, D_MODEL), qrow),
        out_shape=jax.ShapeDtypeStruct((T_LAT, D_MODEL), f32),
        scratch_shapes=[pltpu.VMEM((BLOCK, D_MODEL), f32),
                        pltpu.VMEM((DEC_SEQ + PAST_LEN, H_C * (QK_NOPE + V_C)), bf16)]
        + [pltpu.VMEM(shape, dt) for shape in ((H_A * BLOCK, 3 * BLOCK + PAST_LEN),
                                               (H_B * BLOCK, NA_ROWS * GRID_W + PAST_LEN),
                                               (H_C * BLOCK, DEC_SEQ + PAST_LEN)) for dt in (f32, bf16)],
        compiler_params=_cparams("arbitrary", "arbitrary"),
        name="lat_attn",
    )(sink, qa, qb, qc, ka, va, kb, vb, ckv, kr, *caches, bias_tab, wukv, wout, x, gate)


def _pick_stream(xc_ref, xl_ref, x_scr):
    i = pl.program_id(0)

    @pl.when(i < T_CTX // TM_TOK)
    def _():
        x_scr[...] = xc_ref[...]

    @pl.when(i >= T_CTX // TM_TOK)
    def _():
        x_scr[...] = xl_ref[...]

    return x_scr[...]


def _stream_specs(lat_row0):
    n_ctx = T_CTX // TM_TOK
    return [pl.BlockSpec((TM_TOK, D_MODEL), lambda i: (jnp.minimum(i, n_ctx - 1), 0)),
            pl.BlockSpec((TM_TOK, D_MODEL), lambda i: (lat_row0 // TM_TOK + jnp.maximum(i - n_ctx, 0), 0))]


def _router_kernel(xc_ref, xl_ref, g_ref, sh_ref, sc_ref, wr_ref, br_ref, h_ref, e_ref, gt_ref, x_scr):
    h = _rms(_pick_stream(xc_ref, xl_ref, x_scr), g_ref[...]) * (1.0 + sc_ref[0]) + sh_ref[0]
    _store_row_tiles(h_ref, h)
    logits = jnp.dot(h, wr_ref[...], preferred_element_type=f32, precision=lax.Precision.HIGHEST) + br_ref[...]
    lane = lax.broadcasted_iota(jnp.int32, logits.shape, 1).astype(f32)
    l = jnp.where(lane < N_EXPERTS, logits, -jnp.inf)
    tops, idxs = [], []
    for _ in range(TOP_K):
        m = jnp.max(l, axis=-1, keepdims=True)
        idx = jnp.min(jnp.where(l == m, lane, float(LANE)), axis=-1, keepdims=True)
        tops.append(m)
        idxs.append(idx)
        l = jnp.where(lane == idx, -jnp.inf, l)
    ex = [jnp.exp(t - tops[0]) for t in tops]
    den = ex[0] + ex[1] + ex[2] + ex[3]
    e_out = jnp.zeros(logits.shape, f32)
    g_out = jnp.zeros(logits.shape, f32)
    for k in range(TOP_K):
        e_out = jnp.where(lane == k, idxs[k], e_out)
        g_out = jnp.where(lane == k, ex[k] / den, g_out)
    e_ref[...] = e_out.astype(jnp.int32)
    gt_ref[...] = g_out


def _group_of_tile(i):
    per_b = DEC_SEQ // TM_TOK
    n_ctx = T_CTX // TM_TOK
    return jnp.where(i < n_ctx, 0, 1 + (i - n_ctx) // per_b)


def _router(xc, xl, lat_row0, g, shift, scale, wr, br):
    tm = TM_TOK
    row = lambda i: (i, 0)
    const = lambda i: (0, 0)
    grp = lambda i: (_group_of_tile(i), 0, 0)
    return pl.pallas_call(
        _router_kernel,
        grid=(T_ALL // tm,),
        in_specs=_stream_specs(lat_row0) +
                 [pl.BlockSpec((1, D_MODEL), const),
                  pl.BlockSpec((1, 1, D_MODEL), grp),
                  pl.BlockSpec((1, 1, D_MODEL), grp),
                  pl.BlockSpec((D_MODEL, LANE), const),
                  pl.BlockSpec((1, LANE), const)],
        out_specs=[pl.BlockSpec((tm * ROW_TILE, LANE), row), pl.BlockSpec((tm, LANE), row),
                   pl.BlockSpec((tm, LANE), row)],
        out_shape=[jax.ShapeDtypeStruct((T_ALL * ROW_TILE, LANE), f32),
                   jax.ShapeDtypeStruct((T_ALL, LANE), jnp.int32),
                   jax.ShapeDtypeStruct((T_ALL, LANE), f32)],
        scratch_shapes=[pltpu.VMEM((tm, D_MODEL), f32)],
        compiler_params=_cparams("arbitrary"),
        name="router",
    )(xc, xl, g, shift, scale, wr, br)


def _dispatch_kernel(tok_ref, nu_ref, h_hbm, o_ref, hv, xg, hsem):
    tm = TM_MOE
    i = pl.program_id(0)

    @pl.when(i == 0)
    def _():
        resident = pltpu.make_async_copy(h_hbm, hv, hsem.at[0])
        resident.start()
        resident.wait()

    def one_block(sub, carry):
        blk = i * DISPATCH_BLOCKS + sub
        rows = pl.ds(pl.multiple_of(sub * tm, tm), tm)

        @pl.when(blk < nu_ref[0])
        def _():
            for r in range(tm):
                t = tok_ref[blk * tm + r]
                xg[pl.ds(r * ROW_TILE, ROW_TILE), :] = hv[pl.ds(pl.multiple_of(t * ROW_TILE, ROW_TILE), ROW_TILE), :]
            o_ref[rows, :] = _load_row_tiles(xg).astype(bf16)

        @pl.when(blk >= nu_ref[0])
        def _():
            o_ref[rows, :] = jnp.zeros((tm, D_MODEL), bf16)

        return carry

    lax.fori_loop(0, DISPATCH_BLOCKS, one_block, 0)


def _dispatch(row_tok, n_used, h):
    tm = TM_MOE
    return pl.pallas_call(
        _dispatch_kernel,
        grid_spec=pltpu.PrefetchScalarGridSpec(
            num_scalar_prefetch=2,
            grid=(N_MOE_BLOCKS // DISPATCH_BLOCKS,),
            in_specs=[pl.BlockSpec(memory_space=pl.ANY)],
            out_specs=pl.BlockSpec((DISPATCH_BLOCKS * tm, D_MODEL), lambda i, tok, nu: (i, 0)),
            scratch_shapes=[pltpu.VMEM((T_ALL * ROW_TILE, LANE), f32), pltpu.VMEM((tm * ROW_TILE, LANE), f32),
                            pltpu.SemaphoreType.DMA((1,))]),
        out_shape=jax.ShapeDtypeStruct((N_MOE_BLOCKS * tm, D_MODEL), bf16),
        compiler_params=_cparams("arbitrary"),
        name="dispatch",
    )(row_tok, n_used, h)


def _moe_kernel(layer, be_ref, nu_ref, nxt_ref, dst_ref, x_ref, wgu_hbm, bgu_ref, wd_hbm, bd_ref, y_hbm,
                y0, y1, wgu_st, wd_st, wgu_bf, wd_bf, wsem, ssem):
    tm = TM_MOE
    i = pl.program_id(0)
    nb = pl.num_programs(0)
    used = i < nu_ref[0]
    yb = (y0, y1)

    def out_tile(row):
        return pl.ds(pl.multiple_of(row * ROW_TILE, ROW_TILE), ROW_TILE)

    def scatter_desc(buf, r, dst_row, s):
        return pltpu.make_async_copy(buf.at[out_tile(r)], y_hbm.at[out_tile(dst_row)], ssem.at[s])

    def scatter_wait(s):
        pltpu.make_async_copy(yb[s], y_hbm.at[pl.ds(0, tm * ROW_TILE)], ssem.at[s]).wait()

    def scatter_start(blk, s, unrolled):
        if unrolled:
            for r in range(tm):
                scatter_desc(yb[s], r, dst_ref[(blk + 1) * tm + r], s).start(priority=r % 2)
        else:
            def body(r, carry):
                scatter_desc(yb[s], r, dst_ref[(blk + 1) * tm + r], s).start()
                return carry
            lax.fori_loop(0, tm, body, 0, unroll=8)

    def weight_copies(e):
        return (pltpu.make_async_copy(wgu_hbm.at[layer, e], wgu_st, wsem.at[0]),
                pltpu.make_async_copy(wd_hbm.at[layer, e], wd_st, wsem.at[1]))

    @pl.when(i == 0)
    def _():
        for s in range(2):
            yb[s][...] = jnp.zeros_like(yb[s])
            dummy = pltpu.make_async_copy(yb[s], y_hbm.at[pl.ds((N_ASSIGN + s * tm) * ROW_TILE, tm * ROW_TILE)],
                                          ssem.at[s])
            dummy.start()
            dummy.wait()
        for cp in weight_copies(be_ref[0]):
            cp.start()

    first = jnp.logical_and(used, jnp.logical_or(i == 0, be_ref[i] != be_ref[jnp.maximum(i - 1, 0)]))

    @pl.when(first)
    def _():
        for cp in weight_copies(0):
            cp.wait()
        wgu_bf[...] = wgu_st[...].astype(bf16)
        wd_bf[...] = wd_st[...].astype(bf16)

        @pl.when(nxt_ref[i] >= 0)
        def _():
            for cp in weight_copies(nxt_ref[i]):
                cp.start()

    def step(par):
        cur, oth = par, 1 - par

        @pl.when(i >= 1)
        def _():
            scatter_wait(cur)

        @pl.when(used)
        def _():
            scatter_start(i - 1, oth, unrolled=True)
            gu = _dot(x_ref[...], wgu_bf[...]) + bgu_ref[0, 0]
            x_glu = jnp.minimum(gu[:, :D_FF], SWIGLU_LIMIT)
            x_lin = jnp.clip(gu[:, D_FF:], -SWIGLU_LIMIT, SWIGLU_LIMIT)
            act = x_glu * jax.nn.sigmoid(SWIGLU_ALPHA * x_glu) * (x_lin + 1.0)
            _store_row_tiles(yb[cur], _dot(act.astype(bf16), wd_bf[...]) + bd_ref[0, 0])

        @pl.when(jnp.logical_and(jnp.logical_not(used), i + 1 < nb))
        def _():
            scatter_start(i - 1, oth, unrolled=False)

        @pl.when(i == nb - 1)
        def _():
            scatter_start(i - 1, oth, unrolled=False)
            scatter_wait(oth)

    @pl.when(i % 2 == 0)
    def _():
        step(0)

    @pl.when(i % 2 == 1)
    def _():
        step(1)


def _moe(layer, routing, h, w_gu, b_gu, w_down, b_down):
    tm = TM_MOE
    block_e, n_used, nxt_e, row_tok, row_dst = routing
    xs = _dispatch(row_tok, n_used, h)
    ex4 = lambda i, be, nu, nxt, dst: (layer, be[i], 0, 0)
    return pl.pallas_call(
        functools.partial(_moe_kernel, layer),
        grid_spec=pltpu.PrefetchScalarGridSpec(
            num_scalar_prefetch=4,
            grid=(N_MOE_BLOCKS,),
            in_specs=[pl.BlockSpec((tm, D_MODEL), lambda i, be, nu, nxt, dst: (i, 0)),
                      pl.BlockSpec(memory_space=pl.ANY),
                      pl.BlockSpec((1, 1, 1, 2 * D_FF), ex4),
                      pl.BlockSpec(memory_space=pl.ANY),
                      pl.BlockSpec((1, 1, 1, D_MODEL), ex4)],
            out_specs=pl.BlockSpec(memory_space=pl.ANY),
            scratch_shapes=[pltpu.VMEM((tm * ROW_TILE, LANE), f32), pltpu.VMEM((tm * ROW_TILE, LANE), f32),
                            pltpu.VMEM((D_MODEL, 2 * D_FF), f32), pltpu.VMEM((D_FF, D_MODEL), f32),
                            pltpu.VMEM((D_MODEL, 2 * D_FF), bf16), pltpu.VMEM((D_FF, D_MODEL), bf16),
                            pltpu.SemaphoreType.DMA((2,)), pltpu.SemaphoreType.DMA((2,))]),
        out_shape=jax.ShapeDtypeStruct(((N_ASSIGN + 2 * tm) * ROW_TILE, LANE), f32),
        compiler_params=_cparams("arbitrary"),
        name="moe",
    )(block_e, n_used, nxt_e, row_dst, xs, w_gu, b_gu.reshape(DEPTH, N_EXPERTS, 1, 2 * D_FF),
      w_down, b_down.reshape(DEPTH, N_EXPERTS, 1, D_MODEL))


def _combine_kernel(final, xc_ref, xl_ref, y0_ref, y1_ref, y2_ref, y3_ref, gt_ref, gate_ref, gf_ref, o_ref, x_scr):
    gt = gt_ref[...]
    f = gt[:, 0:1] * _load_row_tiles(y0_ref)
    for k, y_ref in ((1, y1_ref), (2, y2_ref), (3, y3_ref)):
        f = f + gt[:, k:k + 1] * _load_row_tiles(y_ref)
    out = _pick_stream(xc_ref, xl_ref, x_scr) + gate_ref[0] * f
    if final:
        out = _rms(out, gf_ref[...])
    o_ref[...] = out


def _combine(final, xc, xl, lat_row0, y, gates, gate, g_final):
    tm = TM_TOK
    nt = T_ALL // tm
    row = lambda i: (i, 0)
    const = lambda i: (0, 0)
    grp = lambda i: (_group_of_tile(i), 0, 0)
    ysel = [pl.BlockSpec((tm * ROW_TILE, LANE), functools.partial(lambda k, i: (k * nt + i, 0), k))
            for k in range(TOP_K)]
    return pl.pallas_call(
        functools.partial(_combine_kernel, final),
        grid=(nt,),
        in_specs=_stream_specs(lat_row0) + ysel +
                 [pl.BlockSpec((tm, LANE), row),
                  pl.BlockSpec((1, 1, D_MODEL), grp),
                  pl.BlockSpec((1, D_MODEL), const)],
        out_specs=pl.BlockSpec((tm, D_MODEL), row),
        out_shape=jax.ShapeDtypeStruct((T_ALL, D_MODEL), f32),
        scratch_shapes=[pltpu.VMEM((tm, D_MODEL), f32)],
        compiler_params=_cparams("arbitrary"),
        name="combine",
    )(xc, xl, y, y, y, y, gates, gate, g_final)


def _rope_head_tables(d):
    nf = d // 4
    half = d // 2
    t = np.arange(DEC_SEQ)
    inv = ROPE_BASE ** (-np.arange(nf, dtype=np.float32) / nf)
    i = np.arange(d)
    pos = np.where(i[None, :] < half, (t // GRID_W)[:, None], (t % GRID_W)[:, None]).astype(np.float32)
    ang = pos * inv[i % nf][None, :].astype(np.float32)
    first = (i % half) < nf
    cos = np.cos(ang)
    sin = np.where(first[None, :], -np.sin(ang), np.sin(ang))
    partner = np.where(first, i + nf, i - nf)
    return cos.astype(np.float32), sin.astype(np.float32), partner


def _rope_tables():
    cos64, sin64, _ = _rope_head_tables(HEAD_DIM)
    cos32, sin32, _ = _rope_head_tables(QK_ROPE)
    cosa = np.tile(cos64, (1, H_A))
    sina = np.tile(sin64, (1, H_A))
    cosq1 = np.concatenate([np.ones((DEC_SEQ, QK_NOPE), np.float32), cos32,
                            np.ones((DEC_SEQ, QC_PAD - QK_NOPE - QK_ROPE), np.float32)], axis=1)
    sinq1 = np.concatenate([np.zeros((DEC_SEQ, QK_NOPE), np.float32), sin32,
                            np.zeros((DEC_SEQ, QC_PAD - QK_NOPE - QK_ROPE), np.float32)], axis=1)
    cosq = np.tile(cosq1, (1, H_C))
    sinq = np.tile(sinq1, (1, H_C))
    return tuple(jnp.asarray(a) for a in (cosa, sina, cosq, sinq, cos32, sin32))


def _pad_cols(w, n):
    return jnp.pad(w, ((0, 0), (0, n - w.shape[1])))


def _layer_weights(w_in, w_uq):
    cuts = np.cumsum((W_QA, W_KA, W_VA, W_B, W_B, W_B, Q_LORA, KV_LORA, QK_ROPE))[:-1]
    qa, ka, va, qb, kb, vb, cq, ckv, kr = jnp.split(w_in, [int(c) for c in cuts], axis=1)
    _, _, p64 = _rope_head_tables(HEAD_DIM)
    _, _, p32 = _rope_head_tables(QK_ROPE)
    pa = np.concatenate([h * HEAD_DIM + p64 for h in range(H_A)])
    base = jnp.concatenate([qa, ka, va, _pad_cols(qb, 384), _pad_cols(kb, 384), _pad_cols(vb, 384), cq, ckv,
                            _pad_cols(kr, 128)], axis=1)
    w_ctx = base.astype(bf16)
    w_lat = jnp.concatenate([base, qa[:, pa], ka[:, pa[:W_KA]], _pad_cols(kr[:, p32], 128)], axis=1).astype(bf16)
    hq = QK_NOPE + QK_ROPE
    heads = [_pad_cols(w_uq[:, h * hq:(h + 1) * hq], QC_PAD) for h in range(H_C)]
    pq = np.concatenate([np.arange(QK_NOPE), QK_NOPE + p32])
    heads_p = [_pad_cols(w_uq[:, h * hq:(h + 1) * hq][:, pq], QC_PAD) for h in range(H_C)]
    wuq = jnp.concatenate(heads, axis=1).astype(bf16)
    wuq2 = jnp.concatenate(heads + heads_p, axis=1).astype(bf16)
    return w_ctx, w_lat, wuq, wuq2


def _bias_table(rpb):
    col = np.arange(GRID_W)
    col_start = np.clip(col - NA_COLS // 2, 0, GRID_W - NA_COLS)
    col_ok = (col[None, :] >= col_start[:, None]) & (col[None, :] < col_start[:, None] + NA_COLS)
    dc = np.clip(col[None, :] - col[:, None] + (NA_COLS - 1), 0, 2 * NA_COLS - 2)
    onehot = (dc[None] == np.arange(2 * NA_COLS - 1)[:, None, None]).astype(np.float32)
    expanded = jnp.einsum('hrd,dqk->hrqk', rpb.astype(f32), jnp.asarray(onehot), precision=lax.Precision.HIGHEST)
    blocks = jnp.where(col_ok[None, None], expanded, NEG)
    return jnp.concatenate([blocks[:, :-1], blocks[:, 1:]], axis=-1)


def _routing(top_e):
    tm = TM_MOE
    key_bits = 16
    pad_mark = (1 << key_bits) - 1
    flat_e = top_e.T.reshape(N_ASSIGN)
    experts = jnp.arange(N_EXPERTS, dtype=jnp.int32)
    counts = jnp.sum((flat_e[:, None] == experts[None, :]).astype(jnp.int32), axis=0)
    nblk = (counts + tm - 1) // tm
    blk_end = jnp.cumsum(nblk)
    pad_end = jnp.cumsum(nblk * tm - counts)
    slots = jnp.arange(N_MOE_BLOCKS * tm - N_ASSIGN, dtype=jnp.int32)
    pad_e = jnp.sum((pad_end[None, :] <= slots[:, None]).astype(jnp.int32), axis=1)
    keys = jnp.concatenate([(flat_e << key_bits) + jnp.arange(N_ASSIGN, dtype=jnp.int32),
                            (pad_e << key_bits) + pad_mark])
    asg = (jnp.sort(keys) & pad_mark).reshape(N_MOE_BLOCKS, tm)
    valid = asg != pad_mark
    blocks = jnp.arange(N_MOE_BLOCKS, dtype=jnp.int32)
    r = jnp.arange(tm, dtype=jnp.int32)[None, :]
    tok = jnp.where(valid, asg % T_ALL, 0)
    row_dst = jnp.where(valid, asg, N_ASSIGN + (blocks[:, None] % 2) * tm + r)
    row_dst = jnp.concatenate([N_ASSIGN + tm + r, row_dst], axis=0).reshape(-1)
    block_e = jnp.minimum(jnp.sum((blk_end[None, :] <= blocks[:, None]).astype(jnp.int32), axis=1), N_EXPERTS - 1)
    n_used = blk_end[-1].astype(jnp.int32).reshape(1)
    has = jnp.where(counts > 0, experts, N_EXPERTS)
    later = experts[None, :] > experts[:, None]
    nxt = jnp.min(jnp.where(later, has[None, :], N_EXPERTS), axis=1)
    nxt = jnp.where(nxt >= N_EXPERTS, -1, nxt)
    sel = (block_e[:, None] == experts[None, :]).astype(jnp.int32)
    nxt_e = jnp.sum(sel * nxt[None, :], axis=1)
    i32 = lambda a: a.astype(jnp.int32)
    return i32(block_e), n_used, i32(nxt_e), i32(tok).reshape(-1), i32(row_dst)


def kernel(x_prompt, x_sample, cache_a_k, cache_a_v, cache_b_k, cache_b_v, cache_c_kv, cache_c_kr, c, c_ctx, w_ada, b_ada, g_attn, g_ffn, w_in, sink_a, rpb_b, g_cq, g_ckv, w_uq, w_ukv, w_out, w_router, b_router, w_gu, b_gu, w_down, b_down, g_final):
    xc, xl, lat_row0 = x_prompt.reshape(T_CTX, D_MODEL), x_sample.reshape(T_LAT, D_MODEL), 0
    cvec = jnp.concatenate([c_ctx[None, :], c, jnp.zeros((8 - N_GROUPS, D_MODEL), f32)], axis=0)
    mods = _ada(cvec, w_ada, b_ada)[:, :N_GROUPS].reshape(DEPTH, N_GROUPS, 6, 1, D_MODEL)
    tabs = _rope_tables()
    caches = (cache_a_k.reshape(DEC_BATCH, DEPTH, PAST_LEN, W_KA), cache_a_v.reshape(DEC_BATCH, DEPTH, PAST_LEN, W_VA),
              cache_b_k.reshape(DEC_BATCH, DEPTH, PAST_LEN, W_B), cache_b_v.reshape(DEC_BATCH, DEPTH, PAST_LEN, W_B),
              cache_c_kv, cache_c_kr)
    new = [[] for _ in range(6)]
    for layer in range(DEPTH):
        m = [mods[layer, :, j] for j in range(6)]
        w_ctx, w_lat, wuq, wuq2 = _layer_weights(w_in[layer], w_uq[layer])
        wukv = w_ukv[layer].astype(bf16)
        wout = w_out[layer].astype(bf16)
        g1 = g_attn[layer][None, :]
        gcq = g_cq[layer][None, :]
        gckv = g_ckv[layer][None, :]
        sink = sink_a[layer]

        pc = _inproj_ctx(xc, g1, m[0], m[1], w_ctx, gcq, gckv, wuq)
        for lst, a in zip(new, (pc[1], pc[2], pc[4], pc[5], pc[7], pc[8])):
            lst.append(a)
        x_ctx = _ctx_attn(sink, pc, wukv, wout, xc, m[2])

        plat = _inproj_lat(xl, lat_row0, g1, m[0], m[1], w_lat, gcq, gckv, wuq2, tabs)
        x_lat = _lat_attn(layer, sink, plat, caches, _bias_table(rpb_b[layer]), wukv, wout, xl, lat_row0, m[2])

        wr = _pad_cols(w_router[layer], LANE)
        br = _pad_cols(b_router[layer][None, :], LANE)
        h2, top_e, gates = _router(x_ctx, x_lat, 0, g_ffn[layer][None, :], m[3], m[4], wr, br)
        y = _moe(layer, _routing(top_e[:, :TOP_K]), h2, w_gu, b_gu, w_down, b_down)
        x = _combine(layer == DEPTH - 1, x_ctx, x_lat, 0, y, gates, m[5], g_final[None, :])
        xc, xl, lat_row0 = x, x, T_CTX

    y_prompt = x[:T_CTX].reshape(BATCH, SEQ, D_MODEL)
    y_sample = x[T_CTX:].reshape(DEC_BATCH, DEC_SEQ, D_MODEL)
    shapes = ((KV_A, HEAD_DIM), (KV_A, HEAD_DIM), (H_B, HEAD_DIM), (H_B, HEAD_DIM), (KV_LORA,), (QK_ROPE,))
    outs = [jnp.stack([a.reshape((BATCH, SEQ) + s) for a in lst], axis=1) for lst, s in zip(new, shapes)]
    return (y_prompt, y_sample, *outs)
```

```python
import functools

import numpy as np
import jax
import jax.numpy as jnp
from jax import lax
from jax.experimental import pallas as pl
from jax.experimental.pallas import tpu as pltpu

D_MODEL = 1024
BATCH = 32
SEQ = 256
DEPTH = 2
DEC_BATCH = 2
DEC_SEQ = 1024
PAST_LEN = 512
GRID_W = 64
HEAD_DIM = 64
H_A = 6
KV_A = 2
G_A = H_A // KV_A
WINDOW = 128
BLOCK = 128
H_B = 5
NA_ROWS = 8
NA_COLS = 16
H_C = 5
Q_LORA = 384
KV_LORA = 256
QK_NOPE = 64
QK_ROPE = 32
V_C = 64
N_EXPERTS = 32
TOP_K = 4
D_FF = 1024
SWIGLU_ALPHA = 1.702
SWIGLU_LIMIT = 7.0
ROPE_BASE = 10000.0
EPS = 1e-6
NEG = -1e30

T_CTX = BATCH * SEQ
T_LAT = DEC_BATCH * DEC_SEQ
T_ALL = T_CTX + T_LAT
N_GROUPS = 1 + DEC_BATCH
LANE = 128
QC_PAD = 128
ROWS = DEC_SEQ // GRID_W

W_QA, W_KA, W_VA = H_A * HEAD_DIM, KV_A * HEAD_DIM, KV_A * HEAD_DIM
W_B = H_B * HEAD_DIM
OFF_QA = 0
OFF_KA = 384
OFF_VA = 512
OFF_QB = 640
OFF_KB = 1024
OFF_VB = 1408
OFF_CQ = 1792
OFF_CKV = 2176
OFF_KR = 2432
NW_CTX = 2560
OFF_QA_P = 2560
OFF_KA_P = 2944
OFF_KR_P = 3072
NW_LAT = 3200

TM_TOK = 512
TM_LAT_IN = 512
TM_MOE = 256
N_ASSIGN = T_ALL * TOP_K
N_MOE_BLOCKS = N_ASSIGN // TM_MOE + N_EXPERTS
DISPATCH_BLOCKS = 4
VMEM_LIMIT = 56 * 1024 * 1024

f32 = jnp.float32
bf16 = jnp.bfloat16


def _cparams(*sem):
    return pltpu.CompilerParams(dimension_semantics=sem, vmem_limit_bytes=VMEM_LIMIT)


def _rms(xf, g):
    return xf * lax.rsqrt(jnp.mean(xf * xf, axis=-1, keepdims=True) + EPS) * g


def _dot(a, b):
    return jnp.dot(a, b, preferred_element_type=f32)


def _dot_nt(a, b):
    return lax.dot_general(a, b, (((1,), (1,)), ((), ())), preferred_element_type=f32)


ROW_TILE = D_MODEL // LANE


def _store_row_tiles(ref, val):
    n = val.shape[0]
    for c in range(ROW_TILE):
        ref[pl.ds(c, n, stride=ROW_TILE), :] = val[:, c * LANE:(c + 1) * LANE]


def _load_row_tiles(ref):
    n = ref.shape[0] // ROW_TILE
    return jnp.concatenate([ref[pl.ds(c, n, stride=ROW_TILE), :] for c in range(ROW_TILE)], axis=1)


def _softmax_rows(s_ref, p_ref, rows, sinks=None):
    s = s_ref[rows, :]
    m = jnp.max(s, axis=-1, keepdims=True)
    if sinks is not None:
        sink = jnp.concatenate([jnp.full((n, 1), v, f32) for v, n in sinks], axis=0)
        m = jnp.maximum(m, sink)
    p = jnp.exp(s - m)
    l = jnp.sum(p, axis=-1, keepdims=True)
    if sinks is not None:
        l = l + jnp.exp(sink - m)
    p_ref[rows, :] = (p * (1.0 / l)).astype(bf16)


def _ada_kernel(c_ref, w_ref, b_ref, o_ref):
    c = c_ref[...]
    s = c * jax.nn.sigmoid(c)
    o_ref[0] = jnp.dot(s, w_ref[0], preferred_element_type=f32, precision=lax.Precision.HIGHEST) + b_ref[0]


def _ada(cvec, w_ada, b_ada):
    tn = 1536
    return pl.pallas_call(
        _ada_kernel,
        grid=(DEPTH, 6 * D_MODEL // tn),
        in_specs=[pl.BlockSpec((8, D_MODEL), lambda l, j: (0, 0)),
                  pl.BlockSpec((1, D_MODEL, tn), lambda l, j: (l, 0, j)),
                  pl.BlockSpec((1, 1, tn), lambda l, j: (l, 0, j))],
        out_specs=pl.BlockSpec((1, 8, tn), lambda l, j: (l, 0, j)),
        out_shape=jax.ShapeDtypeStruct((DEPTH, 8, 6 * D_MODEL), f32),
        compiler_params=_cparams("arbitrary", "arbitrary"),
        name="ada",
    )(cvec, w_ada, b_ada.reshape(DEPTH, 1, 6 * D_MODEL))


def _inproj_ctx_kernel(x_ref, g_ref, sh_ref, sc_ref, w_ref, gcq_ref, gckv_ref, wuq_ref,
                       qa_ref, ka_ref, va_ref, qb_ref, kb_ref, vb_ref, qc_ref, ckv_ref, kr_ref):
    h = _rms(x_ref[...], g_ref[...]) * (1.0 + sc_ref[0]) + sh_ref[0]
    p = _dot(h.astype(bf16), w_ref[...])
    qa_ref[...] = p[:, OFF_QA:OFF_QA + W_QA].astype(bf16)
    ka_ref[...] = p[:, OFF_KA:OFF_KA + W_KA]
    va_ref[...] = p[:, OFF_VA:OFF_VA + W_VA]
    qb_ref[...] = p[:, OFF_QB:OFF_QB + W_B].astype(bf16)
    kb_ref[...] = p[:, OFF_KB:OFF_KB + W_B]
    vb_ref[...] = p[:, OFF_VB:OFF_VB + W_B]
    cqn = _rms(p[:, OFF_CQ:OFF_CQ + Q_LORA], gcq_ref[...])
    qc_ref[...] = _dot(cqn.astype(bf16), wuq_ref[...]).astype(bf16)
    ckv_ref[...] = _rms(p[:, OFF_CKV:OFF_CKV + KV_LORA], gckv_ref[...])
    kr_ref[...] = p[:, OFF_KR:OFF_KR + QK_ROPE]


def _inproj_ctx(x, g, shift, scale, w, gcq, gckv, wuq):
    tm = TM_TOK
    row = lambda i: (i, 0)
    const = lambda i: (0, 0)
    widths = (W_QA, W_KA, W_VA, W_B, W_B, W_B, H_C * QC_PAD, KV_LORA, QK_ROPE)
    dtypes = (bf16, f32, f32, bf16, f32, f32, bf16, f32, f32)
    return pl.pallas_call(
        _inproj_ctx_kernel,
        grid=(T_CTX // tm,),
        in_specs=[pl.BlockSpec((tm, D_MODEL), row),
                  pl.BlockSpec((1, D_MODEL), const),
                  pl.BlockSpec((1, 1, D_MODEL), lambda i: (0, 0, 0)),
                  pl.BlockSpec((1, 1, D_MODEL), lambda i: (0, 0, 0)),
                  pl.BlockSpec((D_MODEL, NW_CTX), const),
                  pl.BlockSpec((1, Q_LORA), const),
                  pl.BlockSpec((1, KV_LORA), const),
                  pl.BlockSpec((Q_LORA, H_C * QC_PAD), const)],
        out_specs=[pl.BlockSpec((tm, wd), row) for wd in widths],
        out_shape=[jax.ShapeDtypeStruct((T_CTX, wd), dt) for wd, dt in zip(widths, dtypes)],
        compiler_params=_cparams("arbitrary"),
        name="inproj_ctx",
    )(x, g, shift, scale, w, gcq, gckv, wuq)


def _inproj_lat_kernel(x_ref, g_ref, sh_ref, sc_ref, w_ref, gcq_ref, gckv_ref, wuq_ref,
                       cosa_ref, sina_ref, cosq_ref, sinq_ref, cosr_ref, sinr_ref,
                       qa_ref, ka_ref, va_ref, qb_ref, kb_ref, vb_ref, qc_ref, ckv_ref, kr_ref):
    h = _rms(x_ref[...], g_ref[...]) * (1.0 + sc_ref[0]) + sh_ref[0]
    p = _dot(h.astype(bf16), w_ref[...])
    cosa = cosa_ref[...]
    sina = sina_ref[...]
    qa = p[:, OFF_QA:OFF_QA + W_QA] * cosa + p[:, OFF_QA_P:OFF_QA_P + W_QA] * sina
    ka = p[:, OFF_KA:OFF_KA + W_KA] * cosa[:, :W_KA] + p[:, OFF_KA_P:OFF_KA_P + W_KA] * sina[:, :W_KA]
    kr = p[:, OFF_KR:OFF_KR + QK_ROPE] * cosr_ref[...] + p[:, OFF_KR_P:OFF_KR_P + QK_ROPE] * sinr_ref[...]
    qa_ref[...] = qa.astype(bf16)
    ka_ref[...] = ka.astype(bf16)
    va_ref[...] = p[:, OFF_VA:OFF_VA + W_VA].astype(bf16)
    qb_ref[...] = p[:, OFF_QB:OFF_QB + W_B].astype(bf16)
    kb_ref[...] = p[:, OFF_KB:OFF_KB + W_B].astype(bf16)
    vb_ref[...] = p[:, OFF_VB:OFF_VB + W_B].astype(bf16)
    cqn = _rms(p[:, OFF_CQ:OFF_CQ + Q_LORA], gcq_ref[...])
    q2 = _dot(cqn.astype(bf16), wuq_ref[...])
    nq = H_C * QC_PAD
    qc_ref[...] = (q2[:, :nq] * cosq_ref[...] + q2[:, nq:] * sinq_ref[...]).astype(bf16)
    ckv_ref[...] = _rms(p[:, OFF_CKV:OFF_CKV + KV_LORA], gckv_ref[...]).astype(bf16)
    kr_ref[...] = kr.astype(bf16)


def _inproj_lat(x, lat_row0, g, shift, scale, w, gcq, gckv, wuq2, tabs):
    tm = TM_LAT_IN
    per_b = DEC_SEQ // tm
    row0 = lat_row0 // tm
    xrow = lambda i: (row0 + i, 0)
    row = lambda i: (i, 0)
    const = lambda i: (0, 0)
    grp = lambda i: (1 + i // per_b, 0, 0)
    pos = lambda i: (i % per_b, 0)
    cosa, sina, cosq, sinq, cosr, sinr = tabs
    widths = (W_QA, W_KA, W_VA, W_B, W_B, W_B, H_C * QC_PAD, KV_LORA, QK_ROPE)
    return pl.pallas_call(
        _inproj_lat_kernel,
        grid=(T_LAT // tm,),
        in_specs=[pl.BlockSpec((tm, D_MODEL), xrow),
                  pl.BlockSpec((1, D_MODEL), const),
                  pl.BlockSpec((1, 1, D_MODEL), grp),
                  pl.BlockSpec((1, 1, D_MODEL), grp),
                  pl.BlockSpec((D_MODEL, NW_LAT), const),
                  pl.BlockSpec((1, Q_LORA), const),
                  pl.BlockSpec((1, KV_LORA), const),
                  pl.BlockSpec((Q_LORA, 2 * H_C * QC_PAD), const),
                  pl.BlockSpec((tm, W_QA), pos), pl.BlockSpec((tm, W_QA), pos),
                  pl.BlockSpec((tm, H_C * QC_PAD), pos), pl.BlockSpec((tm, H_C * QC_PAD), pos),
                  pl.BlockSpec((tm, QK_ROPE), pos), pl.BlockSpec((tm, QK_ROPE), pos)],
        out_specs=[pl.BlockSpec((tm, wd), row) for wd in widths],
        out_shape=[jax.ShapeDtypeStruct((T_LAT, wd), bf16) for wd in widths],
        compiler_params=_cparams("arbitrary"),
        name="inproj_lat",
    )(x, g, shift, scale, w, gcq, gckv, wuq2, cosa, sina, cosq, sinq, cosr, sinr)


def _ctx_attn_kernel(sink_ref, qa_ref, ka_ref, va_ref, qb_ref, kb_ref, vb_ref, qc_ref, ckv_ref, kr_ref,
                     wukv_ref, wout_ref, x_ref, gate_ref, o_ref, o_scr, s_scr, p_scr):
    n = SEQ
    scale = HEAD_DIM ** -0.5
    scale_c = (QK_NOPE + QK_ROPE) ** -0.5
    ka = ka_ref[...].astype(bf16)
    va = va_ref[...].astype(bf16)
    kb = kb_ref[...].astype(bf16)
    vb = vb_ref[...].astype(bf16)
    kv = _dot(ckv_ref[...].astype(bf16), wukv_ref[...]).astype(bf16)
    kr = kr_ref[...].astype(bf16)
    for h in range(H_A):
        g = h // G_A
        q = qa_ref[:, h * HEAD_DIM:(h + 1) * HEAD_DIM]
        s_scr[h * n:(h + 1) * n, :] = _dot_nt(q, ka[:, g * HEAD_DIM:(g + 1) * HEAD_DIM]) * scale
    for h in range(H_B):
        sl = slice(h * HEAD_DIM, (h + 1) * HEAD_DIM)
        s_scr[(H_A + h) * n:(H_A + h + 1) * n, :] = _dot_nt(qb_ref[:, sl], kb[:, sl]) * scale
    for h in range(H_C):
        qn = qc_ref[:, h * QC_PAD:h * QC_PAD + QK_NOPE]
        qr = qc_ref[:, h * QC_PAD + QK_NOPE:h * QC_PAD + QK_NOPE + QK_ROPE]
        c0 = h * (QK_NOPE + V_C)
        r0 = (H_A + H_B + h) * n
        s_scr[r0:r0 + n, :] = (_dot_nt(qn, kv[:, c0:c0 + QK_NOPE]) + _dot_nt(qr, kr)) * scale_c
    for pair in range((H_A + H_B + H_C) // 2):
        h0 = 2 * pair
        sinks = ((sink_ref[h0], n), (sink_ref[h0 + 1], n)) if h0 < H_A else None
        _softmax_rows(s_scr, p_scr, slice(h0 * n, (h0 + 2) * n), sinks)
    for h in range(H_A):
        g = h // G_A
        o_scr[:, h * HEAD_DIM:(h + 1) * HEAD_DIM] = _dot(p_scr[h * n:(h + 1) * n, :],
                                                         va[:, g * HEAD_DIM:(g + 1) * HEAD_DIM])
    for h in range(H_B):
        sl = slice(h * HEAD_DIM, (h + 1) * HEAD_DIM)
        o_scr[:, W_QA + h * HEAD_DIM:W_QA + (h + 1) * HEAD_DIM] = _dot(p_scr[(H_A + h) * n:(H_A + h + 1) * n, :],
                                                                     vb[:, sl])
    for h in range(H_C):
        c0 = h * (QK_NOPE + V_C)
        r0 = (H_A + H_B + h) * n
        off = W_QA + W_B + h * V_C
        o_scr[:, off:off + V_C] = _dot(p_scr[r0:r0 + n, :], kv[:, c0 + QK_NOPE:c0 + QK_NOPE + V_C])
    y = _dot(o_scr[...].astype(bf16), wout_ref[...])
    o_ref[...] = x_ref[...] + gate_ref[0] * y


def _ctx_attn(sink, proj, wukv, wout, x, gate):
    qa, ka, va, qb, kb, vb, qc, ckv, kr = proj
    row = lambda b: (b, 0)
    const = lambda b: (0, 0)
    in_specs = [pl.BlockSpec(memory_space=pltpu.SMEM)]
    in_specs += [pl.BlockSpec((SEQ, a.shape[1]), row) for a in proj]
    in_specs += [pl.BlockSpec((KV_LORA, H_C * (QK_NOPE + V_C)), const),
                 pl.BlockSpec((D_MODEL, D_MODEL), const),
                 pl.BlockSpec((SEQ, D_MODEL), row),
                 pl.BlockSpec((1, 1, D_MODEL), lambda b: (0, 0, 0))]
    return pl.pallas_call(
        _ctx_attn_kernel,
        grid=(BATCH,),
        in_specs=in_specs,
        out_specs=pl.BlockSpec((SEQ, D_MODEL), row),
        out_shape=jax.ShapeDtypeStruct((T_CTX, D_MODEL), f32),
        scratch_shapes=[pltpu.VMEM((SEQ, D_MODEL), f32),
                        pltpu.VMEM(((H_A + H_B + H_C) * SEQ, SEQ), f32),
                        pltpu.VMEM(((H_A + H_B + H_C) * SEQ, SEQ), bf16)],
        compiler_params=_cparams("arbitrary"),
        name="ctx_attn",
    )(sink, qa, ka, va, qb, kb, vb, qc, ckv, kr, wukv, wout, x, gate)


def _lat_attn_kernel(sink_ref, qa_ref, qb_ref, qc_ref, ka_ref, va_ref, kb_ref, vb_ref, ckv_ref, kr_ref,
                     cak_ref, cav_ref, cbk_ref, cbv_ref, cckv_ref, ckr_ref, bias_ref,
                     wukv_ref, wout_ref, x_ref, gate_ref, o_ref, o_scr, kv_scr, sa, pa, sb, pb, sc, pc):
    qi = pl.program_id(1)
    nb = DEC_SEQ // BLOCK
    scale = HEAD_DIM ** -0.5

    @pl.when(qi == 0)
    def _():
        kv_scr[0:DEC_SEQ, :] = _dot(ckv_ref[...], wukv_ref[...]).astype(bf16)
        kv_scr[DEC_SEQ:DEC_SEQ + PAST_LEN, :] = _dot(cckv_ref[0, 0].astype(bf16), wukv_ref[...]).astype(bf16)

    def blk(ref, j):
        idx = jnp.clip(qi + j, 0, nb - 1)
        return ref[pl.ds(pl.multiple_of(idx * BLOCK, BLOCK), BLOCK), :]

    ka = jnp.concatenate([blk(ka_ref, -1), blk(ka_ref, 0), blk(ka_ref, 1), cak_ref[0, 0].astype(bf16)], axis=0)
    va = jnp.concatenate([blk(va_ref, -1), blk(va_ref, 0), blk(va_ref, 1), cav_ref[0, 0].astype(bf16)], axis=0)
    nk_a = 3 * BLOCK + PAST_LEN
    r = lax.broadcasted_iota(jnp.int32, (BLOCK, nk_a), 0)
    c = lax.broadcasted_iota(jnp.int32, (BLOCK, nk_a), 1)
    valid = (((c < BLOCK) & (c >= r) & (qi > 0))
             | ((c >= BLOCK) & (c < 2 * BLOCK))
             | ((c >= 2 * BLOCK) & (c < 3 * BLOCK) & (c - 2 * BLOCK <= r) & (qi < nb - 1))
             | (c >= 3 * BLOCK))
    for h in range(H_A):
        g = h // G_A
        q = qa_ref[:, h * HEAD_DIM:(h + 1) * HEAD_DIM]
        s = _dot_nt(q, ka[:, g * HEAD_DIM:(g + 1) * HEAD_DIM]) * scale
        sa[h * BLOCK:(h + 1) * BLOCK, :] = jnp.where(valid, s, NEG)

    cbk = cbk_ref[0, 0].astype(bf16)
    cbv = cbv_ref[0, 0].astype(bf16)
    rows_per_blk = BLOCK // GRID_W
    nloc = NA_ROWS * GRID_W
    vcats = []
    for half in range(rows_per_blk):
        grow = qi * rows_per_blk + half
        start = jnp.clip(grow - NA_ROWS // 2, 0, ROWS - NA_ROWS)
        kloc = kb_ref[pl.ds(pl.multiple_of(start * GRID_W, GRID_W), nloc), :]
        vloc = vb_ref[pl.ds(pl.multiple_of(start * GRID_W, GRID_W), nloc), :]
        vcats.append(jnp.concatenate([vloc, cbv], axis=0))
        qrows = slice(half * GRID_W, (half + 1) * GRID_W)
        dr0 = start - grow + (NA_ROWS - 1)
        for h in range(H_B):
            sl = slice(h * HEAD_DIM, (h + 1) * HEAD_DIM)
            q = qb_ref[qrows, sl]
            bias = jnp.concatenate([bias_ref[h, dr0 + 2 * j] for j in range(NA_ROWS // 2)], axis=1)
            s_loc = _dot_nt(q, kloc[:, sl]) * scale + bias
            s_ctx = _dot_nt(q, cbk[:, sl]) * scale
            r0 = (half * H_B + h) * GRID_W
            sb[r0:r0 + GRID_W, :] = jnp.concatenate([s_loc, s_ctx], axis=1)

    kr = jnp.concatenate([kr_ref[...], ckr_ref[0, 0].astype(bf16)], axis=0)
    scale_c = (QK_NOPE + QK_ROPE) ** -0.5
    for h in range(H_C):
        qn = qc_ref[:, h * QC_PAD:h * QC_PAD + QK_NOPE]
        qr = qc_ref[:, h * QC_PAD + QK_NOPE:h * QC_PAD + QK_NOPE + QK_ROPE]
        c0 = h * (QK_NOPE + V_C)
        sc[h * BLOCK:(h + 1) * BLOCK, :] = (_dot_nt(qn, kv_scr[:, c0:c0 + QK_NOPE]) + _dot_nt(qr, kr)) * scale_c

    for pair in range(H_A // 2):
        h0 = 2 * pair
        _softmax_rows(sa, pa, slice(h0 * BLOCK, (h0 + 2) * BLOCK), ((sink_ref[h0], BLOCK), (sink_ref[h0 + 1], BLOCK)))
    for blk2 in range(rows_per_blk * H_B // 2):
        _softmax_rows(sb, pb, slice(blk2 * 2 * GRID_W, (blk2 + 1) * 2 * GRID_W))
    for h in range(H_C):
        _softmax_rows(sc, pc, slice(h * BLOCK, (h + 1) * BLOCK))

    for h in range(H_A):
        g = h // G_A
        o_scr[:, h * HEAD_DIM:(h + 1) * HEAD_DIM] = _dot(pa[h * BLOCK:(h + 1) * BLOCK, :],
                                                         va[:, g * HEAD_DIM:(g + 1) * HEAD_DIM])
    for half in range(rows_per_blk):
        qrows = slice(half * GRID_W, (half + 1) * GRID_W)
        for h in range(H_B):
            sl = slice(h * HEAD_DIM, (h + 1) * HEAD_DIM)
            r0 = (half * H_B + h) * GRID_W
            o_scr[qrows, W_QA + h * HEAD_DIM:W_QA + (h + 1) * HEAD_DIM] = _dot(pb[r0:r0 + GRID_W, :],
                                                                             vcats[half][:, sl])
    for h in range(H_C):
        c0 = h * (QK_NOPE + V_C)
        off = W_QA + W_B + h * V_C
        o_scr[:, off:off + V_C] = _dot(pc[h * BLOCK:(h + 1) * BLOCK, :], kv_scr[:, c0 + QK_NOPE:c0 + QK_NOPE + V_C])

    y = _dot(o_scr[...].astype(bf16), wout_ref[...])
    o_ref[...] = x_ref[...] + gate_ref[0] * y


def _lat_attn(layer, sink, proj, caches, bias_tab, wukv, wout, x, lat_row0, gate):
    qa, ka, va, qb, kb, vb, qc, ckv, kr = proj
    nb = DEC_SEQ // BLOCK
    qrow = lambda b, q: (b * nb + q, 0)
    xrow = lambda b, q: (lat_row0 // BLOCK + b * nb + q, 0)
    brow = lambda b, q: (b, 0)
    const = lambda b, q: (0, 0)
    cidx = lambda b, q: (b, layer, 0, 0)
    in_specs = [pl.BlockSpec(memory_space=pltpu.SMEM)]
    in_specs += [pl.BlockSpec((BLOCK, a.shape[1]), qrow) for a in (qa, qb, qc)]
    in_specs += [pl.BlockSpec((DEC_SEQ, a.shape[1]), brow) for a in (ka, va, kb, vb, ckv, kr)]
    in_specs += [pl.BlockSpec((1, 1, PAST_LEN, a.shape[3]), cidx) for a in caches]
    in_specs += [pl.BlockSpec(bias_tab.shape, lambda b, q: (0, 0, 0, 0)),
                 pl.BlockSpec((KV_LORA, H_C * (QK_NOPE + V_C)), const),
                 pl.BlockSpec((D_MODEL, D_MODEL), const),
                 pl.BlockSpec((BLOCK, D_MODEL), xrow),
                 pl.BlockSpec((1, 1, D_MODEL), lambda b, q: (1 + b, 0, 0))]
    return pl.pallas_call(
        _lat_attn_kernel,
        grid=(DEC_BATCH, nb),
        in_specs=in_specs,
        out_specs=pl.BlockSpec((BLOCK, D_MODEL), qrow),
        out_shape=jax.ShapeDtypeStruct((T_LAT, D_MODEL), f32),
        scratch_shapes=[pltpu.VMEM((BLOCK, D_MODEL), f32),
                        pltpu.VMEM((DEC_SEQ + PAST_LEN, H_C * (QK_NOPE + V_C)), bf16)]
        + [pltpu.VMEM(shape, dt) for shape in ((H_A * BLOCK, 3 * BLOCK + PAST_LEN),
                                               (H_B * BLOCK, NA_ROWS * GRID_W + PAST_LEN),
                                               (H_C * BLOCK, DEC_SEQ + PAST_LEN)) for dt in (f32, bf16)],
        compiler_params=_cparams("arbitrary", "arbitrary"),
        name="lat_attn",
    )(sink, qa, qb, qc, ka, va, kb, vb, ckv, kr, *caches, bias_tab, wukv, wout, x, gate)


def _pick_stream(xc_ref, xl_ref, x_scr):
    i = pl.program_id(0)

    @pl.when(i < T_CTX // TM_TOK)
    def _():
        x_scr[...] = xc_ref[...]

    @pl.when(i >= T_CTX // TM_TOK)
    def _():
        x_scr[...] = xl_ref[...]

    return x_scr[...]


def _stream_specs(lat_row0):
    n_ctx = T_CTX // TM_TOK
    return [pl.BlockSpec((TM_TOK, D_MODEL), lambda i: (jnp.minimum(i, n_ctx - 1), 0)),
            pl.BlockSpec((TM_TOK, D_MODEL), lambda i: (lat_row0 // TM_TOK + jnp.maximum(i - n_ctx, 0), 0))]


def _router_kernel(xc_ref, xl_ref, g_ref, sh_ref, sc_ref, wr_ref, br_ref, h_ref, e_ref, gt_ref, x_scr):
    h = _rms(_pick_stream(xc_ref, xl_ref, x_scr), g_ref[...]) * (1.0 + sc_ref[0]) + sh_ref[0]
    _store_row_tiles(h_ref, h)
    logits = jnp.dot(h, wr_ref[...], preferred_element_type=f32, precision=lax.Precision.HIGHEST) + br_ref[...]
    lane = lax.broadcasted_iota(jnp.int32, logits.shape, 1).astype(f32)
    l = jnp.where(lane < N_EXPERTS, logits, -jnp.inf)
    tops, idxs = [], []
    for _ in range(TOP_K):
        m = jnp.max(l, axis=-1, keepdims=True)
        idx = jnp.min(jnp.where(l == m, lane, float(LANE)), axis=-1, keepdims=True)
        tops.append(m)
        idxs.append(idx)
        l = jnp.where(lane == idx, -jnp.inf, l)
    ex = [jnp.exp(t - tops[0]) for t in tops]
    den = ex[0] + ex[1] + ex[2] + ex[3]
    e_out = jnp.zeros(logits.shape, f32)
    g_out = jnp.zeros(logits.shape, f32)
    for k in range(TOP_K):
        e_out = jnp.where(lane == k, idxs[k], e_out)
        g_out = jnp.where(lane == k, ex[k] / den, g_out)
    e_ref[...] = e_out.astype(jnp.int32)
    gt_ref[...] = g_out


def _group_of_tile(i):
    per_b = DEC_SEQ // TM_TOK
    n_ctx = T_CTX // TM_TOK
    return jnp.where(i < n_ctx, 0, 1 + (i - n_ctx) // per_b)


def _router(xc, xl, lat_row0, g, shift, scale, wr, br):
    tm = TM_TOK
    row = lambda i: (i, 0)
    const = lambda i: (0, 0)
    grp = lambda i: (_group_of_tile(i), 0, 0)
    return pl.pallas_call(
        _router_kernel,
        grid=(T_ALL // tm,),
        in_specs=_stream_specs(lat_row0) +
                 [pl.BlockSpec((1, D_MODEL), const),
                  pl.BlockSpec((1, 1, D_MODEL), grp),
                  pl.BlockSpec((1, 1, D_MODEL), grp),
                  pl.BlockSpec((D_MODEL, LANE), const),
                  pl.BlockSpec((1, LANE), const)],
        out_specs=[pl.BlockSpec((tm * ROW_TILE, LANE), row), pl.BlockSpec((tm, LANE), row),
                   pl.BlockSpec((tm, LANE), row)],
        out_shape=[jax.ShapeDtypeStruct((T_ALL * ROW_TILE, LANE), f32),
                   jax.ShapeDtypeStruct((T_ALL, LANE), jnp.int32),
                   jax.ShapeDtypeStruct((T_ALL, LANE), f32)],
        scratch_shapes=[pltpu.VMEM((tm, D_MODEL), f32)],
        compiler_params=_cparams("arbitrary"),
        name="router",
    )(xc, xl, g, shift, scale, wr, br)


def _dispatch_kernel(tok_ref, nu_ref, h_hbm, o_ref, hv, xg, hsem):
    tm = TM_MOE
    i = pl.program_id(0)

    @pl.when(i == 0)
    def _():
        resident = pltpu.make_async_copy(h_hbm, hv, hsem.at[0])
        resident.start()
        resident.wait()

    def one_block(sub, carry):
        blk = i * DISPATCH_BLOCKS + sub
        rows = pl.ds(pl.multiple_of(sub * tm, tm), tm)

        @pl.when(blk < nu_ref[0])
        def _():
            for r in range(tm):
                t = tok_ref[blk * tm + r]
                xg[pl.ds(r * ROW_TILE, ROW_TILE), :] = hv[pl.ds(pl.multiple_of(t * ROW_TILE, ROW_TILE), ROW_TILE), :]
            o_ref[rows, :] = _load_row_tiles(xg).astype(bf16)

        @pl.when(blk >= nu_ref[0])
        def _():
            o_ref[rows, :] = jnp.zeros((tm, D_MODEL), bf16)

        return carry

    lax.fori_loop(0, DISPATCH_BLOCKS, one_block, 0)


def _dispatch(row_tok, n_used, h):
    tm = TM_MOE
    return pl.pallas_call(
        _dispatch_kernel,
        grid_spec=pltpu.PrefetchScalarGridSpec(
            num_scalar_prefetch=2,
            grid=(N_MOE_BLOCKS // DISPATCH_BLOCKS,),
            in_specs=[pl.BlockSpec(memory_space=pl.ANY)],
            out_specs=pl.BlockSpec((DISPATCH_BLOCKS * tm, D_MODEL), lambda i, tok, nu: (i, 0)),
            scratch_shapes=[pltpu.VMEM((T_ALL * ROW_TILE, LANE), f32), pltpu.VMEM((tm * ROW_TILE, LANE), f32),
                            pltpu.SemaphoreType.DMA((1,))]),
        out_shape=jax.ShapeDtypeStruct((N_MOE_BLOCKS * tm, D_MODEL), bf16),
        compiler_params=_cparams("arbitrary"),
        name="dispatch",
    )(row_tok, n_used, h)


def _moe_kernel(layer, be_ref, nu_ref, nxt_ref, dst_ref, x_ref, wgu_hbm, bgu_ref, wd_hbm, bd_ref, y_hbm,
                y0, y1, wgu_st, wd_st, wgu_bf, wd_bf, wsem, ssem):
    tm = TM_MOE
    i = pl.program_id(0)
    nb = pl.num_programs(0)
    used = i < nu_ref[0]
    yb = (y0, y1)

    def out_tile(row):
        return pl.ds(pl.multiple_of(row * ROW_TILE, ROW_TILE), ROW_TILE)

    def scatter_desc(buf, r, dst_row, s):
        return pltpu.make_async_copy(buf.at[out_tile(r)], y_hbm.at[out_tile(dst_row)], ssem.at[s])

    def scatter_wait(s):
        pltpu.make_async_copy(yb[s], y_hbm.at[pl.ds(0, tm * ROW_TILE)], ssem.at[s]).wait()

    def scatter_start(blk, s, unrolled):
        if unrolled:
            for r in range(tm):
                scatter_desc(yb[s], r, dst_ref[(blk + 1) * tm + r], s).start(priority=r % 2)
        else:
            def body(r, carry):
                scatter_desc(yb[s], r, dst_ref[(blk + 1) * tm + r], s).start()
                return carry
            lax.fori_loop(0, tm, body, 0, unroll=8)

    def weight_copies(e):
        return (pltpu.make_async_copy(wgu_hbm.at[layer, e], wgu_st, wsem.at[0]),
                pltpu.make_async_copy(wd_hbm.at[layer, e], wd_st, wsem.at[1]))

    @pl.when(i == 0)
    def _():
        for s in range(2):
            yb[s][...] = jnp.zeros_like(yb[s])
            dummy = pltpu.make_async_copy(yb[s], y_hbm.at[pl.ds((N_ASSIGN + s * tm) * ROW_TILE, tm * ROW_TILE)],
                                          ssem.at[s])
            dummy.start()
            dummy.wait()
        for cp in weight_copies(be_ref[0]):
            cp.start()

    first = jnp.logical_and(used, jnp.logical_or(i == 0, be_ref[i] != be_ref[jnp.maximum(i - 1, 0)]))

    @pl.when(first)
    def _():
        for cp in weight_copies(0):
            cp.wait()
        wgu_bf[...] = wgu_st[...].astype(bf16)
        wd_bf[...] = wd_st[...].astype(bf16)

        @pl.when(nxt_ref[i] >= 0)
        def _():
            for cp in weight_copies(nxt_ref[i]):
                cp.start()

    def step(par):
        cur, oth = par, 1 - par

        @pl.when(i >= 1)
        def _():
            scatter_wait(cur)

        @pl.when(used)
        def _():
            scatter_start(i - 1, oth, unrolled=True)
            gu = _dot(x_ref[...], wgu_bf[...]) + bgu_ref[0, 0]
            x_glu = jnp.minimum(gu[:, :D_FF], SWIGLU_LIMIT)
            x_lin = jnp.clip(gu[:, D_FF:], -SWIGLU_LIMIT, SWIGLU_LIMIT)
            act = x_glu * jax.nn.sigmoid(SWIGLU_ALPHA * x_glu) * (x_lin + 1.0)
            _store_row_tiles(yb[cur], _dot(act.astype(bf16), wd_bf[...]) + bd_ref[0, 0])

        @pl.when(jnp.logical_and(jnp.logical_not(used), i + 1 < nb))
        def _():
            scatter_start(i - 1, oth, unrolled=False)

        @pl.when(i == nb - 1)
        def _():
            scatter_start(i - 1, oth, unrolled=False)
            scatter_wait(oth)

    @pl.when(i % 2 == 0)
    def _():
        step(0)

    @pl.when(i % 2 == 1)
    def _():
        step(1)


def _moe(layer, routing, h, w_gu, b_gu, w_down, b_down):
    tm = TM_MOE
    block_e, n_used, nxt_e, row_tok, row_dst = routing
    xs = _dispatch(row_tok, n_used, h)
    ex4 = lambda i, be, nu, nxt, dst: (layer, be[i], 0, 0)
    return pl.pallas_call(
        functools.partial(_moe_kernel, layer),
        grid_spec=pltpu.PrefetchScalarGridSpec(
            num_scalar_prefetch=4,
            grid=(N_MOE_BLOCKS,),
            in_specs=[pl.BlockSpec((tm, D_MODEL), lambda i, be, nu, nxt, dst: (i, 0)),
                      pl.BlockSpec(memory_space=pl.ANY),
                      pl.BlockSpec((1, 1, 1, 2 * D_FF), ex4),
                      pl.BlockSpec(memory_space=pl.ANY),
                      pl.BlockSpec((1, 1, 1, D_MODEL), ex4)],
            out_specs=pl.BlockSpec(memory_space=pl.ANY),
            scratch_shapes=[pltpu.VMEM((tm * ROW_TILE, LANE), f32), pltpu.VMEM((tm * ROW_TILE, LANE), f32),
                            pltpu.VMEM((D_MODEL, 2 * D_FF), f32), pltpu.VMEM((D_FF, D_MODEL), f32),
                            pltpu.VMEM((D_MODEL, 2 * D_FF), bf16), pltpu.VMEM((D_FF, D_MODEL), bf16),
                            pltpu.SemaphoreType.DMA((2,)), pltpu.SemaphoreType.DMA((2,))]),
        out_shape=jax.ShapeDtypeStruct(((N_ASSIGN + 2 * tm) * ROW_TILE, LANE), f32),
        compiler_params=_cparams("arbitrary"),
        name="moe",
    )(block_e, n_used, nxt_e, row_dst, xs, w_gu, b_gu.reshape(DEPTH, N_EXPERTS, 1, 2 * D_FF),
      w_down, b_down.reshape(DEPTH, N_EXPERTS, 1, D_MODEL))


def _combine_kernel(final, xc_ref, xl_ref, y0_ref, y1_ref, y2_ref, y3_ref, gt_ref, gate_ref, gf_ref, *rest):
    x_scr = rest[-1]
    gt = gt_ref[...]
    f = gt[:, 0:1] * _load_row_tiles(y0_ref)
    for k, y_ref in ((1, y1_ref), (2, y2_ref), (3, y3_ref)):
        f = f + gt[:, k:k + 1] * _load_row_tiles(y_ref)
    out = _pick_stream(xc_ref, xl_ref, x_scr) + gate_ref[0] * f
    if not final:
        rest[0][...] = out
        return
    out = _rms(out, gf_ref[...])
    oc_ref, ol_ref = rest[0], rest[1]
    i = pl.program_id(0)

    @pl.when(i < T_CTX // TM_TOK)
    def _():
        oc_ref[...] = out

    @pl.when(i >= T_CTX // TM_TOK)
    def _():
        ol_ref[...] = out


def _combine(final, xc, xl, lat_row0, y, gates, gate, g_final):
    tm = TM_TOK
    nt = T_ALL // tm
    n_ctx = T_CTX // tm
    row = lambda i: (i, 0)
    const = lambda i: (0, 0)
    grp = lambda i: (_group_of_tile(i), 0, 0)
    ysel = [pl.BlockSpec((tm * ROW_TILE, LANE), functools.partial(lambda k, i: (k * nt + i, 0), k))
            for k in range(TOP_K)]
    if final:
        out_specs = [pl.BlockSpec((tm, D_MODEL), lambda i: (jnp.minimum(i, n_ctx - 1), 0)),
                     pl.BlockSpec((tm, D_MODEL), lambda i: (jnp.maximum(i - n_ctx, 0), 0))]
        out_shape = [jax.ShapeDtypeStruct((T_CTX, D_MODEL), f32), jax.ShapeDtypeStruct((T_LAT, D_MODEL), f32)]
    else:
        out_specs = pl.BlockSpec((tm, D_MODEL), row)
        out_shape = jax.ShapeDtypeStruct((T_ALL, D_MODEL), f32)
    return pl.pallas_call(
        functools.partial(_combine_kernel, final),
        grid=(nt,),
        in_specs=_stream_specs(lat_row0) + ysel +
                 [pl.BlockSpec((tm, LANE), row),
                  pl.BlockSpec((1, 1, D_MODEL), grp),
                  pl.BlockSpec((1, D_MODEL), const)],
        out_specs=out_specs,
        out_shape=out_shape,
        scratch_shapes=[pltpu.VMEM((tm, D_MODEL), f32)],
        compiler_params=_cparams("arbitrary"),
        name="combine",
    )(xc, xl, y, y, y, y, gates, gate, g_final)


def _rope_head_tables(d):
    nf = d // 4
    half = d // 2
    t = np.arange(DEC_SEQ)
    inv = ROPE_BASE ** (-np.arange(nf, dtype=np.float32) / nf)
    i = np.arange(d)
    pos = np.where(i[None, :] < half, (t // GRID_W)[:, None], (t % GRID_W)[:, None]).astype(np.float32)
    ang = pos * inv[i % nf][None, :].astype(np.float32)
    first = (i % half) < nf
    cos = np.cos(ang)
    sin = np.where(first[None, :], -np.sin(ang), np.sin(ang))
    partner = np.where(first, i + nf, i - nf)
    return cos.astype(np.float32), sin.astype(np.float32), partner


def _rope_tables():
    cos64, sin64, _ = _rope_head_tables(HEAD_DIM)
    cos32, sin32, _ = _rope_head_tables(QK_ROPE)
    cosa = np.tile(cos64, (1, H_A))
    sina = np.tile(sin64, (1, H_A))
    cosq1 = np.concatenate([np.ones((DEC_SEQ, QK_NOPE), np.float32), cos32,
                            np.ones((DEC_SEQ, QC_PAD - QK_NOPE - QK_ROPE), np.float32)], axis=1)
    sinq1 = np.concatenate([np.zeros((DEC_SEQ, QK_NOPE), np.float32), sin32,
                            np.zeros((DEC_SEQ, QC_PAD - QK_NOPE - QK_ROPE), np.float32)], axis=1)
    cosq = np.tile(cosq1, (1, H_C))
    sinq = np.tile(sinq1, (1, H_C))
    return tuple(jnp.asarray(a) for a in (cosa, sina, cosq, sinq, cos32, sin32))


def _pad_cols(w, n):
    return jnp.pad(w, ((0, 0), (0, n - w.shape[1])))


def _layer_weights(w_in, w_uq):
    cuts = np.cumsum((W_QA, W_KA, W_VA, W_B, W_B, W_B, Q_LORA, KV_LORA, QK_ROPE))[:-1]
    qa, ka, va, qb, kb, vb, cq, ckv, kr = jnp.split(w_in, [int(c) for c in cuts], axis=1)
    _, _, p64 = _rope_head_tables(HEAD_DIM)
    _, _, p32 = _rope_head_tables(QK_ROPE)
    pa = np.concatenate([h * HEAD_DIM + p64 for h in range(H_A)])
    base = jnp.concatenate([qa, ka, va, _pad_cols(qb, 384), _pad_cols(kb, 384), _pad_cols(vb, 384), cq, ckv,
                            _pad_cols(kr, 128)], axis=1)
    w_ctx = base.astype(bf16)
    w_lat = jnp.concatenate([base, qa[:, pa], ka[:, pa[:W_KA]], _pad_cols(kr[:, p32], 128)], axis=1).astype(bf16)
    hq = QK_NOPE + QK_ROPE
    heads = [_pad_cols(w_uq[:, h * hq:(h + 1) * hq], QC_PAD) for h in range(H_C)]
    pq = np.concatenate([np.arange(QK_NOPE), QK_NOPE + p32])
    heads_p = [_pad_cols(w_uq[:, h * hq:(h + 1) * hq][:, pq], QC_PAD) for h in range(H_C)]
    wuq = jnp.concatenate(heads, axis=1).astype(bf16)
    wuq2 = jnp.concatenate(heads + heads_p, axis=1).astype(bf16)
    return w_ctx, w_lat, wuq, wuq2


def _bias_table(rpb):
    col = np.arange(GRID_W)
    col_start = np.clip(col - NA_COLS // 2, 0, GRID_W - NA_COLS)
    col_ok = (col[None, :] >= col_start[:, None]) & (col[None, :] < col_start[:, None] + NA_COLS)
    dc = np.clip(col[None, :] - col[:, None] + (NA_COLS - 1), 0, 2 * NA_COLS - 2)
    onehot = (dc[None] == np.arange(2 * NA_COLS - 1)[:, None, None]).astype(np.float32)
    expanded = jnp.einsum('hrd,dqk->hrqk', rpb.astype(f32), jnp.asarray(onehot), precision=lax.Precision.HIGHEST)
    blocks = jnp.where(col_ok[None, None], expanded, NEG)
    return jnp.concatenate([blocks[:, :-1], blocks[:, 1:]], axis=-1)


def _routing(top_e):
    tm = TM_MOE
    key_bits = 16
    pad_mark = (1 << key_bits) - 1
    flat_e = top_e.T.reshape(N_ASSIGN)
    experts = jnp.arange(N_EXPERTS, dtype=jnp.int32)
    counts = jnp.sum((flat_e[:, None] == experts[None, :]).astype(jnp.int32), axis=0)
    nblk = (counts + tm - 1) // tm
    blk_end = jnp.cumsum(nblk)
    pad_end = jnp.cumsum(nblk * tm - counts)
    slots = jnp.arange(N_MOE_BLOCKS * tm - N_ASSIGN, dtype=jnp.int32)
    pad_e = jnp.sum((pad_end[None, :] <= slots[:, None]).astype(jnp.int32), axis=1)
    keys = jnp.concatenate([(flat_e << key_bits) + jnp.arange(N_ASSIGN, dtype=jnp.int32),
                            (pad_e << key_bits) + pad_mark])
    asg = (jnp.sort(keys) & pad_mark).reshape(N_MOE_BLOCKS, tm)
    valid = asg != pad_mark
    blocks = jnp.arange(N_MOE_BLOCKS, dtype=jnp.int32)
    r = jnp.arange(tm, dtype=jnp.int32)[None, :]
    tok = jnp.where(valid, asg % T_ALL, 0)
    row_dst = jnp.where(valid, asg, N_ASSIGN + (blocks[:, None] % 2) * tm + r)
    row_dst = jnp.concatenate([N_ASSIGN + tm + r, row_dst], axis=0).reshape(-1)
    block_e = jnp.minimum(jnp.sum((blk_end[None, :] <= blocks[:, None]).astype(jnp.int32), axis=1), N_EXPERTS - 1)
    n_used = blk_end[-1].astype(jnp.int32).reshape(1)
    has = jnp.where(counts > 0, experts, N_EXPERTS)
    later = experts[None, :] > experts[:, None]
    nxt = jnp.min(jnp.where(later, has[None, :], N_EXPERTS), axis=1)
    nxt = jnp.where(nxt >= N_EXPERTS, -1, nxt)
    sel = (block_e[:, None] == experts[None, :]).astype(jnp.int32)
    nxt_e = jnp.sum(sel * nxt[None, :], axis=1)
    i32 = lambda a: a.astype(jnp.int32)
    return i32(block_e), n_used, i32(nxt_e), i32(tok).reshape(-1), i32(row_dst)


def kernel(x_prompt, x_sample, cache_a_k, cache_a_v, cache_b_k, cache_b_v, cache_c_kv, cache_c_kr, c, c_ctx, w_ada, b_ada, g_attn, g_ffn, w_in, sink_a, rpb_b, g_cq, g_ckv, w_uq, w_ukv, w_out, w_router, b_router, w_gu, b_gu, w_down, b_down, g_final):
    xc, xl, lat_row0 = x_prompt.reshape(T_CTX, D_MODEL), x_sample.reshape(T_LAT, D_MODEL), 0
    cvec = jnp.concatenate([c_ctx[None, :], c, jnp.zeros((8 - N_GROUPS, D_MODEL), f32)], axis=0)
    mods = _ada(cvec, w_ada, b_ada)[:, :N_GROUPS].reshape(DEPTH, N_GROUPS, 6, 1, D_MODEL)
    tabs = _rope_tables()
    caches = (cache_a_k.reshape(DEC_BATCH, DEPTH, PAST_LEN, W_KA), cache_a_v.reshape(DEC_BATCH, DEPTH, PAST_LEN, W_VA),
              cache_b_k.reshape(DEC_BATCH, DEPTH, PAST_LEN, W_B), cache_b_v.reshape(DEC_BATCH, DEPTH, PAST_LEN, W_B),
              cache_c_kv, cache_c_kr)
    new = [[] for _ in range(6)]
    for layer in range(DEPTH):
        m = [mods[layer, :, j] for j in range(6)]
        w_ctx, w_lat, wuq, wuq2 = _layer_weights(w_in[layer], w_uq[layer])
        wukv = w_ukv[layer].astype(bf16)
        wout = w_out[layer].astype(bf16)
        g1 = g_attn[layer][None, :]
        gcq = g_cq[layer][None, :]
        gckv = g_ckv[layer][None, :]
        sink = sink_a[layer]

        pc = _inproj_ctx(xc, g1, m[0], m[1], w_ctx, gcq, gckv, wuq)
        for lst, a in zip(new, (pc[1], pc[2], pc[4], pc[5], pc[7], pc[8])):
            lst.append(a)
        x_ctx = _ctx_attn(sink, pc, wukv, wout, xc, m[2])

        plat = _inproj_lat(xl, lat_row0, g1, m[0], m[1], w_lat, gcq, gckv, wuq2, tabs)
        x_lat = _lat_attn(layer, sink, plat, caches, _bias_table(rpb_b[layer]), wukv, wout, xl, lat_row0, m[2])

        wr = _pad_cols(w_router[layer], LANE)
        br = _pad_cols(b_router[layer][None, :], LANE)
        h2, top_e, gates = _router(x_ctx, x_lat, 0, g_ffn[layer][None, :], m[3], m[4], wr, br)
        y = _moe(layer, _routing(top_e[:, :TOP_K]), h2, w_gu, b_gu, w_down, b_down)
        x = _combine(layer == DEPTH - 1, x_ctx, x_lat, 0, y, gates, m[5], g_final[None, :])
        xc, xl, lat_row0 = x, x, T_CTX

    y_prompt = x[0].reshape(BATCH, SEQ, D_MODEL)
    y_sample = x[1].reshape(DEC_BATCH, DEC_SEQ, D_MODEL)
    shapes = ((KV_A, HEAD_DIM), (KV_A, HEAD_DIM), (H_B, HEAD_DIM), (H_B, HEAD_DIM), (KV_LORA,), (QK_ROPE,))
    outs = [jnp.stack([a.reshape((BATCH, SEQ) + s) for a in lst], axis=1) for lst, s in zip(new, shapes)]
    return (y_prompt, y_sample, *outs)
```

```python
import functools

import numpy as np
import jax
import jax.numpy as jnp
from jax import lax
from jax.experimental import pallas as pl
from jax.experimental.pallas import tpu as pltpu

D_MODEL = 1024
BATCH = 32
SEQ = 256
DEPTH = 2
DEC_BATCH = 2
DEC_SEQ = 1024
PAST_LEN = 512
GRID_W = 64
HEAD_DIM = 64
H_A = 6
KV_A = 2
G_A = H_A // KV_A
WINDOW = 128
BLOCK = 128
H_B = 5
NA_ROWS = 8
NA_COLS = 16
H_C = 5
Q_LORA = 384
KV_LORA = 256
QK_NOPE = 64
QK_ROPE = 32
V_C = 64
N_EXPERTS = 32
TOP_K = 4
D_FF = 1024
SWIGLU_ALPHA = 1.702
SWIGLU_LIMIT = 7.0
ROPE_BASE = 10000.0
EPS = 1e-6
NEG = -1e30

T_CTX = BATCH * SEQ
T_LAT = DEC_BATCH * DEC_SEQ
T_ALL = T_CTX + T_LAT
N_GROUPS = 1 + DEC_BATCH
LANE = 128
QC_PAD = 128
ROWS = DEC_SEQ // GRID_W

W_QA, W_KA, W_VA = H_A * HEAD_DIM, KV_A * HEAD_DIM, KV_A * HEAD_DIM
W_B = H_B * HEAD_DIM
OFF_QA = 0
OFF_KA = 384
OFF_VA = 512
OFF_QB = 640
OFF_KB = 1024
OFF_VB = 1408
OFF_CQ = 1792
OFF_CKV = 2176
OFF_KR = 2432
NW_CTX = 2560
OFF_QA_P = 2560
OFF_KA_P = 2944
OFF_KR_P = 3072
NW_LAT = 3200

TM_TOK = 512
TM_LAT_IN = 512
TM_MOE = 256
N_ASSIGN = T_ALL * TOP_K
N_MOE_BLOCKS = N_ASSIGN // TM_MOE + N_EXPERTS
VMEM_LIMIT = 56 * 1024 * 1024

f32 = jnp.float32
bf16 = jnp.bfloat16


def _cparams(*sem):
    return pltpu.CompilerParams(dimension_semantics=sem, vmem_limit_bytes=VMEM_LIMIT)


def _rms(xf, g):
    return xf * lax.rsqrt(jnp.mean(xf * xf, axis=-1, keepdims=True) + EPS) * g


def _dot(a, b):
    return jnp.dot(a, b, preferred_element_type=f32)


def _dot_nt(a, b):
    return lax.dot_general(a, b, (((1,), (1,)), ((), ())), preferred_element_type=f32)


ROW_TILE = D_MODEL // LANE


def _store_row_tiles(ref, val):
    n = val.shape[0]
    for c in range(ROW_TILE):
        ref[pl.ds(c, n, stride=ROW_TILE), :] = val[:, c * LANE:(c + 1) * LANE]


def _load_row_tiles(ref):
    n = ref.shape[0] // ROW_TILE
    return jnp.concatenate([ref[pl.ds(c, n, stride=ROW_TILE), :] for c in range(ROW_TILE)], axis=1)


def _softmax_rows(s_ref, p_ref, rows, sinks=None):
    s = s_ref[rows, :]
    m = jnp.max(s, axis=-1, keepdims=True)
    if sinks is not None:
        sink = jnp.concatenate([jnp.full((n, 1), v, f32) for v, n in sinks], axis=0)
        m = jnp.maximum(m, sink)
    p = jnp.exp(s - m)
    l = jnp.sum(p, axis=-1, keepdims=True)
    if sinks is not None:
        l = l + jnp.exp(sink - m)
    p_ref[rows, :] = (p * (1.0 / l)).astype(bf16)


def _ada_kernel(c_ref, w_ref, b_ref, o_ref):
    c = c_ref[...]
    s = c * jax.nn.sigmoid(c)
    o_ref[0] = jnp.dot(s, w_ref[0], preferred_element_type=f32, precision=lax.Precision.HIGHEST) + b_ref[0]


def _ada(cvec, w_ada, b_ada):
    tn = 1536
    return pl.pallas_call(
        _ada_kernel,
        grid=(DEPTH, 6 * D_MODEL // tn),
        in_specs=[pl.BlockSpec((8, D_MODEL), lambda l, j: (0, 0)),
                  pl.BlockSpec((1, D_MODEL, tn), lambda l, j: (l, 0, j)),
                  pl.BlockSpec((1, 1, tn), lambda l, j: (l, 0, j))],
        out_specs=pl.BlockSpec((1, 8, tn), lambda l, j: (l, 0, j)),
        out_shape=jax.ShapeDtypeStruct((DEPTH, 8, 6 * D_MODEL), f32),
        compiler_params=_cparams("arbitrary", "arbitrary"),
        name="ada",
    )(cvec, w_ada, b_ada.reshape(DEPTH, 1, 6 * D_MODEL))


def _inproj_ctx_kernel(x_ref, g_ref, sh_ref, sc_ref, w_ref, gcq_ref, gckv_ref, wuq_ref,
                       qa_ref, ka_ref, va_ref, qb_ref, kb_ref, vb_ref, qc_ref, ckv_ref, kr_ref):
    h = _rms(x_ref[...], g_ref[...]) * (1.0 + sc_ref[0]) + sh_ref[0]
    p = _dot(h.astype(bf16), w_ref[...])
    qa_ref[...] = p[:, OFF_QA:OFF_QA + W_QA].astype(bf16)
    ka_ref[...] = p[:, OFF_KA:OFF_KA + W_KA]
    va_ref[...] = p[:, OFF_VA:OFF_VA + W_VA]
    qb_ref[...] = p[:, OFF_QB:OFF_QB + W_B].astype(bf16)
    kb_ref[...] = p[:, OFF_KB:OFF_KB + W_B]
    vb_ref[...] = p[:, OFF_VB:OFF_VB + W_B]
    cqn = _rms(p[:, OFF_CQ:OFF_CQ + Q_LORA], gcq_ref[...])
    qc_ref[...] = _dot(cqn.astype(bf16), wuq_ref[...]).astype(bf16)
    ckv_ref[...] = _rms(p[:, OFF_CKV:OFF_CKV + KV_LORA], gckv_ref[...])
    kr_ref[...] = p[:, OFF_KR:OFF_KR + QK_ROPE]


def _inproj_ctx(x, g, shift, scale, w, gcq, gckv, wuq):
    tm = TM_TOK
    row = lambda i: (i, 0)
    const = lambda i: (0, 0)
    widths = (W_QA, W_KA, W_VA, W_B, W_B, W_B, H_C * QC_PAD, KV_LORA, QK_ROPE)
    dtypes = (bf16, f32, f32, bf16, f32, f32, bf16, f32, f32)
    return pl.pallas_call(
        _inproj_ctx_kernel,
        grid=(T_CTX // tm,),
        in_specs=[pl.BlockSpec((tm, D_MODEL), row),
                  pl.BlockSpec((1, D_MODEL), const),
                  pl.BlockSpec((1, 1, D_MODEL), lambda i: (0, 0, 0)),
                  pl.BlockSpec((1, 1, D_MODEL), lambda i: (0, 0, 0)),
                  pl.BlockSpec((D_MODEL, NW_CTX), const),
                  pl.BlockSpec((1, Q_LORA), const),
                  pl.BlockSpec((1, KV_LORA), const),
                  pl.BlockSpec((Q_LORA, H_C * QC_PAD), const)],
        out_specs=[pl.BlockSpec((tm, wd), row) for wd in widths],
        out_shape=[jax.ShapeDtypeStruct((T_CTX, wd), dt) for wd, dt in zip(widths, dtypes)],
        compiler_params=_cparams("arbitrary"),
        name="inproj_ctx",
    )(x, g, shift, scale, w, gcq, gckv, wuq)


def _inproj_lat_kernel(x_ref, g_ref, sh_ref, sc_ref, w_ref, gcq_ref, gckv_ref, wuq_ref,
                       cosa_ref, sina_ref, cosq_ref, sinq_ref, cosr_ref, sinr_ref,
                       qa_ref, ka_ref, va_ref, qb_ref, kb_ref, vb_ref, qc_ref, ckv_ref, kr_ref):
    h = _rms(x_ref[...], g_ref[...]) * (1.0 + sc_ref[0]) + sh_ref[0]
    p = _dot(h.astype(bf16), w_ref[...])
    cosa = cosa_ref[...]
    sina = sina_ref[...]
    qa = p[:, OFF_QA:OFF_QA + W_QA] * cosa + p[:, OFF_QA_P:OFF_QA_P + W_QA] * sina
    ka = p[:, OFF_KA:OFF_KA + W_KA] * cosa[:, :W_KA] + p[:, OFF_KA_P:OFF_KA_P + W_KA] * sina[:, :W_KA]
    kr = p[:, OFF_KR:OFF_KR + QK_ROPE] * cosr_ref[...] + p[:, OFF_KR_P:OFF_KR_P + QK_ROPE] * sinr_ref[...]
    qa_ref[...] = qa.astype(bf16)
    ka_ref[...] = ka.astype(bf16)
    va_ref[...] = p[:, OFF_VA:OFF_VA + W_VA].astype(bf16)
    qb_ref[...] = p[:, OFF_QB:OFF_QB + W_B].astype(bf16)
    kb_ref[...] = p[:, OFF_KB:OFF_KB + W_B].astype(bf16)
    vb_ref[...] = p[:, OFF_VB:OFF_VB + W_B].astype(bf16)
    cqn = _rms(p[:, OFF_CQ:OFF_CQ + Q_LORA], gcq_ref[...])
    q2 = _dot(cqn.astype(bf16), wuq_ref[...])
    nq = H_C * QC_PAD
    qc_ref[...] = (q2[:, :nq] * cosq_ref[...] + q2[:, nq:] * sinq_ref[...]).astype(bf16)
    ckv_ref[...] = _rms(p[:, OFF_CKV:OFF_CKV + KV_LORA], gckv_ref[...]).astype(bf16)
    kr_ref[...] = kr.astype(bf16)


def _inproj_lat(x, lat_row0, g, shift, scale, w, gcq, gckv, wuq2, tabs):
    tm = TM_LAT_IN
    per_b = DEC_SEQ // tm
    row0 = lat_row0 // tm
    xrow = lambda i: (row0 + i, 0)
    row = lambda i: (i, 0)
    const = lambda i: (0, 0)
    grp = lambda i: (1 + i // per_b, 0, 0)
    pos = lambda i: (i % per_b, 0)
    cosa, sina, cosq, sinq, cosr, sinr = tabs
    widths = (W_QA, W_KA, W_VA, W_B, W_B, W_B, H_C * QC_PAD, KV_LORA, QK_ROPE)
    return pl.pallas_call(
        _inproj_lat_kernel,
        grid=(T_LAT // tm,),
        in_specs=[pl.BlockSpec((tm, D_MODEL), xrow),
                  pl.BlockSpec((1, D_MODEL), const),
                  pl.BlockSpec((1, 1, D_MODEL), grp),
                  pl.BlockSpec((1, 1, D_MODEL), grp),
                  pl.BlockSpec((D_MODEL, NW_LAT), const),
                  pl.BlockSpec((1, Q_LORA), const),
                  pl.BlockSpec((1, KV_LORA), const),
                  pl.BlockSpec((Q_LORA, 2 * H_C * QC_PAD), const),
                  pl.BlockSpec((tm, W_QA), pos), pl.BlockSpec((tm, W_QA), pos),
                  pl.BlockSpec((tm, H_C * QC_PAD), pos), pl.BlockSpec((tm, H_C * QC_PAD), pos),
                  pl.BlockSpec((tm, QK_ROPE), pos), pl.BlockSpec((tm, QK_ROPE), pos)],
        out_specs=[pl.BlockSpec((tm, wd), row) for wd in widths],
        out_shape=[jax.ShapeDtypeStruct((T_LAT, wd), bf16) for wd in widths],
        compiler_params=_cparams("arbitrary"),
        name="inproj_lat",
    )(x, g, shift, scale, w, gcq, gckv, wuq2, cosa, sina, cosq, sinq, cosr, sinr)


def _ctx_attn_kernel(sink_ref, qa_ref, ka_ref, va_ref, qb_ref, kb_ref, vb_ref, qc_ref, ckv_ref, kr_ref,
                     wukv_ref, wout_ref, x_ref, gate_ref, o_ref, o_scr, s_scr, p_scr):
    n = SEQ
    scale = HEAD_DIM ** -0.5
    scale_c = (QK_NOPE + QK_ROPE) ** -0.5
    ka = ka_ref[...].astype(bf16)
    va = va_ref[...].astype(bf16)
    kb = kb_ref[...].astype(bf16)
    vb = vb_ref[...].astype(bf16)
    kv = _dot(ckv_ref[...].astype(bf16), wukv_ref[...]).astype(bf16)
    kr = kr_ref[...].astype(bf16)
    for h in range(H_A):
        g = h // G_A
        q = qa_ref[:, h * HEAD_DIM:(h + 1) * HEAD_DIM]
        s_scr[h * n:(h + 1) * n, :] = _dot_nt(q, ka[:, g * HEAD_DIM:(g + 1) * HEAD_DIM]) * scale
    for h in range(H_B):
        sl = slice(h * HEAD_DIM, (h + 1) * HEAD_DIM)
        s_scr[(H_A + h) * n:(H_A + h + 1) * n, :] = _dot_nt(qb_ref[:, sl], kb[:, sl]) * scale
    for h in range(H_C):
        qn = qc_ref[:, h * QC_PAD:h * QC_PAD + QK_NOPE]
        qr = qc_ref[:, h * QC_PAD + QK_NOPE:h * QC_PAD + QK_NOPE + QK_ROPE]
        c0 = h * (QK_NOPE + V_C)
        r0 = (H_A + H_B + h) * n
        s_scr[r0:r0 + n, :] = (_dot_nt(qn, kv[:, c0:c0 + QK_NOPE]) + _dot_nt(qr, kr)) * scale_c
    for pair in range((H_A + H_B + H_C) // 2):
        h0 = 2 * pair
        sinks = ((sink_ref[h0], n), (sink_ref[h0 + 1], n)) if h0 < H_A else None
        _softmax_rows(s_scr, p_scr, slice(h0 * n, (h0 + 2) * n), sinks)
    for h in range(H_A):
        g = h // G_A
        o_scr[:, h * HEAD_DIM:(h + 1) * HEAD_DIM] = _dot(p_scr[h * n:(h + 1) * n, :],
                                                         va[:, g * HEAD_DIM:(g + 1) * HEAD_DIM])
    for h in range(H_B):
        sl = slice(h * HEAD_DIM, (h + 1) * HEAD_DIM)
        o_scr[:, W_QA + h * HEAD_DIM:W_QA + (h + 1) * HEAD_DIM] = _dot(p_scr[(H_A + h) * n:(H_A + h + 1) * n, :],
                                                                     vb[:, sl])
    for h in range(H_C):
        c0 = h * (QK_NOPE + V_C)
        r0 = (H_A + H_B + h) * n
        off = W_QA + W_B + h * V_C
        o_scr[:, off:off + V_C] = _dot(p_scr[r0:r0 + n, :], kv[:, c0 + QK_NOPE:c0 + QK_NOPE + V_C])
    y = _dot(o_scr[...].astype(bf16), wout_ref[...])
    o_ref[...] = x_ref[...] + gate_ref[0] * y


def _ctx_attn(sink, proj, wukv, wout, x, gate):
    qa, ka, va, qb, kb, vb, qc, ckv, kr = proj
    row = lambda b: (b, 0)
    const = lambda b: (0, 0)
    in_specs = [pl.BlockSpec(memory_space=pltpu.SMEM)]
    in_specs += [pl.BlockSpec((SEQ, a.shape[1]), row) for a in proj]
    in_specs += [pl.BlockSpec((KV_LORA, H_C * (QK_NOPE + V_C)), const),
                 pl.BlockSpec((D_MODEL, D_MODEL), const),
                 pl.BlockSpec((SEQ, D_MODEL), row),
                 pl.BlockSpec((1, 1, D_MODEL), lambda b: (0, 0, 0))]
    return pl.pallas_call(
        _ctx_attn_kernel,
        grid=(BATCH,),
        in_specs=in_specs,
        out_specs=pl.BlockSpec((SEQ, D_MODEL), row),
        out_shape=jax.ShapeDtypeStruct((T_CTX, D_MODEL), f32),
        scratch_shapes=[pltpu.VMEM((SEQ, D_MODEL), f32),
                        pltpu.VMEM(((H_A + H_B + H_C) * SEQ, SEQ), f32),
                        pltpu.VMEM(((H_A + H_B + H_C) * SEQ, SEQ), bf16)],
        compiler_params=_cparams("arbitrary"),
        name="ctx_attn",
    )(sink, qa, ka, va, qb, kb, vb, qc, ckv, kr, wukv, wout, x, gate)


def _lat_attn_kernel(sink_ref, qa_ref, qb_ref, qc_ref, ka_ref, va_ref, kb_ref, vb_ref, ckv_ref, kr_ref,
                     cak_ref, cav_ref, cbk_ref, cbv_ref, cckv_ref, ckr_ref, bias_ref,
                     wukv_ref, wout_ref, x_ref, gate_ref, o_ref, o_scr, kv_scr, sa, pa, sb, pb, sc, pc):
    qi = pl.program_id(1)
    nb = DEC_SEQ // BLOCK
    scale = HEAD_DIM ** -0.5

    @pl.when(qi == 0)
    def _():
        kv_scr[0:DEC_SEQ, :] = _dot(ckv_ref[...], wukv_ref[...]).astype(bf16)
        kv_scr[DEC_SEQ:DEC_SEQ + PAST_LEN, :] = _dot(cckv_ref[0, 0].astype(bf16), wukv_ref[...]).astype(bf16)

    def blk(ref, j):
        idx = jnp.clip(qi + j, 0, nb - 1)
        return ref[pl.ds(pl.multiple_of(idx * BLOCK, BLOCK), BLOCK), :]

    ka = jnp.concatenate([blk(ka_ref, -1), blk(ka_ref, 0), blk(ka_ref, 1), cak_ref[0, 0].astype(bf16)], axis=0)
    va = jnp.concatenate([blk(va_ref, -1), blk(va_ref, 0), blk(va_ref, 1), cav_ref[0, 0].astype(bf16)], axis=0)
    nk_a = 3 * BLOCK + PAST_LEN
    r = lax.broadcasted_iota(jnp.int32, (BLOCK, nk_a), 0)
    c = lax.broadcasted_iota(jnp.int32, (BLOCK, nk_a), 1)
    valid = (((c < BLOCK) & (c >= r) & (qi > 0))
             | ((c >= BLOCK) & (c < 2 * BLOCK))
             | ((c >= 2 * BLOCK) & (c < 3 * BLOCK) & (c - 2 * BLOCK <= r) & (qi < nb - 1))
             | (c >= 3 * BLOCK))
    for h in range(H_A):
        g = h // G_A
        q = qa_ref[:, h * HEAD_DIM:(h + 1) * HEAD_DIM]
        s = _dot_nt(q, ka[:, g * HEAD_DIM:(g + 1) * HEAD_DIM]) * scale
        sa[h * BLOCK:(h + 1) * BLOCK, :] = jnp.where(valid, s, NEG)

    cbk = cbk_ref[0, 0].astype(bf16)
    cbv = cbv_ref[0, 0].astype(bf16)
    rows_per_blk = BLOCK // GRID_W
    nloc = NA_ROWS * GRID_W
    vcats = []
    for half in range(rows_per_blk):
        grow = qi * rows_per_blk + half
        start = jnp.clip(grow - NA_ROWS // 2, 0, ROWS - NA_ROWS)
        kloc = kb_ref[pl.ds(pl.multiple_of(start * GRID_W, GRID_W), nloc), :]
        vloc = vb_ref[pl.ds(pl.multiple_of(start * GRID_W, GRID_W), nloc), :]
        vcats.append(jnp.concatenate([vloc, cbv], axis=0))
        qrows = slice(half * GRID_W, (half + 1) * GRID_W)
        dr0 = start - grow + (NA_ROWS - 1)
        for h in range(H_B):
            sl = slice(h * HEAD_DIM, (h + 1) * HEAD_DIM)
            q = qb_ref[qrows, sl]
            bias = jnp.concatenate([bias_ref[h, dr0 + 2 * j] for j in range(NA_ROWS // 2)], axis=1)
            s_loc = _dot_nt(q, kloc[:, sl]) * scale + bias
            s_ctx = _dot_nt(q, cbk[:, sl]) * scale
            r0 = (half * H_B + h) * GRID_W
            sb[r0:r0 + GRID_W, :] = jnp.concatenate([s_loc, s_ctx], axis=1)

    kr = jnp.concatenate([kr_ref[...], ckr_ref[0, 0].astype(bf16)], axis=0)
    scale_c = (QK_NOPE + QK_ROPE) ** -0.5
    for h in range(H_C):
        qn = qc_ref[:, h * QC_PAD:h * QC_PAD + QK_NOPE]
        qr = qc_ref[:, h * QC_PAD + QK_NOPE:h * QC_PAD + QK_NOPE + QK_ROPE]
        c0 = h * (QK_NOPE + V_C)
        sc[h * BLOCK:(h + 1) * BLOCK, :] = (_dot_nt(qn, kv_scr[:, c0:c0 + QK_NOPE]) + _dot_nt(qr, kr)) * scale_c

    for pair in range(H_A // 2):
        h0 = 2 * pair
        _softmax_rows(sa, pa, slice(h0 * BLOCK, (h0 + 2) * BLOCK), ((sink_ref[h0], BLOCK), (sink_ref[h0 + 1], BLOCK)))
    for blk2 in range(rows_per_blk * H_B // 2):
        _softmax_rows(sb, pb, slice(blk2 * 2 * GRID_W, (blk2 + 1) * 2 * GRID_W))
    for h in range(H_C):
        _softmax_rows(sc, pc, slice(h * BLOCK, (h + 1) * BLOCK))

    for h in range(H_A):
        g = h // G_A
        o_scr[:, h * HEAD_DIM:(h + 1) * HEAD_DIM] = _dot(pa[h * BLOCK:(h + 1) * BLOCK, :],
                                                         va[:, g * HEAD_DIM:(g + 1) * HEAD_DIM])
    for half in range(rows_per_blk):
        qrows = slice(half * GRID_W, (half + 1) * GRID_W)
        for h in range(H_B):
            sl = slice(h * HEAD_DIM, (h + 1) * HEAD_DIM)
            r0 = (half * H_B + h) * GRID_W
            o_scr[qrows, W_QA + h * HEAD_DIM:W_QA + (h + 1) * HEAD_DIM] = _dot(pb[r0:r0 + GRID_W, :],
                                                                             vcats[half][:, sl])
    for h in range(H_C):
        c0 = h * (QK_NOPE + V_C)
        off = W_QA + W_B + h * V_C
        o_scr[:, off:off + V_C] = _dot(pc[h * BLOCK:(h + 1) * BLOCK, :], kv_scr[:, c0 + QK_NOPE:c0 + QK_NOPE + V_C])

    y = _dot(o_scr[...].astype(bf16), wout_ref[...])
    o_ref[...] = x_ref[...] + gate_ref[0] * y


def _lat_attn(layer, sink, proj, caches, bias_tab, wukv, wout, x, lat_row0, gate):
    qa, ka, va, qb, kb, vb, qc, ckv, kr = proj
    nb = DEC_SEQ // BLOCK
    qrow = lambda b, q: (b * nb + q, 0)
    xrow = lambda b, q: (lat_row0 // BLOCK + b * nb + q, 0)
    brow = lambda b, q: (b, 0)
    const = lambda b, q: (0, 0)
    cidx = lambda b, q: (b, layer, 0, 0)
    in_specs = [pl.BlockSpec(memory_space=pltpu.SMEM)]
    in_specs += [pl.BlockSpec((BLOCK, a.shape[1]), qrow) for a in (qa, qb, qc)]
    in_specs += [pl.BlockSpec((DEC_SEQ, a.shape[1]), brow) for a in (ka, va, kb, vb, ckv, kr)]
    in_specs += [pl.BlockSpec((1, 1, PAST_LEN, a.shape[3]), cidx) for a in caches]
    in_specs += [pl.BlockSpec(bias_tab.shape, lambda b, q: (0, 0, 0, 0)),
                 pl.BlockSpec((KV_LORA, H_C * (QK_NOPE + V_C)), const),
                 pl.BlockSpec((D_MODEL, D_MODEL), const),
                 pl.BlockSpec((BLOCK, D_MODEL), xrow),
                 pl.BlockSpec((1, 1, D_MODEL), lambda b, q: (1 + b, 0, 0))]
    return pl.pallas_call(
        _lat_attn_kernel,
        grid=(DEC_BATCH, nb),
        in_specs=in_specs,
        out_specs=pl.BlockSpec((BLOCK, D_MODEL), qrow),
        out_shape=jax.ShapeDtypeStruct((T_LAT, D_MODEL), f32),
        scratch_shapes=[pltpu.VMEM((BLOCK, D_MODEL), f32),
                        pltpu.VMEM((DEC_SEQ + PAST_LEN, H_C * (QK_NOPE + V_C)), bf16)]
        + [pltpu.VMEM(shape, dt) for shape in ((H_A * BLOCK, 3 * BLOCK + PAST_LEN),
                                               (H_B * BLOCK, NA_ROWS * GRID_W + PAST_LEN),
                                               (H_C * BLOCK, DEC_SEQ + PAST_LEN)) for dt in (f32, bf16)],
        compiler_params=_cparams("arbitrary", "arbitrary"),
        name="lat_attn",
    )(sink, qa, qb, qc, ka, va, kb, vb, ckv, kr, *caches, bias_tab, wukv, wout, x, gate)


def _pick_stream(xc_ref, xl_ref, x_scr):
    i = pl.program_id(0)

    @pl.when(i < T_CTX // TM_TOK)
    def _():
        x_scr[...] = xc_ref[...]

    @pl.when(i >= T_CTX // TM_TOK)
    def _():
        x_scr[...] = xl_ref[...]

    return x_scr[...]


def _stream_specs(lat_row0):
    n_ctx = T_CTX // TM_TOK
    return [pl.BlockSpec((TM_TOK, D_MODEL), lambda i: (jnp.minimum(i, n_ctx - 1), 0)),
            pl.BlockSpec((TM_TOK, D_MODEL), lambda i: (lat_row0 // TM_TOK + jnp.maximum(i - n_ctx, 0), 0))]


def _router_kernel(xc_ref, xl_ref, g_ref, sh_ref, sc_ref, wr_ref, br_ref, h_ref, e_ref, gt_ref, x_scr, t_scr):
    h = _rms(_pick_stream(xc_ref, xl_ref, x_scr), g_ref[...]) * (1.0 + sc_ref[0]) + sh_ref[0]
    _store_row_tiles(t_scr, h)
    h_ref[...] = t_scr[...].astype(bf16)
    logits = jnp.dot(h, wr_ref[...], preferred_element_type=f32, precision=lax.Precision.HIGHEST) + br_ref[...]
    lane = lax.broadcasted_iota(jnp.int32, logits.shape, 1).astype(f32)
    l = jnp.where(lane < N_EXPERTS, logits, -jnp.inf)
    tops, idxs = [], []
    for _ in range(TOP_K):
        m = jnp.max(l, axis=-1, keepdims=True)
        idx = jnp.min(jnp.where(l == m, lane, float(LANE)), axis=-1, keepdims=True)
        tops.append(m)
        idxs.append(idx)
        l = jnp.where(lane == idx, -jnp.inf, l)
    ex = [jnp.exp(t - tops[0]) for t in tops]
    den = ex[0] + ex[1] + ex[2] + ex[3]
    e_out = jnp.zeros(logits.shape, f32)
    g_out = jnp.zeros(logits.shape, f32)
    for k in range(TOP_K):
        e_out = jnp.where(lane == k, idxs[k], e_out)
        g_out = jnp.where(lane == k, ex[k] / den, g_out)
    e_ref[...] = e_out.astype(jnp.int32)
    gt_ref[...] = g_out


def _group_of_tile(i):
    per_b = DEC_SEQ // TM_TOK
    n_ctx = T_CTX // TM_TOK
    return jnp.where(i < n_ctx, 0, 1 + (i - n_ctx) // per_b)


def _router(xc, xl, lat_row0, g, shift, scale, wr, br):
    tm = TM_TOK
    row = lambda i: (i, 0)
    const = lambda i: (0, 0)
    grp = lambda i: (_group_of_tile(i), 0, 0)
    return pl.pallas_call(
        _router_kernel,
        grid=(T_ALL // tm,),
        in_specs=_stream_specs(lat_row0) +
                 [pl.BlockSpec((1, D_MODEL), const),
                  pl.BlockSpec((1, 1, D_MODEL), grp),
                  pl.BlockSpec((1, 1, D_MODEL), grp),
                  pl.BlockSpec((D_MODEL, LANE), const),
                  pl.BlockSpec((1, LANE), const)],
        out_specs=[pl.BlockSpec((tm * ROW_TILE, LANE), row), pl.BlockSpec((tm, LANE), row),
                   pl.BlockSpec((tm, LANE), row)],
        out_shape=[jax.ShapeDtypeStruct((T_ALL * ROW_TILE, LANE), bf16),
                   jax.ShapeDtypeStruct((T_ALL, LANE), jnp.int32),
                   jax.ShapeDtypeStruct((T_ALL, LANE), f32)],
        scratch_shapes=[pltpu.VMEM((tm, D_MODEL), f32), pltpu.VMEM((tm * ROW_TILE, LANE), f32)],
        compiler_params=_cparams("arbitrary"),
        name="router",
    )(xc, xl, g, shift, scale, wr, br)


PAIR_TILE = 2 * ROW_TILE


def _moe_kernel(layer, be_ref, nu_ref, nxt_ref, pair_ref, dst_ref, odd_ref, h_hbm, wgu_hbm, bgu_ref, wd_hbm,
                bd_ref, y_hbm, hv, xg, y0, y1, wgu_st, wd_st, wgu_bf, wd_bf, hsem, wsem, ssem):
    tm = TM_MOE
    i = pl.program_id(0)
    nb = pl.num_programs(0)
    used = i < nu_ref[0]
    yb = (y0, y1)

    def out_tile(row):
        return pl.ds(pl.multiple_of(row * ROW_TILE, ROW_TILE), ROW_TILE)

    def scatter_desc(buf, r, dst_row, s):
        return pltpu.make_async_copy(buf.at[out_tile(r)], y_hbm.at[out_tile(dst_row)], ssem.at[s])

    def scatter_wait(s):
        pltpu.make_async_copy(yb[s], y_hbm.at[pl.ds(0, tm * ROW_TILE)], ssem.at[s]).wait()

    def scatter_start(blk, s, unrolled):
        if unrolled:
            for r in range(tm):
                scatter_desc(yb[s], r, dst_ref[(blk + 1) * tm + r], s).start(priority=r % 2)
        else:
            def body(r, carry):
                scatter_desc(yb[s], r, dst_ref[(blk + 1) * tm + r], s).start()
                return carry
            lax.fori_loop(0, tm, body, 0, unroll=8)

    def weight_copies(e):
        return (pltpu.make_async_copy(wgu_hbm.at[layer, e], wgu_st, wsem.at[0]),
                pltpu.make_async_copy(wd_hbm.at[layer, e], wd_st, wsem.at[1]))

    @pl.when(i == 0)
    def _():
        for s in range(2):
            yb[s][...] = jnp.zeros_like(yb[s])
            dummy = pltpu.make_async_copy(yb[s], y_hbm.at[pl.ds((N_ASSIGN + s * tm) * ROW_TILE, tm * ROW_TILE)],
                                          ssem.at[s])
            dummy.start()
            dummy.wait()
        resident = pltpu.make_async_copy(h_hbm, hv, hsem.at[0])
        resident.start()
        for cp in weight_copies(be_ref[0]):
            cp.start()
        resident.wait()

    first = jnp.logical_and(used, jnp.logical_or(i == 0, be_ref[i] != be_ref[jnp.maximum(i - 1, 0)]))

    @pl.when(first)
    def _():
        for cp in weight_copies(0):
            cp.wait()
        wgu_bf[...] = wgu_st[...].astype(bf16)
        wd_bf[...] = wd_st[...].astype(bf16)

        @pl.when(nxt_ref[i] >= 0)
        def _():
            for cp in weight_copies(nxt_ref[i]):
                cp.start()

    def step(par):
        cur, oth = par, 1 - par

        @pl.when(i >= 1)
        def _():
            scatter_wait(cur)

        @pl.when(used)
        def _():
            scatter_start(i - 1, oth, unrolled=True)
            for r in range(tm):
                p = pair_ref[i * tm + r]
                xg[pl.ds(r * PAIR_TILE, PAIR_TILE), :] = hv[pl.ds(pl.multiple_of(p * PAIR_TILE, PAIR_TILE),
                                                                  PAIR_TILE), :].astype(f32)
            odd = odd_ref[...] != 0
            x = jnp.concatenate([jnp.where(odd, xg[pl.ds(ROW_TILE + c, tm, stride=PAIR_TILE), :],
                                           xg[pl.ds(c, tm, stride=PAIR_TILE), :]) for c in range(ROW_TILE)],
                                axis=1).astype(bf16)
            gu = _dot(x, wgu_bf[...]) + bgu_ref[0, 0]
            x_glu = jnp.minimum(gu[:, :D_FF], SWIGLU_LIMIT)
            x_lin = jnp.clip(gu[:, D_FF:], -SWIGLU_LIMIT, SWIGLU_LIMIT)
            act = x_glu * jax.nn.sigmoid(SWIGLU_ALPHA * x_glu) * (x_lin + 1.0)
            _store_row_tiles(yb[cur], _dot(act.astype(bf16), wd_bf[...]) + bd_ref[0, 0])

        @pl.when(jnp.logical_and(jnp.logical_not(used), i + 1 < nb))
        def _():
            scatter_start(i - 1, oth, unrolled=False)

        @pl.when(i == nb - 1)
        def _():
            scatter_start(i - 1, oth, unrolled=False)
            scatter_wait(oth)

    @pl.when(i % 2 == 0)
    def _():
        step(0)

    @pl.when(i % 2 == 1)
    def _():
        step(1)


def _moe(layer, routing, h, w_gu, b_gu, w_down, b_down):
    tm = TM_MOE
    block_e, n_used, nxt_e, row_tok, row_dst = routing
    ex4 = lambda i, be, nu, nxt, pair, dst: (layer, be[i], 0, 0)
    return pl.pallas_call(
        functools.partial(_moe_kernel, layer),
        grid_spec=pltpu.PrefetchScalarGridSpec(
            num_scalar_prefetch=5,
            grid=(N_MOE_BLOCKS,),
            in_specs=[pl.BlockSpec((tm, 1), lambda i, be, nu, nxt, pair, dst: (i, 0)),
                      pl.BlockSpec(memory_space=pl.ANY),
                      pl.BlockSpec(memory_space=pl.ANY),
                      pl.BlockSpec((1, 1, 1, 2 * D_FF), ex4),
                      pl.BlockSpec(memory_space=pl.ANY),
                      pl.BlockSpec((1, 1, 1, D_MODEL), ex4)],
            out_specs=pl.BlockSpec(memory_space=pl.ANY),
            scratch_shapes=[pltpu.VMEM((T_ALL * ROW_TILE, LANE), bf16), pltpu.VMEM((tm * PAIR_TILE, LANE), f32),
                            pltpu.VMEM((tm * ROW_TILE, LANE), f32), pltpu.VMEM((tm * ROW_TILE, LANE), f32),
                            pltpu.VMEM((D_MODEL, 2 * D_FF), f32), pltpu.VMEM((D_FF, D_MODEL), f32),
                            pltpu.VMEM((D_MODEL, 2 * D_FF), bf16), pltpu.VMEM((D_FF, D_MODEL), bf16),
                            pltpu.SemaphoreType.DMA((1,)), pltpu.SemaphoreType.DMA((2,)),
                            pltpu.SemaphoreType.DMA((2,))]),
        out_shape=jax.ShapeDtypeStruct(((N_ASSIGN + 2 * tm) * ROW_TILE, LANE), f32),
        compiler_params=_cparams("arbitrary"),
        name="moe",
    )(block_e, n_used, nxt_e, row_tok // 2, row_dst, (row_tok % 2).reshape(-1, 1), h, w_gu,
      b_gu.reshape(DEPTH, N_EXPERTS, 1, 2 * D_FF), w_down, b_down.reshape(DEPTH, N_EXPERTS, 1, D_MODEL))


def _combine_kernel(final, xc_ref, xl_ref, y0_ref, y1_ref, y2_ref, y3_ref, gt_ref, gate_ref, gf_ref, *rest):
    x_scr = rest[-1]
    gt = gt_ref[...]
    f = gt[:, 0:1] * _load_row_tiles(y0_ref)
    for k, y_ref in ((1, y1_ref), (2, y2_ref), (3, y3_ref)):
        f = f + gt[:, k:k + 1] * _load_row_tiles(y_ref)
    out = _pick_stream(xc_ref, xl_ref, x_scr) + gate_ref[0] * f
    if not final:
        rest[0][...] = out
        return
    out = _rms(out, gf_ref[...])
    oc_ref, ol_ref = rest[0], rest[1]
    i = pl.program_id(0)

    @pl.when(i < T_CTX // TM_TOK)
    def _():
        oc_ref[...] = out

    @pl.when(i >= T_CTX // TM_TOK)
    def _():
        ol_ref[...] = out


def _combine(final, xc, xl, lat_row0, y, gates, gate, g_final):
    tm = TM_TOK
    nt = T_ALL // tm
    n_ctx = T_CTX // tm
    row = lambda i: (i, 0)
    const = lambda i: (0, 0)
    grp = lambda i: (_group_of_tile(i), 0, 0)
    ysel = [pl.BlockSpec((tm * ROW_TILE, LANE), functools.partial(lambda k, i: (k * nt + i, 0), k))
            for k in range(TOP_K)]
    if final:
        out_specs = [pl.BlockSpec((tm, D_MODEL), lambda i: (jnp.minimum(i, n_ctx - 1), 0)),
                     pl.BlockSpec((tm, D_MODEL), lambda i: (jnp.maximum(i - n_ctx, 0), 0))]
        out_shape = [jax.ShapeDtypeStruct((T_CTX, D_MODEL), f32), jax.ShapeDtypeStruct((T_LAT, D_MODEL), f32)]
    else:
        out_specs = pl.BlockSpec((tm, D_MODEL), row)
        out_shape = jax.ShapeDtypeStruct((T_ALL, D_MODEL), f32)
    return pl.pallas_call(
        functools.partial(_combine_kernel, final),
        grid=(nt,),
        in_specs=_stream_specs(lat_row0) + ysel +
                 [pl.BlockSpec((tm, LANE), row),
                  pl.BlockSpec((1, 1, D_MODEL), grp),
                  pl.BlockSpec((1, D_MODEL), const)],
        out_specs=out_specs,
        out_shape=out_shape,
        scratch_shapes=[pltpu.VMEM((tm, D_MODEL), f32)],
        compiler_params=_cparams("arbitrary"),
        name="combine",
    )(xc, xl, y, y, y, y, gates, gate, g_final)


def _rope_head_tables(d):
    nf = d // 4
    half = d // 2
    t = np.arange(DEC_SEQ)
    inv = ROPE_BASE ** (-np.arange(nf, dtype=np.float32) / nf)
    i = np.arange(d)
    pos = np.where(i[None, :] < half, (t // GRID_W)[:, None], (t % GRID_W)[:, None]).astype(np.float32)
    ang = pos * inv[i % nf][None, :].astype(np.float32)
    first = (i % half) < nf
    cos = np.cos(ang)
    sin = np.where(first[None, :], -np.sin(ang), np.sin(ang))
    partner = np.where(first, i + nf, i - nf)
    return cos.astype(np.float32), sin.astype(np.float32), partner


def _rope_tables():
    cos64, sin64, _ = _rope_head_tables(HEAD_DIM)
    cos32, sin32, _ = _rope_head_tables(QK_ROPE)
    cosa = np.tile(cos64, (1, H_A))
    sina = np.tile(sin64, (1, H_A))
    cosq1 = np.concatenate([np.ones((DEC_SEQ, QK_NOPE), np.float32), cos32,
                            np.ones((DEC_SEQ, QC_PAD - QK_NOPE - QK_ROPE), np.float32)], axis=1)
    sinq1 = np.concatenate([np.zeros((DEC_SEQ, QK_NOPE), np.float32), sin32,
                            np.zeros((DEC_SEQ, QC_PAD - QK_NOPE - QK_ROPE), np.float32)], axis=1)
    cosq = np.tile(cosq1, (1, H_C))
    sinq = np.tile(sinq1, (1, H_C))
    return tuple(jnp.asarray(a) for a in (cosa, sina, cosq, sinq, cos32, sin32))


def _pad_cols(w, n):
    return jnp.pad(w, ((0, 0), (0, n - w.shape[1])))


def _layer_weights(w_in, w_uq):
    cuts = np.cumsum((W_QA, W_KA, W_VA, W_B, W_B, W_B, Q_LORA, KV_LORA, QK_ROPE))[:-1]
    qa, ka, va, qb, kb, vb, cq, ckv, kr = jnp.split(w_in, [int(c) for c in cuts], axis=1)
    _, _, p64 = _rope_head_tables(HEAD_DIM)
    _, _, p32 = _rope_head_tables(QK_ROPE)
    pa = np.concatenate([h * HEAD_DIM + p64 for h in range(H_A)])
    base = jnp.concatenate([qa, ka, va, _pad_cols(qb, 384), _pad_cols(kb, 384), _pad_cols(vb, 384), cq, ckv,
                            _pad_cols(kr, 128)], axis=1)
    w_ctx = base.astype(bf16)
    w_lat = jnp.concatenate([base, qa[:, pa], ka[:, pa[:W_KA]], _pad_cols(kr[:, p32], 128)], axis=1).astype(bf16)
    hq = QK_NOPE + QK_ROPE
    heads = [_pad_cols(w_uq[:, h * hq:(h + 1) * hq], QC_PAD) for h in range(H_C)]
    pq = np.concatenate([np.arange(QK_NOPE), QK_NOPE + p32])
    heads_p = [_pad_cols(w_uq[:, h * hq:(h + 1) * hq][:, pq], QC_PAD) for h in range(H_C)]
    wuq = jnp.concatenate(heads, axis=1).astype(bf16)
    wuq2 = jnp.concatenate(heads + heads_p, axis=1).astype(bf16)
    return w_ctx, w_lat, wuq, wuq2


def _bias_table(rpb):
    col = np.arange(GRID_W)
    col_start = np.clip(col - NA_COLS // 2, 0, GRID_W - NA_COLS)
    col_ok = (col[None, :] >= col_start[:, None]) & (col[None, :] < col_start[:, None] + NA_COLS)
    dc = np.clip(col[None, :] - col[:, None] + (NA_COLS - 1), 0, 2 * NA_COLS - 2)
    onehot = (dc[None] == np.arange(2 * NA_COLS - 1)[:, None, None]).astype(np.float32)
    expanded = jnp.einsum('hrd,dqk->hrqk', rpb.astype(f32), jnp.asarray(onehot), precision=lax.Precision.HIGHEST)
    blocks = jnp.where(col_ok[None, None], expanded, NEG)
    return jnp.concatenate([blocks[:, :-1], blocks[:, 1:]], axis=-1)


def _routing(top_e):
    tm = TM_MOE
    key_bits = 16
    pad_mark = (1 << key_bits) - 1
    flat_e = top_e.T.reshape(N_ASSIGN)
    experts = jnp.arange(N_EXPERTS, dtype=jnp.int32)
    counts = jnp.sum((flat_e[:, None] == experts[None, :]).astype(jnp.int32), axis=0)
    nblk = (counts + tm - 1) // tm
    blk_end = jnp.cumsum(nblk)
    pad_end = jnp.cumsum(nblk * tm - counts)
    slots = jnp.arange(N_MOE_BLOCKS * tm - N_ASSIGN, dtype=jnp.int32)
    pad_e = jnp.sum((pad_end[None, :] <= slots[:, None]).astype(jnp.int32), axis=1)
    keys = jnp.concatenate([(flat_e << key_bits) + jnp.arange(N_ASSIGN, dtype=jnp.int32),
                            (pad_e << key_bits) + pad_mark])
    asg = (jnp.sort(keys) & pad_mark).reshape(N_MOE_BLOCKS, tm)
    valid = asg != pad_mark
    blocks = jnp.arange(N_MOE_BLOCKS, dtype=jnp.int32)
    r = jnp.arange(tm, dtype=jnp.int32)[None, :]
    tok = jnp.where(valid, asg % T_ALL, 0)
    row_dst = jnp.where(valid, asg, N_ASSIGN + (blocks[:, None] % 2) * tm + r)
    row_dst = jnp.concatenate([N_ASSIGN + tm + r, row_dst], axis=0).reshape(-1)
    block_e = jnp.minimum(jnp.sum((blk_end[None, :] <= blocks[:, None]).astype(jnp.int32), axis=1), N_EXPERTS - 1)
    n_used = blk_end[-1].astype(jnp.int32).reshape(1)
    has = jnp.where(counts > 0, experts, N_EXPERTS)
    later = experts[None, :] > experts[:, None]
    nxt = jnp.min(jnp.where(later, has[None, :], N_EXPERTS), axis=1)
    nxt = jnp.where(nxt >= N_EXPERTS, -1, nxt)
    sel = (block_e[:, None] == experts[None, :]).astype(jnp.int32)
    nxt_e = jnp.sum(sel * nxt[None, :], axis=1)
    i32 = lambda a: a.astype(jnp.int32)
    return i32(block_e), n_used, i32(nxt_e), i32(tok).reshape(-1), i32(row_dst)


def kernel(x_prompt, x_sample, cache_a_k, cache_a_v, cache_b_k, cache_b_v, cache_c_kv, cache_c_kr, c, c_ctx, w_ada, b_ada, g_attn, g_ffn, w_in, sink_a, rpb_b, g_cq, g_ckv, w_uq, w_ukv, w_out, w_router, b_router, w_gu, b_gu, w_down, b_down, g_final):
    xc, xl, lat_row0 = x_prompt.reshape(T_CTX, D_MODEL), x_sample.reshape(T_LAT, D_MODEL), 0
    cvec = jnp.concatenate([c_ctx[None, :], c, jnp.zeros((8 - N_GROUPS, D_MODEL), f32)], axis=0)
    mods = _ada(cvec, w_ada, b_ada)[:, :N_GROUPS].reshape(DEPTH, N_GROUPS, 6, 1, D_MODEL)
    tabs = _rope_tables()
    caches = (cache_a_k.reshape(DEC_BATCH, DEPTH, PAST_LEN, W_KA), cache_a_v.reshape(DEC_BATCH, DEPTH, PAST_LEN, W_VA),
              cache_b_k.reshape(DEC_BATCH, DEPTH, PAST_LEN, W_B), cache_b_v.reshape(DEC_BATCH, DEPTH, PAST_LEN, W_B),
              cache_c_kv, cache_c_kr)
    new = [[] for _ in range(6)]
    for layer in range(DEPTH):
        m = [mods[layer, :, j] for j in range(6)]
        w_ctx, w_lat, wuq, wuq2 = _layer_weights(w_in[layer], w_uq[layer])
        wukv = w_ukv[layer].astype(bf16)
        wout = w_out[layer].astype(bf16)
        g1 = g_attn[layer][None, :]
        gcq = g_cq[layer][None, :]
        gckv = g_ckv[layer][None, :]
        sink = sink_a[layer]

        pc = _inproj_ctx(xc, g1, m[0], m[1], w_ctx, gcq, gckv, wuq)
        for lst, a in zip(new, (pc[1], pc[2], pc[4], pc[5], pc[7], pc[8])):
            lst.append(a)
        x_ctx = _ctx_attn(sink, pc, wukv, wout, xc, m[2])

        plat = _inproj_lat(xl, lat_row0, g1, m[0], m[1], w_lat, gcq, gckv, wuq2, tabs)
        x_lat = _lat_attn(layer, sink, plat, caches, _bias_table(rpb_b[layer]), wukv, wout, xl, lat_row0, m[2])

        wr = _pad_cols(w_router[layer], LANE)
        br = _pad_cols(b_router[layer][None, :], LANE)
        h2, top_e, gates = _router(x_ctx, x_lat, 0, g_ffn[layer][None, :], m[3], m[4], wr, br)
        y = _moe(layer, _routing(top_e[:, :TOP_K]), h2, w_gu, b_gu, w_down, b_down)
        x = _combine(layer == DEPTH - 1, x_ctx, x_lat, 0, y, gates, m[5], g_final[None, :])
        xc, xl, lat_row0 = x, x, T_CTX

    y_prompt = x[0].reshape(BATCH, SEQ, D_MODEL)
    y_sample = x[1].reshape(DEC_BATCH, DEC_SEQ, D_MODEL)
    shapes = ((KV_A, HEAD_DIM), (KV_A, HEAD_DIM), (H_B, HEAD_DIM), (H_B, HEAD_DIM), (KV_LORA,), (QK_ROPE,))
    outs = [jnp.stack([a.reshape((BATCH, SEQ) + s) for a in lst], axis=1) for lst, s in zip(new, shapes)]
    return (y_prompt, y_sample, *outs)
```

```python
import functools

import numpy as np
import jax
import jax.numpy as jnp
from jax import lax
from jax.experimental import pallas as pl
from jax.experimental.pallas import tpu as pltpu

D_MODEL = 1024
BATCH = 32
SEQ = 256
DEPTH = 2
DEC_BATCH = 2
DEC_SEQ = 1024
PAST_LEN = 512
GRID_W = 64
HEAD_DIM = 64
H_A = 6
KV_A = 2
G_A = H_A // KV_A
WINDOW = 128
BLOCK = 128
H_B = 5
NA_ROWS = 8
NA_COLS = 16
H_C = 5
Q_LORA = 384
KV_LORA = 256
QK_NOPE = 64
QK_ROPE = 32
V_C = 64
N_EXPERTS = 32
TOP_K = 4
D_FF = 1024
SWIGLU_ALPHA = 1.702
SWIGLU_LIMIT = 7.0
ROPE_BASE = 10000.0
EPS = 1e-6
NEG = -1e30

T_CTX = BATCH * SEQ
T_LAT = DEC_BATCH * DEC_SEQ
T_ALL = T_CTX + T_LAT
N_GROUPS = 1 + DEC_BATCH
LANE = 128
QC_PAD = 128
ROWS = DEC_SEQ // GRID_W

W_QA, W_KA, W_VA = H_A * HEAD_DIM, KV_A * HEAD_DIM, KV_A * HEAD_DIM
W_B = H_B * HEAD_DIM
OFF_QA = 0
OFF_KA = 384
OFF_VA = 512
OFF_QB = 640
OFF_KB = 1024
OFF_VB = 1408
OFF_CQ = 1792
OFF_CKV = 2176
OFF_KR = 2432
NW_CTX = 2560
OFF_QA_P = 2560
OFF_KA_P = 2944
OFF_KR_P = 3072
NW_LAT = 3200

TM_TOK = 512
TM_LAT_IN = 512
TM_MOE = 256
N_ASSIGN = T_ALL * TOP_K
N_MOE_BLOCKS = N_ASSIGN // TM_MOE + N_EXPERTS
VMEM_LIMIT = 56 * 1024 * 1024

f32 = jnp.float32
bf16 = jnp.bfloat16


def _cparams(*sem):
    return pltpu.CompilerParams(dimension_semantics=sem, vmem_limit_bytes=VMEM_LIMIT)


def _rms(xf, g):
    return xf * lax.rsqrt(jnp.mean(xf * xf, axis=-1, keepdims=True) + EPS) * g


def _dot(a, b):
    return jnp.dot(a, b, preferred_element_type=f32)


def _dot_nt(a, b):
    return lax.dot_general(a, b, (((1,), (1,)), ((), ())), preferred_element_type=f32)


ROW_TILE = D_MODEL // LANE


def _store_row_tiles(ref, val):
    n = val.shape[0]
    for c in range(ROW_TILE):
        ref[pl.ds(c, n, stride=ROW_TILE), :] = val[:, c * LANE:(c + 1) * LANE]


def _load_row_tiles(ref):
    n = ref.shape[0] // ROW_TILE
    return jnp.concatenate([ref[pl.ds(c, n, stride=ROW_TILE), :] for c in range(ROW_TILE)], axis=1)


def _softmax_rows(s_ref, p_ref, rows, sinks=None):
    s = s_ref[rows, :]
    m = jnp.max(s, axis=-1, keepdims=True)
    if sinks is not None:
        sink = jnp.concatenate([jnp.full((n, 1), v, f32) for v, n in sinks], axis=0)
        m = jnp.maximum(m, sink)
    p = jnp.exp(s - m)
    l = jnp.sum(p, axis=-1, keepdims=True)
    if sinks is not None:
        l = l + jnp.exp(sink - m)
    p_ref[rows, :] = (p * (1.0 / l)).astype(bf16)


def _ada_kernel(c_ref, w_ref, b_ref, o_ref):
    c = c_ref[...]
    s = c * jax.nn.sigmoid(c)
    o_ref[0] = jnp.dot(s, w_ref[0], preferred_element_type=f32, precision=lax.Precision.HIGHEST) + b_ref[0]


def _ada(cvec, w_ada, b_ada):
    tn = 1536
    return pl.pallas_call(
        _ada_kernel,
        grid=(DEPTH, 6 * D_MODEL // tn),
        in_specs=[pl.BlockSpec((8, D_MODEL), lambda l, j: (0, 0)),
                  pl.BlockSpec((1, D_MODEL, tn), lambda l, j: (l, 0, j)),
                  pl.BlockSpec((1, 1, tn), lambda l, j: (l, 0, j))],
        out_specs=pl.BlockSpec((1, 8, tn), lambda l, j: (l, 0, j)),
        out_shape=jax.ShapeDtypeStruct((DEPTH, 8, 6 * D_MODEL), f32),
        compiler_params=_cparams("arbitrary", "arbitrary"),
        name="ada",
    )(cvec, w_ada, b_ada.reshape(DEPTH, 1, 6 * D_MODEL))


def _inproj_ctx_kernel(x_ref, g_ref, sh_ref, sc_ref, w_ref, gcq_ref, gckv_ref, wuq_ref,
                       qa_ref, ka_ref, va_ref, qb_ref, kb_ref, vb_ref, qc_ref, ckv_ref, kr_ref):
    h = _rms(x_ref[...], g_ref[...]) * (1.0 + sc_ref[0]) + sh_ref[0]
    p = _dot(h.astype(bf16), w_ref[...])
    qa_ref[...] = p[:, OFF_QA:OFF_QA + W_QA].astype(bf16)
    ka_ref[...] = p[:, OFF_KA:OFF_KA + W_KA]
    va_ref[...] = p[:, OFF_VA:OFF_VA + W_VA]
    qb_ref[...] = p[:, OFF_QB:OFF_QB + W_B].astype(bf16)
    kb_ref[...] = p[:, OFF_KB:OFF_KB + W_B]
    vb_ref[...] = p[:, OFF_VB:OFF_VB + W_B]
    cqn = _rms(p[:, OFF_CQ:OFF_CQ + Q_LORA], gcq_ref[...])
    qc_ref[...] = _dot(cqn.astype(bf16), wuq_ref[...]).astype(bf16)
    ckv_ref[...] = _rms(p[:, OFF_CKV:OFF_CKV + KV_LORA], gckv_ref[...])
    kr_ref[...] = p[:, OFF_KR:OFF_KR + QK_ROPE]


def _inproj_ctx(x, g, shift, scale, w, gcq, gckv, wuq):
    tm = TM_TOK
    row = lambda i: (i, 0)
    const = lambda i: (0, 0)
    widths = (W_QA, W_KA, W_VA, W_B, W_B, W_B, H_C * QC_PAD, KV_LORA, QK_ROPE)
    dtypes = (bf16, f32, f32, bf16, f32, f32, bf16, f32, f32)
    return pl.pallas_call(
        _inproj_ctx_kernel,
        grid=(T_CTX // tm,),
        in_specs=[pl.BlockSpec((tm, D_MODEL), row),
                  pl.BlockSpec((1, D_MODEL), const),
                  pl.BlockSpec((1, 1, D_MODEL), lambda i: (0, 0, 0)),
                  pl.BlockSpec((1, 1, D_MODEL), lambda i: (0, 0, 0)),
                  pl.BlockSpec((D_MODEL, NW_CTX), const),
                  pl.BlockSpec((1, Q_LORA), const),
                  pl.BlockSpec((1, KV_LORA), const),
                  pl.BlockSpec((Q_LORA, H_C * QC_PAD), const)],
        out_specs=[pl.BlockSpec((tm, wd), row) for wd in widths],
        out_shape=[jax.ShapeDtypeStruct((T_CTX, wd), dt) for wd, dt in zip(widths, dtypes)],
        compiler_params=_cparams("arbitrary"),
        name="inproj_ctx",
    )(x, g, shift, scale, w, gcq, gckv, wuq)


def _inproj_lat_kernel(x_ref, g_ref, sh_ref, sc_ref, w_ref, gcq_ref, gckv_ref, wuq_ref,
                       cosa_ref, sina_ref, cosq_ref, sinq_ref, cosr_ref, sinr_ref,
                       qa_ref, ka_ref, va_ref, qb_ref, kb_ref, vb_ref, qc_ref, ckv_ref, kr_ref):
    h = _rms(x_ref[...], g_ref[...]) * (1.0 + sc_ref[0]) + sh_ref[0]
    p = _dot(h.astype(bf16), w_ref[...])
    cosa = cosa_ref[...]
    sina = sina_ref[...]
    qa = p[:, OFF_QA:OFF_QA + W_QA] * cosa + p[:, OFF_QA_P:OFF_QA_P + W_QA] * sina
    ka = p[:, OFF_KA:OFF_KA + W_KA] * cosa[:, :W_KA] + p[:, OFF_KA_P:OFF_KA_P + W_KA] * sina[:, :W_KA]
    kr = p[:, OFF_KR:OFF_KR + QK_ROPE] * cosr_ref[...] + p[:, OFF_KR_P:OFF_KR_P + QK_ROPE] * sinr_ref[...]
    qa_ref[...] = qa.astype(bf16)
    ka_ref[...] = ka.astype(bf16)
    va_ref[...] = p[:, OFF_VA:OFF_VA + W_VA].astype(bf16)
    qb_ref[...] = p[:, OFF_QB:OFF_QB + W_B].astype(bf16)
    kb_ref[...] = p[:, OFF_KB:OFF_KB + W_B].astype(bf16)
    vb_ref[...] = p[:, OFF_VB:OFF_VB + W_B].astype(bf16)
    cqn = _rms(p[:, OFF_CQ:OFF_CQ + Q_LORA], gcq_ref[...])
    q2 = _dot(cqn.astype(bf16), wuq_ref[...])
    nq = H_C * QC_PAD
    qc_ref[...] = (q2[:, :nq] * cosq_ref[...] + q2[:, nq:] * sinq_ref[...]).astype(bf16)
    ckv_ref[...] = _rms(p[:, OFF_CKV:OFF_CKV + KV_LORA], gckv_ref[...]).astype(bf16)
    kr_ref[...] = kr.astype(bf16)


def _inproj_lat(x, lat_row0, g, shift, scale, w, gcq, gckv, wuq2, tabs):
    tm = TM_LAT_IN
    per_b = DEC_SEQ // tm
    row0 = lat_row0 // tm
    xrow = lambda i: (row0 + i, 0)
    row = lambda i: (i, 0)
    const = lambda i: (0, 0)
    grp = lambda i: (1 + i // per_b, 0, 0)
    pos = lambda i: (i % per_b, 0)
    cosa, sina, cosq, sinq, cosr, sinr = tabs
    widths = (W_QA, W_KA, W_VA, W_B, W_B, W_B, H_C * QC_PAD, KV_LORA, QK_ROPE)
    return pl.pallas_call(
        _inproj_lat_kernel,
        grid=(T_LAT // tm,),
        in_specs=[pl.BlockSpec((tm, D_MODEL), xrow),
                  pl.BlockSpec((1, D_MODEL), const),
                  pl.BlockSpec((1, 1, D_MODEL), grp),
                  pl.BlockSpec((1, 1, D_MODEL), grp),
                  pl.BlockSpec((D_MODEL, NW_LAT), const),
                  pl.BlockSpec((1, Q_LORA), const),
                  pl.BlockSpec((1, KV_LORA), const),
                  pl.BlockSpec((Q_LORA, 2 * H_C * QC_PAD), const),
                  pl.BlockSpec((tm, W_QA), pos), pl.BlockSpec((tm, W_QA), pos),
                  pl.BlockSpec((tm, H_C * QC_PAD), pos), pl.BlockSpec((tm, H_C * QC_PAD), pos),
                  pl.BlockSpec((tm, QK_ROPE), pos), pl.BlockSpec((tm, QK_ROPE), pos)],
        out_specs=[pl.BlockSpec((tm, wd), row) for wd in widths],
        out_shape=[jax.ShapeDtypeStruct((T_LAT, wd), bf16) for wd in widths],
        compiler_params=_cparams("arbitrary"),
        name="inproj_lat",
    )(x, g, shift, scale, w, gcq, gckv, wuq2, cosa, sina, cosq, sinq, cosr, sinr)


def _ctx_attn_kernel(sink_ref, qa_ref, ka_ref, va_ref, qb_ref, kb_ref, vb_ref, qc_ref, ckv_ref, kr_ref,
                     wukv_ref, wout_ref, x_ref, gate_ref, o_ref, o_scr, s_scr, p_scr):
    n = SEQ
    scale = HEAD_DIM ** -0.5
    scale_c = (QK_NOPE + QK_ROPE) ** -0.5
    ka = ka_ref[...].astype(bf16)
    va = va_ref[...].astype(bf16)
    kb = kb_ref[...].astype(bf16)
    vb = vb_ref[...].astype(bf16)
    kv = _dot(ckv_ref[...].astype(bf16), wukv_ref[...]).astype(bf16)
    kr = kr_ref[...].astype(bf16)
    for h in range(H_A):
        g = h // G_A
        q = qa_ref[:, h * HEAD_DIM:(h + 1) * HEAD_DIM]
        s_scr[h * n:(h + 1) * n, :] = _dot_nt(q, ka[:, g * HEAD_DIM:(g + 1) * HEAD_DIM]) * scale
    for h in range(H_B):
        sl = slice(h * HEAD_DIM, (h + 1) * HEAD_DIM)
        s_scr[(H_A + h) * n:(H_A + h + 1) * n, :] = _dot_nt(qb_ref[:, sl], kb[:, sl]) * scale
    for h in range(H_C):
        qn = qc_ref[:, h * QC_PAD:h * QC_PAD + QK_NOPE]
        qr = qc_ref[:, h * QC_PAD + QK_NOPE:h * QC_PAD + QK_NOPE + QK_ROPE]
        c0 = h * (QK_NOPE + V_C)
        r0 = (H_A + H_B + h) * n
        s_scr[r0:r0 + n, :] = (_dot_nt(qn, kv[:, c0:c0 + QK_NOPE]) + _dot_nt(qr, kr)) * scale_c
    for pair in range((H_A + H_B + H_C) // 2):
        h0 = 2 * pair
        sinks = ((sink_ref[h0], n), (sink_ref[h0 + 1], n)) if h0 < H_A else None
        _softmax_rows(s_scr, p_scr, slice(h0 * n, (h0 + 2) * n), sinks)
    for h in range(H_A):
        g = h // G_A
        o_scr[:, h * HEAD_DIM:(h + 1) * HEAD_DIM] = _dot(p_scr[h * n:(h + 1) * n, :],
                                                         va[:, g * HEAD_DIM:(g + 1) * HEAD_DIM])
    for h in range(H_B):
        sl = slice(h * HEAD_DIM, (h + 1) * HEAD_DIM)
        o_scr[:, W_QA + h * HEAD_DIM:W_QA + (h + 1) * HEAD_DIM] = _dot(p_scr[(H_A + h) * n:(H_A + h + 1) * n, :],
                                                                     vb[:, sl])
    for h in range(H_C):
        c0 = h * (QK_NOPE + V_C)
        r0 = (H_A + H_B + h) * n
        off = W_QA + W_B + h * V_C
        o_scr[:, off:off + V_C] = _dot(p_scr[r0:r0 + n, :], kv[:, c0 + QK_NOPE:c0 + QK_NOPE + V_C])
    y = _dot(o_scr[...].astype(bf16), wout_ref[...])
    o_ref[...] = x_ref[...] + gate_ref[0] * y


def _ctx_attn(sink, proj, wukv, wout, x, gate):
    qa, ka, va, qb, kb, vb, qc, ckv, kr = proj
    row = lambda b: (b, 0)
    const = lambda b: (0, 0)
    in_specs = [pl.BlockSpec(memory_space=pltpu.SMEM)]
    in_specs += [pl.BlockSpec((SEQ, a.shape[1]), row) for a in proj]
    in_specs += [pl.BlockSpec((KV_LORA, H_C * (QK_NOPE + V_C)), const),
                 pl.BlockSpec((D_MODEL, D_MODEL), const),
                 pl.BlockSpec((SEQ, D_MODEL), row),
                 pl.BlockSpec((1, 1, D_MODEL), lambda b: (0, 0, 0))]
    return pl.pallas_call(
        _ctx_attn_kernel,
        grid=(BATCH,),
        in_specs=in_specs,
        out_specs=pl.BlockSpec((SEQ, D_MODEL), row),
        out_shape=jax.ShapeDtypeStruct((T_CTX, D_MODEL), f32),
        scratch_shapes=[pltpu.VMEM((SEQ, D_MODEL), f32),
                        pltpu.VMEM(((H_A + H_B + H_C) * SEQ, SEQ), f32),
                        pltpu.VMEM(((H_A + H_B + H_C) * SEQ, SEQ), bf16)],
        compiler_params=_cparams("arbitrary"),
        name="ctx_attn",
    )(sink, qa, ka, va, qb, kb, vb, qc, ckv, kr, wukv, wout, x, gate)


def _lat_attn_kernel(sink_ref, qa_ref, qb_ref, qc_ref, ka_ref, va_ref, kb_ref, vb_ref, ckv_ref, kr_ref,
                     cak_ref, cav_ref, cbk_ref, cbv_ref, cckv_ref, ckr_ref, bias_ref,
                     wukv_ref, wout_ref, x_ref, gate_ref, o_ref, o_scr, kv_scr, sa, pa, sb, pb, sc, pc):
    qi = pl.program_id(1)
    nb = DEC_SEQ // BLOCK
    scale = HEAD_DIM ** -0.5

    @pl.when(qi == 0)
    def _():
        kv_scr[0:DEC_SEQ, :] = _dot(ckv_ref[...], wukv_ref[...]).astype(bf16)
        kv_scr[DEC_SEQ:DEC_SEQ + PAST_LEN, :] = _dot(cckv_ref[0, 0].astype(bf16), wukv_ref[...]).astype(bf16)

    def blk(ref, j):
        idx = jnp.clip(qi + j, 0, nb - 1)
        return ref[pl.ds(pl.multiple_of(idx * BLOCK, BLOCK), BLOCK), :]

    ka = jnp.concatenate([blk(ka_ref, -1), blk(ka_ref, 0), blk(ka_ref, 1), cak_ref[0, 0].astype(bf16)], axis=0)
    va = jnp.concatenate([blk(va_ref, -1), blk(va_ref, 0), blk(va_ref, 1), cav_ref[0, 0].astype(bf16)], axis=0)
    nk_a = 3 * BLOCK + PAST_LEN
    r = lax.broadcasted_iota(jnp.int32, (BLOCK, nk_a), 0)
    c = lax.broadcasted_iota(jnp.int32, (BLOCK, nk_a), 1)
    valid = (((c < BLOCK) & (c >= r) & (qi > 0))
             | ((c >= BLOCK) & (c < 2 * BLOCK))
             | ((c >= 2 * BLOCK) & (c < 3 * BLOCK) & (c - 2 * BLOCK <= r) & (qi < nb - 1))
             | (c >= 3 * BLOCK))
    for h in range(H_A):
        g = h // G_A
        q = qa_ref[:, h * HEAD_DIM:(h + 1) * HEAD_DIM]
        s = _dot_nt(q, ka[:, g * HEAD_DIM:(g + 1) * HEAD_DIM]) * scale
        sa[h * BLOCK:(h + 1) * BLOCK, :] = jnp.where(valid, s, NEG)

    cbk = cbk_ref[0, 0].astype(bf16)
    cbv = cbv_ref[0, 0].astype(bf16)
    rows_per_blk = BLOCK // GRID_W
    nloc = NA_ROWS * GRID_W
    vcats = []
    for half in range(rows_per_blk):
        grow = qi * rows_per_blk + half
        start = jnp.clip(grow - NA_ROWS // 2, 0, ROWS - NA_ROWS)
        kloc = kb_ref[pl.ds(pl.multiple_of(start * GRID_W, GRID_W), nloc), :]
        vloc = vb_ref[pl.ds(pl.multiple_of(start * GRID_W, GRID_W), nloc), :]
        vcats.append(jnp.concatenate([vloc, cbv], axis=0))
        qrows = slice(half * GRID_W, (half + 1) * GRID_W)
        dr0 = start - grow + (NA_ROWS - 1)
        for h in range(H_B):
            sl = slice(h * HEAD_DIM, (h + 1) * HEAD_DIM)
            q = qb_ref[qrows, sl]
            bias = jnp.concatenate([bias_ref[h, dr0 + 2 * j] for j in range(NA_ROWS // 2)], axis=1)
            s_loc = _dot_nt(q, kloc[:, sl]) * scale + bias
            s_ctx = _dot_nt(q, cbk[:, sl]) * scale
            r0 = (half * H_B + h) * GRID_W
            sb[r0:r0 + GRID_W, :] = jnp.concatenate([s_loc, s_ctx], axis=1)

    kr = jnp.concatenate([kr_ref[...], ckr_ref[0, 0].astype(bf16)], axis=0)
    scale_c = (QK_NOPE + QK_ROPE) ** -0.5
    for h in range(H_C):
        qn = qc_ref[:, h * QC_PAD:h * QC_PAD + QK_NOPE]
        qr = qc_ref[:, h * QC_PAD + QK_NOPE:h * QC_PAD + QK_NOPE + QK_ROPE]
        c0 = h * (QK_NOPE + V_C)
        sc[h * BLOCK:(h + 1) * BLOCK, :] = (_dot_nt(qn, kv_scr[:, c0:c0 + QK_NOPE]) + _dot_nt(qr, kr)) * scale_c

    for pair in range(H_A // 2):
        h0 = 2 * pair
        _softmax_rows(sa, pa, slice(h0 * BLOCK, (h0 + 2) * BLOCK), ((sink_ref[h0], BLOCK), (sink_ref[h0 + 1], BLOCK)))
    for blk2 in range(rows_per_blk * H_B // 2):
        _softmax_rows(sb, pb, slice(blk2 * 2 * GRID_W, (blk2 + 1) * 2 * GRID_W))
    for h in range(H_C):
        _softmax_rows(sc, pc, slice(h * BLOCK, (h + 1) * BLOCK))

    for h in range(H_A):
        g = h // G_A
        o_scr[:, h * HEAD_DIM:(h + 1) * HEAD_DIM] = _dot(pa[h * BLOCK:(h + 1) * BLOCK, :],
                                                         va[:, g * HEAD_DIM:(g + 1) * HEAD_DIM])
    for half in range(rows_per_blk):
        qrows = slice(half * GRID_W, (half + 1) * GRID_W)
        for h in range(H_B):
            sl = slice(h * HEAD_DIM, (h + 1) * HEAD_DIM)
            r0 = (half * H_B + h) * GRID_W
            o_scr[qrows, W_QA + h * HEAD_DIM:W_QA + (h + 1) * HEAD_DIM] = _dot(pb[r0:r0 + GRID_W, :],
                                                                             vcats[half][:, sl])
    for h in range(H_C):
        c0 = h * (QK_NOPE + V_C)
        off = W_QA + W_B + h * V_C
        o_scr[:, off:off + V_C] = _dot(pc[h * BLOCK:(h + 1) * BLOCK, :], kv_scr[:, c0 + QK_NOPE:c0 + QK_NOPE + V_C])

    y = _dot(o_scr[...].astype(bf16), wout_ref[...])
    o_ref[...] = x_ref[...] + gate_ref[0] * y


def _lat_attn(layer, sink, proj, caches, bias_tab, wukv, wout, x, lat_row0, gate):
    qa, ka, va, qb, kb, vb, qc, ckv, kr = proj
    nb = DEC_SEQ // BLOCK
    qrow = lambda b, q: (b * nb + q, 0)
    xrow = lambda b, q: (lat_row0 // BLOCK + b * nb + q, 0)
    brow = lambda b, q: (b, 0)
    const = lambda b, q: (0, 0)
    cidx = lambda b, q: (b, layer, 0, 0)
    in_specs = [pl.BlockSpec(memory_space=pltpu.SMEM)]
    in_specs += [pl.BlockSpec((BLOCK, a.shape[1]), qrow) for a in (qa, qb, qc)]
    in_specs += [pl.BlockSpec((DEC_SEQ, a.shape[1]), brow) for a in (ka, va, kb, vb, ckv, kr)]
    in_specs += [pl.BlockSpec((1, 1, PAST_LEN, a.shape[3]), cidx) for a in caches]
    in_specs += [pl.BlockSpec(bias_tab.shape, lambda b, q: (0, 0, 0, 0)),
                 pl.BlockSpec((KV_LORA, H_C * (QK_NOPE + V_C)), const),
                 pl.BlockSpec((D_MODEL, D_MODEL), const),
                 pl.BlockSpec((BLOCK, D_MODEL), xrow),
                 pl.BlockSpec((1, 1, D_MODEL), lambda b, q: (1 + b, 0, 0))]
    return pl.pallas_call(
        _lat_attn_kernel,
        grid=(DEC_BATCH, nb),
        in_specs=in_specs,
        out_specs=pl.BlockSpec((BLOCK, D_MODEL), qrow),
        out_shape=jax.ShapeDtypeStruct((T_LAT, D_MODEL), f32),
        scratch_shapes=[pltpu.VMEM((BLOCK, D_MODEL), f32),
                        pltpu.VMEM((DEC_SEQ + PAST_LEN, H_C * (QK_NOPE + V_C)), bf16)]
        + [pltpu.VMEM(shape, dt) for shape in ((H_A * BLOCK, 3 * BLOCK + PAST_LEN),
                                               (H_B * BLOCK, NA_ROWS * GRID_W + PAST_LEN),
                                               (H_C * BLOCK, DEC_SEQ + PAST_LEN)) for dt in (f32, bf16)],
        compiler_params=_cparams("arbitrary", "arbitrary"),
        name="lat_attn",
    )(sink, qa, qb, qc, ka, va, kb, vb, ckv, kr, *caches, bias_tab, wukv, wout, x, gate)


def _pick_stream(xc_ref, xl_ref, x_scr):
    i = pl.program_id(0)

    @pl.when(i < T_CTX // TM_TOK)
    def _():
        x_scr[...] = xc_ref[...]

    @pl.when(i >= T_CTX // TM_TOK)
    def _():
        x_scr[...] = xl_ref[...]

    return x_scr[...]


def _stream_specs(lat_row0):
    n_ctx = T_CTX // TM_TOK
    return [pl.BlockSpec((TM_TOK, D_MODEL), lambda i: (jnp.minimum(i, n_ctx - 1), 0)),
            pl.BlockSpec((TM_TOK, D_MODEL), lambda i: (lat_row0 // TM_TOK + jnp.maximum(i - n_ctx, 0), 0))]


def _router_kernel(xc_ref, xl_ref, g_ref, sh_ref, sc_ref, wr_ref, br_ref, h_ref, e_ref, gt_ref, x_scr, t_scr):
    h = _rms(_pick_stream(xc_ref, xl_ref, x_scr), g_ref[...]) * (1.0 + sc_ref[0]) + sh_ref[0]
    _store_row_tiles(t_scr, h)
    h_ref[...] = t_scr[...].astype(bf16)
    logits = jnp.dot(h, wr_ref[...], preferred_element_type=f32, precision=lax.Precision.HIGHEST) + br_ref[...]
    lane = lax.broadcasted_iota(jnp.int32, logits.shape, 1).astype(f32)
    l = jnp.where(lane < N_EXPERTS, logits, -jnp.inf)
    tops, idxs = [], []
    for _ in range(TOP_K):
        m = jnp.max(l, axis=-1, keepdims=True)
        idx = jnp.min(jnp.where(l == m, lane, float(LANE)), axis=-1, keepdims=True)
        tops.append(m)
        idxs.append(idx)
        l = jnp.where(lane == idx, -jnp.inf, l)
    ex = [jnp.exp(t - tops[0]) for t in tops]
    den = ex[0] + ex[1] + ex[2] + ex[3]
    e_out = jnp.zeros(logits.shape, f32)
    g_out = jnp.zeros(logits.shape, f32)
    for k in range(TOP_K):
        e_out = jnp.where(lane == k, idxs[k], e_out)
        g_out = jnp.where(lane == k, ex[k] / den, g_out)
    e_ref[...] = e_out.astype(jnp.int32)
    gt_ref[...] = g_out


def _group_of_tile(i):
    per_b = DEC_SEQ // TM_TOK
    n_ctx = T_CTX // TM_TOK
    return jnp.where(i < n_ctx, 0, 1 + (i - n_ctx) // per_b)


def _router(xc, xl, lat_row0, g, shift, scale, wr, br):
    tm = TM_TOK
    row = lambda i: (i, 0)
    const = lambda i: (0, 0)
    grp = lambda i: (_group_of_tile(i), 0, 0)
    return pl.pallas_call(
        _router_kernel,
        grid=(T_ALL // tm,),
        in_specs=_stream_specs(lat_row0) +
                 [pl.BlockSpec((1, D_MODEL), const),
                  pl.BlockSpec((1, 1, D_MODEL), grp),
                  pl.BlockSpec((1, 1, D_MODEL), grp),
                  pl.BlockSpec((D_MODEL, LANE), const),
                  pl.BlockSpec((1, LANE), const)],
        out_specs=[pl.BlockSpec((tm * ROW_TILE, LANE), row), pl.BlockSpec((tm, LANE), row),
                   pl.BlockSpec((tm, LANE), row)],
        out_shape=[jax.ShapeDtypeStruct((T_ALL * ROW_TILE, LANE), bf16),
                   jax.ShapeDtypeStruct((T_ALL, LANE), jnp.int32),
                   jax.ShapeDtypeStruct((T_ALL, LANE), f32)],
        scratch_shapes=[pltpu.VMEM((tm, D_MODEL), f32), pltpu.VMEM((tm * ROW_TILE, LANE), f32)],
        compiler_params=_cparams("arbitrary"),
        name="router",
    )(xc, xl, g, shift, scale, wr, br)


PAIR_TILE = 2 * ROW_TILE


def _moe_kernel(layer, be_ref, nu_ref, nxt_ref, pair_ref, dst_ref, odd_ref, odd_next_ref, h_hbm, wgu_hbm, bgu_ref,
                wd_hbm, bd_ref, y_hbm, hv, xg, x0, x1, y0, y1, wgu_st, wd_st, wgu_bf, wd_bf, hsem, wsem, ssem):
    tm = TM_MOE
    i = pl.program_id(0)
    nb = pl.num_programs(0)
    used = i < nu_ref[0]
    xb, yb = (x0, x1), (y0, y1)

    def gather(blk, odd_flags, dst):
        for r in range(tm):
            p = pair_ref[blk * tm + r]
            xg[pl.ds(r * PAIR_TILE, PAIR_TILE), :] = hv[pl.ds(pl.multiple_of(p * PAIR_TILE, PAIR_TILE),
                                                              PAIR_TILE), :].astype(f32)
        odd = odd_flags[...] != 0
        dst[...] = jnp.concatenate([jnp.where(odd, xg[pl.ds(ROW_TILE + c, tm, stride=PAIR_TILE), :],
                                              xg[pl.ds(c, tm, stride=PAIR_TILE), :]) for c in range(ROW_TILE)],
                                   axis=1).astype(bf16)

    def out_tile(row):
        return pl.ds(pl.multiple_of(row * ROW_TILE, ROW_TILE), ROW_TILE)

    def scatter_desc(buf, r, dst_row, s):
        return pltpu.make_async_copy(buf.at[out_tile(r)], y_hbm.at[out_tile(dst_row)], ssem.at[s])

    def scatter_wait(s):
        pltpu.make_async_copy(yb[s], y_hbm.at[pl.ds(0, tm * ROW_TILE)], ssem.at[s]).wait()

    def scatter_start(blk, s, unrolled):
        if unrolled:
            for r in range(tm):
                scatter_desc(yb[s], r, dst_ref[(blk + 1) * tm + r], s).start(priority=r % 2)
        else:
            def body(r, carry):
                scatter_desc(yb[s], r, dst_ref[(blk + 1) * tm + r], s).start()
                return carry
            lax.fori_loop(0, tm, body, 0, unroll=8)

    def weight_copies(e):
        return (pltpu.make_async_copy(wgu_hbm.at[layer, e], wgu_st, wsem.at[0]),
                pltpu.make_async_copy(wd_hbm.at[layer, e], wd_st, wsem.at[1]))

    @pl.when(i == 0)
    def _():
        for s in range(2):
            yb[s][...] = jnp.zeros_like(yb[s])
            dummy = pltpu.make_async_copy(yb[s], y_hbm.at[pl.ds((N_ASSIGN + s * tm) * ROW_TILE, tm * ROW_TILE)],
                                          ssem.at[s])
            dummy.start()
            dummy.wait()
        resident = pltpu.make_async_copy(h_hbm, hv, hsem.at[0])
        resident.start()
        for cp in weight_copies(be_ref[0]):
            cp.start()
        resident.wait()
        gather(0, odd_ref, xb[0])

    first = jnp.logical_and(used, jnp.logical_or(i == 0, be_ref[i] != be_ref[jnp.maximum(i - 1, 0)]))

    @pl.when(first)
    def _():
        for cp in weight_copies(0):
            cp.wait()
        wgu_bf[...] = wgu_st[...].astype(bf16)
        wd_bf[...] = wd_st[...].astype(bf16)

        @pl.when(nxt_ref[i] >= 0)
        def _():
            for cp in weight_copies(nxt_ref[i]):
                cp.start()

    def step(par):
        cur, oth = par, 1 - par

        @pl.when(i >= 1)
        def _():
            scatter_wait(cur)

        @pl.when(used)
        def _():
            scatter_start(i - 1, oth, unrolled=True)
            gather(i + 1, odd_next_ref, xb[oth])
            gu = _dot(xb[cur][...], wgu_bf[...]) + bgu_ref[0, 0]
            x_glu = jnp.minimum(gu[:, :D_FF], SWIGLU_LIMIT)
            x_lin = jnp.clip(gu[:, D_FF:], -SWIGLU_LIMIT, SWIGLU_LIMIT)
            act = x_glu * jax.nn.sigmoid(SWIGLU_ALPHA * x_glu) * (x_lin + 1.0)
            _store_row_tiles(yb[cur], _dot(act.astype(bf16), wd_bf[...]) + bd_ref[0, 0])

        @pl.when(jnp.logical_and(jnp.logical_not(used), i + 1 < nb))
        def _():
            scatter_start(i - 1, oth, unrolled=False)

        @pl.when(i == nb - 1)
        def _():
            scatter_start(i - 1, oth, unrolled=False)
            scatter_wait(oth)

    @pl.when(i % 2 == 0)
    def _():
        step(0)

    @pl.when(i % 2 == 1)
    def _():
        step(1)


def _moe(layer, routing, h, w_gu, b_gu, w_down, b_down):
    tm = TM_MOE
    block_e, n_used, nxt_e, row_tok, row_dst = routing
    odd = (row_tok % 2).reshape(-1, 1)
    ex4 = lambda i, be, nu, nxt, pair, dst: (layer, be[i], 0, 0)
    return pl.pallas_call(
        functools.partial(_moe_kernel, layer),
        grid_spec=pltpu.PrefetchScalarGridSpec(
            num_scalar_prefetch=5,
            grid=(N_MOE_BLOCKS,),
            in_specs=[pl.BlockSpec((tm, 1), lambda i, be, nu, nxt, pair, dst: (i, 0)),
                      pl.BlockSpec((tm, 1), lambda i, be, nu, nxt, pair, dst: (jnp.minimum(i + 1, N_MOE_BLOCKS - 1), 0)),
                      pl.BlockSpec(memory_space=pl.ANY),
                      pl.BlockSpec(memory_space=pl.ANY),
                      pl.BlockSpec((1, 1, 1, 2 * D_FF), ex4),
                      pl.BlockSpec(memory_space=pl.ANY),
                      pl.BlockSpec((1, 1, 1, D_MODEL), ex4)],
            out_specs=pl.BlockSpec(memory_space=pl.ANY),
            scratch_shapes=[pltpu.VMEM((T_ALL * ROW_TILE, LANE), bf16), pltpu.VMEM((tm * PAIR_TILE, LANE), f32),
                            pltpu.VMEM((tm, D_MODEL), bf16), pltpu.VMEM((tm, D_MODEL), bf16),
                            pltpu.VMEM((tm * ROW_TILE, LANE), f32), pltpu.VMEM((tm * ROW_TILE, LANE), f32),
                            pltpu.VMEM((D_MODEL, 2 * D_FF), f32), pltpu.VMEM((D_FF, D_MODEL), f32),
                            pltpu.VMEM((D_MODEL, 2 * D_FF), bf16), pltpu.VMEM((D_FF, D_MODEL), bf16),
                            pltpu.SemaphoreType.DMA((1,)), pltpu.SemaphoreType.DMA((2,)),
                            pltpu.SemaphoreType.DMA((2,))]),
        out_shape=jax.ShapeDtypeStruct(((N_ASSIGN + 2 * tm) * ROW_TILE, LANE), f32),
        compiler_params=_cparams("arbitrary"),
        name="moe",
    )(block_e, n_used, nxt_e, row_tok // 2, row_dst, odd, odd, h, w_gu,
      b_gu.reshape(DEPTH, N_EXPERTS, 1, 2 * D_FF), w_down, b_down.reshape(DEPTH, N_EXPERTS, 1, D_MODEL))


def _combine_kernel(final, xc_ref, xl_ref, y0_ref, y1_ref, y2_ref, y3_ref, gt_ref, gate_ref, gf_ref, *rest):
    x_scr = rest[-1]
    gt = gt_ref[...]
    f = gt[:, 0:1] * _load_row_tiles(y0_ref)
    for k, y_ref in ((1, y1_ref), (2, y2_ref), (3, y3_ref)):
        f = f + gt[:, k:k + 1] * _load_row_tiles(y_ref)
    out = _pick_stream(xc_ref, xl_ref, x_scr) + gate_ref[0] * f
    if not final:
        rest[0][...] = out
        return
    out = _rms(out, gf_ref[...])
    oc_ref, ol_ref = rest[0], rest[1]
    i = pl.program_id(0)

    @pl.when(i < T_CTX // TM_TOK)
    def _():
        oc_ref[...] = out

    @pl.when(i >= T_CTX // TM_TOK)
    def _():
        ol_ref[...] = out


def _combine(final, xc, xl, lat_row0, y, gates, gate, g_final):
    tm = TM_TOK
    nt = T_ALL // tm
    n_ctx = T_CTX // tm
    row = lambda i: (i, 0)
    const = lambda i: (0, 0)
    grp = lambda i: (_group_of_tile(i), 0, 0)
    ysel = [pl.BlockSpec((tm * ROW_TILE, LANE), functools.partial(lambda k, i: (k * nt + i, 0), k))
            for k in range(TOP_K)]
    if final:
        out_specs = [pl.BlockSpec((tm, D_MODEL), lambda i: (jnp.minimum(i, n_ctx - 1), 0)),
                     pl.BlockSpec((tm, D_MODEL), lambda i: (jnp.maximum(i - n_ctx, 0), 0))]
        out_shape = [jax.ShapeDtypeStruct((T_CTX, D_MODEL), f32), jax.ShapeDtypeStruct((T_LAT, D_MODEL), f32)]
    else:
        out_specs = pl.BlockSpec((tm, D_MODEL), row)
        out_shape = jax.ShapeDtypeStruct((T_ALL, D_MODEL), f32)
    return pl.pallas_call(
        functools.partial(_combine_kernel, final),
        grid=(nt,),
        in_specs=_stream_specs(lat_row0) + ysel +
                 [pl.BlockSpec((tm, LANE), row),
                  pl.BlockSpec((1, 1, D_MODEL), grp),
                  pl.BlockSpec((1, D_MODEL), const)],
        out_specs=out_specs,
        out_shape=out_shape,
        scratch_shapes=[pltpu.VMEM((tm, D_MODEL), f32)],
        compiler_params=_cparams("arbitrary"),
        name="combine",
    )(xc, xl, y, y, y, y, gates, gate, g_final)


def _rope_head_tables(d):
    nf = d // 4
    half = d // 2
    t = np.arange(DEC_SEQ)
    inv = ROPE_BASE ** (-np.arange(nf, dtype=np.float32) / nf)
    i = np.arange(d)
    pos = np.where(i[None, :] < half, (t // GRID_W)[:, None], (t % GRID_W)[:, None]).astype(np.float32)
    ang = pos * inv[i % nf][None, :].astype(np.float32)
    first = (i % half) < nf
    cos = np.cos(ang)
    sin = np.where(first[None, :], -np.sin(ang), np.sin(ang))
    partner = np.where(first, i + nf, i - nf)
    return cos.astype(np.float32), sin.astype(np.float32), partner


def _rope_tables():
    cos64, sin64, _ = _rope_head_tables(HEAD_DIM)
    cos32, sin32, _ = _rope_head_tables(QK_ROPE)
    cosa = np.tile(cos64, (1, H_A))
    sina = np.tile(sin64, (1, H_A))
    cosq1 = np.concatenate([np.ones((DEC_SEQ, QK_NOPE), np.float32), cos32,
                            np.ones((DEC_SEQ, QC_PAD - QK_NOPE - QK_ROPE), np.float32)], axis=1)
    sinq1 = np.concatenate([np.zeros((DEC_SEQ, QK_NOPE), np.float32), sin32,
                            np.zeros((DEC_SEQ, QC_PAD - QK_NOPE - QK_ROPE), np.float32)], axis=1)
    cosq = np.tile(cosq1, (1, H_C))
    sinq = np.tile(sinq1, (1, H_C))
    return tuple(jnp.asarray(a) for a in (cosa, sina, cosq, sinq, cos32, sin32))


def _pad_cols(w, n):
    return jnp.pad(w, ((0, 0), (0, n - w.shape[1])))


def _layer_weights(w_in, w_uq):
    cuts = np.cumsum((W_QA, W_KA, W_VA, W_B, W_B, W_B, Q_LORA, KV_LORA, QK_ROPE))[:-1]
    qa, ka, va, qb, kb, vb, cq, ckv, kr = jnp.split(w_in, [int(c) for c in cuts], axis=1)
    _, _, p64 = _rope_head_tables(HEAD_DIM)
    _, _, p32 = _rope_head_tables(QK_ROPE)
    pa = np.concatenate([h * HEAD_DIM + p64 for h in range(H_A)])
    base = jnp.concatenate([qa, ka, va, _pad_cols(qb, 384), _pad_cols(kb, 384), _pad_cols(vb, 384), cq, ckv,
                            _pad_cols(kr, 128)], axis=1)
    w_ctx = base.astype(bf16)
    w_lat = jnp.concatenate([base, qa[:, pa], ka[:, pa[:W_KA]], _pad_cols(kr[:, p32], 128)], axis=1).astype(bf16)
    hq = QK_NOPE + QK_ROPE
    heads = [_pad_cols(w_uq[:, h * hq:(h + 1) * hq], QC_PAD) for h in range(H_C)]
    pq = np.concatenate([np.arange(QK_NOPE), QK_NOPE + p32])
    heads_p = [_pad_cols(w_uq[:, h * hq:(h + 1) * hq][:, pq], QC_PAD) for h in range(H_C)]
    wuq = jnp.concatenate(heads, axis=1).astype(bf16)
    wuq2 = jnp.concatenate(heads + heads_p, axis=1).astype(bf16)
    return w_ctx, w_lat, wuq, wuq2


def _bias_table(rpb):
    col = np.arange(GRID_W)
    col_start = np.clip(col - NA_COLS // 2, 0, GRID_W - NA_COLS)
    col_ok = (col[None, :] >= col_start[:, None]) & (col[None, :] < col_start[:, None] + NA_COLS)
    dc = np.clip(col[None, :] - col[:, None] + (NA_COLS - 1), 0, 2 * NA_COLS - 2)
    onehot = (dc[None] == np.arange(2 * NA_COLS - 1)[:, None, None]).astype(np.float32)
    expanded = jnp.einsum('hrd,dqk->hrqk', rpb.astype(f32), jnp.asarray(onehot), precision=lax.Precision.HIGHEST)
    blocks = jnp.where(col_ok[None, None], expanded, NEG)
    return jnp.concatenate([blocks[:, :-1], blocks[:, 1:]], axis=-1)


def _routing(top_e):
    tm = TM_MOE
    key_bits = 16
    pad_mark = (1 << key_bits) - 1
    flat_e = top_e.T.reshape(N_ASSIGN)
    experts = jnp.arange(N_EXPERTS, dtype=jnp.int32)
    counts = jnp.sum((flat_e[:, None] == experts[None, :]).astype(jnp.int32), axis=0)
    nblk = (counts + tm - 1) // tm
    blk_end = jnp.cumsum(nblk)
    pad_end = jnp.cumsum(nblk * tm - counts)
    slots = jnp.arange(N_MOE_BLOCKS * tm - N_ASSIGN, dtype=jnp.int32)
    pad_e = jnp.sum((pad_end[None, :] <= slots[:, None]).astype(jnp.int32), axis=1)
    keys = jnp.concatenate([(flat_e << key_bits) + jnp.arange(N_ASSIGN, dtype=jnp.int32),
                            (pad_e << key_bits) + pad_mark])
    asg = (jnp.sort(keys) & pad_mark).reshape(N_MOE_BLOCKS, tm)
    valid = asg != pad_mark
    blocks = jnp.arange(N_MOE_BLOCKS, dtype=jnp.int32)
    r = jnp.arange(tm, dtype=jnp.int32)[None, :]
    tok = jnp.where(valid, asg % T_ALL, 0)
    row_dst = jnp.where(valid, asg, N_ASSIGN + (blocks[:, None] % 2) * tm + r)
    row_dst = jnp.concatenate([N_ASSIGN + tm + r, row_dst], axis=0).reshape(-1)
    block_e = jnp.minimum(jnp.sum((blk_end[None, :] <= blocks[:, None]).astype(jnp.int32), axis=1), N_EXPERTS - 1)
    n_used = blk_end[-1].astype(jnp.int32).reshape(1)
    has = jnp.where(counts > 0, experts, N_EXPERTS)
    later = experts[None, :] > experts[:, None]
    nxt = jnp.min(jnp.where(later, has[None, :], N_EXPERTS), axis=1)
    nxt = jnp.where(nxt >= N_EXPERTS, -1, nxt)
    sel = (block_e[:, None] == experts[None, :]).astype(jnp.int32)
    nxt_e = jnp.sum(sel * nxt[None, :], axis=1)
    i32 = lambda a: a.astype(jnp.int32)
    return i32(block_e), n_used, i32(nxt_e), i32(tok).reshape(-1), i32(row_dst)


def kernel(x_prompt, x_sample, cache_a_k, cache_a_v, cache_b_k, cache_b_v, cache_c_kv, cache_c_kr, c, c_ctx, w_ada, b_ada, g_attn, g_ffn, w_in, sink_a, rpb_b, g_cq, g_ckv, w_uq, w_ukv, w_out, w_router, b_router, w_gu, b_gu, w_down, b_down, g_final):
    xc, xl, lat_row0 = x_prompt.reshape(T_CTX, D_MODEL), x_sample.reshape(T_LAT, D_MODEL), 0
    cvec = jnp.concatenate([c_ctx[None, :], c, jnp.zeros((8 - N_GROUPS, D_MODEL), f32)], axis=0)
    mods = _ada(cvec, w_ada, b_ada)[:, :N_GROUPS].reshape(DEPTH, N_GROUPS, 6, 1, D_MODEL)
    tabs = _rope_tables()
    caches = (cache_a_k.reshape(DEC_BATCH, DEPTH, PAST_LEN, W_KA), cache_a_v.reshape(DEC_BATCH, DEPTH, PAST_LEN, W_VA),
              cache_b_k.reshape(DEC_BATCH, DEPTH, PAST_LEN, W_B), cache_b_v.reshape(DEC_BATCH, DEPTH, PAST_LEN, W_B),
              cache_c_kv, cache_c_kr)
    new = [[] for _ in range(6)]
    for layer in range(DEPTH):
        m = [mods[layer, :, j] for j in range(6)]
        w_ctx, w_lat, wuq, wuq2 = _layer_weights(w_in[layer], w_uq[layer])
        wukv = w_ukv[layer].astype(bf16)
        wout = w_out[layer].astype(bf16)
        g1 = g_attn[layer][None, :]
        gcq = g_cq[layer][None, :]
        gckv = g_ckv[layer][None, :]
        sink = sink_a[layer]

        pc = _inproj_ctx(xc, g1, m[0], m[1], w_ctx, gcq, gckv, wuq)
        for lst, a in zip(new, (pc[1], pc[2], pc[4], pc[5], pc[7], pc[8])):
            lst.append(a)
        x_ctx = _ctx_attn(sink, pc, wukv, wout, xc, m[2])

        plat = _inproj_lat(xl, lat_row0, g1, m[0], m[1], w_lat, gcq, gckv, wuq2, tabs)
        x_lat = _lat_attn(layer, sink, plat, caches, _bias_table(rpb_b[layer]), wukv, wout, xl, lat_row0, m[2])

        wr = _pad_cols(w_router[layer], LANE)
        br = _pad_cols(b_router[layer][None, :], LANE)
        h2, top_e, gates = _router(x_ctx, x_lat, 0, g_ffn[layer][None, :], m[3], m[4], wr, br)
        y = _moe(layer, _routing(top_e[:, :TOP_K]), h2, w_gu, b_gu, w_down, b_down)
        x = _combine(layer == DEPTH - 1, x_ctx, x_lat, 0, y, gates, m[5], g_final[None, :])
        xc, xl, lat_row0 = x, x, T_CTX

    y_prompt = x[0].reshape(BATCH, SEQ, D_MODEL)
    y_sample = x[1].reshape(DEC_BATCH, DEC_SEQ, D_MODEL)
    shapes = ((KV_A, HEAD_DIM), (KV_A, HEAD_DIM), (H_B, HEAD_DIM), (H_B, HEAD_DIM), (KV_LORA,), (QK_ROPE,))
    outs = [jnp.stack([a.reshape((BATCH, SEQ) + s) for a in lst], axis=1) for lst, s in zip(new, shapes)]
    return (y_prompt, y_sample, *outs)
```

```python
import functools

import numpy as np
import jax
import jax.numpy as jnp
from jax import lax
from jax.experimental import pallas as pl
from jax.experimental.pallas import tpu as pltpu

D_MODEL = 1024
BATCH = 32
SEQ = 256
DEPTH = 2
DEC_BATCH = 2
DEC_SEQ = 1024
PAST_LEN = 512
GRID_W = 64
HEAD_DIM = 64
H_A = 6
KV_A = 2
G_A = H_A // KV_A
WINDOW = 128
BLOCK = 128
H_B = 5
NA_ROWS = 8
NA_COLS = 16
H_C = 5
Q_LORA = 384
KV_LORA = 256
QK_NOPE = 64
QK_ROPE = 32
V_C = 64
N_EXPERTS = 32
TOP_K = 4
D_FF = 1024
SWIGLU_ALPHA = 1.702
SWIGLU_LIMIT = 7.0
ROPE_BASE = 10000.0
EPS = 1e-6
NEG = -1e30

T_CTX = BATCH * SEQ
T_LAT = DEC_BATCH * DEC_SEQ
T_ALL = T_CTX + T_LAT
N_GROUPS = 1 + DEC_BATCH
LANE = 128
QC_PAD = 128
ROWS = DEC_SEQ // GRID_W

W_QA, W_KA, W_VA = H_A * HEAD_DIM, KV_A * HEAD_DIM, KV_A * HEAD_DIM
W_B = H_B * HEAD_DIM
OFF_QA = 0
OFF_KA = 384
OFF_VA = 512
OFF_QB = 640
OFF_KB = 1024
OFF_VB = 1408
OFF_CQ = 1792
OFF_CKV = 2176
OFF_KR = 2432
NW_CTX = 2560
OFF_QA_P = 2560
OFF_KA_P = 2944
OFF_KR_P = 3072
NW_LAT = 3200

TM_TOK = 512
TM_LAT_IN = 512
TM_MOE = 256
N_ASSIGN = T_ALL * TOP_K
N_MOE_BLOCKS = N_ASSIGN // TM_MOE + N_EXPERTS
DISPATCH_BLOCKS = 4
VMEM_LIMIT = 56 * 1024 * 1024

f32 = jnp.float32
bf16 = jnp.bfloat16


def _cparams(*sem):
    return pltpu.CompilerParams(dimension_semantics=sem, vmem_limit_bytes=VMEM_LIMIT)


def _rms(xf, g):
    return xf * lax.rsqrt(jnp.mean(xf * xf, axis=-1, keepdims=True) + EPS) * g


def _dot(a, b):
    return jnp.dot(a, b, preferred_element_type=f32)


def _dot_nt(a, b):
    return lax.dot_general(a, b, (((1,), (1,)), ((), ())), preferred_element_type=f32)


ROW_TILE = D_MODEL // LANE


def _store_row_tiles(ref, val):
    n = val.shape[0]
    for c in range(ROW_TILE):
        ref[pl.ds(c, n, stride=ROW_TILE), :] = val[:, c * LANE:(c + 1) * LANE]


def _load_row_tiles(ref):
    n = ref.shape[0] // ROW_TILE
    return jnp.concatenate([ref[pl.ds(c, n, stride=ROW_TILE), :] for c in range(ROW_TILE)], axis=1)


def _softmax_rows(s_ref, p_ref, rows, sinks=None):
    s = s_ref[rows, :]
    m = jnp.max(s, axis=-1, keepdims=True)
    if sinks is not None:
        sink = jnp.concatenate([jnp.full((n, 1), v, f32) for v, n in sinks], axis=0)
        m = jnp.maximum(m, sink)
    p = jnp.exp(s - m)
    l = jnp.sum(p, axis=-1, keepdims=True)
    if sinks is not None:
        l = l + jnp.exp(sink - m)
    p_ref[rows, :] = (p * (1.0 / l)).astype(bf16)


def _ada_kernel(c_ref, w_ref, b_ref, o_ref):
    c = c_ref[...]
    s = c * jax.nn.sigmoid(c)
    o_ref[0] = jnp.dot(s, w_ref[0], preferred_element_type=f32, precision=lax.Precision.HIGHEST) + b_ref[0]


def _ada(cvec, w_ada, b_ada):
    tn = 1536
    return pl.pallas_call(
        _ada_kernel,
        grid=(DEPTH, 6 * D_MODEL // tn),
        in_specs=[pl.BlockSpec((8, D_MODEL), lambda l, j: (0, 0)),
                  pl.BlockSpec((1, D_MODEL, tn), lambda l, j: (l, 0, j)),
                  pl.BlockSpec((1, 1, tn), lambda l, j: (l, 0, j))],
        out_specs=pl.BlockSpec((1, 8, tn), lambda l, j: (l, 0, j)),
        out_shape=jax.ShapeDtypeStruct((DEPTH, 8, 6 * D_MODEL), f32),
        compiler_params=_cparams("arbitrary", "arbitrary"),
        name="ada",
    )(cvec, w_ada, b_ada.reshape(DEPTH, 1, 6 * D_MODEL))


def _inproj_ctx_kernel(x_ref, g_ref, sh_ref, sc_ref, w_ref, gcq_ref, gckv_ref, wuq_ref,
                       qa_ref, ka_ref, va_ref, qb_ref, kb_ref, vb_ref, qc_ref, ckv_ref, kr_ref):
    h = _rms(x_ref[...], g_ref[...]) * (1.0 + sc_ref[0]) + sh_ref[0]
    p = _dot(h.astype(bf16), w_ref[...])
    qa_ref[...] = p[:, OFF_QA:OFF_QA + W_QA].astype(bf16)
    ka_ref[...] = p[:, OFF_KA:OFF_KA + W_KA]
    va_ref[...] = p[:, OFF_VA:OFF_VA + W_VA]
    qb_ref[...] = p[:, OFF_QB:OFF_QB + W_B].astype(bf16)
    kb_ref[...] = p[:, OFF_KB:OFF_KB + W_B]
    vb_ref[...] = p[:, OFF_VB:OFF_VB + W_B]
    cqn = _rms(p[:, OFF_CQ:OFF_CQ + Q_LORA], gcq_ref[...])
    qc_ref[...] = _dot(cqn.astype(bf16), wuq_ref[...]).astype(bf16)
    ckv_ref[...] = _rms(p[:, OFF_CKV:OFF_CKV + KV_LORA], gckv_ref[...])
    kr_ref[...] = p[:, OFF_KR:OFF_KR + QK_ROPE]


def _inproj_ctx(x, g, shift, scale, w, gcq, gckv, wuq):
    tm = TM_TOK
    row = lambda i: (i, 0)
    const = lambda i: (0, 0)
    widths = (W_QA, W_KA, W_VA, W_B, W_B, W_B, H_C * QC_PAD, KV_LORA, QK_ROPE)
    dtypes = (bf16, f32, f32, bf16, f32, f32, bf16, f32, f32)
    return pl.pallas_call(
        _inproj_ctx_kernel,
        grid=(T_CTX // tm,),
        in_specs=[pl.BlockSpec((tm, D_MODEL), row),
                  pl.BlockSpec((1, D_MODEL), const),
                  pl.BlockSpec((1, 1, D_MODEL), lambda i: (0, 0, 0)),
                  pl.BlockSpec((1, 1, D_MODEL), lambda i: (0, 0, 0)),
                  pl.BlockSpec((D_MODEL, NW_CTX), const),
                  pl.BlockSpec((1, Q_LORA), const),
                  pl.BlockSpec((1, KV_LORA), const),
                  pl.BlockSpec((Q_LORA, H_C * QC_PAD), const)],
        out_specs=[pl.BlockSpec((tm, wd), row) for wd in widths],
        out_shape=[jax.ShapeDtypeStruct((T_CTX, wd), dt) for wd, dt in zip(widths, dtypes)],
        compiler_params=_cparams("arbitrary"),
        name="inproj_ctx",
    )(x, g, shift, scale, w, gcq, gckv, wuq)


def _inproj_lat_kernel(x_ref, g_ref, sh_ref, sc_ref, w_ref, gcq_ref, gckv_ref, wuq_ref,
                       cosa_ref, sina_ref, cosq_ref, sinq_ref, cosr_ref, sinr_ref,
                       qa_ref, ka_ref, va_ref, qb_ref, kb_ref, vb_ref, qc_ref, ckv_ref, kr_ref):
    h = _rms(x_ref[...], g_ref[...]) * (1.0 + sc_ref[0]) + sh_ref[0]
    p = _dot(h.astype(bf16), w_ref[...])
    cosa = cosa_ref[...]
    sina = sina_ref[...]
    qa = p[:, OFF_QA:OFF_QA + W_QA] * cosa + p[:, OFF_QA_P:OFF_QA_P + W_QA] * sina
    ka = p[:, OFF_KA:OFF_KA + W_KA] * cosa[:, :W_KA] + p[:, OFF_KA_P:OFF_KA_P + W_KA] * sina[:, :W_KA]
    kr = p[:, OFF_KR:OFF_KR + QK_ROPE] * cosr_ref[...] + p[:, OFF_KR_P:OFF_KR_P + QK_ROPE] * sinr_ref[...]
    qa_ref[...] = qa.astype(bf16)
    ka_ref[...] = ka.astype(bf16)
    va_ref[...] = p[:, OFF_VA:OFF_VA + W_VA].astype(bf16)
    qb_ref[...] = p[:, OFF_QB:OFF_QB + W_B].astype(bf16)
    kb_ref[...] = p[:, OFF_KB:OFF_KB + W_B].astype(bf16)
    vb_ref[...] = p[:, OFF_VB:OFF_VB + W_B].astype(bf16)
    cqn = _rms(p[:, OFF_CQ:OFF_CQ + Q_LORA], gcq_ref[...])
    q2 = _dot(cqn.astype(bf16), wuq_ref[...])
    nq = H_C * QC_PAD
    qc_ref[...] = (q2[:, :nq] * cosq_ref[...] + q2[:, nq:] * sinq_ref[...]).astype(bf16)
    ckv_ref[...] = _rms(p[:, OFF_CKV:OFF_CKV + KV_LORA], gckv_ref[...]).astype(bf16)
    kr_ref[...] = kr.astype(bf16)


def _inproj_lat(x, lat_row0, g, shift, scale, w, gcq, gckv, wuq2, tabs):
    tm = TM_LAT_IN
    per_b = DEC_SEQ // tm
    row0 = lat_row0 // tm
    xrow = lambda i: (row0 + i, 0)
    row = lambda i: (i, 0)
    const = lambda i: (0, 0)
    grp = lambda i: (1 + i // per_b, 0, 0)
    pos = lambda i: (i % per_b, 0)
    cosa, sina, cosq, sinq, cosr, sinr = tabs
    widths = (W_QA, W_KA, W_VA, W_B, W_B, W_B, H_C * QC_PAD, KV_LORA, QK_ROPE)
    return pl.pallas_call(
        _inproj_lat_kernel,
        grid=(T_LAT // tm,),
        in_specs=[pl.BlockSpec((tm, D_MODEL), xrow),
                  pl.BlockSpec((1, D_MODEL), const),
                  pl.BlockSpec((1, 1, D_MODEL), grp),
                  pl.BlockSpec((1, 1, D_MODEL), grp),
                  pl.BlockSpec((D_MODEL, NW_LAT), const),
                  pl.BlockSpec((1, Q_LORA), const),
                  pl.BlockSpec((1, KV_LORA), const),
                  pl.BlockSpec((Q_LORA, 2 * H_C * QC_PAD), const),
                  pl.BlockSpec((tm, W_QA), pos), pl.BlockSpec((tm, W_QA), pos),
                  pl.BlockSpec((tm, H_C * QC_PAD), pos), pl.BlockSpec((tm, H_C * QC_PAD), pos),
                  pl.BlockSpec((tm, QK_ROPE), pos), pl.BlockSpec((tm, QK_ROPE), pos)],
        out_specs=[pl.BlockSpec((tm, wd), row) for wd in widths],
        out_shape=[jax.ShapeDtypeStruct((T_LAT, wd), bf16) for wd in widths],
        compiler_params=_cparams("arbitrary"),
        name="inproj_lat",
    )(x, g, shift, scale, w, gcq, gckv, wuq2, cosa, sina, cosq, sinq, cosr, sinr)


def _ctx_attn_kernel(sink_ref, qa_ref, ka_ref, va_ref, qb_ref, kb_ref, vb_ref, qc_ref, ckv_ref, kr_ref,
                     wukv_ref, wout_ref, x_ref, gate_ref, o_ref, o_scr, s_scr, p_scr):
    n = SEQ
    scale = HEAD_DIM ** -0.5
    scale_c = (QK_NOPE + QK_ROPE) ** -0.5
    ka = ka_ref[...].astype(bf16)
    va = va_ref[...].astype(bf16)
    kb = kb_ref[...].astype(bf16)
    vb = vb_ref[...].astype(bf16)
    kv = _dot(ckv_ref[...].astype(bf16), wukv_ref[...]).astype(bf16)
    kr = kr_ref[...].astype(bf16)
    for h in range(H_A):
        g = h // G_A
        q = qa_ref[:, h * HEAD_DIM:(h + 1) * HEAD_DIM]
        s_scr[h * n:(h + 1) * n, :] = _dot_nt(q, ka[:, g * HEAD_DIM:(g + 1) * HEAD_DIM]) * scale
    for h in range(H_B):
        sl = slice(h * HEAD_DIM, (h + 1) * HEAD_DIM)
        s_scr[(H_A + h) * n:(H_A + h + 1) * n, :] = _dot_nt(qb_ref[:, sl], kb[:, sl]) * scale
    for h in range(H_C):
        qn = qc_ref[:, h * QC_PAD:h * QC_PAD + QK_NOPE]
        qr = qc_ref[:, h * QC_PAD + QK_NOPE:h * QC_PAD + QK_NOPE + QK_ROPE]
        c0 = h * (QK_NOPE + V_C)
        r0 = (H_A + H_B + h) * n
        s_scr[r0:r0 + n, :] = (_dot_nt(qn, kv[:, c0:c0 + QK_NOPE]) + _dot_nt(qr, kr)) * scale_c
    for pair in range((H_A + H_B + H_C) // 2):
        h0 = 2 * pair
        sinks = ((sink_ref[h0], n), (sink_ref[h0 + 1], n)) if h0 < H_A else None
        _softmax_rows(s_scr, p_scr, slice(h0 * n, (h0 + 2) * n), sinks)
    for h in range(H_A):
        g = h // G_A
        o_scr[:, h * HEAD_DIM:(h + 1) * HEAD_DIM] = _dot(p_scr[h * n:(h + 1) * n, :],
                                                         va[:, g * HEAD_DIM:(g + 1) * HEAD_DIM])
    for h in range(H_B):
        sl = slice(h * HEAD_DIM, (h + 1) * HEAD_DIM)
        o_scr[:, W_QA + h * HEAD_DIM:W_QA + (h + 1) * HEAD_DIM] = _dot(p_scr[(H_A + h) * n:(H_A + h + 1) * n, :],
                                                                     vb[:, sl])
    for h in range(H_C):
        c0 = h * (QK_NOPE + V_C)
        r0 = (H_A + H_B + h) * n
        off = W_QA + W_B + h * V_C
        o_scr[:, off:off + V_C] = _dot(p_scr[r0:r0 + n, :], kv[:, c0 + QK_NOPE:c0 + QK_NOPE + V_C])
    y = _dot(o_scr[...].astype(bf16), wout_ref[...])
    o_ref[...] = x_ref[...] + gate_ref[0] * y


def _ctx_attn(sink, proj, wukv, wout, x, gate):
    qa, ka, va, qb, kb, vb, qc, ckv, kr = proj
    row = lambda b: (b, 0)
    const = lambda b: (0, 0)
    in_specs = [pl.BlockSpec(memory_space=pltpu.SMEM)]
    in_specs += [pl.BlockSpec((SEQ, a.shape[1]), row) for a in proj]
    in_specs += [pl.BlockSpec((KV_LORA, H_C * (QK_NOPE + V_C)), const),
                 pl.BlockSpec((D_MODEL, D_MODEL), const),
                 pl.BlockSpec((SEQ, D_MODEL), row),
                 pl.BlockSpec((1, 1, D_MODEL), lambda b: (0, 0, 0))]
    return pl.pallas_call(
        _ctx_attn_kernel,
        grid=(BATCH,),
        in_specs=in_specs,
        out_specs=pl.BlockSpec((SEQ, D_MODEL), row),
        out_shape=jax.ShapeDtypeStruct((T_CTX, D_MODEL), f32),
        scratch_shapes=[pltpu.VMEM((SEQ, D_MODEL), f32),
                        pltpu.VMEM(((H_A + H_B + H_C) * SEQ, SEQ), f32),
                        pltpu.VMEM(((H_A + H_B + H_C) * SEQ, SEQ), bf16)],
        compiler_params=_cparams("arbitrary"),
        name="ctx_attn",
    )(sink, qa, ka, va, qb, kb, vb, qc, ckv, kr, wukv, wout, x, gate)


def _lat_attn_kernel(sink_ref, qa_ref, qb_ref, qc_ref, ka_ref, va_ref, kb_ref, vb_ref, ckv_ref, kr_ref,
                     cak_ref, cav_ref, cbk_ref, cbv_ref, cckv_ref, ckr_ref, bias_ref,
                     wukv_ref, wout_ref, x_ref, gate_ref, o_ref, o_scr, kv_scr, sa, pa, sb, pb, sc, pc):
    qi = pl.program_id(1)
    nb = DEC_SEQ // BLOCK
    scale = HEAD_DIM ** -0.5

    @pl.when(qi == 0)
    def _():
        kv_scr[0:DEC_SEQ, :] = _dot(ckv_ref[...], wukv_ref[...]).astype(bf16)
        kv_scr[DEC_SEQ:DEC_SEQ + PAST_LEN, :] = _dot(cckv_ref[0, 0].astype(bf16), wukv_ref[...]).astype(bf16)

    def blk(ref, j):
        idx = jnp.clip(qi + j, 0, nb - 1)
        return ref[pl.ds(pl.multiple_of(idx * BLOCK, BLOCK), BLOCK), :]

    ka = jnp.concatenate([blk(ka_ref, -1), blk(ka_ref, 0), blk(ka_ref, 1), cak_ref[0, 0].astype(bf16)], axis=0)
    va = jnp.concatenate([blk(va_ref, -1), blk(va_ref, 0), blk(va_ref, 1), cav_ref[0, 0].astype(bf16)], axis=0)
    nk_a = 3 * BLOCK + PAST_LEN
    r = lax.broadcasted_iota(jnp.int32, (BLOCK, nk_a), 0)
    c = lax.broadcasted_iota(jnp.int32, (BLOCK, nk_a), 1)
    valid = (((c < BLOCK) & (c >= r) & (qi > 0))
             | ((c >= BLOCK) & (c < 2 * BLOCK))
             | ((c >= 2 * BLOCK) & (c < 3 * BLOCK) & (c - 2 * BLOCK <= r) & (qi < nb - 1))
             | (c >= 3 * BLOCK))
    for h in range(H_A):
        g = h // G_A
        q = qa_ref[:, h * HEAD_DIM:(h + 1) * HEAD_DIM]
        s = _dot_nt(q, ka[:, g * HEAD_DIM:(g + 1) * HEAD_DIM]) * scale
        sa[h * BLOCK:(h + 1) * BLOCK, :] = jnp.where(valid, s, NEG)

    cbk = cbk_ref[0, 0].astype(bf16)
    cbv = cbv_ref[0, 0].astype(bf16)
    rows_per_blk = BLOCK // GRID_W
    nloc = NA_ROWS * GRID_W
    vcats = []
    for half in range(rows_per_blk):
        grow = qi * rows_per_blk + half
        start = jnp.clip(grow - NA_ROWS // 2, 0, ROWS - NA_ROWS)
        kloc = kb_ref[pl.ds(pl.multiple_of(start * GRID_W, GRID_W), nloc), :]
        vloc = vb_ref[pl.ds(pl.multiple_of(start * GRID_W, GRID_W), nloc), :]
        vcats.append(jnp.concatenate([vloc, cbv], axis=0))
        qrows = slice(half * GRID_W, (half + 1) * GRID_W)
        dr0 = start - grow + (NA_ROWS - 1)
        for h in range(H_B):
            sl = slice(h * HEAD_DIM, (h + 1) * HEAD_DIM)
            q = qb_ref[qrows, sl]
            bias = jnp.concatenate([bias_ref[h, dr0 + 2 * j] for j in range(NA_ROWS // 2)], axis=1)
            s_loc = _dot_nt(q, kloc[:, sl]) * scale + bias
            s_ctx = _dot_nt(q, cbk[:, sl]) * scale
            r0 = (half * H_B + h) * GRID_W
            sb[r0:r0 + GRID_W, :] = jnp.concatenate([s_loc, s_ctx], axis=1)

    kr = jnp.concatenate([kr_ref[...], ckr_ref[0, 0].astype(bf16)], axis=0)
    scale_c = (QK_NOPE + QK_ROPE) ** -0.5
    for h in range(H_C):
        qn = qc_ref[:, h * QC_PAD:h * QC_PAD + QK_NOPE]
        qr = qc_ref[:, h * QC_PAD + QK_NOPE:h * QC_PAD + QK_NOPE + QK_ROPE]
        c0 = h * (QK_NOPE + V_C)
        sc[h * BLOCK:(h + 1) * BLOCK, :] = (_dot_nt(qn, kv_scr[:, c0:c0 + QK_NOPE]) + _dot_nt(qr, kr)) * scale_c

    for pair in range(H_A // 2):
        h0 = 2 * pair
        _softmax_rows(sa, pa, slice(h0 * BLOCK, (h0 + 2) * BLOCK), ((sink_ref[h0], BLOCK), (sink_ref[h0 + 1], BLOCK)))
    for blk2 in range(rows_per_blk * H_B // 2):
        _softmax_rows(sb, pb, slice(blk2 * 2 * GRID_W, (blk2 + 1) * 2 * GRID_W))
    for h in range(H_C):
        _softmax_rows(sc, pc, slice(h * BLOCK, (h + 1) * BLOCK))

    for h in range(H_A):
        g = h // G_A
        o_scr[:, h * HEAD_DIM:(h + 1) * HEAD_DIM] = _dot(pa[h * BLOCK:(h + 1) * BLOCK, :],
                                                         va[:, g * HEAD_DIM:(g + 1) * HEAD_DIM])
    for half in range(rows_per_blk):
        qrows = slice(half * GRID_W, (half + 1) * GRID_W)
        for h in range(H_B):
            sl = slice(h * HEAD_DIM, (h + 1) * HEAD_DIM)
            r0 = (half * H_B + h) * GRID_W
            o_scr[qrows, W_QA + h * HEAD_DIM:W_QA + (h + 1) * HEAD_DIM] = _dot(pb[r0:r0 + GRID_W, :],
                                                                             vcats[half][:, sl])
    for h in range(H_C):
        c0 = h * (QK_NOPE + V_C)
        off = W_QA + W_B + h * V_C
        o_scr[:, off:off + V_C] = _dot(pc[h * BLOCK:(h + 1) * BLOCK, :], kv_scr[:, c0 + QK_NOPE:c0 + QK_NOPE + V_C])

    y = _dot(o_scr[...].astype(bf16), wout_ref[...])
    o_ref[...] = x_ref[...] + gate_ref[0] * y


def _lat_attn(layer, sink, proj, caches, bias_tab, wukv, wout, x, lat_row0, gate):
    qa, ka, va, qb, kb, vb, qc, ckv, kr = proj
    nb = DEC_SEQ // BLOCK
    qrow = lambda b, q: (b * nb + q, 0)
    xrow = lambda b, q: (lat_row0 // BLOCK + b * nb + q, 0)
    brow = lambda b, q: (b, 0)
    const = lambda b, q: (0, 0)
    cidx = lambda b, q: (b, layer, 0, 0)
    in_specs = [pl.BlockSpec(memory_space=pltpu.SMEM)]
    in_specs += [pl.BlockSpec((BLOCK, a.shape[1]), qrow) for a in (qa, qb, qc)]
    in_specs += [pl.BlockSpec((DEC_SEQ, a.shape[1]), brow) for a in (ka, va, kb, vb, ckv, kr)]
    in_specs += [pl.BlockSpec((1, 1, PAST_LEN, a.shape[3]), cidx) for a in caches]
    in_specs += [pl.BlockSpec(bias_tab.shape, lambda b, q: (0, 0, 0, 0)),
                 pl.BlockSpec((KV_LORA, H_C * (QK_NOPE + V_C)), const),
                 pl.BlockSpec((D_MODEL, D_MODEL), const),
                 pl.BlockSpec((BLOCK, D_MODEL), xrow),
                 pl.BlockSpec((1, 1, D_MODEL), lambda b, q: (1 + b, 0, 0))]
    return pl.pallas_call(
        _lat_attn_kernel,
        grid=(DEC_BATCH, nb),
        in_specs=in_specs,
        out_specs=pl.BlockSpec((BLOCK, D_MODEL), qrow),
        out_shape=jax.ShapeDtypeStruct((T_LAT, D_MODEL), f32),
        scratch_shapes=[pltpu.VMEM((BLOCK, D_MODEL), f32),
                        pltpu.VMEM((DEC_SEQ + PAST_LEN, H_C * (QK_NOPE + V_C)), bf16)]
        + [pltpu.VMEM(shape, dt) for shape in ((H_A * BLOCK, 3 * BLOCK + PAST_LEN),
                                               (H_B * BLOCK, NA_ROWS * GRID_W + PAST_LEN),
                                               (H_C * BLOCK, DEC_SEQ + PAST_LEN)) for dt in (f32, bf16)],
        compiler_params=_cparams("arbitrary", "arbitrary"),
        name="lat_attn",
    )(sink, qa, qb, qc, ka, va, kb, vb, ckv, kr, *caches, bias_tab, wukv, wout, x, gate)


def _pick_stream(xc_ref, xl_ref, x_scr):
    i = pl.program_id(0)

    @pl.when(i < T_CTX // TM_TOK)
    def _():
        x_scr[...] = xc_ref[...]

    @pl.when(i >= T_CTX // TM_TOK)
    def _():
        x_scr[...] = xl_ref[...]

    return x_scr[...]


def _stream_specs(lat_row0):
    n_ctx = T_CTX // TM_TOK
    return [pl.BlockSpec((TM_TOK, D_MODEL), lambda i: (jnp.minimum(i, n_ctx - 1), 0)),
            pl.BlockSpec((TM_TOK, D_MODEL), lambda i: (lat_row0 // TM_TOK + jnp.maximum(i - n_ctx, 0), 0))]


def _router_kernel(xc_ref, xl_ref, g_ref, sh_ref, sc_ref, wr_ref, br_ref, h_ref, e_ref, gt_ref, x_scr):
    h = _rms(_pick_stream(xc_ref, xl_ref, x_scr), g_ref[...]) * (1.0 + sc_ref[0]) + sh_ref[0]
    _store_row_tiles(h_ref, h)
    h_hi = h.astype(bf16)
    h_lo = (h - h_hi.astype(f32)).astype(bf16)
    w = wr_ref[...]
    w_hi = w.astype(bf16)
    w_lo = (w - w_hi.astype(f32)).astype(bf16)
    logits = _dot(h_hi, w_hi) + _dot(h_hi, w_lo) + _dot(h_lo, w_hi) + br_ref[...]
    lane = lax.broadcasted_iota(jnp.int32, logits.shape, 1).astype(f32)
    l = jnp.where(lane < N_EXPERTS, logits, -jnp.inf)
    tops, idxs = [], []
    for _ in range(TOP_K):
        m = jnp.max(l, axis=-1, keepdims=True)
        idx = jnp.min(jnp.where(l == m, lane, float(LANE)), axis=-1, keepdims=True)
        tops.append(m)
        idxs.append(idx)
        l = jnp.where(lane == idx, -jnp.inf, l)
    ex = [jnp.exp(t - tops[0]) for t in tops]
    den = ex[0] + ex[1] + ex[2] + ex[3]
    e_out = jnp.zeros(logits.shape, f32)
    g_out = jnp.zeros(logits.shape, f32)
    for k in range(TOP_K):
        e_out = jnp.where(lane == k, idxs[k], e_out)
        g_out = jnp.where(lane == k, ex[k] / den, g_out)
    e_ref[...] = e_out.astype(jnp.int32)
    gt_ref[...] = g_out


def _group_of_tile(i):
    per_b = DEC_SEQ // TM_TOK
    n_ctx = T_CTX // TM_TOK
    return jnp.where(i < n_ctx, 0, 1 + (i - n_ctx) // per_b)


def _router(xc, xl, lat_row0, g, shift, scale, wr, br):
    tm = TM_TOK
    row = lambda i: (i, 0)
    const = lambda i: (0, 0)
    grp = lambda i: (_group_of_tile(i), 0, 0)
    return pl.pallas_call(
        _router_kernel,
        grid=(T_ALL // tm,),
        in_specs=_stream_specs(lat_row0) +
                 [pl.BlockSpec((1, D_MODEL), const),
                  pl.BlockSpec((1, 1, D_MODEL), grp),
                  pl.BlockSpec((1, 1, D_MODEL), grp),
                  pl.BlockSpec((D_MODEL, LANE), const),
                  pl.BlockSpec((1, LANE), const)],
        out_specs=[pl.BlockSpec((tm * ROW_TILE, LANE), row), pl.BlockSpec((tm, LANE), row),
                   pl.BlockSpec((tm, LANE), row)],
        out_shape=[jax.ShapeDtypeStruct((T_ALL * ROW_TILE, LANE), f32),
                   jax.ShapeDtypeStruct((T_ALL, LANE), jnp.int32),
                   jax.ShapeDtypeStruct((T_ALL, LANE), f32)],
        scratch_shapes=[pltpu.VMEM((tm, D_MODEL), f32)],
        compiler_params=_cparams("arbitrary"),
        name="router",
    )(xc, xl, g, shift, scale, wr, br)


def _dispatch_kernel(tok_ref, nu_ref, h_hbm, o_ref, hv, xg, hsem):
    tm = TM_MOE
    i = pl.program_id(0)

    @pl.when(i == 0)
    def _():
        resident = pltpu.make_async_copy(h_hbm, hv, hsem.at[0])
        resident.start()
        resident.wait()

    def one_block(sub, carry):
        blk = i * DISPATCH_BLOCKS + sub
        rows = pl.ds(pl.multiple_of(sub * tm, tm), tm)

        @pl.when(blk < nu_ref[0])
        def _():
            for r in range(tm):
                t = tok_ref[blk * tm + r]
                xg[pl.ds(r, ROW_TILE, stride=tm + 1), :] = hv[pl.ds(pl.multiple_of(t * ROW_TILE, ROW_TILE),
                                                                 ROW_TILE), :]
            o_ref[rows, :] = jnp.concatenate([xg[pl.ds(c * (tm + 1), tm), :] for c in range(ROW_TILE)],
                                             axis=1).astype(bf16)

        @pl.when(blk >= nu_ref[0])
        def _():
            o_ref[rows, :] = jnp.zeros((tm, D_MODEL), bf16)

        return carry

    lax.fori_loop(0, DISPATCH_BLOCKS, one_block, 0)


def _dispatch(row_tok, n_used, h):
    tm = TM_MOE
    return pl.pallas_call(
        _dispatch_kernel,
        grid_spec=pltpu.PrefetchScalarGridSpec(
            num_scalar_prefetch=2,
            grid=(N_MOE_BLOCKS // DISPATCH_BLOCKS,),
            in_specs=[pl.BlockSpec(memory_space=pl.ANY)],
            out_specs=pl.BlockSpec((DISPATCH_BLOCKS * tm, D_MODEL), lambda i, tok, nu: (i, 0)),
            scratch_shapes=[pltpu.VMEM((T_ALL * ROW_TILE, LANE), f32), pltpu.VMEM(((tm + 1) * ROW_TILE, LANE), f32),
                            pltpu.SemaphoreType.DMA((1,))]),
        out_shape=jax.ShapeDtypeStruct((N_MOE_BLOCKS * tm, D_MODEL), bf16),
        compiler_params=_cparams("arbitrary"),
        name="dispatch",
    )(row_tok, n_used, h)


def _moe_kernel(layer, be_ref, nu_ref, nxt_ref, dst_ref, x_ref, wgu_hbm, bgu_ref, wd_hbm, bd_ref, y_hbm,
                y0, y1, wgu_st, wd_st, wgu_bf, wd_bf, wsem, ssem):
    tm = TM_MOE
    i = pl.program_id(0)
    nb = pl.num_programs(0)
    used = i < nu_ref[0]
    yb = (y0, y1)

    def out_tile(row):
        return pl.ds(pl.multiple_of(row * ROW_TILE, ROW_TILE), ROW_TILE)

    def scatter_desc(buf, r, dst_row, s):
        return pltpu.make_async_copy(buf.at[out_tile(r)], y_hbm.at[out_tile(dst_row)], ssem.at[s])

    def scatter_wait(s):
        pltpu.make_async_copy(yb[s], y_hbm.at[pl.ds(0, tm * ROW_TILE)], ssem.at[s]).wait()

    def scatter_start(blk, s, unrolled):
        if unrolled:
            for r in range(tm):
                scatter_desc(yb[s], r, dst_ref[(blk + 1) * tm + r], s).start(priority=r % 2)
        else:
            def body(r, carry):
                scatter_desc(yb[s], r, dst_ref[(blk + 1) * tm + r], s).start()
                return carry
            lax.fori_loop(0, tm, body, 0, unroll=8)

    def weight_copies(e):
        return (pltpu.make_async_copy(wgu_hbm.at[layer, e], wgu_st, wsem.at[0]),
                pltpu.make_async_copy(wd_hbm.at[layer, e], wd_st, wsem.at[1]))

    @pl.when(i == 0)
    def _():
        for s in range(2):
            yb[s][...] = jnp.zeros_like(yb[s])
            dummy = pltpu.make_async_copy(yb[s], y_hbm.at[pl.ds((N_ASSIGN + s * tm) * ROW_TILE, tm * ROW_TILE)],
                                          ssem.at[s])
            dummy.start()
            dummy.wait()
        for cp in weight_copies(be_ref[0]):
            cp.start()

    first = jnp.logical_and(used, jnp.logical_or(i == 0, be_ref[i] != be_ref[jnp.maximum(i - 1, 0)]))

    @pl.when(first)
    def _():
        for cp in weight_copies(0):
            cp.wait()
        wgu_bf[...] = wgu_st[...].astype(bf16)
        wd_bf[...] = wd_st[...].astype(bf16)

        @pl.when(nxt_ref[i] >= 0)
        def _():
            for cp in weight_copies(nxt_ref[i]):
                cp.start()

    def step(par):
        cur, oth = par, 1 - par

        @pl.when(i >= 1)
        def _():
            scatter_wait(cur)

        @pl.when(used)
        def _():
            scatter_start(i - 1, oth, unrolled=True)
            gu = _dot(x_ref[...], wgu_bf[...]) + bgu_ref[0, 0]
            x_glu = jnp.minimum(gu[:, :D_FF], SWIGLU_LIMIT)
            x_lin = jnp.clip(gu[:, D_FF:], -SWIGLU_LIMIT, SWIGLU_LIMIT)
            act = x_glu * jax.nn.sigmoid(SWIGLU_ALPHA * x_glu) * (x_lin + 1.0)
            _store_row_tiles(yb[cur], _dot(act.astype(bf16), wd_bf[...]) + bd_ref[0, 0])

        @pl.when(jnp.logical_and(jnp.logical_not(used), i + 1 < nb))
        def _():
            scatter_start(i - 1, oth, unrolled=False)

        @pl.when(i == nb - 1)
        def _():
            scatter_start(i - 1, oth, unrolled=False)
            scatter_wait(oth)

    @pl.when(i % 2 == 0)
    def _():
        step(0)

    @pl.when(i % 2 == 1)
    def _():
        step(1)


def _moe(layer, routing, h, w_gu, b_gu, w_down, b_down):
    tm = TM_MOE
    block_e, n_used, nxt_e, row_tok, row_dst = routing
    xs = _dispatch(row_tok, n_used, h)
    ex4 = lambda i, be, nu, nxt, dst: (layer, be[i], 0, 0)
    return pl.pallas_call(
        functools.partial(_moe_kernel, layer),
        grid_spec=pltpu.PrefetchScalarGridSpec(
            num_scalar_prefetch=4,
            grid=(N_MOE_BLOCKS,),
            in_specs=[pl.BlockSpec((tm, D_MODEL), lambda i, be, nu, nxt, dst: (i, 0)),
                      pl.BlockSpec(memory_space=pl.ANY),
                      pl.BlockSpec((1, 1, 1, 2 * D_FF), ex4),
                      pl.BlockSpec(memory_space=pl.ANY),
                      pl.BlockSpec((1, 1, 1, D_MODEL), ex4)],
            out_specs=pl.BlockSpec(memory_space=pl.ANY),
            scratch_shapes=[pltpu.VMEM((tm * ROW_TILE, LANE), f32), pltpu.VMEM((tm * ROW_TILE, LANE), f32),
                            pltpu.VMEM((D_MODEL, 2 * D_FF), f32), pltpu.VMEM((D_FF, D_MODEL), f32),
                            pltpu.VMEM((D_MODEL, 2 * D_FF), bf16), pltpu.VMEM((D_FF, D_MODEL), bf16),
                            pltpu.SemaphoreType.DMA((2,)), pltpu.SemaphoreType.DMA((2,))]),
        out_shape=jax.ShapeDtypeStruct(((N_ASSIGN + 2 * tm) * ROW_TILE, LANE), f32),
        compiler_params=_cparams("arbitrary"),
        name="moe",
    )(block_e, n_used, nxt_e, row_dst, xs, w_gu, b_gu.reshape(DEPTH, N_EXPERTS, 1, 2 * D_FF),
      w_down, b_down.reshape(DEPTH, N_EXPERTS, 1, D_MODEL))


def _combine_kernel(final, xc_ref, xl_ref, y0_ref, y1_ref, y2_ref, y3_ref, gt_ref, gate_ref, gf_ref, *rest):
    x_scr = rest[-1]
    gt = gt_ref[...]
    f = gt[:, 0:1] * _load_row_tiles(y0_ref)
    for k, y_ref in ((1, y1_ref), (2, y2_ref), (3, y3_ref)):
        f = f + gt[:, k:k + 1] * _load_row_tiles(y_ref)
    out = _pick_stream(xc_ref, xl_ref, x_scr) + gate_ref[0] * f
    if not final:
        rest[0][...] = out
        return
    out = _rms(out, gf_ref[...])
    oc_ref, ol_ref = rest[0], rest[1]
    i = pl.program_id(0)

    @pl.when(i < T_CTX // TM_TOK)
    def _():
        oc_ref[...] = out

    @pl.when(i >= T_CTX // TM_TOK)
    def _():
        ol_ref[...] = out


def _combine(final, xc, xl, lat_row0, y, gates, gate, g_final):
    tm = TM_TOK
    nt = T_ALL // tm
    n_ctx = T_CTX // tm
    row = lambda i: (i, 0)
    const = lambda i: (0, 0)
    grp = lambda i: (_group_of_tile(i), 0, 0)
    ysel = [pl.BlockSpec((tm * ROW_TILE, LANE), functools.partial(lambda k, i: (k * nt + i, 0), k))
            for k in range(TOP_K)]
    if final:
        out_specs = [pl.BlockSpec((tm, D_MODEL), lambda i: (jnp.minimum(i, n_ctx - 1), 0)),
                     pl.BlockSpec((tm, D_MODEL), lambda i: (jnp.maximum(i - n_ctx, 0), 0))]
        out_shape = [jax.ShapeDtypeStruct((T_CTX, D_MODEL), f32), jax.ShapeDtypeStruct((T_LAT, D_MODEL), f32)]
    else:
        out_specs = pl.BlockSpec((tm, D_MODEL), row)
        out_shape = jax.ShapeDtypeStruct((T_ALL, D_MODEL), f32)
    return pl.pallas_call(
        functools.partial(_combine_kernel, final),
        grid=(nt,),
        in_specs=_stream_specs(lat_row0) + ysel +
                 [pl.BlockSpec((tm, LANE), row),
                  pl.BlockSpec((1, 1, D_MODEL), grp),
                  pl.BlockSpec((1, D_MODEL), const)],
        out_specs=out_specs,
        out_shape=out_shape,
        scratch_shapes=[pltpu.VMEM((tm, D_MODEL), f32)],
        compiler_params=_cparams("arbitrary"),
        name="combine",
    )(xc, xl, y, y, y, y, gates, gate, g_final)


def _rope_head_tables(d):
    nf = d // 4
    half = d // 2
    t = np.arange(DEC_SEQ)
    inv = ROPE_BASE ** (-np.arange(nf, dtype=np.float32) / nf)
    i = np.arange(d)
    pos = np.where(i[None, :] < half, (t // GRID_W)[:, None], (t % GRID_W)[:, None]).astype(np.float32)
    ang = pos * inv[i % nf][None, :].astype(np.float32)
    first = (i % half) < nf
    cos = np.cos(ang)
    sin = np.where(first[None, :], -np.sin(ang), np.sin(ang))
    partner = np.where(first, i + nf, i - nf)
    return cos.astype(np.float32), sin.astype(np.float32), partner


def _rope_tables():
    cos64, sin64, _ = _rope_head_tables(HEAD_DIM)
    cos32, sin32, _ = _rope_head_tables(QK_ROPE)
    cosa = np.tile(cos64, (1, H_A))
    sina = np.tile(sin64, (1, H_A))
    cosq1 = np.concatenate([np.ones((DEC_SEQ, QK_NOPE), np.float32), cos32,
                            np.ones((DEC_SEQ, QC_PAD - QK_NOPE - QK_ROPE), np.float32)], axis=1)
    sinq1 = np.concatenate([np.zeros((DEC_SEQ, QK_NOPE), np.float32), sin32,
                            np.zeros((DEC_SEQ, QC_PAD - QK_NOPE - QK_ROPE), np.float32)], axis=1)
    cosq = np.tile(cosq1, (1, H_C))
    sinq = np.tile(sinq1, (1, H_C))
    return tuple(jnp.asarray(a) for a in (cosa, sina, cosq, sinq, cos32, sin32))


def _pad_cols(w, n):
    return jnp.pad(w, ((0, 0), (0, n - w.shape[1])))


def _layer_weights(w_in, w_uq):
    cuts = np.cumsum((W_QA, W_KA, W_VA, W_B, W_B, W_B, Q_LORA, KV_LORA, QK_ROPE))[:-1]
    qa, ka, va, qb, kb, vb, cq, ckv, kr = jnp.split(w_in, [int(c) for c in cuts], axis=1)
    _, _, p64 = _rope_head_tables(HEAD_DIM)
    _, _, p32 = _rope_head_tables(QK_ROPE)
    pa = np.concatenate([h * HEAD_DIM + p64 for h in range(H_A)])
    base = jnp.concatenate([qa, ka, va, _pad_cols(qb, 384), _pad_cols(kb, 384), _pad_cols(vb, 384), cq, ckv,
                            _pad_cols(kr, 128)], axis=1)
    w_ctx = base.astype(bf16)
    w_lat = jnp.concatenate([base, qa[:, pa], ka[:, pa[:W_KA]], _pad_cols(kr[:, p32], 128)], axis=1).astype(bf16)
    hq = QK_NOPE + QK_ROPE
    heads = [_pad_cols(w_uq[:, h * hq:(h + 1) * hq], QC_PAD) for h in range(H_C)]
    pq = np.concatenate([np.arange(QK_NOPE), QK_NOPE + p32])
    heads_p = [_pad_cols(w_uq[:, h * hq:(h + 1) * hq][:, pq], QC_PAD) for h in range(H_C)]
    wuq = jnp.concatenate(heads, axis=1).astype(bf16)
    wuq2 = jnp.concatenate(heads + heads_p, axis=1).astype(bf16)
    return w_ctx, w_lat, wuq, wuq2


def _bias_table(rpb):
    col = np.arange(GRID_W)
    col_start = np.clip(col - NA_COLS // 2, 0, GRID_W - NA_COLS)
    col_ok = (col[None, :] >= col_start[:, None]) & (col[None, :] < col_start[:, None] + NA_COLS)
    dc = np.clip(col[None, :] - col[:, None] + (NA_COLS - 1), 0, 2 * NA_COLS - 2)
    onehot = (dc[None] == np.arange(2 * NA_COLS - 1)[:, None, None]).astype(np.float32)
    expanded = jnp.einsum('hrd,dqk->hrqk', rpb.astype(f32), jnp.asarray(onehot), precision=lax.Precision.HIGHEST)
    blocks = jnp.where(col_ok[None, None], expanded, NEG)
    return jnp.concatenate([blocks[:, :-1], blocks[:, 1:]], axis=-1)


def _routing(top_e):
    tm = TM_MOE
    key_bits = 16
    pad_mark = (1 << key_bits) - 1
    flat_e = top_e.T.reshape(N_ASSIGN)
    experts = jnp.arange(N_EXPERTS, dtype=jnp.int32)
    counts = jnp.sum((flat_e[:, None] == experts[None, :]).astype(jnp.int32), axis=0)
    nblk = (counts + tm - 1) // tm
    blk_end = jnp.cumsum(nblk)
    pad_end = jnp.cumsum(nblk * tm - counts)
    slots = jnp.arange(N_MOE_BLOCKS * tm - N_ASSIGN, dtype=jnp.int32)
    pad_e = jnp.sum((pad_end[None, :] <= slots[:, None]).astype(jnp.int32), axis=1)
    keys = jnp.concatenate([(flat_e << key_bits) + jnp.arange(N_ASSIGN, dtype=jnp.int32),
                            (pad_e << key_bits) + pad_mark])
    asg = (jnp.sort(keys) & pad_mark).reshape(N_MOE_BLOCKS, tm)
    valid = asg != pad_mark
    blocks = jnp.arange(N_MOE_BLOCKS, dtype=jnp.int32)
    r = jnp.arange(tm, dtype=jnp.int32)[None, :]
    tok = jnp.where(valid, asg % T_ALL, 0)
    row_dst = jnp.where(valid, asg, N_ASSIGN + (blocks[:, None] % 2) * tm + r)
    row_dst = jnp.concatenate([N_ASSIGN + tm + r, row_dst], axis=0).reshape(-1)
    block_e = jnp.minimum(jnp.sum((blk_end[None, :] <= blocks[:, None]).astype(jnp.int32), axis=1), N_EXPERTS - 1)
    n_used = blk_end[-1].astype(jnp.int32).reshape(1)
    has = jnp.where(counts > 0, experts, N_EXPERTS)
    later = experts[None, :] > experts[:, None]
    nxt = jnp.min(jnp.where(later, has[None, :], N_EXPERTS), axis=1)
    nxt = jnp.where(nxt >= N_EXPERTS, -1, nxt)
    sel = (block_e[:, None] == experts[None, :]).astype(jnp.int32)
    nxt_e = jnp.sum(sel * nxt[None, :], axis=1)
    i32 = lambda a: a.astype(jnp.int32)
    return i32(block_e), n_used, i32(nxt_e), i32(tok).reshape(-1), i32(row_dst)


def kernel(x_prompt, x_sample, cache_a_k, cache_a_v, cache_b_k, cache_b_v, cache_c_kv, cache_c_kr, c, c_ctx, w_ada, b_ada, g_attn, g_ffn, w_in, sink_a, rpb_b, g_cq, g_ckv, w_uq, w_ukv, w_out, w_router, b_router, w_gu, b_gu, w_down, b_down, g_final):
    xc, xl, lat_row0 = x_prompt.reshape(T_CTX, D_MODEL), x_sample.reshape(T_LAT, D_MODEL), 0
    cvec = jnp.concatenate([c_ctx[None, :], c, jnp.zeros((8 - N_GROUPS, D_MODEL), f32)], axis=0)
    mods = _ada(cvec, w_ada, b_ada)[:, :N_GROUPS].reshape(DEPTH, N_GROUPS, 6, 1, D_MODEL)
    tabs = _rope_tables()
    caches = (cache_a_k.reshape(DEC_BATCH, DEPTH, PAST_LEN, W_KA), cache_a_v.reshape(DEC_BATCH, DEPTH, PAST_LEN, W_VA),
              cache_b_k.reshape(DEC_BATCH, DEPTH, PAST_LEN, W_B), cache_b_v.reshape(DEC_BATCH, DEPTH, PAST_LEN, W_B),
              cache_c_kv, cache_c_kr)
    new = [[] for _ in range(6)]
    for layer in range(DEPTH):
        m = [mods[layer, :, j] for j in range(6)]
        w_ctx, w_lat, wuq, wuq2 = _layer_weights(w_in[layer], w_uq[layer])
        wukv = w_ukv[layer].astype(bf16)
        wout = w_out[layer].astype(bf16)
        g1 = g_attn[layer][None, :]
        gcq = g_cq[layer][None, :]
        gckv = g_ckv[layer][None, :]
        sink = sink_a[layer]

        pc = _inproj_ctx(xc, g1, m[0], m[1], w_ctx, gcq, gckv, wuq)
        for lst, a in zip(new, (pc[1], pc[2], pc[4], pc[5], pc[7], pc[8])):
            lst.append(a)
        x_ctx = _ctx_attn(sink, pc, wukv, wout, xc, m[2])

        plat = _inproj_lat(xl, lat_row0, g1, m[0], m[1], w_lat, gcq, gckv, wuq2, tabs)
        x_lat = _lat_attn(layer, sink, plat, caches, _bias_table(rpb_b[layer]), wukv, wout, xl, lat_row0, m[2])

        wr = _pad_cols(w_router[layer], LANE)
        br = _pad_cols(b_router[layer][None, :], LANE)
        h2, top_e, gates = _router(x_ctx, x_lat, 0, g_ffn[layer][None, :], m[3], m[4], wr, br)
        y = _moe(layer, _routing(top_e[:, :TOP_K]), h2, w_gu, b_gu, w_down, b_down)
        x = _combine(layer == DEPTH - 1, x_ctx, x_lat, 0, y, gates, m[5], g_final[None, :])
        xc, xl, lat_row0 = x, x, T_CTX

    y_prompt = x[0].reshape(BATCH, SEQ, D_MODEL)
    y_sample = x[1].reshape(DEC_BATCH, DEC_SEQ, D_MODEL)
    shapes = ((KV_A, HEAD_DIM), (KV_A, HEAD_DIM), (H_B, HEAD_DIM), (H_B, HEAD_DIM), (KV_LORA,), (QK_ROPE,))
    outs = [jnp.stack([a.reshape((BATCH, SEQ) + s) for a in lst], axis=1) for lst, s in zip(new, shapes)]
    return (y_prompt, y_sample, *outs)
```

```python
import functools

import numpy as np
import jax
import jax.numpy as jnp
from jax import lax
from jax.experimental import pallas as pl
from jax.experimental.pallas import tpu as pltpu

D_MODEL = 1024
BATCH = 32
SEQ = 256
DEPTH = 2
DEC_BATCH = 2
DEC_SEQ = 1024
PAST_LEN = 512
GRID_W = 64
HEAD_DIM = 64
H_A = 6
KV_A = 2
G_A = H_A // KV_A
WINDOW = 128
BLOCK = 128
H_B = 5
NA_ROWS = 8
NA_COLS = 16
H_C = 5
Q_LORA = 384
KV_LORA = 256
QK_NOPE = 64
QK_ROPE = 32
V_C = 64
N_EXPERTS = 32
TOP_K = 4
D_FF = 1024
SWIGLU_ALPHA = 1.702
SWIGLU_LIMIT = 7.0
ROPE_BASE = 10000.0
EPS = 1e-6
NEG = -1e30

T_CTX = BATCH * SEQ
T_LAT = DEC_BATCH * DEC_SEQ
T_ALL = T_CTX + T_LAT
N_GROUPS = 1 + DEC_BATCH
LANE = 128
QC_PAD = 128
ROWS = DEC_SEQ // GRID_W

W_QA, W_KA, W_VA = H_A * HEAD_DIM, KV_A * HEAD_DIM, KV_A * HEAD_DIM
W_B = H_B * HEAD_DIM
OFF_QA = 0
OFF_KA = 384
OFF_VA = 512
OFF_QB = 640
OFF_KB = 1024
OFF_VB = 1408
OFF_CQ = 1792
OFF_CKV = 2176
OFF_KR = 2432
NW_CTX = 2560
OFF_QA_P = 2560
OFF_KA_P = 2944
OFF_KR_P = 3072
NW_LAT = 3200

TM_TOK = 512
TM_LAT_IN = 512
TM_MOE = 256
N_ASSIGN = T_ALL * TOP_K
N_MOE_BLOCKS = N_ASSIGN // TM_MOE + N_EXPERTS
DISPATCH_BLOCKS = 8
VMEM_LIMIT = 56 * 1024 * 1024

f32 = jnp.float32
bf16 = jnp.bfloat16


def _cparams(*sem):
    return pltpu.CompilerParams(dimension_semantics=sem, vmem_limit_bytes=VMEM_LIMIT)


def _rms(xf, g):
    return xf * lax.rsqrt(jnp.mean(xf * xf, axis=-1, keepdims=True) + EPS) * g


def _dot(a, b):
    return jnp.dot(a, b, preferred_element_type=f32)


def _dot_nt(a, b):
    return lax.dot_general(a, b, (((1,), (1,)), ((), ())), preferred_element_type=f32)


ROW_TILE = D_MODEL // LANE


def _store_row_tiles(ref, val):
    n = val.shape[0]
    for c in range(ROW_TILE):
        ref[pl.ds(c, n, stride=ROW_TILE), :] = val[:, c * LANE:(c + 1) * LANE]


def _load_row_tiles(ref):
    n = ref.shape[0] // ROW_TILE
    return jnp.concatenate([ref[pl.ds(c, n, stride=ROW_TILE), :] for c in range(ROW_TILE)], axis=1)


def _softmax_rows(s_ref, p_ref, rows, sinks=None):
    s = s_ref[rows, :]
    m = jnp.max(s, axis=-1, keepdims=True)
    if sinks is not None:
        sink = jnp.concatenate([jnp.full((n, 1), v, f32) for v, n in sinks], axis=0)
        m = jnp.maximum(m, sink)
    p = jnp.exp(s - m)
    l = jnp.sum(p, axis=-1, keepdims=True)
    if sinks is not None:
        l = l + jnp.exp(sink - m)
    p_ref[rows, :] = (p * (1.0 / l)).astype(bf16)


def _ada_kernel(c_ref, w_ref, b_ref, o_ref):
    c = c_ref[...]
    s = c * jax.nn.sigmoid(c)
    o_ref[0] = jnp.dot(s, w_ref[0], preferred_element_type=f32, precision=lax.Precision.HIGHEST) + b_ref[0]


def _ada(cvec, w_ada, b_ada):
    tn = 1536
    return pl.pallas_call(
        _ada_kernel,
        grid=(DEPTH, 6 * D_MODEL // tn),
        in_specs=[pl.BlockSpec((8, D_MODEL), lambda l, j: (0, 0)),
                  pl.BlockSpec((1, D_MODEL, tn), lambda l, j: (l, 0, j)),
                  pl.BlockSpec((1, 1, tn), lambda l, j: (l, 0, j))],
        out_specs=pl.BlockSpec((1, 8, tn), lambda l, j: (l, 0, j)),
        out_shape=jax.ShapeDtypeStruct((DEPTH, 8, 6 * D_MODEL), f32),
        compiler_params=_cparams("arbitrary", "arbitrary"),
        name="ada",
    )(cvec, w_ada, b_ada.reshape(DEPTH, 1, 6 * D_MODEL))


CACHE_WIDTHS = (W_KA, W_VA, W_B, W_B, KV_LORA, QK_ROPE)


def _inproj_ctx_kernel(layer, x_ref, g_ref, sh_ref, sc_ref, w_ref, gcq_ref, gckv_ref, wuq_ref, *refs):
    qa_ref, qb_ref, qc_ref, ka_ref, va_ref, kb_ref, vb_ref, ckv_ref, kr_ref = refs[-9:]
    h = _rms(x_ref[...], g_ref[...]) * (1.0 + sc_ref[0]) + sh_ref[0]
    p = _dot(h.astype(bf16), w_ref[...])
    qa_ref[...] = p[:, OFF_QA:OFF_QA + W_QA].astype(bf16)
    qb_ref[...] = p[:, OFF_QB:OFF_QB + W_B].astype(bf16)
    cqn = _rms(p[:, OFF_CQ:OFF_CQ + Q_LORA], gcq_ref[...])
    qc_ref[...] = _dot(cqn.astype(bf16), wuq_ref[...]).astype(bf16)
    caches = ((ka_ref, p[:, OFF_KA:OFF_KA + W_KA]), (va_ref, p[:, OFF_VA:OFF_VA + W_VA]),
              (kb_ref, p[:, OFF_KB:OFF_KB + W_B]), (vb_ref, p[:, OFF_VB:OFF_VB + W_B]),
              (ckv_ref, _rms(p[:, OFF_CKV:OFF_CKV + KV_LORA], gckv_ref[...])),
              (kr_ref, p[:, OFF_KR:OFF_KR + QK_ROPE]))
    for ref, val in caches:
        for b in range(TM_TOK // SEQ):
            rows = val[b * SEQ:(b + 1) * SEQ]
            if layer == 0:
                ref[b, 0] = rows
                for later in range(1, DEPTH):
                    ref[b, later] = jnp.zeros_like(rows)
            else:
                ref[b, 0] = rows


def _inproj_ctx(layer, prev_caches, x, g, shift, scale, w, gcq, gckv, wuq):
    tm = TM_TOK
    nb = tm // SEQ
    row = lambda i: (i, 0)
    const = lambda i: (0, 0)
    in_specs = [pl.BlockSpec((tm, D_MODEL), row),
                pl.BlockSpec((1, D_MODEL), const),
                pl.BlockSpec((1, 1, D_MODEL), lambda i: (0, 0, 0)),
                pl.BlockSpec((1, 1, D_MODEL), lambda i: (0, 0, 0)),
                pl.BlockSpec((D_MODEL, NW_CTX), const),
                pl.BlockSpec((1, Q_LORA), const),
                pl.BlockSpec((1, KV_LORA), const),
                pl.BlockSpec((Q_LORA, H_C * QC_PAD), const)]
    q_widths = (W_QA, W_B, H_C * QC_PAD)
    out_specs = [pl.BlockSpec((tm, wd), row) for wd in q_widths]
    out_shape = [jax.ShapeDtypeStruct((T_CTX, wd), bf16) for wd in q_widths]
    if layer == 0:
        out_specs += [pl.BlockSpec((nb, DEPTH, SEQ, wd), lambda i: (i, 0, 0, 0)) for wd in CACHE_WIDTHS]
        aliases, extra = {}, ()
    else:
        in_specs += [pl.BlockSpec(memory_space=pl.ANY) for _ in CACHE_WIDTHS]
        out_specs += [pl.BlockSpec((nb, 1, SEQ, wd), lambda i: (i, layer, 0, 0)) for wd in CACHE_WIDTHS]
        aliases = {8 + j: len(q_widths) + j for j in range(len(CACHE_WIDTHS))}
        extra = tuple(prev_caches)
    out_shape += [jax.ShapeDtypeStruct((BATCH, DEPTH, SEQ, wd), f32) for wd in CACHE_WIDTHS]
    outs = pl.pallas_call(
        functools.partial(_inproj_ctx_kernel, layer),
        grid=(T_CTX // tm,),
        in_specs=in_specs,
        out_specs=out_specs,
        out_shape=out_shape,
        input_output_aliases=aliases,
        compiler_params=_cparams("arbitrary"),
        name="inproj_ctx",
    )(x, g, shift, scale, w, gcq, gckv, wuq, *extra)
    return outs[:3], outs[3:]


def _inproj_lat_kernel(x_ref, g_ref, sh_ref, sc_ref, w_ref, gcq_ref, gckv_ref, wuq_ref,
                       cosa_ref, sina_ref, cosq_ref, sinq_ref, cosr_ref, sinr_ref,
                       qa_ref, ka_ref, va_ref, qb_ref, kb_ref, vb_ref, qc_ref, ckv_ref, kr_ref):
    h = _rms(x_ref[...], g_ref[...]) * (1.0 + sc_ref[0]) + sh_ref[0]
    p = _dot(h.astype(bf16), w_ref[...])
    cosa = cosa_ref[...]
    sina = sina_ref[...]
    qa = p[:, OFF_QA:OFF_QA + W_QA] * cosa + p[:, OFF_QA_P:OFF_QA_P + W_QA] * sina
    ka = p[:, OFF_KA:OFF_KA + W_KA] * cosa[:, :W_KA] + p[:, OFF_KA_P:OFF_KA_P + W_KA] * sina[:, :W_KA]
    kr = p[:, OFF_KR:OFF_KR + QK_ROPE] * cosr_ref[...] + p[:, OFF_KR_P:OFF_KR_P + QK_ROPE] * sinr_ref[...]
    qa_ref[...] = qa.astype(bf16)
    ka_ref[...] = ka.astype(bf16)
    va_ref[...] = p[:, OFF_VA:OFF_VA + W_VA].astype(bf16)
    qb_ref[...] = p[:, OFF_QB:OFF_QB + W_B].astype(bf16)
    kb_ref[...] = p[:, OFF_KB:OFF_KB + W_B].astype(bf16)
    vb_ref[...] = p[:, OFF_VB:OFF_VB + W_B].astype(bf16)
    cqn = _rms(p[:, OFF_CQ:OFF_CQ + Q_LORA], gcq_ref[...])
    q2 = _dot(cqn.astype(bf16), wuq_ref[...])
    nq = H_C * QC_PAD
    qc_ref[...] = (q2[:, :nq] * cosq_ref[...] + q2[:, nq:] * sinq_ref[...]).astype(bf16)
    ckv_ref[...] = _rms(p[:, OFF_CKV:OFF_CKV + KV_LORA], gckv_ref[...]).astype(bf16)
    kr_ref[...] = kr.astype(bf16)


def _inproj_lat(x, lat_row0, g, shift, scale, w, gcq, gckv, wuq2, tabs):
    tm = TM_LAT_IN
    per_b = DEC_SEQ // tm
    row0 = lat_row0 // tm
    xrow = lambda i: (row0 + i, 0)
    row = lambda i: (i, 0)
    const = lambda i: (0, 0)
    grp = lambda i: (1 + i // per_b, 0, 0)
    pos = lambda i: (i % per_b, 0)
    cosa, sina, cosq, sinq, cosr, sinr = tabs
    widths = (W_QA, W_KA, W_VA, W_B, W_B, W_B, H_C * QC_PAD, KV_LORA, QK_ROPE)
    return pl.pallas_call(
        _inproj_lat_kernel,
        grid=(T_LAT // tm,),
        in_specs=[pl.BlockSpec((tm, D_MODEL), xrow),
                  pl.BlockSpec((1, D_MODEL), const),
                  pl.BlockSpec((1, 1, D_MODEL), grp),
                  pl.BlockSpec((1, 1, D_MODEL), grp),
                  pl.BlockSpec((D_MODEL, NW_LAT), const),
                  pl.BlockSpec((1, Q_LORA), const),
                  pl.BlockSpec((1, KV_LORA), const),
                  pl.BlockSpec((Q_LORA, 2 * H_C * QC_PAD), const),
                  pl.BlockSpec((tm, W_QA), pos), pl.BlockSpec((tm, W_QA), pos),
                  pl.BlockSpec((tm, H_C * QC_PAD), pos), pl.BlockSpec((tm, H_C * QC_PAD), pos),
                  pl.BlockSpec((tm, QK_ROPE), pos), pl.BlockSpec((tm, QK_ROPE), pos)],
        out_specs=[pl.BlockSpec((tm, wd), row) for wd in widths],
        out_shape=[jax.ShapeDtypeStruct((T_LAT, wd), bf16) for wd in widths],
        compiler_params=_cparams("arbitrary"),
        name="inproj_lat",
    )(x, g, shift, scale, w, gcq, gckv, wuq2, cosa, sina, cosq, sinq, cosr, sinr)


def _ctx_attn_kernel(sink_ref, qa_ref, ka_ref, va_ref, qb_ref, kb_ref, vb_ref, qc_ref, ckv_ref, kr_ref,
                     wukv_ref, wout_ref, x_ref, gate_ref, o_ref, o_scr, s_scr, p_scr):
    n = SEQ
    scale = HEAD_DIM ** -0.5
    scale_c = (QK_NOPE + QK_ROPE) ** -0.5
    ka = ka_ref[0, 0].astype(bf16)
    va = va_ref[0, 0].astype(bf16)
    kb = kb_ref[0, 0].astype(bf16)
    vb = vb_ref[0, 0].astype(bf16)
    kv = _dot(ckv_ref[0, 0].astype(bf16), wukv_ref[...]).astype(bf16)
    kr = kr_ref[0, 0].astype(bf16)
    for h in range(H_A):
        g = h // G_A
        q = qa_ref[:, h * HEAD_DIM:(h + 1) * HEAD_DIM]
        s_scr[h * n:(h + 1) * n, :] = _dot_nt(q, ka[:, g * HEAD_DIM:(g + 1) * HEAD_DIM]) * scale
    for h in range(H_B):
        sl = slice(h * HEAD_DIM, (h + 1) * HEAD_DIM)
        s_scr[(H_A + h) * n:(H_A + h + 1) * n, :] = _dot_nt(qb_ref[:, sl], kb[:, sl]) * scale
    for h in range(H_C):
        qn = qc_ref[:, h * QC_PAD:h * QC_PAD + QK_NOPE]
        qr = qc_ref[:, h * QC_PAD + QK_NOPE:h * QC_PAD + QK_NOPE + QK_ROPE]
        c0 = h * (QK_NOPE + V_C)
        r0 = (H_A + H_B + h) * n
        s_scr[r0:r0 + n, :] = (_dot_nt(qn, kv[:, c0:c0 + QK_NOPE]) + _dot_nt(qr, kr)) * scale_c
    for pair in range((H_A + H_B + H_C) // 2):
        h0 = 2 * pair
        sinks = ((sink_ref[h0], n), (sink_ref[h0 + 1], n)) if h0 < H_A else None
        _softmax_rows(s_scr, p_scr, slice(h0 * n, (h0 + 2) * n), sinks)
    for h in range(H_A):
        g = h // G_A
        o_scr[:, h * HEAD_DIM:(h + 1) * HEAD_DIM] = _dot(p_scr[h * n:(h + 1) * n, :],
                                                         va[:, g * HEAD_DIM:(g + 1) * HEAD_DIM])
    for h in range(H_B):
        sl = slice(h * HEAD_DIM, (h + 1) * HEAD_DIM)
        o_scr[:, W_QA + h * HEAD_DIM:W_QA + (h + 1) * HEAD_DIM] = _dot(p_scr[(H_A + h) * n:(H_A + h + 1) * n, :],
                                                                     vb[:, sl])
    for h in range(H_C):
        c0 = h * (QK_NOPE + V_C)
        r0 = (H_A + H_B + h) * n
        off = W_QA + W_B + h * V_C
        o_scr[:, off:off + V_C] = _dot(p_scr[r0:r0 + n, :], kv[:, c0 + QK_NOPE:c0 + QK_NOPE + V_C])
    y = _dot(o_scr[...].astype(bf16), wout_ref[...])
    o_ref[...] = x_ref[...] + gate_ref[0] * y


def _ctx_attn(layer, sink, qs, caches, wukv, wout, x, gate):
    qa, qb, qc = qs
    ka, va, kb, vb, ckv, kr = caches
    row = lambda b: (b, 0)
    const = lambda b: (0, 0)
    slot = lambda b: (b, layer, 0, 0)
    qspec = lambda a: pl.BlockSpec((SEQ, a.shape[1]), row)
    cspec = lambda a: pl.BlockSpec((1, 1, SEQ, a.shape[3]), slot)
    in_specs = [pl.BlockSpec(memory_space=pltpu.SMEM),
                qspec(qa), cspec(ka), cspec(va), qspec(qb), cspec(kb), cspec(vb), qspec(qc), cspec(ckv), cspec(kr)]
    in_specs += [pl.BlockSpec((KV_LORA, H_C * (QK_NOPE + V_C)), const),
                 pl.BlockSpec((D_MODEL, D_MODEL), const),
                 pl.BlockSpec((SEQ, D_MODEL), row),
                 pl.BlockSpec((1, 1, D_MODEL), lambda b: (0, 0, 0))]
    return pl.pallas_call(
        _ctx_attn_kernel,
        grid=(BATCH,),
        in_specs=in_specs,
        out_specs=pl.BlockSpec((SEQ, D_MODEL), row),
        out_shape=jax.ShapeDtypeStruct((T_CTX, D_MODEL), f32),
        scratch_shapes=[pltpu.VMEM((SEQ, D_MODEL), f32),
                        pltpu.VMEM(((H_A + H_B + H_C) * SEQ, SEQ), f32),
                        pltpu.VMEM(((H_A + H_B + H_C) * SEQ, SEQ), bf16)],
        compiler_params=_cparams("arbitrary"),
        name="ctx_attn",
    )(sink, qa, ka, va, qb, kb, vb, qc, ckv, kr, wukv, wout, x, gate)


def _lat_attn_kernel(sink_ref, qa_ref, qb_ref, qc_ref, ka_ref, va_ref, kb_ref, vb_ref, ckv_ref, kr_ref,
                     cak_ref, cav_ref, cbk_ref, cbv_ref, cckv_ref, ckr_ref, bias_ref,
                     wukv_ref, wout_ref, x_ref, gate_ref, o_ref, o_scr, kv_scr, sa, pa, sb, pb, sc, pc):
    qi = pl.program_id(1)
    nb = DEC_SEQ // BLOCK
    scale = HEAD_DIM ** -0.5

    @pl.when(qi == 0)
    def _():
        kv_scr[0:DEC_SEQ, :] = _dot(ckv_ref[...], wukv_ref[...]).astype(bf16)
        kv_scr[DEC_SEQ:DEC_SEQ + PAST_LEN, :] = _dot(cckv_ref[0, 0].astype(bf16), wukv_ref[...]).astype(bf16)

    def blk(ref, j):
        idx = jnp.clip(qi + j, 0, nb - 1)
        return ref[pl.ds(pl.multiple_of(idx * BLOCK, BLOCK), BLOCK), :]

    ka = jnp.concatenate([blk(ka_ref, -1), blk(ka_ref, 0), blk(ka_ref, 1), cak_ref[0, 0].astype(bf16)], axis=0)
    va = jnp.concatenate([blk(va_ref, -1), blk(va_ref, 0), blk(va_ref, 1), cav_ref[0, 0].astype(bf16)], axis=0)
    nk_a = 3 * BLOCK + PAST_LEN
    r = lax.broadcasted_iota(jnp.int32, (BLOCK, nk_a), 0)
    c = lax.broadcasted_iota(jnp.int32, (BLOCK, nk_a), 1)
    valid = (((c < BLOCK) & (c >= r) & (qi > 0))
             | ((c >= BLOCK) & (c < 2 * BLOCK))
             | ((c >= 2 * BLOCK) & (c < 3 * BLOCK) & (c - 2 * BLOCK <= r) & (qi < nb - 1))
             | (c >= 3 * BLOCK))
    for h in range(H_A):
        g = h // G_A
        q = qa_ref[:, h * HEAD_DIM:(h + 1) * HEAD_DIM]
        s = _dot_nt(q, ka[:, g * HEAD_DIM:(g + 1) * HEAD_DIM]) * scale
        sa[h * BLOCK:(h + 1) * BLOCK, :] = jnp.where(valid, s, NEG)

    cbk = cbk_ref[0, 0].astype(bf16)
    cbv = cbv_ref[0, 0].astype(bf16)
    rows_per_blk = BLOCK // GRID_W
    nloc = NA_ROWS * GRID_W
    vcats = []
    for half in range(rows_per_blk):
        grow = qi * rows_per_blk + half
        start = jnp.clip(grow - NA_ROWS // 2, 0, ROWS - NA_ROWS)
        kloc = kb_ref[pl.ds(pl.multiple_of(start * GRID_W, GRID_W), nloc), :]
        vloc = vb_ref[pl.ds(pl.multiple_of(start * GRID_W, GRID_W), nloc), :]
        vcats.append(jnp.concatenate([vloc, cbv], axis=0))
        qrows = slice(half * GRID_W, (half + 1) * GRID_W)
        dr0 = start - grow + (NA_ROWS - 1)
        for h in range(H_B):
            sl = slice(h * HEAD_DIM, (h + 1) * HEAD_DIM)
            q = qb_ref[qrows, sl]
            bias = jnp.concatenate([bias_ref[h, dr0 + 2 * j] for j in range(NA_ROWS // 2)], axis=1)
            s_loc = _dot_nt(q, kloc[:, sl]) * scale + bias
            s_ctx = _dot_nt(q, cbk[:, sl]) * scale
            r0 = (half * H_B + h) * GRID_W
            sb[r0:r0 + GRID_W, :] = jnp.concatenate([s_loc, s_ctx], axis=1)

    kr = jnp.concatenate([kr_ref[...], ckr_ref[0, 0].astype(bf16)], axis=0)
    scale_c = (QK_NOPE + QK_ROPE) ** -0.5
    for h in range(H_C):
        qn = qc_ref[:, h * QC_PAD:h * QC_PAD + QK_NOPE]
        qr = qc_ref[:, h * QC_PAD + QK_NOPE:h * QC_PAD + QK_NOPE + QK_ROPE]
        c0 = h * (QK_NOPE + V_C)
        sc[h * BLOCK:(h + 1) * BLOCK, :] = (_dot_nt(qn, kv_scr[:, c0:c0 + QK_NOPE]) + _dot_nt(qr, kr)) * scale_c

    for pair in range(H_A // 2):
        h0 = 2 * pair
        _softmax_rows(sa, pa, slice(h0 * BLOCK, (h0 + 2) * BLOCK), ((sink_ref[h0], BLOCK), (sink_ref[h0 + 1], BLOCK)))
    for blk2 in range(rows_per_blk * H_B // 2):
        _softmax_rows(sb, pb, slice(blk2 * 2 * GRID_W, (blk2 + 1) * 2 * GRID_W))
    for h in range(H_C):
        _softmax_rows(sc, pc, slice(h * BLOCK, (h + 1) * BLOCK))

    for h in range(H_A):
        g = h // G_A
        o_scr[:, h * HEAD_DIM:(h + 1) * HEAD_DIM] = _dot(pa[h * BLOCK:(h + 1) * BLOCK, :],
                                                         va[:, g * HEAD_DIM:(g + 1) * HEAD_DIM])
    for half in range(rows_per_blk):
        qrows = slice(half * GRID_W, (half + 1) * GRID_W)
        for h in range(H_B):
            sl = slice(h * HEAD_DIM, (h + 1) * HEAD_DIM)
            r0 = (half * H_B + h) * GRID_W
            o_scr[qrows, W_QA + h * HEAD_DIM:W_QA + (h + 1) * HEAD_DIM] = _dot(pb[r0:r0 + GRID_W, :],
                                                                             vcats[half][:, sl])
    for h in range(H_C):
        c0 = h * (QK_NOPE + V_C)
        off = W_QA + W_B + h * V_C
        o_scr[:, off:off + V_C] = _dot(pc[h * BLOCK:(h + 1) * BLOCK, :], kv_scr[:, c0 + QK_NOPE:c0 + QK_NOPE + V_C])

    y = _dot(o_scr[...].astype(bf16), wout_ref[...])
    o_ref[...] = x_ref[...] + gate_ref[0] * y


def _lat_attn(layer, sink, proj, caches, bias_tab, wukv, wout, x, lat_row0, gate):
    qa, ka, va, qb, kb, vb, qc, ckv, kr = proj
    nb = DEC_SEQ // BLOCK
    qrow = lambda b, q: (b * nb + q, 0)
    xrow = lambda b, q: (lat_row0 // BLOCK + b * nb + q, 0)
    brow = lambda b, q: (b, 0)
    const = lambda b, q: (0, 0)
    cidx = lambda b, q: (b, layer, 0, 0)
    in_specs = [pl.BlockSpec(memory_space=pltpu.SMEM)]
    in_specs += [pl.BlockSpec((BLOCK, a.shape[1]), qrow) for a in (qa, qb, qc)]
    in_specs += [pl.BlockSpec((DEC_SEQ, a.shape[1]), brow) for a in (ka, va, kb, vb, ckv, kr)]
    in_specs += [pl.BlockSpec((1, 1, PAST_LEN, a.shape[3]), cidx) for a in caches]
    in_specs += [pl.BlockSpec(bias_tab.shape, lambda b, q: (0, 0, 0, 0)),
                 pl.BlockSpec((KV_LORA, H_C * (QK_NOPE + V_C)), const),
                 pl.BlockSpec((D_MODEL, D_MODEL), const),
                 pl.BlockSpec((BLOCK, D_MODEL), xrow),
                 pl.BlockSpec((1, 1, D_MODEL), lambda b, q: (1 + b, 0, 0))]
    return pl.pallas_call(
        _lat_attn_kernel,
        grid=(DEC_BATCH, nb),
        in_specs=in_specs,
        out_specs=pl.BlockSpec((BLOCK, D_MODEL), qrow),
        out_shape=jax.ShapeDtypeStruct((T_LAT, D_MODEL), f32),
        scratch_shapes=[pltpu.VMEM((BLOCK, D_MODEL), f32),
                        pltpu.VMEM((DEC_SEQ + PAST_LEN, H_C * (QK_NOPE + V_C)), bf16)]
        + [pltpu.VMEM(shape, dt) for shape in ((H_A * BLOCK, 3 * BLOCK + PAST_LEN),
                                               (H_B * BLOCK, NA_ROWS * GRID_W + PAST_LEN),
                                               (H_C * BLOCK, DEC_SEQ + PAST_LEN)) for dt in (f32, bf16)],
        compiler_params=_cparams("arbitrary", "arbitrary"),
        name="lat_attn",
    )(sink, qa, qb, qc, ka, va, kb, vb, ckv, kr, *caches, bias_tab, wukv, wout, x, gate)


def _pick_stream(xc_ref, xl_ref, x_scr):
    i = pl.program_id(0)

    @pl.when(i < T_CTX // TM_TOK)
    def _():
        x_scr[...] = xc_ref[...]

    @pl.when(i >= T_CTX // TM_TOK)
    def _():
        x_scr[...] = xl_ref[...]

    return x_scr[...]


def _stream_specs(lat_row0):
    n_ctx = T_CTX // TM_TOK
    return [pl.BlockSpec((TM_TOK, D_MODEL), lambda i: (jnp.minimum(i, n_ctx - 1), 0)),
            pl.BlockSpec((TM_TOK, D_MODEL), lambda i: (lat_row0 // TM_TOK + jnp.maximum(i - n_ctx, 0), 0))]


def _router_kernel(xc_ref, xl_ref, g_ref, sh_ref, sc_ref, wr_ref, br_ref, h_ref, e_ref, gt_ref, x_scr):
    h = _rms(_pick_stream(xc_ref, xl_ref, x_scr), g_ref[...]) * (1.0 + sc_ref[0]) + sh_ref[0]
    _store_row_tiles(h_ref, h)
    h_hi = h.astype(bf16)
    h_lo = (h - h_hi.astype(f32)).astype(bf16)
    w = wr_ref[...]
    w_hi = w.astype(bf16)
    w_lo = (w - w_hi.astype(f32)).astype(bf16)
    logits = _dot(h_hi, w_hi) + _dot(h_hi, w_lo) + _dot(h_lo, w_hi) + br_ref[...]
    lane = lax.broadcasted_iota(jnp.int32, logits.shape, 1).astype(f32)
    l = jnp.where(lane < N_EXPERTS, logits, -jnp.inf)
    tops, idxs = [], []
    for _ in range(TOP_K):
        m = jnp.max(l, axis=-1, keepdims=True)
        idx = jnp.min(jnp.where(l == m, lane, float(LANE)), axis=-1, keepdims=True)
        tops.append(m)
        idxs.append(idx)
        l = jnp.where(lane == idx, -jnp.inf, l)
    ex = [jnp.exp(t - tops[0]) for t in tops]
    den = ex[0] + ex[1] + ex[2] + ex[3]
    e_out = jnp.zeros(logits.shape, f32)
    g_out = jnp.zeros(logits.shape, f32)
    for k in range(TOP_K):
        e_out = jnp.where(lane == k, idxs[k], e_out)
        g_out = jnp.where(lane == k, ex[k] / den, g_out)
    e_ref[...] = e_out.astype(jnp.int32)
    gt_ref[...] = g_out


def _group_of_tile(i):
    per_b = DEC_SEQ // TM_TOK
    n_ctx = T_CTX // TM_TOK
    return jnp.where(i < n_ctx, 0, 1 + (i - n_ctx) // per_b)


def _router(xc, xl, lat_row0, g, shift, scale, wr, br):
    tm = TM_TOK
    row = lambda i: (i, 0)
    const = lambda i: (0, 0)
    grp = lambda i: (_group_of_tile(i), 0, 0)
    return pl.pallas_call(
        _router_kernel,
        grid=(T_ALL // tm,),
        in_specs=_stream_specs(lat_row0) +
                 [pl.BlockSpec((1, D_MODEL), const),
                  pl.BlockSpec((1, 1, D_MODEL), grp),
                  pl.BlockSpec((1, 1, D_MODEL), grp),
                  pl.BlockSpec((D_MODEL, LANE), const),
                  pl.BlockSpec((1, LANE), const)],
        out_specs=[pl.BlockSpec((tm * ROW_TILE, LANE), row), pl.BlockSpec((tm, LANE), row),
                   pl.BlockSpec((tm, LANE), row)],
        out_shape=[jax.ShapeDtypeStruct((T_ALL * ROW_TILE, LANE), f32),
                   jax.ShapeDtypeStruct((T_ALL, LANE), jnp.int32),
                   jax.ShapeDtypeStruct((T_ALL, LANE), f32)],
        scratch_shapes=[pltpu.VMEM((tm, D_MODEL), f32)],
        compiler_params=_cparams("arbitrary"),
        name="router",
    )(xc, xl, g, shift, scale, wr, br)


def _dispatch_kernel(tok_ref, nu_ref, h_hbm, o_ref, hv, xg, hsem):
    tm = TM_MOE
    i = pl.program_id(0)

    @pl.when(i == 0)
    def _():
        resident = pltpu.make_async_copy(h_hbm, hv, hsem.at[0])
        resident.start()
        resident.wait()

    def one_block(sub, carry):
        blk = i * DISPATCH_BLOCKS + sub
        rows = pl.ds(pl.multiple_of(sub * tm, tm), tm)

        @pl.when(blk < nu_ref[0])
        def _():
            for r in range(tm):
                t = tok_ref[blk * tm + r]
                xg[pl.ds(r, ROW_TILE, stride=tm + 1), :] = hv[pl.ds(pl.multiple_of(t * ROW_TILE, ROW_TILE),
                                                                 ROW_TILE), :]
            o_ref[rows, :] = jnp.concatenate([xg[pl.ds(c * (tm + 1), tm), :] for c in range(ROW_TILE)],
                                             axis=1).astype(bf16)

        @pl.when(blk >= nu_ref[0])
        def _():
            o_ref[rows, :] = jnp.zeros((tm, D_MODEL), bf16)

        return carry

    lax.fori_loop(0, DISPATCH_BLOCKS, one_block, 0)


def _dispatch(row_tok, n_used, h):
    tm = TM_MOE
    return pl.pallas_call(
        _dispatch_kernel,
        grid_spec=pltpu.PrefetchScalarGridSpec(
            num_scalar_prefetch=2,
            grid=(N_MOE_BLOCKS // DISPATCH_BLOCKS,),
            in_specs=[pl.BlockSpec(memory_space=pl.ANY)],
            out_specs=pl.BlockSpec((DISPATCH_BLOCKS * tm, D_MODEL), lambda i, tok, nu: (i, 0)),
            scratch_shapes=[pltpu.VMEM((T_ALL * ROW_TILE, LANE), f32), pltpu.VMEM(((tm + 1) * ROW_TILE, LANE), f32),
                            pltpu.SemaphoreType.DMA((1,))]),
        out_shape=jax.ShapeDtypeStruct((N_MOE_BLOCKS * tm, D_MODEL), bf16),
        compiler_params=_cparams("arbitrary"),
        name="dispatch",
    )(row_tok, n_used, h)


def _moe_kernel(layer, be_ref, nu_ref, nxt_ref, dst_ref, x_ref, wgu_hbm, bgu_ref, wd_hbm, bd_ref, y_hbm,
                y0, y1, wgu_st, wd_st, wgu_bf, wd_bf, wsem, ssem):
    tm = TM_MOE
    i = pl.program_id(0)
    nb = pl.num_programs(0)
    used = i < nu_ref[0]
    yb = (y0, y1)

    def out_tile(row):
        return pl.ds(pl.multiple_of(row * ROW_TILE, ROW_TILE), ROW_TILE)

    def scatter_desc(buf, r, dst_row, s):
        return pltpu.make_async_copy(buf.at[out_tile(r)], y_hbm.at[out_tile(dst_row)], ssem.at[s])

    def scatter_wait(s):
        pltpu.make_async_copy(yb[s], y_hbm.at[pl.ds(0, tm * ROW_TILE)], ssem.at[s]).wait()

    def scatter_start(blk, s, unrolled):
        if unrolled:
            for r in range(tm):
                scatter_desc(yb[s], r, dst_ref[(blk + 1) * tm + r], s).start(priority=r % 2)
        else:
            def body(r, carry):
                scatter_desc(yb[s], r, dst_ref[(blk + 1) * tm + r], s).start()
                return carry
            lax.fori_loop(0, tm, body, 0, unroll=8)

    def weight_copies(e):
        return (pltpu.make_async_copy(wgu_hbm.at[layer, e], wgu_st, wsem.at[0]),
                pltpu.make_async_copy(wd_hbm.at[layer, e], wd_st, wsem.at[1]))

    @pl.when(i == 0)
    def _():
        for s in range(2):
            yb[s][...] = jnp.zeros_like(yb[s])
            dummy = pltpu.make_async_copy(yb[s], y_hbm.at[pl.ds((N_ASSIGN + s * tm) * ROW_TILE, tm * ROW_TILE)],
                                          ssem.at[s])
            dummy.start()
            dummy.wait()
        for cp in weight_copies(be_ref[0]):
            cp.start()

    first = jnp.logical_and(used, jnp.logical_or(i == 0, be_ref[i] != be_ref[jnp.maximum(i - 1, 0)]))

    @pl.when(first)
    def _():
        for cp in weight_copies(0):
            cp.wait()
        wgu_bf[...] = wgu_st[...].astype(bf16)
        wd_bf[...] = wd_st[...].astype(bf16)

        @pl.when(nxt_ref[i] >= 0)
        def _():
            for cp in weight_copies(nxt_ref[i]):
                cp.start()

    def step(par):
        cur, oth = par, 1 - par

        @pl.when(i >= 1)
        def _():
            scatter_wait(cur)

        @pl.when(used)
        def _():
            scatter_start(i - 1, oth, unrolled=True)
            gu = _dot(x_ref[...], wgu_bf[...]) + bgu_ref[0, 0]
            x_glu = jnp.minimum(gu[:, :D_FF], SWIGLU_LIMIT)
            x_lin = jnp.clip(gu[:, D_FF:], -SWIGLU_LIMIT, SWIGLU_LIMIT)
            act = x_glu * jax.nn.sigmoid(SWIGLU_ALPHA * x_glu) * (x_lin + 1.0)
            _store_row_tiles(yb[cur], _dot(act.astype(bf16), wd_bf[...]) + bd_ref[0, 0])

        @pl.when(jnp.logical_and(jnp.logical_not(used), i + 1 < nb))
        def _():
            scatter_start(i - 1, oth, unrolled=False)

        @pl.when(i == nb - 1)
        def _():
            scatter_start(i - 1, oth, unrolled=False)
            scatter_wait(oth)

    @pl.when(i % 2 == 0)
    def _():
        step(0)

    @pl.when(i % 2 == 1)
    def _():
        step(1)


def _moe(layer, routing, h, w_gu, b_gu, w_down, b_down):
    tm = TM_MOE
    block_e, n_used, nxt_e, row_tok, row_dst = routing
    xs = _dispatch(row_tok, n_used, h)
    ex4 = lambda i, be, nu, nxt, dst: (layer, be[i], 0, 0)
    return pl.pallas_call(
        functools.partial(_moe_kernel, layer),
        grid_spec=pltpu.PrefetchScalarGridSpec(
            num_scalar_prefetch=4,
            grid=(N_MOE_BLOCKS,),
            in_specs=[pl.BlockSpec((tm, D_MODEL), lambda i, be, nu, nxt, dst: (i, 0)),
                      pl.BlockSpec(memory_space=pl.ANY),
                      pl.BlockSpec((1, 1, 1, 2 * D_FF), ex4),
                      pl.BlockSpec(memory_space=pl.ANY),
                      pl.BlockSpec((1, 1, 1, D_MODEL), ex4)],
            out_specs=pl.BlockSpec(memory_space=pl.ANY),
            scratch_shapes=[pltpu.VMEM((tm * ROW_TILE, LANE), f32), pltpu.VMEM((tm * ROW_TILE, LANE), f32),
                            pltpu.VMEM((D_MODEL, 2 * D_FF), f32), pltpu.VMEM((D_FF, D_MODEL), f32),
                            pltpu.VMEM((D_MODEL, 2 * D_FF), bf16), pltpu.VMEM((D_FF, D_MODEL), bf16),
                            pltpu.SemaphoreType.DMA((2,)), pltpu.SemaphoreType.DMA((2,))]),
        out_shape=jax.ShapeDtypeStruct(((N_ASSIGN + 2 * tm) * ROW_TILE, LANE), f32),
        compiler_params=_cparams("arbitrary"),
        name="moe",
    )(block_e, n_used, nxt_e, row_dst, xs, w_gu, b_gu.reshape(DEPTH, N_EXPERTS, 1, 2 * D_FF),
      w_down, b_down.reshape(DEPTH, N_EXPERTS, 1, D_MODEL))


def _combine_kernel(final, xc_ref, xl_ref, y0_ref, y1_ref, y2_ref, y3_ref, gt_ref, gate_ref, gf_ref, *rest):
    x_scr = rest[-1]
    gt = gt_ref[...]
    f = gt[:, 0:1] * _load_row_tiles(y0_ref)
    for k, y_ref in ((1, y1_ref), (2, y2_ref), (3, y3_ref)):
        f = f + gt[:, k:k + 1] * _load_row_tiles(y_ref)
    out = _pick_stream(xc_ref, xl_ref, x_scr) + gate_ref[0] * f
    if not final:
        rest[0][...] = out
        return
    out = _rms(out, gf_ref[...])
    oc_ref, ol_ref = rest[0], rest[1]
    i = pl.program_id(0)

    @pl.when(i < T_CTX // TM_TOK)
    def _():
        oc_ref[...] = out

    @pl.when(i >= T_CTX // TM_TOK)
    def _():
        ol_ref[...] = out


def _combine(final, xc, xl, lat_row0, y, gates, gate, g_final):
    tm = TM_TOK
    nt = T_ALL // tm
    n_ctx = T_CTX // tm
    row = lambda i: (i, 0)
    const = lambda i: (0, 0)
    grp = lambda i: (_group_of_tile(i), 0, 0)
    ysel = [pl.BlockSpec((tm * ROW_TILE, LANE), functools.partial(lambda k, i: (k * nt + i, 0), k))
            for k in range(TOP_K)]
    if final:
        out_specs = [pl.BlockSpec((tm, D_MODEL), lambda i: (jnp.minimum(i, n_ctx - 1), 0)),
                     pl.BlockSpec((tm, D_MODEL), lambda i: (jnp.maximum(i - n_ctx, 0), 0))]
        out_shape = [jax.ShapeDtypeStruct((T_CTX, D_MODEL), f32), jax.ShapeDtypeStruct((T_LAT, D_MODEL), f32)]
    else:
        out_specs = pl.BlockSpec((tm, D_MODEL), row)
        out_shape = jax.ShapeDtypeStruct((T_ALL, D_MODEL), f32)
    return pl.pallas_call(
        functools.partial(_combine_kernel, final),
        grid=(nt,),
        in_specs=_stream_specs(lat_row0) + ysel +
                 [pl.BlockSpec((tm, LANE), row),
                  pl.BlockSpec((1, 1, D_MODEL), grp),
                  pl.BlockSpec((1, D_MODEL), const)],
        out_specs=out_specs,
        out_shape=out_shape,
        scratch_shapes=[pltpu.VMEM((tm, D_MODEL), f32)],
        compiler_params=_cparams("arbitrary"),
        name="combine",
    )(xc, xl, y, y, y, y, gates, gate, g_final)


def _rope_head_tables(d):
    nf = d // 4
    half = d // 2
    t = np.arange(DEC_SEQ)
    inv = ROPE_BASE ** (-np.arange(nf, dtype=np.float32) / nf)
    i = np.arange(d)
    pos = np.where(i[None, :] < half, (t // GRID_W)[:, None], (t % GRID_W)[:, None]).astype(np.float32)
    ang = pos * inv[i % nf][None, :].astype(np.float32)
    first = (i % half) < nf
    cos = np.cos(ang)
    sin = np.where(first[None, :], -np.sin(ang), np.sin(ang))
    partner = np.where(first, i + nf, i - nf)
    return cos.astype(np.float32), sin.astype(np.float32), partner


def _rope_tables():
    cos64, sin64, _ = _rope_head_tables(HEAD_DIM)
    cos32, sin32, _ = _rope_head_tables(QK_ROPE)
    cosa = np.tile(cos64, (1, H_A))
    sina = np.tile(sin64, (1, H_A))
    cosq1 = np.concatenate([np.ones((DEC_SEQ, QK_NOPE), np.float32), cos32,
                            np.ones((DEC_SEQ, QC_PAD - QK_NOPE - QK_ROPE), np.float32)], axis=1)
    sinq1 = np.concatenate([np.zeros((DEC_SEQ, QK_NOPE), np.float32), sin32,
                            np.zeros((DEC_SEQ, QC_PAD - QK_NOPE - QK_ROPE), np.float32)], axis=1)
    cosq = np.tile(cosq1, (1, H_C))
    sinq = np.tile(sinq1, (1, H_C))
    return tuple(jnp.asarray(a) for a in (cosa, sina, cosq, sinq, cos32, sin32))


def _pad_cols(w, n):
    return jnp.pad(w, ((0, 0), (0, n - w.shape[1])))


def _layer_weights(w_in, w_uq):
    cuts = np.cumsum((W_QA, W_KA, W_VA, W_B, W_B, W_B, Q_LORA, KV_LORA, QK_ROPE))[:-1]
    qa, ka, va, qb, kb, vb, cq, ckv, kr = jnp.split(w_in, [int(c) for c in cuts], axis=1)
    _, _, p64 = _rope_head_tables(HEAD_DIM)
    _, _, p32 = _rope_head_tables(QK_ROPE)
    pa = np.concatenate([h * HEAD_DIM + p64 for h in range(H_A)])
    base = jnp.concatenate([qa, ka, va, _pad_cols(qb, 384), _pad_cols(kb, 384), _pad_cols(vb, 384), cq, ckv,
                            _pad_cols(kr, 128)], axis=1)
    w_ctx = base.astype(bf16)
    w_lat = jnp.concatenate([base, qa[:, pa], ka[:, pa[:W_KA]], _pad_cols(kr[:, p32], 128)], axis=1).astype(bf16)
    hq = QK_NOPE + QK_ROPE
    heads = [_pad_cols(w_uq[:, h * hq:(h + 1) * hq], QC_PAD) for h in range(H_C)]
    pq = np.concatenate([np.arange(QK_NOPE), QK_NOPE + p32])
    heads_p = [_pad_cols(w_uq[:, h * hq:(h + 1) * hq][:, pq], QC_PAD) for h in range(H_C)]
    wuq = jnp.concatenate(heads, axis=1).astype(bf16)
    wuq2 = jnp.concatenate(heads + heads_p, axis=1).astype(bf16)
    return w_ctx, w_lat, wuq, wuq2


def _bias_table(rpb):
    col = np.arange(GRID_W)
    col_start = np.clip(col - NA_COLS // 2, 0, GRID_W - NA_COLS)
    col_ok = (col[None, :] >= col_start[:, None]) & (col[None, :] < col_start[:, None] + NA_COLS)
    dc = np.clip(col[None, :] - col[:, None] + (NA_COLS - 1), 0, 2 * NA_COLS - 2)
    onehot = (dc[None] == np.arange(2 * NA_COLS - 1)[:, None, None]).astype(np.float32)
    expanded = jnp.einsum('hrd,dqk->hrqk', rpb.astype(f32), jnp.asarray(onehot), precision=lax.Precision.HIGHEST)
    blocks = jnp.where(col_ok[None, None], expanded, NEG)
    return jnp.concatenate([blocks[:, :-1], blocks[:, 1:]], axis=-1)


def _routing(top_e):
    tm = TM_MOE
    key_bits = 16
    pad_mark = (1 << key_bits) - 1
    flat_e = top_e.T.reshape(N_ASSIGN)
    experts = jnp.arange(N_EXPERTS, dtype=jnp.int32)
    counts = jnp.sum((flat_e[:, None] == experts[None, :]).astype(jnp.int32), axis=0)
    nblk = (counts + tm - 1) // tm
    blk_end = jnp.cumsum(nblk)
    pad_end = jnp.cumsum(nblk * tm - counts)
    slots = jnp.arange(N_MOE_BLOCKS * tm - N_ASSIGN, dtype=jnp.int32)
    pad_e = jnp.sum((pad_end[None, :] <= slots[:, None]).astype(jnp.int32), axis=1)
    keys = jnp.concatenate([(flat_e << key_bits) + jnp.arange(N_ASSIGN, dtype=jnp.int32),
                            (pad_e << key_bits) + pad_mark])
    asg = (jnp.sort(keys) & pad_mark).reshape(N_MOE_BLOCKS, tm)
    valid = asg != pad_mark
    blocks = jnp.arange(N_MOE_BLOCKS, dtype=jnp.int32)
    r = jnp.arange(tm, dtype=jnp.int32)[None, :]
    tok = jnp.where(valid, asg % T_ALL, 0)
    row_dst = jnp.where(valid, asg, N_ASSIGN + (blocks[:, None] % 2) * tm + r)
    row_dst = jnp.concatenate([N_ASSIGN + tm + r, row_dst], axis=0).reshape(-1)
    block_e = jnp.minimum(jnp.sum((blk_end[None, :] <= blocks[:, None]).astype(jnp.int32), axis=1), N_EXPERTS - 1)
    n_used = blk_end[-1].astype(jnp.int32).reshape(1)
    has = jnp.where(counts > 0, experts, N_EXPERTS)
    later = experts[None, :] > experts[:, None]
    nxt = jnp.min(jnp.where(later, has[None, :], N_EXPERTS), axis=1)
    nxt = jnp.where(nxt >= N_EXPERTS, -1, nxt)
    sel = (block_e[:, None] == experts[None, :]).astype(jnp.int32)
    nxt_e = jnp.sum(sel * nxt[None, :], axis=1)
    i32 = lambda a: a.astype(jnp.int32)
    return i32(block_e), n_used, i32(nxt_e), i32(tok).reshape(-1), i32(row_dst)


def kernel(x_prompt, x_sample, cache_a_k, cache_a_v, cache_b_k, cache_b_v, cache_c_kv, cache_c_kr, c, c_ctx, w_ada, b_ada, g_attn, g_ffn, w_in, sink_a, rpb_b, g_cq, g_ckv, w_uq, w_ukv, w_out, w_router, b_router, w_gu, b_gu, w_down, b_down, g_final):
    xc, xl, lat_row0 = x_prompt.reshape(T_CTX, D_MODEL), x_sample.reshape(T_LAT, D_MODEL), 0
    cvec = jnp.concatenate([c_ctx[None, :], c, jnp.zeros((8 - N_GROUPS, D_MODEL), f32)], axis=0)
    mods = _ada(cvec, w_ada, b_ada)[:, :N_GROUPS].reshape(DEPTH, N_GROUPS, 6, 1, D_MODEL)
    tabs = _rope_tables()
    caches = (cache_a_k.reshape(DEC_BATCH, DEPTH, PAST_LEN, W_KA), cache_a_v.reshape(DEC_BATCH, DEPTH, PAST_LEN, W_VA),
              cache_b_k.reshape(DEC_BATCH, DEPTH, PAST_LEN, W_B), cache_b_v.reshape(DEC_BATCH, DEPTH, PAST_LEN, W_B),
              cache_c_kv, cache_c_kr)
    new = None
    for layer in range(DEPTH):
        m = [mods[layer, :, j] for j in range(6)]
        w_ctx, w_lat, wuq, wuq2 = _layer_weights(w_in[layer], w_uq[layer])
        wukv = w_ukv[layer].astype(bf16)
        wout = w_out[layer].astype(bf16)
        g1 = g_attn[layer][None, :]
        gcq = g_cq[layer][None, :]
        gckv = g_ckv[layer][None, :]
        sink = sink_a[layer]

        qs, new = _inproj_ctx(layer, new, xc, g1, m[0], m[1], w_ctx, gcq, gckv, wuq)
        x_ctx = _ctx_attn(layer, sink, qs, new, wukv, wout, xc, m[2])

        plat = _inproj_lat(xl, lat_row0, g1, m[0], m[1], w_lat, gcq, gckv, wuq2, tabs)
        x_lat = _lat_attn(layer, sink, plat, caches, _bias_table(rpb_b[layer]), wukv, wout, xl, lat_row0, m[2])

        wr = _pad_cols(w_router[layer], LANE)
        br = _pad_cols(b_router[layer][None, :], LANE)
        h2, top_e, gates = _router(x_ctx, x_lat, 0, g_ffn[layer][None, :], m[3], m[4], wr, br)
        y = _moe(layer, _routing(top_e[:, :TOP_K]), h2, w_gu, b_gu, w_down, b_down)
        x = _combine(layer == DEPTH - 1, x_ctx, x_lat, 0, y, gates, m[5], g_final[None, :])
        xc, xl, lat_row0 = x, x, T_CTX

    y_prompt = x[0].reshape(BATCH, SEQ, D_MODEL)
    y_sample = x[1].reshape(DEC_BATCH, DEC_SEQ, D_MODEL)
    shapes = ((KV_A, HEAD_DIM), (KV_A, HEAD_DIM), (H_B, HEAD_DIM), (H_B, HEAD_DIM), (KV_LORA,), (QK_ROPE,))
    outs = [a.reshape((BATCH, DEPTH, SEQ) + s) for a, s in zip(new, shapes)]
    return (y_prompt, y_sample, *outs)
```

```python
import functools

import numpy as np
import jax
import jax.numpy as jnp
from jax import lax
from jax.experimental import pallas as pl
from jax.experimental.pallas import tpu as pltpu

D_MODEL = 1024
BATCH = 32
SEQ = 256
DEPTH = 2
DEC_BATCH = 2
DEC_SEQ = 1024
PAST_LEN = 512
GRID_W = 64
HEAD_DIM = 64
H_A = 6
KV_A = 2
G_A = H_A // KV_A
WINDOW = 128
BLOCK = 128
H_B = 5
NA_ROWS = 8
NA_COLS = 16
H_C = 5
Q_LORA = 384
KV_LORA = 256
QK_NOPE = 64
QK_ROPE = 32
V_C = 64
N_EXPERTS = 32
TOP_K = 4
D_FF = 1024
SWIGLU_ALPHA = 1.702
SWIGLU_LIMIT = 7.0
ROPE_BASE = 10000.0
EPS = 1e-6
NEG = -1e30

T_CTX = BATCH * SEQ
T_LAT = DEC_BATCH * DEC_SEQ
T_ALL = T_CTX + T_LAT
N_GROUPS = 1 + DEC_BATCH
LANE = 128
QC_PAD = 128
ROWS = DEC_SEQ // GRID_W

W_QA, W_KA, W_VA = H_A * HEAD_DIM, KV_A * HEAD_DIM, KV_A * HEAD_DIM
W_B = H_B * HEAD_DIM
OFF_QA = 0
OFF_KA = 384
OFF_VA = 512
OFF_QB = 640
OFF_KB = 1024
OFF_VB = 1408
OFF_CQ = 1792
OFF_CKV = 2176
OFF_KR = 2432
NW_CTX = 2560
OFF_QA_P = 2560
OFF_KA_P = 2944
OFF_KR_P = 3072
NW_LAT = 3200

TM_TOK = 512
TM_LAT_IN = 512
TM_MOE = 256
N_ASSIGN = T_ALL * TOP_K
N_MOE_BLOCKS = N_ASSIGN // TM_MOE + N_EXPERTS
DISPATCH_BLOCKS = 8
VMEM_LIMIT = 56 * 1024 * 1024

f32 = jnp.float32
bf16 = jnp.bfloat16


def _cparams(*sem):
    return pltpu.CompilerParams(dimension_semantics=sem, vmem_limit_bytes=VMEM_LIMIT)


def _rms(xf, g):
    return xf * lax.rsqrt(jnp.mean(xf * xf, axis=-1, keepdims=True) + EPS) * g


def _dot(a, b):
    return jnp.dot(a, b, preferred_element_type=f32)


def _dot_nt(a, b):
    return lax.dot_general(a, b, (((1,), (1,)), ((), ())), preferred_element_type=f32)


ROW_TILE = D_MODEL // LANE


def _store_row_tiles(ref, val):
    n = val.shape[0]
    for c in range(ROW_TILE):
        ref[pl.ds(c, n, stride=ROW_TILE), :] = val[:, c * LANE:(c + 1) * LANE]


def _load_row_tiles(ref):
    n = ref.shape[0] // ROW_TILE
    return jnp.concatenate([ref[pl.ds(c, n, stride=ROW_TILE), :] for c in range(ROW_TILE)], axis=1)


def _softmax_rows(s_ref, p_ref, rows, sinks=None):
    s = s_ref[rows, :]
    m = jnp.max(s, axis=-1, keepdims=True)
    if sinks is not None:
        sink = jnp.concatenate([jnp.full((n, 1), v, f32) for v, n in sinks], axis=0)
        m = jnp.maximum(m, sink)
    p = jnp.exp(s - m)
    l = jnp.sum(p, axis=-1, keepdims=True)
    if sinks is not None:
        l = l + jnp.exp(sink - m)
    p_ref[rows, :] = (p * (1.0 / l)).astype(bf16)


def _ada_kernel(c_ref, w_ref, b_ref, o_ref):
    c = c_ref[...]
    s = c * jax.nn.sigmoid(c)
    o_ref[0] = jnp.dot(s, w_ref[0], preferred_element_type=f32, precision=lax.Precision.HIGHEST) + b_ref[0]


def _ada(cvec, w_ada, b_ada):
    tn = 1536
    return pl.pallas_call(
        _ada_kernel,
        grid=(DEPTH, 6 * D_MODEL // tn),
        in_specs=[pl.BlockSpec((8, D_MODEL), lambda l, j: (0, 0)),
                  pl.BlockSpec((1, D_MODEL, tn), lambda l, j: (l, 0, j)),
                  pl.BlockSpec((1, 1, tn), lambda l, j: (l, 0, j))],
        out_specs=pl.BlockSpec((1, 8, tn), lambda l, j: (l, 0, j)),
        out_shape=jax.ShapeDtypeStruct((DEPTH, 8, 6 * D_MODEL), f32),
        compiler_params=_cparams("arbitrary", "arbitrary"),
        name="ada",
    )(cvec, w_ada, b_ada.reshape(DEPTH, 1, 6 * D_MODEL))


CACHE_WIDTHS = (W_KA, W_VA, W_B, W_B, KV_LORA, QK_ROPE)


def _inproj_ctx_kernel(layer, x_ref, g_ref, sh_ref, sc_ref, w_ref, gcq_ref, gckv_ref, wuq_ref, *refs):
    qa_ref, qb_ref, qc_ref, ka_ref, va_ref, kb_ref, vb_ref, ckv_ref, kr_ref = refs[-9:]
    h = _rms(x_ref[...], g_ref[...]) * (1.0 + sc_ref[0]) + sh_ref[0]
    p = _dot(h.astype(bf16), w_ref[...])
    qa_ref[...] = p[:, OFF_QA:OFF_QA + W_QA].astype(bf16)
    qb_ref[...] = p[:, OFF_QB:OFF_QB + W_B].astype(bf16)
    cqn = _rms(p[:, OFF_CQ:OFF_CQ + Q_LORA], gcq_ref[...])
    qc_ref[...] = _dot(cqn.astype(bf16), wuq_ref[...]).astype(bf16)
    caches = ((ka_ref, p[:, OFF_KA:OFF_KA + W_KA]), (va_ref, p[:, OFF_VA:OFF_VA + W_VA]),
              (kb_ref, p[:, OFF_KB:OFF_KB + W_B]), (vb_ref, p[:, OFF_VB:OFF_VB + W_B]),
              (ckv_ref, _rms(p[:, OFF_CKV:OFF_CKV + KV_LORA], gckv_ref[...])),
              (kr_ref, p[:, OFF_KR:OFF_KR + QK_ROPE]))
    for ref, val in caches:
        for b in range(TM_TOK // SEQ):
            rows = val[b * SEQ:(b + 1) * SEQ]
            if layer == 0:
                ref[b, 0] = rows
                for later in range(1, DEPTH):
                    ref[b, later] = jnp.zeros_like(rows)
            else:
                ref[b, 0] = rows


def _inproj_ctx(layer, prev_caches, x, g, shift, scale, w, gcq, gckv, wuq):
    tm = TM_TOK
    nb = tm // SEQ
    row = lambda i: (i, 0)
    const = lambda i: (0, 0)
    in_specs = [pl.BlockSpec((tm, D_MODEL), row),
                pl.BlockSpec((1, D_MODEL), const),
                pl.BlockSpec((1, 1, D_MODEL), lambda i: (0, 0, 0)),
                pl.BlockSpec((1, 1, D_MODEL), lambda i: (0, 0, 0)),
                pl.BlockSpec((D_MODEL, NW_CTX), const),
                pl.BlockSpec((1, Q_LORA), const),
                pl.BlockSpec((1, KV_LORA), const),
                pl.BlockSpec((Q_LORA, H_C * QC_PAD), const)]
    q_widths = (W_QA, W_B, H_C * QC_PAD)
    out_specs = [pl.BlockSpec((tm, wd), row) for wd in q_widths]
    out_shape = [jax.ShapeDtypeStruct((T_CTX, wd), bf16) for wd in q_widths]
    if layer == 0:
        out_specs += [pl.BlockSpec((nb, DEPTH, SEQ, wd), lambda i: (i, 0, 0, 0)) for wd in CACHE_WIDTHS]
        aliases, extra = {}, ()
    else:
        in_specs += [pl.BlockSpec(memory_space=pl.ANY) for _ in CACHE_WIDTHS]
        out_specs += [pl.BlockSpec((nb, 1, SEQ, wd), lambda i: (i, layer, 0, 0)) for wd in CACHE_WIDTHS]
        aliases = {8 + j: len(q_widths) + j for j in range(len(CACHE_WIDTHS))}
        extra = tuple(prev_caches)
    out_shape += [jax.ShapeDtypeStruct((BATCH, DEPTH, SEQ, wd), f32) for wd in CACHE_WIDTHS]
    outs = pl.pallas_call(
        functools.partial(_inproj_ctx_kernel, layer),
        grid=(T_CTX // tm,),
        in_specs=in_specs,
        out_specs=out_specs,
        out_shape=out_shape,
        input_output_aliases=aliases,
        compiler_params=_cparams("arbitrary"),
        name="inproj_ctx",
    )(x, g, shift, scale, w, gcq, gckv, wuq, *extra)
    return outs[:3], outs[3:]


def _inproj_lat_kernel(x_ref, g_ref, sh_ref, sc_ref, w_ref, gcq_ref, gckv_ref, wuq_ref,
                       cosa_ref, sina_ref, cosq_ref, sinq_ref, cosr_ref, sinr_ref,
                       qa_ref, ka_ref, va_ref, qb_ref, kb_ref, vb_ref, qc_ref, ckv_ref, kr_ref):
    h = _rms(x_ref[...], g_ref[...]) * (1.0 + sc_ref[0]) + sh_ref[0]
    p = _dot(h.astype(bf16), w_ref[...])
    cosa = cosa_ref[...]
    sina = sina_ref[...]
    qa = p[:, OFF_QA:OFF_QA + W_QA] * cosa + p[:, OFF_QA_P:OFF_QA_P + W_QA] * sina
    ka = p[:, OFF_KA:OFF_KA + W_KA] * cosa[:, :W_KA] + p[:, OFF_KA_P:OFF_KA_P + W_KA] * sina[:, :W_KA]
    kr = p[:, OFF_KR:OFF_KR + QK_ROPE] * cosr_ref[...] + p[:, OFF_KR_P:OFF_KR_P + QK_ROPE] * sinr_ref[...]
    qa_ref[...] = qa.astype(bf16)
    ka_ref[...] = ka.astype(bf16)
    va_ref[...] = p[:, OFF_VA:OFF_VA + W_VA].astype(bf16)
    qb_ref[...] = p[:, OFF_QB:OFF_QB + W_B].astype(bf16)
    kb_ref[...] = p[:, OFF_KB:OFF_KB + W_B].astype(bf16)
    vb_ref[...] = p[:, OFF_VB:OFF_VB + W_B].astype(bf16)
    cqn = _rms(p[:, OFF_CQ:OFF_CQ + Q_LORA], gcq_ref[...])
    q2 = _dot(cqn.astype(bf16), wuq_ref[...])
    nq = H_C * QC_PAD
    qc_ref[...] = (q2[:, :nq] * cosq_ref[...] + q2[:, nq:] * sinq_ref[...]).astype(bf16)
    ckv_ref[...] = _rms(p[:, OFF_CKV:OFF_CKV + KV_LORA], gckv_ref[...]).astype(bf16)
    kr_ref[...] = kr.astype(bf16)


def _inproj_lat(x, lat_row0, g, shift, scale, w, gcq, gckv, wuq2, tabs):
    tm = TM_LAT_IN
    per_b = DEC_SEQ // tm
    row0 = lat_row0 // tm
    xrow = lambda i: (row0 + i, 0)
    row = lambda i: (i, 0)
    const = lambda i: (0, 0)
    grp = lambda i: (1 + i // per_b, 0, 0)
    pos = lambda i: (i % per_b, 0)
    cosa, sina, cosq, sinq, cosr, sinr = tabs
    widths = (W_QA, W_KA, W_VA, W_B, W_B, W_B, H_C * QC_PAD, KV_LORA, QK_ROPE)
    return pl.pallas_call(
        _inproj_lat_kernel,
        grid=(T_LAT // tm,),
        in_specs=[pl.BlockSpec((tm, D_MODEL), xrow),
                  pl.BlockSpec((1, D_MODEL), const),
                  pl.BlockSpec((1, 1, D_MODEL), grp),
                  pl.BlockSpec((1, 1, D_MODEL), grp),
                  pl.BlockSpec((D_MODEL, NW_LAT), const),
                  pl.BlockSpec((1, Q_LORA), const),
                  pl.BlockSpec((1, KV_LORA), const),
                  pl.BlockSpec((Q_LORA, 2 * H_C * QC_PAD), const),
                  pl.BlockSpec((tm, W_QA), pos), pl.BlockSpec((tm, W_QA), pos),
                  pl.BlockSpec((tm, H_C * QC_PAD), pos), pl.BlockSpec((tm, H_C * QC_PAD), pos),
                  pl.BlockSpec((tm, QK_ROPE), pos), pl.BlockSpec((tm, QK_ROPE), pos)],
        out_specs=[pl.BlockSpec((tm, wd), row) for wd in widths],
        out_shape=[jax.ShapeDtypeStruct((T_LAT, wd), bf16) for wd in widths],
        compiler_params=_cparams("arbitrary"),
        name="inproj_lat",
    )(x, g, shift, scale, w, gcq, gckv, wuq2, cosa, sina, cosq, sinq, cosr, sinr)


def _ctx_attn_kernel(sink_ref, qa_ref, ka_ref, va_ref, qb_ref, kb_ref, vb_ref, qc_ref, ckv_ref, kr_ref,
                     wukv_ref, wout_ref, x_ref, gate_ref, o_ref, o_scr, s_scr, p_scr):
    n = SEQ
    scale = HEAD_DIM ** -0.5
    scale_c = (QK_NOPE + QK_ROPE) ** -0.5
    ka = ka_ref[0, 0].astype(bf16)
    va = va_ref[0, 0].astype(bf16)
    kb = kb_ref[0, 0].astype(bf16)
    vb = vb_ref[0, 0].astype(bf16)
    kv = _dot(ckv_ref[0, 0].astype(bf16), wukv_ref[...]).astype(bf16)
    kr = kr_ref[0, 0].astype(bf16)
    for h in range(H_A):
        g = h // G_A
        q = qa_ref[:, h * HEAD_DIM:(h + 1) * HEAD_DIM]
        s_scr[h * n:(h + 1) * n, :] = _dot_nt(q, ka[:, g * HEAD_DIM:(g + 1) * HEAD_DIM]) * scale
    for h in range(H_B):
        sl = slice(h * HEAD_DIM, (h + 1) * HEAD_DIM)
        s_scr[(H_A + h) * n:(H_A + h + 1) * n, :] = _dot_nt(qb_ref[:, sl], kb[:, sl]) * scale
    for h in range(H_C):
        qn = qc_ref[:, h * QC_PAD:h * QC_PAD + QK_NOPE]
        qr = qc_ref[:, h * QC_PAD + QK_NOPE:h * QC_PAD + QK_NOPE + QK_ROPE]
        c0 = h * (QK_NOPE + V_C)
        r0 = (H_A + H_B + h) * n
        s_scr[r0:r0 + n, :] = (_dot_nt(qn, kv[:, c0:c0 + QK_NOPE]) + _dot_nt(qr, kr)) * scale_c
    for pair in range((H_A + H_B + H_C) // 2):
        h0 = 2 * pair
        sinks = ((sink_ref[h0], n), (sink_ref[h0 + 1], n)) if h0 < H_A else None
        _softmax_rows(s_scr, p_scr, slice(h0 * n, (h0 + 2) * n), sinks)
    for h in range(H_A):
        g = h // G_A
        o_scr[:, h * HEAD_DIM:(h + 1) * HEAD_DIM] = _dot(p_scr[h * n:(h + 1) * n, :],
                                                         va[:, g * HEAD_DIM:(g + 1) * HEAD_DIM])
    for h in range(H_B):
        sl = slice(h * HEAD_DIM, (h + 1) * HEAD_DIM)
        o_scr[:, W_QA + h * HEAD_DIM:W_QA + (h + 1) * HEAD_DIM] = _dot(p_scr[(H_A + h) * n:(H_A + h + 1) * n, :],
                                                                     vb[:, sl])
    for h in range(H_C):
        c0 = h * (QK_NOPE + V_C)
        r0 = (H_A + H_B + h) * n
        off = W_QA + W_B + h * V_C
        o_scr[:, off:off + V_C] = _dot(p_scr[r0:r0 + n, :], kv[:, c0 + QK_NOPE:c0 + QK_NOPE + V_C])
    y = _dot(o_scr[...].astype(bf16), wout_ref[...])
    o_ref[...] = x_ref[...] + gate_ref[0] * y


def _ctx_attn(layer, sink, qs, caches, wukv, wout, x, gate):
    qa, qb, qc = qs
    ka, va, kb, vb, ckv, kr = caches
    row = lambda b: (b, 0)
    const = lambda b: (0, 0)
    slot = lambda b: (b, layer, 0, 0)
    qspec = lambda a: pl.BlockSpec((SEQ, a.shape[1]), row)
    cspec = lambda a: pl.BlockSpec((1, 1, SEQ, a.shape[3]), slot)
    in_specs = [pl.BlockSpec(memory_space=pltpu.SMEM),
                qspec(qa), cspec(ka), cspec(va), qspec(qb), cspec(kb), cspec(vb), qspec(qc), cspec(ckv), cspec(kr)]
    in_specs += [pl.BlockSpec((KV_LORA, H_C * (QK_NOPE + V_C)), const),
                 pl.BlockSpec((D_MODEL, D_MODEL), const),
                 pl.BlockSpec((SEQ, D_MODEL), row),
                 pl.BlockSpec((1, 1, D_MODEL), lambda b: (0, 0, 0))]
    return pl.pallas_call(
        _ctx_attn_kernel,
        grid=(BATCH,),
        in_specs=in_specs,
        out_specs=pl.BlockSpec((SEQ, D_MODEL), row),
        out_shape=jax.ShapeDtypeStruct((T_CTX, D_MODEL), f32),
        scratch_shapes=[pltpu.VMEM((SEQ, D_MODEL), f32),
                        pltpu.VMEM(((H_A + H_B + H_C) * SEQ, SEQ), f32),
                        pltpu.VMEM(((H_A + H_B + H_C) * SEQ, SEQ), bf16)],
        compiler_params=_cparams("arbitrary"),
        name="ctx_attn",
    )(sink, qa, ka, va, qb, kb, vb, qc, ckv, kr, wukv, wout, x, gate)


def _lat_attn_kernel(sink_ref, qa_ref, qb_ref, qc_ref, ka_ref, va_ref, kb_ref, vb_ref, ckv_ref, kr_ref,
                     cak_ref, cav_ref, cbk_ref, cbv_ref, cckv_ref, ckr_ref, bias_ref,
                     wukv_ref, wout_ref, x_ref, gate_ref, o_ref, o_scr, kv_scr, sa, pa, sb, pb, sc, pc):
    qi = pl.program_id(1)
    nb = DEC_SEQ // BLOCK
    scale = HEAD_DIM ** -0.5

    @pl.when(qi == 0)
    def _():
        kv_scr[0:DEC_SEQ, :] = _dot(ckv_ref[...], wukv_ref[...]).astype(bf16)
        kv_scr[DEC_SEQ:DEC_SEQ + PAST_LEN, :] = _dot(cckv_ref[0, 0].astype(bf16), wukv_ref[...]).astype(bf16)

    def blk(ref, j):
        idx = jnp.clip(qi + j, 0, nb - 1)
        return ref[pl.ds(pl.multiple_of(idx * BLOCK, BLOCK), BLOCK), :]

    ka = jnp.concatenate([blk(ka_ref, -1), blk(ka_ref, 0), blk(ka_ref, 1), cak_ref[0, 0].astype(bf16)], axis=0)
    va = jnp.concatenate([blk(va_ref, -1), blk(va_ref, 0), blk(va_ref, 1), cav_ref[0, 0].astype(bf16)], axis=0)
    nk_a = 3 * BLOCK + PAST_LEN
    r = lax.broadcasted_iota(jnp.int32, (BLOCK, nk_a), 0)
    c = lax.broadcasted_iota(jnp.int32, (BLOCK, nk_a), 1)
    valid = (((c < BLOCK) & (c >= r) & (qi > 0))
             | ((c >= BLOCK) & (c < 2 * BLOCK))
             | ((c >= 2 * BLOCK) & (c < 3 * BLOCK) & (c - 2 * BLOCK <= r) & (qi < nb - 1))
             | (c >= 3 * BLOCK))
    for h in range(H_A):
        g = h // G_A
        q = qa_ref[:, h * HEAD_DIM:(h + 1) * HEAD_DIM]
        s = _dot_nt(q, ka[:, g * HEAD_DIM:(g + 1) * HEAD_DIM]) * scale
        sa[h * BLOCK:(h + 1) * BLOCK, :] = jnp.where(valid, s, NEG)

    cbk = cbk_ref[0, 0].astype(bf16)
    cbv = cbv_ref[0, 0].astype(bf16)
    rows_per_blk = BLOCK // GRID_W
    nloc = NA_ROWS * GRID_W
    vcats = []
    for half in range(rows_per_blk):
        grow = qi * rows_per_blk + half
        start = jnp.clip(grow - NA_ROWS // 2, 0, ROWS - NA_ROWS)
        kloc = kb_ref[pl.ds(pl.multiple_of(start * GRID_W, GRID_W), nloc), :]
        vloc = vb_ref[pl.ds(pl.multiple_of(start * GRID_W, GRID_W), nloc), :]
        vcats.append(jnp.concatenate([vloc, cbv], axis=0))
        qrows = slice(half * GRID_W, (half + 1) * GRID_W)
        dr0 = start - grow + (NA_ROWS - 1)
        for h in range(H_B):
            sl = slice(h * HEAD_DIM, (h + 1) * HEAD_DIM)
            q = qb_ref[qrows, sl]
            bias = jnp.concatenate([bias_ref[h, dr0 + 2 * j] for j in range(NA_ROWS // 2)], axis=1)
            s_loc = _dot_nt(q, kloc[:, sl]) * scale + bias
            s_ctx = _dot_nt(q, cbk[:, sl]) * scale
            r0 = (half * H_B + h) * GRID_W
            sb[r0:r0 + GRID_W, :] = jnp.concatenate([s_loc, s_ctx], axis=1)

    kr = jnp.concatenate([kr_ref[...], ckr_ref[0, 0].astype(bf16)], axis=0)
    scale_c = (QK_NOPE + QK_ROPE) ** -0.5
    for h in range(H_C):
        qn = qc_ref[:, h * QC_PAD:h * QC_PAD + QK_NOPE]
        qr = qc_ref[:, h * QC_PAD + QK_NOPE:h * QC_PAD + QK_NOPE + QK_ROPE]
        c0 = h * (QK_NOPE + V_C)
        sc[h * BLOCK:(h + 1) * BLOCK, :] = (_dot_nt(qn, kv_scr[:, c0:c0 + QK_NOPE]) + _dot_nt(qr, kr)) * scale_c

    for pair in range(H_A // 2):
        h0 = 2 * pair
        _softmax_rows(sa, pa, slice(h0 * BLOCK, (h0 + 2) * BLOCK), ((sink_ref[h0], BLOCK), (sink_ref[h0 + 1], BLOCK)))
    for blk2 in range(rows_per_blk * H_B // 2):
        _softmax_rows(sb, pb, slice(blk2 * 2 * GRID_W, (blk2 + 1) * 2 * GRID_W))
    for h in range(H_C):
        _softmax_rows(sc, pc, slice(h * BLOCK, (h + 1) * BLOCK))

    for h in range(H_A):
        g = h // G_A
        o_scr[:, h * HEAD_DIM:(h + 1) * HEAD_DIM] = _dot(pa[h * BLOCK:(h + 1) * BLOCK, :],
                                                         va[:, g * HEAD_DIM:(g + 1) * HEAD_DIM])
    for half in range(rows_per_blk):
        qrows = slice(half * GRID_W, (half + 1) * GRID_W)
        for h in range(H_B):
            sl = slice(h * HEAD_DIM, (h + 1) * HEAD_DIM)
            r0 = (half * H_B + h) * GRID_W
            o_scr[qrows, W_QA + h * HEAD_DIM:W_QA + (h + 1) * HEAD_DIM] = _dot(pb[r0:r0 + GRID_W, :],
                                                                             vcats[half][:, sl])
    for h in range(H_C):
        c0 = h * (QK_NOPE + V_C)
        off = W_QA + W_B + h * V_C
        o_scr[:, off:off + V_C] = _dot(pc[h * BLOCK:(h + 1) * BLOCK, :], kv_scr[:, c0 + QK_NOPE:c0 + QK_NOPE + V_C])

    y = _dot(o_scr[...].astype(bf16), wout_ref[...])
    o_ref[...] = x_ref[...] + gate_ref[0] * y


def _lat_attn(layer, sink, proj, caches, bias_tab, wukv, wout, x, lat_row0, gate):
    qa, ka, va, qb, kb, vb, qc, ckv, kr = proj
    nb = DEC_SEQ // BLOCK
    qrow = lambda b, q: (b * nb + q, 0)
    xrow = lambda b, q: (lat_row0 // BLOCK + b * nb + q, 0)
    brow = lambda b, q: (b, 0)
    const = lambda b, q: (0, 0)
    cidx = lambda b, q: (b, layer, 0, 0)
    in_specs = [pl.BlockSpec(memory_space=pltpu.SMEM)]
    in_specs += [pl.BlockSpec((BLOCK, a.shape[1]), qrow) for a in (qa, qb, qc)]
    in_specs += [pl.BlockSpec((DEC_SEQ, a.shape[1]), brow) for a in (ka, va, kb, vb, ckv, kr)]
    in_specs += [pl.BlockSpec((1, 1, PAST_LEN, a.shape[3]), cidx) for a in caches]
    in_specs += [pl.BlockSpec(bias_tab.shape, lambda b, q: (0, 0, 0, 0)),
                 pl.BlockSpec((KV_LORA, H_C * (QK_NOPE + V_C)), const),
                 pl.BlockSpec((D_MODEL, D_MODEL), const),
                 pl.BlockSpec((BLOCK, D_MODEL), xrow),
                 pl.BlockSpec((1, 1, D_MODEL), lambda b, q: (1 + b, 0, 0))]
    return pl.pallas_call(
        _lat_attn_kernel,
        grid=(DEC_BATCH, nb),
        in_specs=in_specs,
        out_specs=pl.BlockSpec((BLOCK, D_MODEL), qrow),
        out_shape=jax.ShapeDtypeStruct((T_LAT, D_MODEL), f32),
        scratch_shapes=[pltpu.VMEM((BLOCK, D_MODEL), f32),
                        pltpu.VMEM((DEC_SEQ + PAST_LEN, H_C * (QK_NOPE + V_C)), bf16)]
        + [pltpu.VMEM(shape, dt) for shape in ((H_A * BLOCK, 3 * BLOCK + PAST_LEN),
                                               (H_B * BLOCK, NA_ROWS * GRID_W + PAST_LEN),
                                               (H_C * BLOCK, DEC_SEQ + PAST_LEN)) for dt in (f32, bf16)],
        compiler_params=_cparams("arbitrary", "arbitrary"),
        name="lat_attn",
    )(sink, qa, qb, qc, ka, va, kb, vb, ckv, kr, *caches, bias_tab, wukv, wout, x, gate)


def _pick_stream(xc_ref, xl_ref, x_scr):
    i = pl.program_id(0)

    @pl.when(i < T_CTX // TM_TOK)
    def _():
        x_scr[...] = xc_ref[...]

    @pl.when(i >= T_CTX // TM_TOK)
    def _():
        x_scr[...] = xl_ref[...]

    return x_scr[...]


def _stream_specs(lat_row0):
    n_ctx = T_CTX // TM_TOK
    return [pl.BlockSpec((TM_TOK, D_MODEL), lambda i: (jnp.minimum(i, n_ctx - 1), 0)),
            pl.BlockSpec((TM_TOK, D_MODEL), lambda i: (lat_row0 // TM_TOK + jnp.maximum(i - n_ctx, 0), 0))]


def _router_kernel(xc_ref, xl_ref, g_ref, sh_ref, sc_ref, wr_ref, br_ref, h_ref, e_ref, gt_ref, x_scr):
    h = _rms(_pick_stream(xc_ref, xl_ref, x_scr), g_ref[...]) * (1.0 + sc_ref[0]) + sh_ref[0]
    _store_row_tiles(h_ref, h)
    h_hi = h.astype(bf16)
    h_lo = (h - h_hi.astype(f32)).astype(bf16)
    w = wr_ref[...]
    w_hi = w.astype(bf16)
    w_lo = (w - w_hi.astype(f32)).astype(bf16)
    logits = _dot(h_hi, w_hi) + _dot(h_hi, w_lo) + _dot(h_lo, w_hi) + br_ref[...]
    lane = lax.broadcasted_iota(jnp.int32, logits.shape, 1).astype(f32)
    l = jnp.where(lane < N_EXPERTS, logits, -jnp.inf)
    tops, idxs = [], []
    for _ in range(TOP_K):
        m = jnp.max(l, axis=-1, keepdims=True)
        idx = jnp.min(jnp.where(l == m, lane, float(LANE)), axis=-1, keepdims=True)
        tops.append(m)
        idxs.append(idx)
        l = jnp.where(lane == idx, -jnp.inf, l)
    ex = [jnp.exp(t - tops[0]) for t in tops]
    den = ex[0] + ex[1] + ex[2] + ex[3]
    e_out = jnp.zeros(logits.shape, f32)
    g_out = jnp.zeros(logits.shape, f32)
    for k in range(TOP_K):
        e_out = jnp.where(lane == k, idxs[k], e_out)
        g_out = jnp.where(lane == k, ex[k] / den, g_out)
    e_ref[...] = e_out.astype(jnp.int32)
    gt_ref[...] = g_out


def _group_of_tile(i):
    per_b = DEC_SEQ // TM_TOK
    n_ctx = T_CTX // TM_TOK
    return jnp.where(i < n_ctx, 0, 1 + (i - n_ctx) // per_b)


def _router(xc, xl, lat_row0, g, shift, scale, wr, br):
    tm = TM_TOK
    row = lambda i: (i, 0)
    const = lambda i: (0, 0)
    grp = lambda i: (_group_of_tile(i), 0, 0)
    return pl.pallas_call(
        _router_kernel,
        grid=(T_ALL // tm,),
        in_specs=_stream_specs(lat_row0) +
                 [pl.BlockSpec((1, D_MODEL), const),
                  pl.BlockSpec((1, 1, D_MODEL), grp),
                  pl.BlockSpec((1, 1, D_MODEL), grp),
                  pl.BlockSpec((D_MODEL, LANE), const),
                  pl.BlockSpec((1, LANE), const)],
        out_specs=[pl.BlockSpec((tm * ROW_TILE, LANE), row), pl.BlockSpec((tm, LANE), row),
                   pl.BlockSpec((tm, LANE), row)],
        out_shape=[jax.ShapeDtypeStruct((T_ALL * ROW_TILE, LANE), f32),
                   jax.ShapeDtypeStruct((T_ALL, LANE), jnp.int32),
                   jax.ShapeDtypeStruct((T_ALL, LANE), f32)],
        scratch_shapes=[pltpu.VMEM((tm, D_MODEL), f32)],
        compiler_params=_cparams("arbitrary"),
        name="router",
    )(xc, xl, g, shift, scale, wr, br)


def _dispatch_kernel(tok_ref, nu_ref, h_hbm, o_ref, hv, xg, hsem):
    tm = TM_MOE
    i = pl.program_id(0)

    @pl.when(i == 0)
    def _():
        resident = pltpu.make_async_copy(h_hbm, hv, hsem.at[0])
        resident.start()
        resident.wait()

    def one_block(sub, carry):
        blk = i * DISPATCH_BLOCKS + sub
        rows = pl.ds(pl.multiple_of(sub * tm, tm), tm)

        @pl.when(blk < nu_ref[0])
        def _():
            for r in range(tm):
                t = tok_ref[blk * tm + r]
                xg[pl.ds(r, ROW_TILE, stride=tm + 1), :] = hv[pl.ds(pl.multiple_of(t * ROW_TILE, ROW_TILE),
                                                                 ROW_TILE), :]
            o_ref[rows, :] = jnp.concatenate([xg[pl.ds(c * (tm + 1), tm), :] for c in range(ROW_TILE)],
                                             axis=1).astype(bf16)

        @pl.when(blk >= nu_ref[0])
        def _():
            o_ref[rows, :] = jnp.zeros((tm, D_MODEL), bf16)

        return carry

    lax.fori_loop(0, DISPATCH_BLOCKS, one_block, 0)


def _dispatch(row_tok, n_used, h):
    tm = TM_MOE
    return pl.pallas_call(
        _dispatch_kernel,
        grid_spec=pltpu.PrefetchScalarGridSpec(
            num_scalar_prefetch=2,
            grid=(N_MOE_BLOCKS // DISPATCH_BLOCKS,),
            in_specs=[pl.BlockSpec(memory_space=pl.ANY)],
            out_specs=pl.BlockSpec((DISPATCH_BLOCKS * tm, D_MODEL), lambda i, tok, nu: (i, 0)),
            scratch_shapes=[pltpu.VMEM((T_ALL * ROW_TILE, LANE), f32), pltpu.VMEM(((tm + 1) * ROW_TILE, LANE), f32),
                            pltpu.SemaphoreType.DMA((1,))]),
        out_shape=jax.ShapeDtypeStruct((N_MOE_BLOCKS * tm, D_MODEL), bf16),
        compiler_params=_cparams("arbitrary"),
        name="dispatch",
    )(row_tok, n_used, h)


def _moe_kernel(layer, be_ref, nu_ref, nxt_ref, dst_ref, x_ref, wgu_hbm, bgu_ref, wd_hbm, bd_ref, y_hbm,
                y0, y1, wgu_st, wd_st, wgu_bf, wd_bf, wsem, ssem):
    tm = TM_MOE
    i = pl.program_id(0)
    nb = pl.num_programs(0)
    used = i < nu_ref[0]
    yb = (y0, y1)

    def out_tile(row):
        return pl.ds(pl.multiple_of(row * ROW_TILE, ROW_TILE), ROW_TILE)

    def scatter_desc(buf, r, dst_row, s):
        return pltpu.make_async_copy(buf.at[out_tile(r)], y_hbm.at[out_tile(dst_row)], ssem.at[s])

    def scatter_wait(s):
        pltpu.make_async_copy(yb[s], y_hbm.at[pl.ds(0, tm * ROW_TILE)], ssem.at[s]).wait()

    def scatter_start(blk, s, unrolled):
        if unrolled:
            for r in range(tm):
                scatter_desc(yb[s], r, dst_ref[(blk + 1) * tm + r], s).start(priority=r % 2)
        else:
            def body(r, carry):
                scatter_desc(yb[s], r, dst_ref[(blk + 1) * tm + r], s).start()
                return carry
            lax.fori_loop(0, tm, body, 0, unroll=8)

    def weight_copies(e):
        return (pltpu.make_async_copy(wgu_hbm.at[layer, e], wgu_st, wsem.at[0]),
                pltpu.make_async_copy(wd_hbm.at[layer, e], wd_st, wsem.at[1]))

    @pl.when(i == 0)
    def _():
        for s in range(2):
            yb[s][...] = jnp.zeros_like(yb[s])
            dummy = pltpu.make_async_copy(yb[s], y_hbm.at[pl.ds((N_ASSIGN + s * tm) * ROW_TILE, tm * ROW_TILE)],
                                          ssem.at[s])
            dummy.start()
            dummy.wait()
        for cp in weight_copies(be_ref[0]):
            cp.start()

    first = jnp.logical_and(used, jnp.logical_or(i == 0, be_ref[i] != be_ref[jnp.maximum(i - 1, 0)]))

    @pl.when(first)
    def _():
        for cp in weight_copies(0):
            cp.wait()
        wgu_bf[...] = wgu_st[...].astype(bf16)
        wd_bf[...] = wd_st[...].astype(bf16)

        @pl.when(nxt_ref[i] >= 0)
        def _():
            for cp in weight_copies(nxt_ref[i]):
                cp.start()

    def step(par):
        cur, oth = par, 1 - par

        @pl.when(jnp.logical_and(i >= 1, i - 2 < nu_ref[0]))
        def _():
            scatter_wait(cur)

        @pl.when(used)
        def _():
            scatter_start(i - 1, oth, unrolled=True)
            gu = _dot(x_ref[...], wgu_bf[...]) + bgu_ref[0, 0]
            x_glu = jnp.minimum(gu[:, :D_FF], SWIGLU_LIMIT)
            x_lin = jnp.clip(gu[:, D_FF:], -SWIGLU_LIMIT, SWIGLU_LIMIT)
            act = x_glu * jax.nn.sigmoid(SWIGLU_ALPHA * x_glu) * (x_lin + 1.0)
            _store_row_tiles(yb[cur], _dot(act.astype(bf16), wd_bf[...]) + bd_ref[0, 0])

        flush = jnp.logical_and(jnp.logical_not(used), i - 1 < nu_ref[0])

        @pl.when(flush)
        def _():
            scatter_start(i - 1, oth, unrolled=False)

        @pl.when(jnp.logical_and(flush, i == nb - 1))
        def _():
            scatter_wait(oth)

    @pl.when(i % 2 == 0)
    def _():
        step(0)

    @pl.when(i % 2 == 1)
    def _():
        step(1)


def _moe(layer, routing, h, w_gu, b_gu, w_down, b_down):
    tm = TM_MOE
    block_e, n_used, nxt_e, row_tok, row_dst = routing
    xs = _dispatch(row_tok, n_used, h)
    ex4 = lambda i, be, nu, nxt, dst: (layer, be[i], 0, 0)
    return pl.pallas_call(
        functools.partial(_moe_kernel, layer),
        grid_spec=pltpu.PrefetchScalarGridSpec(
            num_scalar_prefetch=4,
            grid=(N_MOE_BLOCKS,),
            in_specs=[pl.BlockSpec((tm, D_MODEL), lambda i, be, nu, nxt, dst: (i, 0)),
                      pl.BlockSpec(memory_space=pl.ANY),
                      pl.BlockSpec((1, 1, 1, 2 * D_FF), ex4),
                      pl.BlockSpec(memory_space=pl.ANY),
                      pl.BlockSpec((1, 1, 1, D_MODEL), ex4)],
            out_specs=pl.BlockSpec(memory_space=pl.ANY),
            scratch_shapes=[pltpu.VMEM((tm * ROW_TILE, LANE), f32), pltpu.VMEM((tm * ROW_TILE, LANE), f32),
                            pltpu.VMEM((D_MODEL, 2 * D_FF), f32), pltpu.VMEM((D_FF, D_MODEL), f32),
                            pltpu.VMEM((D_MODEL, 2 * D_FF), bf16), pltpu.VMEM((D_FF, D_MODEL), bf16),
                            pltpu.SemaphoreType.DMA((2,)), pltpu.SemaphoreType.DMA((2,))]),
        out_shape=jax.ShapeDtypeStruct(((N_ASSIGN + 2 * tm) * ROW_TILE, LANE), f32),
        compiler_params=_cparams("arbitrary"),
        name="moe",
    )(block_e, n_used, nxt_e, row_dst, xs, w_gu, b_gu.reshape(DEPTH, N_EXPERTS, 1, 2 * D_FF),
      w_down, b_down.reshape(DEPTH, N_EXPERTS, 1, D_MODEL))


def _combine_kernel(final, xc_ref, xl_ref, y0_ref, y1_ref, y2_ref, y3_ref, gt_ref, gate_ref, gf_ref, *rest):
    x_scr = rest[-1]
    gt = gt_ref[...]
    f = gt[:, 0:1] * _load_row_tiles(y0_ref)
    for k, y_ref in ((1, y1_ref), (2, y2_ref), (3, y3_ref)):
        f = f + gt[:, k:k + 1] * _load_row_tiles(y_ref)
    out = _pick_stream(xc_ref, xl_ref, x_scr) + gate_ref[0] * f
    if not final:
        rest[0][...] = out
        return
    out = _rms(out, gf_ref[...])
    oc_ref, ol_ref = rest[0], rest[1]
    i = pl.program_id(0)

    @pl.when(i < T_CTX // TM_TOK)
    def _():
        oc_ref[...] = out

    @pl.when(i >= T_CTX // TM_TOK)
    def _():
        ol_ref[...] = out


def _combine(final, xc, xl, lat_row0, y, gates, gate, g_final):
    tm = TM_TOK
    nt = T_ALL // tm
    n_ctx = T_CTX // tm
    row = lambda i: (i, 0)
    const = lambda i: (0, 0)
    grp = lambda i: (_group_of_tile(i), 0, 0)
    ysel = [pl.BlockSpec((tm * ROW_TILE, LANE), functools.partial(lambda k, i: (k * nt + i, 0), k))
            for k in range(TOP_K)]
    if final:
        out_specs = [pl.BlockSpec((tm, D_MODEL), lambda i: (jnp.minimum(i, n_ctx - 1), 0)),
                     pl.BlockSpec((tm, D_MODEL), lambda i: (jnp.maximum(i - n_ctx, 0), 0))]
        out_shape = [jax.ShapeDtypeStruct((T_CTX, D_MODEL), f32), jax.ShapeDtypeStruct((T_LAT, D_MODEL), f32)]
    else:
        out_specs = pl.BlockSpec((tm, D_MODEL), row)
        out_shape = jax.ShapeDtypeStruct((T_ALL, D_MODEL), f32)
    return pl.pallas_call(
        functools.partial(_combine_kernel, final),
        grid=(nt,),
        in_specs=_stream_specs(lat_row0) + ysel +
                 [pl.BlockSpec((tm, LANE), row),
                  pl.BlockSpec((1, 1, D_MODEL), grp),
                  pl.BlockSpec((1, D_MODEL), const)],
        out_specs=out_specs,
        out_shape=out_shape,
        scratch_shapes=[pltpu.VMEM((tm, D_MODEL), f32)],
        compiler_params=_cparams("arbitrary"),
        name="combine",
    )(xc, xl, y, y, y, y, gates, gate, g_final)


def _rope_head_tables(d):
    nf = d // 4
    half = d // 2
    t = np.arange(DEC_SEQ)
    inv = ROPE_BASE ** (-np.arange(nf, dtype=np.float32) / nf)
    i = np.arange(d)
    pos = np.where(i[None, :] < half, (t // GRID_W)[:, None], (t % GRID_W)[:, None]).astype(np.float32)
    ang = pos * inv[i % nf][None, :].astype(np.float32)
    first = (i % half) < nf
    cos = np.cos(ang)
    sin = np.where(first[None, :], -np.sin(ang), np.sin(ang))
    partner = np.where(first, i + nf, i - nf)
    return cos.astype(np.float32), sin.astype(np.float32), partner


def _rope_tables():
    cos64, sin64, _ = _rope_head_tables(HEAD_DIM)
    cos32, sin32, _ = _rope_head_tables(QK_ROPE)
    cosa = np.tile(cos64, (1, H_A))
    sina = np.tile(sin64, (1, H_A))
    cosq1 = np.concatenate([np.ones((DEC_SEQ, QK_NOPE), np.float32), cos32,
                            np.ones((DEC_SEQ, QC_PAD - QK_NOPE - QK_ROPE), np.float32)], axis=1)
    sinq1 = np.concatenate([np.zeros((DEC_SEQ, QK_NOPE), np.float32), sin32,
                            np.zeros((DEC_SEQ, QC_PAD - QK_NOPE - QK_ROPE), np.float32)], axis=1)
    cosq = np.tile(cosq1, (1, H_C))
    sinq = np.tile(sinq1, (1, H_C))
    return tuple(jnp.asarray(a) for a in (cosa, sina, cosq, sinq, cos32, sin32))


def _pad_cols(w, n):
    return jnp.pad(w, ((0, 0), (0, n - w.shape[1])))


def _layer_weights(w_in, w_uq):
    cuts = np.cumsum((W_QA, W_KA, W_VA, W_B, W_B, W_B, Q_LORA, KV_LORA, QK_ROPE))[:-1]
    qa, ka, va, qb, kb, vb, cq, ckv, kr = jnp.split(w_in, [int(c) for c in cuts], axis=1)
    _, _, p64 = _rope_head_tables(HEAD_DIM)
    _, _, p32 = _rope_head_tables(QK_ROPE)
    pa = np.concatenate([h * HEAD_DIM + p64 for h in range(H_A)])
    base = jnp.concatenate([qa, ka, va, _pad_cols(qb, 384), _pad_cols(kb, 384), _pad_cols(vb, 384), cq, ckv,
                            _pad_cols(kr, 128)], axis=1)
    w_ctx = base.astype(bf16)
    w_lat = jnp.concatenate([base, qa[:, pa], ka[:, pa[:W_KA]], _pad_cols(kr[:, p32], 128)], axis=1).astype(bf16)
    hq = QK_NOPE + QK_ROPE
    heads = [_pad_cols(w_uq[:, h * hq:(h + 1) * hq], QC_PAD) for h in range(H_C)]
    pq = np.concatenate([np.arange(QK_NOPE), QK_NOPE + p32])
    heads_p = [_pad_cols(w_uq[:, h * hq:(h + 1) * hq][:, pq], QC_PAD) for h in range(H_C)]
    wuq = jnp.concatenate(heads, axis=1).astype(bf16)
    wuq2 = jnp.concatenate(heads + heads_p, axis=1).astype(bf16)
    return w_ctx, w_lat, wuq, wuq2


def _bias_table(rpb):
    col = np.arange(GRID_W)
    col_start = np.clip(col - NA_COLS // 2, 0, GRID_W - NA_COLS)
    col_ok = (col[None, :] >= col_start[:, None]) & (col[None, :] < col_start[:, None] + NA_COLS)
    dc = np.clip(col[None, :] - col[:, None] + (NA_COLS - 1), 0, 2 * NA_COLS - 2)
    onehot = (dc[None] == np.arange(2 * NA_COLS - 1)[:, None, None]).astype(np.float32)
    expanded = jnp.einsum('hrd,dqk->hrqk', rpb.astype(f32), jnp.asarray(onehot), precision=lax.Precision.HIGHEST)
    blocks = jnp.where(col_ok[None, None], expanded, NEG)
    return jnp.concatenate([blocks[:, :-1], blocks[:, 1:]], axis=-1)


def _routing(top_e):
    tm = TM_MOE
    key_bits = 16
    pad_mark = (1 << key_bits) - 1
    flat_e = top_e.T.reshape(N_ASSIGN)
    experts = jnp.arange(N_EXPERTS, dtype=jnp.int32)
    counts = jnp.sum((flat_e[:, None] == experts[None, :]).astype(jnp.int32), axis=0)
    nblk = (counts + tm - 1) // tm
    blk_end = jnp.cumsum(nblk)
    pad_end = jnp.cumsum(nblk * tm - counts)
    slots = jnp.arange(N_MOE_BLOCKS * tm - N_ASSIGN, dtype=jnp.int32)
    pad_e = jnp.sum((pad_end[None, :] <= slots[:, None]).astype(jnp.int32), axis=1)
    keys = jnp.concatenate([(flat_e << key_bits) + jnp.arange(N_ASSIGN, dtype=jnp.int32),
                            (pad_e << key_bits) + pad_mark])
    asg = (jnp.sort(keys) & pad_mark).reshape(N_MOE_BLOCKS, tm)
    valid = asg != pad_mark
    blocks = jnp.arange(N_MOE_BLOCKS, dtype=jnp.int32)
    r = jnp.arange(tm, dtype=jnp.int32)[None, :]
    tok = jnp.where(valid, asg % T_ALL, 0)
    row_dst = jnp.where(valid, asg, N_ASSIGN + (blocks[:, None] % 2) * tm + r)
    row_dst = jnp.concatenate([N_ASSIGN + tm + r, row_dst], axis=0).reshape(-1)
    block_e = jnp.minimum(jnp.sum((blk_end[None, :] <= blocks[:, None]).astype(jnp.int32), axis=1), N_EXPERTS - 1)
    n_used = blk_end[-1].astype(jnp.int32).reshape(1)
    has = jnp.where(counts > 0, experts, N_EXPERTS)
    later = experts[None, :] > experts[:, None]
    nxt = jnp.min(jnp.where(later, has[None, :], N_EXPERTS), axis=1)
    nxt = jnp.where(nxt >= N_EXPERTS, -1, nxt)
    sel = (block_e[:, None] == experts[None, :]).astype(jnp.int32)
    nxt_e = jnp.sum(sel * nxt[None, :], axis=1)
    i32 = lambda a: a.astype(jnp.int32)
    return i32(block_e), n_used, i32(nxt_e), i32(tok).reshape(-1), i32(row_dst)


def kernel(x_prompt, x_sample, cache_a_k, cache_a_v, cache_b_k, cache_b_v, cache_c_kv, cache_c_kr, c, c_ctx, w_ada, b_ada, g_attn, g_ffn, w_in, sink_a, rpb_b, g_cq, g_ckv, w_uq, w_ukv, w_out, w_router, b_router, w_gu, b_gu, w_down, b_down, g_final):
    xc, xl, lat_row0 = x_prompt.reshape(T_CTX, D_MODEL), x_sample.reshape(T_LAT, D_MODEL), 0
    cvec = jnp.concatenate([c_ctx[None, :], c, jnp.zeros((8 - N_GROUPS, D_MODEL), f32)], axis=0)
    mods = _ada(cvec, w_ada, b_ada)[:, :N_GROUPS].reshape(DEPTH, N_GROUPS, 6, 1, D_MODEL)
    tabs = _rope_tables()
    caches = (cache_a_k.reshape(DEC_BATCH, DEPTH, PAST_LEN, W_KA), cache_a_v.reshape(DEC_BATCH, DEPTH, PAST_LEN, W_VA),
              cache_b_k.reshape(DEC_BATCH, DEPTH, PAST_LEN, W_B), cache_b_v.reshape(DEC_BATCH, DEPTH, PAST_LEN, W_B),
              cache_c_kv, cache_c_kr)
    new = None
    for layer in range(DEPTH):
        m = [mods[layer, :, j] for j in range(6)]
        w_ctx, w_lat, wuq, wuq2 = _layer_weights(w_in[layer], w_uq[layer])
        wukv = w_ukv[layer].astype(bf16)
        wout = w_out[layer].astype(bf16)
        g1 = g_attn[layer][None, :]
        gcq = g_cq[layer][None, :]
        gckv = g_ckv[layer][None, :]
        sink = sink_a[layer]

        qs, new = _inproj_ctx(layer, new, xc, g1, m[0], m[1], w_ctx, gcq, gckv, wuq)
        x_ctx = _ctx_attn(layer, sink, qs, new, wukv, wout, xc, m[2])

        plat = _inproj_lat(xl, lat_row0, g1, m[0], m[1], w_lat, gcq, gckv, wuq2, tabs)
        x_lat = _lat_attn(layer, sink, plat, caches, _bias_table(rpb_b[layer]), wukv, wout, xl, lat_row0, m[2])

        wr = _pad_cols(w_router[layer], LANE)
        br = _pad_cols(b_router[layer][None, :], LANE)
        h2, top_e, gates = _router(x_ctx, x_lat, 0, g_ffn[layer][None, :], m[3], m[4], wr, br)
        y = _moe(layer, _routing(top_e[:, :TOP_K]), h2, w_gu, b_gu, w_down, b_down)
        x = _combine(layer == DEPTH - 1, x_ctx, x_lat, 0, y, gates, m[5], g_final[None, :])
        xc, xl, lat_row0 = x, x, T_CTX

    y_prompt = x[0].reshape(BATCH, SEQ, D_MODEL)
    y_sample = x[1].reshape(DEC_BATCH, DEC_SEQ, D_MODEL)
    shapes = ((KV_A, HEAD_DIM), (KV_A, HEAD_DIM), (H_B, HEAD_DIM), (H_B, HEAD_DIM), (KV_LORA,), (QK_ROPE,))
    outs = [a.reshape((BATCH, DEPTH, SEQ) + s) for a, s in zip(new, shapes)]
    return (y_prompt, y_sample, *outs)
```

```python
import functools

import numpy as np
import jax
import jax.numpy as jnp
from jax import lax
from jax.experimental import pallas as pl
from jax.experimental.pallas import tpu as pltpu

D_MODEL = 1024
BATCH = 32
SEQ = 256
DEPTH = 2
DEC_BATCH = 2
DEC_SEQ = 1024
PAST_LEN = 512
GRID_W = 64
HEAD_DIM = 64
H_A = 6
KV_A = 2
G_A = H_A // KV_A
WINDOW = 128
BLOCK = 128
H_B = 5
NA_ROWS = 8
NA_COLS = 16
H_C = 5
Q_LORA = 384
KV_LORA = 256
QK_NOPE = 64
QK_ROPE = 32
V_C = 64
N_EXPERTS = 32
TOP_K = 4
D_FF = 1024
SWIGLU_ALPHA = 1.702
SWIGLU_LIMIT = 7.0
ROPE_BASE = 10000.0
EPS = 1e-6
NEG = -1e30

T_CTX = BATCH * SEQ
T_LAT = DEC_BATCH * DEC_SEQ
T_ALL = T_CTX + T_LAT
N_GROUPS = 1 + DEC_BATCH
LANE = 128
QC_PAD = 128
ROWS = DEC_SEQ // GRID_W

W_QA, W_KA, W_VA = H_A * HEAD_DIM, KV_A * HEAD_DIM, KV_A * HEAD_DIM
W_B = H_B * HEAD_DIM
OFF_QA = 0
OFF_KA = 384
OFF_VA = 512
OFF_QB = 640
OFF_KB = 1024
OFF_VB = 1408
OFF_CQ = 1792
OFF_CKV = 2176
OFF_KR = 2432
NW_CTX = 2560
OFF_QA_P = 2560
OFF_KA_P = 2944
OFF_KR_P = 3072
NW_LAT = 3200

TM_TOK = 512
TM_LAT_IN = 512
TM_MOE = 256
N_ASSIGN = T_ALL * TOP_K
N_MOE_BLOCKS = N_ASSIGN // TM_MOE + N_EXPERTS
DISPATCH_BLOCKS = 8
VMEM_LIMIT = 56 * 1024 * 1024

f32 = jnp.float32
bf16 = jnp.bfloat16


def _cparams(*sem):
    return pltpu.CompilerParams(dimension_semantics=sem, vmem_limit_bytes=VMEM_LIMIT)


def _rms(xf, g):
    return xf * lax.rsqrt(jnp.mean(xf * xf, axis=-1, keepdims=True) + EPS) * g


def _dot(a, b):
    return jnp.dot(a, b, preferred_element_type=f32)


def _dot_nt(a, b):
    return lax.dot_general(a, b, (((1,), (1,)), ((), ())), preferred_element_type=f32)


ROW_TILE = D_MODEL // LANE


def _store_row_tiles(ref, val):
    n = val.shape[0]
    for c in range(ROW_TILE):
        ref[pl.ds(c, n, stride=ROW_TILE), :] = val[:, c * LANE:(c + 1) * LANE]


def _load_row_tiles(ref):
    n = ref.shape[0] // ROW_TILE
    return jnp.concatenate([ref[pl.ds(c, n, stride=ROW_TILE), :] for c in range(ROW_TILE)], axis=1)


def _softmax_rows(s_ref, p_ref, rows, sinks=None):
    s = s_ref[rows, :]
    m = jnp.max(s, axis=-1, keepdims=True)
    if sinks is not None:
        sink = jnp.concatenate([jnp.full((n, 1), v, f32) for v, n in sinks], axis=0)
        m = jnp.maximum(m, sink)
    p = jnp.exp(s - m)
    l = jnp.sum(p, axis=-1, keepdims=True)
    if sinks is not None:
        l = l + jnp.exp(sink - m)
    p_ref[rows, :] = (p * (1.0 / l)).astype(bf16)


def _ada_kernel(c_ref, w_ref, b_ref, o_ref):
    c = c_ref[...]
    s = c * jax.nn.sigmoid(c)
    s_hi = s.astype(bf16)
    s_lo = (s - s_hi.astype(f32)).astype(bf16)
    w = w_ref[0]
    w_hi = w.astype(bf16)
    w_lo = (w - w_hi.astype(f32)).astype(bf16)
    o_ref[0] = _dot(s_hi, w_hi) + _dot(s_hi, w_lo) + _dot(s_lo, w_hi) + b_ref[0]


def _ada(cvec, w_ada, b_ada):
    tn = 1536
    return pl.pallas_call(
        _ada_kernel,
        grid=(DEPTH, 6 * D_MODEL // tn),
        in_specs=[pl.BlockSpec((8, D_MODEL), lambda l, j: (0, 0)),
                  pl.BlockSpec((1, D_MODEL, tn), lambda l, j: (l, 0, j)),
                  pl.BlockSpec((1, 1, tn), lambda l, j: (l, 0, j))],
        out_specs=pl.BlockSpec((1, 8, tn), lambda l, j: (l, 0, j)),
        out_shape=jax.ShapeDtypeStruct((DEPTH, 8, 6 * D_MODEL), f32),
        compiler_params=_cparams("arbitrary", "arbitrary"),
        name="ada",
    )(cvec, w_ada, b_ada.reshape(DEPTH, 1, 6 * D_MODEL))


CACHE_WIDTHS = (W_KA, W_VA, W_B, W_B, KV_LORA, QK_ROPE)


def _inproj_ctx_kernel(layer, x_ref, g_ref, sh_ref, sc_ref, w_ref, gcq_ref, gckv_ref, wuq_ref, *refs):
    qa_ref, qb_ref, qc_ref, ka_ref, va_ref, kb_ref, vb_ref, ckv_ref, kr_ref = refs[-9:]
    h = _rms(x_ref[...], g_ref[...]) * (1.0 + sc_ref[0]) + sh_ref[0]
    p = _dot(h.astype(bf16), w_ref[...])
    qa_ref[...] = p[:, OFF_QA:OFF_QA + W_QA].astype(bf16)
    qb_ref[...] = p[:, OFF_QB:OFF_QB + W_B].astype(bf16)
    cqn = _rms(p[:, OFF_CQ:OFF_CQ + Q_LORA], gcq_ref[...])
    qc_ref[...] = _dot(cqn.astype(bf16), wuq_ref[...]).astype(bf16)
    caches = ((ka_ref, p[:, OFF_KA:OFF_KA + W_KA]), (va_ref, p[:, OFF_VA:OFF_VA + W_VA]),
              (kb_ref, p[:, OFF_KB:OFF_KB + W_B]), (vb_ref, p[:, OFF_VB:OFF_VB + W_B]),
              (ckv_ref, _rms(p[:, OFF_CKV:OFF_CKV + KV_LORA], gckv_ref[...])),
              (kr_ref, p[:, OFF_KR:OFF_KR + QK_ROPE]))
    for ref, val in caches:
        for b in range(TM_TOK // SEQ):
            rows = val[b * SEQ:(b + 1) * SEQ]
            if layer == 0:
                ref[b, 0] = rows
                for later in range(1, DEPTH):
                    ref[b, later] = jnp.zeros_like(rows)
            else:
                ref[b, 0] = rows


def _inproj_ctx(layer, prev_caches, x, g, shift, scale, w, gcq, gckv, wuq):
    tm = TM_TOK
    nb = tm // SEQ
    row = lambda i: (i, 0)
    const = lambda i: (0, 0)
    in_specs = [pl.BlockSpec((tm, D_MODEL), row),
                pl.BlockSpec((1, D_MODEL), const),
                pl.BlockSpec((1, 1, D_MODEL), lambda i: (0, 0, 0)),
                pl.BlockSpec((1, 1, D_MODEL), lambda i: (0, 0, 0)),
                pl.BlockSpec((D_MODEL, NW_CTX), const),
                pl.BlockSpec((1, Q_LORA), const),
                pl.BlockSpec((1, KV_LORA), const),
                pl.BlockSpec((Q_LORA, H_C * QC_PAD), const)]
    q_widths = (W_QA, W_B, H_C * QC_PAD)
    out_specs = [pl.BlockSpec((tm, wd), row) for wd in q_widths]
    out_shape = [jax.ShapeDtypeStruct((T_CTX, wd), bf16) for wd in q_widths]
    if layer == 0:
        out_specs += [pl.BlockSpec((nb, DEPTH, SEQ, wd), lambda i: (i, 0, 0, 0)) for wd in CACHE_WIDTHS]
        aliases, extra = {}, ()
    else:
        in_specs += [pl.BlockSpec(memory_space=pl.ANY) for _ in CACHE_WIDTHS]
        out_specs += [pl.BlockSpec((nb, 1, SEQ, wd), lambda i: (i, layer, 0, 0)) for wd in CACHE_WIDTHS]
        aliases = {8 + j: len(q_widths) + j for j in range(len(CACHE_WIDTHS))}
        extra = tuple(prev_caches)
    out_shape += [jax.ShapeDtypeStruct((BATCH, DEPTH, SEQ, wd), f32) for wd in CACHE_WIDTHS]
    outs = pl.pallas_call(
        functools.partial(_inproj_ctx_kernel, layer),
        grid=(T_CTX // tm,),
        in_specs=in_specs,
        out_specs=out_specs,
        out_shape=out_shape,
        input_output_aliases=aliases,
        compiler_params=_cparams("arbitrary"),
        name="inproj_ctx",
    )(x, g, shift, scale, w, gcq, gckv, wuq, *extra)
    return outs[:3], outs[3:]


def _inproj_lat_kernel(x_ref, g_ref, sh_ref, sc_ref, w_ref, gcq_ref, gckv_ref, wuq_ref,
                       cosa_ref, sina_ref, cosq_ref, sinq_ref, cosr_ref, sinr_ref,
                       qa_ref, ka_ref, va_ref, qb_ref, kb_ref, vb_ref, qc_ref, ckv_ref, kr_ref):
    h = _rms(x_ref[...], g_ref[...]) * (1.0 + sc_ref[0]) + sh_ref[0]
    p = _dot(h.astype(bf16), w_ref[...])
    cosa = cosa_ref[...]
    sina = sina_ref[...]
    qa = p[:, OFF_QA:OFF_QA + W_QA] * cosa + p[:, OFF_QA_P:OFF_QA_P + W_QA] * sina
    ka = p[:, OFF_KA:OFF_KA + W_KA] * cosa[:, :W_KA] + p[:, OFF_KA_P:OFF_KA_P + W_KA] * sina[:, :W_KA]
    kr = p[:, OFF_KR:OFF_KR + QK_ROPE] * cosr_ref[...] + p[:, OFF_KR_P:OFF_KR_P + QK_ROPE] * sinr_ref[...]
    qa_ref[...] = qa.astype(bf16)
    ka_ref[...] = ka.astype(bf16)
    va_ref[...] = p[:, OFF_VA:OFF_VA + W_VA].astype(bf16)
    qb_ref[...] = p[:, OFF_QB:OFF_QB + W_B].astype(bf16)
    kb_ref[...] = p[:, OFF_KB:OFF_KB + W_B].astype(bf16)
    vb_ref[...] = p[:, OFF_VB:OFF_VB + W_B].astype(bf16)
    cqn = _rms(p[:, OFF_CQ:OFF_CQ + Q_LORA], gcq_ref[...])
    q2 = _dot(cqn.astype(bf16), wuq_ref[...])
    nq = H_C * QC_PAD
    qc_ref[...] = (q2[:, :nq] * cosq_ref[...] + q2[:, nq:] * sinq_ref[...]).astype(bf16)
    ckv_ref[...] = _rms(p[:, OFF_CKV:OFF_CKV + KV_LORA], gckv_ref[...]).astype(bf16)
    kr_ref[...] = kr.astype(bf16)


def _inproj_lat(x, lat_row0, g, shift, scale, w, gcq, gckv, wuq2, tabs):
    tm = TM_LAT_IN
    per_b = DEC_SEQ // tm
    row0 = lat_row0 // tm
    xrow = lambda i: (row0 + i, 0)
    row = lambda i: (i, 0)
    const = lambda i: (0, 0)
    grp = lambda i: (1 + i // per_b, 0, 0)
    pos = lambda i: (i % per_b, 0)
    cosa, sina, cosq, sinq, cosr, sinr = tabs
    widths = (W_QA, W_KA, W_VA, W_B, W_B, W_B, H_C * QC_PAD, KV_LORA, QK_ROPE)
    return pl.pallas_call(
        _inproj_lat_kernel,
        grid=(T_LAT // tm,),
        in_specs=[pl.BlockSpec((tm, D_MODEL), xrow),
                  pl.BlockSpec((1, D_MODEL), const),
                  pl.BlockSpec((1, 1, D_MODEL), grp),
                  pl.BlockSpec((1, 1, D_MODEL), grp),
                  pl.BlockSpec((D_MODEL, NW_LAT), const),
                  pl.BlockSpec((1, Q_LORA), const),
                  pl.BlockSpec((1, KV_LORA), const),
                  pl.BlockSpec((Q_LORA, 2 * H_C * QC_PAD), const),
                  pl.BlockSpec((tm, W_QA), pos), pl.BlockSpec((tm, W_QA), pos),
                  pl.BlockSpec((tm, H_C * QC_PAD), pos), pl.BlockSpec((tm, H_C * QC_PAD), pos),
                  pl.BlockSpec((tm, QK_ROPE), pos), pl.BlockSpec((tm, QK_ROPE), pos)],
        out_specs=[pl.BlockSpec((tm, wd), row) for wd in widths],
        out_shape=[jax.ShapeDtypeStruct((T_LAT, wd), bf16) for wd in widths],
        compiler_params=_cparams("arbitrary"),
        name="inproj_lat",
    )(x, g, shift, scale, w, gcq, gckv, wuq2, cosa, sina, cosq, sinq, cosr, sinr)


CTX_BATCHES = 2


def _ctx_attn_kernel(sink_ref, qa_ref, ka_ref, va_ref, qb_ref, kb_ref, vb_ref, qc_ref, ckv_ref, kr_ref,
                     wukv_ref, wout_ref, x_ref, gate_ref, o_ref, o_scr, s_scr, p_scr):
    n = SEQ
    scale = HEAD_DIM ** -0.5
    scale_c = (QK_NOPE + QK_ROPE) ** -0.5

    def one_batch(sb, carry):
        rows = pl.ds(pl.multiple_of(sb * n, n), n)
        ka = ka_ref[sb, 0].astype(bf16)
        va = va_ref[sb, 0].astype(bf16)
        kb = kb_ref[sb, 0].astype(bf16)
        vb = vb_ref[sb, 0].astype(bf16)
        kv = _dot(ckv_ref[sb, 0].astype(bf16), wukv_ref[...]).astype(bf16)
        kr = kr_ref[sb, 0].astype(bf16)
        for h in range(H_A):
            g = h // G_A
            q = qa_ref[rows, h * HEAD_DIM:(h + 1) * HEAD_DIM]
            s_scr[h * n:(h + 1) * n, :] = _dot_nt(q, ka[:, g * HEAD_DIM:(g + 1) * HEAD_DIM]) * scale
        for h in range(H_B):
            sl = slice(h * HEAD_DIM, (h + 1) * HEAD_DIM)
            s_scr[(H_A + h) * n:(H_A + h + 1) * n, :] = _dot_nt(qb_ref[rows, sl], kb[:, sl]) * scale
        for h in range(H_C):
            qn = qc_ref[rows, h * QC_PAD:h * QC_PAD + QK_NOPE]
            qr = qc_ref[rows, h * QC_PAD + QK_NOPE:h * QC_PAD + QK_NOPE + QK_ROPE]
            c0 = h * (QK_NOPE + V_C)
            r0 = (H_A + H_B + h) * n
            s_scr[r0:r0 + n, :] = (_dot_nt(qn, kv[:, c0:c0 + QK_NOPE]) + _dot_nt(qr, kr)) * scale_c
        for pair in range((H_A + H_B + H_C) // 2):
            h0 = 2 * pair
            sinks = ((sink_ref[h0], n), (sink_ref[h0 + 1], n)) if h0 < H_A else None
            _softmax_rows(s_scr, p_scr, slice(h0 * n, (h0 + 2) * n), sinks)
        for h in range(H_A):
            g = h // G_A
            o_scr[:, h * HEAD_DIM:(h + 1) * HEAD_DIM] = _dot(p_scr[h * n:(h + 1) * n, :],
                                                             va[:, g * HEAD_DIM:(g + 1) * HEAD_DIM])
        for h in range(H_B):
            sl = slice(h * HEAD_DIM, (h + 1) * HEAD_DIM)
            o_scr[:, W_QA + h * HEAD_DIM:W_QA + (h + 1) * HEAD_DIM] = _dot(
                p_scr[(H_A + h) * n:(H_A + h + 1) * n, :], vb[:, sl])
        for h in range(H_C):
            c0 = h * (QK_NOPE + V_C)
            r0 = (H_A + H_B + h) * n
            off = W_QA + W_B + h * V_C
            o_scr[:, off:off + V_C] = _dot(p_scr[r0:r0 + n, :], kv[:, c0 + QK_NOPE:c0 + QK_NOPE + V_C])
        y = _dot(o_scr[...].astype(bf16), wout_ref[...])
        o_ref[rows, :] = x_ref[rows, :] + gate_ref[0] * y
        return carry

    lax.fori_loop(0, CTX_BATCHES, one_batch, 0)


def _ctx_attn(layer, sink, qs, caches, wukv, wout, x, gate):
    qa, qb, qc = qs
    ka, va, kb, vb, ckv, kr = caches
    row = lambda b: (b, 0)
    const = lambda b: (0, 0)
    slot = lambda b: (b, layer, 0, 0)
    nrow = CTX_BATCHES * SEQ
    qspec = lambda a: pl.BlockSpec((nrow, a.shape[1]), row)
    cspec = lambda a: pl.BlockSpec((CTX_BATCHES, 1, SEQ, a.shape[3]), slot)
    in_specs = [pl.BlockSpec(memory_space=pltpu.SMEM),
                qspec(qa), cspec(ka), cspec(va), qspec(qb), cspec(kb), cspec(vb), qspec(qc), cspec(ckv), cspec(kr)]
    in_specs += [pl.BlockSpec((KV_LORA, H_C * (QK_NOPE + V_C)), const),
                 pl.BlockSpec((D_MODEL, D_MODEL), const),
                 pl.BlockSpec((nrow, D_MODEL), row),
                 pl.BlockSpec((1, 1, D_MODEL), lambda b: (0, 0, 0))]
    return pl.pallas_call(
        _ctx_attn_kernel,
        grid=(BATCH // CTX_BATCHES,),
        in_specs=in_specs,
        out_specs=pl.BlockSpec((nrow, D_MODEL), row),
        out_shape=jax.ShapeDtypeStruct((T_CTX, D_MODEL), f32),
        scratch_shapes=[pltpu.VMEM((SEQ, D_MODEL), f32),
                        pltpu.VMEM(((H_A + H_B + H_C) * SEQ, SEQ), f32),
                        pltpu.VMEM(((H_A + H_B + H_C) * SEQ, SEQ), bf16)],
        compiler_params=_cparams("arbitrary"),
        name="ctx_attn",
    )(sink, qa, ka, va, qb, kb, vb, qc, ckv, kr, wukv, wout, x, gate)


def _lat_attn_kernel(sink_ref, qa_ref, qb_ref, qc_ref, ka_ref, va_ref, kb_ref, vb_ref, ckv_ref, kr_ref,
                     cak_ref, cav_ref, cbk_ref, cbv_ref, cckv_ref, ckr_ref, bias_ref,
                     wukv_ref, wout_ref, x_ref, gate_ref, o_ref, o_scr, kv_scr, sa, pa, sb, pb, sc, pc):
    qi = pl.program_id(1)
    nb = DEC_SEQ // BLOCK
    scale = HEAD_DIM ** -0.5

    @pl.when(qi == 0)
    def _():
        kv_scr[0:DEC_SEQ, :] = _dot(ckv_ref[...], wukv_ref[...]).astype(bf16)
        kv_scr[DEC_SEQ:DEC_SEQ + PAST_LEN, :] = _dot(cckv_ref[0, 0].astype(bf16), wukv_ref[...]).astype(bf16)

    def blk(ref, j):
        idx = jnp.clip(qi + j, 0, nb - 1)
        return ref[pl.ds(pl.multiple_of(idx * BLOCK, BLOCK), BLOCK), :]

    ka = jnp.concatenate([blk(ka_ref, -1), blk(ka_ref, 0), blk(ka_ref, 1), cak_ref[0, 0].astype(bf16)], axis=0)
    va = jnp.concatenate([blk(va_ref, -1), blk(va_ref, 0), blk(va_ref, 1), cav_ref[0, 0].astype(bf16)], axis=0)
    nk_a = 3 * BLOCK + PAST_LEN
    r = lax.broadcasted_iota(jnp.int32, (BLOCK, nk_a), 0)
    c = lax.broadcasted_iota(jnp.int32, (BLOCK, nk_a), 1)
    valid = (((c < BLOCK) & (c >= r) & (qi > 0))
             | ((c >= BLOCK) & (c < 2 * BLOCK))
             | ((c >= 2 * BLOCK) & (c < 3 * BLOCK) & (c - 2 * BLOCK <= r) & (qi < nb - 1))
             | (c >= 3 * BLOCK))
    for h in range(H_A):
        g = h // G_A
        q = qa_ref[:, h * HEAD_DIM:(h + 1) * HEAD_DIM]
        s = _dot_nt(q, ka[:, g * HEAD_DIM:(g + 1) * HEAD_DIM]) * scale
        sa[h * BLOCK:(h + 1) * BLOCK, :] = jnp.where(valid, s, NEG)

    cbk = cbk_ref[0, 0].astype(bf16)
    cbv = cbv_ref[0, 0].astype(bf16)
    rows_per_blk = BLOCK // GRID_W
    nloc = NA_ROWS * GRID_W
    vcats = []
    for half in range(rows_per_blk):
        grow = qi * rows_per_blk + half
        start = jnp.clip(grow - NA_ROWS // 2, 0, ROWS - NA_ROWS)
        kloc = kb_ref[pl.ds(pl.multiple_of(start * GRID_W, GRID_W), nloc), :]
        vloc = vb_ref[pl.ds(pl.multiple_of(start * GRID_W, GRID_W), nloc), :]
        vcats.append(jnp.concatenate([vloc, cbv], axis=0))
        qrows = slice(half * GRID_W, (half + 1) * GRID_W)
        dr0 = start - grow + (NA_ROWS - 1)
        for h in range(H_B):
            sl = slice(h * HEAD_DIM, (h + 1) * HEAD_DIM)
            q = qb_ref[qrows, sl]
            bias = jnp.concatenate([bias_ref[h, dr0 + 2 * j] for j in range(NA_ROWS // 2)], axis=1)
            s_loc = _dot_nt(q, kloc[:, sl]) * scale + bias
            s_ctx = _dot_nt(q, cbk[:, sl]) * scale
            r0 = (half * H_B + h) * GRID_W
            sb[r0:r0 + GRID_W, :] = jnp.concatenate([s_loc, s_ctx], axis=1)

    kr = jnp.concatenate([kr_ref[...], ckr_ref[0, 0].astype(bf16)], axis=0)
    scale_c = (QK_NOPE + QK_ROPE) ** -0.5
    for h in range(H_C):
        qn = qc_ref[:, h * QC_PAD:h * QC_PAD + QK_NOPE]
        qr = qc_ref[:, h * QC_PAD + QK_NOPE:h * QC_PAD + QK_NOPE + QK_ROPE]
        c0 = h * (QK_NOPE + V_C)
        sc[h * BLOCK:(h + 1) * BLOCK, :] = (_dot_nt(qn, kv_scr[:, c0:c0 + QK_NOPE]) + _dot_nt(qr, kr)) * scale_c

    for pair in range(H_A // 2):
        h0 = 2 * pair
        _softmax_rows(sa, pa, slice(h0 * BLOCK, (h0 + 2) * BLOCK), ((sink_ref[h0], BLOCK), (sink_ref[h0 + 1], BLOCK)))
    for blk2 in range(rows_per_blk * H_B // 2):
        _softmax_rows(sb, pb, slice(blk2 * 2 * GRID_W, (blk2 + 1) * 2 * GRID_W))
    for h in range(H_C):
        _softmax_rows(sc, pc, slice(h * BLOCK, (h + 1) * BLOCK))

    for h in range(H_A):
        g = h // G_A
        o_scr[:, h * HEAD_DIM:(h + 1) * HEAD_DIM] = _dot(pa[h * BLOCK:(h + 1) * BLOCK, :],
                                                         va[:, g * HEAD_DIM:(g + 1) * HEAD_DIM])
    for half in range(rows_per_blk):
        qrows = slice(half * GRID_W, (half + 1) * GRID_W)
        for h in range(H_B):
            sl = slice(h * HEAD_DIM, (h + 1) * HEAD_DIM)
            r0 = (half * H_B + h) * GRID_W
            o_scr[qrows, W_QA + h * HEAD_DIM:W_QA + (h + 1) * HEAD_DIM] = _dot(pb[r0:r0 + GRID_W, :],
                                                                             vcats[half][:, sl])
    for h in range(H_C):
        c0 = h * (QK_NOPE + V_C)
        off = W_QA + W_B + h * V_C
        o_scr[:, off:off + V_C] = _dot(pc[h * BLOCK:(h + 1) * BLOCK, :], kv_scr[:, c0 + QK_NOPE:c0 + QK_NOPE + V_C])

    y = _dot(o_scr[...].astype(bf16), wout_ref[...])
    o_ref[...] = x_ref[...] + gate_ref[0] * y


def _lat_attn(layer, sink, proj, caches, bias_tab, wukv, wout, x, lat_row0, gate):
    qa, ka, va, qb, kb, vb, qc, ckv, kr = proj
    nb = DEC_SEQ // BLOCK
    qrow = lambda b, q: (b * nb + q, 0)
    xrow = lambda b, q: (lat_row0 // BLOCK + b * nb + q, 0)
    brow = lambda b, q: (b, 0)
    const = lambda b, q: (0, 0)
    cidx = lambda b, q: (b, layer, 0, 0)
    in_specs = [pl.BlockSpec(memory_space=pltpu.SMEM)]
    in_specs += [pl.BlockSpec((BLOCK, a.shape[1]), qrow) for a in (qa, qb, qc)]
    in_specs += [pl.BlockSpec((DEC_SEQ, a.shape[1]), brow) for a in (ka, va, kb, vb, ckv, kr)]
    in_specs += [pl.BlockSpec((1, 1, PAST_LEN, a.shape[3]), cidx) for a in caches]
    in_specs += [pl.BlockSpec(bias_tab.shape, lambda b, q: (0, 0, 0, 0)),
                 pl.BlockSpec((KV_LORA, H_C * (QK_NOPE + V_C)), const),
                 pl.BlockSpec((D_MODEL, D_MODEL), const),
                 pl.BlockSpec((BLOCK, D_MODEL), xrow),
                 pl.BlockSpec((1, 1, D_MODEL), lambda b, q: (1 + b, 0, 0))]
    return pl.pallas_call(
        _lat_attn_kernel,
        grid=(DEC_BATCH, nb),
        in_specs=in_specs,
        out_specs=pl.BlockSpec((BLOCK, D_MODEL), qrow),
        out_shape=jax.ShapeDtypeStruct((T_LAT, D_MODEL), f32),
        scratch_shapes=[pltpu.VMEM((BLOCK, D_MODEL), f32),
                        pltpu.VMEM((DEC_SEQ + PAST_LEN, H_C * (QK_NOPE + V_C)), bf16)]
        + [pltpu.VMEM(shape, dt) for shape in ((H_A * BLOCK, 3 * BLOCK + PAST_LEN),
                                               (H_B * BLOCK, NA_ROWS * GRID_W + PAST_LEN),
                                               (H_C * BLOCK, DEC_SEQ + PAST_LEN)) for dt in (f32, bf16)],
        compiler_params=_cparams("arbitrary", "arbitrary"),
        name="lat_attn",
    )(sink, qa, qb, qc, ka, va, kb, vb, ckv, kr, *caches, bias_tab, wukv, wout, x, gate)


def _pick_stream(xc_ref, xl_ref, x_scr):
    i = pl.program_id(0)

    @pl.when(i < T_CTX // TM_TOK)
    def _():
        x_scr[...] = xc_ref[...]

    @pl.when(i >= T_CTX // TM_TOK)
    def _():
        x_scr[...] = xl_ref[...]

    return x_scr[...]


def _stream_specs(lat_row0):
    n_ctx = T_CTX // TM_TOK
    return [pl.BlockSpec((TM_TOK, D_MODEL), lambda i: (jnp.minimum(i, n_ctx - 1), 0)),
            pl.BlockSpec((TM_TOK, D_MODEL), lambda i: (lat_row0 // TM_TOK + jnp.maximum(i - n_ctx, 0), 0))]


def _router_kernel(xc_ref, xl_ref, g_ref, sh_ref, sc_ref, wr_ref, br_ref, h_ref, e_ref, gt_ref, x_scr):
    h = _rms(_pick_stream(xc_ref, xl_ref, x_scr), g_ref[...]) * (1.0 + sc_ref[0]) + sh_ref[0]
    _store_row_tiles(h_ref, h)
    h_hi = h.astype(bf16)
    h_lo = (h - h_hi.astype(f32)).astype(bf16)
    w = wr_ref[...]
    w_hi = w.astype(bf16)
    w_lo = (w - w_hi.astype(f32)).astype(bf16)
    logits = _dot(h_hi, w_hi) + _dot(h_hi, w_lo) + _dot(h_lo, w_hi) + br_ref[...]
    lane = lax.broadcasted_iota(jnp.int32, logits.shape, 1).astype(f32)
    l = jnp.where(lane < N_EXPERTS, logits, -jnp.inf)
    tops, idxs = [], []
    for _ in range(TOP_K):
        m = jnp.max(l, axis=-1, keepdims=True)
        idx = jnp.min(jnp.where(l == m, lane, float(LANE)), axis=-1, keepdims=True)
        tops.append(m)
        idxs.append(idx)
        l = jnp.where(lane == idx, -jnp.inf, l)
    ex = [jnp.exp(t - tops[0]) for t in tops]
    den = ex[0] + ex[1] + ex[2] + ex[3]
    e_out = jnp.zeros(logits.shape, f32)
    g_out = jnp.zeros(logits.shape, f32)
    for k in range(TOP_K):
        e_out = jnp.where(lane == k, idxs[k], e_out)
        g_out = jnp.where(lane == k, ex[k] / den, g_out)
    e_ref[...] = e_out.astype(jnp.int32)
    gt_ref[...] = g_out


def _group_of_tile(i):
    per_b = DEC_SEQ // TM_TOK
    n_ctx = T_CTX // TM_TOK
    return jnp.where(i < n_ctx, 0, 1 + (i - n_ctx) // per_b)


def _router(xc, xl, lat_row0, g, shift, scale, wr, br):
    tm = TM_TOK
    row = lambda i: (i, 0)
    const = lambda i: (0, 0)
    grp = lambda i: (_group_of_tile(i), 0, 0)
    return pl.pallas_call(
        _router_kernel,
        grid=(T_ALL // tm,),
        in_specs=_stream_specs(lat_row0) +
                 [pl.BlockSpec((1, D_MODEL), const),
                  pl.BlockSpec((1, 1, D_MODEL), grp),
                  pl.BlockSpec((1, 1, D_MODEL), grp),
                  pl.BlockSpec((D_MODEL, LANE), const),
                  pl.BlockSpec((1, LANE), const)],
        out_specs=[pl.BlockSpec((tm * ROW_TILE, LANE), row), pl.BlockSpec((tm, LANE), row),
                   pl.BlockSpec((tm, LANE), row)],
        out_shape=[jax.ShapeDtypeStruct((T_ALL * ROW_TILE, LANE), f32),
                   jax.ShapeDtypeStruct((T_ALL, LANE), jnp.int32),
                   jax.ShapeDtypeStruct((T_ALL, LANE), f32)],
        scratch_shapes=[pltpu.VMEM((tm, D_MODEL), f32)],
        compiler_params=_cparams("arbitrary"),
        name="router",
    )(xc, xl, g, shift, scale, wr, br)


def _dispatch_kernel(tok_ref, nu_ref, h_hbm, o_ref, hv, xg, hsem):
    tm = TM_MOE
    i = pl.program_id(0)

    @pl.when(i == 0)
    def _():
        resident = pltpu.make_async_copy(h_hbm, hv, hsem.at[0])
        resident.start()
        resident.wait()

    def one_block(sub, carry):
        blk = i * DISPATCH_BLOCKS + sub
        rows = pl.ds(pl.multiple_of(sub * tm, tm), tm)

        @pl.when(blk < nu_ref[0])
        def _():
            for r in range(tm):
                t = tok_ref[blk * tm + r]
                xg[pl.ds(r, ROW_TILE, stride=tm + 1), :] = hv[pl.ds(pl.multiple_of(t * ROW_TILE, ROW_TILE),
                                                                 ROW_TILE), :]
            o_ref[rows, :] = jnp.concatenate([xg[pl.ds(c * (tm + 1), tm), :] for c in range(ROW_TILE)],
                                             axis=1).astype(bf16)

        @pl.when(blk >= nu_ref[0])
        def _():
            o_ref[rows, :] = jnp.zeros((tm, D_MODEL), bf16)

        return carry

    lax.fori_loop(0, DISPATCH_BLOCKS, one_block, 0)


def _dispatch(row_tok, n_used, h):
    tm = TM_MOE
    return pl.pallas_call(
        _dispatch_kernel,
        grid_spec=pltpu.PrefetchScalarGridSpec(
            num_scalar_prefetch=2,
            grid=(N_MOE_BLOCKS // DISPATCH_BLOCKS,),
            in_specs=[pl.BlockSpec(memory_space=pl.ANY)],
            out_specs=pl.BlockSpec((DISPATCH_BLOCKS * tm, D_MODEL), lambda i, tok, nu: (i, 0)),
            scratch_shapes=[pltpu.VMEM((T_ALL * ROW_TILE, LANE), f32), pltpu.VMEM(((tm + 1) * ROW_TILE, LANE), f32),
                            pltpu.SemaphoreType.DMA((1,))]),
        out_shape=jax.ShapeDtypeStruct((N_MOE_BLOCKS * tm, D_MODEL), bf16),
        compiler_params=_cparams("arbitrary"),
        name="dispatch",
    )(row_tok, n_used, h)


def _moe_kernel(layer, be_ref, nu_ref, nxt_ref, dst_ref, x_ref, wgu_hbm, bgu_ref, wd_hbm, bd_ref, y_hbm,
                y0, y1, wgu_st, wd_st, wgu_bf, wd_bf, wsem, ssem):
    tm = TM_MOE
    i = pl.program_id(0)
    nb = pl.num_programs(0)
    used = i < nu_ref[0]
    yb = (y0, y1)

    def out_tile(row):
        return pl.ds(pl.multiple_of(row * ROW_TILE, ROW_TILE), ROW_TILE)

    def scatter_desc(buf, r, dst_row, s):
        return pltpu.make_async_copy(buf.at[out_tile(r)], y_hbm.at[out_tile(dst_row)], ssem.at[s])

    def scatter_wait(s):
        pltpu.make_async_copy(yb[s], y_hbm.at[pl.ds(0, tm * ROW_TILE)], ssem.at[s]).wait()

    def scatter_start(blk, s, unrolled):
        if unrolled:
            for r in range(tm):
                scatter_desc(yb[s], r, dst_ref[(blk + 1) * tm + r], s).start(priority=r % 2)
        else:
            def body(r, carry):
                scatter_desc(yb[s], r, dst_ref[(blk + 1) * tm + r], s).start()
                return carry
            lax.fori_loop(0, tm, body, 0, unroll=8)

    def weight_copies(e):
        return (pltpu.make_async_copy(wgu_hbm.at[layer, e], wgu_st, wsem.at[0]),
                pltpu.make_async_copy(wd_hbm.at[layer, e], wd_st, wsem.at[1]))

    @pl.when(i == 0)
    def _():
        for s in range(2):
            yb[s][...] = jnp.zeros_like(yb[s])
            dummy = pltpu.make_async_copy(yb[s], y_hbm.at[pl.ds((N_ASSIGN + s * tm) * ROW_TILE, tm * ROW_TILE)],
                                          ssem.at[s])
            dummy.start()
            dummy.wait()
        for cp in weight_copies(be_ref[0]):
            cp.start()

    first = jnp.logical_and(used, jnp.logical_or(i == 0, be_ref[i] != be_ref[jnp.maximum(i - 1, 0)]))

    @pl.when(first)
    def _():
        for cp in weight_copies(0):
            cp.wait()
        wgu_bf[...] = wgu_st[...].astype(bf16)
        wd_bf[...] = wd_st[...].astype(bf16)

        @pl.when(nxt_ref[i] >= 0)
        def _():
            for cp in weight_copies(nxt_ref[i]):
                cp.start()

    def step(par):
        cur, oth = par, 1 - par

        @pl.when(jnp.logical_and(i >= 1, i - 2 < nu_ref[0]))
        def _():
            scatter_wait(cur)

        @pl.when(used)
        def _():
            scatter_start(i - 1, oth, unrolled=True)
            gu = _dot(x_ref[...], wgu_bf[...]) + bgu_ref[0, 0]
            x_glu = jnp.minimum(gu[:, :D_FF], SWIGLU_LIMIT)
            x_lin = jnp.clip(gu[:, D_FF:], -SWIGLU_LIMIT, SWIGLU_LIMIT)
            act = x_glu * jax.nn.sigmoid(SWIGLU_ALPHA * x_glu) * (x_lin + 1.0)
            _store_row_tiles(yb[cur], _dot(act.astype(bf16), wd_bf[...]) + bd_ref[0, 0])

        flush = jnp.logical_and(jnp.logical_not(used), i - 1 < nu_ref[0])

        @pl.when(flush)
        def _():
            scatter_start(i - 1, oth, unrolled=False)

        @pl.when(jnp.logical_and(flush, i == nb - 1))
        def _():
            scatter_wait(oth)

    @pl.when(i % 2 == 0)
    def _():
        step(0)

    @pl.when(i % 2 == 1)
    def _():
        step(1)


def _moe(layer, routing, h, w_gu, b_gu, w_down, b_down):
    tm = TM_MOE
    block_e, n_used, nxt_e, row_tok, row_dst = routing
    xs = _dispatch(row_tok, n_used, h)
    ex4 = lambda i, be, nu, nxt, dst: (layer, be[i], 0, 0)
    return pl.pallas_call(
        functools.partial(_moe_kernel, layer),
        grid_spec=pltpu.PrefetchScalarGridSpec(
            num_scalar_prefetch=4,
            grid=(N_MOE_BLOCKS,),
            in_specs=[pl.BlockSpec((tm, D_MODEL), lambda i, be, nu, nxt, dst: (i, 0)),
                      pl.BlockSpec(memory_space=pl.ANY),
                      pl.BlockSpec((1, 1, 1, 2 * D_FF), ex4),
                      pl.BlockSpec(memory_space=pl.ANY),
                      pl.BlockSpec((1, 1, 1, D_MODEL), ex4)],
            out_specs=pl.BlockSpec(memory_space=pl.ANY),
            scratch_shapes=[pltpu.VMEM((tm * ROW_TILE, LANE), f32), pltpu.VMEM((tm * ROW_TILE, LANE), f32),
                            pltpu.VMEM((D_MODEL, 2 * D_FF), f32), pltpu.VMEM((D_FF, D_MODEL), f32),
                            pltpu.VMEM((D_MODEL, 2 * D_FF), bf16), pltpu.VMEM((D_FF, D_MODEL), bf16),
                            pltpu.SemaphoreType.DMA((2,)), pltpu.SemaphoreType.DMA((2,))]),
        out_shape=jax.ShapeDtypeStruct(((N_ASSIGN + 2 * tm) * ROW_TILE, LANE), f32),
        compiler_params=_cparams("arbitrary"),
        name="moe",
    )(block_e, n_used, nxt_e, row_dst, xs, w_gu, b_gu.reshape(DEPTH, N_EXPERTS, 1, 2 * D_FF),
      w_down, b_down.reshape(DEPTH, N_EXPERTS, 1, D_MODEL))


def _combine_kernel(final, xc_ref, xl_ref, y0_ref, y1_ref, y2_ref, y3_ref, gt_ref, gate_ref, gf_ref, *rest):
    x_scr = rest[-1]
    gt = gt_ref[...]
    f = gt[:, 0:1] * _load_row_tiles(y0_ref)
    for k, y_ref in ((1, y1_ref), (2, y2_ref), (3, y3_ref)):
        f = f + gt[:, k:k + 1] * _load_row_tiles(y_ref)
    out = _pick_stream(xc_ref, xl_ref, x_scr) + gate_ref[0] * f
    if not final:
        rest[0][...] = out
        return
    out = _rms(out, gf_ref[...])
    oc_ref, ol_ref = rest[0], rest[1]
    i = pl.program_id(0)

    @pl.when(i < T_CTX // TM_TOK)
    def _():
        oc_ref[...] = out

    @pl.when(i >= T_CTX // TM_TOK)
    def _():
        ol_ref[...] = out


def _combine(final, xc, xl, lat_row0, y, gates, gate, g_final):
    tm = TM_TOK
    nt = T_ALL // tm
    n_ctx = T_CTX // tm
    row = lambda i: (i, 0)
    const = lambda i: (0, 0)
    grp = lambda i: (_group_of_tile(i), 0, 0)
    ysel = [pl.BlockSpec((tm * ROW_TILE, LANE), functools.partial(lambda k, i: (k * nt + i, 0), k))
            for k in range(TOP_K)]
    if final:
        out_specs = [pl.BlockSpec((tm, D_MODEL), lambda i: (jnp.minimum(i, n_ctx - 1), 0)),
                     pl.BlockSpec((tm, D_MODEL), lambda i: (jnp.maximum(i - n_ctx, 0), 0))]
        out_shape = [jax.ShapeDtypeStruct((T_CTX, D_MODEL), f32), jax.ShapeDtypeStruct((T_LAT, D_MODEL), f32)]
    else:
        out_specs = pl.BlockSpec((tm, D_MODEL), row)
        out_shape = jax.ShapeDtypeStruct((T_ALL, D_MODEL), f32)
    return pl.pallas_call(
        functools.partial(_combine_kernel, final),
        grid=(nt,),
        in_specs=_stream_specs(lat_row0) + ysel +
                 [pl.BlockSpec((tm, LANE), row),
                  pl.BlockSpec((1, 1, D_MODEL), grp),
                  pl.BlockSpec((1, D_MODEL), const)],
        out_specs=out_specs,
        out_shape=out_shape,
        scratch_shapes=[pltpu.VMEM((tm, D_MODEL), f32)],
        compiler_params=_cparams("arbitrary"),
        name="combine",
    )(xc, xl, y, y, y, y, gates, gate, g_final)


def _rope_head_tables(d):
    nf = d // 4
    half = d // 2
    t = np.arange(DEC_SEQ)
    inv = ROPE_BASE ** (-np.arange(nf, dtype=np.float32) / nf)
    i = np.arange(d)
    pos = np.where(i[None, :] < half, (t // GRID_W)[:, None], (t % GRID_W)[:, None]).astype(np.float32)
    ang = pos * inv[i % nf][None, :].astype(np.float32)
    first = (i % half) < nf
    cos = np.cos(ang)
    sin = np.where(first[None, :], -np.sin(ang), np.sin(ang))
    partner = np.where(first, i + nf, i - nf)
    return cos.astype(np.float32), sin.astype(np.float32), partner


def _rope_tables():
    cos64, sin64, _ = _rope_head_tables(HEAD_DIM)
    cos32, sin32, _ = _rope_head_tables(QK_ROPE)
    cosa = np.tile(cos64, (1, H_A))
    sina = np.tile(sin64, (1, H_A))
    cosq1 = np.concatenate([np.ones((DEC_SEQ, QK_NOPE), np.float32), cos32,
                            np.ones((DEC_SEQ, QC_PAD - QK_NOPE - QK_ROPE), np.float32)], axis=1)
    sinq1 = np.concatenate([np.zeros((DEC_SEQ, QK_NOPE), np.float32), sin32,
                            np.zeros((DEC_SEQ, QC_PAD - QK_NOPE - QK_ROPE), np.float32)], axis=1)
    cosq = np.tile(cosq1, (1, H_C))
    sinq = np.tile(sinq1, (1, H_C))
    return tuple(jnp.asarray(a) for a in (cosa, sina, cosq, sinq, cos32, sin32))


def _pad_cols(w, n):
    return jnp.pad(w, ((0, 0), (0, n - w.shape[1])))


def _layer_weights(w_in, w_uq):
    cuts = np.cumsum((W_QA, W_KA, W_VA, W_B, W_B, W_B, Q_LORA, KV_LORA, QK_ROPE))[:-1]
    qa, ka, va, qb, kb, vb, cq, ckv, kr = jnp.split(w_in, [int(c) for c in cuts], axis=1)
    _, _, p64 = _rope_head_tables(HEAD_DIM)
    _, _, p32 = _rope_head_tables(QK_ROPE)
    pa = np.concatenate([h * HEAD_DIM + p64 for h in range(H_A)])
    base = jnp.concatenate([qa, ka, va, _pad_cols(qb, 384), _pad_cols(kb, 384), _pad_cols(vb, 384), cq, ckv,
                            _pad_cols(kr, 128)], axis=1)
    w_ctx = base.astype(bf16)
    w_lat = jnp.concatenate([base, qa[:, pa], ka[:, pa[:W_KA]], _pad_cols(kr[:, p32], 128)], axis=1).astype(bf16)
    hq = QK_NOPE + QK_ROPE
    heads = [_pad_cols(w_uq[:, h * hq:(h + 1) * hq], QC_PAD) for h in range(H_C)]
    pq = np.concatenate([np.arange(QK_NOPE), QK_NOPE + p32])
    heads_p = [_pad_cols(w_uq[:, h * hq:(h + 1) * hq][:, pq], QC_PAD) for h in range(H_C)]
    wuq = jnp.concatenate(heads, axis=1).astype(bf16)
    wuq2 = jnp.concatenate(heads + heads_p, axis=1).astype(bf16)
    return w_ctx, w_lat, wuq, wuq2


def _bias_table(rpb):
    col = np.arange(GRID_W)
    col_start = np.clip(col - NA_COLS // 2, 0, GRID_W - NA_COLS)
    col_ok = (col[None, :] >= col_start[:, None]) & (col[None, :] < col_start[:, None] + NA_COLS)
    dc = np.clip(col[None, :] - col[:, None] + (NA_COLS - 1), 0, 2 * NA_COLS - 2)
    onehot = (dc[None] == np.arange(2 * NA_COLS - 1)[:, None, None]).astype(np.float32)
    expanded = jnp.einsum('hrd,dqk->hrqk', rpb.astype(f32), jnp.asarray(onehot), precision=lax.Precision.HIGHEST)
    blocks = jnp.where(col_ok[None, None], expanded, NEG)
    return jnp.concatenate([blocks[:, :-1], blocks[:, 1:]], axis=-1)


def _routing(top_e):
    tm = TM_MOE
    key_bits = 16
    pad_mark = (1 << key_bits) - 1
    flat_e = top_e.T.reshape(N_ASSIGN)
    experts = jnp.arange(N_EXPERTS, dtype=jnp.int32)
    counts = jnp.sum((flat_e[:, None] == experts[None, :]).astype(jnp.int32), axis=0)
    nblk = (counts + tm - 1) // tm
    blk_end = jnp.cumsum(nblk)
    pad_end = jnp.cumsum(nblk * tm - counts)
    slots = jnp.arange(N_MOE_BLOCKS * tm - N_ASSIGN, dtype=jnp.int32)
    pad_e = jnp.sum((pad_end[None, :] <= slots[:, None]).astype(jnp.int32), axis=1)
    keys = jnp.concatenate([(flat_e << key_bits) + jnp.arange(N_ASSIGN, dtype=jnp.int32),
                            (pad_e << key_bits) + pad_mark])
    asg = (jnp.sort(keys, stable=False) & pad_mark).reshape(N_MOE_BLOCKS, tm)
    valid = asg != pad_mark
    blocks = jnp.arange(N_MOE_BLOCKS, dtype=jnp.int32)
    r = jnp.arange(tm, dtype=jnp.int32)[None, :]
    tok = jnp.where(valid, asg % T_ALL, 0)
    row_dst = jnp.where(valid, asg, N_ASSIGN + (blocks[:, None] % 2) * tm + r)
    row_dst = jnp.concatenate([N_ASSIGN + tm + r, row_dst], axis=0).reshape(-1)
    block_e = jnp.minimum(jnp.sum((blk_end[None, :] <= blocks[:, None]).astype(jnp.int32), axis=1), N_EXPERTS - 1)
    n_used = blk_end[-1].astype(jnp.int32).reshape(1)
    has = jnp.where(counts > 0, experts, N_EXPERTS)
    later = experts[None, :] > experts[:, None]
    nxt = jnp.min(jnp.where(later, has[None, :], N_EXPERTS), axis=1)
    nxt = jnp.where(nxt >= N_EXPERTS, -1, nxt)
    sel = (block_e[:, None] == experts[None, :]).astype(jnp.int32)
    nxt_e = jnp.sum(sel * nxt[None, :], axis=1)
    i32 = lambda a: a.astype(jnp.int32)
    return i32(block_e), n_used, i32(nxt_e), i32(tok).reshape(-1), i32(row_dst)


def kernel(x_prompt, x_sample, cache_a_k, cache_a_v, cache_b_k, cache_b_v, cache_c_kv, cache_c_kr, c, c_ctx, w_ada, b_ada, g_attn, g_ffn, w_in, sink_a, rpb_b, g_cq, g_ckv, w_uq, w_ukv, w_out, w_router, b_router, w_gu, b_gu, w_down, b_down, g_final):
    xc, xl, lat_row0 = x_prompt.reshape(T_CTX, D_MODEL), x_sample.reshape(T_LAT, D_MODEL), 0
    cvec = jnp.concatenate([c_ctx[None, :], c, jnp.zeros((8 - N_GROUPS, D_MODEL), f32)], axis=0)
    mods = _ada(cvec, w_ada, b_ada)[:, :N_GROUPS].reshape(DEPTH, N_GROUPS, 6, 1, D_MODEL)
    tabs = _rope_tables()
    caches = (cache_a_k.reshape(DEC_BATCH, DEPTH, PAST_LEN, W_KA), cache_a_v.reshape(DEC_BATCH, DEPTH, PAST_LEN, W_VA),
              cache_b_k.reshape(DEC_BATCH, DEPTH, PAST_LEN, W_B), cache_b_v.reshape(DEC_BATCH, DEPTH, PAST_LEN, W_B),
              cache_c_kv, cache_c_kr)
    new = None
    for layer in range(DEPTH):
        m = [mods[layer, :, j] for j in range(6)]
        w_ctx, w_lat, wuq, wuq2 = _layer_weights(w_in[layer], w_uq[layer])
        wukv = w_ukv[layer].astype(bf16)
        wout = w_out[layer].astype(bf16)
        g1 = g_attn[layer][None, :]
        gcq = g_cq[layer][None, :]
        gckv = g_ckv[layer][None, :]
        sink = sink_a[layer]

        qs, new = _inproj_ctx(layer, new, xc, g1, m[0], m[1], w_ctx, gcq, gckv, wuq)
        x_ctx = _ctx_attn(layer, sink, qs, new, wukv, wout, xc, m[2])

        plat = _inproj_lat(xl, lat_row0, g1, m[0], m[1], w_lat, gcq, gckv, wuq2, tabs)
        x_lat = _lat_attn(layer, sink, plat, caches, _bias_table(rpb_b[layer]), wukv, wout, xl, lat_row0, m[2])

        wr = _pad_cols(w_router[layer], LANE)
        br = _pad_cols(b_router[layer][None, :], LANE)
        h2, top_e, gates = _router(x_ctx, x_lat, 0, g_ffn[layer][None, :], m[3], m[4], wr, br)
        y = _moe(layer, _routing(top_e[:, :TOP_K]), h2, w_gu, b_gu, w_down, b_down)
        x = _combine(layer == DEPTH - 1, x_ctx, x_lat, 0, y, gates, m[5], g_final[None, :])
        xc, xl, lat_row0 = x, x, T_CTX

    y_prompt = x[0].reshape(BATCH, SEQ, D_MODEL)
    y_sample = x[1].reshape(DEC_BATCH, DEC_SEQ, D_MODEL)
    shapes = ((KV_A, HEAD_DIM), (KV_A, HEAD_DIM), (H_B, HEAD_DIM), (H_B, HEAD_DIM), (KV_LORA,), (QK_ROPE,))
    outs = [a.reshape((BATCH, DEPTH, SEQ) + s) for a, s in zip(new, shapes)]
    return (y_prompt, y_sample, *outs)
```

```python
import functools

import numpy as np
import jax
import jax.numpy as jnp
from jax import lax
from jax.experimental import pallas as pl
from jax.experimental.pallas import tpu as pltpu

D_MODEL = 1024
BATCH = 32
SEQ = 256
DEPTH = 2
DEC_BATCH = 2
DEC_SEQ = 1024
PAST_LEN = 512
GRID_W = 64
HEAD_DIM = 64
H_A = 6
KV_A = 2
G_A = H_A // KV_A
WINDOW = 128
BLOCK = 128
H_B = 5
NA_ROWS = 8
NA_COLS = 16
H_C = 5
Q_LORA = 384
KV_LORA = 256
QK_NOPE = 64
QK_ROPE = 32
V_C = 64
N_EXPERTS = 32
TOP_K = 4
D_FF = 1024
SWIGLU_ALPHA = 1.702
SWIGLU_LIMIT = 7.0
ROPE_BASE = 10000.0
EPS = 1e-6
NEG = -1e30

T_CTX = BATCH * SEQ
T_LAT = DEC_BATCH * DEC_SEQ
T_ALL = T_CTX + T_LAT
N_GROUPS = 1 + DEC_BATCH
LANE = 128
QC_PAD = 128
ROWS = DEC_SEQ // GRID_W

W_QA, W_KA, W_VA = H_A * HEAD_DIM, KV_A * HEAD_DIM, KV_A * HEAD_DIM
W_B = H_B * HEAD_DIM
OFF_QA = 0
OFF_KA = 384
OFF_VA = 512
OFF_QB = 640
OFF_KB = 1024
OFF_VB = 1408
OFF_CQ = 1792
OFF_CKV = 2176
OFF_KR = 2432
NW_CTX = 2560
OFF_QA_P = 2560
OFF_KA_P = 2944
OFF_KR_P = 3072
NW_LAT = 3200

TM_TOK = 512
TM_LAT_IN = 512
TM_MOE = 256
N_ASSIGN = T_ALL * TOP_K
N_MOE_BLOCKS = N_ASSIGN // TM_MOE + N_EXPERTS
DISPATCH_BLOCKS = 8
VMEM_LIMIT = 56 * 1024 * 1024

f32 = jnp.float32
bf16 = jnp.bfloat16


def _cparams(*sem):
    return pltpu.CompilerParams(dimension_semantics=sem, vmem_limit_bytes=VMEM_LIMIT)


def _rms(xf, g):
    return xf * lax.rsqrt(jnp.mean(xf * xf, axis=-1, keepdims=True) + EPS) * g


def _dot(a, b):
    return jnp.dot(a, b, preferred_element_type=f32)


def _dot_nt(a, b):
    return lax.dot_general(a, b, (((1,), (1,)), ((), ())), preferred_element_type=f32)


ROW_TILE = D_MODEL // LANE


def _store_row_tiles(ref, val):
    n = val.shape[0]
    for c in range(ROW_TILE):
        ref[pl.ds(c, n, stride=ROW_TILE), :] = val[:, c * LANE:(c + 1) * LANE]


def _load_row_tiles(ref):
    n = ref.shape[0] // ROW_TILE
    return jnp.concatenate([ref[pl.ds(c, n, stride=ROW_TILE), :] for c in range(ROW_TILE)], axis=1)


def _softmax_rows(s_ref, p_ref, rows, sinks=None):
    s = s_ref[rows, :]
    m = jnp.max(s, axis=-1, keepdims=True)
    if sinks is not None:
        sink = jnp.concatenate([jnp.full((n, 1), v, f32) for v, n in sinks], axis=0)
        m = jnp.maximum(m, sink)
    p = jnp.exp(s - m)
    l = jnp.sum(p, axis=-1, keepdims=True)
    if sinks is not None:
        l = l + jnp.exp(sink - m)
    p_ref[rows, :] = (p * (1.0 / l)).astype(bf16)


def _ada_kernel(c_ref, w_ref, b_ref, o_ref):
    c = c_ref[...]
    s = c * jax.nn.sigmoid(c)
    s_hi = s.astype(bf16)
    s_lo = (s - s_hi.astype(f32)).astype(bf16)
    w = w_ref[0]
    w_hi = w.astype(bf16)
    w_lo = (w - w_hi.astype(f32)).astype(bf16)
    o_ref[0] = _dot(s_hi, w_hi) + _dot(s_hi, w_lo) + _dot(s_lo, w_hi) + b_ref[0]


def _ada(cvec, w_ada, b_ada):
    tn = 1536
    return pl.pallas_call(
        _ada_kernel,
        grid=(DEPTH, 6 * D_MODEL // tn),
        in_specs=[pl.BlockSpec((8, D_MODEL), lambda l, j: (0, 0)),
                  pl.BlockSpec((1, D_MODEL, tn), lambda l, j: (l, 0, j)),
                  pl.BlockSpec((1, 1, tn), lambda l, j: (l, 0, j))],
        out_specs=pl.BlockSpec((1, 8, tn), lambda l, j: (l, 0, j)),
        out_shape=jax.ShapeDtypeStruct((DEPTH, 8, 6 * D_MODEL), f32),
        compiler_params=_cparams("arbitrary", "arbitrary"),
        name="ada",
    )(cvec, w_ada, b_ada.reshape(DEPTH, 1, 6 * D_MODEL))


CACHE_WIDTHS = (W_KA, W_VA, W_B, W_B, KV_LORA, QK_ROPE)


def _inproj_ctx_kernel(layer, x_ref, g_ref, sh_ref, sc_ref, w_ref, gcq_ref, gckv_ref, wuq_ref, *refs):
    qa_ref, qb_ref, qc_ref, ka_ref, va_ref, kb_ref, vb_ref, ckv_ref, kr_ref = refs[-9:]
    h = _rms(x_ref[...], g_ref[...]) * (1.0 + sc_ref[0]) + sh_ref[0]
    p = _dot(h.astype(bf16), w_ref[...])
    qa_ref[...] = p[:, OFF_QA:OFF_QA + W_QA].astype(bf16)
    qb_ref[...] = p[:, OFF_QB:OFF_QB + W_B].astype(bf16)
    cqn = _rms(p[:, OFF_CQ:OFF_CQ + Q_LORA], gcq_ref[...])
    qc_ref[...] = _dot(cqn.astype(bf16), wuq_ref[...]).astype(bf16)
    caches = ((ka_ref, p[:, OFF_KA:OFF_KA + W_KA]), (va_ref, p[:, OFF_VA:OFF_VA + W_VA]),
              (kb_ref, p[:, OFF_KB:OFF_KB + W_B]), (vb_ref, p[:, OFF_VB:OFF_VB + W_B]),
              (ckv_ref, _rms(p[:, OFF_CKV:OFF_CKV + KV_LORA], gckv_ref[...])),
              (kr_ref, p[:, OFF_KR:OFF_KR + QK_ROPE]))
    for ref, val in caches:
        for b in range(TM_TOK // SEQ):
            rows = val[b * SEQ:(b + 1) * SEQ]
            if layer == 0:
                ref[b, 0] = rows
                for later in range(1, DEPTH):
                    ref[b, later] = jnp.zeros_like(rows)
            else:
                ref[b, 0] = rows


def _inproj_ctx(layer, prev_caches, x, g, shift, scale, w, gcq, gckv, wuq):
    tm = TM_TOK
    nb = tm // SEQ
    row = lambda i: (i, 0)
    const = lambda i: (0, 0)
    in_specs = [pl.BlockSpec((tm, D_MODEL), row),
                pl.BlockSpec((1, D_MODEL), const),
                pl.BlockSpec((1, 1, D_MODEL), lambda i: (0, 0, 0)),
                pl.BlockSpec((1, 1, D_MODEL), lambda i: (0, 0, 0)),
                pl.BlockSpec((D_MODEL, NW_CTX), const),
                pl.BlockSpec((1, Q_LORA), const),
                pl.BlockSpec((1, KV_LORA), const),
                pl.BlockSpec((Q_LORA, H_C * QC_PAD), const)]
    q_widths = (W_QA, W_B, H_C * QC_PAD)
    out_specs = [pl.BlockSpec((tm, wd), row) for wd in q_widths]
    out_shape = [jax.ShapeDtypeStruct((T_CTX, wd), bf16) for wd in q_widths]
    if layer == 0:
        out_specs += [pl.BlockSpec((nb, DEPTH, SEQ, wd), lambda i: (i, 0, 0, 0)) for wd in CACHE_WIDTHS]
        aliases, extra = {}, ()
    else:
        in_specs += [pl.BlockSpec(memory_space=pl.ANY) for _ in CACHE_WIDTHS]
        out_specs += [pl.BlockSpec((nb, 1, SEQ, wd), lambda i: (i, layer, 0, 0)) for wd in CACHE_WIDTHS]
        aliases = {8 + j: len(q_widths) + j for j in range(len(CACHE_WIDTHS))}
        extra = tuple(prev_caches)
    out_shape += [jax.ShapeDtypeStruct((BATCH, DEPTH, SEQ, wd), f32) for wd in CACHE_WIDTHS]
    outs = pl.pallas_call(
        functools.partial(_inproj_ctx_kernel, layer),
        grid=(T_CTX // tm,),
        in_specs=in_specs,
        out_specs=out_specs,
        out_shape=out_shape,
        input_output_aliases=aliases,
        compiler_params=_cparams("arbitrary"),
        name="inproj_ctx",
    )(x, g, shift, scale, w, gcq, gckv, wuq, *extra)
    return outs[:3], outs[3:]


def _inproj_lat_kernel(x_ref, g_ref, sh_ref, sc_ref, w_ref, gcq_ref, gckv_ref, wuq_ref,
                       cosa_ref, sina_ref, cosq_ref, sinq_ref, cosr_ref, sinr_ref,
                       qa_ref, ka_ref, va_ref, qb_ref, kb_ref, vb_ref, qc_ref, ckv_ref, kr_ref):
    h = _rms(x_ref[...], g_ref[...]) * (1.0 + sc_ref[0]) + sh_ref[0]
    p = _dot(h.astype(bf16), w_ref[...])
    cosa = cosa_ref[...]
    sina = sina_ref[...]
    qa = p[:, OFF_QA:OFF_QA + W_QA] * cosa + p[:, OFF_QA_P:OFF_QA_P + W_QA] * sina
    ka = p[:, OFF_KA:OFF_KA + W_KA] * cosa[:, :W_KA] + p[:, OFF_KA_P:OFF_KA_P + W_KA] * sina[:, :W_KA]
    kr = p[:, OFF_KR:OFF_KR + QK_ROPE] * cosr_ref[...] + p[:, OFF_KR_P:OFF_KR_P + QK_ROPE] * sinr_ref[...]
    qa_ref[...] = qa.astype(bf16)
    ka_ref[...] = ka.astype(bf16)
    va_ref[...] = p[:, OFF_VA:OFF_VA + W_VA].astype(bf16)
    qb_ref[...] = p[:, OFF_QB:OFF_QB + W_B].astype(bf16)
    kb_ref[...] = p[:, OFF_KB:OFF_KB + W_B].astype(bf16)
    vb_ref[...] = p[:, OFF_VB:OFF_VB + W_B].astype(bf16)
    cqn = _rms(p[:, OFF_CQ:OFF_CQ + Q_LORA], gcq_ref[...])
    q2 = _dot(cqn.astype(bf16), wuq_ref[...])
    nq = H_C * QC_PAD
    qc_ref[...] = (q2[:, :nq] * cosq_ref[...] + q2[:, nq:] * sinq_ref[...]).astype(bf16)
    ckv_ref[...] = _rms(p[:, OFF_CKV:OFF_CKV + KV_LORA], gckv_ref[...]).astype(bf16)
    kr_ref[...] = kr.astype(bf16)


def _inproj_lat(x, lat_row0, g, shift, scale, w, gcq, gckv, wuq2, tabs):
    tm = TM_LAT_IN
    per_b = DEC_SEQ // tm
    row0 = lat_row0 // tm
    xrow = lambda i: (row0 + i, 0)
    row = lambda i: (i, 0)
    const = lambda i: (0, 0)
    grp = lambda i: (1 + i // per_b, 0, 0)
    pos = lambda i: (i % per_b, 0)
    cosa, sina, cosq, sinq, cosr, sinr = tabs
    widths = (W_QA, W_KA, W_VA, W_B, W_B, W_B, H_C * QC_PAD, KV_LORA, QK_ROPE)
    return pl.pallas_call(
        _inproj_lat_kernel,
        grid=(T_LAT // tm,),
        in_specs=[pl.BlockSpec((tm, D_MODEL), xrow),
                  pl.BlockSpec((1, D_MODEL), const),
                  pl.BlockSpec((1, 1, D_MODEL), grp),
                  pl.BlockSpec((1, 1, D_MODEL), grp),
                  pl.BlockSpec((D_MODEL, NW_LAT), const),
                  pl.BlockSpec((1, Q_LORA), const),
                  pl.BlockSpec((1, KV_LORA), const),
                  pl.BlockSpec((Q_LORA, 2 * H_C * QC_PAD), const),
                  pl.BlockSpec((tm, W_QA), pos), pl.BlockSpec((tm, W_QA), pos),
                  pl.BlockSpec((tm, H_C * QC_PAD), pos), pl.BlockSpec((tm, H_C * QC_PAD), pos),
                  pl.BlockSpec((tm, QK_ROPE), pos), pl.BlockSpec((tm, QK_ROPE), pos)],
        out_specs=[pl.BlockSpec((tm, wd), row) for wd in widths],
        out_shape=[jax.ShapeDtypeStruct((T_LAT, wd), bf16) for wd in widths],
        compiler_params=_cparams("arbitrary"),
        name="inproj_lat",
    )(x, g, shift, scale, w, gcq, gckv, wuq2, cosa, sina, cosq, sinq, cosr, sinr)


CTX_BATCHES = 2


def _ctx_attn_kernel(sink_ref, qa_ref, ka_ref, va_ref, qb_ref, kb_ref, vb_ref, qc_ref, ckv_ref, kr_ref,
                     wukv_ref, wout_ref, x_ref, gate_ref, o_ref, o_scr, s_scr, p_scr):
    n = SEQ
    scale = HEAD_DIM ** -0.5
    scale_c = (QK_NOPE + QK_ROPE) ** -0.5

    def one_batch(sb, carry):
        rows = pl.ds(pl.multiple_of(sb * n, n), n)
        ka = ka_ref[sb, 0].astype(bf16)
        va = va_ref[sb, 0].astype(bf16)
        kb = kb_ref[sb, 0].astype(bf16)
        vb = vb_ref[sb, 0].astype(bf16)
        kv = _dot(ckv_ref[sb, 0].astype(bf16), wukv_ref[...]).astype(bf16)
        kr = kr_ref[sb, 0].astype(bf16)
        for h in range(H_A):
            g = h // G_A
            q = qa_ref[rows, h * HEAD_DIM:(h + 1) * HEAD_DIM]
            s_scr[h * n:(h + 1) * n, :] = _dot_nt(q, ka[:, g * HEAD_DIM:(g + 1) * HEAD_DIM]) * scale
        for h in range(H_B):
            sl = slice(h * HEAD_DIM, (h + 1) * HEAD_DIM)
            s_scr[(H_A + h) * n:(H_A + h + 1) * n, :] = _dot_nt(qb_ref[rows, sl], kb[:, sl]) * scale
        for h in range(H_C):
            qn = qc_ref[rows, h * QC_PAD:h * QC_PAD + QK_NOPE]
            qr = qc_ref[rows, h * QC_PAD + QK_NOPE:h * QC_PAD + QK_NOPE + QK_ROPE]
            c0 = h * (QK_NOPE + V_C)
            r0 = (H_A + H_B + h) * n
            s_scr[r0:r0 + n, :] = (_dot_nt(qn, kv[:, c0:c0 + QK_NOPE]) + _dot_nt(qr, kr)) * scale_c
        for pair in range((H_A + H_B + H_C) // 2):
            h0 = 2 * pair
            sinks = ((sink_ref[h0], n), (sink_ref[h0 + 1], n)) if h0 < H_A else None
            _softmax_rows(s_scr, p_scr, slice(h0 * n, (h0 + 2) * n), sinks)
        for h in range(H_A):
            g = h // G_A
            o_scr[:, h * HEAD_DIM:(h + 1) * HEAD_DIM] = _dot(p_scr[h * n:(h + 1) * n, :],
                                                             va[:, g * HEAD_DIM:(g + 1) * HEAD_DIM])
        for h in range(H_B):
            sl = slice(h * HEAD_DIM, (h + 1) * HEAD_DIM)
            o_scr[:, W_QA + h * HEAD_DIM:W_QA + (h + 1) * HEAD_DIM] = _dot(
                p_scr[(H_A + h) * n:(H_A + h + 1) * n, :], vb[:, sl])
        for h in range(H_C):
            c0 = h * (QK_NOPE + V_C)
            r0 = (H_A + H_B + h) * n
            off = W_QA + W_B + h * V_C
            o_scr[:, off:off + V_C] = _dot(p_scr[r0:r0 + n, :], kv[:, c0 + QK_NOPE:c0 + QK_NOPE + V_C])
        y = _dot(o_scr[...].astype(bf16), wout_ref[...])
        o_ref[rows, :] = x_ref[rows, :] + gate_ref[0] * y
        return carry

    lax.fori_loop(0, CTX_BATCHES, one_batch, 0)


def _ctx_attn(layer, sink, qs, caches, wukv, wout, x, gate):
    qa, qb, qc = qs
    ka, va, kb, vb, ckv, kr = caches
    row = lambda b: (b, 0)
    const = lambda b: (0, 0)
    slot = lambda b: (b, layer, 0, 0)
    nrow = CTX_BATCHES * SEQ
    qspec = lambda a: pl.BlockSpec((nrow, a.shape[1]), row)
    cspec = lambda a: pl.BlockSpec((CTX_BATCHES, 1, SEQ, a.shape[3]), slot)
    in_specs = [pl.BlockSpec(memory_space=pltpu.SMEM),
                qspec(qa), cspec(ka), cspec(va), qspec(qb), cspec(kb), cspec(vb), qspec(qc), cspec(ckv), cspec(kr)]
    in_specs += [pl.BlockSpec((KV_LORA, H_C * (QK_NOPE + V_C)), const),
                 pl.BlockSpec((D_MODEL, D_MODEL), const),
                 pl.BlockSpec((nrow, D_MODEL), row),
                 pl.BlockSpec((1, 1, D_MODEL), lambda b: (0, 0, 0))]
    return pl.pallas_call(
        _ctx_attn_kernel,
        grid=(BATCH // CTX_BATCHES,),
        in_specs=in_specs,
        out_specs=pl.BlockSpec((nrow, D_MODEL), row),
        out_shape=jax.ShapeDtypeStruct((T_CTX, D_MODEL), f32),
        scratch_shapes=[pltpu.VMEM((SEQ, D_MODEL), f32),
                        pltpu.VMEM(((H_A + H_B + H_C) * SEQ, SEQ), f32),
                        pltpu.VMEM(((H_A + H_B + H_C) * SEQ, SEQ), bf16)],
        compiler_params=_cparams("arbitrary"),
        name="ctx_attn",
    )(sink, qa, ka, va, qb, kb, vb, qc, ckv, kr, wukv, wout, x, gate)


def _lat_attn_kernel(sink_ref, qa_ref, qb_ref, qc_ref, ka_ref, va_ref, kb_ref, vb_ref, ckv_ref, kr_ref,
                     cak_ref, cav_ref, cbk_ref, cbv_ref, cckv_ref, ckr_ref, bias_ref,
                     wukv_ref, wout_ref, x_ref, gate_ref, o_ref, o_scr, kv_scr, sa, pa, sb, pb, sc, pc):
    qi = pl.program_id(1)
    nb = DEC_SEQ // BLOCK
    scale = HEAD_DIM ** -0.5

    @pl.when(qi == 0)
    def _():
        kv_scr[0:DEC_SEQ, :] = _dot(ckv_ref[...], wukv_ref[...]).astype(bf16)
        kv_scr[DEC_SEQ:DEC_SEQ + PAST_LEN, :] = _dot(cckv_ref[0, 0].astype(bf16), wukv_ref[...]).astype(bf16)

    def blk(ref, j):
        idx = jnp.clip(qi + j, 0, nb - 1)
        return ref[pl.ds(pl.multiple_of(idx * BLOCK, BLOCK), BLOCK), :]

    ka = jnp.concatenate([blk(ka_ref, -1), blk(ka_ref, 0), blk(ka_ref, 1), cak_ref[0, 0].astype(bf16)], axis=0)
    va = jnp.concatenate([blk(va_ref, -1), blk(va_ref, 0), blk(va_ref, 1), cav_ref[0, 0].astype(bf16)], axis=0)
    nk_a = 3 * BLOCK + PAST_LEN
    r = lax.broadcasted_iota(jnp.int32, (BLOCK, nk_a), 0)
    c = lax.broadcasted_iota(jnp.int32, (BLOCK, nk_a), 1)
    valid = (((c < BLOCK) & (c >= r) & (qi > 0))
             | ((c >= BLOCK) & (c < 2 * BLOCK))
             | ((c >= 2 * BLOCK) & (c < 3 * BLOCK) & (c - 2 * BLOCK <= r) & (qi < nb - 1))
             | (c >= 3 * BLOCK))
    for h in range(H_A):
        g = h // G_A
        q = qa_ref[:, h * HEAD_DIM:(h + 1) * HEAD_DIM]
        s = _dot_nt(q, ka[:, g * HEAD_DIM:(g + 1) * HEAD_DIM]) * scale
        sa[h * BLOCK:(h + 1) * BLOCK, :] = jnp.where(valid, s, NEG)

    cbk = cbk_ref[0, 0].astype(bf16)
    cbv = cbv_ref[0, 0].astype(bf16)
    rows_per_blk = BLOCK // GRID_W
    nloc = NA_ROWS * GRID_W
    vcats = []
    for half in range(rows_per_blk):
        grow = qi * rows_per_blk + half
        start = jnp.clip(grow - NA_ROWS // 2, 0, ROWS - NA_ROWS)
        kloc = kb_ref[pl.ds(pl.multiple_of(start * GRID_W, GRID_W), nloc), :]
        vloc = vb_ref[pl.ds(pl.multiple_of(start * GRID_W, GRID_W), nloc), :]
        vcats.append(jnp.concatenate([vloc, cbv], axis=0))
        qrows = slice(half * GRID_W, (half + 1) * GRID_W)
        dr0 = start - grow + (NA_ROWS - 1)
        for h in range(H_B):
            sl = slice(h * HEAD_DIM, (h + 1) * HEAD_DIM)
            q = qb_ref[qrows, sl]
            bias = jnp.concatenate([bias_ref[h, dr0 + 2 * j] for j in range(NA_ROWS // 2)], axis=1)
            s_loc = _dot_nt(q, kloc[:, sl]) * scale + bias
            s_ctx = _dot_nt(q, cbk[:, sl]) * scale
            r0 = (half * H_B + h) * GRID_W
            sb[r0:r0 + GRID_W, :] = jnp.concatenate([s_loc, s_ctx], axis=1)

    kr = jnp.concatenate([kr_ref[...], ckr_ref[0, 0].astype(bf16)], axis=0)
    scale_c = (QK_NOPE + QK_ROPE) ** -0.5
    for h in range(H_C):
        qn = qc_ref[:, h * QC_PAD:h * QC_PAD + QK_NOPE]
        qr = qc_ref[:, h * QC_PAD + QK_NOPE:h * QC_PAD + QK_NOPE + QK_ROPE]
        c0 = h * (QK_NOPE + V_C)
        sc[h * BLOCK:(h + 1) * BLOCK, :] = (_dot_nt(qn, kv_scr[:, c0:c0 + QK_NOPE]) + _dot_nt(qr, kr)) * scale_c

    for pair in range(H_A // 2):
        h0 = 2 * pair
        _softmax_rows(sa, pa, slice(h0 * BLOCK, (h0 + 2) * BLOCK), ((sink_ref[h0], BLOCK), (sink_ref[h0 + 1], BLOCK)))
    for blk2 in range(rows_per_blk * H_B // 2):
        _softmax_rows(sb, pb, slice(blk2 * 2 * GRID_W, (blk2 + 1) * 2 * GRID_W))
    for h in range(H_C):
        _softmax_rows(sc, pc, slice(h * BLOCK, (h + 1) * BLOCK))

    for h in range(H_A):
        g = h // G_A
        o_scr[:, h * HEAD_DIM:(h + 1) * HEAD_DIM] = _dot(pa[h * BLOCK:(h + 1) * BLOCK, :],
                                                         va[:, g * HEAD_DIM:(g + 1) * HEAD_DIM])
    for half in range(rows_per_blk):
        qrows = slice(half * GRID_W, (half + 1) * GRID_W)
        for h in range(H_B):
            sl = slice(h * HEAD_DIM, (h + 1) * HEAD_DIM)
            r0 = (half * H_B + h) * GRID_W
            o_scr[qrows, W_QA + h * HEAD_DIM:W_QA + (h + 1) * HEAD_DIM] = _dot(pb[r0:r0 + GRID_W, :],
                                                                             vcats[half][:, sl])
    for h in range(H_C):
        c0 = h * (QK_NOPE + V_C)
        off = W_QA + W_B + h * V_C
        o_scr[:, off:off + V_C] = _dot(pc[h * BLOCK:(h + 1) * BLOCK, :], kv_scr[:, c0 + QK_NOPE:c0 + QK_NOPE + V_C])

    y = _dot(o_scr[...].astype(bf16), wout_ref[...])
    o_ref[...] = x_ref[...] + gate_ref[0] * y


def _lat_attn(layer, sink, proj, caches, bias_tab, wukv, wout, x, lat_row0, gate):
    qa, ka, va, qb, kb, vb, qc, ckv, kr = proj
    nb = DEC_SEQ // BLOCK
    qrow = lambda b, q: (b * nb + q, 0)
    xrow = lambda b, q: (lat_row0 // BLOCK + b * nb + q, 0)
    brow = lambda b, q: (b, 0)
    const = lambda b, q: (0, 0)
    cidx = lambda b, q: (b, layer, 0, 0)
    in_specs = [pl.BlockSpec(memory_space=pltpu.SMEM)]
    in_specs += [pl.BlockSpec((BLOCK, a.shape[1]), qrow) for a in (qa, qb, qc)]
    in_specs += [pl.BlockSpec((DEC_SEQ, a.shape[1]), brow) for a in (ka, va, kb, vb, ckv, kr)]
    in_specs += [pl.BlockSpec((1, 1, PAST_LEN, a.shape[3]), cidx) for a in caches]
    in_specs += [pl.BlockSpec(bias_tab.shape, lambda b, q: (0, 0, 0, 0)),
                 pl.BlockSpec((KV_LORA, H_C * (QK_NOPE + V_C)), const),
                 pl.BlockSpec((D_MODEL, D_MODEL), const),
                 pl.BlockSpec((BLOCK, D_MODEL), xrow),
                 pl.BlockSpec((1, 1, D_MODEL), lambda b, q: (1 + b, 0, 0))]
    return pl.pallas_call(
        _lat_attn_kernel,
        grid=(DEC_BATCH, nb),
        in_specs=in_specs,
        out_specs=pl.BlockSpec((BLOCK, D_MODEL), qrow),
        out_shape=jax.ShapeDtypeStruct((T_LAT, D_MODEL), f32),
        scratch_shapes=[pltpu.VMEM((BLOCK, D_MODEL), f32),
                        pltpu.VMEM((DEC_SEQ + PAST_LEN, H_C * (QK_NOPE + V_C)), bf16)]
        + [pltpu.VMEM(shape, dt) for shape in ((H_A * BLOCK, 3 * BLOCK + PAST_LEN),
                                               (H_B * BLOCK, NA_ROWS * GRID_W + PAST_LEN),
                                               (H_C * BLOCK, DEC_SEQ + PAST_LEN)) for dt in (f32, bf16)],
        compiler_params=_cparams("arbitrary", "arbitrary"),
        name="lat_attn",
    )(sink, qa, qb, qc, ka, va, kb, vb, ckv, kr, *caches, bias_tab, wukv, wout, x, gate)


def _pick_stream(xc_ref, xl_ref, x_scr):
    i = pl.program_id(0)

    @pl.when(i < T_CTX // TM_TOK)
    def _():
        x_scr[...] = xc_ref[...]

    @pl.when(i >= T_CTX // TM_TOK)
    def _():
        x_scr[...] = xl_ref[...]

    return x_scr[...]


def _stream_specs(lat_row0):
    n_ctx = T_CTX // TM_TOK
    return [pl.BlockSpec((TM_TOK, D_MODEL), lambda i: (jnp.minimum(i, n_ctx - 1), 0)),
            pl.BlockSpec((TM_TOK, D_MODEL), lambda i: (lat_row0 // TM_TOK + jnp.maximum(i - n_ctx, 0), 0))]


def _router_kernel(xc_ref, xl_ref, g_ref, sh_ref, sc_ref, wr_ref, br_ref, h_ref, e_ref, gt_ref, x_scr):
    h = _rms(_pick_stream(xc_ref, xl_ref, x_scr), g_ref[...]) * (1.0 + sc_ref[0]) + sh_ref[0]
    _store_row_tiles(h_ref, h)
    h_hi = h.astype(bf16)
    h_lo = (h - h_hi.astype(f32)).astype(bf16)
    w = wr_ref[...]
    w_hi = w.astype(bf16)
    w_lo = (w - w_hi.astype(f32)).astype(bf16)
    logits = _dot(h_hi, w_hi) + _dot(h_hi, w_lo) + _dot(h_lo, w_hi) + br_ref[...]
    lane = lax.broadcasted_iota(jnp.int32, logits.shape, 1).astype(f32)
    l = jnp.where(lane < N_EXPERTS, logits, -jnp.inf)
    tops, idxs = [], []
    for _ in range(TOP_K):
        m = jnp.max(l, axis=-1, keepdims=True)
        idx = jnp.min(jnp.where(l == m, lane, float(LANE)), axis=-1, keepdims=True)
        tops.append(m)
        idxs.append(idx)
        l = jnp.where(lane == idx, -jnp.inf, l)
    ex = [jnp.exp(t - tops[0]) for t in tops]
    den = ex[0] + ex[1] + ex[2] + ex[3]
    e_out = jnp.zeros(logits.shape, f32)
    g_out = jnp.zeros(logits.shape, f32)
    for k in range(TOP_K):
        e_out = jnp.where(lane == k, idxs[k], e_out)
        g_out = jnp.where(lane == k, ex[k] / den, g_out)
    e_ref[...] = e_out.astype(jnp.int32)
    gt_ref[...] = g_out


def _group_of_tile(i):
    per_b = DEC_SEQ // TM_TOK
    n_ctx = T_CTX // TM_TOK
    return jnp.where(i < n_ctx, 0, 1 + (i - n_ctx) // per_b)


def _router(xc, xl, lat_row0, g, shift, scale, wr, br):
    tm = TM_TOK
    row = lambda i: (i, 0)
    const = lambda i: (0, 0)
    grp = lambda i: (_group_of_tile(i), 0, 0)
    return pl.pallas_call(
        _router_kernel,
        grid=(T_ALL // tm,),
        in_specs=_stream_specs(lat_row0) +
                 [pl.BlockSpec((1, D_MODEL), const),
                  pl.BlockSpec((1, 1, D_MODEL), grp),
                  pl.BlockSpec((1, 1, D_MODEL), grp),
                  pl.BlockSpec((D_MODEL, LANE), const),
                  pl.BlockSpec((1, LANE), const)],
        out_specs=[pl.BlockSpec((tm * ROW_TILE, LANE), row), pl.BlockSpec((tm, LANE), row),
                   pl.BlockSpec((tm, LANE), row)],
        out_shape=[jax.ShapeDtypeStruct((T_ALL * ROW_TILE, LANE), f32),
                   jax.ShapeDtypeStruct((T_ALL, LANE), jnp.int32),
                   jax.ShapeDtypeStruct((T_ALL, LANE), f32)],
        scratch_shapes=[pltpu.VMEM((tm, D_MODEL), f32)],
        compiler_params=_cparams("arbitrary"),
        name="router",
    )(xc, xl, g, shift, scale, wr, br)


def _dispatch_kernel(tok_ref, nu_ref, h_hbm, o_ref, hv, xg, hsem):
    tm = TM_MOE
    i = pl.program_id(0)

    @pl.when(i == 0)
    def _():
        resident = pltpu.make_async_copy(h_hbm, hv, hsem.at[0])
        resident.start()
        resident.wait()

    def one_block(sub, carry):
        blk = i * DISPATCH_BLOCKS + sub
        rows = pl.ds(pl.multiple_of(sub * tm, tm), tm)

        @pl.when(blk < nu_ref[0])
        def _():
            for r in range(tm):
                t = tok_ref[blk * tm + r]
                xg[pl.ds(r, ROW_TILE, stride=tm + 1), :] = hv[pl.ds(pl.multiple_of(t * ROW_TILE, ROW_TILE),
                                                                 ROW_TILE), :]
            o_ref[rows, :] = jnp.concatenate([xg[pl.ds(c * (tm + 1), tm), :] for c in range(ROW_TILE)],
                                             axis=1).astype(bf16)

        @pl.when(blk >= nu_ref[0])
        def _():
            o_ref[rows, :] = jnp.zeros((tm, D_MODEL), bf16)

        return carry

    lax.fori_loop(0, DISPATCH_BLOCKS, one_block, 0)


def _dispatch(row_tok, n_used, h):
    tm = TM_MOE
    return pl.pallas_call(
        _dispatch_kernel,
        grid_spec=pltpu.PrefetchScalarGridSpec(
            num_scalar_prefetch=2,
            grid=(N_MOE_BLOCKS // DISPATCH_BLOCKS,),
            in_specs=[pl.BlockSpec(memory_space=pl.ANY)],
            out_specs=pl.BlockSpec((DISPATCH_BLOCKS * tm, D_MODEL), lambda i, tok, nu: (i, 0)),
            scratch_shapes=[pltpu.VMEM((T_ALL * ROW_TILE, LANE), f32), pltpu.VMEM(((tm + 1) * ROW_TILE, LANE), f32),
                            pltpu.SemaphoreType.DMA((1,))]),
        out_shape=jax.ShapeDtypeStruct((N_MOE_BLOCKS * tm, D_MODEL), bf16),
        compiler_params=_cparams("arbitrary"),
        name="dispatch",
    )(row_tok, n_used, h)


def _moe_kernel(layer, be_ref, nu_ref, nxt_ref, dst_ref, x_ref, wgu_hbm, bgu_ref, wd_hbm, bd_ref, y_hbm,
                y0, y1, wgu_st, wd_st, wgu_bf, wd_bf, wsem, ssem):
    tm = TM_MOE
    i = pl.program_id(0)
    nb = pl.num_programs(0)
    used = i < nu_ref[0]
    yb = (y0, y1)

    def out_tile(row):
        return pl.ds(pl.multiple_of(row * ROW_TILE, ROW_TILE), ROW_TILE)

    def scatter_desc(buf, r, dst_row, s):
        return pltpu.make_async_copy(buf.at[out_tile(r)], y_hbm.at[out_tile(dst_row)], ssem.at[s])

    def scatter_wait(s):
        pltpu.make_async_copy(yb[s], y_hbm.at[pl.ds(0, tm * ROW_TILE)], ssem.at[s]).wait()

    def scatter_start(blk, s, unrolled):
        if unrolled:
            for r in range(tm):
                scatter_desc(yb[s], r, dst_ref[(blk + 1) * tm + r], s).start()
        else:
            def body(r, carry):
                scatter_desc(yb[s], r, dst_ref[(blk + 1) * tm + r], s).start()
                return carry
            lax.fori_loop(0, tm, body, 0, unroll=8)

    def weight_copies(e):
        return (pltpu.make_async_copy(wgu_hbm.at[layer, e], wgu_st, wsem.at[0]),
                pltpu.make_async_copy(wd_hbm.at[layer, e], wd_st, wsem.at[1]))

    @pl.when(i == 0)
    def _():
        for s in range(2):
            yb[s][...] = jnp.zeros_like(yb[s])
            dummy = pltpu.make_async_copy(yb[s], y_hbm.at[pl.ds((N_ASSIGN + s * tm) * ROW_TILE, tm * ROW_TILE)],
                                          ssem.at[s])
            dummy.start()
            dummy.wait()
        for cp in weight_copies(be_ref[0]):
            cp.start(priority=1)

    first = jnp.logical_and(used, jnp.logical_or(i == 0, be_ref[i] != be_ref[jnp.maximum(i - 1, 0)]))

    @pl.when(first)
    def _():
        for cp in weight_copies(0):
            cp.wait()
        wgu_bf[...] = wgu_st[...].astype(bf16)
        wd_bf[...] = wd_st[...].astype(bf16)

        @pl.when(nxt_ref[i] >= 0)
        def _():
            for cp in weight_copies(nxt_ref[i]):
                cp.start(priority=1)

    def step(par):
        cur, oth = par, 1 - par

        @pl.when(jnp.logical_and(i >= 1, i - 2 < nu_ref[0]))
        def _():
            scatter_wait(cur)

        @pl.when(used)
        def _():
            scatter_start(i - 1, oth, unrolled=True)
            gu = _dot(x_ref[...], wgu_bf[...]) + bgu_ref[0, 0]
            x_glu = jnp.minimum(gu[:, :D_FF], SWIGLU_LIMIT)
            x_lin = jnp.clip(gu[:, D_FF:], -SWIGLU_LIMIT, SWIGLU_LIMIT)
            act = x_glu * jax.nn.sigmoid(SWIGLU_ALPHA * x_glu) * (x_lin + 1.0)
            _store_row_tiles(yb[cur], _dot(act.astype(bf16), wd_bf[...]) + bd_ref[0, 0])

        flush = jnp.logical_and(jnp.logical_not(used), i - 1 < nu_ref[0])

        @pl.when(flush)
        def _():
            scatter_start(i - 1, oth, unrolled=False)

        @pl.when(jnp.logical_and(flush, i == nb - 1))
        def _():
            scatter_wait(oth)

    @pl.when(i % 2 == 0)
    def _():
        step(0)

    @pl.when(i % 2 == 1)
    def _():
        step(1)


def _moe(layer, routing, h, w_gu, b_gu, w_down, b_down):
    tm = TM_MOE
    block_e, n_used, nxt_e, row_tok, row_dst = routing
    xs = _dispatch(row_tok, n_used, h)
    ex4 = lambda i, be, nu, nxt, dst: (layer, be[i], 0, 0)
    return pl.pallas_call(
        functools.partial(_moe_kernel, layer),
        grid_spec=pltpu.PrefetchScalarGridSpec(
            num_scalar_prefetch=4,
            grid=(N_MOE_BLOCKS,),
            in_specs=[pl.BlockSpec((tm, D_MODEL), lambda i, be, nu, nxt, dst: (i, 0)),
                      pl.BlockSpec(memory_space=pl.ANY),
                      pl.BlockSpec((1, 1, 1, 2 * D_FF), ex4),
                      pl.BlockSpec(memory_space=pl.ANY),
                      pl.BlockSpec((1, 1, 1, D_MODEL), ex4)],
            out_specs=pl.BlockSpec(memory_space=pl.ANY),
            scratch_shapes=[pltpu.VMEM((tm * ROW_TILE, LANE), f32), pltpu.VMEM((tm * ROW_TILE, LANE), f32),
                            pltpu.VMEM((D_MODEL, 2 * D_FF), f32), pltpu.VMEM((D_FF, D_MODEL), f32),
                            pltpu.VMEM((D_MODEL, 2 * D_FF), bf16), pltpu.VMEM((D_FF, D_MODEL), bf16),
                            pltpu.SemaphoreType.DMA((2,)), pltpu.SemaphoreType.DMA((2,))]),
        out_shape=jax.ShapeDtypeStruct(((N_ASSIGN + 2 * tm) * ROW_TILE, LANE), f32),
        compiler_params=_cparams("arbitrary"),
        name="moe",
    )(block_e, n_used, nxt_e, row_dst, xs, w_gu, b_gu.reshape(DEPTH, N_EXPERTS, 1, 2 * D_FF),
      w_down, b_down.reshape(DEPTH, N_EXPERTS, 1, D_MODEL))


def _combine_kernel(final, xc_ref, xl_ref, y0_ref, y1_ref, y2_ref, y3_ref, gt_ref, gate_ref, gf_ref, *rest):
    x_scr = rest[-1]
    gt = gt_ref[...]
    f = gt[:, 0:1] * _load_row_tiles(y0_ref)
    for k, y_ref in ((1, y1_ref), (2, y2_ref), (3, y3_ref)):
        f = f + gt[:, k:k + 1] * _load_row_tiles(y_ref)
    out = _pick_stream(xc_ref, xl_ref, x_scr) + gate_ref[0] * f
    if not final:
        rest[0][...] = out
        return
    out = _rms(out, gf_ref[...])
    oc_ref, ol_ref = rest[0], rest[1]
    i = pl.program_id(0)

    @pl.when(i < T_CTX // TM_TOK)
    def _():
        oc_ref[...] = out

    @pl.when(i >= T_CTX // TM_TOK)
    def _():
        ol_ref[...] = out


def _combine(final, xc, xl, lat_row0, y, gates, gate, g_final):
    tm = TM_TOK
    nt = T_ALL // tm
    n_ctx = T_CTX // tm
    row = lambda i: (i, 0)
    const = lambda i: (0, 0)
    grp = lambda i: (_group_of_tile(i), 0, 0)
    ysel = [pl.BlockSpec((tm * ROW_TILE, LANE), functools.partial(lambda k, i: (k * nt + i, 0), k))
            for k in range(TOP_K)]
    if final:
        out_specs = [pl.BlockSpec((tm, D_MODEL), lambda i: (jnp.minimum(i, n_ctx - 1), 0)),
                     pl.BlockSpec((tm, D_MODEL), lambda i: (jnp.maximum(i - n_ctx, 0), 0))]
        out_shape = [jax.ShapeDtypeStruct((T_CTX, D_MODEL), f32), jax.ShapeDtypeStruct((T_LAT, D_MODEL), f32)]
    else:
        out_specs = pl.BlockSpec((tm, D_MODEL), row)
        out_shape = jax.ShapeDtypeStruct((T_ALL, D_MODEL), f32)
    return pl.pallas_call(
        functools.partial(_combine_kernel, final),
        grid=(nt,),
        in_specs=_stream_specs(lat_row0) + ysel +
                 [pl.BlockSpec((tm, LANE), row),
                  pl.BlockSpec((1, 1, D_MODEL), grp),
                  pl.BlockSpec((1, D_MODEL), const)],
        out_specs=out_specs,
        out_shape=out_shape,
        scratch_shapes=[pltpu.VMEM((tm, D_MODEL), f32)],
        compiler_params=_cparams("arbitrary"),
        name="combine",
    )(xc, xl, y, y, y, y, gates, gate, g_final)


def _rope_head_tables(d):
    nf = d // 4
    half = d // 2
    t = np.arange(DEC_SEQ)
    inv = ROPE_BASE ** (-np.arange(nf, dtype=np.float32) / nf)
    i = np.arange(d)
    pos = np.where(i[None, :] < half, (t // GRID_W)[:, None], (t % GRID_W)[:, None]).astype(np.float32)
    ang = pos * inv[i % nf][None, :].astype(np.float32)
    first = (i % half) < nf
    cos = np.cos(ang)
    sin = np.where(first[None, :], -np.sin(ang), np.sin(ang))
    partner = np.where(first, i + nf, i - nf)
    return cos.astype(np.float32), sin.astype(np.float32), partner


def _rope_tables():
    cos64, sin64, _ = _rope_head_tables(HEAD_DIM)
    cos32, sin32, _ = _rope_head_tables(QK_ROPE)
    cosa = np.tile(cos64, (1, H_A))
    sina = np.tile(sin64, (1, H_A))
    cosq1 = np.concatenate([np.ones((DEC_SEQ, QK_NOPE), np.float32), cos32,
                            np.ones((DEC_SEQ, QC_PAD - QK_NOPE - QK_ROPE), np.float32)], axis=1)
    sinq1 = np.concatenate([np.zeros((DEC_SEQ, QK_NOPE), np.float32), sin32,
                            np.zeros((DEC_SEQ, QC_PAD - QK_NOPE - QK_ROPE), np.float32)], axis=1)
    cosq = np.tile(cosq1, (1, H_C))
    sinq = np.tile(sinq1, (1, H_C))
    return tuple(jnp.asarray(a) for a in (cosa, sina, cosq, sinq, cos32, sin32))


def _pad_cols(w, n):
    return jnp.pad(w, ((0, 0), (0, n - w.shape[1])))


def _layer_weights(w_in, w_uq):
    cuts = np.cumsum((W_QA, W_KA, W_VA, W_B, W_B, W_B, Q_LORA, KV_LORA, QK_ROPE))[:-1]
    qa, ka, va, qb, kb, vb, cq, ckv, kr = jnp.split(w_in, [int(c) for c in cuts], axis=1)
    _, _, p64 = _rope_head_tables(HEAD_DIM)
    _, _, p32 = _rope_head_tables(QK_ROPE)
    pa = np.concatenate([h * HEAD_DIM + p64 for h in range(H_A)])
    base = jnp.concatenate([qa, ka, va, _pad_cols(qb, 384), _pad_cols(kb, 384), _pad_cols(vb, 384), cq, ckv,
                            _pad_cols(kr, 128)], axis=1)
    w_ctx = base.astype(bf16)
    w_lat = jnp.concatenate([base, qa[:, pa], ka[:, pa[:W_KA]], _pad_cols(kr[:, p32], 128)], axis=1).astype(bf16)
    hq = QK_NOPE + QK_ROPE
    heads = [_pad_cols(w_uq[:, h * hq:(h + 1) * hq], QC_PAD) for h in range(H_C)]
    pq = np.concatenate([np.arange(QK_NOPE), QK_NOPE + p32])
    heads_p = [_pad_cols(w_uq[:, h * hq:(h + 1) * hq][:, pq], QC_PAD) for h in range(H_C)]
    wuq = jnp.concatenate(heads, axis=1).astype(bf16)
    wuq2 = jnp.concatenate(heads + heads_p, axis=1).astype(bf16)
    return w_ctx, w_lat, wuq, wuq2


def _bias_table(rpb):
    col = np.arange(GRID_W)
    col_start = np.clip(col - NA_COLS // 2, 0, GRID_W - NA_COLS)
    col_ok = (col[None, :] >= col_start[:, None]) & (col[None, :] < col_start[:, None] + NA_COLS)
    dc = np.clip(col[None, :] - col[:, None] + (NA_COLS - 1), 0, 2 * NA_COLS - 2)
    onehot = (dc[None] == np.arange(2 * NA_COLS - 1)[:, None, None]).astype(np.float32)
    expanded = jnp.einsum('hrd,dqk->hrqk', rpb.astype(f32), jnp.asarray(onehot), precision=lax.Precision.HIGHEST)
    blocks = jnp.where(col_ok[None, None], expanded, NEG)
    return jnp.concatenate([blocks[:, :-1], blocks[:, 1:]], axis=-1)


def _routing(top_e):
    tm = TM_MOE
    key_bits = 16
    pad_mark = (1 << key_bits) - 1
    flat_e = top_e.T.reshape(N_ASSIGN)
    experts = jnp.arange(N_EXPERTS, dtype=jnp.int32)
    counts = jnp.sum((flat_e[:, None] == experts[None, :]).astype(jnp.int32), axis=0)
    nblk = (counts + tm - 1) // tm
    blk_end = jnp.cumsum(nblk)
    pad_end = jnp.cumsum(nblk * tm - counts)
    slots = jnp.arange(N_MOE_BLOCKS * tm - N_ASSIGN, dtype=jnp.int32)
    pad_e = jnp.sum((pad_end[None, :] <= slots[:, None]).astype(jnp.int32), axis=1)
    keys = jnp.concatenate([(flat_e << key_bits) + jnp.arange(N_ASSIGN, dtype=jnp.int32),
                            (pad_e << key_bits) + pad_mark])
    asg = (jnp.sort(keys, stable=False) & pad_mark).reshape(N_MOE_BLOCKS, tm)
    valid = asg != pad_mark
    blocks = jnp.arange(N_MOE_BLOCKS, dtype=jnp.int32)
    r = jnp.arange(tm, dtype=jnp.int32)[None, :]
    tok = jnp.where(valid, asg % T_ALL, 0)
    row_dst = jnp.where(valid, asg, N_ASSIGN + (blocks[:, None] % 2) * tm + r)
    row_dst = jnp.concatenate([N_ASSIGN + tm + r, row_dst], axis=0).reshape(-1)
    block_e = jnp.minimum(jnp.sum((blk_end[None, :] <= blocks[:, None]).astype(jnp.int32), axis=1), N_EXPERTS - 1)
    n_used = blk_end[-1].astype(jnp.int32).reshape(1)
    has = jnp.where(counts > 0, experts, N_EXPERTS)
    later = experts[None, :] > experts[:, None]
    nxt = jnp.min(jnp.where(later, has[None, :], N_EXPERTS), axis=1)
    nxt = jnp.where(nxt >= N_EXPERTS, -1, nxt)
    sel = (block_e[:, None] == experts[None, :]).astype(jnp.int32)
    nxt_e = jnp.sum(sel * nxt[None, :], axis=1)
    i32 = lambda a: a.astype(jnp.int32)
    return i32(block_e), n_used, i32(nxt_e), i32(tok).reshape(-1), i32(row_dst)


def kernel(x_prompt, x_sample, cache_a_k, cache_a_v, cache_b_k, cache_b_v, cache_c_kv, cache_c_kr, c, c_ctx, w_ada, b_ada, g_attn, g_ffn, w_in, sink_a, rpb_b, g_cq, g_ckv, w_uq, w_ukv, w_out, w_router, b_router, w_gu, b_gu, w_down, b_down, g_final):
    xc, xl, lat_row0 = x_prompt.reshape(T_CTX, D_MODEL), x_sample.reshape(T_LAT, D_MODEL), 0
    cvec = jnp.concatenate([c_ctx[None, :], c, jnp.zeros((8 - N_GROUPS, D_MODEL), f32)], axis=0)
    mods = _ada(cvec, w_ada, b_ada)[:, :N_GROUPS].reshape(DEPTH, N_GROUPS, 6, 1, D_MODEL)
    tabs = _rope_tables()
    caches = (cache_a_k.reshape(DEC_BATCH, DEPTH, PAST_LEN, W_KA), cache_a_v.reshape(DEC_BATCH, DEPTH, PAST_LEN, W_VA),
              cache_b_k.reshape(DEC_BATCH, DEPTH, PAST_LEN, W_B), cache_b_v.reshape(DEC_BATCH, DEPTH, PAST_LEN, W_B),
              cache_c_kv, cache_c_kr)
    new = None
    for layer in range(DEPTH):
        m = [mods[layer, :, j] for j in range(6)]
        w_ctx, w_lat, wuq, wuq2 = _layer_weights(w_in[layer], w_uq[layer])
        wukv = w_ukv[layer].astype(bf16)
        wout = w_out[layer].astype(bf16)
        g1 = g_attn[layer][None, :]
        gcq = g_cq[layer][None, :]
        gckv = g_ckv[layer][None, :]
        sink = sink_a[layer]

        qs, new = _inproj_ctx(layer, new, xc, g1, m[0], m[1], w_ctx, gcq, gckv, wuq)
        x_ctx = _ctx_attn(layer, sink, qs, new, wukv, wout, xc, m[2])

        plat = _inproj_lat(xl, lat_row0, g1, m[0], m[1], w_lat, gcq, gckv, wuq2, tabs)
        x_lat = _lat_attn(layer, sink, plat, caches, _bias_table(rpb_b[layer]), wukv, wout, xl, lat_row0, m[2])

        wr = _pad_cols(w_router[layer], LANE)
        br = _pad_cols(b_router[layer][None, :], LANE)
        h2, top_e, gates = _router(x_ctx, x_lat, 0, g_ffn[layer][None, :], m[3], m[4], wr, br)
        y = _moe(layer, _routing(top_e[:, :TOP_K]), h2, w_gu, b_gu, w_down, b_down)
        x = _combine(layer == DEPTH - 1, x_ctx, x_lat, 0, y, gates, m[5], g_final[None, :])
        xc, xl, lat_row0 = x, x, T_CTX

    y_prompt = x[0].reshape(BATCH, SEQ, D_MODEL)
    y_sample = x[1].reshape(DEC_BATCH, DEC_SEQ, D_MODEL)
    shapes = ((KV_A, HEAD_DIM), (KV_A, HEAD_DIM), (H_B, HEAD_DIM), (H_B, HEAD_DIM), (KV_LORA,), (QK_ROPE,))
    outs = [a.reshape((BATCH, DEPTH, SEQ) + s) for a, s in zip(new, shapes)]
    return (y_prompt, y_sample, *outs)
```

```python
import functools

import numpy as np
import jax
import jax.numpy as jnp
from jax import lax
from jax.experimental import pallas as pl
from jax.experimental.pallas import tpu as pltpu

D_MODEL = 1024
BATCH = 32
SEQ = 256
DEPTH = 2
DEC_BATCH = 2
DEC_SEQ = 1024
PAST_LEN = 512
GRID_W = 64
HEAD_DIM = 64
H_A = 6
KV_A = 2
G_A = H_A // KV_A
WINDOW = 128
BLOCK = 128
H_B = 5
NA_ROWS = 8
NA_COLS = 16
H_C = 5
Q_LORA = 384
KV_LORA = 256
QK_NOPE = 64
QK_ROPE = 32
V_C = 64
N_EXPERTS = 32
TOP_K = 4
D_FF = 1024
SWIGLU_ALPHA = 1.702
SWIGLU_LIMIT = 7.0
ROPE_BASE = 10000.0
EPS = 1e-6
NEG = -1e30

T_CTX = BATCH * SEQ
T_LAT = DEC_BATCH * DEC_SEQ
T_ALL = T_CTX + T_LAT
N_GROUPS = 1 + DEC_BATCH
LANE = 128
QC_PAD = 128
ROWS = DEC_SEQ // GRID_W

W_QA, W_KA, W_VA = H_A * HEAD_DIM, KV_A * HEAD_DIM, KV_A * HEAD_DIM
W_B = H_B * HEAD_DIM
OFF_QA = 0
OFF_KA = 384
OFF_VA = 512
OFF_QB = 640
OFF_KB = 1024
OFF_VB = 1408
OFF_CQ = 1792
OFF_CKV = 2176
OFF_KR = 2432
NW_CTX = 2560
OFF_QA_P = 2560
OFF_KA_P = 2944
OFF_KR_P = 3072
NW_LAT = 3200

TM_TOK = 512
TM_LAT_IN = 512
TM_MOE = 256
N_ASSIGN = T_ALL * TOP_K
N_MOE_BLOCKS = N_ASSIGN // TM_MOE + N_EXPERTS
DISPATCH_BLOCKS = 8
VMEM_LIMIT = 56 * 1024 * 1024

f32 = jnp.float32
bf16 = jnp.bfloat16


def _cparams(*sem):
    return pltpu.CompilerParams(dimension_semantics=sem, vmem_limit_bytes=VMEM_LIMIT)


def _rms(xf, g):
    return xf * lax.rsqrt(jnp.mean(xf * xf, axis=-1, keepdims=True) + EPS) * g


def _dot(a, b):
    return jnp.dot(a, b, preferred_element_type=f32)


def _dot_nt(a, b):
    return lax.dot_general(a, b, (((1,), (1,)), ((), ())), preferred_element_type=f32)


ROW_TILE = D_MODEL // LANE


def _store_row_tiles(ref, val):
    n = val.shape[0]
    for c in range(ROW_TILE):
        ref[pl.ds(c, n, stride=ROW_TILE), :] = val[:, c * LANE:(c + 1) * LANE]


def _load_row_tiles(ref):
    n = ref.shape[0] // ROW_TILE
    return jnp.concatenate([ref[pl.ds(c, n, stride=ROW_TILE), :] for c in range(ROW_TILE)], axis=1)


def _softmax_rows(s_ref, p_ref, rows, sinks=None):
    s = s_ref[rows, :]
    m = jnp.max(s, axis=-1, keepdims=True)
    if sinks is not None:
        sink = jnp.concatenate([jnp.full((n, 1), v, f32) for v, n in sinks], axis=0)
        m = jnp.maximum(m, sink)
    p = jnp.exp(s - m)
    l = jnp.sum(p, axis=-1, keepdims=True)
    if sinks is not None:
        l = l + jnp.exp(sink - m)
    p_ref[rows, :] = (p * (1.0 / l)).astype(bf16)


def _ada_kernel(c_ref, w_ref, b_ref, o_ref):
    c = c_ref[...]
    s = c * jax.nn.sigmoid(c)
    s_hi = s.astype(bf16)
    s_lo = (s - s_hi.astype(f32)).astype(bf16)
    w = w_ref[0]
    w_hi = w.astype(bf16)
    w_lo = (w - w_hi.astype(f32)).astype(bf16)
    o_ref[0] = _dot(s_hi, w_hi) + _dot(s_hi, w_lo) + _dot(s_lo, w_hi) + b_ref[0]


def _ada(cvec, w_ada, b_ada):
    tn = 1536
    return pl.pallas_call(
        _ada_kernel,
        grid=(DEPTH, 6 * D_MODEL // tn),
        in_specs=[pl.BlockSpec((8, D_MODEL), lambda l, j: (0, 0)),
                  pl.BlockSpec((1, D_MODEL, tn), lambda l, j: (l, 0, j)),
                  pl.BlockSpec((1, 1, tn), lambda l, j: (l, 0, j))],
        out_specs=pl.BlockSpec((1, 8, tn), lambda l, j: (l, 0, j)),
        out_shape=jax.ShapeDtypeStruct((DEPTH, 8, 6 * D_MODEL), f32),
        compiler_params=_cparams("arbitrary", "arbitrary"),
        name="ada",
    )(cvec, w_ada, b_ada.reshape(DEPTH, 1, 6 * D_MODEL))


CACHE_WIDTHS = (W_KA, W_VA, W_B, W_B, KV_LORA, QK_ROPE)


def _inproj_ctx_kernel(layer, x_ref, g_ref, sh_ref, sc_ref, w_ref, gcq_ref, gckv_ref, wuq_ref, *refs):
    qa_ref, qb_ref, qc_ref, ka_ref, va_ref, kb_ref, vb_ref, ckv_ref, kr_ref = refs[-9:]
    h = _rms(x_ref[...], g_ref[...]) * (1.0 + sc_ref[0]) + sh_ref[0]
    p = _dot(h.astype(bf16), w_ref[...])
    qa_ref[...] = p[:, OFF_QA:OFF_QA + W_QA].astype(bf16)
    qb_ref[...] = p[:, OFF_QB:OFF_QB + W_B].astype(bf16)
    cqn = _rms(p[:, OFF_CQ:OFF_CQ + Q_LORA], gcq_ref[...])
    qc_ref[...] = _dot(cqn.astype(bf16), wuq_ref[...]).astype(bf16)
    caches = ((ka_ref, p[:, OFF_KA:OFF_KA + W_KA]), (va_ref, p[:, OFF_VA:OFF_VA + W_VA]),
              (kb_ref, p[:, OFF_KB:OFF_KB + W_B]), (vb_ref, p[:, OFF_VB:OFF_VB + W_B]),
              (ckv_ref, _rms(p[:, OFF_CKV:OFF_CKV + KV_LORA], gckv_ref[...])),
              (kr_ref, p[:, OFF_KR:OFF_KR + QK_ROPE]))
    for ref, val in caches:
        for b in range(TM_TOK // SEQ):
            rows = val[b * SEQ:(b + 1) * SEQ]
            if layer == 0:
                ref[b, 0] = rows
                for later in range(1, DEPTH):
                    ref[b, later] = jnp.zeros_like(rows)
            else:
                ref[b, 0] = rows


def _inproj_ctx(layer, prev_caches, x, g, shift, scale, w, gcq, gckv, wuq):
    tm = TM_TOK
    nb = tm // SEQ
    row = lambda i: (i, 0)
    const = lambda i: (0, 0)
    in_specs = [pl.BlockSpec((tm, D_MODEL), row),
                pl.BlockSpec((1, D_MODEL), const),
                pl.BlockSpec((1, 1, D_MODEL), lambda i: (0, 0, 0)),
                pl.BlockSpec((1, 1, D_MODEL), lambda i: (0, 0, 0)),
                pl.BlockSpec((D_MODEL, NW_CTX), const),
                pl.BlockSpec((1, Q_LORA), const),
                pl.BlockSpec((1, KV_LORA), const),
                pl.BlockSpec((Q_LORA, H_C * QC_PAD), const)]
    q_widths = (W_QA, W_B, H_C * QC_PAD)
    out_specs = [pl.BlockSpec((tm, wd), row) for wd in q_widths]
    out_shape = [jax.ShapeDtypeStruct((T_CTX, wd), bf16) for wd in q_widths]
    if layer == 0:
        out_specs += [pl.BlockSpec((nb, DEPTH, SEQ, wd), lambda i: (i, 0, 0, 0)) for wd in CACHE_WIDTHS]
        aliases, extra = {}, ()
    else:
        in_specs += [pl.BlockSpec(memory_space=pl.ANY) for _ in CACHE_WIDTHS]
        out_specs += [pl.BlockSpec((nb, 1, SEQ, wd), lambda i: (i, layer, 0, 0)) for wd in CACHE_WIDTHS]
        aliases = {8 + j: len(q_widths) + j for j in range(len(CACHE_WIDTHS))}
        extra = tuple(prev_caches)
    out_shape += [jax.ShapeDtypeStruct((BATCH, DEPTH, SEQ, wd), f32) for wd in CACHE_WIDTHS]
    outs = pl.pallas_call(
        functools.partial(_inproj_ctx_kernel, layer),
        grid=(T_CTX // tm,),
        in_specs=in_specs,
        out_specs=out_specs,
        out_shape=out_shape,
        input_output_aliases=aliases,
        compiler_params=_cparams("arbitrary"),
        name="inproj_ctx",
    )(x, g, shift, scale, w, gcq, gckv, wuq, *extra)
    return outs[:3], outs[3:]


def _inproj_lat_kernel(x_ref, g_ref, sh_ref, sc_ref, w_ref, gcq_ref, gckv_ref, wuq_ref,
                       cosa_ref, sina_ref, cosq_ref, sinq_ref, cosr_ref, sinr_ref,
                       qa_ref, ka_ref, va_ref, qb_ref, kb_ref, vb_ref, qc_ref, ckv_ref, kr_ref):
    h = _rms(x_ref[...], g_ref[...]) * (1.0 + sc_ref[0]) + sh_ref[0]
    p = _dot(h.astype(bf16), w_ref[...])
    cosa = cosa_ref[...]
    sina = sina_ref[...]
    qa = p[:, OFF_QA:OFF_QA + W_QA] * cosa + p[:, OFF_QA_P:OFF_QA_P + W_QA] * sina
    ka = p[:, OFF_KA:OFF_KA + W_KA] * cosa[:, :W_KA] + p[:, OFF_KA_P:OFF_KA_P + W_KA] * sina[:, :W_KA]
    kr = p[:, OFF_KR:OFF_KR + QK_ROPE] * cosr_ref[...] + p[:, OFF_KR_P:OFF_KR_P + QK_ROPE] * sinr_ref[...]
    qa_ref[...] = qa.astype(bf16)
    ka_ref[...] = ka.astype(bf16)
    va_ref[...] = p[:, OFF_VA:OFF_VA + W_VA].astype(bf16)
    qb_ref[...] = p[:, OFF_QB:OFF_QB + W_B].astype(bf16)
    kb_ref[...] = p[:, OFF_KB:OFF_KB + W_B].astype(bf16)
    vb_ref[...] = p[:, OFF_VB:OFF_VB + W_B].astype(bf16)
    cqn = _rms(p[:, OFF_CQ:OFF_CQ + Q_LORA], gcq_ref[...])
    q2 = _dot(cqn.astype(bf16), wuq_ref[...])
    nq = H_C * QC_PAD
    qc_ref[...] = (q2[:, :nq] * cosq_ref[...] + q2[:, nq:] * sinq_ref[...]).astype(bf16)
    ckv_ref[...] = _rms(p[:, OFF_CKV:OFF_CKV + KV_LORA], gckv_ref[...]).astype(bf16)
    kr_ref[...] = kr.astype(bf16)


def _inproj_lat(x, lat_row0, g, shift, scale, w, gcq, gckv, wuq2, tabs):
    tm = TM_LAT_IN
    per_b = DEC_SEQ // tm
    row0 = lat_row0 // tm
    xrow = lambda i: (row0 + i, 0)
    row = lambda i: (i, 0)
    const = lambda i: (0, 0)
    grp = lambda i: (1 + i // per_b, 0, 0)
    pos = lambda i: (i % per_b, 0)
    cosa, sina, cosq, sinq, cosr, sinr = tabs
    widths = (W_QA, W_KA, W_VA, W_B, W_B, W_B, H_C * QC_PAD, KV_LORA, QK_ROPE)
    return pl.pallas_call(
        _inproj_lat_kernel,
        grid=(T_LAT // tm,),
        in_specs=[pl.BlockSpec((tm, D_MODEL), xrow),
                  pl.BlockSpec((1, D_MODEL), const),
                  pl.BlockSpec((1, 1, D_MODEL), grp),
                  pl.BlockSpec((1, 1, D_MODEL), grp),
                  pl.BlockSpec((D_MODEL, NW_LAT), const),
                  pl.BlockSpec((1, Q_LORA), const),
                  pl.BlockSpec((1, KV_LORA), const),
                  pl.BlockSpec((Q_LORA, 2 * H_C * QC_PAD), const),
                  pl.BlockSpec((tm, W_QA), pos), pl.BlockSpec((tm, W_QA), pos),
                  pl.BlockSpec((tm, H_C * QC_PAD), pos), pl.BlockSpec((tm, H_C * QC_PAD), pos),
                  pl.BlockSpec((tm, QK_ROPE), pos), pl.BlockSpec((tm, QK_ROPE), pos)],
        out_specs=[pl.BlockSpec((tm, wd), row) for wd in widths],
        out_shape=[jax.ShapeDtypeStruct((T_LAT, wd), bf16) for wd in widths],
        compiler_params=_cparams("arbitrary"),
        name="inproj_lat",
    )(x, g, shift, scale, w, gcq, gckv, wuq2, cosa, sina, cosq, sinq, cosr, sinr)


CTX_BATCHES = 2


def _ctx_attn_kernel(sink_ref, qa_ref, ka_ref, va_ref, qb_ref, kb_ref, vb_ref, qc_ref, ckv_ref, kr_ref,
                     wukv_ref, wout_ref, x_ref, gate_ref, o_ref, o_scr, s_scr, p_scr):
    n = SEQ
    scale = HEAD_DIM ** -0.5
    scale_c = (QK_NOPE + QK_ROPE) ** -0.5

    def one_batch(sb, carry):
        rows = pl.ds(pl.multiple_of(sb * n, n), n)
        ka = ka_ref[sb, 0].astype(bf16)
        va = va_ref[sb, 0].astype(bf16)
        kb = kb_ref[sb, 0].astype(bf16)
        vb = vb_ref[sb, 0].astype(bf16)
        kv = _dot(ckv_ref[sb, 0].astype(bf16), wukv_ref[...]).astype(bf16)
        kr = kr_ref[sb, 0].astype(bf16)
        for h in range(H_A):
            g = h // G_A
            q = qa_ref[rows, h * HEAD_DIM:(h + 1) * HEAD_DIM]
            s_scr[h * n:(h + 1) * n, :] = _dot_nt(q, ka[:, g * HEAD_DIM:(g + 1) * HEAD_DIM]) * scale
        for h in range(H_B):
            sl = slice(h * HEAD_DIM, (h + 1) * HEAD_DIM)
            s_scr[(H_A + h) * n:(H_A + h + 1) * n, :] = _dot_nt(qb_ref[rows, sl], kb[:, sl]) * scale
        for h in range(H_C):
            qn = qc_ref[rows, h * QC_PAD:h * QC_PAD + QK_NOPE]
            qr = qc_ref[rows, h * QC_PAD + QK_NOPE:h * QC_PAD + QK_NOPE + QK_ROPE]
            c0 = h * (QK_NOPE + V_C)
            r0 = (H_A + H_B + h) * n
            s_scr[r0:r0 + n, :] = (_dot_nt(qn, kv[:, c0:c0 + QK_NOPE]) + _dot_nt(qr, kr)) * scale_c
        for pair in range((H_A + H_B + H_C) // 2):
            h0 = 2 * pair
            sinks = ((sink_ref[h0], n), (sink_ref[h0 + 1], n)) if h0 < H_A else None
            _softmax_rows(s_scr, p_scr, slice(h0 * n, (h0 + 2) * n), sinks)
        for h in range(H_A):
            g = h // G_A
            o_scr[:, h * HEAD_DIM:(h + 1) * HEAD_DIM] = _dot(p_scr[h * n:(h + 1) * n, :],
                                                             va[:, g * HEAD_DIM:(g + 1) * HEAD_DIM])
        for h in range(H_B):
            sl = slice(h * HEAD_DIM, (h + 1) * HEAD_DIM)
            o_scr[:, W_QA + h * HEAD_DIM:W_QA + (h + 1) * HEAD_DIM] = _dot(
                p_scr[(H_A + h) * n:(H_A + h + 1) * n, :], vb[:, sl])
        for h in range(H_C):
            c0 = h * (QK_NOPE + V_C)
            r0 = (H_A + H_B + h) * n
            off = W_QA + W_B + h * V_C
            o_scr[:, off:off + V_C] = _dot(p_scr[r0:r0 + n, :], kv[:, c0 + QK_NOPE:c0 + QK_NOPE + V_C])
        y = _dot(o_scr[...].astype(bf16), wout_ref[...])
        o_ref[rows, :] = x_ref[rows, :] + gate_ref[0] * y
        return carry

    lax.fori_loop(0, CTX_BATCHES, one_batch, 0)


def _ctx_attn(layer, sink, qs, caches, wukv, wout, x, gate):
    qa, qb, qc = qs
    ka, va, kb, vb, ckv, kr = caches
    row = lambda b: (b, 0)
    const = lambda b: (0, 0)
    slot = lambda b: (b, layer, 0, 0)
    nrow = CTX_BATCHES * SEQ
    qspec = lambda a: pl.BlockSpec((nrow, a.shape[1]), row)
    cspec = lambda a: pl.BlockSpec((CTX_BATCHES, 1, SEQ, a.shape[3]), slot)
    in_specs = [pl.BlockSpec(memory_space=pltpu.SMEM),
                qspec(qa), cspec(ka), cspec(va), qspec(qb), cspec(kb), cspec(vb), qspec(qc), cspec(ckv), cspec(kr)]
    in_specs += [pl.BlockSpec((KV_LORA, H_C * (QK_NOPE + V_C)), const),
                 pl.BlockSpec((D_MODEL, D_MODEL), const),
                 pl.BlockSpec((nrow, D_MODEL), row),
                 pl.BlockSpec((1, 1, D_MODEL), lambda b: (0, 0, 0))]
    return pl.pallas_call(
        _ctx_attn_kernel,
        grid=(BATCH // CTX_BATCHES,),
        in_specs=in_specs,
        out_specs=pl.BlockSpec((nrow, D_MODEL), row),
        out_shape=jax.ShapeDtypeStruct((T_CTX, D_MODEL), f32),
        scratch_shapes=[pltpu.VMEM((SEQ, D_MODEL), f32),
                        pltpu.VMEM(((H_A + H_B + H_C) * SEQ, SEQ), f32),
                        pltpu.VMEM(((H_A + H_B + H_C) * SEQ, SEQ), bf16)],
        compiler_params=_cparams("arbitrary"),
        name="ctx_attn",
    )(sink, qa, ka, va, qb, kb, vb, qc, ckv, kr, wukv, wout, x, gate)


def _lat_attn_kernel(sink_ref, qa_ref, qb_ref, qc_ref, ka_ref, va_ref, kb_ref, vb_ref, ckv_ref, kr_ref,
                     cak_ref, cav_ref, cbk_ref, cbv_ref, cckv_ref, ckr_ref, bias_ref,
                     wukv_ref, wout_ref, x_ref, gate_ref, o_ref, o_scr, kv_scr, sa, pa, sb, pb, sc, pc):
    qi = pl.program_id(1)
    nb = DEC_SEQ // BLOCK
    scale = HEAD_DIM ** -0.5

    @pl.when(qi == 0)
    def _():
        kv_scr[0:DEC_SEQ, :] = _dot(ckv_ref[...], wukv_ref[...]).astype(bf16)
        kv_scr[DEC_SEQ:DEC_SEQ + PAST_LEN, :] = _dot(cckv_ref[0, 0].astype(bf16), wukv_ref[...]).astype(bf16)

    def blk(ref, j):
        idx = jnp.clip(qi + j, 0, nb - 1)
        return ref[pl.ds(pl.multiple_of(idx * BLOCK, BLOCK), BLOCK), :]

    ka = jnp.concatenate([blk(ka_ref, -1), blk(ka_ref, 0), blk(ka_ref, 1), cak_ref[0, 0].astype(bf16)], axis=0)
    va = jnp.concatenate([blk(va_ref, -1), blk(va_ref, 0), blk(va_ref, 1), cav_ref[0, 0].astype(bf16)], axis=0)
    nk_a = 3 * BLOCK + PAST_LEN
    r = lax.broadcasted_iota(jnp.int32, (BLOCK, nk_a), 0)
    c = lax.broadcasted_iota(jnp.int32, (BLOCK, nk_a), 1)
    valid = (((c < BLOCK) & (c >= r) & (qi > 0))
             | ((c >= BLOCK) & (c < 2 * BLOCK))
             | ((c >= 2 * BLOCK) & (c < 3 * BLOCK) & (c - 2 * BLOCK <= r) & (qi < nb - 1))
             | (c >= 3 * BLOCK))
    for h in range(H_A):
        g = h // G_A
        q = qa_ref[:, h * HEAD_DIM:(h + 1) * HEAD_DIM]
        s = _dot_nt(q, ka[:, g * HEAD_DIM:(g + 1) * HEAD_DIM]) * scale
        sa[h * BLOCK:(h + 1) * BLOCK, :] = jnp.where(valid, s, NEG)

    cbk = cbk_ref[0, 0].astype(bf16)
    cbv = cbv_ref[0, 0].astype(bf16)
    rows_per_blk = BLOCK // GRID_W
    nloc = NA_ROWS * GRID_W
    vcats = []
    for half in range(rows_per_blk):
        grow = qi * rows_per_blk + half
        start = jnp.clip(grow - NA_ROWS // 2, 0, ROWS - NA_ROWS)
        kloc = kb_ref[pl.ds(pl.multiple_of(start * GRID_W, GRID_W), nloc), :]
        vloc = vb_ref[pl.ds(pl.multiple_of(start * GRID_W, GRID_W), nloc), :]
        vcats.append(jnp.concatenate([vloc, cbv], axis=0))
        qrows = slice(half * GRID_W, (half + 1) * GRID_W)
        dr0 = start - grow + (NA_ROWS - 1)
        for h in range(H_B):
            sl = slice(h * HEAD_DIM, (h + 1) * HEAD_DIM)
            q = qb_ref[qrows, sl]
            bias = jnp.concatenate([bias_ref[h, dr0 + 2 * j] for j in range(NA_ROWS // 2)], axis=1)
            s_loc = _dot_nt(q, kloc[:, sl]) * scale + bias
            s_ctx = _dot_nt(q, cbk[:, sl]) * scale
            r0 = (half * H_B + h) * GRID_W
            sb[r0:r0 + GRID_W, :] = jnp.concatenate([s_loc, s_ctx], axis=1)

    kr = jnp.concatenate([kr_ref[...], ckr_ref[0, 0].astype(bf16)], axis=0)
    scale_c = (QK_NOPE + QK_ROPE) ** -0.5
    for h in range(H_C):
        qn = qc_ref[:, h * QC_PAD:h * QC_PAD + QK_NOPE]
        qr = qc_ref[:, h * QC_PAD + QK_NOPE:h * QC_PAD + QK_NOPE + QK_ROPE]
        c0 = h * (QK_NOPE + V_C)
        sc[h * BLOCK:(h + 1) * BLOCK, :] = (_dot_nt(qn, kv_scr[:, c0:c0 + QK_NOPE]) + _dot_nt(qr, kr)) * scale_c

    for pair in range(H_A // 2):
        h0 = 2 * pair
        _softmax_rows(sa, pa, slice(h0 * BLOCK, (h0 + 2) * BLOCK), ((sink_ref[h0], BLOCK), (sink_ref[h0 + 1], BLOCK)))
    for blk2 in range(rows_per_blk * H_B // 2):
        _softmax_rows(sb, pb, slice(blk2 * 2 * GRID_W, (blk2 + 1) * 2 * GRID_W))
    for h in range(H_C):
        _softmax_rows(sc, pc, slice(h * BLOCK, (h + 1) * BLOCK))

    for h in range(H_A):
        g = h // G_A
        o_scr[:, h * HEAD_DIM:(h + 1) * HEAD_DIM] = _dot(pa[h * BLOCK:(h + 1) * BLOCK, :],
                                                         va[:, g * HEAD_DIM:(g + 1) * HEAD_DIM])
    for half in range(rows_per_blk):
        qrows = slice(half * GRID_W, (half + 1) * GRID_W)
        for h in range(H_B):
            sl = slice(h * HEAD_DIM, (h + 1) * HEAD_DIM)
            r0 = (half * H_B + h) * GRID_W
            o_scr[qrows, W_QA + h * HEAD_DIM:W_QA + (h + 1) * HEAD_DIM] = _dot(pb[r0:r0 + GRID_W, :],
                                                                             vcats[half][:, sl])
    for h in range(H_C):
        c0 = h * (QK_NOPE + V_C)
        off = W_QA + W_B + h * V_C
        o_scr[:, off:off + V_C] = _dot(pc[h * BLOCK:(h + 1) * BLOCK, :], kv_scr[:, c0 + QK_NOPE:c0 + QK_NOPE + V_C])

    y = _dot(o_scr[...].astype(bf16), wout_ref[...])
    o_ref[...] = x_ref[...] + gate_ref[0] * y


def _lat_attn(layer, sink, proj, caches, bias_tab, wukv, wout, x, lat_row0, gate):
    qa, ka, va, qb, kb, vb, qc, ckv, kr = proj
    nb = DEC_SEQ // BLOCK
    qrow = lambda b, q: (b * nb + q, 0)
    xrow = lambda b, q: (lat_row0 // BLOCK + b * nb + q, 0)
    brow = lambda b, q: (b, 0)
    const = lambda b, q: (0, 0)
    cidx = lambda b, q: (b, layer, 0, 0)
    in_specs = [pl.BlockSpec(memory_space=pltpu.SMEM)]
    in_specs += [pl.BlockSpec((BLOCK, a.shape[1]), qrow) for a in (qa, qb, qc)]
    in_specs += [pl.BlockSpec((DEC_SEQ, a.shape[1]), brow) for a in (ka, va, kb, vb, ckv, kr)]
    in_specs += [pl.BlockSpec((1, 1, PAST_LEN, a.shape[3]), cidx) for a in caches]
    in_specs += [pl.BlockSpec(bias_tab.shape, lambda b, q: (0, 0, 0, 0)),
                 pl.BlockSpec((KV_LORA, H_C * (QK_NOPE + V_C)), const),
                 pl.BlockSpec((D_MODEL, D_MODEL), const),
                 pl.BlockSpec((BLOCK, D_MODEL), xrow),
                 pl.BlockSpec((1, 1, D_MODEL), lambda b, q: (1 + b, 0, 0))]
    return pl.pallas_call(
        _lat_attn_kernel,
        grid=(DEC_BATCH, nb),
        in_specs=in_specs,
        out_specs=pl.BlockSpec((BLOCK, D_MODEL), qrow),
        out_shape=jax.ShapeDtypeStruct((T_LAT, D_MODEL), f32),
        scratch_shapes=[pltpu.VMEM((BLOCK, D_MODEL), f32),
                        pltpu.VMEM((DEC_SEQ + PAST_LEN, H_C * (QK_NOPE + V_C)), bf16)]
        + [pltpu.VMEM(shape, dt) for shape in ((H_A * BLOCK, 3 * BLOCK + PAST_LEN),
                                               (H_B * BLOCK, NA_ROWS * GRID_W + PAST_LEN),
                                               (H_C * BLOCK, DEC_SEQ + PAST_LEN)) for dt in (f32, bf16)],
        compiler_params=_cparams("arbitrary", "arbitrary"),
        name="lat_attn",
    )(sink, qa, qb, qc, ka, va, kb, vb, ckv, kr, *caches, bias_tab, wukv, wout, x, gate)


def _pick_stream(xc_ref, xl_ref, x_scr):
    i = pl.program_id(0)

    @pl.when(i < T_CTX // TM_TOK)
    def _():
        x_scr[...] = xc_ref[...]

    @pl.when(i >= T_CTX // TM_TOK)
    def _():
        x_scr[...] = xl_ref[...]

    return x_scr[...]


def _stream_specs(lat_row0):
    n_ctx = T_CTX // TM_TOK
    return [pl.BlockSpec((TM_TOK, D_MODEL), lambda i: (jnp.minimum(i, n_ctx - 1), 0)),
            pl.BlockSpec((TM_TOK, D_MODEL), lambda i: (lat_row0 // TM_TOK + jnp.maximum(i - n_ctx, 0), 0))]


def _router_kernel(xc_ref, xl_ref, g_ref, sh_ref, sc_ref, wr_ref, br_ref, h_ref, e_ref, gt_ref, x_scr):
    h = _rms(_pick_stream(xc_ref, xl_ref, x_scr), g_ref[...]) * (1.0 + sc_ref[0]) + sh_ref[0]
    _store_row_tiles(h_ref, h)
    h_hi = h.astype(bf16)
    h_lo = (h - h_hi.astype(f32)).astype(bf16)
    w = wr_ref[...]
    w_hi = w.astype(bf16)
    w_lo = (w - w_hi.astype(f32)).astype(bf16)
    logits = _dot(h_hi, w_hi) + _dot(h_hi, w_lo) + _dot(h_lo, w_hi) + br_ref[...]
    lane = lax.broadcasted_iota(jnp.int32, logits.shape, 1).astype(f32)
    l = jnp.where(lane < N_EXPERTS, logits, -jnp.inf)
    tops, idxs = [], []
    for _ in range(TOP_K):
        m = jnp.max(l, axis=-1, keepdims=True)
        idx = jnp.min(jnp.where(l == m, lane, float(LANE)), axis=-1, keepdims=True)
        tops.append(m)
        idxs.append(idx)
        l = jnp.where(lane == idx, -jnp.inf, l)
    ex = [jnp.exp(t - tops[0]) for t in tops]
    den = ex[0] + ex[1] + ex[2] + ex[3]
    e_out = jnp.zeros(logits.shape, f32)
    g_out = jnp.zeros(logits.shape, f32)
    for k in range(TOP_K):
        e_out = jnp.where(lane == k, idxs[k], e_out)
        g_out = jnp.where(lane == k, ex[k] / den, g_out)
    e_ref[...] = e_out.astype(jnp.int32)
    gt_ref[...] = g_out


def _group_of_tile(i):
    per_b = DEC_SEQ // TM_TOK
    n_ctx = T_CTX // TM_TOK
    return jnp.where(i < n_ctx, 0, 1 + (i - n_ctx) // per_b)


def _router(xc, xl, lat_row0, g, shift, scale, wr, br):
    tm = TM_TOK
    row = lambda i: (i, 0)
    const = lambda i: (0, 0)
    grp = lambda i: (_group_of_tile(i), 0, 0)
    return pl.pallas_call(
        _router_kernel,
        grid=(T_ALL // tm,),
        in_specs=_stream_specs(lat_row0) +
                 [pl.BlockSpec((1, D_MODEL), const),
                  pl.BlockSpec((1, 1, D_MODEL), grp),
                  pl.BlockSpec((1, 1, D_MODEL), grp),
                  pl.BlockSpec((D_MODEL, LANE), const),
                  pl.BlockSpec((1, LANE), const)],
        out_specs=[pl.BlockSpec((tm * ROW_TILE, LANE), row), pl.BlockSpec((tm, LANE), row),
                   pl.BlockSpec((tm, LANE), row)],
        out_shape=[jax.ShapeDtypeStruct((T_ALL * ROW_TILE, LANE), f32),
                   jax.ShapeDtypeStruct((T_ALL, LANE), jnp.int32),
                   jax.ShapeDtypeStruct((T_ALL, LANE), f32)],
        scratch_shapes=[pltpu.VMEM((tm, D_MODEL), f32)],
        compiler_params=_cparams("arbitrary"),
        name="router",
    )(xc, xl, g, shift, scale, wr, br)


def _dispatch_kernel(tok_ref, nu_ref, h_hbm, o_ref, hv, xg, hsem):
    tm = TM_MOE
    i = pl.program_id(0)

    @pl.when(i == 0)
    def _():
        resident = pltpu.make_async_copy(h_hbm, hv, hsem.at[0])
        resident.start()
        resident.wait()

    def one_block(sub, carry):
        blk = i * DISPATCH_BLOCKS + sub
        rows = pl.ds(pl.multiple_of(sub * tm, tm), tm)

        @pl.when(blk < nu_ref[0])
        def _():
            for r in range(tm):
                t = tok_ref[blk * tm + r]
                xg[pl.ds(r, ROW_TILE, stride=tm + 1), :] = hv[pl.ds(pl.multiple_of(t * ROW_TILE, ROW_TILE),
                                                                 ROW_TILE), :]
            o_ref[rows, :] = jnp.concatenate([xg[pl.ds(c * (tm + 1), tm), :] for c in range(ROW_TILE)],
                                             axis=1).astype(bf16)

        @pl.when(blk >= nu_ref[0])
        def _():
            o_ref[rows, :] = jnp.zeros((tm, D_MODEL), bf16)

        return carry

    lax.fori_loop(0, DISPATCH_BLOCKS, one_block, 0)


def _dispatch(row_tok, n_used, h):
    tm = TM_MOE
    return pl.pallas_call(
        _dispatch_kernel,
        grid_spec=pltpu.PrefetchScalarGridSpec(
            num_scalar_prefetch=2,
            grid=(N_MOE_BLOCKS // DISPATCH_BLOCKS,),
            in_specs=[pl.BlockSpec(memory_space=pl.ANY)],
            out_specs=pl.BlockSpec((DISPATCH_BLOCKS * tm, D_MODEL), lambda i, tok, nu: (i, 0)),
            scratch_shapes=[pltpu.VMEM((T_ALL * ROW_TILE, LANE), f32), pltpu.VMEM(((tm + 1) * ROW_TILE, LANE), f32),
                            pltpu.SemaphoreType.DMA((1,))]),
        out_shape=jax.ShapeDtypeStruct((N_MOE_BLOCKS * tm, D_MODEL), bf16),
        compiler_params=_cparams("arbitrary"),
        name="dispatch",
    )(row_tok, n_used, h)


def _moe_kernel(layer, be_ref, nu_ref, nxt_ref, dst_ref, x_ref, wgu_hbm, bgu_ref, wd_hbm, bd_ref, y_hbm,
                y0, y1, wgu_st, wd_st, wgu_bf, wd_bf, wsem, ssem):
    tm = TM_MOE
    i = pl.program_id(0)
    nb = pl.num_programs(0)
    used = i < nu_ref[0]
    yb = (y0, y1)

    def out_tile(row):
        return pl.ds(pl.multiple_of(row * ROW_TILE, ROW_TILE), ROW_TILE)

    def scatter_desc(buf, r, dst_row, s):
        return pltpu.make_async_copy(buf.at[out_tile(r)], y_hbm.at[out_tile(dst_row)], ssem.at[s])

    def scatter_wait(s):
        pltpu.make_async_copy(yb[s], y_hbm.at[pl.ds(0, tm * ROW_TILE)], ssem.at[s]).wait()

    def scatter_start(blk, s, unrolled):
        if unrolled:
            for r in range(tm):
                scatter_desc(yb[s], r, dst_ref[(blk + 1) * tm + r], s).start(priority=r % 2)
        else:
            def body(r, carry):
                scatter_desc(yb[s], r, dst_ref[(blk + 1) * tm + r], s).start()
                return carry
            lax.fori_loop(0, tm, body, 0, unroll=8)

    def weight_copies(e):
        return (pltpu.make_async_copy(wgu_hbm.at[layer, e], wgu_st, wsem.at[0]),
                pltpu.make_async_copy(wd_hbm.at[layer, e], wd_st, wsem.at[1]))

    @pl.when(i == 0)
    def _():
        for s in range(2):
            yb[s][...] = jnp.zeros_like(yb[s])
            dummy = pltpu.make_async_copy(yb[s], y_hbm.at[pl.ds((N_ASSIGN + s * tm) * ROW_TILE, tm * ROW_TILE)],
                                          ssem.at[s])
            dummy.start()
            dummy.wait()
        for cp in weight_copies(be_ref[0]):
            cp.start(priority=1)

    first = jnp.logical_and(used, jnp.logical_or(i == 0, be_ref[i] != be_ref[jnp.maximum(i - 1, 0)]))

    @pl.when(first)
    def _():
        for cp in weight_copies(0):
            cp.wait()
        wgu_bf[...] = wgu_st[...].astype(bf16)
        wd_bf[...] = wd_st[...].astype(bf16)

        @pl.when(nxt_ref[i] >= 0)
        def _():
            for cp in weight_copies(nxt_ref[i]):
                cp.start(priority=1)

    def step(par):
        cur, oth = par, 1 - par

        @pl.when(jnp.logical_and(i >= 1, i - 2 < nu_ref[0]))
        def _():
            scatter_wait(cur)

        @pl.when(used)
        def _():
            scatter_start(i - 1, oth, unrolled=True)
            gu = _dot(x_ref[...], wgu_bf[...]) + bgu_ref[0, 0]
            x_glu = jnp.minimum(gu[:, :D_FF], SWIGLU_LIMIT)
            x_lin = jnp.clip(gu[:, D_FF:], -SWIGLU_LIMIT, SWIGLU_LIMIT)
            act = x_glu * jax.nn.sigmoid(SWIGLU_ALPHA * x_glu) * (x_lin + 1.0)
            _store_row_tiles(yb[cur], _dot(act.astype(bf16), wd_bf[...]) + bd_ref[0, 0])

        flush = jnp.logical_and(jnp.logical_not(used), i - 1 < nu_ref[0])

        @pl.when(flush)
        def _():
            scatter_start(i - 1, oth, unrolled=False)

        @pl.when(jnp.logical_and(flush, i == nb - 1))
        def _():
            scatter_wait(oth)

    @pl.when(i % 2 == 0)
    def _():
        step(0)

    @pl.when(i % 2 == 1)
    def _():
        step(1)


def _moe(layer, routing, h, w_gu, b_gu, w_down, b_down):
    tm = TM_MOE
    block_e, n_used, nxt_e, row_tok, row_dst = routing
    xs = _dispatch(row_tok, n_used, h)
    ex4 = lambda i, be, nu, nxt, dst: (layer, be[i], 0, 0)
    return pl.pallas_call(
        functools.partial(_moe_kernel, layer),
        grid_spec=pltpu.PrefetchScalarGridSpec(
            num_scalar_prefetch=4,
            grid=(N_MOE_BLOCKS,),
            in_specs=[pl.BlockSpec((tm, D_MODEL), lambda i, be, nu, nxt, dst: (i, 0)),
                      pl.BlockSpec(memory_space=pl.ANY),
                      pl.BlockSpec((1, 1, 1, 2 * D_FF), ex4),
                      pl.BlockSpec(memory_space=pl.ANY),
                      pl.BlockSpec((1, 1, 1, D_MODEL), ex4)],
            out_specs=pl.BlockSpec(memory_space=pl.ANY),
            scratch_shapes=[pltpu.VMEM((tm * ROW_TILE, LANE), f32), pltpu.VMEM((tm * ROW_TILE, LANE), f32),
                            pltpu.VMEM((D_MODEL, 2 * D_FF), f32), pltpu.VMEM((D_FF, D_MODEL), f32),
                            pltpu.VMEM((D_MODEL, 2 * D_FF), bf16), pltpu.VMEM((D_FF, D_MODEL), bf16),
                            pltpu.SemaphoreType.DMA((2,)), pltpu.SemaphoreType.DMA((2,))]),
        out_shape=jax.ShapeDtypeStruct(((N_ASSIGN + 2 * tm) * ROW_TILE, LANE), f32),
        compiler_params=_cparams("arbitrary"),
        name="moe",
    )(block_e, n_used, nxt_e, row_dst, xs, w_gu, b_gu.reshape(DEPTH, N_EXPERTS, 1, 2 * D_FF),
      w_down, b_down.reshape(DEPTH, N_EXPERTS, 1, D_MODEL))


def _combine_kernel(final, xc_ref, xl_ref, y0_ref, y1_ref, y2_ref, y3_ref, gt_ref, gate_ref, gf_ref, *rest):
    x_scr = rest[-1]
    gt = gt_ref[...]
    f = gt[:, 0:1] * _load_row_tiles(y0_ref)
    for k, y_ref in ((1, y1_ref), (2, y2_ref), (3, y3_ref)):
        f = f + gt[:, k:k + 1] * _load_row_tiles(y_ref)
    out = _pick_stream(xc_ref, xl_ref, x_scr) + gate_ref[0] * f
    if not final:
        rest[0][...] = out
        return
    out = _rms(out, gf_ref[...])
    oc_ref, ol_ref = rest[0], rest[1]
    i = pl.program_id(0)

    @pl.when(i < T_CTX // TM_TOK)
    def _():
        oc_ref[...] = out

    @pl.when(i >= T_CTX // TM_TOK)
    def _():
        ol_ref[...] = out


def _combine(final, xc, xl, lat_row0, y, gates, gate, g_final):
    tm = TM_TOK
    nt = T_ALL // tm
    n_ctx = T_CTX // tm
    row = lambda i: (i, 0)
    const = lambda i: (0, 0)
    grp = lambda i: (_group_of_tile(i), 0, 0)
    ysel = [pl.BlockSpec((tm * ROW_TILE, LANE), functools.partial(lambda k, i: (k * nt + i, 0), k))
            for k in range(TOP_K)]
    if final:
        out_specs = [pl.BlockSpec((tm, D_MODEL), lambda i: (jnp.minimum(i, n_ctx - 1), 0)),
                     pl.BlockSpec((tm, D_MODEL), lambda i: (jnp.maximum(i - n_ctx, 0), 0))]
        out_shape = [jax.ShapeDtypeStruct((T_CTX, D_MODEL), f32), jax.ShapeDtypeStruct((T_LAT, D_MODEL), f32)]
    else:
        out_specs = pl.BlockSpec((tm, D_MODEL), row)
        out_shape = jax.ShapeDtypeStruct((T_ALL, D_MODEL), f32)
    return pl.pallas_call(
        functools.partial(_combine_kernel, final),
        grid=(nt,),
        in_specs=_stream_specs(lat_row0) + ysel +
                 [pl.BlockSpec((tm, LANE), row),
                  pl.BlockSpec((1, 1, D_MODEL), grp),
                  pl.BlockSpec((1, D_MODEL), const)],
        out_specs=out_specs,
        out_shape=out_shape,
        scratch_shapes=[pltpu.VMEM((tm, D_MODEL), f32)],
        compiler_params=_cparams("arbitrary"),
        name="combine",
    )(xc, xl, y, y, y, y, gates, gate, g_final)


def _rope_head_tables(d):
    nf = d // 4
    half = d // 2
    t = np.arange(DEC_SEQ)
    inv = ROPE_BASE ** (-np.arange(nf, dtype=np.float32) / nf)
    i = np.arange(d)
    pos = np.where(i[None, :] < half, (t // GRID_W)[:, None], (t % GRID_W)[:, None]).astype(np.float32)
    ang = pos * inv[i % nf][None, :].astype(np.float32)
    first = (i % half) < nf
    cos = np.cos(ang)
    sin = np.where(first[None, :], -np.sin(ang), np.sin(ang))
    partner = np.where(first, i + nf, i - nf)
    return cos.astype(np.float32), sin.astype(np.float32), partner


def _rope_tables():
    cos64, sin64, _ = _rope_head_tables(HEAD_DIM)
    cos32, sin32, _ = _rope_head_tables(QK_ROPE)
    cosa = np.tile(cos64, (1, H_A))
    sina = np.tile(sin64, (1, H_A))
    cosq1 = np.concatenate([np.ones((DEC_SEQ, QK_NOPE), np.float32), cos32,
                            np.ones((DEC_SEQ, QC_PAD - QK_NOPE - QK_ROPE), np.float32)], axis=1)
    sinq1 = np.concatenate([np.zeros((DEC_SEQ, QK_NOPE), np.float32), sin32,
                            np.zeros((DEC_SEQ, QC_PAD - QK_NOPE - QK_ROPE), np.float32)], axis=1)
    cosq = np.tile(cosq1, (1, H_C))
    sinq = np.tile(sinq1, (1, H_C))
    return tuple(jnp.asarray(a) for a in (cosa, sina, cosq, sinq, cos32, sin32))


def _pad_cols(w, n):
    return jnp.pad(w, ((0, 0), (0, n - w.shape[1])))


def _layer_weights(w_in, w_uq):
    cuts = np.cumsum((W_QA, W_KA, W_VA, W_B, W_B, W_B, Q_LORA, KV_LORA, QK_ROPE))[:-1]
    qa, ka, va, qb, kb, vb, cq, ckv, kr = jnp.split(w_in, [int(c) for c in cuts], axis=1)
    _, _, p64 = _rope_head_tables(HEAD_DIM)
    _, _, p32 = _rope_head_tables(QK_ROPE)
    pa = np.concatenate([h * HEAD_DIM + p64 for h in range(H_A)])
    base = jnp.concatenate([qa, ka, va, _pad_cols(qb, 384), _pad_cols(kb, 384), _pad_cols(vb, 384), cq, ckv,
                            _pad_cols(kr, 128)], axis=1)
    w_ctx = base.astype(bf16)
    w_lat = jnp.concatenate([base, qa[:, pa], ka[:, pa[:W_KA]], _pad_cols(kr[:, p32], 128)], axis=1).astype(bf16)
    hq = QK_NOPE + QK_ROPE
    heads = [_pad_cols(w_uq[:, h * hq:(h + 1) * hq], QC_PAD) for h in range(H_C)]
    pq = np.concatenate([np.arange(QK_NOPE), QK_NOPE + p32])
    heads_p = [_pad_cols(w_uq[:, h * hq:(h + 1) * hq][:, pq], QC_PAD) for h in range(H_C)]
    wuq = jnp.concatenate(heads, axis=1).astype(bf16)
    wuq2 = jnp.concatenate(heads + heads_p, axis=1).astype(bf16)
    return w_ctx, w_lat, wuq, wuq2


def _bias_table(rpb):
    col = np.arange(GRID_W)
    col_start = np.clip(col - NA_COLS // 2, 0, GRID_W - NA_COLS)
    col_ok = (col[None, :] >= col_start[:, None]) & (col[None, :] < col_start[:, None] + NA_COLS)
    dc = np.clip(col[None, :] - col[:, None] + (NA_COLS - 1), 0, 2 * NA_COLS - 2)
    onehot = (dc[None] == np.arange(2 * NA_COLS - 1)[:, None, None]).astype(np.float32)
    expanded = jnp.einsum('hrd,dqk->hrqk', rpb.astype(f32), jnp.asarray(onehot), precision=lax.Precision.HIGHEST)
    blocks = jnp.where(col_ok[None, None], expanded, NEG)
    return jnp.concatenate([blocks[:, :-1], blocks[:, 1:]], axis=-1)


def _routing(top_e):
    tm = TM_MOE
    key_bits = 16
    pad_mark = (1 << key_bits) - 1
    flat_e = top_e.T.reshape(N_ASSIGN)
    experts = jnp.arange(N_EXPERTS, dtype=jnp.int32)
    counts = jnp.sum((flat_e[:, None] == experts[None, :]).astype(jnp.int32), axis=0)
    nblk = (counts + tm - 1) // tm
    blk_end = jnp.cumsum(nblk)
    pad_end = jnp.cumsum(nblk * tm - counts)
    slots = jnp.arange(N_MOE_BLOCKS * tm - N_ASSIGN, dtype=jnp.int32)
    pad_e = jnp.sum((pad_end[None, :] <= slots[:, None]).astype(jnp.int32), axis=1)
    keys = jnp.concatenate([(flat_e << key_bits) + jnp.arange(N_ASSIGN, dtype=jnp.int32),
                            (pad_e << key_bits) + pad_mark])
    asg = (jnp.sort(keys, stable=False) & pad_mark).reshape(N_MOE_BLOCKS, tm)
    valid = asg != pad_mark
    blocks = jnp.arange(N_MOE_BLOCKS, dtype=jnp.int32)
    r = jnp.arange(tm, dtype=jnp.int32)[None, :]
    tok = jnp.where(valid, asg % T_ALL, 0)
    row_dst = jnp.where(valid, asg, N_ASSIGN + (blocks[:, None] % 2) * tm + r)
    row_dst = jnp.concatenate([N_ASSIGN + tm + r, row_dst], axis=0).reshape(-1)
    block_e = jnp.minimum(jnp.sum((blk_end[None, :] <= blocks[:, None]).astype(jnp.int32), axis=1), N_EXPERTS - 1)
    n_used = blk_end[-1].astype(jnp.int32).reshape(1)
    has = jnp.where(counts > 0, experts, N_EXPERTS)
    later = experts[None, :] > experts[:, None]
    nxt = jnp.min(jnp.where(later, has[None, :], N_EXPERTS), axis=1)
    nxt = jnp.where(nxt >= N_EXPERTS, -1, nxt)
    sel = (block_e[:, None] == experts[None, :]).astype(jnp.int32)
    nxt_e = jnp.sum(sel * nxt[None, :], axis=1)
    i32 = lambda a: a.astype(jnp.int32)
    return i32(block_e), n_used, i32(nxt_e), i32(tok).reshape(-1), i32(row_dst)


def kernel(x_prompt, x_sample, cache_a_k, cache_a_v, cache_b_k, cache_b_v, cache_c_kv, cache_c_kr, c, c_ctx, w_ada, b_ada, g_attn, g_ffn, w_in, sink_a, rpb_b, g_cq, g_ckv, w_uq, w_ukv, w_out, w_router, b_router, w_gu, b_gu, w_down, b_down, g_final):
    xc, xl, lat_row0 = x_prompt.reshape(T_CTX, D_MODEL), x_sample.reshape(T_LAT, D_MODEL), 0
    cvec = jnp.concatenate([c_ctx[None, :], c, jnp.zeros((8 - N_GROUPS, D_MODEL), f32)], axis=0)
    mods = _ada(cvec, w_ada, b_ada)[:, :N_GROUPS].reshape(DEPTH, N_GROUPS, 6, 1, D_MODEL)
    tabs = _rope_tables()
    caches = (cache_a_k.reshape(DEC_BATCH, DEPTH, PAST_LEN, W_KA), cache_a_v.reshape(DEC_BATCH, DEPTH, PAST_LEN, W_VA),
              cache_b_k.reshape(DEC_BATCH, DEPTH, PAST_LEN, W_B), cache_b_v.reshape(DEC_BATCH, DEPTH, PAST_LEN, W_B),
              cache_c_kv, cache_c_kr)
    new = None
    for layer in range(DEPTH):
        m = [mods[layer, :, j] for j in range(6)]
        w_ctx, w_lat, wuq, wuq2 = _layer_weights(w_in[layer], w_uq[layer])
        wukv = w_ukv[layer].astype(bf16)
        wout = w_out[layer].astype(bf16)
        g1 = g_attn[layer][None, :]
        gcq = g_cq[layer][None, :]
        gckv = g_ckv[layer][None, :]
        sink = sink_a[layer]

        qs, new = _inproj_ctx(layer, new, xc, g1, m[0], m[1], w_ctx, gcq, gckv, wuq)
        x_ctx = _ctx_attn(layer, sink, qs, new, wukv, wout, xc, m[2])

        plat = _inproj_lat(xl, lat_row0, g1, m[0], m[1], w_lat, gcq, gckv, wuq2, tabs)
        x_lat = _lat_attn(layer, sink, plat, caches, _bias_table(rpb_b[layer]), wukv, wout, xl, lat_row0, m[2])

        wr = _pad_cols(w_router[layer], LANE)
        br = _pad_cols(b_router[layer][None, :], LANE)
        h2, top_e, gates = _router(x_ctx, x_lat, 0, g_ffn[layer][None, :], m[3], m[4], wr, br)
        y = _moe(layer, _routing(top_e[:, :TOP_K]), h2, w_gu, b_gu, w_down, b_down)
        x = _combine(layer == DEPTH - 1, x_ctx, x_lat, 0, y, gates, m[5], g_final[None, :])
        xc, xl, lat_row0 = x, x, T_CTX

    y_prompt = x[0].reshape(BATCH, SEQ, D_MODEL)
    y_sample = x[1].reshape(DEC_BATCH, DEC_SEQ, D_MODEL)
    shapes = ((KV_A, HEAD_DIM), (KV_A, HEAD_DIM), (H_B, HEAD_DIM), (H_B, HEAD_DIM), (KV_LORA,), (QK_ROPE,))
    outs = [a.reshape((BATCH, DEPTH, SEQ) + s) for a, s in zip(new, shapes)]
    return (y_prompt, y_sample, *outs)
```

```python
import functools

import numpy as np
import jax
import jax.numpy as jnp
from jax import lax
from jax.experimental import pallas as pl
from jax.experimental.pallas import tpu as pltpu

D_MODEL = 1024
BATCH = 32
SEQ = 256
DEPTH = 2
DEC_BATCH = 2
DEC_SEQ = 1024
PAST_LEN = 512
GRID_W = 64
HEAD_DIM = 64
H_A = 6
KV_A = 2
G_A = H_A // KV_A
WINDOW = 128
BLOCK = 128
H_B = 5
NA_ROWS = 8
NA_COLS = 16
H_C = 5
Q_LORA = 384
KV_LORA = 256
QK_NOPE = 64
QK_ROPE = 32
V_C = 64
N_EXPERTS = 32
TOP_K = 4
D_FF = 1024
SWIGLU_ALPHA = 1.702
SWIGLU_LIMIT = 7.0
ROPE_BASE = 10000.0
EPS = 1e-6
NEG = -1e30

T_CTX = BATCH * SEQ
T_LAT = DEC_BATCH * DEC_SEQ
T_ALL = T_CTX + T_LAT
N_GROUPS = 1 + DEC_BATCH
LANE = 128
QC_PAD = 128
ROWS = DEC_SEQ // GRID_W

W_QA, W_KA, W_VA = H_A * HEAD_DIM, KV_A * HEAD_DIM, KV_A * HEAD_DIM
W_B = H_B * HEAD_DIM
OFF_QA = 0
OFF_KA = 384
OFF_VA = 512
OFF_QB = 640
OFF_KB = 1024
OFF_VB = 1408
OFF_CQ = 1792
OFF_CKV = 2176
OFF_KR = 2432
NW_CTX = 2560
OFF_QA_P = 2560
OFF_KA_P = 2944
OFF_KR_P = 3072
NW_LAT = 3200

TM_TOK = 512
TM_LAT_IN = 512
TM_MOE = 256
N_ASSIGN = T_ALL * TOP_K
N_MOE_BLOCKS = N_ASSIGN // TM_MOE + N_EXPERTS
DISPATCH_BLOCKS = 8
VMEM_LIMIT = 56 * 1024 * 1024

f32 = jnp.float32
bf16 = jnp.bfloat16


def _cparams(*sem):
    return pltpu.CompilerParams(dimension_semantics=sem, vmem_limit_bytes=VMEM_LIMIT)


def _rms(xf, g):
    return xf * lax.rsqrt(jnp.mean(xf * xf, axis=-1, keepdims=True) + EPS) * g


def _dot(a, b):
    return jnp.dot(a, b, preferred_element_type=f32)


def _dot_nt(a, b):
    return lax.dot_general(a, b, (((1,), (1,)), ((), ())), preferred_element_type=f32)


ROW_TILE = D_MODEL // LANE


def _store_row_tiles(ref, val):
    n = val.shape[0]
    for c in range(ROW_TILE):
        ref[pl.ds(c, n, stride=ROW_TILE), :] = val[:, c * LANE:(c + 1) * LANE]


def _load_row_tiles(ref):
    n = ref.shape[0] // ROW_TILE
    return jnp.concatenate([ref[pl.ds(c, n, stride=ROW_TILE), :] for c in range(ROW_TILE)], axis=1)


def _softmax_rows(s_ref, p_ref, rows, sinks=None):
    s = s_ref[rows, :]
    m = jnp.max(s, axis=-1, keepdims=True)
    if sinks is not None:
        sink = jnp.concatenate([jnp.full((n, 1), v, f32) for v, n in sinks], axis=0)
        m = jnp.maximum(m, sink)
    p = jnp.exp(s - m)
    l = jnp.sum(p, axis=-1, keepdims=True)
    if sinks is not None:
        l = l + jnp.exp(sink - m)
    p_ref[rows, :] = (p * (1.0 / l)).astype(bf16)


def _ada_kernel(c_ref, w_ref, b_ref, o_ref):
    c = c_ref[...]
    s = c * jax.nn.sigmoid(c)
    s_hi = s.astype(bf16)
    s_lo = (s - s_hi.astype(f32)).astype(bf16)
    w = w_ref[0]
    w_hi = w.astype(bf16)
    w_lo = (w - w_hi.astype(f32)).astype(bf16)
    o_ref[0] = _dot(s_hi, w_hi) + _dot(s_hi, w_lo) + _dot(s_lo, w_hi) + b_ref[0]


def _ada(cvec, w_ada, b_ada):
    tn = 1536
    return pl.pallas_call(
        _ada_kernel,
        grid=(DEPTH, 6 * D_MODEL // tn),
        in_specs=[pl.BlockSpec((8, D_MODEL), lambda l, j: (0, 0)),
                  pl.BlockSpec((1, D_MODEL, tn), lambda l, j: (l, 0, j)),
                  pl.BlockSpec((1, 1, tn), lambda l, j: (l, 0, j))],
        out_specs=pl.BlockSpec((1, 8, tn), lambda l, j: (l, 0, j)),
        out_shape=jax.ShapeDtypeStruct((DEPTH, 8, 6 * D_MODEL), f32),
        compiler_params=_cparams("arbitrary", "arbitrary"),
        name="ada",
    )(cvec, w_ada, b_ada.reshape(DEPTH, 1, 6 * D_MODEL))


CACHE_WIDTHS = (W_KA, W_VA, W_B, W_B, KV_LORA, QK_ROPE)


def _inproj_ctx_kernel(layer, x_ref, g_ref, sh_ref, sc_ref, w_ref, gcq_ref, gckv_ref, wuq_ref, *refs):
    qa_ref, qb_ref, qc_ref, ka_ref, va_ref, kb_ref, vb_ref, ckv_ref, kr_ref = refs[-9:]
    h = _rms(x_ref[...], g_ref[...]) * (1.0 + sc_ref[0]) + sh_ref[0]
    p = _dot(h.astype(bf16), w_ref[...])
    qa_ref[...] = p[:, OFF_QA:OFF_QA + W_QA].astype(bf16)
    qb_ref[...] = p[:, OFF_QB:OFF_QB + W_B].astype(bf16)
    cqn = _rms(p[:, OFF_CQ:OFF_CQ + Q_LORA], gcq_ref[...])
    qc_ref[...] = _dot(cqn.astype(bf16), wuq_ref[...]).astype(bf16)
    caches = ((ka_ref, p[:, OFF_KA:OFF_KA + W_KA]), (va_ref, p[:, OFF_VA:OFF_VA + W_VA]),
              (kb_ref, p[:, OFF_KB:OFF_KB + W_B]), (vb_ref, p[:, OFF_VB:OFF_VB + W_B]),
              (ckv_ref, _rms(p[:, OFF_CKV:OFF_CKV + KV_LORA], gckv_ref[...])),
              (kr_ref, p[:, OFF_KR:OFF_KR + QK_ROPE]))
    for ref, val in caches:
        for b in range(TM_TOK // SEQ):
            rows = val[b * SEQ:(b + 1) * SEQ]
            if layer == 0:
                ref[b, 0] = rows
                for later in range(1, DEPTH):
                    ref[b, later] = jnp.zeros_like(rows)
            else:
                ref[b, 0] = rows


def _inproj_ctx(layer, prev_caches, x, g, shift, scale, w, gcq, gckv, wuq):
    tm = TM_TOK
    nb = tm // SEQ
    row = lambda i: (i, 0)
    const = lambda i: (0, 0)
    in_specs = [pl.BlockSpec((tm, D_MODEL), row),
                pl.BlockSpec((1, D_MODEL), const),
                pl.BlockSpec((1, 1, D_MODEL), lambda i: (0, 0, 0)),
                pl.BlockSpec((1, 1, D_MODEL), lambda i: (0, 0, 0)),
                pl.BlockSpec((None, D_MODEL, NW_CTX), lambda i: (layer, 0, 0)),
                pl.BlockSpec((1, Q_LORA), const),
                pl.BlockSpec((1, KV_LORA), const),
                pl.BlockSpec((None, Q_LORA, H_C * QC_PAD), lambda i: (layer, 0, 0))]
    q_widths = (W_QA, W_B, H_C * QC_PAD)
    out_specs = [pl.BlockSpec((tm, wd), row) for wd in q_widths]
    out_shape = [jax.ShapeDtypeStruct((T_CTX, wd), bf16) for wd in q_widths]
    if layer == 0:
        out_specs += [pl.BlockSpec((nb, DEPTH, SEQ, wd), lambda i: (i, 0, 0, 0)) for wd in CACHE_WIDTHS]
        aliases, extra = {}, ()
    else:
        in_specs += [pl.BlockSpec(memory_space=pl.ANY) for _ in CACHE_WIDTHS]
        out_specs += [pl.BlockSpec((nb, 1, SEQ, wd), lambda i: (i, layer, 0, 0)) for wd in CACHE_WIDTHS]
        aliases = {8 + j: len(q_widths) + j for j in range(len(CACHE_WIDTHS))}
        extra = tuple(prev_caches)
    out_shape += [jax.ShapeDtypeStruct((BATCH, DEPTH, SEQ, wd), f32) for wd in CACHE_WIDTHS]
    outs = pl.pallas_call(
        functools.partial(_inproj_ctx_kernel, layer),
        grid=(T_CTX // tm,),
        in_specs=in_specs,
        out_specs=out_specs,
        out_shape=out_shape,
        input_output_aliases=aliases,
        compiler_params=_cparams("arbitrary"),
        name="inproj_ctx",
    )(x, g, shift, scale, w, gcq, gckv, wuq, *extra)
    return outs[:3], outs[3:]


def _inproj_lat_kernel(x_ref, g_ref, sh_ref, sc_ref, w_ref, gcq_ref, gckv_ref, wuq_ref,
                       cosa_ref, sina_ref, cosq_ref, sinq_ref, cosr_ref, sinr_ref,
                       qa_ref, ka_ref, va_ref, qb_ref, kb_ref, vb_ref, qc_ref, ckv_ref, kr_ref):
    h = _rms(x_ref[...], g_ref[...]) * (1.0 + sc_ref[0]) + sh_ref[0]
    p = _dot(h.astype(bf16), w_ref[...])
    cosa = cosa_ref[...]
    sina = sina_ref[...]
    qa = p[:, OFF_QA:OFF_QA + W_QA] * cosa + p[:, OFF_QA_P:OFF_QA_P + W_QA] * sina
    ka = p[:, OFF_KA:OFF_KA + W_KA] * cosa[:, :W_KA] + p[:, OFF_KA_P:OFF_KA_P + W_KA] * sina[:, :W_KA]
    kr = p[:, OFF_KR:OFF_KR + QK_ROPE] * cosr_ref[...] + p[:, OFF_KR_P:OFF_KR_P + QK_ROPE] * sinr_ref[...]
    qa_ref[...] = qa.astype(bf16)
    ka_ref[...] = ka.astype(bf16)
    va_ref[...] = p[:, OFF_VA:OFF_VA + W_VA].astype(bf16)
    qb_ref[...] = p[:, OFF_QB:OFF_QB + W_B].astype(bf16)
    kb_ref[...] = p[:, OFF_KB:OFF_KB + W_B].astype(bf16)
    vb_ref[...] = p[:, OFF_VB:OFF_VB + W_B].astype(bf16)
    cqn = _rms(p[:, OFF_CQ:OFF_CQ + Q_LORA], gcq_ref[...])
    q2 = _dot(cqn.astype(bf16), wuq_ref[...])
    nq = H_C * QC_PAD
    qc_ref[...] = (q2[:, :nq] * cosq_ref[...] + q2[:, nq:] * sinq_ref[...]).astype(bf16)
    ckv_ref[...] = _rms(p[:, OFF_CKV:OFF_CKV + KV_LORA], gckv_ref[...]).astype(bf16)
    kr_ref[...] = kr.astype(bf16)


def _inproj_lat(layer, x, lat_row0, g, shift, scale, w, gcq, gckv, wuq2, tabs):
    tm = TM_LAT_IN
    per_b = DEC_SEQ // tm
    row0 = lat_row0 // tm
    xrow = lambda i: (row0 + i, 0)
    row = lambda i: (i, 0)
    const = lambda i: (0, 0)
    grp = lambda i: (1 + i // per_b, 0, 0)
    pos = lambda i: (i % per_b, 0)
    cosa, sina, cosq, sinq, cosr, sinr = tabs
    widths = (W_QA, W_KA, W_VA, W_B, W_B, W_B, H_C * QC_PAD, KV_LORA, QK_ROPE)
    return pl.pallas_call(
        _inproj_lat_kernel,
        grid=(T_LAT // tm,),
        in_specs=[pl.BlockSpec((tm, D_MODEL), xrow),
                  pl.BlockSpec((1, D_MODEL), const),
                  pl.BlockSpec((1, 1, D_MODEL), grp),
                  pl.BlockSpec((1, 1, D_MODEL), grp),
                  pl.BlockSpec((None, D_MODEL, NW_LAT), lambda i: (layer, 0, 0)),
                  pl.BlockSpec((1, Q_LORA), const),
                  pl.BlockSpec((1, KV_LORA), const),
                  pl.BlockSpec((None, Q_LORA, 2 * H_C * QC_PAD), lambda i: (layer, 0, 0)),
                  pl.BlockSpec((tm, W_QA), pos), pl.BlockSpec((tm, W_QA), pos),
                  pl.BlockSpec((tm, H_C * QC_PAD), pos), pl.BlockSpec((tm, H_C * QC_PAD), pos),
                  pl.BlockSpec((tm, QK_ROPE), pos), pl.BlockSpec((tm, QK_ROPE), pos)],
        out_specs=[pl.BlockSpec((tm, wd), row) for wd in widths],
        out_shape=[jax.ShapeDtypeStruct((T_LAT, wd), bf16) for wd in widths],
        compiler_params=_cparams("arbitrary"),
        name="inproj_lat",
    )(x, g, shift, scale, w, gcq, gckv, wuq2, cosa, sina, cosq, sinq, cosr, sinr)


CTX_BATCHES = 2


def _ctx_attn_kernel(sink_ref, qa_ref, ka_ref, va_ref, qb_ref, kb_ref, vb_ref, qc_ref, ckv_ref, kr_ref,
                     wukv_ref, wout_ref, x_ref, gate_ref, o_ref, o_scr, s_scr, p_scr):
    n = SEQ
    scale = HEAD_DIM ** -0.5
    scale_c = (QK_NOPE + QK_ROPE) ** -0.5

    def one_batch(sb, carry):
        rows = pl.ds(pl.multiple_of(sb * n, n), n)
        ka = ka_ref[sb, 0].astype(bf16)
        va = va_ref[sb, 0].astype(bf16)
        kb = kb_ref[sb, 0].astype(bf16)
        vb = vb_ref[sb, 0].astype(bf16)
        kv = _dot(ckv_ref[sb, 0].astype(bf16), wukv_ref[...]).astype(bf16)
        kr = kr_ref[sb, 0].astype(bf16)
        for h in range(H_A):
            g = h // G_A
            q = qa_ref[rows, h * HEAD_DIM:(h + 1) * HEAD_DIM]
            s_scr[h * n:(h + 1) * n, :] = _dot_nt(q, ka[:, g * HEAD_DIM:(g + 1) * HEAD_DIM]) * scale
        for h in range(H_B):
            sl = slice(h * HEAD_DIM, (h + 1) * HEAD_DIM)
            s_scr[(H_A + h) * n:(H_A + h + 1) * n, :] = _dot_nt(qb_ref[rows, sl], kb[:, sl]) * scale
        for h in range(H_C):
            qn = qc_ref[rows, h * QC_PAD:h * QC_PAD + QK_NOPE]
            qr = qc_ref[rows, h * QC_PAD + QK_NOPE:h * QC_PAD + QK_NOPE + QK_ROPE]
            c0 = h * (QK_NOPE + V_C)
            r0 = (H_A + H_B + h) * n
            s_scr[r0:r0 + n, :] = (_dot_nt(qn, kv[:, c0:c0 + QK_NOPE]) + _dot_nt(qr, kr)) * scale_c
        for pair in range((H_A + H_B + H_C) // 2):
            h0 = 2 * pair
            sinks = ((sink_ref[h0], n), (sink_ref[h0 + 1], n)) if h0 < H_A else None
            _softmax_rows(s_scr, p_scr, slice(h0 * n, (h0 + 2) * n), sinks)
        for h in range(H_A):
            g = h // G_A
            o_scr[:, h * HEAD_DIM:(h + 1) * HEAD_DIM] = _dot(p_scr[h * n:(h + 1) * n, :],
                                                             va[:, g * HEAD_DIM:(g + 1) * HEAD_DIM])
        for h in range(H_B):
            sl = slice(h * HEAD_DIM, (h + 1) * HEAD_DIM)
            o_scr[:, W_QA + h * HEAD_DIM:W_QA + (h + 1) * HEAD_DIM] = _dot(
                p_scr[(H_A + h) * n:(H_A + h + 1) * n, :], vb[:, sl])
        for h in range(H_C):
            c0 = h * (QK_NOPE + V_C)
            r0 = (H_A + H_B + h) * n
            off = W_QA + W_B + h * V_C
            o_scr[:, off:off + V_C] = _dot(p_scr[r0:r0 + n, :], kv[:, c0 + QK_NOPE:c0 + QK_NOPE + V_C])
        y = _dot(o_scr[...].astype(bf16), wout_ref[...])
        o_ref[rows, :] = x_ref[rows, :] + gate_ref[0] * y
        return carry

    lax.fori_loop(0, CTX_BATCHES, one_batch, 0)


def _ctx_attn(layer, sink, qs, caches, wukv, wout, x, gate):
    qa, qb, qc = qs
    ka, va, kb, vb, ckv, kr = caches
    row = lambda b: (b, 0)
    const = lambda b: (0, 0)
    slot = lambda b: (b, layer, 0, 0)
    nrow = CTX_BATCHES * SEQ
    qspec = lambda a: pl.BlockSpec((nrow, a.shape[1]), row)
    cspec = lambda a: pl.BlockSpec((CTX_BATCHES, 1, SEQ, a.shape[3]), slot)
    in_specs = [pl.BlockSpec(memory_space=pltpu.SMEM),
                qspec(qa), cspec(ka), cspec(va), qspec(qb), cspec(kb), cspec(vb), qspec(qc), cspec(ckv), cspec(kr)]
    in_specs += [pl.BlockSpec((None, KV_LORA, H_C * (QK_NOPE + V_C)), lambda b: (layer, 0, 0)),
                 pl.BlockSpec((None, D_MODEL, D_MODEL), lambda b: (layer, 0, 0)),
                 pl.BlockSpec((nrow, D_MODEL), row),
                 pl.BlockSpec((1, 1, D_MODEL), lambda b: (0, 0, 0))]
    return pl.pallas_call(
        _ctx_attn_kernel,
        grid=(BATCH // CTX_BATCHES,),
        in_specs=in_specs,
        out_specs=pl.BlockSpec((nrow, D_MODEL), row),
        out_shape=jax.ShapeDtypeStruct((T_CTX, D_MODEL), f32),
        scratch_shapes=[pltpu.VMEM((SEQ, D_MODEL), f32),
                        pltpu.VMEM(((H_A + H_B + H_C) * SEQ, SEQ), f32),
                        pltpu.VMEM(((H_A + H_B + H_C) * SEQ, SEQ), bf16)],
        compiler_params=_cparams("arbitrary"),
        name="ctx_attn",
    )(sink, qa, ka, va, qb, kb, vb, qc, ckv, kr, wukv, wout, x, gate)


def _lat_attn_kernel(sink_ref, qa_ref, qb_ref, qc_ref, ka_ref, va_ref, kb_ref, vb_ref, ckv_ref, kr_ref,
                     cak_ref, cav_ref, cbk_ref, cbv_ref, cckv_ref, ckr_ref, bias_ref,
                     wukv_ref, wout_ref, x_ref, gate_ref, o_ref, o_scr, kv_scr, sa, pa, sb, pb, sc, pc):
    qi = pl.program_id(1)
    nb = DEC_SEQ // BLOCK
    scale = HEAD_DIM ** -0.5

    @pl.when(qi == 0)
    def _():
        kv_scr[0:DEC_SEQ, :] = _dot(ckv_ref[...], wukv_ref[...]).astype(bf16)
        kv_scr[DEC_SEQ:DEC_SEQ + PAST_LEN, :] = _dot(cckv_ref[0, 0].astype(bf16), wukv_ref[...]).astype(bf16)

    def blk(ref, j):
        idx = jnp.clip(qi + j, 0, nb - 1)
        return ref[pl.ds(pl.multiple_of(idx * BLOCK, BLOCK), BLOCK), :]

    ka = jnp.concatenate([blk(ka_ref, -1), blk(ka_ref, 0), blk(ka_ref, 1), cak_ref[0, 0].astype(bf16)], axis=0)
    va = jnp.concatenate([blk(va_ref, -1), blk(va_ref, 0), blk(va_ref, 1), cav_ref[0, 0].astype(bf16)], axis=0)
    nk_a = 3 * BLOCK + PAST_LEN
    r = lax.broadcasted_iota(jnp.int32, (BLOCK, nk_a), 0)
    c = lax.broadcasted_iota(jnp.int32, (BLOCK, nk_a), 1)
    valid = (((c < BLOCK) & (c >= r) & (qi > 0))
             | ((c >= BLOCK) & (c < 2 * BLOCK))
             | ((c >= 2 * BLOCK) & (c < 3 * BLOCK) & (c - 2 * BLOCK <= r) & (qi < nb - 1))
             | (c >= 3 * BLOCK))
    for h in range(H_A):
        g = h // G_A
        q = qa_ref[:, h * HEAD_DIM:(h + 1) * HEAD_DIM]
        s = _dot_nt(q, ka[:, g * HEAD_DIM:(g + 1) * HEAD_DIM]) * scale
        sa[h * BLOCK:(h + 1) * BLOCK, :] = jnp.where(valid, s, NEG)

    cbk = cbk_ref[0, 0].astype(bf16)
    cbv = cbv_ref[0, 0].astype(bf16)
    rows_per_blk = BLOCK // GRID_W
    nloc = NA_ROWS * GRID_W
    vcats = []
    for half in range(rows_per_blk):
        grow = qi * rows_per_blk + half
        start = jnp.clip(grow - NA_ROWS // 2, 0, ROWS - NA_ROWS)
        kloc = kb_ref[pl.ds(pl.multiple_of(start * GRID_W, GRID_W), nloc), :]
        vloc = vb_ref[pl.ds(pl.multiple_of(start * GRID_W, GRID_W), nloc), :]
        vcats.append(jnp.concatenate([vloc, cbv], axis=0))
        qrows = slice(half * GRID_W, (half + 1) * GRID_W)
        dr0 = start - grow + (NA_ROWS - 1)
        for h in range(H_B):
            sl = slice(h * HEAD_DIM, (h + 1) * HEAD_DIM)
            q = qb_ref[qrows, sl]
            bias = jnp.concatenate([bias_ref[h, dr0 + 2 * j] for j in range(NA_ROWS // 2)], axis=1)
            s_loc = _dot_nt(q, kloc[:, sl]) * scale + bias
            s_ctx = _dot_nt(q, cbk[:, sl]) * scale
            r0 = (half * H_B + h) * GRID_W
            sb[r0:r0 + GRID_W, :] = jnp.concatenate([s_loc, s_ctx], axis=1)

    kr = jnp.concatenate([kr_ref[...], ckr_ref[0, 0].astype(bf16)], axis=0)
    scale_c = (QK_NOPE + QK_ROPE) ** -0.5
    for h in range(H_C):
        qn = qc_ref[:, h * QC_PAD:h * QC_PAD + QK_NOPE]
        qr = qc_ref[:, h * QC_PAD + QK_NOPE:h * QC_PAD + QK_NOPE + QK_ROPE]
        c0 = h * (QK_NOPE + V_C)
        sc[h * BLOCK:(h + 1) * BLOCK, :] = (_dot_nt(qn, kv_scr[:, c0:c0 + QK_NOPE]) + _dot_nt(qr, kr)) * scale_c

    for pair in range(H_A // 2):
        h0 = 2 * pair
        _softmax_rows(sa, pa, slice(h0 * BLOCK, (h0 + 2) * BLOCK), ((sink_ref[h0], BLOCK), (sink_ref[h0 + 1], BLOCK)))
    for blk2 in range(rows_per_blk * H_B // 2):
        _softmax_rows(sb, pb, slice(blk2 * 2 * GRID_W, (blk2 + 1) * 2 * GRID_W))
    for h in range(H_C):
        _softmax_rows(sc, pc, slice(h * BLOCK, (h + 1) * BLOCK))

    for h in range(H_A):
        g = h // G_A
        o_scr[:, h * HEAD_DIM:(h + 1) * HEAD_DIM] = _dot(pa[h * BLOCK:(h + 1) * BLOCK, :],
                                                         va[:, g * HEAD_DIM:(g + 1) * HEAD_DIM])
    for half in range(rows_per_blk):
        qrows = slice(half * GRID_W, (half + 1) * GRID_W)
        for h in range(H_B):
            sl = slice(h * HEAD_DIM, (h + 1) * HEAD_DIM)
            r0 = (half * H_B + h) * GRID_W
            o_scr[qrows, W_QA + h * HEAD_DIM:W_QA + (h + 1) * HEAD_DIM] = _dot(pb[r0:r0 + GRID_W, :],
                                                                             vcats[half][:, sl])
    for h in range(H_C):
        c0 = h * (QK_NOPE + V_C)
        off = W_QA + W_B + h * V_C
        o_scr[:, off:off + V_C] = _dot(pc[h * BLOCK:(h + 1) * BLOCK, :], kv_scr[:, c0 + QK_NOPE:c0 + QK_NOPE + V_C])

    y = _dot(o_scr[...].astype(bf16), wout_ref[...])
    o_ref[...] = x_ref[...] + gate_ref[0] * y


def _lat_attn(layer, sink, proj, caches, bias_tab, wukv, wout, x, lat_row0, gate):
    qa, ka, va, qb, kb, vb, qc, ckv, kr = proj
    nb = DEC_SEQ // BLOCK
    qrow = lambda b, q: (b * nb + q, 0)
    xrow = lambda b, q: (lat_row0 // BLOCK + b * nb + q, 0)
    brow = lambda b, q: (b, 0)
    const = lambda b, q: (0, 0)
    cidx = lambda b, q: (b, layer, 0, 0)
    in_specs = [pl.BlockSpec(memory_space=pltpu.SMEM)]
    in_specs += [pl.BlockSpec((BLOCK, a.shape[1]), qrow) for a in (qa, qb, qc)]
    in_specs += [pl.BlockSpec((DEC_SEQ, a.shape[1]), brow) for a in (ka, va, kb, vb, ckv, kr)]
    in_specs += [pl.BlockSpec((1, 1, PAST_LEN, a.shape[3]), cidx) for a in caches]
    in_specs += [pl.BlockSpec((None,) + bias_tab.shape[1:], lambda b, q: (layer, 0, 0, 0, 0)),
                 pl.BlockSpec((None, KV_LORA, H_C * (QK_NOPE + V_C)), lambda b, q: (layer, 0, 0)),
                 pl.BlockSpec((None, D_MODEL, D_MODEL), lambda b, q: (layer, 0, 0)),
                 pl.BlockSpec((BLOCK, D_MODEL), xrow),
                 pl.BlockSpec((1, 1, D_MODEL), lambda b, q: (1 + b, 0, 0))]
    return pl.pallas_call(
        _lat_attn_kernel,
        grid=(DEC_BATCH, nb),
        in_specs=in_specs,
        out_specs=pl.BlockSpec((BLOCK, D_MODEL), qrow),
        out_shape=jax.ShapeDtypeStruct((T_LAT, D_MODEL), f32),
        scratch_shapes=[pltpu.VMEM((BLOCK, D_MODEL), f32),
                        pltpu.VMEM((DEC_SEQ + PAST_LEN, H_C * (QK_NOPE + V_C)), bf16)]
        + [pltpu.VMEM(shape, dt) for shape in ((H_A * BLOCK, 3 * BLOCK + PAST_LEN),
                                               (H_B * BLOCK, NA_ROWS * GRID_W + PAST_LEN),
                                               (H_C * BLOCK, DEC_SEQ + PAST_LEN)) for dt in (f32, bf16)],
        compiler_params=_cparams("arbitrary", "arbitrary"),
        name="lat_attn",
    )(sink, qa, qb, qc, ka, va, kb, vb, ckv, kr, *caches, bias_tab, wukv, wout, x, gate)


def _pick_stream(xc_ref, xl_ref, x_scr):
    i = pl.program_id(0)

    @pl.when(i < T_CTX // TM_TOK)
    def _():
        x_scr[...] = xc_ref[...]

    @pl.when(i >= T_CTX // TM_TOK)
    def _():
        x_scr[...] = xl_ref[...]

    return x_scr[...]


def _stream_specs(lat_row0):
    n_ctx = T_CTX // TM_TOK
    return [pl.BlockSpec((TM_TOK, D_MODEL), lambda i: (jnp.minimum(i, n_ctx - 1), 0)),
            pl.BlockSpec((TM_TOK, D_MODEL), lambda i: (lat_row0 // TM_TOK + jnp.maximum(i - n_ctx, 0), 0))]


def _router_kernel(xc_ref, xl_ref, g_ref, sh_ref, sc_ref, wr_ref, br_ref, h_ref, e_ref, gt_ref, x_scr):
    h = _rms(_pick_stream(xc_ref, xl_ref, x_scr), g_ref[...]) * (1.0 + sc_ref[0]) + sh_ref[0]
    _store_row_tiles(h_ref, h)
    h_hi = h.astype(bf16)
    h_lo = (h - h_hi.astype(f32)).astype(bf16)
    w = wr_ref[...]
    w_hi = w.astype(bf16)
    w_lo = (w - w_hi.astype(f32)).astype(bf16)
    logits = _dot(h_hi, w_hi) + _dot(h_hi, w_lo) + _dot(h_lo, w_hi) + br_ref[...]
    lane = lax.broadcasted_iota(jnp.int32, logits.shape, 1).astype(f32)
    l = jnp.where(lane < N_EXPERTS, logits, -jnp.inf)
    tops, idxs = [], []
    for _ in range(TOP_K):
        m = jnp.max(l, axis=-1, keepdims=True)
        idx = jnp.min(jnp.where(l == m, lane, float(LANE)), axis=-1, keepdims=True)
        tops.append(m)
        idxs.append(idx)
        l = jnp.where(lane == idx, -jnp.inf, l)
    ex = [jnp.exp(t - tops[0]) for t in tops]
    den = ex[0] + ex[1] + ex[2] + ex[3]
    e_out = jnp.zeros(logits.shape, f32)
    g_out = jnp.zeros(logits.shape, f32)
    for k in range(TOP_K):
        e_out = jnp.where(lane == k, idxs[k], e_out)
        g_out = jnp.where(lane == k, ex[k] / den, g_out)
    e_ref[...] = e_out.astype(jnp.int32)
    gt_ref[...] = g_out


def _group_of_tile(i):
    per_b = DEC_SEQ // TM_TOK
    n_ctx = T_CTX // TM_TOK
    return jnp.where(i < n_ctx, 0, 1 + (i - n_ctx) // per_b)


def _router(layer, xc, xl, lat_row0, g, shift, scale, wr, br):
    tm = TM_TOK
    row = lambda i: (i, 0)
    const = lambda i: (0, 0)
    grp = lambda i: (_group_of_tile(i), 0, 0)
    return pl.pallas_call(
        _router_kernel,
        grid=(T_ALL // tm,),
        in_specs=_stream_specs(lat_row0) +
                 [pl.BlockSpec((1, D_MODEL), const),
                  pl.BlockSpec((1, 1, D_MODEL), grp),
                  pl.BlockSpec((1, 1, D_MODEL), grp),
                  pl.BlockSpec((None, D_MODEL, LANE), lambda i: (layer, 0, 0)),
                  pl.BlockSpec((None, 1, LANE), lambda i: (layer, 0, 0))],
        out_specs=[pl.BlockSpec((tm * ROW_TILE, LANE), row), pl.BlockSpec((tm, LANE), row),
                   pl.BlockSpec((tm, LANE), row)],
        out_shape=[jax.ShapeDtypeStruct((T_ALL * ROW_TILE, LANE), f32),
                   jax.ShapeDtypeStruct((T_ALL, LANE), jnp.int32),
                   jax.ShapeDtypeStruct((T_ALL, LANE), f32)],
        scratch_shapes=[pltpu.VMEM((tm, D_MODEL), f32)],
        compiler_params=_cparams("arbitrary"),
        name="router",
    )(xc, xl, g, shift, scale, wr, br)


def _dispatch_kernel(tok_ref, nu_ref, h_hbm, o_ref, hv, xg, hsem):
    tm = TM_MOE
    i = pl.program_id(0)

    @pl.when(i == 0)
    def _():
        resident = pltpu.make_async_copy(h_hbm, hv, hsem.at[0])
        resident.start()
        resident.wait()

    def one_block(sub, carry):
        blk = i * DISPATCH_BLOCKS + sub
        rows = pl.ds(pl.multiple_of(sub * tm, tm), tm)

        @pl.when(blk < nu_ref[0])
        def _():
            for r in range(tm):
                t = tok_ref[blk * tm + r]
                xg[pl.ds(r, ROW_TILE, stride=tm + 1), :] = hv[pl.ds(pl.multiple_of(t * ROW_TILE, ROW_TILE),
                                                                 ROW_TILE), :]
            o_ref[rows, :] = jnp.concatenate([xg[pl.ds(c * (tm + 1), tm), :] for c in range(ROW_TILE)],
                                             axis=1).astype(bf16)

        @pl.when(blk >= nu_ref[0])
        def _():
            o_ref[rows, :] = jnp.zeros((tm, D_MODEL), bf16)

        return carry

    lax.fori_loop(0, DISPATCH_BLOCKS, one_block, 0)


def _dispatch(row_tok, n_used, h):
    tm = TM_MOE
    return pl.pallas_call(
        _dispatch_kernel,
        grid_spec=pltpu.PrefetchScalarGridSpec(
            num_scalar_prefetch=2,
            grid=(N_MOE_BLOCKS // DISPATCH_BLOCKS,),
            in_specs=[pl.BlockSpec(memory_space=pl.ANY)],
            out_specs=pl.BlockSpec((DISPATCH_BLOCKS * tm, D_MODEL), lambda i, tok, nu: (i, 0)),
            scratch_shapes=[pltpu.VMEM((T_ALL * ROW_TILE, LANE), f32), pltpu.VMEM(((tm + 1) * ROW_TILE, LANE), f32),
                            pltpu.SemaphoreType.DMA((1,))]),
        out_shape=jax.ShapeDtypeStruct((N_MOE_BLOCKS * tm, D_MODEL), bf16),
        compiler_params=_cparams("arbitrary"),
        name="dispatch",
    )(row_tok, n_used, h)


def _moe_kernel(layer, be_ref, nu_ref, nxt_ref, dst_ref, x_ref, wgu_hbm, bgu_ref, wd_hbm, bd_ref, y_hbm,
                y0, y1, wgu_st, wd_st, wgu_bf, wd_bf, wsem, ssem):
    tm = TM_MOE
    i = pl.program_id(0)
    nb = pl.num_programs(0)
    used = i < nu_ref[0]
    yb = (y0, y1)

    def out_tile(row):
        return pl.ds(pl.multiple_of(row * ROW_TILE, ROW_TILE), ROW_TILE)

    def scatter_desc(buf, r, dst_row, s):
        return pltpu.make_async_copy(buf.at[out_tile(r)], y_hbm.at[out_tile(dst_row)], ssem.at[s])

    def scatter_wait(s):
        pltpu.make_async_copy(yb[s], y_hbm.at[pl.ds(0, tm * ROW_TILE)], ssem.at[s]).wait()

    def scatter_start(blk, s, unrolled):
        if unrolled:
            for r in range(tm):
                scatter_desc(yb[s], r, dst_ref[(blk + 1) * tm + r], s).start(priority=r % 2)
        else:
            def body(r, carry):
                scatter_desc(yb[s], r, dst_ref[(blk + 1) * tm + r], s).start()
                return carry
            lax.fori_loop(0, tm, body, 0, unroll=8)

    def weight_copies(e):
        return (pltpu.make_async_copy(wgu_hbm.at[layer, e], wgu_st, wsem.at[0]),
                pltpu.make_async_copy(wd_hbm.at[layer, e], wd_st, wsem.at[1]))

    @pl.when(i == 0)
    def _():
        for s in range(2):
            yb[s][...] = jnp.zeros_like(yb[s])
            dummy = pltpu.make_async_copy(yb[s], y_hbm.at[pl.ds((N_ASSIGN + s * tm) * ROW_TILE, tm * ROW_TILE)],
                                          ssem.at[s])
            dummy.start()
            dummy.wait()
        for cp in weight_copies(be_ref[0]):
            cp.start()

    first = jnp.logical_and(used, jnp.logical_or(i == 0, be_ref[i] != be_ref[jnp.maximum(i - 1, 0)]))

    @pl.when(first)
    def _():
        for cp in weight_copies(0):
            cp.wait()
        wgu_bf[...] = wgu_st[...].astype(bf16)
        wd_bf[...] = wd_st[...].astype(bf16)

        @pl.when(nxt_ref[i] >= 0)
        def _():
            for cp in weight_copies(nxt_ref[i]):
                cp.start()

    def step(par):
        cur, oth = par, 1 - par

        @pl.when(jnp.logical_and(i >= 1, i - 2 < nu_ref[0]))
        def _():
            scatter_wait(cur)

        @pl.when(used)
        def _():
            scatter_start(i - 1, oth, unrolled=True)
            gu = _dot(x_ref[...], wgu_bf[...]) + bgu_ref[0, 0]
            x_glu = jnp.minimum(gu[:, :D_FF], SWIGLU_LIMIT)
            x_lin = jnp.clip(gu[:, D_FF:], -SWIGLU_LIMIT, SWIGLU_LIMIT)
            act = x_glu * jax.nn.sigmoid(SWIGLU_ALPHA * x_glu) * (x_lin + 1.0)
            _store_row_tiles(yb[cur], _dot(act.astype(bf16), wd_bf[...]) + bd_ref[0, 0])

        flush = jnp.logical_and(jnp.logical_not(used), i - 1 < nu_ref[0])

        @pl.when(flush)
        def _():
            scatter_start(i - 1, oth, unrolled=False)

        @pl.when(jnp.logical_and(flush, i == nb - 1))
        def _():
            scatter_wait(oth)

    @pl.when(i % 2 == 0)
    def _():
        step(0)

    @pl.when(i % 2 == 1)
    def _():
        step(1)


def _moe(layer, routing, h, w_gu, b_gu, w_down, b_down):
    tm = TM_MOE
    block_e, n_used, nxt_e, row_tok, row_dst = routing
    xs = _dispatch(row_tok, n_used, h)
    ex4 = lambda i, be, nu, nxt, dst: (layer, be[i], 0, 0)
    return pl.pallas_call(
        functools.partial(_moe_kernel, layer),
        grid_spec=pltpu.PrefetchScalarGridSpec(
            num_scalar_prefetch=4,
            grid=(N_MOE_BLOCKS,),
            in_specs=[pl.BlockSpec((tm, D_MODEL), lambda i, be, nu, nxt, dst: (i, 0)),
                      pl.BlockSpec(memory_space=pl.ANY),
                      pl.BlockSpec((1, 1, 1, 2 * D_FF), ex4),
                      pl.BlockSpec(memory_space=pl.ANY),
                      pl.BlockSpec((1, 1, 1, D_MODEL), ex4)],
            out_specs=pl.BlockSpec(memory_space=pl.ANY),
            scratch_shapes=[pltpu.VMEM((tm * ROW_TILE, LANE), f32), pltpu.VMEM((tm * ROW_TILE, LANE), f32),
                            pltpu.VMEM((D_MODEL, 2 * D_FF), f32), pltpu.VMEM((D_FF, D_MODEL), f32),
                            pltpu.VMEM((D_MODEL, 2 * D_FF), bf16), pltpu.VMEM((D_FF, D_MODEL), bf16),
                            pltpu.SemaphoreType.DMA((2,)), pltpu.SemaphoreType.DMA((2,))]),
        out_shape=jax.ShapeDtypeStruct(((N_ASSIGN + 2 * tm) * ROW_TILE, LANE), f32),
        compiler_params=_cparams("arbitrary"),
        name="moe",
    )(block_e, n_used, nxt_e, row_dst, xs, w_gu, b_gu.reshape(DEPTH, N_EXPERTS, 1, 2 * D_FF),
      w_down, b_down.reshape(DEPTH, N_EXPERTS, 1, D_MODEL))


def _combine_kernel(final, xc_ref, xl_ref, y0_ref, y1_ref, y2_ref, y3_ref, gt_ref, gate_ref, gf_ref, *rest):
    x_scr = rest[-1]
    gt = gt_ref[...]
    f = gt[:, 0:1] * _load_row_tiles(y0_ref)
    for k, y_ref in ((1, y1_ref), (2, y2_ref), (3, y3_ref)):
        f = f + gt[:, k:k + 1] * _load_row_tiles(y_ref)
    out = _pick_stream(xc_ref, xl_ref, x_scr) + gate_ref[0] * f
    if not final:
        rest[0][...] = out
        return
    out = _rms(out, gf_ref[...])
    oc_ref, ol_ref = rest[0], rest[1]
    i = pl.program_id(0)

    @pl.when(i < T_CTX // TM_TOK)
    def _():
        oc_ref[...] = out

    @pl.when(i >= T_CTX // TM_TOK)
    def _():
        ol_ref[...] = out


def _combine(final, xc, xl, lat_row0, y, gates, gate, g_final):
    tm = TM_TOK
    nt = T_ALL // tm
    n_ctx = T_CTX // tm
    row = lambda i: (i, 0)
    const = lambda i: (0, 0)
    grp = lambda i: (_group_of_tile(i), 0, 0)
    ysel = [pl.BlockSpec((tm * ROW_TILE, LANE), functools.partial(lambda k, i: (k * nt + i, 0), k))
            for k in range(TOP_K)]
    if final:
        out_specs = [pl.BlockSpec((tm, D_MODEL), lambda i: (jnp.minimum(i, n_ctx - 1), 0)),
                     pl.BlockSpec((tm, D_MODEL), lambda i: (jnp.maximum(i - n_ctx, 0), 0))]
        out_shape = [jax.ShapeDtypeStruct((T_CTX, D_MODEL), f32), jax.ShapeDtypeStruct((T_LAT, D_MODEL), f32)]
    else:
        out_specs = pl.BlockSpec((tm, D_MODEL), row)
        out_shape = jax.ShapeDtypeStruct((T_ALL, D_MODEL), f32)
    return pl.pallas_call(
        functools.partial(_combine_kernel, final),
        grid=(nt,),
        in_specs=_stream_specs(lat_row0) + ysel +
                 [pl.BlockSpec((tm, LANE), row),
                  pl.BlockSpec((1, 1, D_MODEL), grp),
                  pl.BlockSpec((1, D_MODEL), const)],
        out_specs=out_specs,
        out_shape=out_shape,
        scratch_shapes=[pltpu.VMEM((tm, D_MODEL), f32)],
        compiler_params=_cparams("arbitrary"),
        name="combine",
    )(xc, xl, y, y, y, y, gates, gate, g_final)


def _rope_head_tables(d):
    nf = d // 4
    half = d // 2
    t = np.arange(DEC_SEQ)
    inv = ROPE_BASE ** (-np.arange(nf, dtype=np.float32) / nf)
    i = np.arange(d)
    pos = np.where(i[None, :] < half, (t // GRID_W)[:, None], (t % GRID_W)[:, None]).astype(np.float32)
    ang = pos * inv[i % nf][None, :].astype(np.float32)
    first = (i % half) < nf
    cos = np.cos(ang)
    sin = np.where(first[None, :], -np.sin(ang), np.sin(ang))
    partner = np.where(first, i + nf, i - nf)
    return cos.astype(np.float32), sin.astype(np.float32), partner


def _rope_tables():
    cos64, sin64, _ = _rope_head_tables(HEAD_DIM)
    cos32, sin32, _ = _rope_head_tables(QK_ROPE)
    cosa = np.tile(cos64, (1, H_A))
    sina = np.tile(sin64, (1, H_A))
    cosq1 = np.concatenate([np.ones((DEC_SEQ, QK_NOPE), np.float32), cos32,
                            np.ones((DEC_SEQ, QC_PAD - QK_NOPE - QK_ROPE), np.float32)], axis=1)
    sinq1 = np.concatenate([np.zeros((DEC_SEQ, QK_NOPE), np.float32), sin32,
                            np.zeros((DEC_SEQ, QC_PAD - QK_NOPE - QK_ROPE), np.float32)], axis=1)
    cosq = np.tile(cosq1, (1, H_C))
    sinq = np.tile(sinq1, (1, H_C))
    return tuple(jnp.asarray(a) for a in (cosa, sina, cosq, sinq, cos32, sin32))


def _pad_cols(w, n):
    return jnp.pad(w, ((0, 0), (0, n - w.shape[1])))


def _layer_weights(w_in, w_uq):
    cuts = np.cumsum((W_QA, W_KA, W_VA, W_B, W_B, W_B, Q_LORA, KV_LORA, QK_ROPE))[:-1]
    qa, ka, va, qb, kb, vb, cq, ckv, kr = jnp.split(w_in, [int(c) for c in cuts], axis=1)
    _, _, p64 = _rope_head_tables(HEAD_DIM)
    _, _, p32 = _rope_head_tables(QK_ROPE)
    pa = np.concatenate([h * HEAD_DIM + p64 for h in range(H_A)])
    base = jnp.concatenate([qa, ka, va, _pad_cols(qb, 384), _pad_cols(kb, 384), _pad_cols(vb, 384), cq, ckv,
                            _pad_cols(kr, 128)], axis=1)
    w_ctx = base.astype(bf16)
    w_lat = jnp.concatenate([base, qa[:, pa], ka[:, pa[:W_KA]], _pad_cols(kr[:, p32], 128)], axis=1).astype(bf16)
    hq = QK_NOPE + QK_ROPE
    heads = [_pad_cols(w_uq[:, h * hq:(h + 1) * hq], QC_PAD) for h in range(H_C)]
    pq = np.concatenate([np.arange(QK_NOPE), QK_NOPE + p32])
    heads_p = [_pad_cols(w_uq[:, h * hq:(h + 1) * hq][:, pq], QC_PAD) for h in range(H_C)]
    wuq = jnp.concatenate(heads, axis=1).astype(bf16)
    wuq2 = jnp.concatenate(heads + heads_p, axis=1).astype(bf16)
    return w_ctx, w_lat, wuq, wuq2


def _bias_table(rpb):
    col = np.arange(GRID_W)
    col_start = np.clip(col - NA_COLS // 2, 0, GRID_W - NA_COLS)
    col_ok = (col[None, :] >= col_start[:, None]) & (col[None, :] < col_start[:, None] + NA_COLS)
    dc = np.clip(col[None, :] - col[:, None] + (NA_COLS - 1), 0, 2 * NA_COLS - 2)
    onehot = (dc[None] == np.arange(2 * NA_COLS - 1)[:, None, None]).astype(np.float32)
    expanded = jnp.einsum('hrd,dqk->hrqk', rpb.astype(f32), jnp.asarray(onehot), precision=lax.Precision.HIGHEST)
    blocks = jnp.where(col_ok[None, None], expanded, NEG)
    return jnp.concatenate([blocks[:, :-1], blocks[:, 1:]], axis=-1)


def _routing(top_e):
    tm = TM_MOE
    key_bits = 16
    pad_mark = (1 << key_bits) - 1
    flat_e = top_e.T.reshape(N_ASSIGN)
    experts = jnp.arange(N_EXPERTS, dtype=jnp.int32)
    counts = jnp.sum((flat_e[:, None] == experts[None, :]).astype(jnp.int32), axis=0)
    nblk = (counts + tm - 1) // tm
    blk_end = jnp.cumsum(nblk)
    pad_end = jnp.cumsum(nblk * tm - counts)
    slots = jnp.arange(N_MOE_BLOCKS * tm - N_ASSIGN, dtype=jnp.int32)
    pad_e = jnp.sum((pad_end[None, :] <= slots[:, None]).astype(jnp.int32), axis=1)
    keys = jnp.concatenate([(flat_e << key_bits) + jnp.arange(N_ASSIGN, dtype=jnp.int32),
                            (pad_e << key_bits) + pad_mark])
    asg = (jnp.sort(keys, stable=False) & pad_mark).reshape(N_MOE_BLOCKS, tm)
    valid = asg != pad_mark
    blocks = jnp.arange(N_MOE_BLOCKS, dtype=jnp.int32)
    r = jnp.arange(tm, dtype=jnp.int32)[None, :]
    tok = jnp.where(valid, asg % T_ALL, 0)
    row_dst = jnp.where(valid, asg, N_ASSIGN + (blocks[:, None] % 2) * tm + r)
    row_dst = jnp.concatenate([N_ASSIGN + tm + r, row_dst], axis=0).reshape(-1)
    block_e = jnp.minimum(jnp.sum((blk_end[None, :] <= blocks[:, None]).astype(jnp.int32), axis=1), N_EXPERTS - 1)
    n_used = blk_end[-1].astype(jnp.int32).reshape(1)
    has = jnp.where(counts > 0, experts, N_EXPERTS)
    later = experts[None, :] > experts[:, None]
    nxt = jnp.min(jnp.where(later, has[None, :], N_EXPERTS), axis=1)
    nxt = jnp.where(nxt >= N_EXPERTS, -1, nxt)
    sel = (block_e[:, None] == experts[None, :]).astype(jnp.int32)
    nxt_e = jnp.sum(sel * nxt[None, :], axis=1)
    i32 = lambda a: a.astype(jnp.int32)
    return i32(block_e), n_used, i32(nxt_e), i32(tok).reshape(-1), i32(row_dst)


def kernel(x_prompt, x_sample, cache_a_k, cache_a_v, cache_b_k, cache_b_v, cache_c_kv, cache_c_kr, c, c_ctx, w_ada, b_ada, g_attn, g_ffn, w_in, sink_a, rpb_b, g_cq, g_ckv, w_uq, w_ukv, w_out, w_router, b_router, w_gu, b_gu, w_down, b_down, g_final):
    xc, xl, lat_row0 = x_prompt.reshape(T_CTX, D_MODEL), x_sample.reshape(T_LAT, D_MODEL), 0
    cvec = jnp.concatenate([c_ctx[None, :], c, jnp.zeros((8 - N_GROUPS, D_MODEL), f32)], axis=0)
    mods = _ada(cvec, w_ada, b_ada)[:, :N_GROUPS].reshape(DEPTH, N_GROUPS, 6, 1, D_MODEL)
    tabs = _rope_tables()
    caches = (cache_a_k.reshape(DEC_BATCH, DEPTH, PAST_LEN, W_KA), cache_a_v.reshape(DEC_BATCH, DEPTH, PAST_LEN, W_VA),
              cache_b_k.reshape(DEC_BATCH, DEPTH, PAST_LEN, W_B), cache_b_v.reshape(DEC_BATCH, DEPTH, PAST_LEN, W_B),
              cache_c_kv, cache_c_kr)
    new = None
    w_ctx, w_lat, wuq, wuq2 = jax.vmap(_layer_weights)(w_in, w_uq)
    wukv = w_ukv.astype(bf16)
    wout = w_out.astype(bf16)
    bias_tab = jax.vmap(_bias_table)(rpb_b)
    wr = jnp.pad(w_router, ((0, 0), (0, 0), (0, LANE - N_EXPERTS)))
    br = jnp.pad(b_router, ((0, 0), (0, LANE - N_EXPERTS)))[:, None, :]
    for layer in range(DEPTH):
        m = [mods[layer, :, j] for j in range(6)]
        g1 = g_attn[layer][None, :]
        gcq = g_cq[layer][None, :]
        gckv = g_ckv[layer][None, :]
        sink = sink_a[layer]

        qs, new = _inproj_ctx(layer, new, xc, g1, m[0], m[1], w_ctx, gcq, gckv, wuq)
        x_ctx = _ctx_attn(layer, sink, qs, new, wukv, wout, xc, m[2])

        plat = _inproj_lat(layer, xl, lat_row0, g1, m[0], m[1], w_lat, gcq, gckv, wuq2, tabs)
        x_lat = _lat_attn(layer, sink, plat, caches, bias_tab, wukv, wout, xl, lat_row0, m[2])

        h2, top_e, gates = _router(layer, x_ctx, x_lat, 0, g_ffn[layer][None, :], m[3], m[4], wr, br)
        y = _moe(layer, _routing(top_e[:, :TOP_K]), h2, w_gu, b_gu, w_down, b_down)
        x = _combine(layer == DEPTH - 1, x_ctx, x_lat, 0, y, gates, m[5], g_final[None, :])
        xc, xl, lat_row0 = x, x, T_CTX

    y_prompt = x[0].reshape(BATCH, SEQ, D_MODEL)
    y_sample = x[1].reshape(DEC_BATCH, DEC_SEQ, D_MODEL)
    shapes = ((KV_A, HEAD_DIM), (KV_A, HEAD_DIM), (H_B, HEAD_DIM), (H_B, HEAD_DIM), (KV_LORA,), (QK_ROPE,))
    outs = [a.reshape((BATCH, DEPTH, SEQ) + s) for a, s in zip(new, shapes)]
    return (y_prompt, y_sample, *outs)
```

```python
import functools

import numpy as np
import jax
import jax.numpy as jnp
from jax import lax
from jax.experimental import pallas as pl
from jax.experimental.pallas import tpu as pltpu

D_MODEL = 1024
BATCH = 32
SEQ = 256
DEPTH = 2
DEC_BATCH = 2
DEC_SEQ = 1024
PAST_LEN = 512
GRID_W = 64
HEAD_DIM = 64
H_A = 6
KV_A = 2
G_A = H_A // KV_A
WINDOW = 128
BLOCK = 128
H_B = 5
NA_ROWS = 8
NA_COLS = 16
H_C = 5
Q_LORA = 384
KV_LORA = 256
QK_NOPE = 64
QK_ROPE = 32
V_C = 64
N_EXPERTS = 32
TOP_K = 4
D_FF = 1024
SWIGLU_ALPHA = 1.702
SWIGLU_LIMIT = 7.0
ROPE_BASE = 10000.0
EPS = 1e-6
NEG = -1e30

T_CTX = BATCH * SEQ
T_LAT = DEC_BATCH * DEC_SEQ
T_ALL = T_CTX + T_LAT
N_GROUPS = 1 + DEC_BATCH
LANE = 128
QC_PAD = 128
ROWS = DEC_SEQ // GRID_W

W_QA, W_KA, W_VA = H_A * HEAD_DIM, KV_A * HEAD_DIM, KV_A * HEAD_DIM
W_B = H_B * HEAD_DIM
OFF_QA = 0
OFF_KA = 384
OFF_VA = 512
OFF_QB = 640
OFF_KB = 1024
OFF_VB = 1408
OFF_CQ = 1792
OFF_CKV = 2176
OFF_KR = 2432
NW_CTX = 2560
OFF_QA_P = 2560
OFF_KA_P = 2944
OFF_KR_P = 3072
NW_LAT = 3200

TM_TOK = 512
TM_LAT_IN = 512
TM_MOE = 256
N_ASSIGN = T_ALL * TOP_K
N_MOE_BLOCKS = N_ASSIGN // TM_MOE + N_EXPERTS
DISPATCH_BLOCKS = 8
VMEM_LIMIT = 56 * 1024 * 1024

f32 = jnp.float32
bf16 = jnp.bfloat16


def _cparams(*sem):
    return pltpu.CompilerParams(dimension_semantics=sem, vmem_limit_bytes=VMEM_LIMIT)


def _mod_spec(mod, group_of):
    _, layer, j = mod
    return pl.BlockSpec((None, 1, None, 1, D_MODEL), lambda *idx: (layer, group_of(*idx), j, 0, 0))


def _rms(xf, g):
    return xf * lax.rsqrt(jnp.mean(xf * xf, axis=-1, keepdims=True) + EPS) * g


def _dot(a, b):
    return jnp.dot(a, b, preferred_element_type=f32)


def _dot_nt(a, b):
    return lax.dot_general(a, b, (((1,), (1,)), ((), ())), preferred_element_type=f32)


ROW_TILE = D_MODEL // LANE


def _store_row_tiles(ref, val):
    n = val.shape[0]
    for c in range(ROW_TILE):
        ref[pl.ds(c, n, stride=ROW_TILE), :] = val[:, c * LANE:(c + 1) * LANE]


def _load_row_tiles(ref):
    n = ref.shape[0] // ROW_TILE
    return jnp.concatenate([ref[pl.ds(c, n, stride=ROW_TILE), :] for c in range(ROW_TILE)], axis=1)


def _softmax_rows(s_ref, p_ref, rows, sinks=None):
    s = s_ref[rows, :]
    m = jnp.max(s, axis=-1, keepdims=True)
    if sinks is not None:
        sink = jnp.concatenate([jnp.full((n, 1), v, f32) for v, n in sinks], axis=0)
        m = jnp.maximum(m, sink)
    p = jnp.exp(s - m)
    l = jnp.sum(p, axis=-1, keepdims=True)
    if sinks is not None:
        l = l + jnp.exp(sink - m)
    p_ref[rows, :] = (p * (1.0 / l)).astype(bf16)


def _ada_kernel(c_ref, w_ref, b_ref, o_ref):
    c = c_ref[...]
    s = c * jax.nn.sigmoid(c)
    s_hi = s.astype(bf16)
    s_lo = (s - s_hi.astype(f32)).astype(bf16)
    w = w_ref[0]
    w_hi = w.astype(bf16)
    w_lo = (w - w_hi.astype(f32)).astype(bf16)
    o_ref[0] = _dot(s_hi, w_hi) + _dot(s_hi, w_lo) + _dot(s_lo, w_hi) + b_ref[0]


def _ada(cvec, w_ada, b_ada):
    tn = 1536
    return pl.pallas_call(
        _ada_kernel,
        grid=(DEPTH, 6 * D_MODEL // tn),
        in_specs=[pl.BlockSpec((8, D_MODEL), lambda l, j: (0, 0)),
                  pl.BlockSpec((1, D_MODEL, tn), lambda l, j: (l, 0, j)),
                  pl.BlockSpec((1, 1, tn), lambda l, j: (l, 0, j))],
        out_specs=pl.BlockSpec((1, 8, tn), lambda l, j: (l, 0, j)),
        out_shape=jax.ShapeDtypeStruct((DEPTH, 8, 6 * D_MODEL), f32),
        compiler_params=_cparams("arbitrary", "arbitrary"),
        name="ada",
    )(cvec, w_ada, b_ada.reshape(DEPTH, 1, 6 * D_MODEL))


CACHE_WIDTHS = (W_KA, W_VA, W_B, W_B, KV_LORA, QK_ROPE)


def _inproj_ctx_kernel(layer, x_ref, g_ref, sh_ref, sc_ref, w_ref, gcq_ref, gckv_ref, wuq_ref, *refs):
    qa_ref, qb_ref, qc_ref, ka_ref, va_ref, kb_ref, vb_ref, ckv_ref, kr_ref = refs[-9:]
    h = _rms(x_ref[...], g_ref[...]) * (1.0 + sc_ref[0]) + sh_ref[0]
    p = _dot(h.astype(bf16), w_ref[...])
    qa_ref[...] = p[:, OFF_QA:OFF_QA + W_QA].astype(bf16)
    qb_ref[...] = p[:, OFF_QB:OFF_QB + W_B].astype(bf16)
    cqn = _rms(p[:, OFF_CQ:OFF_CQ + Q_LORA], gcq_ref[...])
    qc_ref[...] = _dot(cqn.astype(bf16), wuq_ref[...]).astype(bf16)
    caches = ((ka_ref, p[:, OFF_KA:OFF_KA + W_KA]), (va_ref, p[:, OFF_VA:OFF_VA + W_VA]),
              (kb_ref, p[:, OFF_KB:OFF_KB + W_B]), (vb_ref, p[:, OFF_VB:OFF_VB + W_B]),
              (ckv_ref, _rms(p[:, OFF_CKV:OFF_CKV + KV_LORA], gckv_ref[...])),
              (kr_ref, p[:, OFF_KR:OFF_KR + QK_ROPE]))
    for ref, val in caches:
        for b in range(TM_TOK // SEQ):
            rows = val[b * SEQ:(b + 1) * SEQ]
            if layer == 0:
                ref[b, 0] = rows
                for later in range(1, DEPTH):
                    ref[b, later] = jnp.zeros_like(rows)
            else:
                ref[b, 0] = rows


def _inproj_ctx(layer, prev_caches, x, g, shift, scale, w, gcq, gckv, wuq):
    tm = TM_TOK
    nb = tm // SEQ
    row = lambda i: (i, 0)
    const = lambda i: (0, 0)
    in_specs = [pl.BlockSpec((tm, D_MODEL), row),
                pl.BlockSpec((1, D_MODEL), const),
                _mod_spec(shift, lambda i: 0),
                _mod_spec(scale, lambda i: 0),
                pl.BlockSpec((None, D_MODEL, NW_CTX), lambda i: (layer, 0, 0)),
                pl.BlockSpec((1, Q_LORA), const),
                pl.BlockSpec((1, KV_LORA), const),
                pl.BlockSpec((None, Q_LORA, H_C * QC_PAD), lambda i: (layer, 0, 0))]
    q_widths = (W_QA, W_B, H_C * QC_PAD)
    out_specs = [pl.BlockSpec((tm, wd), row) for wd in q_widths]
    out_shape = [jax.ShapeDtypeStruct((T_CTX, wd), bf16) for wd in q_widths]
    if layer == 0:
        out_specs += [pl.BlockSpec((nb, DEPTH, SEQ, wd), lambda i: (i, 0, 0, 0)) for wd in CACHE_WIDTHS]
        aliases, extra = {}, ()
    else:
        in_specs += [pl.BlockSpec(memory_space=pl.ANY) for _ in CACHE_WIDTHS]
        out_specs += [pl.BlockSpec((nb, 1, SEQ, wd), lambda i: (i, layer, 0, 0)) for wd in CACHE_WIDTHS]
        aliases = {8 + j: len(q_widths) + j for j in range(len(CACHE_WIDTHS))}
        extra = tuple(prev_caches)
    out_shape += [jax.ShapeDtypeStruct((BATCH, DEPTH, SEQ, wd), f32) for wd in CACHE_WIDTHS]
    outs = pl.pallas_call(
        functools.partial(_inproj_ctx_kernel, layer),
        grid=(T_CTX // tm,),
        in_specs=in_specs,
        out_specs=out_specs,
        out_shape=out_shape,
        input_output_aliases=aliases,
        compiler_params=_cparams("arbitrary"),
        name="inproj_ctx",
    )(x, g, shift[0], scale[0], w, gcq, gckv, wuq, *extra)
    return outs[:3], outs[3:]


def _inproj_lat_kernel(x_ref, g_ref, sh_ref, sc_ref, w_ref, gcq_ref, gckv_ref, wuq_ref,
                       cosa_ref, sina_ref, cosq_ref, sinq_ref, cosr_ref, sinr_ref,
                       qa_ref, ka_ref, va_ref, qb_ref, kb_ref, vb_ref, qc_ref, ckv_ref, kr_ref):
    h = _rms(x_ref[...], g_ref[...]) * (1.0 + sc_ref[0]) + sh_ref[0]
    p = _dot(h.astype(bf16), w_ref[...])
    cosa = cosa_ref[...]
    sina = sina_ref[...]
    qa = p[:, OFF_QA:OFF_QA + W_QA] * cosa + p[:, OFF_QA_P:OFF_QA_P + W_QA] * sina
    ka = p[:, OFF_KA:OFF_KA + W_KA] * cosa[:, :W_KA] + p[:, OFF_KA_P:OFF_KA_P + W_KA] * sina[:, :W_KA]
    kr = p[:, OFF_KR:OFF_KR + QK_ROPE] * cosr_ref[...] + p[:, OFF_KR_P:OFF_KR_P + QK_ROPE] * sinr_ref[...]
    qa_ref[...] = qa.astype(bf16)
    ka_ref[...] = ka.astype(bf16)
    va_ref[...] = p[:, OFF_VA:OFF_VA + W_VA].astype(bf16)
    qb_ref[...] = p[:, OFF_QB:OFF_QB + W_B].astype(bf16)
    kb_ref[...] = p[:, OFF_KB:OFF_KB + W_B].astype(bf16)
    vb_ref[...] = p[:, OFF_VB:OFF_VB + W_B].astype(bf16)
    cqn = _rms(p[:, OFF_CQ:OFF_CQ + Q_LORA], gcq_ref[...])
    q2 = _dot(cqn.astype(bf16), wuq_ref[...])
    nq = H_C * QC_PAD
    qc_ref[...] = (q2[:, :nq] * cosq_ref[...] + q2[:, nq:] * sinq_ref[...]).astype(bf16)
    ckv_ref[...] = _rms(p[:, OFF_CKV:OFF_CKV + KV_LORA], gckv_ref[...]).astype(bf16)
    kr_ref[...] = kr.astype(bf16)


def _inproj_lat(layer, x, lat_row0, g, shift, scale, w, gcq, gckv, wuq2, tabs):
    tm = TM_LAT_IN
    per_b = DEC_SEQ // tm
    row0 = lat_row0 // tm
    xrow = lambda i: (row0 + i, 0)
    row = lambda i: (i, 0)
    const = lambda i: (0, 0)
    pos = lambda i: (i % per_b, 0)
    cosa, sina, cosq, sinq, cosr, sinr = tabs
    widths = (W_QA, W_KA, W_VA, W_B, W_B, W_B, H_C * QC_PAD, KV_LORA, QK_ROPE)
    return pl.pallas_call(
        _inproj_lat_kernel,
        grid=(T_LAT // tm,),
        in_specs=[pl.BlockSpec((tm, D_MODEL), xrow),
                  pl.BlockSpec((1, D_MODEL), const),
                  _mod_spec(shift, lambda i: 1 + i // per_b),
                  _mod_spec(scale, lambda i: 1 + i // per_b),
                  pl.BlockSpec((None, D_MODEL, NW_LAT), lambda i: (layer, 0, 0)),
                  pl.BlockSpec((1, Q_LORA), const),
                  pl.BlockSpec((1, KV_LORA), const),
                  pl.BlockSpec((None, Q_LORA, 2 * H_C * QC_PAD), lambda i: (layer, 0, 0)),
                  pl.BlockSpec((tm, W_QA), pos), pl.BlockSpec((tm, W_QA), pos),
                  pl.BlockSpec((tm, H_C * QC_PAD), pos), pl.BlockSpec((tm, H_C * QC_PAD), pos),
                  pl.BlockSpec((tm, QK_ROPE), pos), pl.BlockSpec((tm, QK_ROPE), pos)],
        out_specs=[pl.BlockSpec((tm, wd), row) for wd in widths],
        out_shape=[jax.ShapeDtypeStruct((T_LAT, wd), bf16) for wd in widths],
        compiler_params=_cparams("arbitrary"),
        name="inproj_lat",
    )(x, g, shift[0], scale[0], w, gcq, gckv, wuq2, cosa, sina, cosq, sinq, cosr, sinr)


CTX_BATCHES = 2


def _ctx_attn_kernel(sink_ref, qa_ref, ka_ref, va_ref, qb_ref, kb_ref, vb_ref, qc_ref, ckv_ref, kr_ref,
                     wukv_ref, wout_ref, x_ref, gate_ref, o_ref, o_scr, s_scr, p_scr):
    n = SEQ
    scale = HEAD_DIM ** -0.5
    scale_c = (QK_NOPE + QK_ROPE) ** -0.5

    def one_batch(sb, carry):
        rows = pl.ds(pl.multiple_of(sb * n, n), n)
        ka = ka_ref[sb, 0].astype(bf16)
        va = va_ref[sb, 0].astype(bf16)
        kb = kb_ref[sb, 0].astype(bf16)
        vb = vb_ref[sb, 0].astype(bf16)
        kv = _dot(ckv_ref[sb, 0].astype(bf16), wukv_ref[...]).astype(bf16)
        kr = kr_ref[sb, 0].astype(bf16)
        for h in range(H_A):
            g = h // G_A
            q = qa_ref[rows, h * HEAD_DIM:(h + 1) * HEAD_DIM]
            s_scr[h * n:(h + 1) * n, :] = _dot_nt(q, ka[:, g * HEAD_DIM:(g + 1) * HEAD_DIM]) * scale
        for h in range(H_B):
            sl = slice(h * HEAD_DIM, (h + 1) * HEAD_DIM)
            s_scr[(H_A + h) * n:(H_A + h + 1) * n, :] = _dot_nt(qb_ref[rows, sl], kb[:, sl]) * scale
        for h in range(H_C):
            qn = qc_ref[rows, h * QC_PAD:h * QC_PAD + QK_NOPE]
            qr = qc_ref[rows, h * QC_PAD + QK_NOPE:h * QC_PAD + QK_NOPE + QK_ROPE]
            c0 = h * (QK_NOPE + V_C)
            r0 = (H_A + H_B + h) * n
            s_scr[r0:r0 + n, :] = (_dot_nt(qn, kv[:, c0:c0 + QK_NOPE]) + _dot_nt(qr, kr)) * scale_c
        for pair in range((H_A + H_B + H_C) // 2):
            h0 = 2 * pair
            sinks = ((sink_ref[h0], n), (sink_ref[h0 + 1], n)) if h0 < H_A else None
            _softmax_rows(s_scr, p_scr, slice(h0 * n, (h0 + 2) * n), sinks)
        for h in range(H_A):
            g = h // G_A
            o_scr[:, h * HEAD_DIM:(h + 1) * HEAD_DIM] = _dot(p_scr[h * n:(h + 1) * n, :],
                                                             va[:, g * HEAD_DIM:(g + 1) * HEAD_DIM])
        for h in range(H_B):
            sl = slice(h * HEAD_DIM, (h + 1) * HEAD_DIM)
            o_scr[:, W_QA + h * HEAD_DIM:W_QA + (h + 1) * HEAD_DIM] = _dot(
                p_scr[(H_A + h) * n:(H_A + h + 1) * n, :], vb[:, sl])
        for h in range(H_C):
            c0 = h * (QK_NOPE + V_C)
            r0 = (H_A + H_B + h) * n
            off = W_QA + W_B + h * V_C
            o_scr[:, off:off + V_C] = _dot(p_scr[r0:r0 + n, :], kv[:, c0 + QK_NOPE:c0 + QK_NOPE + V_C])
        y = _dot(o_scr[...].astype(bf16), wout_ref[...])
        o_ref[rows, :] = x_ref[rows, :] + gate_ref[0] * y
        return carry

    lax.fori_loop(0, CTX_BATCHES, one_batch, 0)


def _ctx_attn(layer, sink, qs, caches, wukv, wout, x, gate):
    qa, qb, qc = qs
    ka, va, kb, vb, ckv, kr = caches
    row = lambda b: (b, 0)
    const = lambda b: (0, 0)
    slot = lambda b: (b, layer, 0, 0)
    nrow = CTX_BATCHES * SEQ
    qspec = lambda a: pl.BlockSpec((nrow, a.shape[1]), row)
    cspec = lambda a: pl.BlockSpec((CTX_BATCHES, 1, SEQ, a.shape[3]), slot)
    in_specs = [pl.BlockSpec(memory_space=pltpu.SMEM),
                qspec(qa), cspec(ka), cspec(va), qspec(qb), cspec(kb), cspec(vb), qspec(qc), cspec(ckv), cspec(kr)]
    in_specs += [pl.BlockSpec((None, KV_LORA, H_C * (QK_NOPE + V_C)), lambda b: (layer, 0, 0)),
                 pl.BlockSpec((None, D_MODEL, D_MODEL), lambda b: (layer, 0, 0)),
                 pl.BlockSpec((nrow, D_MODEL), row),
                 _mod_spec(gate, lambda b: 0)]
    return pl.pallas_call(
        _ctx_attn_kernel,
        grid=(BATCH // CTX_BATCHES,),
        in_specs=in_specs,
        out_specs=pl.BlockSpec((nrow, D_MODEL), row),
        out_shape=jax.ShapeDtypeStruct((T_CTX, D_MODEL), f32),
        scratch_shapes=[pltpu.VMEM((SEQ, D_MODEL), f32),
                        pltpu.VMEM(((H_A + H_B + H_C) * SEQ, SEQ), f32),
                        pltpu.VMEM(((H_A + H_B + H_C) * SEQ, SEQ), bf16)],
        compiler_params=_cparams("arbitrary"),
        name="ctx_attn",
    )(sink, qa, ka, va, qb, kb, vb, qc, ckv, kr, wukv, wout, x, gate[0])


def _lat_attn_kernel(sink_ref, qa_ref, qb_ref, qc_ref, ka_ref, va_ref, kb_ref, vb_ref, ckv_ref, kr_ref,
                     cak_ref, cav_ref, cbk_ref, cbv_ref, cckv_ref, ckr_ref, bias_ref,
                     wukv_ref, wout_ref, x_ref, gate_ref, o_ref, o_scr, kv_scr, sa, pa, sb, pb, sc, pc):
    qi = pl.program_id(1)
    nb = DEC_SEQ // BLOCK
    scale = HEAD_DIM ** -0.5

    @pl.when(qi == 0)
    def _():
        kv_scr[0:DEC_SEQ, :] = _dot(ckv_ref[...], wukv_ref[...]).astype(bf16)
        kv_scr[DEC_SEQ:DEC_SEQ + PAST_LEN, :] = _dot(cckv_ref[0, 0].astype(bf16), wukv_ref[...]).astype(bf16)

    def blk(ref, j):
        idx = jnp.clip(qi + j, 0, nb - 1)
        return ref[pl.ds(pl.multiple_of(idx * BLOCK, BLOCK), BLOCK), :]

    ka = jnp.concatenate([blk(ka_ref, -1), blk(ka_ref, 0), blk(ka_ref, 1), cak_ref[0, 0].astype(bf16)], axis=0)
    va = jnp.concatenate([blk(va_ref, -1), blk(va_ref, 0), blk(va_ref, 1), cav_ref[0, 0].astype(bf16)], axis=0)
    nk_a = 3 * BLOCK + PAST_LEN
    r = lax.broadcasted_iota(jnp.int32, (BLOCK, nk_a), 0)
    c = lax.broadcasted_iota(jnp.int32, (BLOCK, nk_a), 1)
    valid = (((c < BLOCK) & (c >= r) & (qi > 0))
             | ((c >= BLOCK) & (c < 2 * BLOCK))
             | ((c >= 2 * BLOCK) & (c < 3 * BLOCK) & (c - 2 * BLOCK <= r) & (qi < nb - 1))
             | (c >= 3 * BLOCK))
    for h in range(H_A):
        g = h // G_A
        q = qa_ref[:, h * HEAD_DIM:(h + 1) * HEAD_DIM]
        s = _dot_nt(q, ka[:, g * HEAD_DIM:(g + 1) * HEAD_DIM]) * scale
        sa[h * BLOCK:(h + 1) * BLOCK, :] = jnp.where(valid, s, NEG)

    cbk = cbk_ref[0, 0].astype(bf16)
    cbv = cbv_ref[0, 0].astype(bf16)
    rows_per_blk = BLOCK // GRID_W
    nloc = NA_ROWS * GRID_W
    vcats = []
    for half in range(rows_per_blk):
        grow = qi * rows_per_blk + half
        start = jnp.clip(grow - NA_ROWS // 2, 0, ROWS - NA_ROWS)
        kloc = kb_ref[pl.ds(pl.multiple_of(start * GRID_W, GRID_W), nloc), :]
        vloc = vb_ref[pl.ds(pl.multiple_of(start * GRID_W, GRID_W), nloc), :]
        vcats.append(jnp.concatenate([vloc, cbv], axis=0))
        qrows = slice(half * GRID_W, (half + 1) * GRID_W)
        dr0 = start - grow + (NA_ROWS - 1)
        for h in range(H_B):
            sl = slice(h * HEAD_DIM, (h + 1) * HEAD_DIM)
            q = qb_ref[qrows, sl]
            bias = jnp.concatenate([bias_ref[h, dr0 + 2 * j] for j in range(NA_ROWS // 2)], axis=1)
            s_loc = _dot_nt(q, kloc[:, sl]) * scale + bias
            s_ctx = _dot_nt(q, cbk[:, sl]) * scale
            r0 = (half * H_B + h) * GRID_W
            sb[r0:r0 + GRID_W, :] = jnp.concatenate([s_loc, s_ctx], axis=1)

    kr = jnp.concatenate([kr_ref[...], ckr_ref[0, 0].astype(bf16)], axis=0)
    scale_c = (QK_NOPE + QK_ROPE) ** -0.5
    for h in range(H_C):
        qn = qc_ref[:, h * QC_PAD:h * QC_PAD + QK_NOPE]
        qr = qc_ref[:, h * QC_PAD + QK_NOPE:h * QC_PAD + QK_NOPE + QK_ROPE]
        c0 = h * (QK_NOPE + V_C)
        sc[h * BLOCK:(h + 1) * BLOCK, :] = (_dot_nt(qn, kv_scr[:, c0:c0 + QK_NOPE]) + _dot_nt(qr, kr)) * scale_c

    for pair in range(H_A // 2):
        h0 = 2 * pair
        _softmax_rows(sa, pa, slice(h0 * BLOCK, (h0 + 2) * BLOCK), ((sink_ref[h0], BLOCK), (sink_ref[h0 + 1], BLOCK)))
    for blk2 in range(rows_per_blk * H_B // 2):
        _softmax_rows(sb, pb, slice(blk2 * 2 * GRID_W, (blk2 + 1) * 2 * GRID_W))
    for h in range(H_C):
        _softmax_rows(sc, pc, slice(h * BLOCK, (h + 1) * BLOCK))

    for h in range(H_A):
        g = h // G_A
        o_scr[:, h * HEAD_DIM:(h + 1) * HEAD_DIM] = _dot(pa[h * BLOCK:(h + 1) * BLOCK, :],
                                                         va[:, g * HEAD_DIM:(g + 1) * HEAD_DIM])
    for half in range(rows_per_blk):
        qrows = slice(half * GRID_W, (half + 1) * GRID_W)
        for h in range(H_B):
            sl = slice(h * HEAD_DIM, (h + 1) * HEAD_DIM)
            r0 = (half * H_B + h) * GRID_W
            o_scr[qrows, W_QA + h * HEAD_DIM:W_QA + (h + 1) * HEAD_DIM] = _dot(pb[r0:r0 + GRID_W, :],
                                                                             vcats[half][:, sl])
    for h in range(H_C):
        c0 = h * (QK_NOPE + V_C)
        off = W_QA + W_B + h * V_C
        o_scr[:, off:off + V_C] = _dot(pc[h * BLOCK:(h + 1) * BLOCK, :], kv_scr[:, c0 + QK_NOPE:c0 + QK_NOPE + V_C])

    y = _dot(o_scr[...].astype(bf16), wout_ref[...])
    o_ref[...] = x_ref[...] + gate_ref[0] * y


def _lat_attn(layer, sink, proj, caches, bias_tab, wukv, wout, x, lat_row0, gate):
    qa, ka, va, qb, kb, vb, qc, ckv, kr = proj
    nb = DEC_SEQ // BLOCK
    qrow = lambda b, q: (b * nb + q, 0)
    xrow = lambda b, q: (lat_row0 // BLOCK + b * nb + q, 0)
    brow = lambda b, q: (b, 0)
    const = lambda b, q: (0, 0)
    cidx = lambda b, q: (b, layer, 0, 0)
    in_specs = [pl.BlockSpec(memory_space=pltpu.SMEM)]
    in_specs += [pl.BlockSpec((BLOCK, a.shape[1]), qrow) for a in (qa, qb, qc)]
    in_specs += [pl.BlockSpec((DEC_SEQ, a.shape[1]), brow) for a in (ka, va, kb, vb, ckv, kr)]
    in_specs += [pl.BlockSpec((1, 1, PAST_LEN, a.shape[3]), cidx) for a in caches]
    in_specs += [pl.BlockSpec((None,) + bias_tab.shape[1:], lambda b, q: (layer, 0, 0, 0, 0)),
                 pl.BlockSpec((None, KV_LORA, H_C * (QK_NOPE + V_C)), lambda b, q: (layer, 0, 0)),
                 pl.BlockSpec((None, D_MODEL, D_MODEL), lambda b, q: (layer, 0, 0)),
                 pl.BlockSpec((BLOCK, D_MODEL), xrow),
                 _mod_spec(gate, lambda b, q: 1 + b)]
    return pl.pallas_call(
        _lat_attn_kernel,
        grid=(DEC_BATCH, nb),
        in_specs=in_specs,
        out_specs=pl.BlockSpec((BLOCK, D_MODEL), qrow),
        out_shape=jax.ShapeDtypeStruct((T_LAT, D_MODEL), f32),
        scratch_shapes=[pltpu.VMEM((BLOCK, D_MODEL), f32),
                        pltpu.VMEM((DEC_SEQ + PAST_LEN, H_C * (QK_NOPE + V_C)), bf16)]
        + [pltpu.VMEM(shape, dt) for shape in ((H_A * BLOCK, 3 * BLOCK + PAST_LEN),
                                               (H_B * BLOCK, NA_ROWS * GRID_W + PAST_LEN),
                                               (H_C * BLOCK, DEC_SEQ + PAST_LEN)) for dt in (f32, bf16)],
        compiler_params=_cparams("arbitrary", "arbitrary"),
        name="lat_attn",
    )(sink, qa, qb, qc, ka, va, kb, vb, ckv, kr, *caches, bias_tab, wukv, wout, x, gate[0])


def _pick_stream(xc_ref, xl_ref, x_scr):
    i = pl.program_id(0)

    @pl.when(i < T_CTX // TM_TOK)
    def _():
        x_scr[...] = xc_ref[...]

    @pl.when(i >= T_CTX // TM_TOK)
    def _():
        x_scr[...] = xl_ref[...]

    return x_scr[...]


def _stream_specs(lat_row0):
    n_ctx = T_CTX // TM_TOK
    return [pl.BlockSpec((TM_TOK, D_MODEL), lambda i: (jnp.minimum(i, n_ctx - 1), 0)),
            pl.BlockSpec((TM_TOK, D_MODEL), lambda i: (lat_row0 // TM_TOK + jnp.maximum(i - n_ctx, 0), 0))]


def _router_kernel(xc_ref, xl_ref, g_ref, sh_ref, sc_ref, wr_ref, br_ref, h_ref, e_ref, gt_ref, x_scr):
    h = _rms(_pick_stream(xc_ref, xl_ref, x_scr), g_ref[...]) * (1.0 + sc_ref[0]) + sh_ref[0]
    _store_row_tiles(h_ref, h)
    h_hi = h.astype(bf16)
    h_lo = (h - h_hi.astype(f32)).astype(bf16)
    w = wr_ref[...]
    w_hi = w.astype(bf16)
    w_lo = (w - w_hi.astype(f32)).astype(bf16)
    logits = _dot(h_hi, w_hi) + _dot(h_hi, w_lo) + _dot(h_lo, w_hi) + br_ref[...]
    lane = lax.broadcasted_iota(jnp.int32, logits.shape, 1).astype(f32)
    l = jnp.where(lane < N_EXPERTS, logits, -jnp.inf)
    tops, idxs = [], []
    for _ in range(TOP_K):
        m = jnp.max(l, axis=-1, keepdims=True)
        idx = jnp.min(jnp.where(l == m, lane, float(LANE)), axis=-1, keepdims=True)
        tops.append(m)
        idxs.append(idx)
        l = jnp.where(lane == idx, -jnp.inf, l)
    ex = [jnp.exp(t - tops[0]) for t in tops]
    den = ex[0] + ex[1] + ex[2] + ex[3]
    e_out = jnp.zeros(logits.shape, f32)
    g_out = jnp.zeros(logits.shape, f32)
    for k in range(TOP_K):
        e_out = jnp.where(lane == k, idxs[k], e_out)
        g_out = jnp.where(lane == k, ex[k] / den, g_out)
    e_ref[...] = e_out.astype(jnp.int32)
    gt_ref[...] = g_out


def _group_of_tile(i):
    per_b = DEC_SEQ // TM_TOK
    n_ctx = T_CTX // TM_TOK
    return jnp.where(i < n_ctx, 0, 1 + (i - n_ctx) // per_b)


def _router(layer, xc, xl, lat_row0, g, shift, scale, wr, br):
    tm = TM_TOK
    row = lambda i: (i, 0)
    const = lambda i: (0, 0)
    return pl.pallas_call(
        _router_kernel,
        grid=(T_ALL // tm,),
        in_specs=_stream_specs(lat_row0) +
                 [pl.BlockSpec((1, D_MODEL), const),
                  _mod_spec(shift, _group_of_tile),
                  _mod_spec(scale, _group_of_tile),
                  pl.BlockSpec((None, D_MODEL, LANE), lambda i: (layer, 0, 0)),
                  pl.BlockSpec((None, 1, LANE), lambda i: (layer, 0, 0))],
        out_specs=[pl.BlockSpec((tm * ROW_TILE, LANE), row), pl.BlockSpec((tm, LANE), row),
                   pl.BlockSpec((tm, LANE), row)],
        out_shape=[jax.ShapeDtypeStruct((T_ALL * ROW_TILE, LANE), f32),
                   jax.ShapeDtypeStruct((T_ALL, LANE), jnp.int32),
                   jax.ShapeDtypeStruct((T_ALL, LANE), f32)],
        scratch_shapes=[pltpu.VMEM((tm, D_MODEL), f32)],
        compiler_params=_cparams("arbitrary"),
        name="router",
    )(xc, xl, g, shift[0], scale[0], wr, br)


def _dispatch_kernel(tok_ref, nu_ref, h_hbm, o_ref, hv, xg, hsem):
    tm = TM_MOE
    i = pl.program_id(0)

    @pl.when(i == 0)
    def _():
        resident = pltpu.make_async_copy(h_hbm, hv, hsem.at[0])
        resident.start()
        resident.wait()

    def one_block(sub, carry):
        blk = i * DISPATCH_BLOCKS + sub
        rows = pl.ds(pl.multiple_of(sub * tm, tm), tm)

        @pl.when(blk < nu_ref[0])
        def _():
            for r in range(tm):
                t = tok_ref[blk * tm + r]
                xg[pl.ds(r, ROW_TILE, stride=tm + 1), :] = hv[pl.ds(pl.multiple_of(t * ROW_TILE, ROW_TILE),
                                                                 ROW_TILE), :]
            o_ref[rows, :] = jnp.concatenate([xg[pl.ds(c * (tm + 1), tm), :] for c in range(ROW_TILE)],
                                             axis=1).astype(bf16)

        @pl.when(blk >= nu_ref[0])
        def _():
            o_ref[rows, :] = jnp.zeros((tm, D_MODEL), bf16)

        return carry

    lax.fori_loop(0, DISPATCH_BLOCKS, one_block, 0)


def _dispatch(row_tok, n_used, h):
    tm = TM_MOE
    return pl.pallas_call(
        _dispatch_kernel,
        grid_spec=pltpu.PrefetchScalarGridSpec(
            num_scalar_prefetch=2,
            grid=(N_MOE_BLOCKS // DISPATCH_BLOCKS,),
            in_specs=[pl.BlockSpec(memory_space=pl.ANY)],
            out_specs=pl.BlockSpec((DISPATCH_BLOCKS * tm, D_MODEL), lambda i, tok, nu: (i, 0)),
            scratch_shapes=[pltpu.VMEM((T_ALL * ROW_TILE, LANE), f32), pltpu.VMEM(((tm + 1) * ROW_TILE, LANE), f32),
                            pltpu.SemaphoreType.DMA((1,))]),
        out_shape=jax.ShapeDtypeStruct((N_MOE_BLOCKS * tm, D_MODEL), bf16),
        compiler_params=_cparams("arbitrary"),
        name="dispatch",
    )(row_tok, n_used, h)


def _moe_kernel(layer, be_ref, nu_ref, nxt_ref, dst_ref, x_ref, wgu_hbm, bgu_ref, wd_hbm, bd_ref, y_hbm,
                y0, y1, wgu_st, wd_st, wgu_bf, wd_bf, wsem, ssem):
    tm = TM_MOE
    i = pl.program_id(0)
    nb = pl.num_programs(0)
    used = i < nu_ref[0]
    yb = (y0, y1)

    def out_tile(row):
        return pl.ds(pl.multiple_of(row * ROW_TILE, ROW_TILE), ROW_TILE)

    def scatter_desc(buf, r, dst_row, s):
        return pltpu.make_async_copy(buf.at[out_tile(r)], y_hbm.at[out_tile(dst_row)], ssem.at[s])

    def scatter_wait(s):
        pltpu.make_async_copy(yb[s], y_hbm.at[pl.ds(0, tm * ROW_TILE)], ssem.at[s]).wait()

    def scatter_start(blk, s, unrolled):
        if unrolled:
            for r in range(tm):
                scatter_desc(yb[s], r, dst_ref[(blk + 1) * tm + r], s).start(priority=r % 2)
        else:
            def body(r, carry):
                scatter_desc(yb[s], r, dst_ref[(blk + 1) * tm + r], s).start()
                return carry
            lax.fori_loop(0, tm, body, 0, unroll=8)

    def weight_copies(e):
        return (pltpu.make_async_copy(wgu_hbm.at[layer, e], wgu_st, wsem.at[0]),
                pltpu.make_async_copy(wd_hbm.at[layer, e], wd_st, wsem.at[1]))

    @pl.when(i == 0)
    def _():
        for s in range(2):
            yb[s][...] = jnp.zeros_like(yb[s])
            dummy = pltpu.make_async_copy(yb[s], y_hbm.at[pl.ds((N_ASSIGN + s * tm) * ROW_TILE, tm * ROW_TILE)],
                                          ssem.at[s])
            dummy.start()
            dummy.wait()
        for cp in weight_copies(be_ref[0]):
            cp.start()

    first = jnp.logical_and(used, jnp.logical_or(i == 0, be_ref[i] != be_ref[jnp.maximum(i - 1, 0)]))

    @pl.when(first)
    def _():
        for cp in weight_copies(0):
            cp.wait()
        wgu_bf[...] = wgu_st[...].astype(bf16)
        wd_bf[...] = wd_st[...].astype(bf16)

        @pl.when(nxt_ref[i] >= 0)
        def _():
            for cp in weight_copies(nxt_ref[i]):
                cp.start()

    def step(par):
        cur, oth = par, 1 - par

        @pl.when(jnp.logical_and(i >= 1, i - 2 < nu_ref[0]))
        def _():
            scatter_wait(cur)

        @pl.when(used)
        def _():
            scatter_start(i - 1, oth, unrolled=True)
            gu = _dot(x_ref[...], wgu_bf[...]) + bgu_ref[0, 0]
            x_glu = jnp.minimum(gu[:, :D_FF], SWIGLU_LIMIT)
            x_lin = jnp.clip(gu[:, D_FF:], -SWIGLU_LIMIT, SWIGLU_LIMIT)
            act = x_glu * jax.nn.sigmoid(SWIGLU_ALPHA * x_glu) * (x_lin + 1.0)
            _store_row_tiles(yb[cur], _dot(act.astype(bf16), wd_bf[...]) + bd_ref[0, 0])

        flush = jnp.logical_and(jnp.logical_not(used), i - 1 < nu_ref[0])

        @pl.when(flush)
        def _():
            scatter_start(i - 1, oth, unrolled=False)

        @pl.when(jnp.logical_and(flush, i == nb - 1))
        def _():
            scatter_wait(oth)

    @pl.when(i % 2 == 0)
    def _():
        step(0)

    @pl.when(i % 2 == 1)
    def _():
        step(1)


def _moe(layer, routing, h, w_gu, b_gu, w_down, b_down):
    tm = TM_MOE
    block_e, n_used, nxt_e, row_tok, row_dst = routing
    xs = _dispatch(row_tok, n_used, h)
    ex4 = lambda i, be, nu, nxt, dst: (layer, be[i], 0, 0)
    return pl.pallas_call(
        functools.partial(_moe_kernel, layer),
        grid_spec=pltpu.PrefetchScalarGridSpec(
            num_scalar_prefetch=4,
            grid=(N_MOE_BLOCKS,),
            in_specs=[pl.BlockSpec((tm, D_MODEL), lambda i, be, nu, nxt, dst: (i, 0)),
                      pl.BlockSpec(memory_space=pl.ANY),
                      pl.BlockSpec((1, 1, 1, 2 * D_FF), ex4),
                      pl.BlockSpec(memory_space=pl.ANY),
                      pl.BlockSpec((1, 1, 1, D_MODEL), ex4)],
            out_specs=pl.BlockSpec(memory_space=pl.ANY),
            scratch_shapes=[pltpu.VMEM((tm * ROW_TILE, LANE), f32), pltpu.VMEM((tm * ROW_TILE, LANE), f32),
                            pltpu.VMEM((D_MODEL, 2 * D_FF), f32), pltpu.VMEM((D_FF, D_MODEL), f32),
                            pltpu.VMEM((D_MODEL, 2 * D_FF), bf16), pltpu.VMEM((D_FF, D_MODEL), bf16),
                            pltpu.SemaphoreType.DMA((2,)), pltpu.SemaphoreType.DMA((2,))]),
        out_shape=jax.ShapeDtypeStruct(((N_ASSIGN + 2 * tm) * ROW_TILE, LANE), f32),
        compiler_params=_cparams("arbitrary"),
        name="moe",
    )(block_e, n_used, nxt_e, row_dst, xs, w_gu, b_gu.reshape(DEPTH, N_EXPERTS, 1, 2 * D_FF),
      w_down, b_down.reshape(DEPTH, N_EXPERTS, 1, D_MODEL))


def _combine_kernel(final, xc_ref, xl_ref, y0_ref, y1_ref, y2_ref, y3_ref, gt_ref, gate_ref, gf_ref, *rest):
    x_scr = rest[-1]
    gt = gt_ref[...]
    f = gt[:, 0:1] * _load_row_tiles(y0_ref)
    for k, y_ref in ((1, y1_ref), (2, y2_ref), (3, y3_ref)):
        f = f + gt[:, k:k + 1] * _load_row_tiles(y_ref)
    out = _pick_stream(xc_ref, xl_ref, x_scr) + gate_ref[0] * f
    if not final:
        rest[0][...] = out
        return
    out = _rms(out, gf_ref[...])
    oc_ref, ol_ref = rest[0], rest[1]
    i = pl.program_id(0)

    @pl.when(i < T_CTX // TM_TOK)
    def _():
        oc_ref[...] = out

    @pl.when(i >= T_CTX // TM_TOK)
    def _():
        ol_ref[...] = out


def _combine(final, xc, xl, lat_row0, y, gates, gate, g_final):
    tm = TM_TOK
    nt = T_ALL // tm
    n_ctx = T_CTX // tm
    row = lambda i: (i, 0)
    const = lambda i: (0, 0)
    ysel = [pl.BlockSpec((tm * ROW_TILE, LANE), functools.partial(lambda k, i: (k * nt + i, 0), k))
            for k in range(TOP_K)]
    if final:
        out_specs = [pl.BlockSpec((tm, D_MODEL), lambda i: (jnp.minimum(i, n_ctx - 1), 0)),
                     pl.BlockSpec((tm, D_MODEL), lambda i: (jnp.maximum(i - n_ctx, 0), 0))]
        out_shape = [jax.ShapeDtypeStruct((T_CTX, D_MODEL), f32), jax.ShapeDtypeStruct((T_LAT, D_MODEL), f32)]
    else:
        out_specs = pl.BlockSpec((tm, D_MODEL), row)
        out_shape = jax.ShapeDtypeStruct((T_ALL, D_MODEL), f32)
    return pl.pallas_call(
        functools.partial(_combine_kernel, final),
        grid=(nt,),
        in_specs=_stream_specs(lat_row0) + ysel +
                 [pl.BlockSpec((tm, LANE), row),
                  _mod_spec(gate, _group_of_tile),
                  pl.BlockSpec((1, D_MODEL), const)],
        out_specs=out_specs,
        out_shape=out_shape,
        scratch_shapes=[pltpu.VMEM((tm, D_MODEL), f32)],
        compiler_params=_cparams("arbitrary"),
        name="combine",
    )(xc, xl, y, y, y, y, gates, gate[0], g_final)


def _rope_head_tables(d):
    nf = d // 4
    half = d // 2
    t = np.arange(DEC_SEQ)
    inv = ROPE_BASE ** (-np.arange(nf, dtype=np.float32) / nf)
    i = np.arange(d)
    pos = np.where(i[None, :] < half, (t // GRID_W)[:, None], (t % GRID_W)[:, None]).astype(np.float32)
    ang = pos * inv[i % nf][None, :].astype(np.float32)
    first = (i % half) < nf
    cos = np.cos(ang)
    sin = np.where(first[None, :], -np.sin(ang), np.sin(ang))
    partner = np.where(first, i + nf, i - nf)
    return cos.astype(np.float32), sin.astype(np.float32), partner


def _rope_tables():
    cos64, sin64, _ = _rope_head_tables(HEAD_DIM)
    cos32, sin32, _ = _rope_head_tables(QK_ROPE)
    cosa = np.tile(cos64, (1, H_A))
    sina = np.tile(sin64, (1, H_A))
    cosq1 = np.concatenate([np.ones((DEC_SEQ, QK_NOPE), np.float32), cos32,
                            np.ones((DEC_SEQ, QC_PAD - QK_NOPE - QK_ROPE), np.float32)], axis=1)
    sinq1 = np.concatenate([np.zeros((DEC_SEQ, QK_NOPE), np.float32), sin32,
                            np.zeros((DEC_SEQ, QC_PAD - QK_NOPE - QK_ROPE), np.float32)], axis=1)
    cosq = np.tile(cosq1, (1, H_C))
    sinq = np.tile(sinq1, (1, H_C))
    return tuple(jnp.asarray(a) for a in (cosa, sina, cosq, sinq, cos32, sin32))


def _pad_cols(w, n):
    return jnp.pad(w, ((0, 0), (0, n - w.shape[1])))


def _layer_weights(w_in, w_uq):
    cuts = np.cumsum((W_QA, W_KA, W_VA, W_B, W_B, W_B, Q_LORA, KV_LORA, QK_ROPE))[:-1]
    qa, ka, va, qb, kb, vb, cq, ckv, kr = jnp.split(w_in, [int(c) for c in cuts], axis=1)
    _, _, p64 = _rope_head_tables(HEAD_DIM)
    _, _, p32 = _rope_head_tables(QK_ROPE)
    pa = np.concatenate([h * HEAD_DIM + p64 for h in range(H_A)])
    base = jnp.concatenate([qa, ka, va, _pad_cols(qb, 384), _pad_cols(kb, 384), _pad_cols(vb, 384), cq, ckv,
                            _pad_cols(kr, 128)], axis=1)
    w_ctx = base.astype(bf16)
    w_lat = jnp.concatenate([base, qa[:, pa], ka[:, pa[:W_KA]], _pad_cols(kr[:, p32], 128)], axis=1).astype(bf16)
    hq = QK_NOPE + QK_ROPE
    heads = [_pad_cols(w_uq[:, h * hq:(h + 1) * hq], QC_PAD) for h in range(H_C)]
    pq = np.concatenate([np.arange(QK_NOPE), QK_NOPE + p32])
    heads_p = [_pad_cols(w_uq[:, h * hq:(h + 1) * hq][:, pq], QC_PAD) for h in range(H_C)]
    wuq = jnp.concatenate(heads, axis=1).astype(bf16)
    wuq2 = jnp.concatenate(heads + heads_p, axis=1).astype(bf16)
    return w_ctx, w_lat, wuq, wuq2


def _bias_table(rpb):
    col = np.arange(GRID_W)
    col_start = np.clip(col - NA_COLS // 2, 0, GRID_W - NA_COLS)
    col_ok = (col[None, :] >= col_start[:, None]) & (col[None, :] < col_start[:, None] + NA_COLS)
    dc = np.clip(col[None, :] - col[:, None] + (NA_COLS - 1), 0, 2 * NA_COLS - 2)
    onehot = (dc[None] == np.arange(2 * NA_COLS - 1)[:, None, None]).astype(np.float32)
    expanded = jnp.einsum('hrd,dqk->hrqk', rpb.astype(f32), jnp.asarray(onehot), precision=lax.Precision.HIGHEST)
    blocks = jnp.where(col_ok[None, None], expanded, NEG)
    return jnp.concatenate([blocks[:, :-1], blocks[:, 1:]], axis=-1)


def _routing(top_e):
    tm = TM_MOE
    key_bits = 16
    pad_mark = (1 << key_bits) - 1
    flat_e = top_e.T.reshape(N_ASSIGN)
    experts = jnp.arange(N_EXPERTS, dtype=jnp.int32)
    counts = jnp.sum((flat_e[:, None] == experts[None, :]).astype(jnp.int32), axis=0)
    nblk = (counts + tm - 1) // tm
    blk_end = jnp.cumsum(nblk)
    pad_end = jnp.cumsum(nblk * tm - counts)
    slots = jnp.arange(N_MOE_BLOCKS * tm - N_ASSIGN, dtype=jnp.int32)
    pad_e = jnp.sum((pad_end[None, :] <= slots[:, None]).astype(jnp.int32), axis=1)
    keys = jnp.concatenate([(flat_e << key_bits) + jnp.arange(N_ASSIGN, dtype=jnp.int32),
                            (pad_e << key_bits) + pad_mark])
    asg = (jnp.sort(keys, stable=False) & pad_mark).reshape(N_MOE_BLOCKS, tm)
    valid = asg != pad_mark
    blocks = jnp.arange(N_MOE_BLOCKS, dtype=jnp.int32)
    r = jnp.arange(tm, dtype=jnp.int32)[None, :]
    tok = jnp.where(valid, asg % T_ALL, 0)
    row_dst = jnp.where(valid, asg, N_ASSIGN + (blocks[:, None] % 2) * tm + r)
    row_dst = jnp.concatenate([N_ASSIGN + tm + r, row_dst], axis=0).reshape(-1)
    block_e = jnp.minimum(jnp.sum((blk_end[None, :] <= blocks[:, None]).astype(jnp.int32), axis=1), N_EXPERTS - 1)
    n_used = blk_end[-1].astype(jnp.int32).reshape(1)
    has = jnp.where(counts > 0, experts, N_EXPERTS)
    later = experts[None, :] > experts[:, None]
    nxt = jnp.min(jnp.where(later, has[None, :], N_EXPERTS), axis=1)
    nxt = jnp.where(nxt >= N_EXPERTS, -1, nxt)
    sel = (block_e[:, None] == experts[None, :]).astype(jnp.int32)
    nxt_e = jnp.sum(sel * nxt[None, :], axis=1)
    i32 = lambda a: a.astype(jnp.int32)
    return i32(block_e), n_used, i32(nxt_e), i32(tok).reshape(-1), i32(row_dst)


def kernel(x_prompt, x_sample, cache_a_k, cache_a_v, cache_b_k, cache_b_v, cache_c_kv, cache_c_kr, c, c_ctx, w_ada, b_ada, g_attn, g_ffn, w_in, sink_a, rpb_b, g_cq, g_ckv, w_uq, w_ukv, w_out, w_router, b_router, w_gu, b_gu, w_down, b_down, g_final):
    xc, xl, lat_row0 = x_prompt.reshape(T_CTX, D_MODEL), x_sample.reshape(T_LAT, D_MODEL), 0
    cvec = jnp.concatenate([c_ctx[None, :], c, jnp.zeros((8 - N_GROUPS, D_MODEL), f32)], axis=0)
    mods = _ada(cvec, w_ada, b_ada)[:, :N_GROUPS].reshape(DEPTH, N_GROUPS, 6, 1, D_MODEL)
    tabs = _rope_tables()
    caches = (cache_a_k.reshape(DEC_BATCH, DEPTH, PAST_LEN, W_KA), cache_a_v.reshape(DEC_BATCH, DEPTH, PAST_LEN, W_VA),
              cache_b_k.reshape(DEC_BATCH, DEPTH, PAST_LEN, W_B), cache_b_v.reshape(DEC_BATCH, DEPTH, PAST_LEN, W_B),
              cache_c_kv, cache_c_kr)
    new = None
    w_ctx, w_lat, wuq, wuq2 = jax.vmap(_layer_weights)(w_in, w_uq)
    wukv = w_ukv.astype(bf16)
    wout = w_out.astype(bf16)
    bias_tab = jax.vmap(_bias_table)(rpb_b)
    wr = jnp.pad(w_router, ((0, 0), (0, 0), (0, LANE - N_EXPERTS)))
    br = jnp.pad(b_router, ((0, 0), (0, LANE - N_EXPERTS)))[:, None, :]
    for layer in range(DEPTH):
        m = [(mods, layer, j) for j in range(6)]
        g1 = g_attn[layer][None, :]
        gcq = g_cq[layer][None, :]
        gckv = g_ckv[layer][None, :]
        sink = sink_a[layer]

        qs, new = _inproj_ctx(layer, new, xc, g1, m[0], m[1], w_ctx, gcq, gckv, wuq)
        x_ctx = _ctx_attn(layer, sink, qs, new, wukv, wout, xc, m[2])

        plat = _inproj_lat(layer, xl, lat_row0, g1, m[0], m[1], w_lat, gcq, gckv, wuq2, tabs)
        x_lat = _lat_attn(layer, sink, plat, caches, bias_tab, wukv, wout, xl, lat_row0, m[2])

        h2, top_e, gates = _router(layer, x_ctx, x_lat, 0, g_ffn[layer][None, :], m[3], m[4], wr, br)
        y = _moe(layer, _routing(top_e[:, :TOP_K]), h2, w_gu, b_gu, w_down, b_down)
        x = _combine(layer == DEPTH - 1, x_ctx, x_lat, 0, y, gates, m[5], g_final[None, :])
        xc, xl, lat_row0 = x, x, T_CTX

    y_prompt = x[0].reshape(BATCH, SEQ, D_MODEL)
    y_sample = x[1].reshape(DEC_BATCH, DEC_SEQ, D_MODEL)
    shapes = ((KV_A, HEAD_DIM), (KV_A, HEAD_DIM), (H_B, HEAD_DIM), (H_B, HEAD_DIM), (KV_LORA,), (QK_ROPE,))
    outs = [a.reshape((BATCH, DEPTH, SEQ) + s) for a, s in zip(new, shapes)]
    return (y_prompt, y_sample, *outs)
```

```python
import functools

import numpy as np
import jax
import jax.numpy as jnp
from jax import lax
from jax.experimental import pallas as pl
from jax.experimental.pallas import tpu as pltpu

D_MODEL = 1024
BATCH = 32
SEQ = 256
DEPTH = 2
DEC_BATCH = 2
DEC_SEQ = 1024
PAST_LEN = 512
GRID_W = 64
HEAD_DIM = 64
H_A = 6
KV_A = 2
G_A = H_A // KV_A
WINDOW = 128
BLOCK = 128
H_B = 5
NA_ROWS = 8
NA_COLS = 16
H_C = 5
Q_LORA = 384
KV_LORA = 256
QK_NOPE = 64
QK_ROPE = 32
V_C = 64
N_EXPERTS = 32
TOP_K = 4
D_FF = 1024
SWIGLU_ALPHA = 1.702
SWIGLU_LIMIT = 7.0
ROPE_BASE = 10000.0
EPS = 1e-6
NEG = -1e30

T_CTX = BATCH * SEQ
T_LAT = DEC_BATCH * DEC_SEQ
T_ALL = T_CTX + T_LAT
N_GROUPS = 1 + DEC_BATCH
LANE = 128
QC_PAD = 128
ROWS = DEC_SEQ // GRID_W

W_QA, W_KA, W_VA = H_A * HEAD_DIM, KV_A * HEAD_DIM, KV_A * HEAD_DIM
W_B = H_B * HEAD_DIM
OFF_QA = 0
OFF_KA = 384
OFF_VA = 512
OFF_QB = 640
OFF_KB = 1024
OFF_VB = 1408
OFF_CQ = 1792
OFF_CKV = 2176
OFF_KR = 2432
NW_CTX = 2560
OFF_QA_P = 2560
OFF_KA_P = 2944
OFF_KR_P = 3072
NW_LAT = 3200

TM_TOK = 512
TM_LAT_IN = 512
TM_MOE = 256
N_ASSIGN = T_ALL * TOP_K
N_MOE_BLOCKS = N_ASSIGN // TM_MOE + N_EXPERTS
DISPATCH_BLOCKS = 8
VMEM_LIMIT = 56 * 1024 * 1024

f32 = jnp.float32
bf16 = jnp.bfloat16


def _cparams(*sem):
    return pltpu.CompilerParams(dimension_semantics=sem, vmem_limit_bytes=VMEM_LIMIT)


def _mod_spec(mod, group_of):
    _, layer, j = mod
    return pl.BlockSpec((None, 1, None, 1, D_MODEL), lambda *idx: (layer, group_of(*idx), j, 0, 0))


def _rms(xf, g):
    return xf * lax.rsqrt(jnp.mean(xf * xf, axis=-1, keepdims=True) + EPS) * g


def _dot(a, b):
    return jnp.dot(a, b, preferred_element_type=f32)


def _dot_nt(a, b):
    return lax.dot_general(a, b, (((1,), (1,)), ((), ())), preferred_element_type=f32)


ROW_TILE = D_MODEL // LANE


def _store_row_tiles(ref, val):
    n = val.shape[0]
    for c in range(ROW_TILE):
        ref[pl.ds(c, n, stride=ROW_TILE), :] = val[:, c * LANE:(c + 1) * LANE]


def _load_row_tiles(ref):
    n = ref.shape[0] // ROW_TILE
    return jnp.concatenate([ref[pl.ds(c, n, stride=ROW_TILE), :] for c in range(ROW_TILE)], axis=1)


def _softmax_rows(s_ref, p_ref, rows, sinks=None):
    s = s_ref[rows, :]
    m = jnp.max(s, axis=-1, keepdims=True)
    if sinks is not None:
        sink = jnp.concatenate([jnp.full((n, 1), v, f32) for v, n in sinks], axis=0)
        m = jnp.maximum(m, sink)
    p = jnp.exp(s - m)
    l = jnp.sum(p, axis=-1, keepdims=True)
    if sinks is not None:
        l = l + jnp.exp(sink - m)
    p_ref[rows, :] = (p * (1.0 / l)).astype(bf16)


def _ada_kernel(c_ref, w_ref, b_ref, o_ref):
    c = c_ref[...]
    s = c * jax.nn.sigmoid(c)
    s_hi = s.astype(bf16)
    s_lo = (s - s_hi.astype(f32)).astype(bf16)
    w = w_ref[0]
    w_hi = w.astype(bf16)
    w_lo = (w - w_hi.astype(f32)).astype(bf16)
    o_ref[0] = _dot(s_hi, w_hi) + _dot(s_hi, w_lo) + _dot(s_lo, w_hi) + b_ref[0]


def _ada(cvec, w_ada, b_ada):
    tn = 1536
    return pl.pallas_call(
        _ada_kernel,
        grid=(DEPTH, 6 * D_MODEL // tn),
        in_specs=[pl.BlockSpec((8, D_MODEL), lambda l, j: (0, 0)),
                  pl.BlockSpec((1, D_MODEL, tn), lambda l, j: (l, 0, j)),
                  pl.BlockSpec((1, 1, tn), lambda l, j: (l, 0, j))],
        out_specs=pl.BlockSpec((1, 8, tn), lambda l, j: (l, 0, j)),
        out_shape=jax.ShapeDtypeStruct((DEPTH, 8, 6 * D_MODEL), f32),
        compiler_params=_cparams("arbitrary", "arbitrary"),
        name="ada",
    )(cvec, w_ada, b_ada.reshape(DEPTH, 1, 6 * D_MODEL))


CACHE_WIDTHS = (W_KA, W_VA, W_B, W_B, KV_LORA, QK_ROPE)


def _inproj_ctx_kernel(layer, x_ref, g_ref, sh_ref, sc_ref, w_ref, gcq_ref, gckv_ref, wuq_ref, *refs):
    qa_ref, qb_ref, qc_ref, ka_ref, va_ref, kb_ref, vb_ref, ckv_ref, kr_ref = refs[-9:]
    h = _rms(x_ref[...], g_ref[...]) * (1.0 + sc_ref[0]) + sh_ref[0]
    p = _dot(h.astype(bf16), w_ref[...])
    qa_ref[...] = p[:, OFF_QA:OFF_QA + W_QA].astype(bf16)
    qb_ref[...] = p[:, OFF_QB:OFF_QB + W_B].astype(bf16)
    cqn = _rms(p[:, OFF_CQ:OFF_CQ + Q_LORA], gcq_ref[...])
    qc_ref[...] = _dot(cqn.astype(bf16), wuq_ref[...]).astype(bf16)
    caches = ((ka_ref, p[:, OFF_KA:OFF_KA + W_KA]), (va_ref, p[:, OFF_VA:OFF_VA + W_VA]),
              (kb_ref, p[:, OFF_KB:OFF_KB + W_B]), (vb_ref, p[:, OFF_VB:OFF_VB + W_B]),
              (ckv_ref, _rms(p[:, OFF_CKV:OFF_CKV + KV_LORA], gckv_ref[...])),
              (kr_ref, p[:, OFF_KR:OFF_KR + QK_ROPE]))
    for ref, val in caches:
        for b in range(TM_TOK // SEQ):
            rows = val[b * SEQ:(b + 1) * SEQ]
            if layer == 0:
                ref[b, 0] = rows
                for later in range(1, DEPTH):
                    ref[b, later] = jnp.zeros_like(rows)
            else:
                ref[b, 0] = rows


def _inproj_ctx(layer, prev_caches, x, g, shift, scale, w, gcq, gckv, wuq):
    tm = TM_TOK
    nb = tm // SEQ
    row = lambda i: (i, 0)
    const = lambda i: (0, 0)
    in_specs = [pl.BlockSpec((tm, D_MODEL), row),
                pl.BlockSpec((None, 1, D_MODEL), lambda i: (layer, 0, 0)),
                _mod_spec(shift, lambda i: 0),
                _mod_spec(scale, lambda i: 0),
                pl.BlockSpec((None, D_MODEL, NW_CTX), lambda i: (layer, 0, 0)),
                pl.BlockSpec((None, 1, Q_LORA), lambda i: (layer, 0, 0)),
                pl.BlockSpec((None, 1, KV_LORA), lambda i: (layer, 0, 0)),
                pl.BlockSpec((None, Q_LORA, H_C * QC_PAD), lambda i: (layer, 0, 0))]
    q_widths = (W_QA, W_B, H_C * QC_PAD)
    out_specs = [pl.BlockSpec((tm, wd), row) for wd in q_widths]
    out_shape = [jax.ShapeDtypeStruct((T_CTX, wd), bf16) for wd in q_widths]
    if layer == 0:
        out_specs += [pl.BlockSpec((nb, DEPTH, SEQ, wd), lambda i: (i, 0, 0, 0)) for wd in CACHE_WIDTHS]
        aliases, extra = {}, ()
    else:
        in_specs += [pl.BlockSpec(memory_space=pl.ANY) for _ in CACHE_WIDTHS]
        out_specs += [pl.BlockSpec((nb, 1, SEQ, wd), lambda i: (i, layer, 0, 0)) for wd in CACHE_WIDTHS]
        aliases = {8 + j: len(q_widths) + j for j in range(len(CACHE_WIDTHS))}
        extra = tuple(prev_caches)
    out_shape += [jax.ShapeDtypeStruct((BATCH, DEPTH, SEQ, wd), f32) for wd in CACHE_WIDTHS]
    outs = pl.pallas_call(
        functools.partial(_inproj_ctx_kernel, layer),
        grid=(T_CTX // tm,),
        in_specs=in_specs,
        out_specs=out_specs,
        out_shape=out_shape,
        input_output_aliases=aliases,
        compiler_params=_cparams("arbitrary"),
        name="inproj_ctx",
    )(x, g, shift[0], scale[0], w, gcq, gckv, wuq, *extra)
    return outs[:3], outs[3:]


def _inproj_lat_kernel(x_ref, g_ref, sh_ref, sc_ref, w_ref, gcq_ref, gckv_ref, wuq_ref,
                       cosa_ref, sina_ref, cosq_ref, sinq_ref, cosr_ref, sinr_ref,
                       qa_ref, ka_ref, va_ref, qb_ref, kb_ref, vb_ref, qc_ref, ckv_ref, kr_ref):
    h = _rms(x_ref[...], g_ref[...]) * (1.0 + sc_ref[0]) + sh_ref[0]
    p = _dot(h.astype(bf16), w_ref[...])
    cosa = cosa_ref[...]
    sina = sina_ref[...]
    qa = p[:, OFF_QA:OFF_QA + W_QA] * cosa + p[:, OFF_QA_P:OFF_QA_P + W_QA] * sina
    ka = p[:, OFF_KA:OFF_KA + W_KA] * cosa[:, :W_KA] + p[:, OFF_KA_P:OFF_KA_P + W_KA] * sina[:, :W_KA]
    kr = p[:, OFF_KR:OFF_KR + QK_ROPE] * cosr_ref[...] + p[:, OFF_KR_P:OFF_KR_P + QK_ROPE] * sinr_ref[...]
    qa_ref[...] = qa.astype(bf16)
    ka_ref[...] = ka.astype(bf16)
    va_ref[...] = p[:, OFF_VA:OFF_VA + W_VA].astype(bf16)
    qb_ref[...] = p[:, OFF_QB:OFF_QB + W_B].astype(bf16)
    kb_ref[...] = p[:, OFF_KB:OFF_KB + W_B].astype(bf16)
    vb_ref[...] = p[:, OFF_VB:OFF_VB + W_B].astype(bf16)
    cqn = _rms(p[:, OFF_CQ:OFF_CQ + Q_LORA], gcq_ref[...])
    q2 = _dot(cqn.astype(bf16), wuq_ref[...])
    nq = H_C * QC_PAD
    qc_ref[...] = (q2[:, :nq] * cosq_ref[...] + q2[:, nq:] * sinq_ref[...]).astype(bf16)
    ckv_ref[...] = _rms(p[:, OFF_CKV:OFF_CKV + KV_LORA], gckv_ref[...]).astype(bf16)
    kr_ref[...] = kr.astype(bf16)


def _inproj_lat(layer, x, lat_row0, g, shift, scale, w, gcq, gckv, wuq2, tabs):
    tm = TM_LAT_IN
    per_b = DEC_SEQ // tm
    row0 = lat_row0 // tm
    xrow = lambda i: (row0 + i, 0)
    row = lambda i: (i, 0)
    const = lambda i: (0, 0)
    pos = lambda i: (i % per_b, 0)
    cosa, sina, cosq, sinq, cosr, sinr = tabs
    widths = (W_QA, W_KA, W_VA, W_B, W_B, W_B, H_C * QC_PAD, KV_LORA, QK_ROPE)
    return pl.pallas_call(
        _inproj_lat_kernel,
        grid=(T_LAT // tm,),
        in_specs=[pl.BlockSpec((tm, D_MODEL), xrow),
                  pl.BlockSpec((None, 1, D_MODEL), lambda i: (layer, 0, 0)),
                  _mod_spec(shift, lambda i: 1 + i // per_b),
                  _mod_spec(scale, lambda i: 1 + i // per_b),
                  pl.BlockSpec((None, D_MODEL, NW_LAT), lambda i: (layer, 0, 0)),
                  pl.BlockSpec((None, 1, Q_LORA), lambda i: (layer, 0, 0)),
                  pl.BlockSpec((None, 1, KV_LORA), lambda i: (layer, 0, 0)),
                  pl.BlockSpec((None, Q_LORA, 2 * H_C * QC_PAD), lambda i: (layer, 0, 0)),
                  pl.BlockSpec((tm, W_QA), pos), pl.BlockSpec((tm, W_QA), pos),
                  pl.BlockSpec((tm, H_C * QC_PAD), pos), pl.BlockSpec((tm, H_C * QC_PAD), pos),
                  pl.BlockSpec((tm, QK_ROPE), pos), pl.BlockSpec((tm, QK_ROPE), pos)],
        out_specs=[pl.BlockSpec((tm, wd), row) for wd in widths],
        out_shape=[jax.ShapeDtypeStruct((T_LAT, wd), bf16) for wd in widths],
        compiler_params=_cparams("arbitrary"),
        name="inproj_lat",
    )(x, g, shift[0], scale[0], w, gcq, gckv, wuq2, cosa, sina, cosq, sinq, cosr, sinr)


CTX_BATCHES = 2


def _ctx_attn_kernel(sink_ref, qa_ref, ka_ref, va_ref, qb_ref, kb_ref, vb_ref, qc_ref, ckv_ref, kr_ref,
                     wukv_ref, wout_ref, x_ref, gate_ref, o_ref, o_scr, s_scr, p_scr):
    n = SEQ
    scale = HEAD_DIM ** -0.5
    scale_c = (QK_NOPE + QK_ROPE) ** -0.5

    def one_batch(sb, carry):
        rows = pl.ds(pl.multiple_of(sb * n, n), n)
        ka = ka_ref[sb, 0].astype(bf16)
        va = va_ref[sb, 0].astype(bf16)
        kb = kb_ref[sb, 0].astype(bf16)
        vb = vb_ref[sb, 0].astype(bf16)
        kv = _dot(ckv_ref[sb, 0].astype(bf16), wukv_ref[...]).astype(bf16)
        kr = kr_ref[sb, 0].astype(bf16)
        for h in range(H_A):
            g = h // G_A
            q = qa_ref[rows, h * HEAD_DIM:(h + 1) * HEAD_DIM]
            s_scr[h * n:(h + 1) * n, :] = _dot_nt(q, ka[:, g * HEAD_DIM:(g + 1) * HEAD_DIM]) * scale
        for h in range(H_B):
            sl = slice(h * HEAD_DIM, (h + 1) * HEAD_DIM)
            s_scr[(H_A + h) * n:(H_A + h + 1) * n, :] = _dot_nt(qb_ref[rows, sl], kb[:, sl]) * scale
        for h in range(H_C):
            qn = qc_ref[rows, h * QC_PAD:h * QC_PAD + QK_NOPE]
            qr = qc_ref[rows, h * QC_PAD + QK_NOPE:h * QC_PAD + QK_NOPE + QK_ROPE]
            c0 = h * (QK_NOPE + V_C)
            r0 = (H_A + H_B + h) * n
            s_scr[r0:r0 + n, :] = (_dot_nt(qn, kv[:, c0:c0 + QK_NOPE]) + _dot_nt(qr, kr)) * scale_c
        for pair in range((H_A + H_B + H_C) // 2):
            h0 = 2 * pair
            sinks = ((sink_ref[h0], n), (sink_ref[h0 + 1], n)) if h0 < H_A else None
            _softmax_rows(s_scr, p_scr, slice(h0 * n, (h0 + 2) * n), sinks)
        for h in range(H_A):
            g = h // G_A
            o_scr[:, h * HEAD_DIM:(h + 1) * HEAD_DIM] = _dot(p_scr[h * n:(h + 1) * n, :],
                                                             va[:, g * HEAD_DIM:(g + 1) * HEAD_DIM])
        for h in range(H_B):
            sl = slice(h * HEAD_DIM, (h + 1) * HEAD_DIM)
            o_scr[:, W_QA + h * HEAD_DIM:W_QA + (h + 1) * HEAD_DIM] = _dot(
                p_scr[(H_A + h) * n:(H_A + h + 1) * n, :], vb[:, sl])
        for h in range(H_C):
            c0 = h * (QK_NOPE + V_C)
            r0 = (H_A + H_B + h) * n
            off = W_QA + W_B + h * V_C
            o_scr[:, off:off + V_C] = _dot(p_scr[r0:r0 + n, :], kv[:, c0 + QK_NOPE:c0 + QK_NOPE + V_C])
        y = _dot(o_scr[...].astype(bf16), wout_ref[...])
        o_ref[rows, :] = x_ref[rows, :] + gate_ref[0] * y
        return carry

    lax.fori_loop(0, CTX_BATCHES, one_batch, 0)


def _ctx_attn(layer, sink, qs, caches, wukv, wout, x, gate):
    qa, qb, qc = qs
    ka, va, kb, vb, ckv, kr = caches
    row = lambda b: (b, 0)
    const = lambda b: (0, 0)
    slot = lambda b: (b, layer, 0, 0)
    nrow = CTX_BATCHES * SEQ
    qspec = lambda a: pl.BlockSpec((nrow, a.shape[1]), row)
    cspec = lambda a: pl.BlockSpec((CTX_BATCHES, 1, SEQ, a.shape[3]), slot)
    in_specs = [pl.BlockSpec(memory_space=pltpu.SMEM),
                qspec(qa), cspec(ka), cspec(va), qspec(qb), cspec(kb), cspec(vb), qspec(qc), cspec(ckv), cspec(kr)]
    in_specs += [pl.BlockSpec((None, KV_LORA, H_C * (QK_NOPE + V_C)), lambda b: (layer, 0, 0)),
                 pl.BlockSpec((None, D_MODEL, D_MODEL), lambda b: (layer, 0, 0)),
                 pl.BlockSpec((nrow, D_MODEL), row),
                 _mod_spec(gate, lambda b: 0)]
    return pl.pallas_call(
        _ctx_attn_kernel,
        grid=(BATCH // CTX_BATCHES,),
        in_specs=in_specs,
        out_specs=pl.BlockSpec((nrow, D_MODEL), row),
        out_shape=jax.ShapeDtypeStruct((T_CTX, D_MODEL), f32),
        scratch_shapes=[pltpu.VMEM((SEQ, D_MODEL), f32),
                        pltpu.VMEM(((H_A + H_B + H_C) * SEQ, SEQ), f32),
                        pltpu.VMEM(((H_A + H_B + H_C) * SEQ, SEQ), bf16)],
        compiler_params=_cparams("arbitrary"),
        name="ctx_attn",
    )(sink, qa, ka, va, qb, kb, vb, qc, ckv, kr, wukv, wout, x, gate[0])


def _lat_attn_kernel(sink_ref, qa_ref, qb_ref, qc_ref, ka_ref, va_ref, kb_ref, vb_ref, ckv_ref, kr_ref,
                     cak_ref, cav_ref, cbk_ref, cbv_ref, cckv_ref, ckr_ref, bias_ref,
                     wukv_ref, wout_ref, x_ref, gate_ref, o_ref, o_scr, kv_scr, sa, pa, sb, pb, sc, pc):
    qi = pl.program_id(1)
    nb = DEC_SEQ // BLOCK
    scale = HEAD_DIM ** -0.5

    @pl.when(qi == 0)
    def _():
        kv_scr[0:DEC_SEQ, :] = _dot(ckv_ref[...], wukv_ref[...]).astype(bf16)
        kv_scr[DEC_SEQ:DEC_SEQ + PAST_LEN, :] = _dot(cckv_ref[0, 0].astype(bf16), wukv_ref[...]).astype(bf16)

    def blk(ref, j):
        idx = jnp.clip(qi + j, 0, nb - 1)
        return ref[pl.ds(pl.multiple_of(idx * BLOCK, BLOCK), BLOCK), :]

    ka = jnp.concatenate([blk(ka_ref, -1), blk(ka_ref, 0), blk(ka_ref, 1), cak_ref[0, 0].astype(bf16)], axis=0)
    va = jnp.concatenate([blk(va_ref, -1), blk(va_ref, 0), blk(va_ref, 1), cav_ref[0, 0].astype(bf16)], axis=0)
    nk_a = 3 * BLOCK + PAST_LEN
    r = lax.broadcasted_iota(jnp.int32, (BLOCK, nk_a), 0)
    c = lax.broadcasted_iota(jnp.int32, (BLOCK, nk_a), 1)
    valid = (((c < BLOCK) & (c >= r) & (qi > 0))
             | ((c >= BLOCK) & (c < 2 * BLOCK))
             | ((c >= 2 * BLOCK) & (c < 3 * BLOCK) & (c - 2 * BLOCK <= r) & (qi < nb - 1))
             | (c >= 3 * BLOCK))
    for h in range(H_A):
        g = h // G_A
        q = qa_ref[:, h * HEAD_DIM:(h + 1) * HEAD_DIM]
        s = _dot_nt(q, ka[:, g * HEAD_DIM:(g + 1) * HEAD_DIM]) * scale
        sa[h * BLOCK:(h + 1) * BLOCK, :] = jnp.where(valid, s, NEG)

    cbk = cbk_ref[0, 0].astype(bf16)
    cbv = cbv_ref[0, 0].astype(bf16)
    rows_per_blk = BLOCK // GRID_W
    nloc = NA_ROWS * GRID_W
    vcats = []
    for half in range(rows_per_blk):
        grow = qi * rows_per_blk + half
        start = jnp.clip(grow - NA_ROWS // 2, 0, ROWS - NA_ROWS)
        kloc = kb_ref[pl.ds(pl.multiple_of(start * GRID_W, GRID_W), nloc), :]
        vloc = vb_ref[pl.ds(pl.multiple_of(start * GRID_W, GRID_W), nloc), :]
        vcats.append(jnp.concatenate([vloc, cbv], axis=0))
        qrows = slice(half * GRID_W, (half + 1) * GRID_W)
        dr0 = start - grow + (NA_ROWS - 1)
        for h in range(H_B):
            sl = slice(h * HEAD_DIM, (h + 1) * HEAD_DIM)
            q = qb_ref[qrows, sl]
            bias = jnp.concatenate([bias_ref[h, dr0 + 2 * j] for j in range(NA_ROWS // 2)], axis=1)
            s_loc = _dot_nt(q, kloc[:, sl]) * scale + bias
            s_ctx = _dot_nt(q, cbk[:, sl]) * scale
            r0 = (half * H_B + h) * GRID_W
            sb[r0:r0 + GRID_W, :] = jnp.concatenate([s_loc, s_ctx], axis=1)

    kr = jnp.concatenate([kr_ref[...], ckr_ref[0, 0].astype(bf16)], axis=0)
    scale_c = (QK_NOPE + QK_ROPE) ** -0.5
    for h in range(H_C):
        qn = qc_ref[:, h * QC_PAD:h * QC_PAD + QK_NOPE]
        qr = qc_ref[:, h * QC_PAD + QK_NOPE:h * QC_PAD + QK_NOPE + QK_ROPE]
        c0 = h * (QK_NOPE + V_C)
        sc[h * BLOCK:(h + 1) * BLOCK, :] = (_dot_nt(qn, kv_scr[:, c0:c0 + QK_NOPE]) + _dot_nt(qr, kr)) * scale_c

    for pair in range(H_A // 2):
        h0 = 2 * pair
        _softmax_rows(sa, pa, slice(h0 * BLOCK, (h0 + 2) * BLOCK), ((sink_ref[h0], BLOCK), (sink_ref[h0 + 1], BLOCK)))
    for blk2 in range(rows_per_blk * H_B // 2):
        _softmax_rows(sb, pb, slice(blk2 * 2 * GRID_W, (blk2 + 1) * 2 * GRID_W))
    for h in range(H_C):
        _softmax_rows(sc, pc, slice(h * BLOCK, (h + 1) * BLOCK))

    for h in range(H_A):
        g = h // G_A
        o_scr[:, h * HEAD_DIM:(h + 1) * HEAD_DIM] = _dot(pa[h * BLOCK:(h + 1) * BLOCK, :],
                                                         va[:, g * HEAD_DIM:(g + 1) * HEAD_DIM])
    for half in range(rows_per_blk):
        qrows = slice(half * GRID_W, (half + 1) * GRID_W)
        for h in range(H_B):
            sl = slice(h * HEAD_DIM, (h + 1) * HEAD_DIM)
            r0 = (half * H_B + h) * GRID_W
            o_scr[qrows, W_QA + h * HEAD_DIM:W_QA + (h + 1) * HEAD_DIM] = _dot(pb[r0:r0 + GRID_W, :],
                                                                             vcats[half][:, sl])
    for h in range(H_C):
        c0 = h * (QK_NOPE + V_C)
        off = W_QA + W_B + h * V_C
        o_scr[:, off:off + V_C] = _dot(pc[h * BLOCK:(h + 1) * BLOCK, :], kv_scr[:, c0 + QK_NOPE:c0 + QK_NOPE + V_C])

    y = _dot(o_scr[...].astype(bf16), wout_ref[...])
    o_ref[...] = x_ref[...] + gate_ref[0] * y


def _lat_attn(layer, sink, proj, caches, bias_tab, wukv, wout, x, lat_row0, gate):
    qa, ka, va, qb, kb, vb, qc, ckv, kr = proj
    nb = DEC_SEQ // BLOCK
    qrow = lambda b, q: (b * nb + q, 0)
    xrow = lambda b, q: (lat_row0 // BLOCK + b * nb + q, 0)
    brow = lambda b, q: (b, 0)
    const = lambda b, q: (0, 0)
    cidx = lambda b, q: (b, layer, 0, 0)
    in_specs = [pl.BlockSpec(memory_space=pltpu.SMEM)]
    in_specs += [pl.BlockSpec((BLOCK, a.shape[1]), qrow) for a in (qa, qb, qc)]
    in_specs += [pl.BlockSpec((DEC_SEQ, a.shape[1]), brow) for a in (ka, va, kb, vb, ckv, kr)]
    in_specs += [pl.BlockSpec((1, 1, PAST_LEN, a.shape[3]), cidx) for a in caches]
    in_specs += [pl.BlockSpec((None,) + bias_tab.shape[1:], lambda b, q: (layer, 0, 0, 0, 0)),
                 pl.BlockSpec((None, KV_LORA, H_C * (QK_NOPE + V_C)), lambda b, q: (layer, 0, 0)),
                 pl.BlockSpec((None, D_MODEL, D_MODEL), lambda b, q: (layer, 0, 0)),
                 pl.BlockSpec((BLOCK, D_MODEL), xrow),
                 _mod_spec(gate, lambda b, q: 1 + b)]
    return pl.pallas_call(
        _lat_attn_kernel,
        grid=(DEC_BATCH, nb),
        in_specs=in_specs,
        out_specs=pl.BlockSpec((BLOCK, D_MODEL), qrow),
        out_shape=jax.ShapeDtypeStruct((T_LAT, D_MODEL), f32),
        scratch_shapes=[pltpu.VMEM((BLOCK, D_MODEL), f32),
                        pltpu.VMEM((DEC_SEQ + PAST_LEN, H_C * (QK_NOPE + V_C)), bf16)]
        + [pltpu.VMEM(shape, dt) for shape in ((H_A * BLOCK, 3 * BLOCK + PAST_LEN),
                                               (H_B * BLOCK, NA_ROWS * GRID_W + PAST_LEN),
                                               (H_C * BLOCK, DEC_SEQ + PAST_LEN)) for dt in (f32, bf16)],
        compiler_params=_cparams("arbitrary", "arbitrary"),
        name="lat_attn",
    )(sink, qa, qb, qc, ka, va, kb, vb, ckv, kr, *caches, bias_tab, wukv, wout, x, gate[0])


def _pick_stream(xc_ref, xl_ref, x_scr):
    i = pl.program_id(0)

    @pl.when(i < T_CTX // TM_TOK)
    def _():
        x_scr[...] = xc_ref[...]

    @pl.when(i >= T_CTX // TM_TOK)
    def _():
        x_scr[...] = xl_ref[...]

    return x_scr[...]


def _stream_specs(lat_row0):
    n_ctx = T_CTX // TM_TOK
    return [pl.BlockSpec((TM_TOK, D_MODEL), lambda i: (jnp.minimum(i, n_ctx - 1), 0)),
            pl.BlockSpec((TM_TOK, D_MODEL), lambda i: (lat_row0 // TM_TOK + jnp.maximum(i - n_ctx, 0), 0))]


def _router_kernel(xc_ref, xl_ref, g_ref, sh_ref, sc_ref, wr_ref, br_ref, h_ref, e_ref, gt_ref, x_scr):
    h = _rms(_pick_stream(xc_ref, xl_ref, x_scr), g_ref[...]) * (1.0 + sc_ref[0]) + sh_ref[0]
    _store_row_tiles(h_ref, h)
    h_hi = h.astype(bf16)
    h_lo = (h - h_hi.astype(f32)).astype(bf16)
    w = wr_ref[...]
    w_hi = w.astype(bf16)
    w_lo = (w - w_hi.astype(f32)).astype(bf16)
    logits = _dot(h_hi, w_hi) + _dot(h_hi, w_lo) + _dot(h_lo, w_hi) + br_ref[...]
    lane = lax.broadcasted_iota(jnp.int32, logits.shape, 1).astype(f32)
    l = jnp.where(lane < N_EXPERTS, logits, -jnp.inf)
    tops, idxs = [], []
    for _ in range(TOP_K):
        m = jnp.max(l, axis=-1, keepdims=True)
        idx = jnp.min(jnp.where(l == m, lane, float(LANE)), axis=-1, keepdims=True)
        tops.append(m)
        idxs.append(idx)
        l = jnp.where(lane == idx, -jnp.inf, l)
    ex = [jnp.exp(t - tops[0]) for t in tops]
    den = ex[0] + ex[1] + ex[2] + ex[3]
    e_out = jnp.zeros(logits.shape, f32)
    g_out = jnp.zeros(logits.shape, f32)
    for k in range(TOP_K):
        e_out = jnp.where(lane == k, idxs[k], e_out)
        g_out = jnp.where(lane == k, ex[k] / den, g_out)
    e_ref[...] = e_out.astype(jnp.int32)
    gt_ref[...] = g_out


def _group_of_tile(i):
    per_b = DEC_SEQ // TM_TOK
    n_ctx = T_CTX // TM_TOK
    return jnp.where(i < n_ctx, 0, 1 + (i - n_ctx) // per_b)


def _router(layer, xc, xl, lat_row0, g, shift, scale, wr, br):
    tm = TM_TOK
    row = lambda i: (i, 0)
    const = lambda i: (0, 0)
    return pl.pallas_call(
        _router_kernel,
        grid=(T_ALL // tm,),
        in_specs=_stream_specs(lat_row0) +
                 [pl.BlockSpec((None, 1, D_MODEL), lambda i: (layer, 0, 0)),
                  _mod_spec(shift, _group_of_tile),
                  _mod_spec(scale, _group_of_tile),
                  pl.BlockSpec((None, D_MODEL, LANE), lambda i: (layer, 0, 0)),
                  pl.BlockSpec((None, 1, LANE), lambda i: (layer, 0, 0))],
        out_specs=[pl.BlockSpec((tm * ROW_TILE, LANE), row), pl.BlockSpec((tm, LANE), row),
                   pl.BlockSpec((tm, LANE), row)],
        out_shape=[jax.ShapeDtypeStruct((T_ALL * ROW_TILE, LANE), f32),
                   jax.ShapeDtypeStruct((T_ALL, LANE), jnp.int32),
                   jax.ShapeDtypeStruct((T_ALL, LANE), f32)],
        scratch_shapes=[pltpu.VMEM((tm, D_MODEL), f32)],
        compiler_params=_cparams("arbitrary"),
        name="router",
    )(xc, xl, g, shift[0], scale[0], wr, br)


def _dispatch_kernel(tok_ref, nu_ref, h_hbm, o_ref, hv, xg, hsem):
    tm = TM_MOE
    i = pl.program_id(0)

    @pl.when(i == 0)
    def _():
        resident = pltpu.make_async_copy(h_hbm, hv, hsem.at[0])
        resident.start()
        resident.wait()

    def one_block(sub, carry):
        blk = i * DISPATCH_BLOCKS + sub
        rows = pl.ds(pl.multiple_of(sub * tm, tm), tm)

        @pl.when(blk < nu_ref[0])
        def _():
            for r in range(tm):
                t = tok_ref[blk * tm + r]
                xg[pl.ds(r, ROW_TILE, stride=tm + 1), :] = hv[pl.ds(pl.multiple_of(t * ROW_TILE, ROW_TILE),
                                                                 ROW_TILE), :]
            o_ref[rows, :] = jnp.concatenate([xg[pl.ds(c * (tm + 1), tm), :] for c in range(ROW_TILE)],
                                             axis=1).astype(bf16)

        @pl.when(blk >= nu_ref[0])
        def _():
            o_ref[rows, :] = jnp.zeros((tm, D_MODEL), bf16)

        return carry

    lax.fori_loop(0, DISPATCH_BLOCKS, one_block, 0)


def _dispatch(row_tok, n_used, h):
    tm = TM_MOE
    return pl.pallas_call(
        _dispatch_kernel,
        grid_spec=pltpu.PrefetchScalarGridSpec(
            num_scalar_prefetch=2,
            grid=(N_MOE_BLOCKS // DISPATCH_BLOCKS,),
            in_specs=[pl.BlockSpec(memory_space=pl.ANY)],
            out_specs=pl.BlockSpec((DISPATCH_BLOCKS * tm, D_MODEL), lambda i, tok, nu: (i, 0)),
            scratch_shapes=[pltpu.VMEM((T_ALL * ROW_TILE, LANE), f32), pltpu.VMEM(((tm + 1) * ROW_TILE, LANE), f32),
                            pltpu.SemaphoreType.DMA((1,))]),
        out_shape=jax.ShapeDtypeStruct((N_MOE_BLOCKS * tm, D_MODEL), bf16),
        compiler_params=_cparams("arbitrary"),
        name="dispatch",
    )(row_tok, n_used, h)


def _moe_kernel(layer, be_ref, nu_ref, nxt_ref, dst_ref, x_ref, wgu_hbm, bgu_ref, wd_hbm, bd_ref, y_hbm,
                y0, y1, wgu_st, wd_st, wgu_bf, wd_bf, wsem, ssem):
    tm = TM_MOE
    i = pl.program_id(0)
    nb = pl.num_programs(0)
    used = i < nu_ref[0]
    yb = (y0, y1)

    def out_tile(row):
        return pl.ds(pl.multiple_of(row * ROW_TILE, ROW_TILE), ROW_TILE)

    def scatter_desc(buf, r, dst_row, s):
        return pltpu.make_async_copy(buf.at[out_tile(r)], y_hbm.at[out_tile(dst_row)], ssem.at[s])

    def scatter_wait(s):
        pltpu.make_async_copy(yb[s], y_hbm.at[pl.ds(0, tm * ROW_TILE)], ssem.at[s]).wait()

    def scatter_start(blk, s, unrolled):
        if unrolled:
            for r in range(tm):
                scatter_desc(yb[s], r, dst_ref[(blk + 1) * tm + r], s).start(priority=r % 2)
        else:
            def body(r, carry):
                scatter_desc(yb[s], r, dst_ref[(blk + 1) * tm + r], s).start()
                return carry
            lax.fori_loop(0, tm, body, 0, unroll=8)

    def weight_copies(e):
        return (pltpu.make_async_copy(wgu_hbm.at[layer, e], wgu_st, wsem.at[0]),
                pltpu.make_async_copy(wd_hbm.at[layer, e], wd_st, wsem.at[1]))

    @pl.when(i == 0)
    def _():
        for s in range(2):
            yb[s][...] = jnp.zeros_like(yb[s])
            dummy = pltpu.make_async_copy(yb[s], y_hbm.at[pl.ds((N_ASSIGN + s * tm) * ROW_TILE, tm * ROW_TILE)],
                                          ssem.at[s])
            dummy.start()
            dummy.wait()
        for cp in weight_copies(be_ref[0]):
            cp.start()

    first = jnp.logical_and(used, jnp.logical_or(i == 0, be_ref[i] != be_ref[jnp.maximum(i - 1, 0)]))

    @pl.when(first)
    def _():
        for cp in weight_copies(0):
            cp.wait()
        wgu_bf[...] = wgu_st[...].astype(bf16)
        wd_bf[...] = wd_st[...].astype(bf16)

        @pl.when(nxt_ref[i] >= 0)
        def _():
            for cp in weight_copies(nxt_ref[i]):
                cp.start()

    def step(par):
        cur, oth = par, 1 - par

        @pl.when(jnp.logical_and(i >= 1, i - 2 < nu_ref[0]))
        def _():
            scatter_wait(cur)

        @pl.when(used)
        def _():
            scatter_start(i - 1, oth, unrolled=True)
            gu = _dot(x_ref[...], wgu_bf[...]) + bgu_ref[0, 0]
            x_glu = jnp.minimum(gu[:, :D_FF], SWIGLU_LIMIT)
            x_lin = jnp.clip(gu[:, D_FF:], -SWIGLU_LIMIT, SWIGLU_LIMIT)
            act = x_glu * jax.nn.sigmoid(SWIGLU_ALPHA * x_glu) * (x_lin + 1.0)
            _store_row_tiles(yb[cur], _dot(act.astype(bf16), wd_bf[...]) + bd_ref[0, 0])

        flush = jnp.logical_and(jnp.logical_not(used), i - 1 < nu_ref[0])

        @pl.when(flush)
        def _():
            scatter_start(i - 1, oth, unrolled=False)

        @pl.when(jnp.logical_and(flush, i == nb - 1))
        def _():
            scatter_wait(oth)

    @pl.when(i % 2 == 0)
    def _():
        step(0)

    @pl.when(i % 2 == 1)
    def _():
        step(1)


def _moe(layer, routing, h, w_gu, b_gu, w_down, b_down):
    tm = TM_MOE
    block_e, n_used, nxt_e, row_tok, row_dst = routing
    xs = _dispatch(row_tok, n_used, h)
    ex4 = lambda i, be, nu, nxt, dst: (layer, be[i], 0, 0)
    return pl.pallas_call(
        functools.partial(_moe_kernel, layer),
        grid_spec=pltpu.PrefetchScalarGridSpec(
            num_scalar_prefetch=4,
            grid=(N_MOE_BLOCKS,),
            in_specs=[pl.BlockSpec((tm, D_MODEL), lambda i, be, nu, nxt, dst: (i, 0)),
                      pl.BlockSpec(memory_space=pl.ANY),
                      pl.BlockSpec((1, 1, 1, 2 * D_FF), ex4),
                      pl.BlockSpec(memory_space=pl.ANY),
                      pl.BlockSpec((1, 1, 1, D_MODEL), ex4)],
            out_specs=pl.BlockSpec(memory_space=pl.ANY),
            scratch_shapes=[pltpu.VMEM((tm * ROW_TILE, LANE), f32), pltpu.VMEM((tm * ROW_TILE, LANE), f32),
                            pltpu.VMEM((D_MODEL, 2 * D_FF), f32), pltpu.VMEM((D_FF, D_MODEL), f32),
                            pltpu.VMEM((D_MODEL, 2 * D_FF), bf16), pltpu.VMEM((D_FF, D_MODEL), bf16),
                            pltpu.SemaphoreType.DMA((2,)), pltpu.SemaphoreType.DMA((2,))]),
        out_shape=jax.ShapeDtypeStruct(((N_ASSIGN + 2 * tm) * ROW_TILE, LANE), f32),
        compiler_params=_cparams("arbitrary"),
        name="moe",
    )(block_e, n_used, nxt_e, row_dst, xs, w_gu, b_gu.reshape(DEPTH, N_EXPERTS, 1, 2 * D_FF),
      w_down, b_down.reshape(DEPTH, N_EXPERTS, 1, D_MODEL))


def _combine_kernel(final, xc_ref, xl_ref, y0_ref, y1_ref, y2_ref, y3_ref, gt_ref, gate_ref, gf_ref, *rest):
    x_scr = rest[-1]
    gt = gt_ref[...]
    f = gt[:, 0:1] * _load_row_tiles(y0_ref)
    for k, y_ref in ((1, y1_ref), (2, y2_ref), (3, y3_ref)):
        f = f + gt[:, k:k + 1] * _load_row_tiles(y_ref)
    out = _pick_stream(xc_ref, xl_ref, x_scr) + gate_ref[0] * f
    if not final:
        rest[0][...] = out
        return
    out = _rms(out, gf_ref[...])
    oc_ref, ol_ref = rest[0], rest[1]
    i = pl.program_id(0)

    @pl.when(i < T_CTX // TM_TOK)
    def _():
        oc_ref[...] = out

    @pl.when(i >= T_CTX // TM_TOK)
    def _():
        ol_ref[...] = out


def _combine(final, xc, xl, lat_row0, y, gates, gate, g_final):
    tm = TM_TOK
    nt = T_ALL // tm
    n_ctx = T_CTX // tm
    row = lambda i: (i, 0)
    const = lambda i: (0, 0)
    ysel = [pl.BlockSpec((tm * ROW_TILE, LANE), functools.partial(lambda k, i: (k * nt + i, 0), k))
            for k in range(TOP_K)]
    if final:
        out_specs = [pl.BlockSpec((tm, D_MODEL), lambda i: (jnp.minimum(i, n_ctx - 1), 0)),
                     pl.BlockSpec((tm, D_MODEL), lambda i: (jnp.maximum(i - n_ctx, 0), 0))]
        out_shape = [jax.ShapeDtypeStruct((T_CTX, D_MODEL), f32), jax.ShapeDtypeStruct((T_LAT, D_MODEL), f32)]
    else:
        out_specs = pl.BlockSpec((tm, D_MODEL), row)
        out_shape = jax.ShapeDtypeStruct((T_ALL, D_MODEL), f32)
    return pl.pallas_call(
        functools.partial(_combine_kernel, final),
        grid=(nt,),
        in_specs=_stream_specs(lat_row0) + ysel +
                 [pl.BlockSpec((tm, LANE), row),
                  _mod_spec(gate, _group_of_tile),
                  pl.BlockSpec((1, D_MODEL), const)],
        out_specs=out_specs,
        out_shape=out_shape,
        scratch_shapes=[pltpu.VMEM((tm, D_MODEL), f32)],
        compiler_params=_cparams("arbitrary"),
        name="combine",
    )(xc, xl, y, y, y, y, gates, gate[0], g_final)


def _rope_head_tables(d):
    nf = d // 4
    half = d // 2
    t = np.arange(DEC_SEQ)
    inv = ROPE_BASE ** (-np.arange(nf, dtype=np.float32) / nf)
    i = np.arange(d)
    pos = np.where(i[None, :] < half, (t // GRID_W)[:, None], (t % GRID_W)[:, None]).astype(np.float32)
    ang = pos * inv[i % nf][None, :].astype(np.float32)
    first = (i % half) < nf
    cos = np.cos(ang)
    sin = np.where(first[None, :], -np.sin(ang), np.sin(ang))
    partner = np.where(first, i + nf, i - nf)
    return cos.astype(np.float32), sin.astype(np.float32), partner


def _rope_tables():
    cos64, sin64, _ = _rope_head_tables(HEAD_DIM)
    cos32, sin32, _ = _rope_head_tables(QK_ROPE)
    cosa = np.tile(cos64, (1, H_A))
    sina = np.tile(sin64, (1, H_A))
    cosq1 = np.concatenate([np.ones((DEC_SEQ, QK_NOPE), np.float32), cos32,
                            np.ones((DEC_SEQ, QC_PAD - QK_NOPE - QK_ROPE), np.float32)], axis=1)
    sinq1 = np.concatenate([np.zeros((DEC_SEQ, QK_NOPE), np.float32), sin32,
                            np.zeros((DEC_SEQ, QC_PAD - QK_NOPE - QK_ROPE), np.float32)], axis=1)
    cosq = np.tile(cosq1, (1, H_C))
    sinq = np.tile(sinq1, (1, H_C))
    return tuple(jnp.asarray(a) for a in (cosa, sina, cosq, sinq, cos32, sin32))


def _pad_cols(w, n):
    return jnp.pad(w, ((0, 0), (0, n - w.shape[1])))


def _layer_weights(w_in, w_uq):
    cuts = np.cumsum((W_QA, W_KA, W_VA, W_B, W_B, W_B, Q_LORA, KV_LORA, QK_ROPE))[:-1]
    qa, ka, va, qb, kb, vb, cq, ckv, kr = jnp.split(w_in, [int(c) for c in cuts], axis=1)
    _, _, p64 = _rope_head_tables(HEAD_DIM)
    _, _, p32 = _rope_head_tables(QK_ROPE)
    pa = np.concatenate([h * HEAD_DIM + p64 for h in range(H_A)])
    base = jnp.concatenate([qa, ka, va, _pad_cols(qb, 384), _pad_cols(kb, 384), _pad_cols(vb, 384), cq, ckv,
                            _pad_cols(kr, 128)], axis=1)
    w_ctx = base.astype(bf16)
    w_lat = jnp.concatenate([base, qa[:, pa], ka[:, pa[:W_KA]], _pad_cols(kr[:, p32], 128)], axis=1).astype(bf16)
    hq = QK_NOPE + QK_ROPE
    heads = [_pad_cols(w_uq[:, h * hq:(h + 1) * hq], QC_PAD) for h in range(H_C)]
    pq = np.concatenate([np.arange(QK_NOPE), QK_NOPE + p32])
    heads_p = [_pad_cols(w_uq[:, h * hq:(h + 1) * hq][:, pq], QC_PAD) for h in range(H_C)]
    wuq = jnp.concatenate(heads, axis=1).astype(bf16)
    wuq2 = jnp.concatenate(heads + heads_p, axis=1).astype(bf16)
    return w_ctx, w_lat, wuq, wuq2


def _bias_table(rpb):
    col = np.arange(GRID_W)
    col_start = np.clip(col - NA_COLS // 2, 0, GRID_W - NA_COLS)
    col_ok = (col[None, :] >= col_start[:, None]) & (col[None, :] < col_start[:, None] + NA_COLS)
    dc = np.clip(col[None, :] - col[:, None] + (NA_COLS - 1), 0, 2 * NA_COLS - 2)
    onehot = (dc[None] == np.arange(2 * NA_COLS - 1)[:, None, None]).astype(np.float32)
    expanded = jnp.einsum('hrd,dqk->hrqk', rpb.astype(f32), jnp.asarray(onehot), precision=lax.Precision.HIGHEST)
    blocks = jnp.where(col_ok[None, None], expanded, NEG)
    return jnp.concatenate([blocks[:, :-1], blocks[:, 1:]], axis=-1)


def _routing(top_e):
    tm = TM_MOE
    key_bits = 16
    pad_mark = (1 << key_bits) - 1
    flat_e = top_e.T.reshape(N_ASSIGN)
    experts = jnp.arange(N_EXPERTS, dtype=jnp.int32)
    counts = jnp.sum((flat_e[:, None] == experts[None, :]).astype(jnp.int32), axis=0)
    nblk = (counts + tm - 1) // tm
    blk_end = jnp.cumsum(nblk)
    pad_end = jnp.cumsum(nblk * tm - counts)
    slots = jnp.arange(N_MOE_BLOCKS * tm - N_ASSIGN, dtype=jnp.int32)
    pad_e = jnp.sum((pad_end[None, :] <= slots[:, None]).astype(jnp.int32), axis=1)
    keys = jnp.concatenate([(flat_e << key_bits) + jnp.arange(N_ASSIGN, dtype=jnp.int32),
                            (pad_e << key_bits) + pad_mark])
    asg = (jnp.sort(keys, stable=False) & pad_mark).reshape(N_MOE_BLOCKS, tm)
    valid = asg != pad_mark
    blocks = jnp.arange(N_MOE_BLOCKS, dtype=jnp.int32)
    r = jnp.arange(tm, dtype=jnp.int32)[None, :]
    tok = jnp.where(valid, asg % T_ALL, 0)
    row_dst = jnp.where(valid, asg, N_ASSIGN + (blocks[:, None] % 2) * tm + r)
    row_dst = jnp.concatenate([N_ASSIGN + tm + r, row_dst], axis=0).reshape(-1)
    block_e = jnp.minimum(jnp.sum((blk_end[None, :] <= blocks[:, None]).astype(jnp.int32), axis=1), N_EXPERTS - 1)
    n_used = blk_end[-1].astype(jnp.int32).reshape(1)
    has = jnp.where(counts > 0, experts, N_EXPERTS)
    later = experts[None, :] > experts[:, None]
    nxt = jnp.min(jnp.where(later, has[None, :], N_EXPERTS), axis=1)
    nxt = jnp.where(nxt >= N_EXPERTS, -1, nxt)
    sel = (block_e[:, None] == experts[None, :]).astype(jnp.int32)
    nxt_e = jnp.sum(sel * nxt[None, :], axis=1)
    i32 = lambda a: a.astype(jnp.int32)
    return i32(block_e), n_used, i32(nxt_e), i32(tok).reshape(-1), i32(row_dst)


def kernel(x_prompt, x_sample, cache_a_k, cache_a_v, cache_b_k, cache_b_v, cache_c_kv, cache_c_kr, c, c_ctx, w_ada, b_ada, g_attn, g_ffn, w_in, sink_a, rpb_b, g_cq, g_ckv, w_uq, w_ukv, w_out, w_router, b_router, w_gu, b_gu, w_down, b_down, g_final):
    xc, xl, lat_row0 = x_prompt.reshape(T_CTX, D_MODEL), x_sample.reshape(T_LAT, D_MODEL), 0
    cvec = jnp.concatenate([c_ctx[None, :], c, jnp.zeros((8 - N_GROUPS, D_MODEL), f32)], axis=0)
    mods = _ada(cvec, w_ada, b_ada)[:, :N_GROUPS].reshape(DEPTH, N_GROUPS, 6, 1, D_MODEL)
    tabs = _rope_tables()
    caches = (cache_a_k.reshape(DEC_BATCH, DEPTH, PAST_LEN, W_KA), cache_a_v.reshape(DEC_BATCH, DEPTH, PAST_LEN, W_VA),
              cache_b_k.reshape(DEC_BATCH, DEPTH, PAST_LEN, W_B), cache_b_v.reshape(DEC_BATCH, DEPTH, PAST_LEN, W_B),
              cache_c_kv, cache_c_kr)
    new = None
    w_ctx, w_lat, wuq, wuq2 = jax.vmap(_layer_weights)(w_in, w_uq)
    wukv = w_ukv.astype(bf16)
    wout = w_out.astype(bf16)
    bias_tab = jax.vmap(_bias_table)(rpb_b)
    wr = jnp.pad(w_router, ((0, 0), (0, 0), (0, LANE - N_EXPERTS)))
    br = jnp.pad(b_router, ((0, 0), (0, LANE - N_EXPERTS)))[:, None, :]
    for layer in range(DEPTH):
        m = [(mods, layer, j) for j in range(6)]
        g1 = g_attn[:, None, :]
        gcq = g_cq[:, None, :]
        gckv = g_ckv[:, None, :]
        sink = sink_a[layer]

        qs, new = _inproj_ctx(layer, new, xc, g1, m[0], m[1], w_ctx, gcq, gckv, wuq)
        x_ctx = _ctx_attn(layer, sink, qs, new, wukv, wout, xc, m[2])

        plat = _inproj_lat(layer, xl, lat_row0, g1, m[0], m[1], w_lat, gcq, gckv, wuq2, tabs)
        x_lat = _lat_attn(layer, sink, plat, caches, bias_tab, wukv, wout, xl, lat_row0, m[2])

        h2, top_e, gates = _router(layer, x_ctx, x_lat, 0, g_ffn[:, None, :], m[3], m[4], wr, br)
        y = _moe(layer, _routing(top_e[:, :TOP_K]), h2, w_gu, b_gu, w_down, b_down)
        x = _combine(layer == DEPTH - 1, x_ctx, x_lat, 0, y, gates, m[5], g_final[None, :])
        xc, xl, lat_row0 = x, x, T_CTX

    y_prompt = x[0].reshape(BATCH, SEQ, D_MODEL)
    y_sample = x[1].reshape(DEC_BATCH, DEC_SEQ, D_MODEL)
    shapes = ((KV_A, HEAD_DIM), (KV_A, HEAD_DIM), (H_B, HEAD_DIM), (H_B, HEAD_DIM), (KV_LORA,), (QK_ROPE,))
    outs = [a.reshape((BATCH, DEPTH, SEQ) + s) for a, s in zip(new, shapes)]
    return (y_prompt, y_sample, *outs)
```

```python
import functools

import numpy as np
import jax
import jax.numpy as jnp
from jax import lax
from jax.experimental import pallas as pl
from jax.experimental.pallas import tpu as pltpu

D_MODEL = 1024
BATCH = 32
SEQ = 256
DEPTH = 2
DEC_BATCH = 2
DEC_SEQ = 1024
PAST_LEN = 512
GRID_W = 64
HEAD_DIM = 64
H_A = 6
KV_A = 2
G_A = H_A // KV_A
WINDOW = 128
BLOCK = 128
H_B = 5
NA_ROWS = 8
NA_COLS = 16
H_C = 5
Q_LORA = 384
KV_LORA = 256
QK_NOPE = 64
QK_ROPE = 32
V_C = 64
N_EXPERTS = 32
TOP_K = 4
D_FF = 1024
SWIGLU_ALPHA = 1.702
SWIGLU_LIMIT = 7.0
ROPE_BASE = 10000.0
EPS = 1e-6
NEG = -1e30

T_CTX = BATCH * SEQ
T_LAT = DEC_BATCH * DEC_SEQ
T_ALL = T_CTX + T_LAT
N_GROUPS = 1 + DEC_BATCH
LANE = 128
QC_PAD = 128
ROWS = DEC_SEQ // GRID_W

W_QA, W_KA, W_VA = H_A * HEAD_DIM, KV_A * HEAD_DIM, KV_A * HEAD_DIM
W_B = H_B * HEAD_DIM
OFF_QA = 0
OFF_KA = 384
OFF_VA = 512
OFF_QB = 640
OFF_KB = 1024
OFF_VB = 1408
OFF_CQ = 1792
OFF_CKV = 2176
OFF_KR = 2432
NW_CTX = 2560
OFF_QA_P = 2560
OFF_KA_P = 2944
OFF_KR_P = 3072
NW_LAT = 3200

TM_TOK = 512
TM_LAT_IN = 512
TM_MOE = 256
N_ASSIGN = T_ALL * TOP_K
N_MOE_BLOCKS = N_ASSIGN // TM_MOE + N_EXPERTS
DISPATCH_BLOCKS = 8
VMEM_LIMIT = 56 * 1024 * 1024

f32 = jnp.float32
bf16 = jnp.bfloat16


def _cparams(*sem):
    return pltpu.CompilerParams(dimension_semantics=sem, vmem_limit_bytes=VMEM_LIMIT)


def _mod_spec(mod, group_of):
    _, layer, j = mod
    return pl.BlockSpec((None, 1, None, 1, D_MODEL), lambda *idx: (layer, group_of(*idx), j, 0, 0))


def _rms(xf, g):
    return xf * lax.rsqrt(jnp.mean(xf * xf, axis=-1, keepdims=True) + EPS) * g


def _dot(a, b):
    return jnp.dot(a, b, preferred_element_type=f32)


def _dot_nt(a, b):
    return lax.dot_general(a, b, (((1,), (1,)), ((), ())), preferred_element_type=f32)


ROW_TILE = D_MODEL // LANE


def _store_row_tiles(ref, val):
    n = val.shape[0]
    for c in range(ROW_TILE):
        ref[pl.ds(c, n, stride=ROW_TILE), :] = val[:, c * LANE:(c + 1) * LANE]


def _load_row_tiles(ref):
    n = ref.shape[0] // ROW_TILE
    return jnp.concatenate([ref[pl.ds(c, n, stride=ROW_TILE), :] for c in range(ROW_TILE)], axis=1)


def _softmax_rows(s_ref, p_ref, rows, sinks=None):
    s = s_ref[rows, :]
    m = jnp.max(s, axis=-1, keepdims=True)
    if sinks is not None:
        sink = jnp.concatenate([jnp.full((n, 1), v, f32) for v, n in sinks], axis=0)
        m = jnp.maximum(m, sink)
    p = jnp.exp(s - m)
    l = jnp.sum(p, axis=-1, keepdims=True)
    if sinks is not None:
        l = l + jnp.exp(sink - m)
    p_ref[rows, :] = (p * (1.0 / l)).astype(bf16)


def _ada_kernel(c_ref, w_ref, b_ref, o_ref):
    c = c_ref[...]
    s = c * jax.nn.sigmoid(c)
    s_hi = s.astype(bf16)
    s_lo = (s - s_hi.astype(f32)).astype(bf16)
    w = w_ref[0]
    w_hi = w.astype(bf16)
    w_lo = (w - w_hi.astype(f32)).astype(bf16)
    o_ref[0] = _dot(s_hi, w_hi) + _dot(s_hi, w_lo) + _dot(s_lo, w_hi) + b_ref[0]


def _ada(cvec, w_ada, b_ada):
    tn = 1536
    return pl.pallas_call(
        _ada_kernel,
        grid=(DEPTH, 6 * D_MODEL // tn),
        in_specs=[pl.BlockSpec((8, D_MODEL), lambda l, j: (0, 0)),
                  pl.BlockSpec((1, D_MODEL, tn), lambda l, j: (l, 0, j)),
                  pl.BlockSpec((1, 1, tn), lambda l, j: (l, 0, j))],
        out_specs=pl.BlockSpec((1, 8, tn), lambda l, j: (l, 0, j)),
        out_shape=jax.ShapeDtypeStruct((DEPTH, 8, 6 * D_MODEL), f32),
        compiler_params=_cparams("arbitrary", "arbitrary"),
        name="ada",
    )(cvec, w_ada, b_ada.reshape(DEPTH, 1, 6 * D_MODEL))


CACHE_WIDTHS = (W_KA, W_VA, W_B, W_B, KV_LORA, QK_ROPE)


def _inproj_ctx_kernel(layer, x_ref, g_ref, sh_ref, sc_ref, w_ref, gcq_ref, gckv_ref, wuq_ref, *refs):
    qa_ref, qb_ref, qc_ref, ka_ref, va_ref, kb_ref, vb_ref, ckv_ref, kr_ref = refs[-9:]
    h = _rms(x_ref[...], g_ref[...]) * (1.0 + sc_ref[0]) + sh_ref[0]
    p = _dot(h.astype(bf16), w_ref[...])
    qa_ref[...] = p[:, OFF_QA:OFF_QA + W_QA].astype(bf16)
    qb_ref[...] = p[:, OFF_QB:OFF_QB + W_B].astype(bf16)
    cqn = _rms(p[:, OFF_CQ:OFF_CQ + Q_LORA], gcq_ref[...])
    qc_ref[...] = _dot(cqn.astype(bf16), wuq_ref[...]).astype(bf16)
    caches = ((ka_ref, p[:, OFF_KA:OFF_KA + W_KA]), (va_ref, p[:, OFF_VA:OFF_VA + W_VA]),
              (kb_ref, p[:, OFF_KB:OFF_KB + W_B]), (vb_ref, p[:, OFF_VB:OFF_VB + W_B]),
              (ckv_ref, _rms(p[:, OFF_CKV:OFF_CKV + KV_LORA], gckv_ref[...])),
              (kr_ref, p[:, OFF_KR:OFF_KR + QK_ROPE]))
    for ref, val in caches:
        for b in range(TM_TOK // SEQ):
            rows = val[b * SEQ:(b + 1) * SEQ]
            if layer == 0:
                ref[b, 0] = rows
                for later in range(1, DEPTH):
                    ref[b, later] = jnp.zeros_like(rows)
            else:
                ref[b, 0] = rows


def _inproj_ctx(layer, prev_caches, x, g, shift, scale, w, gcq, gckv, wuq):
    tm = TM_TOK
    nb = tm // SEQ
    row = lambda i: (i, 0)
    const = lambda i: (0, 0)
    in_specs = [pl.BlockSpec((tm, D_MODEL), row),
                pl.BlockSpec((1, D_MODEL), const),
                _mod_spec(shift, lambda i: 0),
                _mod_spec(scale, lambda i: 0),
                pl.BlockSpec((None, D_MODEL, NW_CTX), lambda i: (layer, 0, 0)),
                pl.BlockSpec((1, Q_LORA), const),
                pl.BlockSpec((1, KV_LORA), const),
                pl.BlockSpec((None, Q_LORA, H_C * QC_PAD), lambda i: (layer, 0, 0))]
    q_widths = (W_QA, W_B, H_C * QC_PAD)
    out_specs = [pl.BlockSpec((tm, wd), row) for wd in q_widths]
    out_shape = [jax.ShapeDtypeStruct((T_CTX, wd), bf16) for wd in q_widths]
    if layer == 0:
        out_specs += [pl.BlockSpec((nb, DEPTH, SEQ, wd), lambda i: (i, 0, 0, 0)) for wd in CACHE_WIDTHS]
        aliases, extra = {}, ()
    else:
        in_specs += [pl.BlockSpec(memory_space=pl.ANY) for _ in CACHE_WIDTHS]
        out_specs += [pl.BlockSpec((nb, 1, SEQ, wd), lambda i: (i, layer, 0, 0)) for wd in CACHE_WIDTHS]
        aliases = {8 + j: len(q_widths) + j for j in range(len(CACHE_WIDTHS))}
        extra = tuple(prev_caches)
    out_shape += [jax.ShapeDtypeStruct((BATCH, DEPTH, SEQ, wd), f32) for wd in CACHE_WIDTHS]
    outs = pl.pallas_call(
        functools.partial(_inproj_ctx_kernel, layer),
        grid=(T_CTX // tm,),
        in_specs=in_specs,
        out_specs=out_specs,
        out_shape=out_shape,
        input_output_aliases=aliases,
        compiler_params=_cparams("arbitrary"),
        name="inproj_ctx",
    )(x, g, shift[0], scale[0], w, gcq, gckv, wuq, *extra)
    return outs[:3], outs[3:]


def _inproj_lat_kernel(x_ref, g_ref, sh_ref, sc_ref, w_ref, gcq_ref, gckv_ref, wuq_ref,
                       cosa_ref, sina_ref, cosq_ref, sinq_ref, cosr_ref, sinr_ref,
                       qa_ref, ka_ref, va_ref, qb_ref, kb_ref, vb_ref, qc_ref, ckv_ref, kr_ref):
    h = _rms(x_ref[...], g_ref[...]) * (1.0 + sc_ref[0]) + sh_ref[0]
    p = _dot(h.astype(bf16), w_ref[...])
    cosa = cosa_ref[...]
    sina = sina_ref[...]
    qa = p[:, OFF_QA:OFF_QA + W_QA] * cosa + p[:, OFF_QA_P:OFF_QA_P + W_QA] * sina
    ka = p[:, OFF_KA:OFF_KA + W_KA] * cosa[:, :W_KA] + p[:, OFF_KA_P:OFF_KA_P + W_KA] * sina[:, :W_KA]
    kr = p[:, OFF_KR:OFF_KR + QK_ROPE] * cosr_ref[...] + p[:, OFF_KR_P:OFF_KR_P + QK_ROPE] * sinr_ref[...]
    qa_ref[...] = qa.astype(bf16)
    ka_ref[...] = ka.astype(bf16)
    va_ref[...] = p[:, OFF_VA:OFF_VA + W_VA].astype(bf16)
    qb_ref[...] = p[:, OFF_QB:OFF_QB + W_B].astype(bf16)
    kb_ref[...] = p[:, OFF_KB:OFF_KB + W_B].astype(bf16)
    vb_ref[...] = p[:, OFF_VB:OFF_VB + W_B].astype(bf16)
    cqn = _rms(p[:, OFF_CQ:OFF_CQ + Q_LORA], gcq_ref[...])
    q2 = _dot(cqn.astype(bf16), wuq_ref[...])
    nq = H_C * QC_PAD
    qc_ref[...] = (q2[:, :nq] * cosq_ref[...] + q2[:, nq:] * sinq_ref[...]).astype(bf16)
    ckv_ref[...] = _rms(p[:, OFF_CKV:OFF_CKV + KV_LORA], gckv_ref[...]).astype(bf16)
    kr_ref[...] = kr.astype(bf16)


def _inproj_lat(layer, x, lat_row0, g, shift, scale, w, gcq, gckv, wuq2, tabs):
    tm = TM_LAT_IN
    per_b = DEC_SEQ // tm
    row0 = lat_row0 // tm
    xrow = lambda i: (row0 + i, 0)
    row = lambda i: (i, 0)
    const = lambda i: (0, 0)
    pos = lambda i: (i % per_b, 0)
    cosa, sina, cosq, sinq, cosr, sinr = tabs
    widths = (W_QA, W_KA, W_VA, W_B, W_B, W_B, H_C * QC_PAD, KV_LORA, QK_ROPE)
    return pl.pallas_call(
        _inproj_lat_kernel,
        grid=(T_LAT // tm,),
        in_specs=[pl.BlockSpec((tm, D_MODEL), xrow),
                  pl.BlockSpec((1, D_MODEL), const),
                  _mod_spec(shift, lambda i: 1 + i // per_b),
                  _mod_spec(scale, lambda i: 1 + i // per_b),
                  pl.BlockSpec((None, D_MODEL, NW_LAT), lambda i: (layer, 0, 0)),
                  pl.BlockSpec((1, Q_LORA), const),
                  pl.BlockSpec((1, KV_LORA), const),
                  pl.BlockSpec((None, Q_LORA, 2 * H_C * QC_PAD), lambda i: (layer, 0, 0)),
                  pl.BlockSpec((tm, W_QA), pos), pl.BlockSpec((tm, W_QA), pos),
                  pl.BlockSpec((tm, H_C * QC_PAD), pos), pl.BlockSpec((tm, H_C * QC_PAD), pos),
                  pl.BlockSpec((tm, QK_ROPE), pos), pl.BlockSpec((tm, QK_ROPE), pos)],
        out_specs=[pl.BlockSpec((tm, wd), row) for wd in widths],
        out_shape=[jax.ShapeDtypeStruct((T_LAT, wd), bf16) for wd in widths],
        compiler_params=_cparams("arbitrary"),
        name="inproj_lat",
    )(x, g, shift[0], scale[0], w, gcq, gckv, wuq2, cosa, sina, cosq, sinq, cosr, sinr)


CTX_BATCHES = 2


def _ctx_attn_kernel(sink_ref, qa_ref, ka_ref, va_ref, qb_ref, kb_ref, vb_ref, qc_ref, ckv_ref, kr_ref,
                     wukv_ref, wout_ref, x_ref, gate_ref, o_ref, o_scr, s_scr, p_scr):
    n = SEQ
    scale = HEAD_DIM ** -0.5
    scale_c = (QK_NOPE + QK_ROPE) ** -0.5

    def one_batch(sb, carry):
        rows = pl.ds(pl.multiple_of(sb * n, n), n)
        ka = ka_ref[sb, 0].astype(bf16)
        va = va_ref[sb, 0].astype(bf16)
        kb = kb_ref[sb, 0].astype(bf16)
        vb = vb_ref[sb, 0].astype(bf16)
        kv = _dot(ckv_ref[sb, 0].astype(bf16), wukv_ref[...]).astype(bf16)
        kr = kr_ref[sb, 0].astype(bf16)
        for h in range(H_A):
            g = h // G_A
            q = qa_ref[rows, h * HEAD_DIM:(h + 1) * HEAD_DIM]
            s_scr[h * n:(h + 1) * n, :] = _dot_nt(q, ka[:, g * HEAD_DIM:(g + 1) * HEAD_DIM]) * scale
        for h in range(H_B):
            sl = slice(h * HEAD_DIM, (h + 1) * HEAD_DIM)
            s_scr[(H_A + h) * n:(H_A + h + 1) * n, :] = _dot_nt(qb_ref[rows, sl], kb[:, sl]) * scale
        for h in range(H_C):
            qn = qc_ref[rows, h * QC_PAD:h * QC_PAD + QK_NOPE]
            qr = qc_ref[rows, h * QC_PAD + QK_NOPE:h * QC_PAD + QK_NOPE + QK_ROPE]
            c0 = h * (QK_NOPE + V_C)
            r0 = (H_A + H_B + h) * n
            s_scr[r0:r0 + n, :] = (_dot_nt(qn, kv[:, c0:c0 + QK_NOPE]) + _dot_nt(qr, kr)) * scale_c
        for pair in range((H_A + H_B + H_C) // 2):
            h0 = 2 * pair
            sinks = ((sink_ref[h0], n), (sink_ref[h0 + 1], n)) if h0 < H_A else None
            _softmax_rows(s_scr, p_scr, slice(h0 * n, (h0 + 2) * n), sinks)
        for h in range(H_A):
            g = h // G_A
            o_scr[:, h * HEAD_DIM:(h + 1) * HEAD_DIM] = _dot(p_scr[h * n:(h + 1) * n, :],
                                                             va[:, g * HEAD_DIM:(g + 1) * HEAD_DIM])
        for h in range(H_B):
            sl = slice(h * HEAD_DIM, (h + 1) * HEAD_DIM)
            o_scr[:, W_QA + h * HEAD_DIM:W_QA + (h + 1) * HEAD_DIM] = _dot(
                p_scr[(H_A + h) * n:(H_A + h + 1) * n, :], vb[:, sl])
        for h in range(H_C):
            c0 = h * (QK_NOPE + V_C)
            r0 = (H_A + H_B + h) * n
            off = W_QA + W_B + h * V_C
            o_scr[:, off:off + V_C] = _dot(p_scr[r0:r0 + n, :], kv[:, c0 + QK_NOPE:c0 + QK_NOPE + V_C])
        y = _dot(o_scr[...].astype(bf16), wout_ref[...])
        o_ref[rows, :] = x_ref[rows, :] + gate_ref[0] * y
        return carry

    lax.fori_loop(0, CTX_BATCHES, one_batch, 0)


def _ctx_attn(layer, sink, qs, caches, wukv, wout, x, gate):
    qa, qb, qc = qs
    ka, va, kb, vb, ckv, kr = caches
    row = lambda b: (b, 0)
    const = lambda b: (0, 0)
    slot = lambda b: (b, layer, 0, 0)
    nrow = CTX_BATCHES * SEQ
    qspec = lambda a: pl.BlockSpec((nrow, a.shape[1]), row)
    cspec = lambda a: pl.BlockSpec((CTX_BATCHES, 1, SEQ, a.shape[3]), slot)
    in_specs = [pl.BlockSpec(memory_space=pltpu.SMEM),
                qspec(qa), cspec(ka), cspec(va), qspec(qb), cspec(kb), cspec(vb), qspec(qc), cspec(ckv), cspec(kr)]
    in_specs += [pl.BlockSpec((None, KV_LORA, H_C * (QK_NOPE + V_C)), lambda b: (layer, 0, 0)),
                 pl.BlockSpec((None, D_MODEL, D_MODEL), lambda b: (layer, 0, 0)),
                 pl.BlockSpec((nrow, D_MODEL), row),
                 _mod_spec(gate, lambda b: 0)]
    return pl.pallas_call(
        _ctx_attn_kernel,
        grid=(BATCH // CTX_BATCHES,),
        in_specs=in_specs,
        out_specs=pl.BlockSpec((nrow, D_MODEL), row),
        out_shape=jax.ShapeDtypeStruct((T_CTX, D_MODEL), f32),
        scratch_shapes=[pltpu.VMEM((SEQ, D_MODEL), f32),
                        pltpu.VMEM(((H_A + H_B + H_C) * SEQ, SEQ), f32),
                        pltpu.VMEM(((H_A + H_B + H_C) * SEQ, SEQ), bf16)],
        compiler_params=_cparams("arbitrary"),
        name="ctx_attn",
    )(sink, qa, ka, va, qb, kb, vb, qc, ckv, kr, wukv, wout, x, gate[0])


def _lat_attn_kernel(sink_ref, qa_ref, qb_ref, qc_ref, ka_ref, va_ref, kb_ref, vb_ref, ckv_ref, kr_ref,
                     cak_ref, cav_ref, cbk_ref, cbv_ref, cckv_ref, ckr_ref, bias_ref,
                     wukv_ref, wout_ref, x_ref, gate_ref, o_ref, o_scr, kv_scr, sa, pa, sb, pb, sc, pc):
    qi = pl.program_id(1)
    nb = DEC_SEQ // BLOCK
    scale = HEAD_DIM ** -0.5

    @pl.when(qi == 0)
    def _():
        kv_scr[0:DEC_SEQ, :] = _dot(ckv_ref[...], wukv_ref[...]).astype(bf16)
        kv_scr[DEC_SEQ:DEC_SEQ + PAST_LEN, :] = _dot(cckv_ref[0, 0].astype(bf16), wukv_ref[...]).astype(bf16)

    def blk(ref, j):
        idx = jnp.clip(qi + j, 0, nb - 1)
        return ref[pl.ds(pl.multiple_of(idx * BLOCK, BLOCK), BLOCK), :]

    ka = jnp.concatenate([blk(ka_ref, -1), blk(ka_ref, 0), blk(ka_ref, 1), cak_ref[0, 0].astype(bf16)], axis=0)
    va = jnp.concatenate([blk(va_ref, -1), blk(va_ref, 0), blk(va_ref, 1), cav_ref[0, 0].astype(bf16)], axis=0)
    nk_a = 3 * BLOCK + PAST_LEN
    r = lax.broadcasted_iota(jnp.int32, (BLOCK, nk_a), 0)
    c = lax.broadcasted_iota(jnp.int32, (BLOCK, nk_a), 1)
    valid = (((c < BLOCK) & (c >= r) & (qi > 0))
             | ((c >= BLOCK) & (c < 2 * BLOCK))
             | ((c >= 2 * BLOCK) & (c < 3 * BLOCK) & (c - 2 * BLOCK <= r) & (qi < nb - 1))
             | (c >= 3 * BLOCK))
    for h in range(H_A):
        g = h // G_A
        q = qa_ref[:, h * HEAD_DIM:(h + 1) * HEAD_DIM]
        s = _dot_nt(q, ka[:, g * HEAD_DIM:(g + 1) * HEAD_DIM]) * scale
        sa[h * BLOCK:(h + 1) * BLOCK, :] = jnp.where(valid, s, NEG)

    cbk = cbk_ref[0, 0].astype(bf16)
    cbv = cbv_ref[0, 0].astype(bf16)
    rows_per_blk = BLOCK // GRID_W
    nloc = NA_ROWS * GRID_W
    vcats = []
    for half in range(rows_per_blk):
        grow = qi * rows_per_blk + half
        start = jnp.clip(grow - NA_ROWS // 2, 0, ROWS - NA_ROWS)
        kloc = kb_ref[pl.ds(pl.multiple_of(start * GRID_W, GRID_W), nloc), :]
        vloc = vb_ref[pl.ds(pl.multiple_of(start * GRID_W, GRID_W), nloc), :]
        vcats.append(jnp.concatenate([vloc, cbv], axis=0))
        qrows = slice(half * GRID_W, (half + 1) * GRID_W)
        dr0 = start - grow + (NA_ROWS - 1)
        for h in range(H_B):
            sl = slice(h * HEAD_DIM, (h + 1) * HEAD_DIM)
            q = qb_ref[qrows, sl]
            bias = jnp.concatenate([bias_ref[h, dr0 + 2 * j] for j in range(NA_ROWS // 2)], axis=1)
            s_loc = _dot_nt(q, kloc[:, sl]) * scale + bias
            s_ctx = _dot_nt(q, cbk[:, sl]) * scale
            r0 = (half * H_B + h) * GRID_W
            sb[r0:r0 + GRID_W, :] = jnp.concatenate([s_loc, s_ctx], axis=1)

    kr = jnp.concatenate([kr_ref[...], ckr_ref[0, 0].astype(bf16)], axis=0)
    scale_c = (QK_NOPE + QK_ROPE) ** -0.5
    for h in range(H_C):
        qn = qc_ref[:, h * QC_PAD:h * QC_PAD + QK_NOPE]
        qr = qc_ref[:, h * QC_PAD + QK_NOPE:h * QC_PAD + QK_NOPE + QK_ROPE]
        c0 = h * (QK_NOPE + V_C)
        sc[h * BLOCK:(h + 1) * BLOCK, :] = (_dot_nt(qn, kv_scr[:, c0:c0 + QK_NOPE]) + _dot_nt(qr, kr)) * scale_c

    for pair in range(H_A // 2):
        h0 = 2 * pair
        _softmax_rows(sa, pa, slice(h0 * BLOCK, (h0 + 2) * BLOCK), ((sink_ref[h0], BLOCK), (sink_ref[h0 + 1], BLOCK)))
    for blk2 in range(rows_per_blk * H_B // 2):
        _softmax_rows(sb, pb, slice(blk2 * 2 * GRID_W, (blk2 + 1) * 2 * GRID_W))
    for h in range(H_C):
        _softmax_rows(sc, pc, slice(h * BLOCK, (h + 1) * BLOCK))

    for h in range(H_A):
        g = h // G_A
        o_scr[:, h * HEAD_DIM:(h + 1) * HEAD_DIM] = _dot(pa[h * BLOCK:(h + 1) * BLOCK, :],
                                                         va[:, g * HEAD_DIM:(g + 1) * HEAD_DIM])
    for half in range(rows_per_blk):
        qrows = slice(half * GRID_W, (half + 1) * GRID_W)
        for h in range(H_B):
            sl = slice(h * HEAD_DIM, (h + 1) * HEAD_DIM)
            r0 = (half * H_B + h) * GRID_W
            o_scr[qrows, W_QA + h * HEAD_DIM:W_QA + (h + 1) * HEAD_DIM] = _dot(pb[r0:r0 + GRID_W, :],
                                                                             vcats[half][:, sl])
    for h in range(H_C):
        c0 = h * (QK_NOPE + V_C)
        off = W_QA + W_B + h * V_C
        o_scr[:, off:off + V_C] = _dot(pc[h * BLOCK:(h + 1) * BLOCK, :], kv_scr[:, c0 + QK_NOPE:c0 + QK_NOPE + V_C])

    y = _dot(o_scr[...].astype(bf16), wout_ref[...])
    o_ref[...] = x_ref[...] + gate_ref[0] * y


def _lat_attn(layer, sink, proj, caches, bias_tab, wukv, wout, x, lat_row0, gate):
    qa, ka, va, qb, kb, vb, qc, ckv, kr = proj
    nb = DEC_SEQ // BLOCK
    qrow = lambda b, q: (b * nb + q, 0)
    xrow = lambda b, q: (lat_row0 // BLOCK + b * nb + q, 0)
    brow = lambda b, q: (b, 0)
    const = lambda b, q: (0, 0)
    cidx = lambda b, q: (b, layer, 0, 0)
    in_specs = [pl.BlockSpec(memory_space=pltpu.SMEM)]
    in_specs += [pl.BlockSpec((BLOCK, a.shape[1]), qrow) for a in (qa, qb, qc)]
    in_specs += [pl.BlockSpec((DEC_SEQ, a.shape[1]), brow) for a in (ka, va, kb, vb, ckv, kr)]
    in_specs += [pl.BlockSpec((1, 1, PAST_LEN, a.shape[3]), cidx) for a in caches]
    in_specs += [pl.BlockSpec((None,) + bias_tab.shape[1:], lambda b, q: (layer, 0, 0, 0, 0)),
                 pl.BlockSpec((None, KV_LORA, H_C * (QK_NOPE + V_C)), lambda b, q: (layer, 0, 0)),
                 pl.BlockSpec((None, D_MODEL, D_MODEL), lambda b, q: (layer, 0, 0)),
                 pl.BlockSpec((BLOCK, D_MODEL), xrow),
                 _mod_spec(gate, lambda b, q: 1 + b)]
    return pl.pallas_call(
        _lat_attn_kernel,
        grid=(DEC_BATCH, nb),
        in_specs=in_specs,
        out_specs=pl.BlockSpec((BLOCK, D_MODEL), qrow),
        out_shape=jax.ShapeDtypeStruct((T_LAT, D_MODEL), f32),
        scratch_shapes=[pltpu.VMEM((BLOCK, D_MODEL), f32),
                        pltpu.VMEM((DEC_SEQ + PAST_LEN, H_C * (QK_NOPE + V_C)), bf16)]
        + [pltpu.VMEM(shape, dt) for shape in ((H_A * BLOCK, 3 * BLOCK + PAST_LEN),
                                               (H_B * BLOCK, NA_ROWS * GRID_W + PAST_LEN),
                                               (H_C * BLOCK, DEC_SEQ + PAST_LEN)) for dt in (f32, bf16)],
        compiler_params=_cparams("arbitrary", "arbitrary"),
        name="lat_attn",
    )(sink, qa, qb, qc, ka, va, kb, vb, ckv, kr, *caches, bias_tab, wukv, wout, x, gate[0])


def _pick_stream(xc_ref, xl_ref, x_scr):
    i = pl.program_id(0)

    @pl.when(i < T_CTX // TM_TOK)
    def _():
        x_scr[...] = xc_ref[...]

    @pl.when(i >= T_CTX // TM_TOK)
    def _():
        x_scr[...] = xl_ref[...]

    return x_scr[...]


def _stream_specs(lat_row0):
    n_ctx = T_CTX // TM_TOK
    return [pl.BlockSpec((TM_TOK, D_MODEL), lambda i: (jnp.minimum(i, n_ctx - 1), 0)),
            pl.BlockSpec((TM_TOK, D_MODEL), lambda i: (lat_row0 // TM_TOK + jnp.maximum(i - n_ctx, 0), 0))]


def _router_kernel(xc_ref, xl_ref, g_ref, sh_ref, sc_ref, wr_ref, br_ref, h_ref, e_ref, gt_ref, cnt_ref, x_scr):
    h = _rms(_pick_stream(xc_ref, xl_ref, x_scr), g_ref[...]) * (1.0 + sc_ref[0]) + sh_ref[0]
    _store_row_tiles(h_ref, h)
    h_hi = h.astype(bf16)
    h_lo = (h - h_hi.astype(f32)).astype(bf16)
    w = wr_ref[...]
    w_hi = w.astype(bf16)
    w_lo = (w - w_hi.astype(f32)).astype(bf16)
    logits = _dot(h_hi, w_hi) + _dot(h_hi, w_lo) + _dot(h_lo, w_hi) + br_ref[...]
    lane = lax.broadcasted_iota(jnp.int32, logits.shape, 1).astype(f32)
    l = jnp.where(lane < N_EXPERTS, logits, -jnp.inf)
    tops, idxs = [], []
    for _ in range(TOP_K):
        m = jnp.max(l, axis=-1, keepdims=True)
        idx = jnp.min(jnp.where(l == m, lane, float(LANE)), axis=-1, keepdims=True)
        tops.append(m)
        idxs.append(idx)
        l = jnp.where(lane == idx, -jnp.inf, l)
    ex = [jnp.exp(t - tops[0]) for t in tops]
    den = ex[0] + ex[1] + ex[2] + ex[3]
    e_out = jnp.zeros(logits.shape, f32)
    g_out = jnp.zeros(logits.shape, f32)
    for k in range(TOP_K):
        e_out = jnp.where(lane == k, idxs[k], e_out)
        g_out = jnp.where(lane == k, ex[k] / den, g_out)
    e_ref[...] = e_out.astype(jnp.int32)
    gt_ref[...] = g_out

    @pl.when(pl.program_id(0) == 0)
    def _():
        cnt_ref[...] = jnp.zeros_like(cnt_ref)

    picked = sum((lane == idx).astype(f32) for idx in idxs)
    cnt_ref[0:1, :] += jnp.sum(picked, axis=0, keepdims=True)


def _group_of_tile(i):
    per_b = DEC_SEQ // TM_TOK
    n_ctx = T_CTX // TM_TOK
    return jnp.where(i < n_ctx, 0, 1 + (i - n_ctx) // per_b)


def _router(layer, xc, xl, lat_row0, g, shift, scale, wr, br):
    tm = TM_TOK
    row = lambda i: (i, 0)
    const = lambda i: (0, 0)
    return pl.pallas_call(
        _router_kernel,
        grid=(T_ALL // tm,),
        in_specs=_stream_specs(lat_row0) +
                 [pl.BlockSpec((1, D_MODEL), const),
                  _mod_spec(shift, _group_of_tile),
                  _mod_spec(scale, _group_of_tile),
                  pl.BlockSpec((None, D_MODEL, LANE), lambda i: (layer, 0, 0)),
                  pl.BlockSpec((None, 1, LANE), lambda i: (layer, 0, 0))],
        out_specs=[pl.BlockSpec((tm * ROW_TILE, LANE), row), pl.BlockSpec((tm, LANE), row),
                   pl.BlockSpec((tm, LANE), row), pl.BlockSpec((8, LANE), const)],
        out_shape=[jax.ShapeDtypeStruct((T_ALL * ROW_TILE, LANE), f32),
                   jax.ShapeDtypeStruct((T_ALL, LANE), jnp.int32),
                   jax.ShapeDtypeStruct((T_ALL, LANE), f32),
                   jax.ShapeDtypeStruct((8, LANE), f32)],
        scratch_shapes=[pltpu.VMEM((tm, D_MODEL), f32)],
        compiler_params=_cparams("arbitrary"),
        name="router",
    )(xc, xl, g, shift[0], scale[0], wr, br)


def _dispatch_kernel(tok_ref, nu_ref, h_hbm, o_ref, hv, xg, hsem):
    tm = TM_MOE
    i = pl.program_id(0)

    @pl.when(i == 0)
    def _():
        resident = pltpu.make_async_copy(h_hbm, hv, hsem.at[0])
        resident.start()
        resident.wait()

    def one_block(sub, carry):
        blk = i * DISPATCH_BLOCKS + sub
        rows = pl.ds(pl.multiple_of(sub * tm, tm), tm)

        @pl.when(blk < nu_ref[0])
        def _():
            for r in range(tm):
                t = tok_ref[blk * tm + r]
                xg[pl.ds(r, ROW_TILE, stride=tm + 1), :] = hv[pl.ds(pl.multiple_of(t * ROW_TILE, ROW_TILE),
                                                                 ROW_TILE), :]
            o_ref[rows, :] = jnp.concatenate([xg[pl.ds(c * (tm + 1), tm), :] for c in range(ROW_TILE)],
                                             axis=1).astype(bf16)

        @pl.when(blk >= nu_ref[0])
        def _():
            o_ref[rows, :] = jnp.zeros((tm, D_MODEL), bf16)

        return carry

    lax.fori_loop(0, DISPATCH_BLOCKS, one_block, 0)


def _dispatch(row_tok, n_used, h):
    tm = TM_MOE
    return pl.pallas_call(
        _dispatch_kernel,
        grid_spec=pltpu.PrefetchScalarGridSpec(
            num_scalar_prefetch=2,
            grid=(N_MOE_BLOCKS // DISPATCH_BLOCKS,),
            in_specs=[pl.BlockSpec(memory_space=pl.ANY)],
            out_specs=pl.BlockSpec((DISPATCH_BLOCKS * tm, D_MODEL), lambda i, tok, nu: (i, 0)),
            scratch_shapes=[pltpu.VMEM((T_ALL * ROW_TILE, LANE), f32), pltpu.VMEM(((tm + 1) * ROW_TILE, LANE), f32),
                            pltpu.SemaphoreType.DMA((1,))]),
        out_shape=jax.ShapeDtypeStruct((N_MOE_BLOCKS * tm, D_MODEL), bf16),
        compiler_params=_cparams("arbitrary"),
        name="dispatch",
    )(row_tok, n_used, h)


def _moe_kernel(layer, be_ref, nu_ref, nxt_ref, dst_ref, x_ref, wgu_hbm, bgu_ref, wd_hbm, bd_ref, y_hbm,
                y0, y1, wgu_st, wd_st, wgu_bf, wd_bf, wsem, ssem):
    tm = TM_MOE
    i = pl.program_id(0)
    nb = pl.num_programs(0)
    used = i < nu_ref[0]
    yb = (y0, y1)

    def out_tile(row):
        return pl.ds(pl.multiple_of(row * ROW_TILE, ROW_TILE), ROW_TILE)

    def scatter_desc(buf, r, dst_row, s):
        return pltpu.make_async_copy(buf.at[out_tile(r)], y_hbm.at[out_tile(dst_row)], ssem.at[s])

    def scatter_wait(s):
        pltpu.make_async_copy(yb[s], y_hbm.at[pl.ds(0, tm * ROW_TILE)], ssem.at[s]).wait()

    def scatter_start(blk, s, unrolled):
        if unrolled:
            for r in range(tm):
                scatter_desc(yb[s], r, dst_ref[(blk + 1) * tm + r], s).start(priority=r % 2)
        else:
            def body(r, carry):
                scatter_desc(yb[s], r, dst_ref[(blk + 1) * tm + r], s).start()
                return carry
            lax.fori_loop(0, tm, body, 0, unroll=8)

    def weight_copies(e):
        return (pltpu.make_async_copy(wgu_hbm.at[layer, e], wgu_st, wsem.at[0]),
                pltpu.make_async_copy(wd_hbm.at[layer, e], wd_st, wsem.at[1]))

    @pl.when(i == 0)
    def _():
        for s in range(2):
            yb[s][...] = jnp.zeros_like(yb[s])
            dummy = pltpu.make_async_copy(yb[s], y_hbm.at[pl.ds((N_ASSIGN + s * tm) * ROW_TILE, tm * ROW_TILE)],
                                          ssem.at[s])
            dummy.start()
            dummy.wait()
        for cp in weight_copies(be_ref[0]):
            cp.start()

    first = jnp.logical_and(used, jnp.logical_or(i == 0, be_ref[i] != be_ref[jnp.maximum(i - 1, 0)]))

    @pl.when(first)
    def _():
        for cp in weight_copies(0):
            cp.wait()
        wgu_bf[...] = wgu_st[...].astype(bf16)
        wd_bf[...] = wd_st[...].astype(bf16)

        @pl.when(nxt_ref[i] >= 0)
        def _():
            for cp in weight_copies(nxt_ref[i]):
                cp.start()

    def step(par):
        cur, oth = par, 1 - par

        @pl.when(jnp.logical_and(i >= 1, i - 2 < nu_ref[0]))
        def _():
            scatter_wait(cur)

        @pl.when(used)
        def _():
            scatter_start(i - 1, oth, unrolled=True)
            gu = _dot(x_ref[...], wgu_bf[...]) + bgu_ref[0, 0]
            x_glu = jnp.minimum(gu[:, :D_FF], SWIGLU_LIMIT)
            x_lin = jnp.clip(gu[:, D_FF:], -SWIGLU_LIMIT, SWIGLU_LIMIT)
            act = x_glu * jax.nn.sigmoid(SWIGLU_ALPHA * x_glu) * (x_lin + 1.0)
            _store_row_tiles(yb[cur], _dot(act.astype(bf16), wd_bf[...]) + bd_ref[0, 0])

        flush = jnp.logical_and(jnp.logical_not(used), i - 1 < nu_ref[0])

        @pl.when(flush)
        def _():
            scatter_start(i - 1, oth, unrolled=False)

        @pl.when(jnp.logical_and(flush, i == nb - 1))
        def _():
            scatter_wait(oth)

    @pl.when(i % 2 == 0)
    def _():
        step(0)

    @pl.when(i % 2 == 1)
    def _():
        step(1)


def _moe(layer, routing, h, w_gu, b_gu, w_down, b_down):
    tm = TM_MOE
    block_e, n_used, nxt_e, row_tok, row_dst = routing
    xs = _dispatch(row_tok, n_used, h)
    ex4 = lambda i, be, nu, nxt, dst: (layer, be[i], 0, 0)
    return pl.pallas_call(
        functools.partial(_moe_kernel, layer),
        grid_spec=pltpu.PrefetchScalarGridSpec(
            num_scalar_prefetch=4,
            grid=(N_MOE_BLOCKS,),
            in_specs=[pl.BlockSpec((tm, D_MODEL), lambda i, be, nu, nxt, dst: (i, 0)),
                      pl.BlockSpec(memory_space=pl.ANY),
                      pl.BlockSpec((1, 1, 1, 2 * D_FF), ex4),
                      pl.BlockSpec(memory_space=pl.ANY),
                      pl.BlockSpec((1, 1, 1, D_MODEL), ex4)],
            out_specs=pl.BlockSpec(memory_space=pl.ANY),
            scratch_shapes=[pltpu.VMEM((tm * ROW_TILE, LANE), f32), pltpu.VMEM((tm * ROW_TILE, LANE), f32),
                            pltpu.VMEM((D_MODEL, 2 * D_FF), f32), pltpu.VMEM((D_FF, D_MODEL), f32),
                            pltpu.VMEM((D_MODEL, 2 * D_FF), bf16), pltpu.VMEM((D_FF, D_MODEL), bf16),
                            pltpu.SemaphoreType.DMA((2,)), pltpu.SemaphoreType.DMA((2,))]),
        out_shape=jax.ShapeDtypeStruct(((N_ASSIGN + 2 * tm) * ROW_TILE, LANE), f32),
        compiler_params=_cparams("arbitrary"),
        name="moe",
    )(block_e, n_used, nxt_e, row_dst, xs, w_gu, b_gu.reshape(DEPTH, N_EXPERTS, 1, 2 * D_FF),
      w_down, b_down.reshape(DEPTH, N_EXPERTS, 1, D_MODEL))


def _combine_kernel(final, xc_ref, xl_ref, y0_ref, y1_ref, y2_ref, y3_ref, gt_ref, gate_ref, gf_ref, *rest):
    x_scr = rest[-1]
    gt = gt_ref[...]
    f = gt[:, 0:1] * _load_row_tiles(y0_ref)
    for k, y_ref in ((1, y1_ref), (2, y2_ref), (3, y3_ref)):
        f = f + gt[:, k:k + 1] * _load_row_tiles(y_ref)
    out = _pick_stream(xc_ref, xl_ref, x_scr) + gate_ref[0] * f
    if not final:
        rest[0][...] = out
        return
    out = _rms(out, gf_ref[...])
    oc_ref, ol_ref = rest[0], rest[1]
    i = pl.program_id(0)

    @pl.when(i < T_CTX // TM_TOK)
    def _():
        oc_ref[...] = out

    @pl.when(i >= T_CTX // TM_TOK)
    def _():
        ol_ref[...] = out


def _combine(final, xc, xl, lat_row0, y, gates, gate, g_final):
    tm = TM_TOK
    nt = T_ALL // tm
    n_ctx = T_CTX // tm
    row = lambda i: (i, 0)
    const = lambda i: (0, 0)
    ysel = [pl.BlockSpec((tm * ROW_TILE, LANE), functools.partial(lambda k, i: (k * nt + i, 0), k))
            for k in range(TOP_K)]
    if final:
        out_specs = [pl.BlockSpec((tm, D_MODEL), lambda i: (jnp.minimum(i, n_ctx - 1), 0)),
                     pl.BlockSpec((tm, D_MODEL), lambda i: (jnp.maximum(i - n_ctx, 0), 0))]
        out_shape = [jax.ShapeDtypeStruct((T_CTX, D_MODEL), f32), jax.ShapeDtypeStruct((T_LAT, D_MODEL), f32)]
    else:
        out_specs = pl.BlockSpec((tm, D_MODEL), row)
        out_shape = jax.ShapeDtypeStruct((T_ALL, D_MODEL), f32)
    return pl.pallas_call(
        functools.partial(_combine_kernel, final),
        grid=(nt,),
        in_specs=_stream_specs(lat_row0) + ysel +
                 [pl.BlockSpec((tm, LANE), row),
                  _mod_spec(gate, _group_of_tile),
                  pl.BlockSpec((1, D_MODEL), const)],
        out_specs=out_specs,
        out_shape=out_shape,
        scratch_shapes=[pltpu.VMEM((tm, D_MODEL), f32)],
        compiler_params=_cparams("arbitrary"),
        name="combine",
    )(xc, xl, y, y, y, y, gates, gate[0], g_final)


def _rope_head_tables(d):
    nf = d // 4
    half = d // 2
    t = np.arange(DEC_SEQ)
    inv = ROPE_BASE ** (-np.arange(nf, dtype=np.float32) / nf)
    i = np.arange(d)
    pos = np.where(i[None, :] < half, (t // GRID_W)[:, None], (t % GRID_W)[:, None]).astype(np.float32)
    ang = pos * inv[i % nf][None, :].astype(np.float32)
    first = (i % half) < nf
    cos = np.cos(ang)
    sin = np.where(first[None, :], -np.sin(ang), np.sin(ang))
    partner = np.where(first, i + nf, i - nf)
    return cos.astype(np.float32), sin.astype(np.float32), partner


def _rope_tables():
    cos64, sin64, _ = _rope_head_tables(HEAD_DIM)
    cos32, sin32, _ = _rope_head_tables(QK_ROPE)
    cosa = np.tile(cos64, (1, H_A))
    sina = np.tile(sin64, (1, H_A))
    cosq1 = np.concatenate([np.ones((DEC_SEQ, QK_NOPE), np.float32), cos32,
                            np.ones((DEC_SEQ, QC_PAD - QK_NOPE - QK_ROPE), np.float32)], axis=1)
    sinq1 = np.concatenate([np.zeros((DEC_SEQ, QK_NOPE), np.float32), sin32,
                            np.zeros((DEC_SEQ, QC_PAD - QK_NOPE - QK_ROPE), np.float32)], axis=1)
    cosq = np.tile(cosq1, (1, H_C))
    sinq = np.tile(sinq1, (1, H_C))
    return tuple(jnp.asarray(a) for a in (cosa, sina, cosq, sinq, cos32, sin32))


def _pad_cols(w, n):
    return jnp.pad(w, ((0, 0), (0, n - w.shape[1])))


def _layer_weights(w_in, w_uq):
    cuts = np.cumsum((W_QA, W_KA, W_VA, W_B, W_B, W_B, Q_LORA, KV_LORA, QK_ROPE))[:-1]
    qa, ka, va, qb, kb, vb, cq, ckv, kr = jnp.split(w_in, [int(c) for c in cuts], axis=1)
    _, _, p64 = _rope_head_tables(HEAD_DIM)
    _, _, p32 = _rope_head_tables(QK_ROPE)
    pa = np.concatenate([h * HEAD_DIM + p64 for h in range(H_A)])
    base = jnp.concatenate([qa, ka, va, _pad_cols(qb, 384), _pad_cols(kb, 384), _pad_cols(vb, 384), cq, ckv,
                            _pad_cols(kr, 128)], axis=1)
    w_ctx = base.astype(bf16)
    w_lat = jnp.concatenate([base, qa[:, pa], ka[:, pa[:W_KA]], _pad_cols(kr[:, p32], 128)], axis=1).astype(bf16)
    hq = QK_NOPE + QK_ROPE
    heads = [_pad_cols(w_uq[:, h * hq:(h + 1) * hq], QC_PAD) for h in range(H_C)]
    pq = np.concatenate([np.arange(QK_NOPE), QK_NOPE + p32])
    heads_p = [_pad_cols(w_uq[:, h * hq:(h + 1) * hq][:, pq], QC_PAD) for h in range(H_C)]
    wuq = jnp.concatenate(heads, axis=1).astype(bf16)
    wuq2 = jnp.concatenate(heads + heads_p, axis=1).astype(bf16)
    return w_ctx, w_lat, wuq, wuq2


def _bias_table(rpb):
    col = np.arange(GRID_W)
    col_start = np.clip(col - NA_COLS // 2, 0, GRID_W - NA_COLS)
    col_ok = (col[None, :] >= col_start[:, None]) & (col[None, :] < col_start[:, None] + NA_COLS)
    dc = np.clip(col[None, :] - col[:, None] + (NA_COLS - 1), 0, 2 * NA_COLS - 2)
    onehot = (dc[None] == np.arange(2 * NA_COLS - 1)[:, None, None]).astype(np.float32)
    expanded = jnp.einsum('hrd,dqk->hrqk', rpb.astype(f32), jnp.asarray(onehot), precision=lax.Precision.HIGHEST)
    blocks = jnp.where(col_ok[None, None], expanded, NEG)
    return jnp.concatenate([blocks[:, :-1], blocks[:, 1:]], axis=-1)


def _routing(top_e, counts):
    tm = TM_MOE
    key_bits = 16
    pad_mark = (1 << key_bits) - 1
    flat_e = top_e.T.reshape(N_ASSIGN)
    experts = jnp.arange(N_EXPERTS, dtype=jnp.int32)
    nblk = (counts + tm - 1) // tm
    blk_end = jnp.cumsum(nblk)
    pad_end = jnp.cumsum(nblk * tm - counts)
    slots = jnp.arange(N_MOE_BLOCKS * tm - N_ASSIGN, dtype=jnp.int32)
    pad_e = jnp.sum((pad_end[None, :] <= slots[:, None]).astype(jnp.int32), axis=1)
    keys = jnp.concatenate([(flat_e << key_bits) + jnp.arange(N_ASSIGN, dtype=jnp.int32),
                            (pad_e << key_bits) + pad_mark])
    asg = (jnp.sort(keys, stable=False) & pad_mark).reshape(N_MOE_BLOCKS, tm)
    valid = asg != pad_mark
    blocks = jnp.arange(N_MOE_BLOCKS, dtype=jnp.int32)
    r = jnp.arange(tm, dtype=jnp.int32)[None, :]
    tok = jnp.where(valid, asg % T_ALL, 0)
    row_dst = jnp.where(valid, asg, N_ASSIGN + (blocks[:, None] % 2) * tm + r)
    row_dst = jnp.concatenate([N_ASSIGN + tm + r, row_dst], axis=0).reshape(-1)
    block_e = jnp.minimum(jnp.sum((blk_end[None, :] <= blocks[:, None]).astype(jnp.int32), axis=1), N_EXPERTS - 1)
    n_used = blk_end[-1].astype(jnp.int32).reshape(1)
    has = jnp.where(counts > 0, experts, N_EXPERTS)
    later = experts[None, :] > experts[:, None]
    nxt = jnp.min(jnp.where(later, has[None, :], N_EXPERTS), axis=1)
    nxt = jnp.where(nxt >= N_EXPERTS, -1, nxt)
    sel = (block_e[:, None] == experts[None, :]).astype(jnp.int32)
    nxt_e = jnp.sum(sel * nxt[None, :], axis=1)
    i32 = lambda a: a.astype(jnp.int32)
    return i32(block_e), n_used, i32(nxt_e), i32(tok).reshape(-1), i32(row_dst)


def kernel(x_prompt, x_sample, cache_a_k, cache_a_v, cache_b_k, cache_b_v, cache_c_kv, cache_c_kr, c, c_ctx, w_ada, b_ada, g_attn, g_ffn, w_in, sink_a, rpb_b, g_cq, g_ckv, w_uq, w_ukv, w_out, w_router, b_router, w_gu, b_gu, w_down, b_down, g_final):
    xc, xl, lat_row0 = x_prompt.reshape(T_CTX, D_MODEL), x_sample.reshape(T_LAT, D_MODEL), 0
    cvec = jnp.concatenate([c_ctx[None, :], c, jnp.zeros((8 - N_GROUPS, D_MODEL), f32)], axis=0)
    mods = _ada(cvec, w_ada, b_ada)[:, :N_GROUPS].reshape(DEPTH, N_GROUPS, 6, 1, D_MODEL)
    tabs = _rope_tables()
    caches = (cache_a_k.reshape(DEC_BATCH, DEPTH, PAST_LEN, W_KA), cache_a_v.reshape(DEC_BATCH, DEPTH, PAST_LEN, W_VA),
              cache_b_k.reshape(DEC_BATCH, DEPTH, PAST_LEN, W_B), cache_b_v.reshape(DEC_BATCH, DEPTH, PAST_LEN, W_B),
              cache_c_kv, cache_c_kr)
    new = None
    w_ctx, w_lat, wuq, wuq2 = jax.vmap(_layer_weights)(w_in, w_uq)
    wukv = w_ukv.astype(bf16)
    wout = w_out.astype(bf16)
    bias_tab = jax.vmap(_bias_table)(rpb_b)
    wr = jnp.pad(w_router, ((0, 0), (0, 0), (0, LANE - N_EXPERTS)))
    br = jnp.pad(b_router, ((0, 0), (0, LANE - N_EXPERTS)))[:, None, :]
    for layer in range(DEPTH):
        m = [(mods, layer, j) for j in range(6)]
        g1 = g_attn[layer][None, :]
        gcq = g_cq[layer][None, :]
        gckv = g_ckv[layer][None, :]
        sink = sink_a[layer]

        qs, new = _inproj_ctx(layer, new, xc, g1, m[0], m[1], w_ctx, gcq, gckv, wuq)
        x_ctx = _ctx_attn(layer, sink, qs, new, wukv, wout, xc, m[2])

        plat = _inproj_lat(layer, xl, lat_row0, g1, m[0], m[1], w_lat, gcq, gckv, wuq2, tabs)
        x_lat = _lat_attn(layer, sink, plat, caches, bias_tab, wukv, wout, xl, lat_row0, m[2])

        h2, top_e, gates, cnt = _router(layer, x_ctx, x_lat, 0, g_ffn[layer][None, :], m[3], m[4], wr, br)
        counts = cnt[0, :N_EXPERTS].astype(jnp.int32)
        y = _moe(layer, _routing(top_e[:, :TOP_K], counts), h2, w_gu, b_gu, w_down, b_down)
        x = _combine(layer == DEPTH - 1, x_ctx, x_lat, 0, y, gates, m[5], g_final[None, :])
        xc, xl, lat_row0 = x, x, T_CTX

    y_prompt = x[0].reshape(BATCH, SEQ, D_MODEL)
    y_sample = x[1].reshape(DEC_BATCH, DEC_SEQ, D_MODEL)
    shapes = ((KV_A, HEAD_DIM), (KV_A, HEAD_DIM), (H_B, HEAD_DIM), (H_B, HEAD_DIM), (KV_LORA,), (QK_ROPE,))
    outs = [a.reshape((BATCH, DEPTH, SEQ) + s) for a, s in zip(new, shapes)]
    return (y_prompt, y_sample, *outs)
```

```python
import functools

import numpy as np
import jax
import jax.numpy as jnp
from jax import lax
from jax.experimental import pallas as pl
from jax.experimental.pallas import tpu as pltpu

D_MODEL = 1024
BATCH = 32
SEQ = 256
DEPTH = 2
DEC_BATCH = 2
DEC_SEQ = 1024
PAST_LEN = 512
GRID_W = 64
HEAD_DIM = 64
H_A = 6
KV_A = 2
G_A = H_A // KV_A
WINDOW = 128
BLOCK = 128
H_B = 5
NA_ROWS = 8
NA_COLS = 16
H_C = 5
Q_LORA = 384
KV_LORA = 256
QK_NOPE = 64
QK_ROPE = 32
V_C = 64
N_EXPERTS = 32
TOP_K = 4
D_FF = 1024
SWIGLU_ALPHA = 1.702
SWIGLU_LIMIT = 7.0
ROPE_BASE = 10000.0
EPS = 1e-6
NEG = -1e30

T_CTX = BATCH * SEQ
T_LAT = DEC_BATCH * DEC_SEQ
T_ALL = T_CTX + T_LAT
N_GROUPS = 1 + DEC_BATCH
LANE = 128
QC_PAD = 128
ROWS = DEC_SEQ // GRID_W

W_QA, W_KA, W_VA = H_A * HEAD_DIM, KV_A * HEAD_DIM, KV_A * HEAD_DIM
W_B = H_B * HEAD_DIM
OFF_QA = 0
OFF_KA = 384
OFF_VA = 512
OFF_QB = 640
OFF_KB = 1024
OFF_VB = 1408
OFF_CQ = 1792
OFF_CKV = 2176
OFF_KR = 2432
NW_CTX = 2560
OFF_QA_P = 2560
OFF_KA_P = 2944
OFF_KR_P = 3072
NW_LAT = 3200

TM_TOK = 512
TM_LAT_IN = 512
TM_MOE = 256
N_ASSIGN = T_ALL * TOP_K
N_MOE_BLOCKS = N_ASSIGN // TM_MOE + N_EXPERTS
DISPATCH_BLOCKS = 8
VMEM_LIMIT = 56 * 1024 * 1024

f32 = jnp.float32
bf16 = jnp.bfloat16


def _cparams(*sem):
    return pltpu.CompilerParams(dimension_semantics=sem, vmem_limit_bytes=VMEM_LIMIT)


def _mod_spec(mod, group_of):
    _, layer, j = mod
    return pl.BlockSpec((None, 1, None, 1, D_MODEL), lambda *idx: (layer, group_of(*idx), j, 0, 0))


def _rms(xf, g):
    return xf * lax.rsqrt(jnp.mean(xf * xf, axis=-1, keepdims=True) + EPS) * g


def _dot(a, b):
    return jnp.dot(a, b, preferred_element_type=f32)


def _dot_nt(a, b):
    return lax.dot_general(a, b, (((1,), (1,)), ((), ())), preferred_element_type=f32)


ROW_TILE = D_MODEL // LANE


def _store_row_tiles(ref, val):
    n = val.shape[0]
    for c in range(ROW_TILE):
        ref[pl.ds(c, n, stride=ROW_TILE), :] = val[:, c * LANE:(c + 1) * LANE]


def _load_row_tiles(ref):
    n = ref.shape[0] // ROW_TILE
    return jnp.concatenate([ref[pl.ds(c, n, stride=ROW_TILE), :] for c in range(ROW_TILE)], axis=1)


def _softmax_rows(s_ref, p_ref, rows, sinks=None):
    s = s_ref[rows, :]
    m = jnp.max(s, axis=-1, keepdims=True)
    if sinks is not None:
        sink = jnp.concatenate([jnp.full((n, 1), v, f32) for v, n in sinks], axis=0)
        m = jnp.maximum(m, sink)
    p = jnp.exp(s - m)
    l = jnp.sum(p, axis=-1, keepdims=True)
    if sinks is not None:
        l = l + jnp.exp(sink - m)
    p_ref[rows, :] = (p * (1.0 / l)).astype(bf16)


def _ada_kernel(c_ref, w_ref, b_ref, o_ref):
    c = c_ref[...]
    s = c * jax.nn.sigmoid(c)
    s_hi = s.astype(bf16)
    s_lo = (s - s_hi.astype(f32)).astype(bf16)
    w = w_ref[0]
    w_hi = w.astype(bf16)
    w_lo = (w - w_hi.astype(f32)).astype(bf16)
    o_ref[0] = _dot(s_hi, w_hi) + _dot(s_hi, w_lo) + _dot(s_lo, w_hi) + b_ref[0]


def _ada(cvec, w_ada, b_ada):
    tn = 1536
    return pl.pallas_call(
        _ada_kernel,
        grid=(DEPTH, 6 * D_MODEL // tn),
        in_specs=[pl.BlockSpec((8, D_MODEL), lambda l, j: (0, 0)),
                  pl.BlockSpec((1, D_MODEL, tn), lambda l, j: (l, 0, j)),
                  pl.BlockSpec((1, 1, tn), lambda l, j: (l, 0, j))],
        out_specs=pl.BlockSpec((1, 8, tn), lambda l, j: (l, 0, j)),
        out_shape=jax.ShapeDtypeStruct((DEPTH, 8, 6 * D_MODEL), f32),
        compiler_params=_cparams("arbitrary", "arbitrary"),
        name="ada",
    )(cvec, w_ada, b_ada.reshape(DEPTH, 1, 6 * D_MODEL))


CACHE_WIDTHS = (W_KA, W_VA, W_B, W_B, KV_LORA, QK_ROPE)


def _inproj_ctx_kernel(layer, x_ref, g_ref, sh_ref, sc_ref, w_ref, gcq_ref, gckv_ref, wuq_ref, *refs):
    qa_ref, qb_ref, qc_ref, ka_ref, va_ref, kb_ref, vb_ref, ckv_ref, kr_ref = refs[-9:]
    h = _rms(x_ref[...], g_ref[...]) * (1.0 + sc_ref[0]) + sh_ref[0]
    p = _dot(h.astype(bf16), w_ref[...])
    qa_ref[...] = p[:, OFF_QA:OFF_QA + W_QA].astype(bf16)
    qb_ref[...] = p[:, OFF_QB:OFF_QB + W_B].astype(bf16)
    cqn = _rms(p[:, OFF_CQ:OFF_CQ + Q_LORA], gcq_ref[...])
    qc_ref[...] = _dot(cqn.astype(bf16), wuq_ref[...]).astype(bf16)
    caches = ((ka_ref, p[:, OFF_KA:OFF_KA + W_KA]), (va_ref, p[:, OFF_VA:OFF_VA + W_VA]),
              (kb_ref, p[:, OFF_KB:OFF_KB + W_B]), (vb_ref, p[:, OFF_VB:OFF_VB + W_B]),
              (ckv_ref, _rms(p[:, OFF_CKV:OFF_CKV + KV_LORA], gckv_ref[...])),
              (kr_ref, p[:, OFF_KR:OFF_KR + QK_ROPE]))
    for ref, val in caches:
        for b in range(TM_TOK // SEQ):
            rows = val[b * SEQ:(b + 1) * SEQ]
            if layer == 0:
                ref[b, 0] = rows
                for later in range(1, DEPTH):
                    ref[b, later] = jnp.zeros_like(rows)
            else:
                ref[b, 0] = rows


def _inproj_ctx(layer, prev_caches, x, g, shift, scale, w, gcq, gckv, wuq):
    tm = TM_TOK
    nb = tm // SEQ
    row = lambda i: (i, 0)
    const = lambda i: (0, 0)
    in_specs = [pl.BlockSpec((tm, D_MODEL), row),
                pl.BlockSpec((1, D_MODEL), const),
                _mod_spec(shift, lambda i: 0),
                _mod_spec(scale, lambda i: 0),
                pl.BlockSpec((None, D_MODEL, NW_CTX), lambda i: (layer, 0, 0)),
                pl.BlockSpec((1, Q_LORA), const),
                pl.BlockSpec((1, KV_LORA), const),
                pl.BlockSpec((None, Q_LORA, H_C * QC_PAD), lambda i: (layer, 0, 0))]
    q_widths = (W_QA, W_B, H_C * QC_PAD)
    out_specs = [pl.BlockSpec((tm, wd), row) for wd in q_widths]
    out_shape = [jax.ShapeDtypeStruct((T_CTX, wd), bf16) for wd in q_widths]
    if layer == 0:
        out_specs += [pl.BlockSpec((nb, DEPTH, SEQ, wd), lambda i: (i, 0, 0, 0)) for wd in CACHE_WIDTHS]
        aliases, extra = {}, ()
    else:
        in_specs += [pl.BlockSpec(memory_space=pl.ANY) for _ in CACHE_WIDTHS]
        out_specs += [pl.BlockSpec((nb, 1, SEQ, wd), lambda i: (i, layer, 0, 0)) for wd in CACHE_WIDTHS]
        aliases = {8 + j: len(q_widths) + j for j in range(len(CACHE_WIDTHS))}
        extra = tuple(prev_caches)
    out_shape += [jax.ShapeDtypeStruct((BATCH, DEPTH, SEQ, wd), f32) for wd in CACHE_WIDTHS]
    outs = pl.pallas_call(
        functools.partial(_inproj_ctx_kernel, layer),
        grid=(T_CTX // tm,),
        in_specs=in_specs,
        out_specs=out_specs,
        out_shape=out_shape,
        input_output_aliases=aliases,
        compiler_params=_cparams("arbitrary"),
        name="inproj_ctx",
    )(x, g, shift[0], scale[0], w, gcq, gckv, wuq, *extra)
    return outs[:3], outs[3:]


def _inproj_lat_kernel(x_ref, g_ref, sh_ref, sc_ref, w_ref, gcq_ref, gckv_ref, wuq_ref,
                       cosa_ref, sina_ref, cosq_ref, sinq_ref, cosr_ref, sinr_ref,
                       qa_ref, ka_ref, va_ref, qb_ref, kb_ref, vb_ref, qc_ref, ckv_ref, kr_ref):
    h = _rms(x_ref[...], g_ref[...]) * (1.0 + sc_ref[0]) + sh_ref[0]
    p = _dot(h.astype(bf16), w_ref[...])
    cosa = cosa_ref[...]
    sina = sina_ref[...]
    qa = p[:, OFF_QA:OFF_QA + W_QA] * cosa + p[:, OFF_QA_P:OFF_QA_P + W_QA] * sina
    ka = p[:, OFF_KA:OFF_KA + W_KA] * cosa[:, :W_KA] + p[:, OFF_KA_P:OFF_KA_P + W_KA] * sina[:, :W_KA]
    kr = p[:, OFF_KR:OFF_KR + QK_ROPE] * cosr_ref[...] + p[:, OFF_KR_P:OFF_KR_P + QK_ROPE] * sinr_ref[...]
    qa_ref[...] = qa.astype(bf16)
    ka_ref[...] = ka.astype(bf16)
    va_ref[...] = p[:, OFF_VA:OFF_VA + W_VA].astype(bf16)
    qb_ref[...] = p[:, OFF_QB:OFF_QB + W_B].astype(bf16)
    kb_ref[...] = p[:, OFF_KB:OFF_KB + W_B].astype(bf16)
    vb_ref[...] = p[:, OFF_VB:OFF_VB + W_B].astype(bf16)
    cqn = _rms(p[:, OFF_CQ:OFF_CQ + Q_LORA], gcq_ref[...])
    q2 = _dot(cqn.astype(bf16), wuq_ref[...])
    nq = H_C * QC_PAD
    qc_ref[...] = (q2[:, :nq] * cosq_ref[...] + q2[:, nq:] * sinq_ref[...]).astype(bf16)
    ckv_ref[...] = _rms(p[:, OFF_CKV:OFF_CKV + KV_LORA], gckv_ref[...]).astype(bf16)
    kr_ref[...] = kr.astype(bf16)


def _inproj_lat(layer, x, lat_row0, g, shift, scale, w, gcq, gckv, wuq2, tabs):
    tm = TM_LAT_IN
    per_b = DEC_SEQ // tm
    row0 = lat_row0 // tm
    xrow = lambda i: (row0 + i, 0)
    row = lambda i: (i, 0)
    const = lambda i: (0, 0)
    pos = lambda i: (i % per_b, 0)
    cosa, sina, cosq, sinq, cosr, sinr = tabs
    widths = (W_QA, W_KA, W_VA, W_B, W_B, W_B, H_C * QC_PAD, KV_LORA, QK_ROPE)
    return pl.pallas_call(
        _inproj_lat_kernel,
        grid=(T_LAT // tm,),
        in_specs=[pl.BlockSpec((tm, D_MODEL), xrow),
                  pl.BlockSpec((1, D_MODEL), const),
                  _mod_spec(shift, lambda i: 1 + i // per_b),
                  _mod_spec(scale, lambda i: 1 + i // per_b),
                  pl.BlockSpec((None, D_MODEL, NW_LAT), lambda i: (layer, 0, 0)),
                  pl.BlockSpec((1, Q_LORA), const),
                  pl.BlockSpec((1, KV_LORA), const),
                  pl.BlockSpec((None, Q_LORA, 2 * H_C * QC_PAD), lambda i: (layer, 0, 0)),
                  pl.BlockSpec((tm, W_QA), pos), pl.BlockSpec((tm, W_QA), pos),
                  pl.BlockSpec((tm, H_C * QC_PAD), pos), pl.BlockSpec((tm, H_C * QC_PAD), pos),
                  pl.BlockSpec((tm, QK_ROPE), pos), pl.BlockSpec((tm, QK_ROPE), pos)],
        out_specs=[pl.BlockSpec((tm, wd), row) for wd in widths],
        out_shape=[jax.ShapeDtypeStruct((T_LAT, wd), bf16) for wd in widths],
        compiler_params=_cparams("arbitrary"),
        name="inproj_lat",
    )(x, g, shift[0], scale[0], w, gcq, gckv, wuq2, cosa, sina, cosq, sinq, cosr, sinr)


CTX_BATCHES = 2


def _ctx_attn_kernel(sink_ref, qa_ref, ka_ref, va_ref, qb_ref, kb_ref, vb_ref, qc_ref, ckv_ref, kr_ref,
                     wukv_ref, wout_ref, x_ref, gate_ref, o_ref, o_scr, s_scr, p_scr):
    n = SEQ
    scale = HEAD_DIM ** -0.5
    scale_c = (QK_NOPE + QK_ROPE) ** -0.5

    def one_batch(sb, carry):
        rows = pl.ds(pl.multiple_of(sb * n, n), n)
        ka = ka_ref[sb, 0].astype(bf16)
        va = va_ref[sb, 0].astype(bf16)
        kb = kb_ref[sb, 0].astype(bf16)
        vb = vb_ref[sb, 0].astype(bf16)
        kv = _dot(ckv_ref[sb, 0].astype(bf16), wukv_ref[...]).astype(bf16)
        kr = kr_ref[sb, 0].astype(bf16)
        for h in range(H_A):
            g = h // G_A
            q = qa_ref[rows, h * HEAD_DIM:(h + 1) * HEAD_DIM]
            s_scr[h * n:(h + 1) * n, :] = _dot_nt(q, ka[:, g * HEAD_DIM:(g + 1) * HEAD_DIM]) * scale
        for h in range(H_B):
            sl = slice(h * HEAD_DIM, (h + 1) * HEAD_DIM)
            s_scr[(H_A + h) * n:(H_A + h + 1) * n, :] = _dot_nt(qb_ref[rows, sl], kb[:, sl]) * scale
        for h in range(H_C):
            qn = qc_ref[rows, h * QC_PAD:h * QC_PAD + QK_NOPE]
            qr = qc_ref[rows, h * QC_PAD + QK_NOPE:h * QC_PAD + QK_NOPE + QK_ROPE]
            c0 = h * (QK_NOPE + V_C)
            r0 = (H_A + H_B + h) * n
            s_scr[r0:r0 + n, :] = (_dot_nt(qn, kv[:, c0:c0 + QK_NOPE]) + _dot_nt(qr, kr)) * scale_c
        for pair in range((H_A + H_B + H_C) // 2):
            h0 = 2 * pair
            sinks = ((sink_ref[h0], n), (sink_ref[h0 + 1], n)) if h0 < H_A else None
            _softmax_rows(s_scr, p_scr, slice(h0 * n, (h0 + 2) * n), sinks)
        for h in range(H_A):
            g = h // G_A
            o_scr[:, h * HEAD_DIM:(h + 1) * HEAD_DIM] = _dot(p_scr[h * n:(h + 1) * n, :],
                                                             va[:, g * HEAD_DIM:(g + 1) * HEAD_DIM])
        for h in range(H_B):
            sl = slice(h * HEAD_DIM, (h + 1) * HEAD_DIM)
            o_scr[:, W_QA + h * HEAD_DIM:W_QA + (h + 1) * HEAD_DIM] = _dot(
                p_scr[(H_A + h) * n:(H_A + h + 1) * n, :], vb[:, sl])
        for h in range(H_C):
            c0 = h * (QK_NOPE + V_C)
            r0 = (H_A + H_B + h) * n
            off = W_QA + W_B + h * V_C
            o_scr[:, off:off + V_C] = _dot(p_scr[r0:r0 + n, :], kv[:, c0 + QK_NOPE:c0 + QK_NOPE + V_C])
        y = _dot(o_scr[...].astype(bf16), wout_ref[...])
        o_ref[rows, :] = x_ref[rows, :] + gate_ref[0] * y
        return carry

    lax.fori_loop(0, CTX_BATCHES, one_batch, 0)


def _ctx_attn(layer, sink, qs, caches, wukv, wout, x, gate):
    qa, qb, qc = qs
    ka, va, kb, vb, ckv, kr = caches
    row = lambda b: (b, 0)
    const = lambda b: (0, 0)
    slot = lambda b: (b, layer, 0, 0)
    nrow = CTX_BATCHES * SEQ
    qspec = lambda a: pl.BlockSpec((nrow, a.shape[1]), row)
    cspec = lambda a: pl.BlockSpec((CTX_BATCHES, 1, SEQ, a.shape[3]), slot)
    in_specs = [pl.BlockSpec(memory_space=pltpu.SMEM),
                qspec(qa), cspec(ka), cspec(va), qspec(qb), cspec(kb), cspec(vb), qspec(qc), cspec(ckv), cspec(kr)]
    in_specs += [pl.BlockSpec((None, KV_LORA, H_C * (QK_NOPE + V_C)), lambda b: (layer, 0, 0)),
                 pl.BlockSpec((None, D_MODEL, D_MODEL), lambda b: (layer, 0, 0)),
                 pl.BlockSpec((nrow, D_MODEL), row),
                 _mod_spec(gate, lambda b: 0)]
    return pl.pallas_call(
        _ctx_attn_kernel,
        grid=(BATCH // CTX_BATCHES,),
        in_specs=in_specs,
        out_specs=pl.BlockSpec((nrow, D_MODEL), row),
        out_shape=jax.ShapeDtypeStruct((T_CTX, D_MODEL), f32),
        scratch_shapes=[pltpu.VMEM((SEQ, D_MODEL), f32),
                        pltpu.VMEM(((H_A + H_B + H_C) * SEQ, SEQ), f32),
                        pltpu.VMEM(((H_A + H_B + H_C) * SEQ, SEQ), bf16)],
        compiler_params=_cparams("arbitrary"),
        name="ctx_attn",
    )(sink, qa, ka, va, qb, kb, vb, qc, ckv, kr, wukv, wout, x, gate[0])


def _lat_attn_kernel(sink_ref, qa_ref, qb_ref, qc_ref, ka_ref, va_ref, kb_ref, vb_ref, ckv_ref, kr_ref,
                     cak_ref, cav_ref, cbk_ref, cbv_ref, cckv_ref, ckr_ref, bias_ref,
                     wukv_ref, wout_ref, x_ref, gate_ref, o_ref, o_scr, kv_scr, sa, pa, sb, pb, sc, pc):
    qi = pl.program_id(1)
    nb = DEC_SEQ // BLOCK
    scale = HEAD_DIM ** -0.5

    @pl.when(qi == 0)
    def _():
        kv_scr[0:DEC_SEQ, :] = _dot(ckv_ref[...], wukv_ref[...]).astype(bf16)
        kv_scr[DEC_SEQ:DEC_SEQ + PAST_LEN, :] = _dot(cckv_ref[0, 0].astype(bf16), wukv_ref[...]).astype(bf16)

    def blk(ref, j):
        idx = jnp.clip(qi + j, 0, nb - 1)
        return ref[pl.ds(pl.multiple_of(idx * BLOCK, BLOCK), BLOCK), :]

    ka = jnp.concatenate([blk(ka_ref, -1), blk(ka_ref, 0), blk(ka_ref, 1), cak_ref[0, 0].astype(bf16)], axis=0)
    va = jnp.concatenate([blk(va_ref, -1), blk(va_ref, 0), blk(va_ref, 1), cav_ref[0, 0].astype(bf16)], axis=0)
    nk_a = 3 * BLOCK + PAST_LEN
    r = lax.broadcasted_iota(jnp.int32, (BLOCK, nk_a), 0)
    c = lax.broadcasted_iota(jnp.int32, (BLOCK, nk_a), 1)
    valid = (((c < BLOCK) & (c >= r) & (qi > 0))
             | ((c >= BLOCK) & (c < 2 * BLOCK))
             | ((c >= 2 * BLOCK) & (c < 3 * BLOCK) & (c - 2 * BLOCK <= r) & (qi < nb - 1))
             | (c >= 3 * BLOCK))
    for h in range(H_A):
        g = h // G_A
        q = qa_ref[:, h * HEAD_DIM:(h + 1) * HEAD_DIM]
        s = _dot_nt(q, ka[:, g * HEAD_DIM:(g + 1) * HEAD_DIM]) * scale
        sa[h * BLOCK:(h + 1) * BLOCK, :] = jnp.where(valid, s, NEG)

    cbk = cbk_ref[0, 0].astype(bf16)
    cbv = cbv_ref[0, 0].astype(bf16)
    rows_per_blk = BLOCK // GRID_W
    nloc = NA_ROWS * GRID_W
    vcats = []
    for half in range(rows_per_blk):
        grow = qi * rows_per_blk + half
        start = jnp.clip(grow - NA_ROWS // 2, 0, ROWS - NA_ROWS)
        kloc = kb_ref[pl.ds(pl.multiple_of(start * GRID_W, GRID_W), nloc), :]
        vloc = vb_ref[pl.ds(pl.multiple_of(start * GRID_W, GRID_W), nloc), :]
        vcats.append(jnp.concatenate([vloc, cbv], axis=0))
        qrows = slice(half * GRID_W, (half + 1) * GRID_W)
        dr0 = start - grow + (NA_ROWS - 1)
        for h in range(H_B):
            sl = slice(h * HEAD_DIM, (h + 1) * HEAD_DIM)
            q = qb_ref[qrows, sl]
            bias = jnp.concatenate([bias_ref[h, dr0 + 2 * j] for j in range(NA_ROWS // 2)], axis=1)
            s_loc = _dot_nt(q, kloc[:, sl]) * scale + bias
            s_ctx = _dot_nt(q, cbk[:, sl]) * scale
            r0 = (half * H_B + h) * GRID_W
            sb[r0:r0 + GRID_W, :] = jnp.concatenate([s_loc, s_ctx], axis=1)

    kr = jnp.concatenate([kr_ref[...], ckr_ref[0, 0].astype(bf16)], axis=0)
    scale_c = (QK_NOPE + QK_ROPE) ** -0.5
    for h in range(H_C):
        qn = qc_ref[:, h * QC_PAD:h * QC_PAD + QK_NOPE]
        qr = qc_ref[:, h * QC_PAD + QK_NOPE:h * QC_PAD + QK_NOPE + QK_ROPE]
        c0 = h * (QK_NOPE + V_C)
        sc[h * BLOCK:(h + 1) * BLOCK, :] = (_dot_nt(qn, kv_scr[:, c0:c0 + QK_NOPE]) + _dot_nt(qr, kr)) * scale_c

    for pair in range(H_A // 2):
        h0 = 2 * pair
        _softmax_rows(sa, pa, slice(h0 * BLOCK, (h0 + 2) * BLOCK), ((sink_ref[h0], BLOCK), (sink_ref[h0 + 1], BLOCK)))
    for blk2 in range(rows_per_blk * H_B // 2):
        _softmax_rows(sb, pb, slice(blk2 * 2 * GRID_W, (blk2 + 1) * 2 * GRID_W))
    for h in range(H_C):
        _softmax_rows(sc, pc, slice(h * BLOCK, (h + 1) * BLOCK))

    for h in range(H_A):
        g = h // G_A
        o_scr[:, h * HEAD_DIM:(h + 1) * HEAD_DIM] = _dot(pa[h * BLOCK:(h + 1) * BLOCK, :],
                                                         va[:, g * HEAD_DIM:(g + 1) * HEAD_DIM])
    for half in range(rows_per_blk):
        qrows = slice(half * GRID_W, (half + 1) * GRID_W)
        for h in range(H_B):
            sl = slice(h * HEAD_DIM, (h + 1) * HEAD_DIM)
            r0 = (half * H_B + h) * GRID_W
            o_scr[qrows, W_QA + h * HEAD_DIM:W_QA + (h + 1) * HEAD_DIM] = _dot(pb[r0:r0 + GRID_W, :],
                                                                             vcats[half][:, sl])
    for h in range(H_C):
        c0 = h * (QK_NOPE + V_C)
        off = W_QA + W_B + h * V_C
        o_scr[:, off:off + V_C] = _dot(pc[h * BLOCK:(h + 1) * BLOCK, :], kv_scr[:, c0 + QK_NOPE:c0 + QK_NOPE + V_C])

    y = _dot(o_scr[...].astype(bf16), wout_ref[...])
    o_ref[...] = x_ref[...] + gate_ref[0] * y


def _lat_attn(layer, sink, proj, caches, bias_tab, wukv, wout, x, lat_row0, gate):
    qa, ka, va, qb, kb, vb, qc, ckv, kr = proj
    nb = DEC_SEQ // BLOCK
    qrow = lambda b, q: (b * nb + q, 0)
    xrow = lambda b, q: (lat_row0 // BLOCK + b * nb + q, 0)
    brow = lambda b, q: (b, 0)
    const = lambda b, q: (0, 0)
    cidx = lambda b, q: (b, layer, 0, 0)
    in_specs = [pl.BlockSpec(memory_space=pltpu.SMEM)]
    in_specs += [pl.BlockSpec((BLOCK, a.shape[1]), qrow) for a in (qa, qb, qc)]
    in_specs += [pl.BlockSpec((DEC_SEQ, a.shape[1]), brow) for a in (ka, va, kb, vb, ckv, kr)]
    in_specs += [pl.BlockSpec((1, 1, PAST_LEN, a.shape[3]), cidx) for a in caches]
    in_specs += [pl.BlockSpec((None,) + bias_tab.shape[1:], lambda b, q: (layer, 0, 0, 0, 0)),
                 pl.BlockSpec((None, KV_LORA, H_C * (QK_NOPE + V_C)), lambda b, q: (layer, 0, 0)),
                 pl.BlockSpec((None, D_MODEL, D_MODEL), lambda b, q: (layer, 0, 0)),
                 pl.BlockSpec((BLOCK, D_MODEL), xrow),
                 _mod_spec(gate, lambda b, q: 1 + b)]
    return pl.pallas_call(
        _lat_attn_kernel,
        grid=(DEC_BATCH, nb),
        in_specs=in_specs,
        out_specs=pl.BlockSpec((BLOCK, D_MODEL), qrow),
        out_shape=jax.ShapeDtypeStruct((T_LAT, D_MODEL), f32),
        scratch_shapes=[pltpu.VMEM((BLOCK, D_MODEL), f32),
                        pltpu.VMEM((DEC_SEQ + PAST_LEN, H_C * (QK_NOPE + V_C)), bf16)]
        + [pltpu.VMEM(shape, dt) for shape in ((H_A * BLOCK, 3 * BLOCK + PAST_LEN),
                                               (H_B * BLOCK, NA_ROWS * GRID_W + PAST_LEN),
                                               (H_C * BLOCK, DEC_SEQ + PAST_LEN)) for dt in (f32, bf16)],
        compiler_params=_cparams("arbitrary", "arbitrary"),
        name="lat_attn",
    )(sink, qa, qb, qc, ka, va, kb, vb, ckv, kr, *caches, bias_tab, wukv, wout, x, gate[0])


def _pick_stream(xc_ref, xl_ref, x_scr):
    i = pl.program_id(0)

    @pl.when(i < T_CTX // TM_TOK)
    def _():
        x_scr[...] = xc_ref[...]

    @pl.when(i >= T_CTX // TM_TOK)
    def _():
        x_scr[...] = xl_ref[...]

    return x_scr[...]


def _stream_specs(lat_row0):
    n_ctx = T_CTX // TM_TOK
    return [pl.BlockSpec((TM_TOK, D_MODEL), lambda i: (jnp.minimum(i, n_ctx - 1), 0)),
            pl.BlockSpec((TM_TOK, D_MODEL), lambda i: (lat_row0 // TM_TOK + jnp.maximum(i - n_ctx, 0), 0))]


def _router_kernel(xc_ref, xl_ref, g_ref, sh_ref, sc_ref, wr_ref, br_ref, h_ref, e_ref, gt_ref, x_scr):
    h = _rms(_pick_stream(xc_ref, xl_ref, x_scr), g_ref[...]) * (1.0 + sc_ref[0]) + sh_ref[0]
    _store_row_tiles(h_ref, h)
    h_hi = h.astype(bf16)
    h_lo = (h - h_hi.astype(f32)).astype(bf16)
    w = wr_ref[...]
    w_hi = w.astype(bf16)
    w_lo = (w - w_hi.astype(f32)).astype(bf16)
    logits = _dot(h_hi, w_hi) + _dot(h_hi, w_lo) + _dot(h_lo, w_hi) + br_ref[...]
    lane = lax.broadcasted_iota(jnp.int32, logits.shape, 1).astype(f32)
    l = jnp.where(lane < N_EXPERTS, logits, -jnp.inf)
    tops, idxs = [], []
    for _ in range(TOP_K):
        m = jnp.max(l, axis=-1, keepdims=True)
        idx = jnp.min(jnp.where(l == m, lane, float(LANE)), axis=-1, keepdims=True)
        tops.append(m)
        idxs.append(idx)
        l = jnp.where(lane == idx, -jnp.inf, l)
    ex = [jnp.exp(t - tops[0]) for t in tops]
    den = ex[0] + ex[1] + ex[2] + ex[3]
    e_out = jnp.zeros(logits.shape, f32)
    g_out = jnp.zeros(logits.shape, f32)
    for k in range(TOP_K):
        e_out = jnp.where(lane == k, idxs[k], e_out)
        g_out = jnp.where(lane == k, ex[k] / den, g_out)
    e_ref[...] = e_out.astype(jnp.int32)
    gt_ref[...] = g_out


def _group_of_tile(i):
    per_b = DEC_SEQ // TM_TOK
    n_ctx = T_CTX // TM_TOK
    return jnp.where(i < n_ctx, 0, 1 + (i - n_ctx) // per_b)


def _router(layer, xc, xl, lat_row0, g, shift, scale, wr, br):
    tm = TM_TOK
    row = lambda i: (i, 0)
    const = lambda i: (0, 0)
    return pl.pallas_call(
        _router_kernel,
        grid=(T_ALL // tm,),
        in_specs=_stream_specs(lat_row0) +
                 [pl.BlockSpec((1, D_MODEL), const),
                  _mod_spec(shift, _group_of_tile),
                  _mod_spec(scale, _group_of_tile),
                  pl.BlockSpec((None, D_MODEL, LANE), lambda i: (layer, 0, 0)),
                  pl.BlockSpec((None, 1, LANE), lambda i: (layer, 0, 0))],
        out_specs=[pl.BlockSpec((tm * ROW_TILE, LANE), row), pl.BlockSpec((tm, LANE), row),
                   pl.BlockSpec((tm, LANE), row)],
        out_shape=[jax.ShapeDtypeStruct((T_ALL * ROW_TILE, LANE), f32),
                   jax.ShapeDtypeStruct((T_ALL, LANE), jnp.int32),
                   jax.ShapeDtypeStruct((T_ALL, LANE), f32)],
        scratch_shapes=[pltpu.VMEM((tm, D_MODEL), f32)],
        compiler_params=_cparams("arbitrary"),
        name="router",
    )(xc, xl, g, shift[0], scale[0], wr, br)


def _dispatch_kernel(tok_ref, nu_ref, h_hbm, o_ref, hv, xg, hsem):
    tm = TM_MOE
    i = pl.program_id(0)

    @pl.when(i == 0)
    def _():
        resident = pltpu.make_async_copy(h_hbm, hv, hsem.at[0])
        resident.start()
        resident.wait()

    def one_block(sub, carry):
        blk = i * DISPATCH_BLOCKS + sub
        rows = pl.ds(pl.multiple_of(sub * tm, tm), tm)

        @pl.when(blk < nu_ref[0])
        def _():
            for r in range(tm):
                t = tok_ref[blk * tm + r]
                xg[pl.ds(r, ROW_TILE, stride=tm + 1), :] = hv[pl.ds(pl.multiple_of(t * ROW_TILE, ROW_TILE),
                                                                 ROW_TILE), :]
            o_ref[rows, :] = jnp.concatenate([xg[pl.ds(c * (tm + 1), tm), :] for c in range(ROW_TILE)],
                                             axis=1).astype(bf16)

        @pl.when(blk >= nu_ref[0])
        def _():
            o_ref[rows, :] = jnp.zeros((tm, D_MODEL), bf16)

        return carry

    lax.fori_loop(0, DISPATCH_BLOCKS, one_block, 0)


def _dispatch(row_tok, n_used, h):
    tm = TM_MOE
    return pl.pallas_call(
        _dispatch_kernel,
        grid_spec=pltpu.PrefetchScalarGridSpec(
            num_scalar_prefetch=2,
            grid=(N_MOE_BLOCKS // DISPATCH_BLOCKS,),
            in_specs=[pl.BlockSpec(memory_space=pl.ANY)],
            out_specs=pl.BlockSpec((DISPATCH_BLOCKS * tm, D_MODEL), lambda i, tok, nu: (i, 0)),
            scratch_shapes=[pltpu.VMEM((T_ALL * ROW_TILE, LANE), f32), pltpu.VMEM(((tm + 1) * ROW_TILE, LANE), f32),
                            pltpu.SemaphoreType.DMA((1,))]),
        out_shape=jax.ShapeDtypeStruct((N_MOE_BLOCKS * tm, D_MODEL), bf16),
        compiler_params=_cparams("arbitrary"),
        name="dispatch",
    )(row_tok, n_used, h)


def _moe_kernel(layer, be_ref, nu_ref, nxt_ref, dst_ref, x_ref, wgu_hbm, bgu_ref, wd_hbm, bd_ref, y_hbm,
                y0, y1, wgu_st, wd_st, wgu_bf, wd_bf, wsem, ssem):
    tm = TM_MOE
    i = pl.program_id(0)
    nb = pl.num_programs(0)
    used = i < nu_ref[0]
    yb = (y0, y1)

    def out_tile(row):
        return pl.ds(pl.multiple_of(row * ROW_TILE, ROW_TILE), ROW_TILE)

    def scatter_desc(buf, r, dst_row, s):
        return pltpu.make_async_copy(buf.at[out_tile(r)], y_hbm.at[out_tile(dst_row)], ssem.at[s])

    def scatter_wait(s):
        pltpu.make_async_copy(yb[s], y_hbm.at[pl.ds(0, tm * ROW_TILE)], ssem.at[s]).wait()

    def scatter_start(blk, s, unrolled):
        if unrolled:
            for r in range(tm):
                scatter_desc(yb[s], r, dst_ref[(blk + 1) * tm + r], s).start(priority=r % 2)
        else:
            def body(r, carry):
                scatter_desc(yb[s], r, dst_ref[(blk + 1) * tm + r], s).start()
                return carry
            lax.fori_loop(0, tm, body, 0, unroll=8)

    def weight_copies(e):
        return (pltpu.make_async_copy(wgu_hbm.at[layer, e], wgu_st, wsem.at[0]),
                pltpu.make_async_copy(wd_hbm.at[layer, e], wd_st, wsem.at[1]))

    @pl.when(i == 0)
    def _():
        for s in range(2):
            yb[s][...] = jnp.zeros_like(yb[s])
            dummy = pltpu.make_async_copy(yb[s], y_hbm.at[pl.ds((N_ASSIGN + s * tm) * ROW_TILE, tm * ROW_TILE)],
                                          ssem.at[s])
            dummy.start()
            dummy.wait()
        for cp in weight_copies(be_ref[0]):
            cp.start()

    first = jnp.logical_and(used, jnp.logical_or(i == 0, be_ref[i] != be_ref[jnp.maximum(i - 1, 0)]))

    @pl.when(first)
    def _():
        for cp in weight_copies(0):
            cp.wait()
        wgu_bf[...] = wgu_st[...].astype(bf16)
        wd_bf[...] = wd_st[...].astype(bf16)

        @pl.when(nxt_ref[i] >= 0)
        def _():
            for cp in weight_copies(nxt_ref[i]):
                cp.start()

    def step(par):
        cur, oth = par, 1 - par

        @pl.when(jnp.logical_and(i >= 1, i - 2 < nu_ref[0]))
        def _():
            scatter_wait(cur)

        @pl.when(used)
        def _():
            scatter_start(i - 1, oth, unrolled=True)
            gu = _dot(x_ref[...], wgu_bf[...]) + bgu_ref[0, 0]
            x_glu = jnp.minimum(gu[:, :D_FF], SWIGLU_LIMIT)
            x_lin = jnp.clip(gu[:, D_FF:], -SWIGLU_LIMIT, SWIGLU_LIMIT)
            act = x_glu * jax.nn.sigmoid(SWIGLU_ALPHA * x_glu) * (x_lin + 1.0)
            _store_row_tiles(yb[cur], _dot(act.astype(bf16), wd_bf[...]) + bd_ref[0, 0])

        flush = jnp.logical_and(jnp.logical_not(used), i - 1 < nu_ref[0])

        @pl.when(flush)
        def _():
            scatter_start(i - 1, oth, unrolled=False)

        @pl.when(jnp.logical_and(flush, i == nb - 1))
        def _():
            scatter_wait(oth)

    @pl.when(i % 2 == 0)
    def _():
        step(0)

    @pl.when(i % 2 == 1)
    def _():
        step(1)


def _moe(layer, routing, h, w_gu, b_gu, w_down, b_down):
    tm = TM_MOE
    block_e, n_used, nxt_e, row_tok, row_dst = routing
    xs = _dispatch(row_tok, n_used, h)
    ex4 = lambda i, be, nu, nxt, dst: (layer, be[i], 0, 0)
    return pl.pallas_call(
        functools.partial(_moe_kernel, layer),
        grid_spec=pltpu.PrefetchScalarGridSpec(
            num_scalar_prefetch=4,
            grid=(N_MOE_BLOCKS,),
            in_specs=[pl.BlockSpec((tm, D_MODEL), lambda i, be, nu, nxt, dst: (i, 0)),
                      pl.BlockSpec(memory_space=pl.ANY),
                      pl.BlockSpec((1, 1, 1, 2 * D_FF), ex4),
                      pl.BlockSpec(memory_space=pl.ANY),
                      pl.BlockSpec((1, 1, 1, D_MODEL), ex4)],
            out_specs=pl.BlockSpec(memory_space=pl.ANY),
            scratch_shapes=[pltpu.VMEM((tm * ROW_TILE, LANE), f32), pltpu.VMEM((tm * ROW_TILE, LANE), f32),
                            pltpu.VMEM((D_MODEL, 2 * D_FF), f32), pltpu.VMEM((D_FF, D_MODEL), f32),
                            pltpu.VMEM((D_MODEL, 2 * D_FF), bf16), pltpu.VMEM((D_FF, D_MODEL), bf16),
                            pltpu.SemaphoreType.DMA((2,)), pltpu.SemaphoreType.DMA((2,))]),
        out_shape=jax.ShapeDtypeStruct(((N_ASSIGN + 2 * tm) * ROW_TILE, LANE), f32),
        compiler_params=_cparams("arbitrary"),
        name="moe",
    )(block_e, n_used, nxt_e, row_dst, xs, w_gu, b_gu.reshape(DEPTH, N_EXPERTS, 1, 2 * D_FF),
      w_down, b_down.reshape(DEPTH, N_EXPERTS, 1, D_MODEL))


Y_RING = 3


def _combine_kernel(final, xc_ref, xl_ref, y_hbm, gt_ref, gate_ref, gf_ref, *rest):
    x_scr, ybuf, ysem = rest[-3:]
    i = pl.program_id(0)
    nt = pl.num_programs(0)
    rows = TM_TOK * ROW_TILE

    def fetch(step, k):
        src = y_hbm.at[pl.ds(pl.multiple_of((k * nt + step) * rows, rows), rows)]
        return pltpu.make_async_copy(src, ybuf.at[step % Y_RING, k], ysem.at[step % Y_RING, k])

    @pl.when(i == 0)
    def _():
        for step in range(Y_RING - 1):
            for k in range(TOP_K):
                fetch(step, k).start()

    @pl.when(i + Y_RING - 1 < nt)
    def _():
        for k in range(TOP_K):
            fetch(i + Y_RING - 1, k).start()

    for k in range(TOP_K):
        fetch(i, k).wait()
    slot = i % Y_RING
    gt = gt_ref[...]
    f = gt[:, 0:1] * _load_row_tiles(ybuf.at[slot, 0])
    for k in range(1, TOP_K):
        f = f + gt[:, k:k + 1] * _load_row_tiles(ybuf.at[slot, k])
    out = _pick_stream(xc_ref, xl_ref, x_scr) + gate_ref[0] * f
    if not final:
        rest[0][...] = out
        return
    out = _rms(out, gf_ref[...])
    oc_ref, ol_ref = rest[0], rest[1]
    i = pl.program_id(0)

    @pl.when(i < T_CTX // TM_TOK)
    def _():
        oc_ref[...] = out

    @pl.when(i >= T_CTX // TM_TOK)
    def _():
        ol_ref[...] = out


def _combine(final, xc, xl, lat_row0, y, gates, gate, g_final):
    tm = TM_TOK
    nt = T_ALL // tm
    n_ctx = T_CTX // tm
    row = lambda i: (i, 0)
    const = lambda i: (0, 0)
    if final:
        out_specs = [pl.BlockSpec((tm, D_MODEL), lambda i: (jnp.minimum(i, n_ctx - 1), 0)),
                     pl.BlockSpec((tm, D_MODEL), lambda i: (jnp.maximum(i - n_ctx, 0), 0))]
        out_shape = [jax.ShapeDtypeStruct((T_CTX, D_MODEL), f32), jax.ShapeDtypeStruct((T_LAT, D_MODEL), f32)]
    else:
        out_specs = pl.BlockSpec((tm, D_MODEL), row)
        out_shape = jax.ShapeDtypeStruct((T_ALL, D_MODEL), f32)
    return pl.pallas_call(
        functools.partial(_combine_kernel, final),
        grid=(nt,),
        in_specs=_stream_specs(lat_row0) +
                 [pl.BlockSpec(memory_space=pl.ANY),
                  pl.BlockSpec((tm, LANE), row),
                  _mod_spec(gate, _group_of_tile),
                  pl.BlockSpec((1, D_MODEL), const)],
        out_specs=out_specs,
        out_shape=out_shape,
        scratch_shapes=[pltpu.VMEM((tm, D_MODEL), f32),
                        pltpu.VMEM((Y_RING, TOP_K, tm * ROW_TILE, LANE), f32),
                        pltpu.SemaphoreType.DMA((Y_RING, TOP_K))],
        compiler_params=_cparams("arbitrary"),
        name="combine",
    )(xc, xl, y, gates, gate[0], g_final)


def _rope_head_tables(d):
    nf = d // 4
    half = d // 2
    t = np.arange(DEC_SEQ)
    inv = ROPE_BASE ** (-np.arange(nf, dtype=np.float32) / nf)
    i = np.arange(d)
    pos = np.where(i[None, :] < half, (t // GRID_W)[:, None], (t % GRID_W)[:, None]).astype(np.float32)
    ang = pos * inv[i % nf][None, :].astype(np.float32)
    first = (i % half) < nf
    cos = np.cos(ang)
    sin = np.where(first[None, :], -np.sin(ang), np.sin(ang))
    partner = np.where(first, i + nf, i - nf)
    return cos.astype(np.float32), sin.astype(np.float32), partner


def _rope_tables():
    cos64, sin64, _ = _rope_head_tables(HEAD_DIM)
    cos32, sin32, _ = _rope_head_tables(QK_ROPE)
    cosa = np.tile(cos64, (1, H_A))
    sina = np.tile(sin64, (1, H_A))
    cosq1 = np.concatenate([np.ones((DEC_SEQ, QK_NOPE), np.float32), cos32,
                            np.ones((DEC_SEQ, QC_PAD - QK_NOPE - QK_ROPE), np.float32)], axis=1)
    sinq1 = np.concatenate([np.zeros((DEC_SEQ, QK_NOPE), np.float32), sin32,
                            np.zeros((DEC_SEQ, QC_PAD - QK_NOPE - QK_ROPE), np.float32)], axis=1)
    cosq = np.tile(cosq1, (1, H_C))
    sinq = np.tile(sinq1, (1, H_C))
    return tuple(jnp.asarray(a) for a in (cosa, sina, cosq, sinq, cos32, sin32))


def _pad_cols(w, n):
    return jnp.pad(w, ((0, 0), (0, n - w.shape[1])))


def _layer_weights(w_in, w_uq):
    cuts = np.cumsum((W_QA, W_KA, W_VA, W_B, W_B, W_B, Q_LORA, KV_LORA, QK_ROPE))[:-1]
    qa, ka, va, qb, kb, vb, cq, ckv, kr = jnp.split(w_in, [int(c) for c in cuts], axis=1)
    _, _, p64 = _rope_head_tables(HEAD_DIM)
    _, _, p32 = _rope_head_tables(QK_ROPE)
    pa = np.concatenate([h * HEAD_DIM + p64 for h in range(H_A)])
    base = jnp.concatenate([qa, ka, va, _pad_cols(qb, 384), _pad_cols(kb, 384), _pad_cols(vb, 384), cq, ckv,
                            _pad_cols(kr, 128)], axis=1)
    w_ctx = base.astype(bf16)
    w_lat = jnp.concatenate([base, qa[:, pa], ka[:, pa[:W_KA]], _pad_cols(kr[:, p32], 128)], axis=1).astype(bf16)
    hq = QK_NOPE + QK_ROPE
    heads = [_pad_cols(w_uq[:, h * hq:(h + 1) * hq], QC_PAD) for h in range(H_C)]
    pq = np.concatenate([np.arange(QK_NOPE), QK_NOPE + p32])
    heads_p = [_pad_cols(w_uq[:, h * hq:(h + 1) * hq][:, pq], QC_PAD) for h in range(H_C)]
    wuq = jnp.concatenate(heads, axis=1).astype(bf16)
    wuq2 = jnp.concatenate(heads + heads_p, axis=1).astype(bf16)
    return w_ctx, w_lat, wuq, wuq2


def _bias_table(rpb):
    col = np.arange(GRID_W)
    col_start = np.clip(col - NA_COLS // 2, 0, GRID_W - NA_COLS)
    col_ok = (col[None, :] >= col_start[:, None]) & (col[None, :] < col_start[:, None] + NA_COLS)
    dc = np.clip(col[None, :] - col[:, None] + (NA_COLS - 1), 0, 2 * NA_COLS - 2)
    onehot = (dc[None] == np.arange(2 * NA_COLS - 1)[:, None, None]).astype(np.float32)
    expanded = jnp.einsum('hrd,dqk->hrqk', rpb.astype(f32), jnp.asarray(onehot), precision=lax.Precision.HIGHEST)
    blocks = jnp.where(col_ok[None, None], expanded, NEG)
    return jnp.concatenate([blocks[:, :-1], blocks[:, 1:]], axis=-1)


def _routing(top_e):
    tm = TM_MOE
    key_bits = 16
    pad_mark = (1 << key_bits) - 1
    flat_e = top_e.T.reshape(N_ASSIGN)
    experts = jnp.arange(N_EXPERTS, dtype=jnp.int32)
    counts = jnp.sum((flat_e[:, None] == experts[None, :]).astype(jnp.int32), axis=0)
    nblk = (counts + tm - 1) // tm
    blk_end = jnp.cumsum(nblk)
    pad_end = jnp.cumsum(nblk * tm - counts)
    slots = jnp.arange(N_MOE_BLOCKS * tm - N_ASSIGN, dtype=jnp.int32)
    pad_e = jnp.sum((pad_end[None, :] <= slots[:, None]).astype(jnp.int32), axis=1)
    keys = jnp.concatenate([(flat_e << key_bits) + jnp.arange(N_ASSIGN, dtype=jnp.int32),
                            (pad_e << key_bits) + pad_mark])
    asg = (jnp.sort(keys, stable=False) & pad_mark).reshape(N_MOE_BLOCKS, tm)
    valid = asg != pad_mark
    blocks = jnp.arange(N_MOE_BLOCKS, dtype=jnp.int32)
    r = jnp.arange(tm, dtype=jnp.int32)[None, :]
    tok = jnp.where(valid, asg % T_ALL, 0)
    row_dst = jnp.where(valid, asg, N_ASSIGN + (blocks[:, None] % 2) * tm + r)
    row_dst = jnp.concatenate([N_ASSIGN + tm + r, row_dst], axis=0).reshape(-1)
    block_e = jnp.minimum(jnp.sum((blk_end[None, :] <= blocks[:, None]).astype(jnp.int32), axis=1), N_EXPERTS - 1)
    n_used = blk_end[-1].astype(jnp.int32).reshape(1)
    has = jnp.where(counts > 0, experts, N_EXPERTS)
    later = experts[None, :] > experts[:, None]
    nxt = jnp.min(jnp.where(later, has[None, :], N_EXPERTS), axis=1)
    nxt = jnp.where(nxt >= N_EXPERTS, -1, nxt)
    sel = (block_e[:, None] == experts[None, :]).astype(jnp.int32)
    nxt_e = jnp.sum(sel * nxt[None, :], axis=1)
    i32 = lambda a: a.astype(jnp.int32)
    return i32(block_e), n_used, i32(nxt_e), i32(tok).reshape(-1), i32(row_dst)


def kernel(x_prompt, x_sample, cache_a_k, cache_a_v, cache_b_k, cache_b_v, cache_c_kv, cache_c_kr, c, c_ctx, w_ada, b_ada, g_attn, g_ffn, w_in, sink_a, rpb_b, g_cq, g_ckv, w_uq, w_ukv, w_out, w_router, b_router, w_gu, b_gu, w_down, b_down, g_final):
    xc, xl, lat_row0 = x_prompt.reshape(T_CTX, D_MODEL), x_sample.reshape(T_LAT, D_MODEL), 0
    cvec = jnp.concatenate([c_ctx[None, :], c, jnp.zeros((8 - N_GROUPS, D_MODEL), f32)], axis=0)
    mods = _ada(cvec, w_ada, b_ada)[:, :N_GROUPS].reshape(DEPTH, N_GROUPS, 6, 1, D_MODEL)
    tabs = _rope_tables()
    caches = (cache_a_k.reshape(DEC_BATCH, DEPTH, PAST_LEN, W_KA), cache_a_v.reshape(DEC_BATCH, DEPTH, PAST_LEN, W_VA),
              cache_b_k.reshape(DEC_BATCH, DEPTH, PAST_LEN, W_B), cache_b_v.reshape(DEC_BATCH, DEPTH, PAST_LEN, W_B),
              cache_c_kv, cache_c_kr)
    new = None
    w_ctx, w_lat, wuq, wuq2 = jax.vmap(_layer_weights)(w_in, w_uq)
    wukv = w_ukv.astype(bf16)
    wout = w_out.astype(bf16)
    bias_tab = jax.vmap(_bias_table)(rpb_b)
    wr = jnp.pad(w_router, ((0, 0), (0, 0), (0, LANE - N_EXPERTS)))
    br = jnp.pad(b_router, ((0, 0), (0, LANE - N_EXPERTS)))[:, None, :]
    for layer in range(DEPTH):
        m = [(mods, layer, j) for j in range(6)]
        g1 = g_attn[layer][None, :]
        gcq = g_cq[layer][None, :]
        gckv = g_ckv[layer][None, :]
        sink = sink_a[layer]

        qs, new = _inproj_ctx(layer, new, xc, g1, m[0], m[1], w_ctx, gcq, gckv, wuq)
        x_ctx = _ctx_attn(layer, sink, qs, new, wukv, wout, xc, m[2])

        plat = _inproj_lat(layer, xl, lat_row0, g1, m[0], m[1], w_lat, gcq, gckv, wuq2, tabs)
        x_lat = _lat_attn(layer, sink, plat, caches, bias_tab, wukv, wout, xl, lat_row0, m[2])

        h2, top_e, gates = _router(layer, x_ctx, x_lat, 0, g_ffn[layer][None, :], m[3], m[4], wr, br)
        y = _moe(layer, _routing(top_e[:, :TOP_K]), h2, w_gu, b_gu, w_down, b_down)
        x = _combine(layer == DEPTH - 1, x_ctx, x_lat, 0, y, gates, m[5], g_final[None, :])
        xc, xl, lat_row0 = x, x, T_CTX

    y_prompt = x[0].reshape(BATCH, SEQ, D_MODEL)
    y_sample = x[1].reshape(DEC_BATCH, DEC_SEQ, D_MODEL)
    shapes = ((KV_A, HEAD_DIM), (KV_A, HEAD_DIM), (H_B, HEAD_DIM), (H_B, HEAD_DIM), (KV_LORA,), (QK_ROPE,))
    outs = [a.reshape((BATCH, DEPTH, SEQ) + s) for a, s in zip(new, shapes)]
    return (y_prompt, y_sample, *outs)
```
